```python
import jax, jax.numpy as jnp
from jax import lax
import numpy as np

D_MODEL = 1024
BATCH = 4
SEQ = 4096
DEPTH = 4

GRID_W = 64
CTX_LEN = 256
N_MIXERS = 3
N_MOD = 6
NORM_EPS = 1e-6

D_RNN = D_MODEL
RG_BLOCKS = 8
RG_BW = D_RNN // RG_BLOCKS
CONV_W = 4
RG_C = 8.0

HEAD_DIM = 128
N_Q_HEADS = D_MODEL // HEAD_DIM
N_KV_HEADS = 2
GQA_GROUP = N_Q_HEADS // N_KV_HEADS
Q_BLOCK = 128
ROPE_THETA = 10000.0
AXIS_DIM = HEAD_DIM // 2
N_FREQ = AXIS_DIM // 2

CHUNK = 128
D_CM = 2 * D_MODEL
CM_GROUPS = 8
CM_GW = D_CM // CM_GROUPS

N_GROUPS = 4
EXPERTS_PER_GROUP = 8
N_EXPERTS = N_GROUPS * EXPERTS_PER_GROUP
TOP_K = 2
D_EXPERT = 512
MOE_BLOCK = 128

kernel_name = "hybrid_rglru_gqa_gmlp_hmoe_prefix_dit"


def rms_norm(x, g):
    xf = x.astype(jnp.float32)
    y = xf * lax.rsqrt(jnp.mean(xf * xf, axis=-1, keepdims=True) + NORM_EPS)
    return (y * g.astype(jnp.float32)).astype(x.dtype)


def layer_norm(x, g, b):
    xf = x.astype(jnp.float32)
    mu = jnp.mean(xf, axis=-1, keepdims=True)
    var = jnp.mean(jnp.square(xf - mu), axis=-1, keepdims=True)
    y = (xf - mu) * lax.rsqrt(var + NORM_EPS) * g.astype(jnp.float32) + b.astype(jnp.float32)
    return y.astype(x.dtype)


def modulate(h, shift, scale):
    return h * (1 + scale) + shift


def centred_dwconv(x, w, b):
    left = CONV_W // 2
    y = lax.conv_general_dilated(x, w[:, None, :].astype(x.dtype), (1,), [(left, CONV_W - 1 - left)],
                                 dimension_numbers=('NWC', 'WIO', 'NWC'),
                                 feature_group_count=x.shape[-1])
    return y + b


def _linear_combine(left, right):
    a_l, b_l = left
    a_r, b_r = right
    return a_l * a_r, a_r * b_l + b_r


def rglru_scan(xs, wa, ba, wi, bi, lam, h0):
    B, L, _ = xs.shape
    xb = xs.reshape(B, L, RG_BLOCKS, RG_BW)
    r = jax.nn.sigmoid((jnp.einsum('blnc,ncd->blnd', xb, wa).reshape(B, L, D_RNN) + ba).astype(jnp.float32))
    i = jax.nn.sigmoid((jnp.einsum('blnc,ncd->blnd', xb, wi).reshape(B, L, D_RNN) + bi).astype(jnp.float32))
    log_a = -RG_C * r * jax.nn.softplus(-lam.astype(jnp.float32))
    a = jnp.exp(log_a)
    b = jnp.sqrt(-jnp.expm1(2.0 * log_a)) * i * xs.astype(jnp.float32)
    a_cum, h = lax.associative_scan(_linear_combine, (a, b), axis=1)
    if h0 is not None:
        h = h + a_cum * h0[:, None, :]
    return h, h[:, -1]


def rglru_mixer(h_ctx, h_lat, w_in, conv_w, conv_b, wa, ba, wi, bi, lam, w_out, with_ctx):
    z_lat = h_lat @ w_in
    gate_lat, x_lat = z_lat[..., :D_RNN], z_lat[..., D_RNN:]
    if with_ctx:
        z_ctx = h_ctx @ w_in
        gate_ctx, x_ctx = z_ctx[..., :D_RNN], z_ctx[..., D_RNN:]
    else:
        x_ctx = h_ctx @ w_in[:, D_RNN:]
    x_lat = centred_dwconv(x_lat, conv_w, conv_b)
    x_ctx = centred_dwconv(x_ctx, conv_w, conv_b)
    hl_dirs, hc_dirs = [], []
    for d in range(2):
        flip = (lambda t: t[:, ::-1]) if d == 1 else (lambda t: t)
        hc, hc_last = rglru_scan(flip(x_ctx), wa[d], ba[d], wi[d], bi[d], lam[d], None)
        hl, _ = rglru_scan(flip(x_lat), wa[d], ba[d], wi[d], bi[d], lam[d], hc_last)
        hl_dirs.append(flip(hl))
        if with_ctx:
            hc_dirs.append(flip(hc))
    y_lat = (jax.nn.gelu(gate_lat) * (hl_dirs[0] + hl_dirs[1]).astype(h_lat.dtype)) @ w_out
    y_ctx = None
    if with_ctx:
        y_ctx = (jax.nn.gelu(gate_ctx) * (hc_dirs[0] + hc_dirs[1]).astype(h_ctx.dtype)) @ w_out
    return y_ctx, y_lat


def axial_rope_tables(rows):
    row = jnp.broadcast_to(jnp.arange(rows, dtype=jnp.float32)[:, None], (rows, GRID_W)).reshape(-1)
    col = jnp.broadcast_to(jnp.arange(GRID_W, dtype=jnp.float32)[None, :], (rows, GRID_W)).reshape(-1)
    inv = ROPE_THETA ** (-jnp.arange(N_FREQ, dtype=jnp.float32) * 2.0 / AXIS_DIM)
    ang = jnp.stack([row[:, None] * inv, col[:, None] * inv], axis=1)
    return jnp.cos(ang), jnp.sin(ang)


def apply_axial_rope(x, cos, sin):
    xf = x.astype(jnp.float32).reshape(*x.shape[:-1], 2, 2, N_FREQ)
    x1, x2 = xf[..., 0, :], xf[..., 1, :]
    out = jnp.stack([x1 * cos - x2 * sin, x2 * cos + x1 * sin], axis=-2)
    return out.reshape(x.shape).astype(x.dtype)


def _project_heads(h, w_qkv, q_g, k_g, need_q):
    B, L, _ = h.shape
    nq = N_Q_HEADS * HEAD_DIM
    nkv = N_KV_HEADS * HEAD_DIM
    kv = h @ w_qkv[:, nq:]
    k = kv[..., :nkv].reshape(B, L, N_KV_HEADS, HEAD_DIM).transpose(0, 2, 1, 3)
    v = kv[..., nkv:].reshape(B, L, N_KV_HEADS, HEAD_DIM).transpose(0, 2, 1, 3)
    q = None
    if need_q:
        q = (h @ w_qkv[:, :nq]).reshape(B, L, N_KV_HEADS, GQA_GROUP, HEAD_DIM).transpose(0, 2, 3, 1, 4)
        q = rms_norm(q, q_g)
    return q, rms_norm(k, k_g), v


def _attend(q, k, v):
    s = jnp.einsum('bkgqd,bknd->bkgqn', q, k, preferred_element_type=jnp.float32) * (HEAD_DIM ** -0.5)
    p = jax.nn.softmax(s, axis=-1)
    return jnp.einsum('bkgqn,bknd->bkgqd', p.astype(v.dtype), v)


def attention_mixer(h_ctx, h_lat, w_qkv, q_g, k_g, w_o, cos, sin, with_ctx):
    B, L, _ = h_lat.shape
    q_c, k_c, v_c = _project_heads(h_ctx, w_qkv, q_g, k_g, with_ctx)
    q_l, k_l, v_l = _project_heads(h_lat, w_qkv, q_g, k_g, True)
    q_l = apply_axial_rope(q_l, cos, sin)
    k_l = apply_axial_rope(k_l, cos, sin)
    k_all = jnp.concatenate([k_c, k_l], axis=2)
    v_all = jnp.concatenate([v_c, v_l], axis=2)
    nblk = L // Q_BLOCK
    q_blocks = q_l.reshape(B, N_KV_HEADS, GQA_GROUP, nblk, Q_BLOCK, HEAD_DIM).transpose(3, 0, 1, 2, 4, 5)
    o = lax.map(lambda qb: _attend(qb, k_all, v_all), q_blocks)
    o = o.transpose(1, 0, 4, 2, 3, 5).reshape(B, L, N_Q_HEADS * HEAD_DIM)
    y_lat = o @ w_o
    y_ctx = None
    if with_ctx:
        oc = _attend(q_c, k_c, v_c)
        oc = oc.transpose(0, 3, 1, 2, 4).reshape(B, h_ctx.shape[1], N_Q_HEADS * HEAD_DIM)
        y_ctx = oc @ w_o
    return y_ctx, y_lat


def chunk_mlp(h, w_in, ln_g, ln_b, w_s, b_s, w_out):
    B, L, _ = h.shape
    z = jax.nn.gelu(h @ w_in)
    u, v = z[..., :D_CM], z[..., D_CM:]
    v = layer_norm(v, ln_g, ln_b).reshape(B, L // CHUNK, CHUNK, CM_GROUPS, CM_GW)
    v = jnp.einsum('gpq,bnqgc->bnpgc', w_s, v) + b_s.T[:, :, None]
    return (u * v.reshape(B, L, D_CM)) @ w_out


def hier_moe(xt, w_group, b_group, w_router, b_router, w_gate, w_up, w_down):
    T, D = xt.shape
    gl = (xt @ w_group + b_group).astype(jnp.float32)
    gp = jax.nn.softmax(gl, axis=-1)
    _, g_sel = lax.top_k(gl, 1)
    gate_g = jnp.take_along_axis(gp, g_sel, axis=1)
    el = (jnp.einsum('td,dge->tge', xt, w_router) + b_router).astype(jnp.float32)
    el = jnp.take_along_axis(el, g_sel[:, :, None], axis=1)[:, 0]
    top_v, top_i = lax.top_k(el, TOP_K)
    weights = jax.nn.softmax(top_v, axis=-1) * gate_g
    expert = g_sel * EXPERTS_PER_GROUP + top_i
    A = T * TOP_K
    flat_e = expert.reshape(-1)
    flat_w = weights.reshape(-1)
    flat_tok = jnp.repeat(jnp.arange(T, dtype=jnp.int32), TOP_K)
    order = jnp.argsort(flat_e)
    se = flat_e[order]
    counts = jnp.bincount(flat_e, length=N_EXPERTS)
    padded = (counts + MOE_BLOCK - 1) // MOE_BLOCK * MOE_BLOCK
    pad_end = jnp.cumsum(padded)
    pad_start = pad_end - padded
    start = jnp.cumsum(counts) - counts
    dest = pad_start[se] + jnp.arange(A, dtype=jnp.int32) - start[se]
    R = -(-A // MOE_BLOCK) * MOE_BLOCK + N_EXPERTS * MOE_BLOCK
    n_blocks = R // MOE_BLOCK
    row_tok = jnp.full((R,), T, jnp.int32).at[dest].set(flat_tok[order])
    row_w = jnp.zeros((R,), jnp.float32).at[dest].set(flat_w[order])
    blk_e = jnp.minimum(jnp.searchsorted(pad_end, jnp.arange(n_blocks, dtype=jnp.int32) * MOE_BLOCK,
                                         side='right'), N_EXPERTS - 1)
    x_rows = jnp.concatenate([xt, jnp.zeros((1, D), xt.dtype)], axis=0)[row_tok]
    x_rows = x_rows.reshape(n_blocks, MOE_BLOCK, D)

    def expert_block(args):
        xb, e = args
        return (jax.nn.silu(xb @ w_gate[e]) * (xb @ w_up[e])) @ w_down[e]

    y_rows = lax.map(expert_block, (x_rows, blk_e)).reshape(R, D)
    y = jax.ops.segment_sum(y_rows * row_w[:, None].astype(y_rows.dtype), row_tok, num_segments=T + 1)
    return y[:T]


def setup_inputs(seed: int = 0) -> dict:
    key = jax.random.key(seed)
    keys = iter(jax.random.split(key, 48))
    f32 = jnp.float32

    def normal(shape, scale):
        return jax.random.normal(next(keys), shape, f32) * scale

    n_a = len(range(0, DEPTH, N_MIXERS))
    n_b = len(range(1, DEPTH, N_MIXERS))
    n_c = len(range(2, DEPTH, N_MIXERS))
    a0 = jax.random.uniform(next(keys), (n_a, 2, D_RNN), f32, 0.9, 0.999) ** (1.0 / RG_C)
    hd_all = N_Q_HEADS * HEAD_DIM
    return {
        "x": normal((BATCH, SEQ, D_MODEL), 1.0),
        "c": normal((BATCH, D_MODEL), 1.0),
        "ctx": normal((BATCH, CTX_LEN, D_MODEL), 1.0),
        "c_ctx": normal((D_MODEL,), 1.0),
        "ada_w": normal((DEPTH, D_MODEL, N_MOD * D_MODEL), 0.5 * D_MODEL ** -0.5),
        "ada_b": normal((DEPTH, N_MOD * D_MODEL), 0.02),
        "norm_mix_g": 1.0 + normal((DEPTH, D_MODEL), 0.02),
        "norm_ffn_g": 1.0 + normal((DEPTH, D_MODEL), 0.02),
        "rg_w_in": normal((n_a, D_MODEL, 2 * D_RNN), D_MODEL ** -0.5),
        "rg_conv_w": normal((n_a, CONV_W, D_RNN), CONV_W ** -0.5),
        "rg_conv_b": normal((n_a, D_RNN), 0.02),
        "rg_wa": normal((n_a, 2, RG_BLOCKS, RG_BW, RG_BW), RG_BW ** -0.5),
        "rg_ba": normal((n_a, 2, D_RNN), 0.02),
        "rg_wi": normal((n_a, 2, RG_BLOCKS, RG_BW, RG_BW), RG_BW ** -0.5),
        "rg_bi": normal((n_a, 2, D_RNN), 0.02),
        "rg_lambda": jnp.log(a0) - jnp.log1p(-a0),
        "rg_w_out": normal((n_a, D_RNN, D_MODEL), D_RNN ** -0.5),
        "at_w_qkv": normal((n_b, D_MODEL, (N_Q_HEADS + 2 * N_KV_HEADS) * HEAD_DIM), D_MODEL ** -0.5),
        "at_q_g": 1.0 + normal((n_b, HEAD_DIM), 0.02),
        "at_k_g": 1.0 + normal((n_b, HEAD_DIM), 0.02),
        "at_w_o": normal((n_b, hd_all, D_MODEL), hd_all ** -0.5),
        "cm_w_in": normal((n_c, D_MODEL, 2 * D_CM), D_MODEL ** -0.5),
        "cm_ln_g": 1.0 + normal((n_c, D_CM), 0.02),
        "cm_ln_b": normal((n_c, D_CM), 0.02),
        "cm_w_s": normal((n_c, CM_GROUPS, CHUNK, CHUNK), CHUNK ** -0.5),
        "cm_b_s": 1.0 + normal((n_c, CM_GROUPS, CHUNK), 0.02),
        "cm_w_out": normal((n_c, D_CM, D_MODEL), D_CM ** -0.5),
        "moe_w_group": normal((DEPTH, D_MODEL, N_GROUPS), D_MODEL ** -0.5),
        "moe_b_group": normal((DEPTH, N_GROUPS), 0.01),
        "moe_w_router": normal((DEPTH, D_MODEL, N_GROUPS, EXPERTS_PER_GROUP), D_MODEL ** -0.5),
        "moe_b_router": normal((DEPTH, N_GROUPS, EXPERTS_PER_GROUP), 0.01),
        "moe_w_gate": normal((DEPTH, N_EXPERTS, D_MODEL, D_EXPERT), D_MODEL ** -0.5),
        "moe_w_up": normal((DEPTH, N_EXPERTS, D_MODEL, D_EXPERT), D_MODEL ** -0.5),
        "moe_w_down": normal((DEPTH, N_EXPERTS, D_EXPERT, D_MODEL), D_EXPERT ** -0.5),
    }


def reference(x, c, ctx, c_ctx, ada_w, ada_b, norm_mix_g, norm_ffn_g,
              rg_w_in, rg_conv_w, rg_conv_b, rg_wa, rg_ba, rg_wi, rg_bi, rg_lambda, rg_w_out,
              at_w_qkv, at_q_g, at_k_g, at_w_o,
              cm_w_in, cm_ln_g, cm_ln_b, cm_w_s, cm_b_s, cm_w_out,
              moe_w_group, moe_b_group, moe_w_router, moe_b_router, moe_w_gate, moe_w_up, moe_w_down):
    B, S, D = x.shape
    rows = S // GRID_W
    cos, sin = axial_rope_tables(rows)
    c_act = jax.nn.silu(c)
    cc_act = jax.nn.silu(c_ctx)
    for l in range(DEPTH):
        kind = l % N_MIXERS
        j = l // N_MIXERS
        last = l == DEPTH - 1
        with_ctx = not last
        mod_l = (c_act @ ada_w[l] + ada_b[l]).reshape(B, N_MOD, 1, D)
        mod_c = (cc_act @ ada_w[l] + ada_b[l]).reshape(N_MOD, D)
        h_lat = modulate(rms_norm(x, norm_mix_g[l]), mod_l[:, 0], mod_l[:, 1])
        need_ctx_in = not (last and kind == 2)
        h_ctx = modulate(rms_norm(ctx, norm_mix_g[l]), mod_c[0], mod_c[1]) if need_ctx_in else None
        if kind == 0:
            y_ctx, y_lat = rglru_mixer(h_ctx, h_lat, rg_w_in[j], rg_conv_w[j], rg_conv_b[j], rg_wa[j], rg_ba[j],
                                       rg_wi[j], rg_bi[j], rg_lambda[j], rg_w_out[j], with_ctx)
        elif kind == 1:
            y_ctx, y_lat = attention_mixer(h_ctx, h_lat, at_w_qkv[j], at_q_g[j], at_k_g[j], at_w_o[j],
                                           cos, sin, with_ctx)
        else:
            y_lat = chunk_mlp(h_lat, cm_w_in[j], cm_ln_g[j], cm_ln_b[j], cm_w_s[j], cm_b_s[j], cm_w_out[j])
            y_ctx = None
            if with_ctx:
                y_ctx = chunk_mlp(h_ctx, cm_w_in[j], cm_ln_g[j], cm_ln_b[j], cm_w_s[j], cm_b_s[j], cm_w_out[j])
        x = x + mod_l[:, 2] * y_lat
        hf_l = modulate(rms_norm(x, norm_ffn_g[l]), mod_l[:, 3], mod_l[:, 4])
        if last:
            y = hier_moe(hf_l.reshape(-1, D), moe_w_group[l], moe_b_group[l], moe_w_router[l],
                         moe_b_router[l], moe_w_gate[l], moe_w_up[l], moe_w_down[l])
            x = x + mod_l[:, 5] * y.reshape(B, S, D)
        else:
            ctx = ctx + mod_c[2] * y_ctx
            hf_c = modulate(rms_norm(ctx, norm_ffn_g[l]), mod_c[3], mod_c[4])
            n_ctx_tok = hf_c.shape[0] * hf_c.shape[1]
            tokens = jnp.concatenate([hf_c.reshape(-1, D), hf_l.reshape(-1, D)], axis=0)
            y = hier_moe(tokens, moe_w_group[l], moe_b_group[l], moe_w_router[l], moe_b_router[l],
                         moe_w_gate[l], moe_w_up[l], moe_w_down[l])
            ctx = ctx + mod_c[5] * y[:n_ctx_tok].reshape(ctx.shape)
            x = x + mod_l[:, 5] * y[n_ctx_tok:].reshape(B, S, D)
    return x
```

```python
import functools

import jax
import jax.numpy as jnp
from jax import lax
from jax.experimental import pallas as pl
from jax.experimental.pallas import tpu as pltpu

F32 = jnp.float32
BF16 = jnp.bfloat16
I32 = jnp.int32
HIGHEST = lax.Precision.HIGHEST

NORM_EPS = 1e-6
N_MOD = 6
GRID_W = 64
RG_BLOCKS = 8
CONV_W = 4
RG_C = 8.0
HEAD_DIM = 128
N_KV_HEADS = 2
GQA_GROUP = 4
ROPE_THETA = 10000.0
CHUNK = 128
CM_GROUPS = 8
N_GROUPS = 4
EXPERTS_PER_GROUP = 8
N_EXPERTS = N_GROUPS * EXPERTS_PER_GROUP

LANES = 128
SUBLANES = 8
TM = 256
TL = 256
HALO = 8
MOE_BM = 256
MOE_TC = 128
VMEM_LIMIT = 52 * 2**20


def _cp(*sem):
    return pltpu.CompilerParams(dimension_semantics=sem, vmem_limit_bytes=VMEM_LIMIT)


def _norm_mod(x, g, shift, scale):
    ms = jnp.mean(x * x, axis=-1, keepdims=True)
    y = x * lax.rsqrt(ms + NORM_EPS) * g
    return y * (1.0 + scale) + shift


def _mod_spec(d, rows_per_sample, n_samples):
    return pl.BlockSpec((1, 1, N_MOD * d),
                        lambda i, *_: (jnp.minimum(i // rows_per_sample, n_samples), 0, 0))


def _ada_kernel(c_ref, w_ref, b_ref, o_ref):
    cin = c_ref[...]
    act = cin * jax.nn.sigmoid(cin)
    o_ref[0] = jnp.dot(act, w_ref[0], preferred_element_type=F32, precision=HIGHEST) + b_ref[0]


def _ada_table(cin, ada_w, ada_b):
    depth, d, n = ada_w.shape
    tn = d
    return pl.pallas_call(
        _ada_kernel,
        grid=(depth, n // tn),
        in_specs=[pl.BlockSpec((SUBLANES, d), lambda l, j: (0, 0)),
                  pl.BlockSpec((1, d, tn), lambda l, j: (l, 0, j)),
                  pl.BlockSpec((1, 1, tn), lambda l, j: (l, 0, j))],
        out_specs=pl.BlockSpec((1, SUBLANES, tn), lambda l, j: (l, 0, j)),
        out_shape=jax.ShapeDtypeStruct((depth, SUBLANES, n), F32),
        compiler_params=_cp("parallel", "parallel"),
        name="ada_table",
    )(cin, ada_w, ada_b.reshape(depth, 1, n))


def _out_kernel(y_ref, x_ref, mod_ref, w_ref, o_ref, *, gate_idx):
    d = x_ref.shape[-1]
    gate = mod_ref[0][:, gate_idx * d:(gate_idx + 1) * d]
    y = jnp.dot(y_ref[...].astype(BF16), w_ref[...], preferred_element_type=F32)
    o_ref[...] = x_ref[...] + gate * y


def _out_proj(y, x, mod, w, nrows, s, b, gate_idx):
    d = x.shape[-1]
    k = y.shape[-1]
    return pl.pallas_call(
        functools.partial(_out_kernel, gate_idx=gate_idx),
        grid=(nrows // TM,),
        in_specs=[pl.BlockSpec((TM, k), lambda i: (i, 0)),
                  pl.BlockSpec((TM, d), lambda i: (i, 0)),
                  _mod_spec(d, s // TM, b),
                  pl.BlockSpec((k, d), lambda i: (0, 0))],
        out_specs=pl.BlockSpec((TM, d), lambda i: (i, 0)),
        out_shape=jax.ShapeDtypeStruct((nrows, d), F32),
        compiler_params=_cp("parallel"),
        name="out_proj",
    )(y, x, mod, w)


def _rg_in_kernel(x_ref, g_ref, mod_ref, w_ref, gg_ref, xin_ref):
    d = x_ref.shape[-1]
    m = mod_ref[0]
    h = _norm_mod(x_ref[...], g_ref[...], m[:, 0:d], m[:, d:2 * d])
    z = jnp.dot(h.astype(BF16), w_ref[...], preferred_element_type=F32)
    gg_ref[...] = jax.nn.gelu(z[:, :d])
    xin_ref[...] = z[:, d:]


def _rg_in(x, g, mod, w, s, b):
    t, d = x.shape
    return pl.pallas_call(
        _rg_in_kernel,
        grid=(t // TM,),
        in_specs=[pl.BlockSpec((TM, d), lambda i: (i, 0)),
                  pl.BlockSpec((1, d), lambda i: (0, 0)),
                  _mod_spec(d, s // TM, b),
                  pl.BlockSpec((d, 2 * d), lambda i: (0, 0))],
        out_specs=[pl.BlockSpec((TM, d), lambda i: (i, 0)),
                   pl.BlockSpec((TM, d), lambda i: (i, 0))],
        out_shape=[jax.ShapeDtypeStruct((t, d), F32), jax.ShapeDtypeStruct((t, d), F32)],
        compiler_params=_cp("parallel"),
        name="rg_in",
    )(x, g, mod, w)


def _rg_scan_kernel(xm_ref, xprev_ref, xnext_ref, cw_ref, cb_ref, wa_ref, wi_ref,
                    ba_ref, bi_ref, lam_ref, *rest, reverse, nlat):
    if reverse:
        hf_ref, gg_ref, out_ref, xpad, xc, a_s, b_s, h_s, hcar = rest
    else:
        out_ref, xpad, xc, a_s, b_s, hcar = rest
    rows = TL * SUBLANES
    hrows = HALO * SUBLANES
    j = pl.program_id(1)
    m = (nlat - j) if reverse else (j - 1)
    has_prev = jnp.logical_and(j >= 1, m > 0)
    has_next = jnp.logical_and(j >= 1, m < nlat - 1)

    @pl.when(j == 0)
    def _():
        hcar[...] = jnp.zeros_like(hcar)

    xpad[0:hrows, :] = jnp.where(has_prev, xprev_ref[...], 0.0)
    xpad[hrows:hrows + rows, :] = xm_ref[...]
    xpad[hrows + rows:2 * hrows + rows, :] = jnp.where(has_next, xnext_ref[...], 0.0)
    acc = jnp.broadcast_to(cb_ref[...][None], (TL, SUBLANES, LANES))
    for k in range(CONV_W):
        off = (HALO + k - CONV_W // 2) * SUBLANES
        tap = xpad[off:off + rows, :].reshape(TL, SUBLANES, LANES)
        acc = acc + tap * cw_ref[k][None]
    xc[...] = acc.reshape(rows, LANES)

    for n in range(RG_BLOCKS):
        cols = slice(n * LANES, (n + 1) * LANES)
        xn = xc[pl.ds(n, TL, stride=SUBLANES), :]
        xb = xn.astype(BF16)
        r = jax.nn.sigmoid(jnp.dot(xb, wa_ref[n], preferred_element_type=F32) + ba_ref[:, cols])
        gi = jax.nn.sigmoid(jnp.dot(xb, wi_ref[n], preferred_element_type=F32) + bi_ref[:, cols])
        a = jnp.exp(-RG_C * r * jax.nn.softplus(-lam_ref[:, cols]))
        a_s[pl.ds(n, TL, stride=SUBLANES), :] = a
        b_s[pl.ds(n, TL, stride=SUBLANES), :] = jnp.sqrt(1.0 - a * a) * gi * xn

    h_dst = h_s if reverse else out_ref

    def step(s, h):
        t = (TL - 1 - s) if reverse else s
        r0 = pl.multiple_of(t * SUBLANES, SUBLANES)
        h = a_s[pl.ds(r0, SUBLANES), :] * h + b_s[pl.ds(r0, SUBLANES), :]
        h_dst[pl.ds(r0, SUBLANES), :] = h
        return h

    hcar[...] = lax.fori_loop(0, TL, step, hcar[...], unroll=8)
    if reverse:
        out_ref[...] = gg_ref[...] * (hf_ref[...] + h_s[...])


def _rg_scan(xin8, conv_w, conv_b, wa, wi, ba, bi, lam, s, c, b, reverse, hf8=None, gg8=None):
    assert c == TL and s % TL == 0
    rows = TL * SUBLANES
    hrows = HALO * SUBLANES
    nlat = s // TL
    t = xin8.shape[0] // SUBLANES
    n_halo = t // HALO

    def chunk(bi_, j):
        lat = bi_ * nlat + ((nlat - j) if reverse else (j - 1))
        return jnp.where(j == 0, (b * s) // TL + bi_, lat)

    main = pl.BlockSpec((rows, LANES), lambda bi_, j: (chunk(bi_, j), 0))
    prev = pl.BlockSpec((hrows, LANES),
                        lambda bi_, j: (jnp.maximum(chunk(bi_, j) * (TL // HALO) - 1, 0), 0))
    nxt = pl.BlockSpec((hrows, LANES),
                       lambda bi_, j: (jnp.minimum((chunk(bi_, j) + 1) * (TL // HALO), n_halo - 1), 0))
    full = lambda shape: pl.BlockSpec(shape, lambda bi_, j: (0,) * len(shape))
    d = RG_BLOCKS * LANES
    in_specs = [main, prev, nxt, full((CONV_W, SUBLANES, LANES)), full((SUBLANES, LANES)),
                full((RG_BLOCKS, LANES, LANES)), full((RG_BLOCKS, LANES, LANES)),
                full((1, d)), full((1, d)), full((1, d))]
    args = [xin8, xin8, xin8, conv_w.reshape(CONV_W, SUBLANES, LANES), conv_b.reshape(SUBLANES, LANES),
            wa.astype(BF16), wi.astype(BF16), ba.reshape(1, d), bi.reshape(1, d), lam.reshape(1, d)]
    scratch = [pltpu.VMEM((rows + 2 * hrows, LANES), F32), pltpu.VMEM((rows, LANES), F32),
               pltpu.VMEM((rows, LANES), F32), pltpu.VMEM((rows, LANES), F32)]
    if reverse:
        in_specs += [main, main]
        args += [hf8, gg8]
        scratch.append(pltpu.VMEM((rows, LANES), F32))
    scratch.append(pltpu.VMEM((SUBLANES, LANES), F32))
    return pl.pallas_call(
        functools.partial(_rg_scan_kernel, reverse=reverse, nlat=nlat),
        grid=(b, nlat + 1),
        in_specs=in_specs,
        out_specs=main,
        out_shape=jax.ShapeDtypeStruct(xin8.shape, F32),
        scratch_shapes=scratch,
        compiler_params=_cp("parallel", "arbitrary"),
        name="rg_scan_bwd" if reverse else "rg_scan_fwd",
    )(*args)


def _rglru_mixer(tok, g, mod, w_in, conv_w, conv_b, wa, ba, wi, bi, lam, w_out, s, c, b, nrows_out):
    t, d = tok.shape
    gg, xin = _rg_in(tok, g, mod, w_in.astype(BF16), s, b)
    xin8 = xin.reshape(t * SUBLANES, LANES)
    gg8 = gg.reshape(t * SUBLANES, LANES)
    hf8 = _rg_scan(xin8, conv_w, conv_b, wa[0], wi[0], ba[0], bi[0], lam[0], s, c, b, False)
    y8 = _rg_scan(xin8, conv_w, conv_b, wa[1], wi[1], ba[1], bi[1], lam[1], s, c, b, True, hf8, gg8)
    y = y8.reshape(t, d)
    return _out_proj(y, tok, mod, w_out.astype(BF16), nrows_out, s, b, 2)


def _rope_tables(s):
    pos = jnp.arange(s, dtype=F32)
    row = jnp.floor(pos / GRID_W)
    col = pos - row * GRID_W
    n_freq = HEAD_DIM // 4
    inv = ROPE_THETA ** (-jnp.arange(n_freq, dtype=F32) * 2.0 / (HEAD_DIM // 2))
    ar = row[:, None] * inv
    ac = col[:, None] * inv
    cos = jnp.concatenate([jnp.cos(ar), jnp.cos(ar), jnp.cos(ac), jnp.cos(ac)], axis=1)
    sin = jnp.concatenate([-jnp.sin(ar), jnp.sin(ar), -jnp.sin(ac), jnp.sin(ac)], axis=1)
    cos = jnp.concatenate([cos, jnp.ones((TM, HEAD_DIM), F32)], axis=0)
    sin = jnp.concatenate([sin, jnp.zeros((TM, HEAD_DIM), F32)], axis=0)
    return cos, sin


def _qkv_kernel(x_ref, g_ref, mod_ref, w_ref, qg_ref, kg_ref, cos_ref, sin_ref, q_ref, k_ref, v_ref):
    d = x_ref.shape[-1]
    m = mod_ref[0]
    h = _norm_mod(x_ref[...], g_ref[...], m[:, 0:d], m[:, d:2 * d])
    z = jnp.dot(h.astype(BF16), w_ref[...], preferred_element_type=F32)
    cos = cos_ref[...]
    sin = sin_ref[...]
    lane = lax.broadcasted_iota(I32, cos.shape, 1)
    first_half = (lane % (HEAD_DIM // 2)) < (HEAD_DIM // 4)

    def head(zc, gain):
        ms = jnp.mean(zc * zc, axis=-1, keepdims=True)
        y = zc * lax.rsqrt(ms + NORM_EPS) * gain
        partner = jnp.where(first_half, pltpu.roll(y, HEAD_DIM - HEAD_DIM // 4, 1),
                            pltpu.roll(y, HEAD_DIM // 4, 1))
        return y * cos + partner * sin

    nq = q_ref.shape[-1] // HEAD_DIM
    nk = k_ref.shape[-1] // HEAD_DIM
    for j in range(nq):
        q_ref[:, j * HEAD_DIM:(j + 1) * HEAD_DIM] = (
            head(z[:, j * HEAD_DIM:(j + 1) * HEAD_DIM], qg_ref[...]) * (HEAD_DIM ** -0.5)).astype(BF16)
    for j in range(nk):
        c0 = (nq + j) * HEAD_DIM
        k_ref[:, j * HEAD_DIM:(j + 1) * HEAD_DIM] = head(z[:, c0:c0 + HEAD_DIM], kg_ref[...]).astype(BF16)
    v_ref[...] = z[:, (nq + nk) * HEAD_DIM:].astype(BF16)


def _qkv(x, g, mod, w, qg, kg, cos, sin, s, b):
    t, d = x.shape
    nkv = N_KV_HEADS * HEAD_DIM
    n_pos = s // TM
    return pl.pallas_call(
        _qkv_kernel,
        grid=(t // TM,),
        in_specs=[pl.BlockSpec((TM, d), lambda i: (i, 0)),
                  pl.BlockSpec((1, d), lambda i: (0, 0)),
                  _mod_spec(d, s // TM, b),
                  pl.BlockSpec(w.shape, lambda i: (0, 0)),
                  pl.BlockSpec((1, HEAD_DIM), lambda i: (0, 0)),
                  pl.BlockSpec((1, HEAD_DIM), lambda i: (0, 0)),
                  pl.BlockSpec((TM, HEAD_DIM), lambda i: (jnp.where(i < b * n_pos, i % n_pos, n_pos), 0)),
                  pl.BlockSpec((TM, HEAD_DIM), lambda i: (jnp.where(i < b * n_pos, i % n_pos, n_pos), 0))],
        out_specs=[pl.BlockSpec((TM, d), lambda i: (i, 0)),
                   pl.BlockSpec((TM, nkv), lambda i: (i, 0)),
                   pl.BlockSpec((TM, nkv), lambda i: (i, 0))],
        out_shape=[jax.ShapeDtypeStruct((t, d), BF16), jax.ShapeDtypeStruct((t, nkv), BF16),
                   jax.ShapeDtypeStruct((t, nkv), BF16)],
        compiler_params=_cp("parallel"),
        name="qkv_proj",
    )(x, g, mod, w, qg, kg, cos, sin)


def _attn_kernel(q_ref, kc_ref, vc_ref, *rest, with_lat):
    if with_lat:
        kl_ref, vl_ref, o_ref = rest
    else:
        (o_ref,) = rest
    nt = (((1,), (1,)), ((), ()))
    for g in range(GQA_GROUP):
        cols = slice(g * HEAD_DIM, (g + 1) * HEAD_DIM)
        q = q_ref[:, cols]
        sc = lax.dot_general(q, kc_ref[...], nt, preferred_element_type=F32)
        mx = jnp.max(sc, axis=-1, keepdims=True)
        if with_lat:
            sl = lax.dot_general(q, kl_ref[...], nt, preferred_element_type=F32)
            mx = jnp.maximum(mx, jnp.max(sl, axis=-1, keepdims=True))
        pc = jnp.exp(sc - mx)
        den = jnp.sum(pc, axis=-1, keepdims=True)
        o = jnp.dot(pc.astype(BF16), vc_ref[...], preferred_element_type=F32)
        if with_lat:
            pl_ = jnp.exp(sl - mx)
            den = den + jnp.sum(pl_, axis=-1, keepdims=True)
            o = o + jnp.dot(pl_.astype(BF16), vl_ref[...], preferred_element_type=F32)
        o_ref[:, cols] = (o / den).astype(BF16)


def _attention(q, k, v, s, c, b):
    t, d = q.shape
    gw = GQA_GROUP * HEAD_DIM
    tq = TM
    nq = s // tq
    o_lat = pl.pallas_call(
        functools.partial(_attn_kernel, with_lat=True),
        grid=(b, N_KV_HEADS, nq),
        in_specs=[pl.BlockSpec((tq, gw), lambda bi, h, i: (bi * nq + i, h)),
                  pl.BlockSpec((c, HEAD_DIM), lambda bi, h, i: ((b * s) // c + bi, h)),
                  pl.BlockSpec((c, HEAD_DIM), lambda bi, h, i: ((b * s) // c + bi, h)),
                  pl.BlockSpec((s, HEAD_DIM), lambda bi, h, i: (bi, h)),
                  pl.BlockSpec((s, HEAD_DIM), lambda bi, h, i: (bi, h))],
        out_specs=pl.BlockSpec((tq, gw), lambda bi, h, i: (bi * nq + i, h)),
        out_shape=jax.ShapeDtypeStruct((b * s, d), BF16),
        compiler_params=_cp("parallel", "parallel", "arbitrary"),
        name="attn_lat",
    )(q, k, v, k, v)
    o_ctx = pl.pallas_call(
        functools.partial(_attn_kernel, with_lat=False),
        grid=(b, N_KV_HEADS),
        in_specs=[pl.BlockSpec((c, gw), lambda bi, h: ((b * s) // c + bi, h)),
                  pl.BlockSpec((c, HEAD_DIM), lambda bi, h: ((b * s) // c + bi, h)),
                  pl.BlockSpec((c, HEAD_DIM), lambda bi, h: ((b * s) // c + bi, h))],
        out_specs=pl.BlockSpec((c, gw), lambda bi, h: (bi, h)),
        out_shape=jax.ShapeDtypeStruct((b * c, d), BF16),
        compiler_params=_cp("parallel", "parallel"),
        name="attn_ctx",
    )(q, k, v)
    return jnp.concatenate([o_lat, o_ctx], axis=0)


def _attention_mixer(tok, g, mod, w_qkv, qg, kg, w_o, s, c, b):
    cos, sin = _rope_tables(s)
    q, k, v = _qkv(tok, g, mod, w_qkv.astype(BF16), qg.reshape(1, -1), kg.reshape(1, -1), cos, sin, s, b)
    o = _attention(q, k, v, s, c, b)
    return _out_proj(o, tok, mod, w_o.astype(BF16), tok.shape[0], s, b, 2)


def _gmlp_kernel(x_ref, g_ref, mod_ref, w_in_ref, lng_ref, lnb_ref, ws_ref, bs_ref, w_out_ref, o_ref, uv_ref):
    d = x_ref.shape[-1]
    dcm = lng_ref.shape[-1]
    gw = dcm // CM_GROUPS
    x = x_ref[...]
    m = mod_ref[0]
    h = _norm_mod(x, g_ref[...], m[:, 0:d], m[:, d:2 * d])
    z = jax.nn.gelu(jnp.dot(h.astype(BF16), w_in_ref[...], preferred_element_type=F32))
    u = z[:, :dcm]
    v = z[:, dcm:]
    mu = jnp.mean(v, axis=-1, keepdims=True)
    vc = v - mu
    var = jnp.mean(vc * vc, axis=-1, keepdims=True)
    vn = (vc * lax.rsqrt(var + NORM_EPS) * lng_ref[...] + lnb_ref[...]).astype(BF16)
    for ck in range(x.shape[0] // CHUNK):
        rows = slice(ck * CHUNK, (ck + 1) * CHUNK)
        for gi in range(CM_GROUPS):
            cols = slice(gi * gw, (gi + 1) * gw)
            mix = jnp.dot(ws_ref[gi], vn[rows, cols], preferred_element_type=F32) + bs_ref[:, gi:gi + 1]
            uv_ref[rows, cols] = (u[rows, cols] * mix).astype(BF16)
    y = jnp.dot(uv_ref[...], w_out_ref[...], preferred_element_type=F32)
    o_ref[...] = x + m[:, 2 * d:3 * d] * y


def _gmlp_mixer(tok, g, mod, w_in, ln_g, ln_b, w_s, b_s, w_out, s, b):
    t, d = tok.shape
    dcm = ln_g.shape[-1]
    full = lambda shape: pl.BlockSpec(shape, lambda i: (0,) * len(shape))
    return pl.pallas_call(
        _gmlp_kernel,
        grid=(t // TM,),
        in_specs=[pl.BlockSpec((TM, d), lambda i: (i, 0)),
                  full((1, d)),
                  _mod_spec(d, s // TM, b),
                  full((d, 2 * dcm)), full((1, dcm)), full((1, dcm)),
                  full((CM_GROUPS, CHUNK, CHUNK)), full((CHUNK, CM_GROUPS)), full((dcm, d))],
        out_specs=pl.BlockSpec((TM, d), lambda i: (i, 0)),
        out_shape=jax.ShapeDtypeStruct((t, d), F32),
        scratch_shapes=[pltpu.VMEM((TM, dcm), BF16)],
        compiler_params=_cp("parallel"),
        name="gmlp",
    )(tok, g, mod, w_in.astype(BF16), ln_g.reshape(1, dcm), ln_b.reshape(1, dcm),
      w_s.astype(BF16), b_s.T, w_out.astype(BF16))


def _router_kernel(x_ref, g_ref, mod_ref, wr_ref, br_ref, hf_ref, eid_ref, ew_ref):
    d = x_ref.shape[-1]
    m = mod_ref[0]
    hf = _norm_mod(x_ref[...], g_ref[...], m[:, 3 * d:4 * d], m[:, 4 * d:5 * d])
    hf_ref[...] = hf
    logits = jnp.dot(hf, wr_ref[...], preferred_element_type=F32, precision=HIGHEST) + br_ref[...]
    lane = lax.broadcasted_iota(I32, logits.shape, 1)
    neg = -jnp.inf
    gl = jnp.where(lane < N_GROUPS, logits, neg)
    gmax = jnp.max(gl, axis=-1, keepdims=True)
    gsel = jnp.min(jnp.where(gl == gmax, lane, LANES), axis=-1, keepdims=True)
    gate_g = 1.0 / jnp.sum(jnp.exp(gl - gmax), axis=-1, keepdims=True)
    lo = N_GROUPS + gsel * EXPERTS_PER_GROUP
    el = jnp.where(jnp.logical_and(lane >= lo, lane < lo + EXPERTS_PER_GROUP), logits, neg)
    v1 = jnp.max(el, axis=-1, keepdims=True)
    i1 = jnp.min(jnp.where(el == v1, lane, LANES), axis=-1, keepdims=True)
    el2 = jnp.where(lane == i1, neg, el)
    v2 = jnp.max(el2, axis=-1, keepdims=True)
    i2 = jnp.min(jnp.where(el2 == v2, lane, LANES), axis=-1, keepdims=True)
    e21 = jnp.exp(v2 - v1)
    w1 = gate_g / (1.0 + e21)
    w2 = w1 * e21
    eid_ref[...] = jnp.where(lane == 0, i1 - N_GROUPS, jnp.where(lane == 1, i2 - N_GROUPS, 0))
    ew_ref[...] = jnp.where(lane == 0, w1, jnp.where(lane == 1, w2, 0.0))


def _router(x, g, mod, wr, br, nrows, s, b):
    d = x.shape[-1]
    row = lambda w: pl.BlockSpec((TM, w), lambda i: (i, 0))
    return pl.pallas_call(
        _router_kernel,
        grid=(nrows // TM,),
        in_specs=[row(d), pl.BlockSpec((1, d), lambda i: (0, 0)), _mod_spec(d, s // TM, b),
                  pl.BlockSpec((d, LANES), lambda i: (0, 0)), pl.BlockSpec((1, LANES), lambda i: (0, 0))],
        out_specs=[row(d), row(LANES), row(LANES)],
        out_shape=[jax.ShapeDtypeStruct((nrows, d), F32), jax.ShapeDtypeStruct((nrows, LANES), I32),
                   jax.ShapeDtypeStruct((nrows, LANES), F32)],
        compiler_params=_cp("parallel"),
        name="moe_router",
    )(x, g, mod, wr, br)


def _dispatch_plan(eid, bm):
    n_tok = eid.shape[0]
    n_asg = 2 * n_tok
    flat_e = eid.reshape(-1)
    onehot = (flat_e[:, None] == jnp.arange(N_EXPERTS, dtype=I32)[None, :]).astype(I32)
    csum = jnp.cumsum(onehot, axis=0)
    counts = csum[-1]
    rank = jnp.take_along_axis(csum, flat_e[:, None], axis=1)[:, 0] - 1
    padded = (counts + bm - 1) // bm * bm
    pad_end = jnp.cumsum(padded)
    pad_start = pad_end - padded
    pos = (pad_start[flat_e] + rank).astype(I32)
    n_rows = n_asg + N_EXPERTS * bm
    n_blk = n_rows // bm
    row_tok = jnp.zeros((n_rows,), I32).at[pos].set(jnp.arange(n_asg, dtype=I32) // 2, unique_indices=True)
    n_used = (pad_end[-1] // bm).astype(I32)
    blk = jnp.arange(n_blk, dtype=I32)
    blk_e = jnp.searchsorted(pad_end, jnp.minimum(blk, n_used - 1) * bm, side="right").astype(I32)
    blk_e = jnp.minimum(blk_e, N_EXPERTS - 1)
    return pos, row_tok, blk_e, n_used.reshape(1)


def _expert_kernel(blk_e_ref, n_used_ref, row_tok_ref, hf_hbm, wg_ref, wu_ref, wd_ref, y_ref,
                   xbuf, sem, wgb, wub, wdb):
    i = pl.program_id(0)
    n_used = n_used_ref[0]
    bm = xbuf.shape[1]

    def gather(blk, slot):
        def body(r, carry):
            tok = row_tok_ref[blk * bm + r]
            pltpu.make_async_copy(hf_hbm.at[pl.ds(tok, 1)], xbuf.at[slot, pl.ds(r, 1)], sem.at[slot]).start()
            return carry
        lax.fori_loop(0, bm, body, 0)

    @pl.when(i == 0)
    def _():
        gather(0, 0)

    @pl.when(i + 1 < n_used)
    def _():
        gather(i + 1, (i + 1) % 2)

    @pl.when(i < n_used)
    def _():
        slot = i % 2
        pltpu.make_async_copy(hf_hbm.at[pl.ds(0, bm)], xbuf.at[slot], sem.at[slot]).wait()
        changed = jnp.logical_or(i == 0, blk_e_ref[i] != blk_e_ref[jnp.maximum(i - 1, 0)])

        @pl.when(changed)
        def _():
            wgb[...] = wg_ref[0].astype(BF16)
            wub[...] = wu_ref[0].astype(BF16)
            wdb[...] = wd_ref[0].astype(BF16)

        xb = xbuf[slot].astype(BF16)
        gt = jnp.dot(xb, wgb[...], preferred_element_type=F32)
        up = jnp.dot(xb, wub[...], preferred_element_type=F32)
        act = (gt * jax.nn.sigmoid(gt) * up).astype(BF16)
        y_ref[...] = jnp.dot(act, wdb[...], preferred_element_type=F32)

    @pl.when(i >= n_used)
    def _():
        y_ref[...] = jnp.zeros_like(y_ref)


def _experts(hf, row_tok, blk_e, n_used, w_gate, w_up, w_down):
    d = hf.shape[-1]
    de = w_gate.shape[-1]
    bm = MOE_BM
    n_rows = row_tok.shape[0]
    grid_spec = pltpu.PrefetchScalarGridSpec(
        num_scalar_prefetch=3,
        grid=(n_rows // bm,),
        in_specs=[pl.BlockSpec(memory_space=pl.ANY),
                  pl.BlockSpec((1, d, de), lambda i, be, nu, rt: (be[i], 0, 0)),
                  pl.BlockSpec((1, d, de), lambda i, be, nu, rt: (be[i], 0, 0)),
                  pl.BlockSpec((1, de, d), lambda i, be, nu, rt: (be[i], 0, 0))],
        out_specs=pl.BlockSpec((bm, d), lambda i, be, nu, rt: (i, 0)),
        scratch_shapes=[pltpu.VMEM((2, bm, d), F32), pltpu.SemaphoreType.DMA((2,)),
                        pltpu.VMEM((d, de), BF16), pltpu.VMEM((d, de), BF16), pltpu.VMEM((de, d), BF16)],
    )
    return pl.pallas_call(
        _expert_kernel,
        grid_spec=grid_spec,
        out_shape=jax.ShapeDtypeStruct((n_rows, d), F32),
        compiler_params=_cp("arbitrary"),
        name="moe_experts",
    )(blk_e, n_used, row_tok, hf, w_gate, w_up, w_down)


def _combine_kernel(pos_ref, y_hbm, x_ref, ew_ref, mod_ref, o_ref, ybuf, sem):
    i = pl.program_id(0)
    n_blk = pl.num_programs(0)
    tc = x_ref.shape[0]
    d = x_ref.shape[1]

    def gather(blk, slot):
        def body(r, carry):
            for k in range(2):
                p = pos_ref[(blk * tc + r) * 2 + k]
                pltpu.make_async_copy(y_hbm.at[pl.ds(p, 1)], ybuf.at[slot, k, pl.ds(r, 1)], sem.at[slot]).start()
            return carry
        lax.fori_loop(0, tc, body, 0)

    @pl.when(i == 0)
    def _():
        gather(0, 0)

    @pl.when(i + 1 < n_blk)
    def _():
        gather(i + 1, (i + 1) % 2)

    slot = i % 2
    for k in range(2):
        pltpu.make_async_copy(y_hbm.at[pl.ds(0, tc)], ybuf.at[slot, k], sem.at[slot]).wait()
    ew = ew_ref[...]
    y = ew[:, 0:1] * ybuf[slot, 0] + ew[:, 1:2] * ybuf[slot, 1]
    o_ref[...] = x_ref[...] + mod_ref[0][:, 5 * d:6 * d] * y


def _combine(pos, y_rows, x, ew, mod, nrows, s, b):
    d = x.shape[-1]
    tc = MOE_TC
    grid_spec = pltpu.PrefetchScalarGridSpec(
        num_scalar_prefetch=1,
        grid=(nrows // tc,),
        in_specs=[pl.BlockSpec(memory_space=pl.ANY),
                  pl.BlockSpec((tc, d), lambda i, p: (i, 0)),
                  pl.BlockSpec((tc, LANES), lambda i, p: (i, 0)),
                  _mod_spec(d, s // tc, b)],
        out_specs=pl.BlockSpec((tc, d), lambda i, p: (i, 0)),
        scratch_shapes=[pltpu.VMEM((2, 2, tc, d), F32), pltpu.SemaphoreType.DMA((2,))],
    )
    return pl.pallas_call(
        _combine_kernel,
        grid_spec=grid_spec,
        out_shape=jax.ShapeDtypeStruct((nrows, d), F32),
        compiler_params=_cp("arbitrary"),
        name="moe_combine",
    )(pos, y_rows, x, ew, mod)


def _hier_moe(x, g, mod, w_group, b_group, w_router, b_router, w_gate, w_up, w_down, nrows, s, b):
    d = x.shape[-1]
    pad = LANES - N_GROUPS - N_EXPERTS
    wr = jnp.concatenate([w_group, w_router.reshape(d, N_EXPERTS), jnp.zeros((d, pad), F32)], axis=1)
    br = jnp.concatenate([b_group, b_router.reshape(N_EXPERTS), jnp.zeros((pad,), F32)]).reshape(1, LANES)
    hf, eid, ew = _router(x, g, mod, wr, br, nrows, s, b)
    pos, row_tok, blk_e, n_used = _dispatch_plan(eid[:, :2], MOE_BM)
    y_rows = _experts(hf, row_tok, blk_e, n_used, w_gate, w_up, w_down)
    return _combine(pos, y_rows, x, ew, mod, nrows, s, b)


def kernel(x, c, ctx, c_ctx, ada_w, ada_b, norm_mix_g, norm_ffn_g, rg_w_in, rg_conv_w, rg_conv_b, rg_wa, rg_ba, rg_wi, rg_bi, rg_lambda, rg_w_out, at_w_qkv, at_q_g, at_k_g, at_w_o, cm_w_in, cm_ln_g, cm_ln_b, cm_w_s, cm_b_s, cm_w_out, moe_w_group, moe_b_group, moe_w_router, moe_b_router, moe_w_gate, moe_w_up, moe_w_down):
    b, s, d = x.shape
    cl = ctx.shape[1]
    depth = ada_w.shape[0]
    n_lat = b * s
    assert b < SUBLANES and s % TM == 0 and cl % TM == 0 and d == RG_BLOCKS * LANES

    cin = jnp.concatenate([c, c_ctx[None, :], jnp.zeros((SUBLANES - b - 1, d), F32)], axis=0)
    mod_all = _ada_table(cin, ada_w, ada_b).reshape(depth, SUBLANES, 1, N_MOD * d)
    tok = jnp.concatenate([x.reshape(n_lat, d), ctx.reshape(b * cl, d)], axis=0)

    for l in range(depth):
        kind = l % 3
        j = l // 3
        last = l == depth - 1
        mod = mod_all[l]
        g_mix = norm_mix_g[l].reshape(1, d)
        nrows = n_lat if last else tok.shape[0]
        if kind == 0:
            tok_mix = _rglru_mixer(tok, g_mix, mod, rg_w_in[j], rg_conv_w[j], rg_conv_b[j], rg_wa[j], rg_ba[j],
                                   rg_wi[j], rg_bi[j], rg_lambda[j], rg_w_out[j], s, cl, b, nrows)
        elif kind == 1:
            tok_mix = _attention_mixer(tok, g_mix, mod, at_w_qkv[j], at_q_g[j], at_k_g[j], at_w_o[j], s, cl, b)
        else:
            tok_mix = _gmlp_mixer(tok, g_mix, mod, cm_w_in[j], cm_ln_g[j], cm_ln_b[j], cm_w_s[j], cm_b_s[j],
                                  cm_w_out[j], s, b)
        tok = _hier_moe(tok_mix, norm_ffn_g[l].reshape(1, d), mod, moe_w_group[l], moe_b_group[l],
                        moe_w_router[l], moe_b_router[l], moe_w_gate[l], moe_w_up[l], moe_w_down[l],
                        nrows, s, b)
    return tok[:n_lat].reshape(b, s, d)
```

```python
import functools

import jax
import jax.numpy as jnp
from jax import lax
from jax.experimental import pallas as pl
from jax.experimental.pallas import tpu as pltpu
from jax.experimental.pallas import tpu_sc as plsc

F32 = jnp.float32
BF16 = jnp.bfloat16
I32 = jnp.int32
HIGHEST = lax.Precision.HIGHEST

NORM_EPS = 1e-6
N_MOD = 6
GRID_W = 64
RG_BLOCKS = 8
CONV_W = 4
RG_C = 8.0
HEAD_DIM = 128
N_KV_HEADS = 2
GQA_GROUP = 4
ROPE_THETA = 10000.0
CHUNK = 128
CM_GROUPS = 8
N_GROUPS = 4
EXPERTS_PER_GROUP = 8
N_EXPERTS = N_GROUPS * EXPERTS_PER_GROUP

LANES = 128
SUBLANES = 8
TM = 256
TL = 256
HALO = 8
MOE_BM = 256
SC_CORES = 2
SC_WORKERS = 32
SC_CHUNK = 32
VMEM_LIMIT = 52 * 2**20


def _cp(*sem):
    return pltpu.CompilerParams(dimension_semantics=sem, vmem_limit_bytes=VMEM_LIMIT)


def _norm_mod(x, g, shift, scale):
    ms = jnp.mean(x * x, axis=-1, keepdims=True)
    y = x * lax.rsqrt(ms + NORM_EPS) * g
    return y * (1.0 + scale) + shift


def _mod_spec(d, rows_per_sample, n_samples):
    return pl.BlockSpec((1, 1, N_MOD * d),
                        lambda i, *_: (jnp.minimum(i // rows_per_sample, n_samples), 0, 0))


def _ada_kernel(c_ref, w_ref, b_ref, o_ref):
    cin = c_ref[...]
    act = cin * jax.nn.sigmoid(cin)
    o_ref[0] = jnp.dot(act, w_ref[0], preferred_element_type=F32, precision=HIGHEST) + b_ref[0]


def _ada_table(cin, ada_w, ada_b):
    depth, d, n = ada_w.shape
    tn = d
    return pl.pallas_call(
        _ada_kernel,
        grid=(depth, n // tn),
        in_specs=[pl.BlockSpec((SUBLANES, d), lambda l, j: (0, 0)),
                  pl.BlockSpec((1, d, tn), lambda l, j: (l, 0, j)),
                  pl.BlockSpec((1, 1, tn), lambda l, j: (l, 0, j))],
        out_specs=pl.BlockSpec((1, SUBLANES, tn), lambda l, j: (l, 0, j)),
        out_shape=jax.ShapeDtypeStruct((depth, SUBLANES, n), F32),
        compiler_params=_cp("parallel", "parallel"),
        name="ada_table",
    )(cin, ada_w, ada_b.reshape(depth, 1, n))


def _out_kernel(y_ref, x_ref, mod_ref, w_ref, o_ref, *, gate_idx):
    d = x_ref.shape[-1]
    gate = mod_ref[0][:, gate_idx * d:(gate_idx + 1) * d]
    y = jnp.dot(y_ref[...].astype(BF16), w_ref[...], preferred_element_type=F32)
    o_ref[...] = x_ref[...] + gate * y


def _out_proj(y, x, mod, w, nrows, s, b, gate_idx):
    d = x.shape[-1]
    k = y.shape[-1]
    return pl.pallas_call(
        functools.partial(_out_kernel, gate_idx=gate_idx),
        grid=(nrows // TM,),
        in_specs=[pl.BlockSpec((TM, k), lambda i: (i, 0)),
                  pl.BlockSpec((TM, d), lambda i: (i, 0)),
                  _mod_spec(d, s // TM, b),
                  pl.BlockSpec((k, d), lambda i: (0, 0))],
        out_specs=pl.BlockSpec((TM, d), lambda i: (i, 0)),
        out_shape=jax.ShapeDtypeStruct((nrows, d), F32),
        compiler_params=_cp("parallel"),
        name="out_proj",
    )(y, x, mod, w)


def _rg_in_kernel(x_ref, g_ref, mod_ref, w_ref, gg_ref, xin_ref):
    d = x_ref.shape[-1]
    m = mod_ref[0]
    h = _norm_mod(x_ref[...], g_ref[...], m[:, 0:d], m[:, d:2 * d])
    z = jnp.dot(h.astype(BF16), w_ref[...], preferred_element_type=F32)
    gg_ref[...] = jax.nn.gelu(z[:, :d])
    xin_ref[...] = z[:, d:]


def _rg_in(x, g, mod, w, s, b):
    t, d = x.shape
    return pl.pallas_call(
        _rg_in_kernel,
        grid=(t // TM,),
        in_specs=[pl.BlockSpec((TM, d), lambda i: (i, 0)),
                  pl.BlockSpec((1, d), lambda i: (0, 0)),
                  _mod_spec(d, s // TM, b),
                  pl.BlockSpec((d, 2 * d), lambda i: (0, 0))],
        out_specs=[pl.BlockSpec((TM, d), lambda i: (i, 0)),
                   pl.BlockSpec((TM, d), lambda i: (i, 0))],
        out_shape=[jax.ShapeDtypeStruct((t, d), F32), jax.ShapeDtypeStruct((t, d), F32)],
        compiler_params=_cp("parallel"),
        name="rg_in",
    )(x, g, mod, w)


def _rg_scan_kernel(xm_ref, xprev_ref, xnext_ref, cw_ref, cb_ref, wa_ref, wi_ref,
                    ba_ref, bi_ref, lam_ref, *rest, reverse, nlat):
    if reverse:
        hf_ref, gg_ref, out_ref, xpad, xc, a_s, b_s, h_s, hcar = rest
    else:
        out_ref, xpad, xc, a_s, b_s, hcar = rest
    rows = TL * SUBLANES
    hrows = HALO * SUBLANES
    j = pl.program_id(1)
    m = (nlat - j) if reverse else (j - 1)
    has_prev = jnp.logical_and(j >= 1, m > 0)
    has_next = jnp.logical_and(j >= 1, m < nlat - 1)

    @pl.when(j == 0)
    def _():
        hcar[...] = jnp.zeros_like(hcar)

    xpad[0:hrows, :] = jnp.where(has_prev, xprev_ref[...], 0.0)
    xpad[hrows:hrows + rows, :] = xm_ref[...]
    xpad[hrows + rows:2 * hrows + rows, :] = jnp.where(has_next, xnext_ref[...], 0.0)
    acc = jnp.broadcast_to(cb_ref[...][None], (TL, SUBLANES, LANES))
    for k in range(CONV_W):
        off = (HALO + k - CONV_W // 2) * SUBLANES
        tap = xpad[off:off + rows, :].reshape(TL, SUBLANES, LANES)
        acc = acc + tap * cw_ref[k][None]
    xc[...] = acc.reshape(rows, LANES)

    for n in range(RG_BLOCKS):
        cols = slice(n * LANES, (n + 1) * LANES)
        xn = xc[pl.ds(n, TL, stride=SUBLANES), :]
        xb = xn.astype(BF16)
        r = jax.nn.sigmoid(jnp.dot(xb, wa_ref[n], preferred_element_type=F32) + ba_ref[:, cols])
        gi = jax.nn.sigmoid(jnp.dot(xb, wi_ref[n], preferred_element_type=F32) + bi_ref[:, cols])
        a = jnp.exp(-RG_C * r * jax.nn.softplus(-lam_ref[:, cols]))
        a_s[pl.ds(n, TL, stride=SUBLANES), :] = a
        b_s[pl.ds(n, TL, stride=SUBLANES), :] = jnp.sqrt(1.0 - a * a) * gi * xn

    h_dst = h_s if reverse else out_ref

    def step(s, h):
        t = (TL - 1 - s) if reverse else s
        r0 = pl.multiple_of(t * SUBLANES, SUBLANES)
        h = a_s[pl.ds(r0, SUBLANES), :] * h + b_s[pl.ds(r0, SUBLANES), :]
        h_dst[pl.ds(r0, SUBLANES), :] = h
        return h

    hcar[...] = lax.fori_loop(0, TL, step, hcar[...], unroll=8)
    if reverse:
        out_ref[...] = gg_ref[...] * (hf_ref[...] + h_s[...])


def _rg_scan(xin8, conv_w, conv_b, wa, wi, ba, bi, lam, s, c, b, reverse, hf8=None, gg8=None):
    assert c == TL and s % TL == 0
    rows = TL * SUBLANES
    hrows = HALO * SUBLANES
    nlat = s // TL
    t = xin8.shape[0] // SUBLANES
    n_halo = t // HALO

    def chunk(bi_, j):
        lat = bi_ * nlat + ((nlat - j) if reverse else (j - 1))
        return jnp.where(j == 0, (b * s) // TL + bi_, lat)

    main = pl.BlockSpec((rows, LANES), lambda bi_, j: (chunk(bi_, j), 0))
    prev = pl.BlockSpec((hrows, LANES),
                        lambda bi_, j: (jnp.maximum(chunk(bi_, j) * (TL // HALO) - 1, 0), 0))
    nxt = pl.BlockSpec((hrows, LANES),
                       lambda bi_, j: (jnp.minimum((chunk(bi_, j) + 1) * (TL // HALO), n_halo - 1), 0))
    full = lambda shape: pl.BlockSpec(shape, lambda bi_, j: (0,) * len(shape))
    d = RG_BLOCKS * LANES
    in_specs = [main, prev, nxt, full((CONV_W, SUBLANES, LANES)), full((SUBLANES, LANES)),
                full((RG_BLOCKS, LANES, LANES)), full((RG_BLOCKS, LANES, LANES)),
                full((1, d)), full((1, d)), full((1, d))]
    args = [xin8, xin8, xin8, conv_w.reshape(CONV_W, SUBLANES, LANES), conv_b.reshape(SUBLANES, LANES),
            wa.astype(BF16), wi.astype(BF16), ba.reshape(1, d), bi.reshape(1, d), lam.reshape(1, d)]
    scratch = [pltpu.VMEM((rows + 2 * hrows, LANES), F32), pltpu.VMEM((rows, LANES), F32),
               pltpu.VMEM((rows, LANES), F32), pltpu.VMEM((rows, LANES), F32)]
    if reverse:
        in_specs += [main, main]
        args += [hf8, gg8]
        scratch.append(pltpu.VMEM((rows, LANES), F32))
    scratch.append(pltpu.VMEM((SUBLANES, LANES), F32))
    return pl.pallas_call(
        functools.partial(_rg_scan_kernel, reverse=reverse, nlat=nlat),
        grid=(b, nlat + 1),
        in_specs=in_specs,
        out_specs=main,
        out_shape=jax.ShapeDtypeStruct(xin8.shape, F32),
        scratch_shapes=scratch,
        compiler_params=_cp("parallel", "arbitrary"),
        name="rg_scan_bwd" if reverse else "rg_scan_fwd",
    )(*args)


def _rglru_mixer(tok, g, mod, w_in, conv_w, conv_b, wa, ba, wi, bi, lam, w_out, s, c, b, nrows_out):
    t, d = tok.shape
    gg, xin = _rg_in(tok, g, mod, w_in.astype(BF16), s, b)
    xin8 = xin.reshape(t * SUBLANES, LANES)
    gg8 = gg.reshape(t * SUBLANES, LANES)
    hf8 = _rg_scan(xin8, conv_w, conv_b, wa[0], wi[0], ba[0], bi[0], lam[0], s, c, b, False)
    y8 = _rg_scan(xin8, conv_w, conv_b, wa[1], wi[1], ba[1], bi[1], lam[1], s, c, b, True, hf8, gg8)
    y = y8.reshape(t, d)
    return _out_proj(y, tok, mod, w_out.astype(BF16), nrows_out, s, b, 2)


def _rope_tables(s):
    pos = jnp.arange(s, dtype=F32)
    row = jnp.floor(pos / GRID_W)
    col = pos - row * GRID_W
    n_freq = HEAD_DIM // 4
    inv = ROPE_THETA ** (-jnp.arange(n_freq, dtype=F32) * 2.0 / (HEAD_DIM // 2))
    ar = row[:, None] * inv
    ac = col[:, None] * inv
    cos = jnp.concatenate([jnp.cos(ar), jnp.cos(ar), jnp.cos(ac), jnp.cos(ac)], axis=1)
    sin = jnp.concatenate([-jnp.sin(ar), jnp.sin(ar), -jnp.sin(ac), jnp.sin(ac)], axis=1)
    cos = jnp.concatenate([cos, jnp.ones((TM, HEAD_DIM), F32)], axis=0)
    sin = jnp.concatenate([sin, jnp.zeros((TM, HEAD_DIM), F32)], axis=0)
    return cos, sin


def _qkv_kernel(x_ref, g_ref, mod_ref, w_ref, qg_ref, kg_ref, cos_ref, sin_ref, q_ref, k_ref, v_ref):
    d = x_ref.shape[-1]
    m = mod_ref[0]
    h = _norm_mod(x_ref[...], g_ref[...], m[:, 0:d], m[:, d:2 * d])
    z = jnp.dot(h.astype(BF16), w_ref[...], preferred_element_type=F32)
    cos = cos_ref[...]
    sin = sin_ref[...]
    lane = lax.broadcasted_iota(I32, cos.shape, 1)
    first_half = (lane % (HEAD_DIM // 2)) < (HEAD_DIM // 4)

    def head(zc, gain):
        ms = jnp.mean(zc * zc, axis=-1, keepdims=True)
        y = zc * lax.rsqrt(ms + NORM_EPS) * gain
        partner = jnp.where(first_half, pltpu.roll(y, HEAD_DIM - HEAD_DIM // 4, 1),
                            pltpu.roll(y, HEAD_DIM // 4, 1))
        return y * cos + partner * sin

    nq = q_ref.shape[-1] // HEAD_DIM
    nk = k_ref.shape[-1] // HEAD_DIM
    for j in range(nq):
        q_ref[:, j * HEAD_DIM:(j + 1) * HEAD_DIM] = (
            head(z[:, j * HEAD_DIM:(j + 1) * HEAD_DIM], qg_ref[...]) * (HEAD_DIM ** -0.5)).astype(BF16)
    for j in range(nk):
        c0 = (nq + j) * HEAD_DIM
        k_ref[:, j * HEAD_DIM:(j + 1) * HEAD_DIM] = head(z[:, c0:c0 + HEAD_DIM], kg_ref[...]).astype(BF16)
    v_ref[...] = z[:, (nq + nk) * HEAD_DIM:].astype(BF16)


def _qkv(x, g, mod, w, qg, kg, cos, sin, s, b):
    t, d = x.shape
    nkv = N_KV_HEADS * HEAD_DIM
    n_pos = s // TM
    return pl.pallas_call(
        _qkv_kernel,
        grid=(t // TM,),
        in_specs=[pl.BlockSpec((TM, d), lambda i: (i, 0)),
                  pl.BlockSpec((1, d), lambda i: (0, 0)),
                  _mod_spec(d, s // TM, b),
                  pl.BlockSpec(w.shape, lambda i: (0, 0)),
                  pl.BlockSpec((1, HEAD_DIM), lambda i: (0, 0)),
                  pl.BlockSpec((1, HEAD_DIM), lambda i: (0, 0)),
                  pl.BlockSpec((TM, HEAD_DIM), lambda i: (jnp.where(i < b * n_pos, i % n_pos, n_pos), 0)),
                  pl.BlockSpec((TM, HEAD_DIM), lambda i: (jnp.where(i < b * n_pos, i % n_pos, n_pos), 0))],
        out_specs=[pl.BlockSpec((TM, d), lambda i: (i, 0)),
                   pl.BlockSpec((TM, nkv), lambda i: (i, 0)),
                   pl.BlockSpec((TM, nkv), lambda i: (i, 0))],
        out_shape=[jax.ShapeDtypeStruct((t, d), BF16), jax.ShapeDtypeStruct((t, nkv), BF16),
                   jax.ShapeDtypeStruct((t, nkv), BF16)],
        compiler_params=_cp("parallel"),
        name="qkv_proj",
    )(x, g, mod, w, qg, kg, cos, sin)


def _attn_kernel(q_ref, kc_ref, vc_ref, *rest, with_lat):
    if with_lat:
        kl_ref, vl_ref, o_ref = rest
    else:
        (o_ref,) = rest
    nt = (((1,), (1,)), ((), ()))
    for g in range(GQA_GROUP):
        cols = slice(g * HEAD_DIM, (g + 1) * HEAD_DIM)
        q = q_ref[:, cols]
        sc = lax.dot_general(q, kc_ref[...], nt, preferred_element_type=F32)
        mx = jnp.max(sc, axis=-1, keepdims=True)
        if with_lat:
            sl = lax.dot_general(q, kl_ref[...], nt, preferred_element_type=F32)
            mx = jnp.maximum(mx, jnp.max(sl, axis=-1, keepdims=True))
        pc = jnp.exp(sc - mx)
        den = jnp.sum(pc, axis=-1, keepdims=True)
        o = jnp.dot(pc.astype(BF16), vc_ref[...], preferred_element_type=F32)
        if with_lat:
            pl_ = jnp.exp(sl - mx)
            den = den + jnp.sum(pl_, axis=-1, keepdims=True)
            o = o + jnp.dot(pl_.astype(BF16), vl_ref[...], preferred_element_type=F32)
        o_ref[:, cols] = (o / den).astype(BF16)


def _attention(q, k, v, s, c, b):
    t, d = q.shape
    gw = GQA_GROUP * HEAD_DIM
    tq = TM
    nq = s // tq
    o_lat = pl.pallas_call(
        functools.partial(_attn_kernel, with_lat=True),
        grid=(b, N_KV_HEADS, nq),
        in_specs=[pl.BlockSpec((tq, gw), lambda bi, h, i: (bi * nq + i, h)),
                  pl.BlockSpec((c, HEAD_DIM), lambda bi, h, i: ((b * s) // c + bi, h)),
                  pl.BlockSpec((c, HEAD_DIM), lambda bi, h, i: ((b * s) // c + bi, h)),
                  pl.BlockSpec((s, HEAD_DIM), lambda bi, h, i: (bi, h)),
                  pl.BlockSpec((s, HEAD_DIM), lambda bi, h, i: (bi, h))],
        out_specs=pl.BlockSpec((tq, gw), lambda bi, h, i: (bi * nq + i, h)),
        out_shape=jax.ShapeDtypeStruct((b * s, d), BF16),
        compiler_params=_cp("parallel", "parallel", "arbitrary"),
        name="attn_lat",
    )(q, k, v, k, v)
    o_ctx = pl.pallas_call(
        functools.partial(_attn_kernel, with_lat=False),
        grid=(b, N_KV_HEADS),
        in_specs=[pl.BlockSpec((c, gw), lambda bi, h: ((b * s) // c + bi, h)),
                  pl.BlockSpec((c, HEAD_DIM), lambda bi, h: ((b * s) // c + bi, h)),
                  pl.BlockSpec((c, HEAD_DIM), lambda bi, h: ((b * s) // c + bi, h))],
        out_specs=pl.BlockSpec((c, gw), lambda bi, h: (bi, h)),
        out_shape=jax.ShapeDtypeStruct((b * c, d), BF16),
        compiler_params=_cp("parallel", "parallel"),
        name="attn_ctx",
    )(q, k, v)
    return jnp.concatenate([o_lat, o_ctx], axis=0)


def _attention_mixer(tok, g, mod, w_qkv, qg, kg, w_o, s, c, b):
    cos, sin = _rope_tables(s)
    q, k, v = _qkv(tok, g, mod, w_qkv.astype(BF16), qg.reshape(1, -1), kg.reshape(1, -1), cos, sin, s, b)
    o = _attention(q, k, v, s, c, b)
    return _out_proj(o, tok, mod, w_o.astype(BF16), tok.shape[0], s, b, 2)


def _gmlp_kernel(x_ref, g_ref, mod_ref, w_in_ref, lng_ref, lnb_ref, ws_ref, bs_ref, w_out_ref, o_ref, uv_ref):
    d = x_ref.shape[-1]
    dcm = lng_ref.shape[-1]
    gw = dcm // CM_GROUPS
    x = x_ref[...]
    m = mod_ref[0]
    h = _norm_mod(x, g_ref[...], m[:, 0:d], m[:, d:2 * d])
    z = jax.nn.gelu(jnp.dot(h.astype(BF16), w_in_ref[...], preferred_element_type=F32))
    u = z[:, :dcm]
    v = z[:, dcm:]
    mu = jnp.mean(v, axis=-1, keepdims=True)
    vc = v - mu
    var = jnp.mean(vc * vc, axis=-1, keepdims=True)
    vn = (vc * lax.rsqrt(var + NORM_EPS) * lng_ref[...] + lnb_ref[...]).astype(BF16)
    for ck in range(x.shape[0] // CHUNK):
        rows = slice(ck * CHUNK, (ck + 1) * CHUNK)
        for gi in range(CM_GROUPS):
            cols = slice(gi * gw, (gi + 1) * gw)
            mix = jnp.dot(ws_ref[gi], vn[rows, cols], preferred_element_type=F32) + bs_ref[:, gi:gi + 1]
            uv_ref[rows, cols] = (u[rows, cols] * mix).astype(BF16)
    y = jnp.dot(uv_ref[...], w_out_ref[...], preferred_element_type=F32)
    o_ref[...] = x + m[:, 2 * d:3 * d] * y


def _gmlp_mixer(tok, g, mod, w_in, ln_g, ln_b, w_s, b_s, w_out, s, b):
    t, d = tok.shape
    dcm = ln_g.shape[-1]
    full = lambda shape: pl.BlockSpec(shape, lambda i: (0,) * len(shape))
    return pl.pallas_call(
        _gmlp_kernel,
        grid=(t // TM,),
        in_specs=[pl.BlockSpec((TM, d), lambda i: (i, 0)),
                  full((1, d)),
                  _mod_spec(d, s // TM, b),
                  full((d, 2 * dcm)), full((1, dcm)), full((1, dcm)),
                  full((CM_GROUPS, CHUNK, CHUNK)), full((CHUNK, CM_GROUPS)), full((dcm, d))],
        out_specs=pl.BlockSpec((TM, d), lambda i: (i, 0)),
        out_shape=jax.ShapeDtypeStruct((t, d), F32),
        scratch_shapes=[pltpu.VMEM((TM, dcm), BF16)],
        compiler_params=_cp("parallel"),
        name="gmlp",
    )(tok, g, mod, w_in.astype(BF16), ln_g.reshape(1, dcm), ln_b.reshape(1, dcm),
      w_s.astype(BF16), b_s.T, w_out.astype(BF16))


def _router_kernel(x_ref, g_ref, mod_ref, wr_ref, br_ref, hf_ref, rt_ref, ew_ref, cnt_ref, cnt_s):
    d = x_ref.shape[-1]
    tm = x_ref.shape[0]

    @pl.when(pl.program_id(0) == 0)
    def _():
        cnt_s[...] = jnp.zeros_like(cnt_s)

    m = mod_ref[0]
    hf = _norm_mod(x_ref[...], g_ref[...], m[:, 3 * d:4 * d], m[:, 4 * d:5 * d])
    hf_ref[...] = hf
    logits = jnp.dot(hf, wr_ref[...], preferred_element_type=F32, precision=HIGHEST) + br_ref[...]
    lane = lax.broadcasted_iota(I32, logits.shape, 1)
    neg = -jnp.inf
    gl = jnp.where(lane < N_GROUPS, logits, neg)
    gmax = jnp.max(gl, axis=-1, keepdims=True)
    gsel = jnp.min(jnp.where(gl == gmax, lane, LANES), axis=-1, keepdims=True)
    gate_g = 1.0 / jnp.sum(jnp.exp(gl - gmax), axis=-1, keepdims=True)
    lo = N_GROUPS + gsel * EXPERTS_PER_GROUP
    el = jnp.where(jnp.logical_and(lane >= lo, lane < lo + EXPERTS_PER_GROUP), logits, neg)
    v1 = jnp.max(el, axis=-1, keepdims=True)
    i1 = jnp.min(jnp.where(el == v1, lane, LANES), axis=-1, keepdims=True)
    el2 = jnp.where(lane == i1, neg, el)
    v2 = jnp.max(el2, axis=-1, keepdims=True)
    i2 = jnp.min(jnp.where(el2 == v2, lane, LANES), axis=-1, keepdims=True)
    e21 = jnp.exp(v2 - v1)
    w1 = gate_g / (1.0 + e21)
    w2 = w1 * e21
    ew_ref[...] = jnp.where(lane == 0, w1, jnp.where(lane == 1, w2, 0.0))

    oh1 = lane == i1
    oh2 = lane == i2
    above = (lax.broadcasted_iota(I32, (tm, tm), 1) < lax.broadcasted_iota(I32, (tm, tm), 0)).astype(BF16)
    pre1 = jnp.dot(above, oh1.astype(BF16), preferred_element_type=F32)
    pre2 = jnp.dot(above, oh2.astype(BF16), preferred_element_type=F32)
    tot1 = jnp.sum(oh1.astype(F32), axis=0, keepdims=True)
    tot2 = jnp.sum(oh2.astype(F32), axis=0, keepdims=True)
    cnt = cnt_s[...]
    rank1 = jnp.sum(jnp.where(oh1, cnt + pre1, 0.0), axis=-1, keepdims=True).astype(I32)
    rank2 = jnp.sum(jnp.where(oh2, cnt + tot1 + pre2, 0.0), axis=-1, keepdims=True).astype(I32)
    cnt = cnt + tot1 + tot2
    cnt_s[...] = cnt
    cnt_ref[...] = jnp.broadcast_to(cnt, cnt_ref.shape)
    rt_ref[...] = jnp.where(lane == 0, i1 - N_GROUPS, jnp.where(lane == 1, i2 - N_GROUPS,
                            jnp.where(lane == 2, rank1, jnp.where(lane == 3, rank2, 0))))


def _router(x, g, mod, wr, br, nrows, s, b):
    d = x.shape[-1]
    row = lambda w: pl.BlockSpec((TM, w), lambda i: (i, 0))
    return pl.pallas_call(
        _router_kernel,
        grid=(nrows // TM,),
        in_specs=[row(d), pl.BlockSpec((1, d), lambda i: (0, 0)), _mod_spec(d, s // TM, b),
                  pl.BlockSpec((d, LANES), lambda i: (0, 0)), pl.BlockSpec((1, LANES), lambda i: (0, 0))],
        out_specs=[row(d), row(LANES), row(LANES), pl.BlockSpec((SUBLANES, LANES), lambda i: (0, 0))],
        out_shape=[jax.ShapeDtypeStruct((nrows, d), F32), jax.ShapeDtypeStruct((nrows, LANES), I32),
                   jax.ShapeDtypeStruct((nrows, LANES), F32), jax.ShapeDtypeStruct((SUBLANES, LANES), F32)],
        scratch_shapes=[pltpu.VMEM((1, LANES), F32)],
        compiler_params=_cp("arbitrary"),
        name="moe_router",
    )(x, g, mod, wr, br)


def _dispatch_plan(rt, cnt, bm):
    n_tok = rt.shape[0]
    counts = cnt[0, N_GROUPS:N_GROUPS + N_EXPERTS].astype(I32)
    padded = (counts + bm - 1) // bm * bm
    pad_end = jnp.cumsum(padded)
    pad_start = pad_end - padded
    experts = jnp.arange(N_EXPERTS, dtype=I32)
    start_of = jnp.sum(jnp.where(rt[:, 0:2, None] == experts, pad_start, 0), axis=-1)
    pos = (start_of + rt[:, 2:4]).astype(I32)
    n_rows = 2 * n_tok + N_EXPERTS * bm
    n_used = (pad_end[-1] // bm).astype(I32)
    first_row = jnp.minimum(jnp.arange(n_rows // bm, dtype=I32), n_used - 1) * bm
    blk_e = jnp.sum((pad_end[None, :] <= first_row[:, None]).astype(I32), axis=1)
    blk_e = jnp.minimum(blk_e, N_EXPERTS - 1).astype(I32)
    return pos[:, 0], pos[:, 1], blk_e, n_used.reshape(1), n_rows


def _sc_mesh():
    return plsc.VectorSubcoreMesh(core_axis_name="c", subcore_axis_name="s")


def _sc_worker_base(per_worker):
    return (lax.axis_index("s") * SC_CORES + lax.axis_index("c")) * per_worker


def _sc_dispatch(hf, pos0, pos1, n_rows):
    t, d = hf.shape
    per_w = t // SC_WORKERS
    assert per_w * SC_WORKERS == t and per_w % SC_CHUNK == 0

    @functools.partial(
        pl.kernel, mesh=_sc_mesh(), out_type=jax.ShapeDtypeStruct((n_rows, d), hf.dtype),
        scratch_types=[pltpu.VMEM((SC_CHUNK,), I32), pltpu.VMEM((SC_CHUNK,), I32),
                       pltpu.VMEM((SC_CHUNK, d), hf.dtype)])
    def dispatch(hf_hbm, p0_hbm, p1_hbm, out_hbm, i0_v, i1_v, rows_v):
        base = _sc_worker_base(per_w)

        @pl.loop(0, per_w // SC_CHUNK)
        def _(ck):
            off = pl.multiple_of(base + ck * SC_CHUNK, SUBLANES)
            pltpu.sync_copy(hf_hbm.at[pl.ds(off, SC_CHUNK)], rows_v)
            pltpu.sync_copy(p0_hbm.at[pl.ds(off, SC_CHUNK)], i0_v)
            pltpu.sync_copy(p1_hbm.at[pl.ds(off, SC_CHUNK)], i1_v)
            pltpu.sync_copy(rows_v, out_hbm.at[i0_v])
            pltpu.sync_copy(rows_v, out_hbm.at[i1_v])

    return dispatch(hf, pos0, pos1)


def _sc_gather(rows, idx):
    n = idx.shape[0]
    d = rows.shape[1]
    per_w = n // SC_WORKERS
    assert per_w * SC_WORKERS == n and per_w % SC_CHUNK == 0

    @functools.partial(
        pl.kernel, mesh=_sc_mesh(), out_type=jax.ShapeDtypeStruct((n, d), rows.dtype),
        scratch_types=[pltpu.VMEM((SC_CHUNK,), I32), pltpu.VMEM((SC_CHUNK, d), rows.dtype)])
    def gather(rows_hbm, i_hbm, out_hbm, i_v, rows_v):
        base = _sc_worker_base(per_w)

        @pl.loop(0, per_w // SC_CHUNK)
        def _(ck):
            off = pl.multiple_of(base + ck * SC_CHUNK, SUBLANES)
            pltpu.sync_copy(i_hbm.at[pl.ds(off, SC_CHUNK)], i_v)
            pltpu.sync_copy(rows_hbm.at[i_v], rows_v)
            pltpu.sync_copy(rows_v, out_hbm.at[pl.ds(off, SC_CHUNK)])

    return gather(rows, idx)


def _expert_kernel(blk_e_ref, n_used_ref, x_ref, wg_ref, wu_ref, wd_ref, y_ref, wgb, wub, wdb):
    i = pl.program_id(0)
    n_used = n_used_ref[0]

    @pl.when(i < n_used)
    def _():
        changed = jnp.logical_or(i == 0, blk_e_ref[i] != blk_e_ref[jnp.maximum(i - 1, 0)])

        @pl.when(changed)
        def _():
            wgb[...] = wg_ref[0].astype(BF16)
            wub[...] = wu_ref[0].astype(BF16)
            wdb[...] = wd_ref[0].astype(BF16)

        xb = x_ref[...].astype(BF16)
        gt = jnp.dot(xb, wgb[...], preferred_element_type=F32)
        up = jnp.dot(xb, wub[...], preferred_element_type=F32)
        act = (gt * jax.nn.sigmoid(gt) * up).astype(BF16)
        y_ref[...] = jnp.dot(act, wdb[...], preferred_element_type=F32)

    @pl.when(i >= n_used)
    def _():
        y_ref[...] = jnp.zeros_like(y_ref)


def _experts(x_rows, blk_e, n_used, w_gate, w_up, w_down, layer):
    n_rows, d = x_rows.shape
    depth, n_e, _, de = w_gate.shape
    bm = MOE_BM
    w_idx = lambda i, be, nu: (layer * n_e + be[i], 0, 0)
    grid_spec = pltpu.PrefetchScalarGridSpec(
        num_scalar_prefetch=2,
        grid=(n_rows // bm,),
        in_specs=[pl.BlockSpec((bm, d), lambda i, be, nu: (jnp.minimum(i, nu[0] - 1), 0)),
                  pl.BlockSpec((1, d, de), w_idx),
                  pl.BlockSpec((1, d, de), w_idx),
                  pl.BlockSpec((1, de, d), w_idx)],
        out_specs=pl.BlockSpec((bm, d), lambda i, be, nu: (i, 0)),
        scratch_shapes=[pltpu.VMEM((d, de), BF16), pltpu.VMEM((d, de), BF16), pltpu.VMEM((de, d), BF16)],
    )
    return pl.pallas_call(
        _expert_kernel,
        grid_spec=grid_spec,
        out_shape=jax.ShapeDtypeStruct((n_rows, d), F32),
        compiler_params=_cp("arbitrary"),
        name="moe_experts",
    )(blk_e, n_used, x_rows, w_gate.reshape(depth * n_e, d, de), w_up.reshape(depth * n_e, d, de),
      w_down.reshape(depth * n_e, de, d))


def _combine_kernel(x_ref, y0_ref, y1_ref, ew_ref, mod_ref, o_ref):
    d = x_ref.shape[1]
    ew = ew_ref[...]
    y = ew[:, 0:1] * y0_ref[...] + ew[:, 1:2] * y1_ref[...]
    o_ref[...] = x_ref[...] + mod_ref[0][:, 5 * d:6 * d] * y


def _combine(y01, x, ew, mod, nrows, s, b):
    d = x.shape[-1]
    nb = nrows // TM
    return pl.pallas_call(
        _combine_kernel,
        grid=(nb,),
        in_specs=[pl.BlockSpec((TM, d), lambda i: (i, 0)),
                  pl.BlockSpec((TM, d), lambda i: (i, 0)),
                  pl.BlockSpec((TM, d), lambda i: (i + nb, 0)),
                  pl.BlockSpec((TM, LANES), lambda i: (i, 0)),
                  _mod_spec(d, s // TM, b)],
        out_specs=pl.BlockSpec((TM, d), lambda i: (i, 0)),
        out_shape=jax.ShapeDtypeStruct((nrows, d), F32),
        compiler_params=_cp("parallel"),
        name="moe_combine",
    )(x, y01, y01, ew, mod)


def _hier_moe(x, g, mod, w_group, b_group, w_router, b_router, w_gate, w_up, w_down, layer, nrows, s, b):
    d = x.shape[-1]
    pad = LANES - N_GROUPS - N_EXPERTS
    wr = jnp.concatenate([w_group, w_router.reshape(d, N_EXPERTS), jnp.zeros((d, pad), F32)], axis=1)
    br = jnp.concatenate([b_group, b_router.reshape(N_EXPERTS), jnp.zeros((pad,), F32)]).reshape(1, LANES)
    hf, rt, ew, cnt = _router(x, g, mod, wr, br, nrows, s, b)
    pos0, pos1, blk_e, n_used, n_rows = _dispatch_plan(rt, cnt, MOE_BM)
    x_rows = _sc_dispatch(hf, pos0, pos1, n_rows)
    y_rows = _experts(x_rows, blk_e, n_used, w_gate, w_up, w_down, layer)
    y01 = _sc_gather(y_rows, jnp.concatenate([pos0, pos1]))
    return _combine(y01, x, ew, mod, nrows, s, b)


def kernel(x, c, ctx, c_ctx, ada_w, ada_b, norm_mix_g, norm_ffn_g, rg_w_in, rg_conv_w, rg_conv_b, rg_wa, rg_ba, rg_wi, rg_bi, rg_lambda, rg_w_out, at_w_qkv, at_q_g, at_k_g, at_w_o, cm_w_in, cm_ln_g, cm_ln_b, cm_w_s, cm_b_s, cm_w_out, moe_w_group, moe_b_group, moe_w_router, moe_b_router, moe_w_gate, moe_w_up, moe_w_down):
    b, s, d = x.shape
    cl = ctx.shape[1]
    depth = ada_w.shape[0]
    n_lat = b * s
    assert b < SUBLANES and s % TM == 0 and cl % TM == 0 and d == RG_BLOCKS * LANES

    cin = jnp.concatenate([c, c_ctx[None, :], jnp.zeros((SUBLANES - b - 1, d), F32)], axis=0)
    mod_all = _ada_table(cin, ada_w, ada_b).reshape(depth, SUBLANES, 1, N_MOD * d)
    tok = jnp.concatenate([x.reshape(n_lat, d), ctx.reshape(b * cl, d)], axis=0)

    for l in range(depth):
        kind = l % 3
        j = l // 3
        last = l == depth - 1
        mod = mod_all[l]
        g_mix = norm_mix_g[l].reshape(1, d)
        nrows = n_lat if last else tok.shape[0]
        if kind == 0:
            tok_mix = _rglru_mixer(tok, g_mix, mod, rg_w_in[j], rg_conv_w[j], rg_conv_b[j], rg_wa[j], rg_ba[j],
                                   rg_wi[j], rg_bi[j], rg_lambda[j], rg_w_out[j], s, cl, b, nrows)
        elif kind == 1:
            tok_mix = _attention_mixer(tok, g_mix, mod, at_w_qkv[j], at_q_g[j], at_k_g[j], at_w_o[j], s, cl, b)
        else:
            tok_mix = _gmlp_mixer(tok, g_mix, mod, cm_w_in[j], cm_ln_g[j], cm_ln_b[j], cm_w_s[j], cm_b_s[j],
                                  cm_w_out[j], s, b)
        tok = _hier_moe(tok_mix, norm_ffn_g[l].reshape(1, d), mod, moe_w_group[l], moe_b_group[l],
                        moe_w_router[l], moe_b_router[l], moe_w_gate, moe_w_up, moe_w_down, l,
                        nrows, s, b)
    return tok[:n_lat].reshape(b, s, d)
```

```python
import functools

import jax
import jax.numpy as jnp
from jax import lax
from jax.experimental import pallas as pl
from jax.experimental.pallas import tpu as pltpu
from jax.experimental.pallas import tpu_sc as plsc

F32 = jnp.float32
BF16 = jnp.bfloat16
I32 = jnp.int32
U32 = jnp.uint32
HIGHEST = lax.Precision.HIGHEST

NORM_EPS = 1e-6
N_MOD = 6
GRID_W = 64
RG_BLOCKS = 8
CONV_W = 4
RG_C = 8.0
HEAD_DIM = 128
N_KV_HEADS = 2
GQA_GROUP = 4
ROPE_THETA = 10000.0
CHUNK = 128
CM_GROUPS = 8
N_GROUPS = 4
EXPERTS_PER_GROUP = 8
N_EXPERTS = N_GROUPS * EXPERTS_PER_GROUP

LANES = 128
SUBLANES = 8
TM = 256
TL = 256
HALO = 8
ATT_KC = 512
LOG2E = 1.4426950408889634
MOE_BM = 256
SC_CORES = 2
SC_WORKERS = 32
SC_CHUNK = 32
VMEM_LIMIT = 52 * 2**20


def _cp(*sem):
    return pltpu.CompilerParams(dimension_semantics=sem, vmem_limit_bytes=VMEM_LIMIT)


def _norm_mod(x, g, shift, scale):
    ms = jnp.mean(x * x, axis=-1, keepdims=True)
    y = x * lax.rsqrt(ms + NORM_EPS) * g
    return y * (1.0 + scale) + shift


def _sigmoid(x):
    return 0.5 * jnp.tanh(0.5 * x) + 0.5


def _pack_bf16_pairs(x):
    h = x.shape[-1] // 2
    hi = lax.bitcast_convert_type(x[:, :h].astype(BF16).astype(F32), U32)
    lo = lax.bitcast_convert_type(x[:, h:].astype(BF16).astype(F32), U32)
    return lax.bitcast_convert_type(hi | (lo >> 16), I32)


def _unpack_bf16_pairs(w):
    u = lax.bitcast_convert_type(w, U32)
    hi = lax.bitcast_convert_type(u & jnp.uint32(0xFFFF0000), F32)
    lo = lax.bitcast_convert_type(u << 16, F32)
    return hi, lo


def _mod_spec(d, rows_per_sample, n_samples):
    return pl.BlockSpec((1, 1, N_MOD * d),
                        lambda i, *_: (jnp.minimum(i // rows_per_sample, n_samples), 0, 0))


def _ada_kernel(c_ref, w_ref, b_ref, o_ref):
    cin = c_ref[...]
    act = cin * jax.nn.sigmoid(cin)
    o_ref[0] = jnp.dot(act, w_ref[0], preferred_element_type=F32, precision=HIGHEST) + b_ref[0]


def _ada_table(cin, ada_w, ada_b):
    depth, d, n = ada_w.shape
    tn = d
    return pl.pallas_call(
        _ada_kernel,
        grid=(depth, n // tn),
        in_specs=[pl.BlockSpec((SUBLANES, d), lambda l, j: (0, 0)),
                  pl.BlockSpec((1, d, tn), lambda l, j: (l, 0, j)),
                  pl.BlockSpec((1, 1, tn), lambda l, j: (l, 0, j))],
        out_specs=pl.BlockSpec((1, SUBLANES, tn), lambda l, j: (l, 0, j)),
        out_shape=jax.ShapeDtypeStruct((depth, SUBLANES, n), F32),
        compiler_params=_cp("parallel", "parallel"),
        name="ada_table",
    )(cin, ada_w, ada_b.reshape(depth, 1, n))


def _out_kernel(y_ref, x_ref, mod_ref, w_ref, o_ref, *, gate_idx):
    d = x_ref.shape[-1]
    gate = mod_ref[0][:, gate_idx * d:(gate_idx + 1) * d]
    y = jnp.dot(y_ref[...].astype(BF16), w_ref[...], preferred_element_type=F32)
    o_ref[...] = x_ref[...] + gate * y


def _out_proj(y, x, mod, w, nrows, s, b, gate_idx):
    d = x.shape[-1]
    k = y.shape[-1]
    return pl.pallas_call(
        functools.partial(_out_kernel, gate_idx=gate_idx),
        grid=(nrows // TM,),
        in_specs=[pl.BlockSpec((TM, k), lambda i: (i, 0)),
                  pl.BlockSpec((TM, d), lambda i: (i, 0)),
                  _mod_spec(d, s // TM, b),
                  pl.BlockSpec((k, d), lambda i: (0, 0))],
        out_specs=pl.BlockSpec((TM, d), lambda i: (i, 0)),
        out_shape=jax.ShapeDtypeStruct((nrows, d), F32),
        compiler_params=_cp("parallel"),
        name="out_proj",
    )(y, x, mod, w)


def _rg_in_kernel(x_ref, g_ref, mod_ref, w_ref, gg_ref, xin_ref):
    d = x_ref.shape[-1]
    m = mod_ref[0]
    h = _norm_mod(x_ref[...], g_ref[...], m[:, 0:d], m[:, d:2 * d])
    z = jnp.dot(h.astype(BF16), w_ref[...], preferred_element_type=F32)
    tm = x_ref.shape[0]
    for n in range(d // LANES):
        cols = slice(n * LANES, (n + 1) * LANES)
        gg_ref[pl.ds(n, tm, stride=SUBLANES), :] = jax.nn.gelu(z[:, cols])
        xin_ref[pl.ds(n, tm, stride=SUBLANES), :] = z[:, d + n * LANES:d + (n + 1) * LANES]


def _rg_in(x, g, mod, w, s, b):
    t, d = x.shape
    assert d == SUBLANES * LANES
    return pl.pallas_call(
        _rg_in_kernel,
        grid=(t // TM,),
        in_specs=[pl.BlockSpec((TM, d), lambda i: (i, 0)),
                  pl.BlockSpec((1, d), lambda i: (0, 0)),
                  _mod_spec(d, s // TM, b),
                  pl.BlockSpec((d, 2 * d), lambda i: (0, 0))],
        out_specs=[pl.BlockSpec((TM * SUBLANES, LANES), lambda i: (i, 0)),
                   pl.BlockSpec((TM * SUBLANES, LANES), lambda i: (i, 0))],
        out_shape=[jax.ShapeDtypeStruct((t * SUBLANES, LANES), F32),
                   jax.ShapeDtypeStruct((t * SUBLANES, LANES), F32)],
        compiler_params=_cp("parallel"),
        name="rg_in",
    )(x, g, mod, w)


def _rg_scan_kernel(xm_ref, xprev_ref, xnext_ref, cw_ref, cb_ref, wa_ref, wi_ref,
                    ba_ref, bi_ref, lam_ref, *rest, reverse, nlat):
    if reverse:
        hf_ref, gg_ref, out_ref, xpad, xc, a_s, b_s, h_s, hcar = rest
    else:
        out_ref, xpad, xc, a_s, b_s, hcar = rest
    rows = TL * SUBLANES
    hrows = HALO * SUBLANES
    j = pl.program_id(1)
    m = (nlat - j) if reverse else (j - 1)
    has_prev = jnp.logical_and(j >= 1, m > 0)
    has_next = jnp.logical_and(j >= 1, m < nlat - 1)

    @pl.when(j == 0)
    def _():
        hcar[...] = jnp.zeros_like(hcar)

    xpad[0:hrows, :] = jnp.where(has_prev, xprev_ref[...], 0.0)
    xpad[hrows:hrows + rows, :] = xm_ref[...]
    xpad[hrows + rows:2 * hrows + rows, :] = jnp.where(has_next, xnext_ref[...], 0.0)
    acc = jnp.broadcast_to(cb_ref[...][None], (TL, SUBLANES, LANES))
    for k in range(CONV_W):
        off = (HALO + k - CONV_W // 2) * SUBLANES
        tap = xpad[off:off + rows, :].reshape(TL, SUBLANES, LANES)
        acc = acc + tap * cw_ref[k][None]
    xc[...] = acc.reshape(rows, LANES)

    for n in range(RG_BLOCKS):
        cols = slice(n * LANES, (n + 1) * LANES)
        xn = xc[pl.ds(n, TL, stride=SUBLANES), :]
        xb = xn.astype(BF16)
        r = _sigmoid(jnp.dot(xb, wa_ref[n], preferred_element_type=F32) + ba_ref[:, cols])
        gi = _sigmoid(jnp.dot(xb, wi_ref[n], preferred_element_type=F32) + bi_ref[:, cols])
        a = jnp.exp(-RG_C * r * jax.nn.softplus(-lam_ref[:, cols]))
        a_s[pl.ds(n, TL, stride=SUBLANES), :] = a
        b_s[pl.ds(n, TL, stride=SUBLANES), :] = jnp.sqrt(1.0 - a * a) * gi * xn

    h_dst = h_s if reverse else out_ref

    def step(s, h):
        t = (TL - 1 - s) if reverse else s
        r0 = pl.multiple_of(t * SUBLANES, SUBLANES)
        h = a_s[pl.ds(r0, SUBLANES), :] * h + b_s[pl.ds(r0, SUBLANES), :]
        h_dst[pl.ds(r0, SUBLANES), :] = h
        return h

    hcar[...] = lax.fori_loop(0, TL, step, hcar[...], unroll=8)
    if reverse:
        h_s[...] = gg_ref[...] * (hf_ref[...] + h_s[...])
        for n in range(RG_BLOCKS):
            out_ref[:, n * LANES:(n + 1) * LANES] = h_s[pl.ds(n, TL, stride=SUBLANES), :].astype(BF16)


def _rg_scan(xin8, conv_w, conv_b, wa, wi, ba, bi, lam, s, c, b, reverse, hf8=None, gg8=None):
    assert c == TL and s % TL == 0
    rows = TL * SUBLANES
    hrows = HALO * SUBLANES
    nlat = s // TL
    t = xin8.shape[0] // SUBLANES
    n_halo = t // HALO

    def chunk(bi_, j):
        lat = bi_ * nlat + ((nlat - j) if reverse else (j - 1))
        return jnp.where(j == 0, (b * s) // TL + bi_, lat)

    main = pl.BlockSpec((rows, LANES), lambda bi_, j: (chunk(bi_, j), 0))
    prev = pl.BlockSpec((hrows, LANES),
                        lambda bi_, j: (jnp.maximum(chunk(bi_, j) * (TL // HALO) - 1, 0), 0))
    nxt = pl.BlockSpec((hrows, LANES),
                       lambda bi_, j: (jnp.minimum((chunk(bi_, j) + 1) * (TL // HALO), n_halo - 1), 0))
    full = lambda shape: pl.BlockSpec(shape, lambda bi_, j: (0,) * len(shape))
    d = RG_BLOCKS * LANES
    in_specs = [main, prev, nxt, full((CONV_W, SUBLANES, LANES)), full((SUBLANES, LANES)),
                full((RG_BLOCKS, LANES, LANES)), full((RG_BLOCKS, LANES, LANES)),
                full((1, d)), full((1, d)), full((1, d))]
    args = [xin8, xin8, xin8, conv_w.reshape(CONV_W, SUBLANES, LANES), conv_b.reshape(SUBLANES, LANES),
            wa.astype(BF16), wi.astype(BF16), ba.reshape(1, d), bi.reshape(1, d), lam.reshape(1, d)]
    scratch = [pltpu.VMEM((rows + 2 * hrows, LANES), F32), pltpu.VMEM((rows, LANES), F32),
               pltpu.VMEM((rows, LANES), F32), pltpu.VMEM((rows, LANES), F32)]
    if reverse:
        in_specs += [main, main]
        args += [hf8, gg8]
        scratch.append(pltpu.VMEM((rows, LANES), F32))
    scratch.append(pltpu.VMEM((SUBLANES, LANES), F32))
    return pl.pallas_call(
        functools.partial(_rg_scan_kernel, reverse=reverse, nlat=nlat),
        grid=(b, nlat + 1),
        in_specs=in_specs,
        out_specs=pl.BlockSpec((TL, d), lambda bi_, j: (chunk(bi_, j), 0)) if reverse else main,
        out_shape=jax.ShapeDtypeStruct((t, d), BF16) if reverse else jax.ShapeDtypeStruct(xin8.shape, F32),
        scratch_shapes=scratch,
        compiler_params=_cp("parallel", "arbitrary"),
        name="rg_scan_bwd" if reverse else "rg_scan_fwd",
    )(*args)


def _rglru_mixer(tok, g, mod, w_in, conv_w, conv_b, wa, ba, wi, bi, lam, w_out, s, c, b, nrows_out):
    gg8, xin8 = _rg_in(tok, g, mod, w_in.astype(BF16), s, b)
    hf8 = _rg_scan(xin8, conv_w, conv_b, wa[0], wi[0], ba[0], bi[0], lam[0], s, c, b, False)
    y = _rg_scan(xin8, conv_w, conv_b, wa[1], wi[1], ba[1], bi[1], lam[1], s, c, b, True, hf8, gg8)
    return _out_proj(y, tok, mod, w_out.astype(BF16), nrows_out, s, b, 2)


def _rope_tables(s):
    pos = jnp.arange(s, dtype=F32)
    row = jnp.floor(pos / GRID_W)
    col = pos - row * GRID_W
    n_freq = HEAD_DIM // 4
    inv = ROPE_THETA ** (-jnp.arange(n_freq, dtype=F32) * 2.0 / (HEAD_DIM // 2))
    ar = row[:, None] * inv
    ac = col[:, None] * inv
    cos = jnp.concatenate([jnp.cos(ar), jnp.cos(ar), jnp.cos(ac), jnp.cos(ac)], axis=1)
    sin = jnp.concatenate([-jnp.sin(ar), jnp.sin(ar), -jnp.sin(ac), jnp.sin(ac)], axis=1)
    cos = jnp.concatenate([cos, jnp.ones((TM, HEAD_DIM), F32)], axis=0)
    sin = jnp.concatenate([sin, jnp.zeros((TM, HEAD_DIM), F32)], axis=0)
    return cos, sin


def _qkv_kernel(x_ref, g_ref, mod_ref, w_ref, qg_ref, kg_ref, cos_ref, sin_ref, q_ref, k_ref, v_ref):
    d = x_ref.shape[-1]
    m = mod_ref[0]
    h = _norm_mod(x_ref[...], g_ref[...], m[:, 0:d], m[:, d:2 * d])
    z = jnp.dot(h.astype(BF16), w_ref[...], preferred_element_type=F32)
    cos = cos_ref[...]
    sin = sin_ref[...]
    lane = lax.broadcasted_iota(I32, cos.shape, 1)
    first_half = (lane % (HEAD_DIM // 2)) < (HEAD_DIM // 4)

    def head(zc, gain):
        ms = jnp.mean(zc * zc, axis=-1, keepdims=True)
        y = zc * lax.rsqrt(ms + NORM_EPS) * gain
        partner = jnp.where(first_half, pltpu.roll(y, HEAD_DIM - HEAD_DIM // 4, 1),
                            pltpu.roll(y, HEAD_DIM // 4, 1))
        return y * cos + partner * sin

    nq = q_ref.shape[-1] // HEAD_DIM
    nk = k_ref.shape[-1] // HEAD_DIM
    for j in range(nq):
        q_ref[:, j * HEAD_DIM:(j + 1) * HEAD_DIM] = (
            head(z[:, j * HEAD_DIM:(j + 1) * HEAD_DIM], qg_ref[...]) * (HEAD_DIM ** -0.5 * LOG2E)).astype(BF16)
    for j in range(nk):
        c0 = (nq + j) * HEAD_DIM
        k_ref[:, j * HEAD_DIM:(j + 1) * HEAD_DIM] = head(z[:, c0:c0 + HEAD_DIM], kg_ref[...]).astype(BF16)
    v_ref[...] = z[:, (nq + nk) * HEAD_DIM:].astype(BF16)


def _qkv(x, g, mod, w, qg, kg, cos, sin, s, b):
    t, d = x.shape
    nkv = N_KV_HEADS * HEAD_DIM
    n_pos = s // TM
    return pl.pallas_call(
        _qkv_kernel,
        grid=(t // TM,),
        in_specs=[pl.BlockSpec((TM, d), lambda i: (i, 0)),
                  pl.BlockSpec((1, d), lambda i: (0, 0)),
                  _mod_spec(d, s // TM, b),
                  pl.BlockSpec(w.shape, lambda i: (0, 0)),
                  pl.BlockSpec((1, HEAD_DIM), lambda i: (0, 0)),
                  pl.BlockSpec((1, HEAD_DIM), lambda i: (0, 0)),
                  pl.BlockSpec((TM, HEAD_DIM), lambda i: (jnp.where(i < b * n_pos, i % n_pos, n_pos), 0)),
                  pl.BlockSpec((TM, HEAD_DIM), lambda i: (jnp.where(i < b * n_pos, i % n_pos, n_pos), 0))],
        out_specs=[pl.BlockSpec((TM, d), lambda i: (i, 0)),
                   pl.BlockSpec((TM, nkv), lambda i: (i, 0)),
                   pl.BlockSpec((TM, nkv), lambda i: (i, 0))],
        out_shape=[jax.ShapeDtypeStruct((t, d), BF16), jax.ShapeDtypeStruct((t, nkv), BF16),
                   jax.ShapeDtypeStruct((t, nkv), BF16)],
        compiler_params=_cp("parallel"),
        name="qkv_proj",
    )(x, g, mod, w, qg, kg, cos, sin)


def _attn_kernel(q_ref, kc_ref, vc_ref, *rest, n_lat):
    if n_lat:
        kl_ref, vl_ref, o_ref, s_scr, vaug = rest
    else:
        o_ref, s_scr, vaug = rest
    n_ctx = kc_ref.shape[0]
    tq = q_ref.shape[0]

    def fill_values():
        vaug[:, HEAD_DIM:] = jnp.ones((n_ctx + n_lat, HEAD_DIM), BF16)
        vaug[0:n_ctx, 0:HEAD_DIM] = vc_ref[...]
        if n_lat:
            vaug[n_ctx:, 0:HEAD_DIM] = vl_ref[...]

    if n_lat:
        pl.when(pl.program_id(2) == 0)(fill_values)
    else:
        fill_values()

    chunks = [(0, n_ctx)] + [(n_ctx + j, ATT_KC) for j in range(0, n_lat, ATT_KC)]
    nt = (((1,), (1,)), ((), ()))
    for g in range(GQA_GROUP):
        cols = slice(g * HEAD_DIM, (g + 1) * HEAD_DIM)
        q = q_ref[:, cols]
        m_part = jnp.full((tq, LANES), -jnp.inf, F32)
        for off, size in chunks:
            keys = kc_ref[...] if off == 0 else kl_ref[off - n_ctx:off - n_ctx + size, :]
            sc = lax.dot_general(q, keys, nt, preferred_element_type=F32)
            s_scr[:, off:off + size] = sc
            for j in range(0, size, LANES):
                m_part = jnp.maximum(m_part, sc[:, j:j + LANES])
        m_row = jnp.max(m_part, axis=-1, keepdims=True)
        acc = jnp.zeros((tq, 2 * HEAD_DIM), F32)
        for off, size in chunks:
            p = jnp.exp2((s_scr[:, off:off + size] - m_row).astype(BF16))
            acc = acc + jnp.dot(p, vaug[off:off + size, :], preferred_element_type=F32)
        o_ref[:, cols] = (acc[:, :HEAD_DIM] / acc[:, HEAD_DIM:]).astype(BF16)


def _attention(q, k, v, s, c, b):
    t, d = q.shape
    gw = GQA_GROUP * HEAD_DIM
    tq = TM
    nq = s // tq
    assert s % ATT_KC == 0
    ctx_blk = lambda bi, h, *_: ((b * s) // c + bi, h)
    o_lat = pl.pallas_call(
        functools.partial(_attn_kernel, n_lat=s),
        grid=(b, N_KV_HEADS, nq),
        in_specs=[pl.BlockSpec((tq, gw), lambda bi, h, i: (bi * nq + i, h)),
                  pl.BlockSpec((c, HEAD_DIM), ctx_blk),
                  pl.BlockSpec((c, HEAD_DIM), ctx_blk),
                  pl.BlockSpec((s, HEAD_DIM), lambda bi, h, i: (bi, h)),
                  pl.BlockSpec((s, HEAD_DIM), lambda bi, h, i: (bi, h))],
        out_specs=pl.BlockSpec((tq, gw), lambda bi, h, i: (bi * nq + i, h)),
        out_shape=jax.ShapeDtypeStruct((b * s, d), BF16),
        scratch_shapes=[pltpu.VMEM((tq, c + s), F32), pltpu.VMEM((c + s, 2 * HEAD_DIM), BF16)],
        compiler_params=_cp("parallel", "parallel", "arbitrary"),
        name="attn_lat",
    )(q, k, v, k, v)
    o_ctx = pl.pallas_call(
        functools.partial(_attn_kernel, n_lat=0),
        grid=(b, N_KV_HEADS),
        in_specs=[pl.BlockSpec((c, gw), ctx_blk),
                  pl.BlockSpec((c, HEAD_DIM), ctx_blk),
                  pl.BlockSpec((c, HEAD_DIM), ctx_blk)],
        out_specs=pl.BlockSpec((c, gw), lambda bi, h: (bi, h)),
        out_shape=jax.ShapeDtypeStruct((b * c, d), BF16),
        scratch_shapes=[pltpu.VMEM((c, c), F32), pltpu.VMEM((c, 2 * HEAD_DIM), BF16)],
        compiler_params=_cp("parallel", "parallel"),
        name="attn_ctx",
    )(q, k, v)
    return jnp.concatenate([o_lat, o_ctx], axis=0)


def _attention_mixer(tok, g, mod, w_qkv, qg, kg, w_o, s, c, b):
    cos, sin = _rope_tables(s)
    q, k, v = _qkv(tok, g, mod, w_qkv.astype(BF16), qg.reshape(1, -1), kg.reshape(1, -1), cos, sin, s, b)
    o = _attention(q, k, v, s, c, b)
    return _out_proj(o, tok, mod, w_o.astype(BF16), tok.shape[0], s, b, 2)


def _gmlp_kernel(x_ref, g_ref, mod_ref, w_in_ref, lng_ref, lnb_ref, ws_ref, bs_ref, w_out_ref, o_ref, uv_ref):
    d = x_ref.shape[-1]
    dcm = lng_ref.shape[-1]
    gw = dcm // CM_GROUPS
    x = x_ref[...]
    m = mod_ref[0]
    h = _norm_mod(x, g_ref[...], m[:, 0:d], m[:, d:2 * d])
    z = jax.nn.gelu(jnp.dot(h.astype(BF16), w_in_ref[...], preferred_element_type=F32))
    u = z[:, :dcm]
    v = z[:, dcm:]
    mu = jnp.mean(v, axis=-1, keepdims=True)
    vc = v - mu
    var = jnp.mean(vc * vc, axis=-1, keepdims=True)
    vn = (vc * lax.rsqrt(var + NORM_EPS) * lng_ref[...] + lnb_ref[...]).astype(BF16)
    for ck in range(x.shape[0] // CHUNK):
        rows = slice(ck * CHUNK, (ck + 1) * CHUNK)
        for gi in range(CM_GROUPS):
            cols = slice(gi * gw, (gi + 1) * gw)
            mix = jnp.dot(ws_ref[gi], vn[rows, cols], preferred_element_type=F32) + bs_ref[:, gi:gi + 1]
            uv_ref[rows, cols] = (u[rows, cols] * mix).astype(BF16)
    y = jnp.dot(uv_ref[...], w_out_ref[...], preferred_element_type=F32)
    o_ref[...] = x + m[:, 2 * d:3 * d] * y


def _gmlp_mixer(tok, g, mod, w_in, ln_g, ln_b, w_s, b_s, w_out, s, b):
    t, d = tok.shape
    dcm = ln_g.shape[-1]
    full = lambda shape: pl.BlockSpec(shape, lambda i: (0,) * len(shape))
    return pl.pallas_call(
        _gmlp_kernel,
        grid=(t // TM,),
        in_specs=[pl.BlockSpec((TM, d), lambda i: (i, 0)),
                  full((1, d)),
                  _mod_spec(d, s // TM, b),
                  full((d, 2 * dcm)), full((1, dcm)), full((1, dcm)),
                  full((CM_GROUPS, CHUNK, CHUNK)), full((CHUNK, CM_GROUPS)), full((dcm, d))],
        out_specs=pl.BlockSpec((TM, d), lambda i: (i, 0)),
        out_shape=jax.ShapeDtypeStruct((t, d), F32),
        scratch_shapes=[pltpu.VMEM((TM, dcm), BF16)],
        compiler_params=_cp("parallel"),
        name="gmlp",
    )(tok, g, mod, w_in.astype(BF16), ln_g.reshape(1, dcm), ln_b.reshape(1, dcm),
      w_s.astype(BF16), b_s.T, w_out.astype(BF16))


def _router_kernel(x_ref, g_ref, mod_ref, wrh_ref, wrl_ref, br_ref, hf_ref, rt_ref, ew_ref, cnt_ref, cnt_s):
    d = x_ref.shape[-1]
    tm = x_ref.shape[0]

    @pl.when(pl.program_id(0) == 0)
    def _():
        cnt_s[...] = jnp.zeros_like(cnt_s)

    m = mod_ref[0]
    hf = _norm_mod(x_ref[...], g_ref[...], m[:, 3 * d:4 * d], m[:, 4 * d:5 * d])
    hf_ref[...] = _pack_bf16_pairs(hf)
    hf_hi = hf.astype(BF16)
    hf_lo = (hf - hf_hi.astype(F32)).astype(BF16)
    logits = (jnp.dot(hf_hi, wrh_ref[...], preferred_element_type=F32)
              + jnp.dot(hf_lo, wrh_ref[...], preferred_element_type=F32)
              + jnp.dot(hf_hi, wrl_ref[...], preferred_element_type=F32)) + br_ref[...]
    lane = lax.broadcasted_iota(I32, logits.shape, 1)
    neg = -jnp.inf
    gl = jnp.where(lane < N_GROUPS, logits, neg)
    gmax = jnp.max(gl, axis=-1, keepdims=True)
    gsel = jnp.min(jnp.where(gl == gmax, lane, LANES), axis=-1, keepdims=True)
    gate_g = 1.0 / jnp.sum(jnp.exp(gl - gmax), axis=-1, keepdims=True)
    lo = N_GROUPS + gsel * EXPERTS_PER_GROUP
    el = jnp.where(jnp.logical_and(lane >= lo, lane < lo + EXPERTS_PER_GROUP), logits, neg)
    v1 = jnp.max(el, axis=-1, keepdims=True)
    i1 = jnp.min(jnp.where(el == v1, lane, LANES), axis=-1, keepdims=True)
    el2 = jnp.where(lane == i1, neg, el)
    v2 = jnp.max(el2, axis=-1, keepdims=True)
    i2 = jnp.min(jnp.where(el2 == v2, lane, LANES), axis=-1, keepdims=True)
    e21 = jnp.exp(v2 - v1)
    w1 = gate_g / (1.0 + e21)
    w2 = w1 * e21
    ew_ref[...] = jnp.where(lane == 0, w1, jnp.where(lane == 1, w2, 0.0))

    oh1 = lane == i1
    oh2 = lane == i2
    above = (lax.broadcasted_iota(I32, (tm, tm), 1) < lax.broadcasted_iota(I32, (tm, tm), 0)).astype(BF16)
    pre1 = jnp.dot(above, oh1.astype(BF16), preferred_element_type=F32)
    pre2 = jnp.dot(above, oh2.astype(BF16), preferred_element_type=F32)
    tot1 = jnp.sum(oh1.astype(F32), axis=0, keepdims=True)
    tot2 = jnp.sum(oh2.astype(F32), axis=0, keepdims=True)
    cnt = cnt_s[...]
    rank1 = jnp.sum(jnp.where(oh1, cnt + pre1, 0.0), axis=-1, keepdims=True).astype(I32)
    rank2 = jnp.sum(jnp.where(oh2, cnt + tot1 + pre2, 0.0), axis=-1, keepdims=True).astype(I32)
    cnt = cnt + tot1 + tot2
    cnt_s[...] = cnt
    cnt_ref[...] = jnp.broadcast_to(cnt, cnt_ref.shape)
    rt_ref[...] = jnp.where(lane == 0, i1 - N_GROUPS, jnp.where(lane == 1, i2 - N_GROUPS,
                            jnp.where(lane == 2, rank1, jnp.where(lane == 3, rank2, 0))))


def _router(x, g, mod, wr, br, nrows, s, b):
    d = x.shape[-1]
    row = lambda w: pl.BlockSpec((TM, w), lambda i: (i, 0))
    wr_hi = wr.astype(BF16)
    wr_lo = (wr - wr_hi.astype(F32)).astype(BF16)
    return pl.pallas_call(
        _router_kernel,
        grid=(nrows // TM,),
        in_specs=[row(d), pl.BlockSpec((1, d), lambda i: (0, 0)), _mod_spec(d, s // TM, b),
                  pl.BlockSpec((d, LANES), lambda i: (0, 0)), pl.BlockSpec((d, LANES), lambda i: (0, 0)),
                  pl.BlockSpec((1, LANES), lambda i: (0, 0))],
        out_specs=[row(d // 2), row(LANES), row(LANES), pl.BlockSpec((SUBLANES, LANES), lambda i: (0, 0))],
        out_shape=[jax.ShapeDtypeStruct((nrows, d // 2), I32), jax.ShapeDtypeStruct((nrows, LANES), I32),
                   jax.ShapeDtypeStruct((nrows, LANES), F32), jax.ShapeDtypeStruct((SUBLANES, LANES), F32)],
        scratch_shapes=[pltpu.VMEM((1, LANES), F32)],
        compiler_params=_cp("arbitrary"),
        name="moe_router",
    )(x, g, mod, wr_hi, wr_lo, br)


def _dispatch_plan(rt, cnt, bm):
    n_tok = rt.shape[0]
    counts = cnt[0, N_GROUPS:N_GROUPS + N_EXPERTS].astype(I32)
    padded = (counts + bm - 1) // bm * bm
    pad_end = jnp.cumsum(padded)
    pad_start = pad_end - padded
    experts = jnp.arange(N_EXPERTS, dtype=I32)
    start_of = jnp.sum(jnp.where(rt[:, 0:2, None] == experts, pad_start, 0), axis=-1)
    pos = (start_of + rt[:, 2:4]).astype(I32)
    n_rows = 2 * n_tok + N_EXPERTS * bm
    n_used = (pad_end[-1] // bm).astype(I32)
    first_row = jnp.minimum(jnp.arange(n_rows // bm, dtype=I32), n_used - 1) * bm
    blk_e = jnp.sum((pad_end[None, :] <= first_row[:, None]).astype(I32), axis=1)
    blk_e = jnp.minimum(blk_e, N_EXPERTS - 1).astype(I32)
    return pos[:, 0], pos[:, 1], blk_e, n_used.reshape(1), n_rows


def _sc_mesh():
    return plsc.VectorSubcoreMesh(core_axis_name="c", subcore_axis_name="s")


def _sc_worker_base(per_worker):
    return (lax.axis_index("s") * SC_CORES + lax.axis_index("c")) * per_worker


def _sc_dispatch(hf, pos0, pos1, n_rows):
    t, d = hf.shape
    per_w = t // SC_WORKERS
    assert per_w * SC_WORKERS == t and per_w % SC_CHUNK == 0

    @functools.partial(
        pl.kernel, mesh=_sc_mesh(), out_type=jax.ShapeDtypeStruct((n_rows, d), hf.dtype),
        scratch_types=[pltpu.VMEM((SC_CHUNK,), I32), pltpu.VMEM((SC_CHUNK,), I32),
                       pltpu.VMEM((SC_CHUNK, d), hf.dtype)])
    def dispatch(hf_hbm, p0_hbm, p1_hbm, out_hbm, i0_v, i1_v, rows_v):
        base = _sc_worker_base(per_w)

        @pl.loop(0, per_w // SC_CHUNK)
        def _(ck):
            off = pl.multiple_of(base + ck * SC_CHUNK, SUBLANES)
            pltpu.sync_copy(hf_hbm.at[pl.ds(off, SC_CHUNK)], rows_v)
            pltpu.sync_copy(p0_hbm.at[pl.ds(off, SC_CHUNK)], i0_v)
            pltpu.sync_copy(p1_hbm.at[pl.ds(off, SC_CHUNK)], i1_v)
            pltpu.sync_copy(rows_v, out_hbm.at[i0_v])
            pltpu.sync_copy(rows_v, out_hbm.at[i1_v])

    return dispatch(hf, pos0, pos1)


def _sc_gather(rows, idx):
    n = idx.shape[0]
    d = rows.shape[1]
    per_w = n // SC_WORKERS
    assert per_w * SC_WORKERS == n and per_w % SC_CHUNK == 0

    @functools.partial(
        pl.kernel, mesh=_sc_mesh(), out_type=jax.ShapeDtypeStruct((n, d), rows.dtype),
        scratch_types=[pltpu.VMEM((SC_CHUNK,), I32), pltpu.VMEM((SC_CHUNK, d), rows.dtype)])
    def gather(rows_hbm, i_hbm, out_hbm, i_v, rows_v):
        base = _sc_worker_base(per_w)

        @pl.loop(0, per_w // SC_CHUNK)
        def _(ck):
            off = pl.multiple_of(base + ck * SC_CHUNK, SUBLANES)
            pltpu.sync_copy(i_hbm.at[pl.ds(off, SC_CHUNK)], i_v)
            pltpu.sync_copy(rows_hbm.at[i_v], rows_v)
            pltpu.sync_copy(rows_v, out_hbm.at[pl.ds(off, SC_CHUNK)])

    return gather(rows, idx)


def _expert_kernel(blk_e_ref, n_used_ref, x_ref, wg_ref, wu_ref, wd_ref, y_ref, wgb, wub, wdb):
    i = pl.program_id(0)
    n_used = n_used_ref[0]

    @pl.when(i < n_used)
    def _():
        changed = jnp.logical_or(i == 0, blk_e_ref[i] != blk_e_ref[jnp.maximum(i - 1, 0)])

        @pl.when(changed)
        def _():
            wgb[...] = wg_ref[0].astype(BF16)
            wub[...] = wu_ref[0].astype(BF16)
            wdb[...] = wd_ref[0].astype(BF16)

        x_hi, x_lo = _unpack_bf16_pairs(x_ref[...])
        xb = jnp.concatenate([x_hi.astype(BF16), x_lo.astype(BF16)], axis=1)
        gt = jnp.dot(xb, wgb[...], preferred_element_type=F32)
        up = jnp.dot(xb, wub[...], preferred_element_type=F32)
        act = (gt * _sigmoid(gt) * up).astype(BF16)
        y_ref[...] = _pack_bf16_pairs(jnp.dot(act, wdb[...], preferred_element_type=F32))

    @pl.when(i >= n_used)
    def _():
        y_ref[...] = jnp.zeros_like(y_ref)


def _experts(x_rows, blk_e, n_used, w_gate, w_up, w_down, layer):
    n_rows, dp = x_rows.shape
    depth, n_e, d, de = w_gate.shape
    bm = MOE_BM
    w_idx = lambda i, be, nu: (layer * n_e + be[i], 0, 0)
    grid_spec = pltpu.PrefetchScalarGridSpec(
        num_scalar_prefetch=2,
        grid=(n_rows // bm,),
        in_specs=[pl.BlockSpec((bm, dp), lambda i, be, nu: (jnp.minimum(i, nu[0] - 1), 0)),
                  pl.BlockSpec((1, d, de), w_idx),
                  pl.BlockSpec((1, d, de), w_idx),
                  pl.BlockSpec((1, de, d), w_idx)],
        out_specs=pl.BlockSpec((bm, dp), lambda i, be, nu: (i, 0)),
        scratch_shapes=[pltpu.VMEM((d, de), BF16), pltpu.VMEM((d, de), BF16), pltpu.VMEM((de, d), BF16)],
    )
    return pl.pallas_call(
        _expert_kernel,
        grid_spec=grid_spec,
        out_shape=jax.ShapeDtypeStruct((n_rows, dp), I32),
        compiler_params=_cp("arbitrary"),
        name="moe_experts",
    )(blk_e, n_used, x_rows, w_gate.reshape(depth * n_e, d, de), w_up.reshape(depth * n_e, d, de),
      w_down.reshape(depth * n_e, de, d))


def _combine_kernel(x_ref, y0_ref, y1_ref, ew_ref, mod_ref, o_ref):
    d = x_ref.shape[1]
    h = d // 2
    ew = ew_ref[...]
    gate = mod_ref[0][:, 5 * d:6 * d]
    y0_hi, y0_lo = _unpack_bf16_pairs(y0_ref[...])
    y1_hi, y1_lo = _unpack_bf16_pairs(y1_ref[...])
    o_ref[:, :h] = x_ref[:, :h] + gate[:, :h] * (ew[:, 0:1] * y0_hi + ew[:, 1:2] * y1_hi)
    o_ref[:, h:] = x_ref[:, h:] + gate[:, h:] * (ew[:, 0:1] * y0_lo + ew[:, 1:2] * y1_lo)


def _combine(y01, x, ew, mod, nrows, s, b):
    d = x.shape[-1]
    nb = nrows // TM
    return pl.pallas_call(
        _combine_kernel,
        grid=(nb,),
        in_specs=[pl.BlockSpec((TM, d), lambda i: (i, 0)),
                  pl.BlockSpec((TM, d // 2), lambda i: (i, 0)),
                  pl.BlockSpec((TM, d // 2), lambda i: (i + nb, 0)),
                  pl.BlockSpec((TM, LANES), lambda i: (i, 0)),
                  _mod_spec(d, s // TM, b)],
        out_specs=pl.BlockSpec((TM, d), lambda i: (i, 0)),
        out_shape=jax.ShapeDtypeStruct((nrows, d), F32),
        compiler_params=_cp("parallel"),
        name="moe_combine",
    )(x, y01, y01, ew, mod)


def _hier_moe(x, g, mod, w_group, b_group, w_router, b_router, w_gate, w_up, w_down, layer, nrows, s, b):
    d = x.shape[-1]
    pad = LANES - N_GROUPS - N_EXPERTS
    wr = jnp.concatenate([w_group, w_router.reshape(d, N_EXPERTS), jnp.zeros((d, pad), F32)], axis=1)
    br = jnp.concatenate([b_group, b_router.reshape(N_EXPERTS), jnp.zeros((pad,), F32)]).reshape(1, LANES)
    hf, rt, ew, cnt = _router(x, g, mod, wr, br, nrows, s, b)
    pos0, pos1, blk_e, n_used, n_rows = _dispatch_plan(rt, cnt, MOE_BM)
    x_rows = _sc_dispatch(hf, pos0, pos1, n_rows)
    y_rows = _experts(x_rows, blk_e, n_used, w_gate, w_up, w_down, layer)
    y01 = _sc_gather(y_rows, jnp.concatenate([pos0, pos1]))
    return _combine(y01, x, ew, mod, nrows, s, b)


def kernel(x, c, ctx, c_ctx, ada_w, ada_b, norm_mix_g, norm_ffn_g, rg_w_in, rg_conv_w, rg_conv_b, rg_wa, rg_ba, rg_wi, rg_bi, rg_lambda, rg_w_out, at_w_qkv, at_q_g, at_k_g, at_w_o, cm_w_in, cm_ln_g, cm_ln_b, cm_w_s, cm_b_s, cm_w_out, moe_w_group, moe_b_group, moe_w_router, moe_b_router, moe_w_gate, moe_w_up, moe_w_down):
    b, s, d = x.shape
    cl = ctx.shape[1]
    depth = ada_w.shape[0]
    n_lat = b * s
    assert b < SUBLANES and s % TM == 0 and cl % TM == 0 and d == RG_BLOCKS * LANES

    cin = jnp.concatenate([c, c_ctx[None, :], jnp.zeros((SUBLANES - b - 1, d), F32)], axis=0)
    mod_all = _ada_table(cin, ada_w, ada_b).reshape(depth, SUBLANES, 1, N_MOD * d)
    tok = jnp.concatenate([x.reshape(n_lat, d), ctx.reshape(b * cl, d)], axis=0)

    for l in range(depth):
        kind = l % 3
        j = l // 3
        last = l == depth - 1
        mod = mod_all[l]
        g_mix = norm_mix_g[l].reshape(1, d)
        nrows = n_lat if last else tok.shape[0]
        if kind == 0:
            tok_mix = _rglru_mixer(tok, g_mix, mod, rg_w_in[j], rg_conv_w[j], rg_conv_b[j], rg_wa[j], rg_ba[j],
                                   rg_wi[j], rg_bi[j], rg_lambda[j], rg_w_out[j], s, cl, b, nrows)
        elif kind == 1:
            tok_mix = _attention_mixer(tok, g_mix, mod, at_w_qkv[j], at_q_g[j], at_k_g[j], at_w_o[j], s, cl, b)
        else:
            tok_mix = _gmlp_mixer(tok, g_mix, mod, cm_w_in[j], cm_ln_g[j], cm_ln_b[j], cm_w_s[j], cm_b_s[j],
                                  cm_w_out[j], s, b)
        tok = _hier_moe(tok_mix, norm_ffn_g[l].reshape(1, d), mod, moe_w_group[l], moe_b_group[l],
                        moe_w_router[l], moe_b_router[l], moe_w_gate, moe_w_up, moe_w_down, l,
                        nrows, s, b)
    return tok[:n_lat].reshape(b, s, d)
```

```python
import functools

import jax
import jax.numpy as jnp
from jax import lax
from jax.experimental import pallas as pl
from jax.experimental.pallas import tpu as pltpu
from jax.experimental.pallas import tpu_sc as plsc

F32 = jnp.float32
BF16 = jnp.bfloat16
I32 = jnp.int32
U32 = jnp.uint32
HIGHEST = lax.Precision.HIGHEST

NORM_EPS = 1e-6
N_MOD = 6
GRID_W = 64
RG_BLOCKS = 8
CONV_W = 4
RG_C = 8.0
HEAD_DIM = 128
N_KV_HEADS = 2
GQA_GROUP = 4
ROPE_THETA = 10000.0
CHUNK = 128
CM_GROUPS = 8
N_GROUPS = 4
EXPERTS_PER_GROUP = 8
N_EXPERTS = N_GROUPS * EXPERTS_PER_GROUP

LANES = 128
SUBLANES = 8
TM = 256
TR = 512
TL = 256
HALO = 8
ATT_KC = 512
LOG2E = 1.4426950408889634
MOE_BM = 256
SC_CORES = 2
SC_WORKERS = 32
SC_CHUNK = 32
VMEM_LIMIT = 52 * 2**20


def _cp(*sem):
    return pltpu.CompilerParams(dimension_semantics=sem, vmem_limit_bytes=VMEM_LIMIT)


def _norm_mod(x, g, shift, scale):
    ms = jnp.mean(x * x, axis=-1, keepdims=True)
    y = x * lax.rsqrt(ms + NORM_EPS) * g
    return y * (1.0 + scale) + shift


def _sigmoid(x):
    return 0.5 * jnp.tanh(0.5 * x) + 0.5


def _pack_bf16_pairs(x):
    h = x.shape[-1] // 2
    hi = lax.bitcast_convert_type(x[:, :h].astype(BF16).astype(F32), U32)
    lo = lax.bitcast_convert_type(x[:, h:].astype(BF16).astype(F32), U32)
    return lax.bitcast_convert_type(hi | (lo >> 16), I32)


def _unpack_bf16_pairs(w):
    u = lax.bitcast_convert_type(w, U32)
    hi = lax.bitcast_convert_type(u & jnp.uint32(0xFFFF0000), F32)
    lo = lax.bitcast_convert_type(u << 16, F32)
    return hi, lo


def _mod_spec(d, rows_per_sample, n_samples):
    return pl.BlockSpec((1, 1, N_MOD * d),
                        lambda i, *_: (jnp.minimum(i // rows_per_sample, n_samples), 0, 0))


def _ada_kernel(c_ref, w_ref, b_ref, o_ref):
    cin = c_ref[...]
    act = cin * jax.nn.sigmoid(cin)
    o_ref[0] = jnp.dot(act, w_ref[0], preferred_element_type=F32, precision=HIGHEST) + b_ref[0]


def _ada_table(cin, ada_w, ada_b):
    depth, d, n = ada_w.shape
    tn = d
    return pl.pallas_call(
        _ada_kernel,
        grid=(depth, n // tn),
        in_specs=[pl.BlockSpec((SUBLANES, d), lambda l, j: (0, 0)),
                  pl.BlockSpec((1, d, tn), lambda l, j: (l, 0, j)),
                  pl.BlockSpec((1, 1, tn), lambda l, j: (l, 0, j))],
        out_specs=pl.BlockSpec((1, SUBLANES, tn), lambda l, j: (l, 0, j)),
        out_shape=jax.ShapeDtypeStruct((depth, SUBLANES, n), F32),
        compiler_params=_cp("parallel", "parallel"),
        name="ada_table",
    )(cin, ada_w, ada_b.reshape(depth, 1, n))


def _out_kernel(y_ref, x_ref, mod_ref, w_ref, o_ref, *, gate_idx):
    d = x_ref.shape[-1]
    gate = mod_ref[0][:, gate_idx * d:(gate_idx + 1) * d]
    y = jnp.dot(y_ref[...].astype(BF16), w_ref[...], preferred_element_type=F32)
    o_ref[...] = x_ref[...] + gate * y


def _out_proj(y, x, mod, w, nrows, s, b, gate_idx):
    d = x.shape[-1]
    k = y.shape[-1]
    return pl.pallas_call(
        functools.partial(_out_kernel, gate_idx=gate_idx),
        grid=(nrows // TR,),
        in_specs=[pl.BlockSpec((TR, k), lambda i: (i, 0)),
                  pl.BlockSpec((TR, d), lambda i: (i, 0)),
                  _mod_spec(d, s // TR, b),
                  pl.BlockSpec((k, d), lambda i: (0, 0))],
        out_specs=pl.BlockSpec((TR, d), lambda i: (i, 0)),
        out_shape=jax.ShapeDtypeStruct((nrows, d), F32),
        compiler_params=_cp("parallel"),
        name="out_proj",
    )(y, x, mod, w)


def _rg_in_kernel(x_ref, g_ref, mod_ref, w_ref, gg_ref, xin_ref):
    d = x_ref.shape[-1]
    m = mod_ref[0]
    h = _norm_mod(x_ref[...], g_ref[...], m[:, 0:d], m[:, d:2 * d])
    z = jnp.dot(h.astype(BF16), w_ref[...], preferred_element_type=F32)
    tm = x_ref.shape[0]
    for n in range(d // LANES):
        cols = slice(n * LANES, (n + 1) * LANES)
        gg_ref[pl.ds(n, tm, stride=SUBLANES), :] = jax.nn.gelu(z[:, cols])
        xin_ref[pl.ds(n, tm, stride=SUBLANES), :] = z[:, d + n * LANES:d + (n + 1) * LANES]


def _rg_in(x, g, mod, w, s, b):
    t, d = x.shape
    assert d == SUBLANES * LANES
    return pl.pallas_call(
        _rg_in_kernel,
        grid=(t // TM,),
        in_specs=[pl.BlockSpec((TM, d), lambda i: (i, 0)),
                  pl.BlockSpec((1, d), lambda i: (0, 0)),
                  _mod_spec(d, s // TM, b),
                  pl.BlockSpec((d, 2 * d), lambda i: (0, 0))],
        out_specs=[pl.BlockSpec((TM * SUBLANES, LANES), lambda i: (i, 0)),
                   pl.BlockSpec((TM * SUBLANES, LANES), lambda i: (i, 0))],
        out_shape=[jax.ShapeDtypeStruct((t * SUBLANES, LANES), F32),
                   jax.ShapeDtypeStruct((t * SUBLANES, LANES), F32)],
        compiler_params=_cp("parallel"),
        name="rg_in",
    )(x, g, mod, w)


def _rg_scan_kernel(xm_ref, xprev_ref, xnext_ref, cw_ref, cb_ref, wa_ref, wi_ref,
                    ba_ref, bi_ref, lam_ref, *rest, reverse, nlat):
    if reverse:
        hf_ref, gg_ref, out_ref, xpad, xc, a_s, b_s, h_s, hcar = rest
    else:
        out_ref, xpad, xc, a_s, b_s, hcar = rest
    rows = TL * SUBLANES
    hrows = HALO * SUBLANES
    j = pl.program_id(1)
    m = (nlat - j) if reverse else (j - 1)
    has_prev = jnp.logical_and(j >= 1, m > 0)
    has_next = jnp.logical_and(j >= 1, m < nlat - 1)

    @pl.when(j == 0)
    def _():
        hcar[...] = jnp.zeros_like(hcar)

    xpad[0:hrows, :] = jnp.where(has_prev, xprev_ref[...], 0.0)
    xpad[hrows:hrows + rows, :] = xm_ref[...]
    xpad[hrows + rows:2 * hrows + rows, :] = jnp.where(has_next, xnext_ref[...], 0.0)
    acc = jnp.broadcast_to(cb_ref[...][None], (TL, SUBLANES, LANES))
    for k in range(CONV_W):
        off = (HALO + k - CONV_W // 2) * SUBLANES
        tap = xpad[off:off + rows, :].reshape(TL, SUBLANES, LANES)
        acc = acc + tap * cw_ref[k][None]
    xc[...] = acc.reshape(rows, LANES)

    for n in range(RG_BLOCKS):
        cols = slice(n * LANES, (n + 1) * LANES)
        xn = xc[pl.ds(n, TL, stride=SUBLANES), :]
        xb = xn.astype(BF16)
        r = _sigmoid(jnp.dot(xb, wa_ref[n], preferred_element_type=F32) + ba_ref[:, cols])
        gi = _sigmoid(jnp.dot(xb, wi_ref[n], preferred_element_type=F32) + bi_ref[:, cols])
        a = jnp.exp(-RG_C * r * jax.nn.softplus(-lam_ref[:, cols]))
        a_s[pl.ds(n, TL, stride=SUBLANES), :] = a
        b_s[pl.ds(n, TL, stride=SUBLANES), :] = jnp.sqrt(1.0 - a * a) * gi * xn

    h_dst = h_s if reverse else out_ref

    def step(s, h):
        t = (TL - 1 - s) if reverse else s
        r0 = pl.multiple_of(t * SUBLANES, SUBLANES)
        h = a_s[pl.ds(r0, SUBLANES), :] * h + b_s[pl.ds(r0, SUBLANES), :]
        h_dst[pl.ds(r0, SUBLANES), :] = h
        return h

    hcar[...] = lax.fori_loop(0, TL, step, hcar[...], unroll=8)
    if reverse:
        h_s[...] = gg_ref[...] * (hf_ref[...] + h_s[...])
        for n in range(RG_BLOCKS):
            out_ref[:, n * LANES:(n + 1) * LANES] = h_s[pl.ds(n, TL, stride=SUBLANES), :].astype(BF16)


def _rg_scan(xin8, conv_w, conv_b, wa, wi, ba, bi, lam, s, c, b, reverse, hf8=None, gg8=None):
    assert c == TL and s % TL == 0
    rows = TL * SUBLANES
    hrows = HALO * SUBLANES
    nlat = s // TL
    t = xin8.shape[0] // SUBLANES
    n_halo = t // HALO

    def chunk(bi_, j):
        lat = bi_ * nlat + ((nlat - j) if reverse else (j - 1))
        return jnp.where(j == 0, (b * s) // TL + bi_, lat)

    main = pl.BlockSpec((rows, LANES), lambda bi_, j: (chunk(bi_, j), 0))
    prev = pl.BlockSpec((hrows, LANES),
                        lambda bi_, j: (jnp.maximum(chunk(bi_, j) * (TL // HALO) - 1, 0), 0))
    nxt = pl.BlockSpec((hrows, LANES),
                       lambda bi_, j: (jnp.minimum((chunk(bi_, j) + 1) * (TL // HALO), n_halo - 1), 0))
    full = lambda shape: pl.BlockSpec(shape, lambda bi_, j: (0,) * len(shape))
    d = RG_BLOCKS * LANES
    in_specs = [main, prev, nxt, full((CONV_W, SUBLANES, LANES)), full((SUBLANES, LANES)),
                full((RG_BLOCKS, LANES, LANES)), full((RG_BLOCKS, LANES, LANES)),
                full((1, d)), full((1, d)), full((1, d))]
    args = [xin8, xin8, xin8, conv_w.reshape(CONV_W, SUBLANES, LANES), conv_b.reshape(SUBLANES, LANES),
            wa.astype(BF16), wi.astype(BF16), ba.reshape(1, d), bi.reshape(1, d), lam.reshape(1, d)]
    scratch = [pltpu.VMEM((rows + 2 * hrows, LANES), F32), pltpu.VMEM((rows, LANES), F32),
               pltpu.VMEM((rows, LANES), F32), pltpu.VMEM((rows, LANES), F32)]
    if reverse:
        in_specs += [main, main]
        args += [hf8, gg8]
        scratch.append(pltpu.VMEM((rows, LANES), F32))
    scratch.append(pltpu.VMEM((SUBLANES, LANES), F32))
    return pl.pallas_call(
        functools.partial(_rg_scan_kernel, reverse=reverse, nlat=nlat),
        grid=(b, nlat + 1),
        in_specs=in_specs,
        out_specs=pl.BlockSpec((TL, d), lambda bi_, j: (chunk(bi_, j), 0)) if reverse else main,
        out_shape=jax.ShapeDtypeStruct((t, d), BF16) if reverse else jax.ShapeDtypeStruct(xin8.shape, F32),
        scratch_shapes=scratch,
        compiler_params=_cp("parallel", "arbitrary"),
        name="rg_scan_bwd" if reverse else "rg_scan_fwd",
    )(*args)


def _rglru_mixer(tok, g, mod, w_in, conv_w, conv_b, wa, ba, wi, bi, lam, w_out, s, c, b, nrows_out):
    gg8, xin8 = _rg_in(tok, g, mod, w_in.astype(BF16), s, b)
    hf8 = _rg_scan(xin8, conv_w, conv_b, wa[0], wi[0], ba[0], bi[0], lam[0], s, c, b, False)
    y = _rg_scan(xin8, conv_w, conv_b, wa[1], wi[1], ba[1], bi[1], lam[1], s, c, b, True, hf8, gg8)
    return _out_proj(y, tok, mod, w_out.astype(BF16), nrows_out, s, b, 2)


def _rope_tables(s):
    pos = jnp.arange(s, dtype=F32)
    row = jnp.floor(pos / GRID_W)
    col = pos - row * GRID_W
    n_freq = HEAD_DIM // 4
    inv = ROPE_THETA ** (-jnp.arange(n_freq, dtype=F32) * 2.0 / (HEAD_DIM // 2))
    ar = row[:, None] * inv
    ac = col[:, None] * inv
    cos = jnp.concatenate([jnp.cos(ar), jnp.cos(ar), jnp.cos(ac), jnp.cos(ac)], axis=1)
    sin = jnp.concatenate([-jnp.sin(ar), jnp.sin(ar), -jnp.sin(ac), jnp.sin(ac)], axis=1)
    cos = jnp.concatenate([cos, jnp.ones((TM, HEAD_DIM), F32)], axis=0)
    sin = jnp.concatenate([sin, jnp.zeros((TM, HEAD_DIM), F32)], axis=0)
    return cos, sin


def _qkv_kernel(x_ref, g_ref, mod_ref, w_ref, qg_ref, kg_ref, cos_ref, sin_ref, q_ref, k_ref, v_ref):
    d = x_ref.shape[-1]
    m = mod_ref[0]
    h = _norm_mod(x_ref[...], g_ref[...], m[:, 0:d], m[:, d:2 * d])
    z = jnp.dot(h.astype(BF16), w_ref[...], preferred_element_type=F32)
    cos = cos_ref[...]
    sin = sin_ref[...]
    lane = lax.broadcasted_iota(I32, cos.shape, 1)
    first_half = (lane % (HEAD_DIM // 2)) < (HEAD_DIM // 4)

    def head(zc, gain):
        ms = jnp.mean(zc * zc, axis=-1, keepdims=True)
        y = zc * lax.rsqrt(ms + NORM_EPS) * gain
        partner = jnp.where(first_half, pltpu.roll(y, HEAD_DIM - HEAD_DIM // 4, 1),
                            pltpu.roll(y, HEAD_DIM // 4, 1))
        return y * cos + partner * sin

    nq = q_ref.shape[-1] // HEAD_DIM
    nk = k_ref.shape[-1] // HEAD_DIM
    for j in range(nq):
        q_ref[:, j * HEAD_DIM:(j + 1) * HEAD_DIM] = (
            head(z[:, j * HEAD_DIM:(j + 1) * HEAD_DIM], qg_ref[...]) * (HEAD_DIM ** -0.5 * LOG2E)).astype(BF16)
    for j in range(nk):
        c0 = (nq + j) * HEAD_DIM
        k_ref[:, j * HEAD_DIM:(j + 1) * HEAD_DIM] = head(z[:, c0:c0 + HEAD_DIM], kg_ref[...]).astype(BF16)
    v_ref[...] = z[:, (nq + nk) * HEAD_DIM:].astype(BF16)


def _qkv(x, g, mod, w, qg, kg, cos, sin, s, b):
    t, d = x.shape
    nkv = N_KV_HEADS * HEAD_DIM
    n_pos = s // TM
    return pl.pallas_call(
        _qkv_kernel,
        grid=(t // TM,),
        in_specs=[pl.BlockSpec((TM, d), lambda i: (i, 0)),
                  pl.BlockSpec((1, d), lambda i: (0, 0)),
                  _mod_spec(d, s // TM, b),
                  pl.BlockSpec(w.shape, lambda i: (0, 0)),
                  pl.BlockSpec((1, HEAD_DIM), lambda i: (0, 0)),
                  pl.BlockSpec((1, HEAD_DIM), lambda i: (0, 0)),
                  pl.BlockSpec((TM, HEAD_DIM), lambda i: (jnp.where(i < b * n_pos, i % n_pos, n_pos), 0)),
                  pl.BlockSpec((TM, HEAD_DIM), lambda i: (jnp.where(i < b * n_pos, i % n_pos, n_pos), 0))],
        out_specs=[pl.BlockSpec((TM, d), lambda i: (i, 0)),
                   pl.BlockSpec((TM, nkv), lambda i: (i, 0)),
                   pl.BlockSpec((TM, nkv), lambda i: (i, 0))],
        out_shape=[jax.ShapeDtypeStruct((t, d), BF16), jax.ShapeDtypeStruct((t, nkv), BF16),
                   jax.ShapeDtypeStruct((t, nkv), BF16)],
        compiler_params=_cp("parallel"),
        name="qkv_proj",
    )(x, g, mod, w, qg, kg, cos, sin)


def _attn_kernel(q_ref, kc_ref, vc_ref, *rest, n_lat):
    if n_lat:
        kl_ref, vl_ref, o_ref, s_scr, vaug = rest
    else:
        o_ref, s_scr, vaug = rest
    n_ctx = kc_ref.shape[0]
    tq = q_ref.shape[0]

    def fill_values():
        vaug[:, HEAD_DIM:] = jnp.ones((n_ctx + n_lat, HEAD_DIM), BF16)
        vaug[0:n_ctx, 0:HEAD_DIM] = vc_ref[...]
        if n_lat:
            vaug[n_ctx:, 0:HEAD_DIM] = vl_ref[...]

    if n_lat:
        pl.when(pl.program_id(2) == 0)(fill_values)
    else:
        fill_values()

    chunks = [(0, n_ctx)] + [(n_ctx + j, ATT_KC) for j in range(0, n_lat, ATT_KC)]
    nt = (((1,), (1,)), ((), ()))
    q_all = jnp.concatenate([q_ref[:, g * HEAD_DIM:(g + 1) * HEAD_DIM] for g in range(GQA_GROUP)], axis=0)
    m_part = jnp.full((GQA_GROUP * tq, LANES), -jnp.inf, F32)
    for off, size in chunks:
        keys = kc_ref[...] if off == 0 else kl_ref[off - n_ctx:off - n_ctx + size, :]
        sc = lax.dot_general(q_all, keys, nt, preferred_element_type=F32)
        s_scr[:, off:off + size] = sc
        for j in range(0, size, LANES):
            m_part = jnp.maximum(m_part, sc[:, j:j + LANES])
    m_row = jnp.max(m_part, axis=-1, keepdims=True)
    hr = GQA_GROUP * tq // 2
    acc = [jnp.zeros((hr, 2 * HEAD_DIM), F32), jnp.zeros((hr, 2 * HEAD_DIM), F32)]
    for off, size in chunks:
        for r in range(2):
            rows = slice(r * hr, (r + 1) * hr)
            p = jnp.exp2((s_scr[rows, off:off + size] - m_row[rows]).astype(BF16))
            acc[r] = acc[r] + jnp.dot(p, vaug[off:off + size, :], preferred_element_type=F32)
    for r in range(2):
        out = (acc[r][:, :HEAD_DIM] / acc[r][:, HEAD_DIM:]).astype(BF16)
        for j in range(GQA_GROUP // 2):
            g = r * (GQA_GROUP // 2) + j
            o_ref[:, g * HEAD_DIM:(g + 1) * HEAD_DIM] = out[j * tq:(j + 1) * tq]


def _attn_ctx_kernel(q_ref, kc_ref, vc_ref, o_all_ref, o_ref, s_scr, vaug):
    del o_all_ref
    _attn_kernel(q_ref, kc_ref, vc_ref, o_ref, s_scr, vaug, n_lat=0)


def _attention(q, k, v, s, c, b):
    t, d = q.shape
    gw = GQA_GROUP * HEAD_DIM
    tq = TM
    nq = s // tq
    assert s % ATT_KC == 0
    ctx_blk = lambda bi, h, *_: ((b * s) // c + bi, h)
    o_lat = pl.pallas_call(
        functools.partial(_attn_kernel, n_lat=s),
        grid=(b, N_KV_HEADS, nq),
        in_specs=[pl.BlockSpec((tq, gw), lambda bi, h, i: (bi * nq + i, h)),
                  pl.BlockSpec((c, HEAD_DIM), ctx_blk),
                  pl.BlockSpec((c, HEAD_DIM), ctx_blk),
                  pl.BlockSpec((s, HEAD_DIM), lambda bi, h, i: (bi, h)),
                  pl.BlockSpec((s, HEAD_DIM), lambda bi, h, i: (bi, h))],
        out_specs=pl.BlockSpec((tq, gw), lambda bi, h, i: (bi * nq + i, h)),
        out_shape=jax.ShapeDtypeStruct((t, d), BF16),
        scratch_shapes=[pltpu.VMEM((GQA_GROUP * tq, c + s), F32), pltpu.VMEM((c + s, 2 * HEAD_DIM), BF16)],
        compiler_params=_cp("parallel", "parallel", "arbitrary"),
        name="attn_lat",
    )(q, k, v, k, v)
    return pl.pallas_call(
        _attn_ctx_kernel,
        grid=(b, N_KV_HEADS),
        in_specs=[pl.BlockSpec((c, gw), ctx_blk),
                  pl.BlockSpec((c, HEAD_DIM), ctx_blk),
                  pl.BlockSpec((c, HEAD_DIM), ctx_blk),
                  pl.BlockSpec(memory_space=pl.ANY)],
        out_specs=pl.BlockSpec((c, gw), ctx_blk),
        out_shape=jax.ShapeDtypeStruct((t, d), BF16),
        scratch_shapes=[pltpu.VMEM((GQA_GROUP * c, c), F32), pltpu.VMEM((c, 2 * HEAD_DIM), BF16)],
        input_output_aliases={3: 0},
        compiler_params=_cp("parallel", "parallel"),
        name="attn_ctx",
    )(q, k, v, o_lat)


def _attention_mixer(tok, g, mod, w_qkv, qg, kg, w_o, s, c, b):
    cos, sin = _rope_tables(s)
    q, k, v = _qkv(tok, g, mod, w_qkv.astype(BF16), qg.reshape(1, -1), kg.reshape(1, -1), cos, sin, s, b)
    o = _attention(q, k, v, s, c, b)
    return _out_proj(o, tok, mod, w_o.astype(BF16), tok.shape[0], s, b, 2)


def _gmlp_kernel(x_ref, g_ref, mod_ref, w_in_ref, lng_ref, lnb_ref, ws_ref, bs_ref, w_out_ref, o_ref, uv_ref):
    d = x_ref.shape[-1]
    dcm = lng_ref.shape[-1]
    gw = dcm // CM_GROUPS
    x = x_ref[...]
    m = mod_ref[0]
    h = _norm_mod(x, g_ref[...], m[:, 0:d], m[:, d:2 * d])
    z = jax.nn.gelu(jnp.dot(h.astype(BF16), w_in_ref[...], preferred_element_type=F32))
    u = z[:, :dcm]
    v = z[:, dcm:]
    mu = jnp.mean(v, axis=-1, keepdims=True)
    vc = v - mu
    var = jnp.mean(vc * vc, axis=-1, keepdims=True)
    vn = (vc * lax.rsqrt(var + NORM_EPS) * lng_ref[...] + lnb_ref[...]).astype(BF16)
    for ck in range(x.shape[0] // CHUNK):
        rows = slice(ck * CHUNK, (ck + 1) * CHUNK)
        for gi in range(CM_GROUPS):
            cols = slice(gi * gw, (gi + 1) * gw)
            mix = jnp.dot(ws_ref[gi], vn[rows, cols], preferred_element_type=F32) + bs_ref[:, gi:gi + 1]
            uv_ref[rows, cols] = (u[rows, cols] * mix).astype(BF16)
    y = jnp.dot(uv_ref[...], w_out_ref[...], preferred_element_type=F32)
    o_ref[...] = x + m[:, 2 * d:3 * d] * y


def _gmlp_mixer(tok, g, mod, w_in, ln_g, ln_b, w_s, b_s, w_out, s, b):
    t, d = tok.shape
    dcm = ln_g.shape[-1]
    full = lambda shape: pl.BlockSpec(shape, lambda i: (0,) * len(shape))
    return pl.pallas_call(
        _gmlp_kernel,
        grid=(t // TM,),
        in_specs=[pl.BlockSpec((TM, d), lambda i: (i, 0)),
                  full((1, d)),
                  _mod_spec(d, s // TM, b),
                  full((d, 2 * dcm)), full((1, dcm)), full((1, dcm)),
                  full((CM_GROUPS, CHUNK, CHUNK)), full((CHUNK, CM_GROUPS)), full((dcm, d))],
        out_specs=pl.BlockSpec((TM, d), lambda i: (i, 0)),
        out_shape=jax.ShapeDtypeStruct((t, d), F32),
        scratch_shapes=[pltpu.VMEM((TM, dcm), BF16)],
        compiler_params=_cp("parallel"),
        name="gmlp",
    )(tok, g, mod, w_in.astype(BF16), ln_g.reshape(1, dcm), ln_b.reshape(1, dcm),
      w_s.astype(BF16), b_s.T, w_out.astype(BF16))


def _router_kernel(x_ref, g_ref, mod_ref, wrh_ref, wrl_ref, br_ref, hf_ref, rt_ref, ew_ref, cnt_ref, cnt_s):
    d = x_ref.shape[-1]
    tm = x_ref.shape[0]

    @pl.when(pl.program_id(0) == 0)
    def _():
        cnt_s[...] = jnp.zeros_like(cnt_s)

    m = mod_ref[0]
    hf = _norm_mod(x_ref[...], g_ref[...], m[:, 3 * d:4 * d], m[:, 4 * d:5 * d])
    hf_ref[...] = _pack_bf16_pairs(hf)
    hf_hi = hf.astype(BF16)
    hf_lo = (hf - hf_hi.astype(F32)).astype(BF16)
    logits = (jnp.dot(hf_hi, wrh_ref[...], preferred_element_type=F32)
              + jnp.dot(hf_lo, wrh_ref[...], preferred_element_type=F32)
              + jnp.dot(hf_hi, wrl_ref[...], preferred_element_type=F32)) + br_ref[...]
    lane = lax.broadcasted_iota(I32, logits.shape, 1)
    neg = -jnp.inf
    gl = jnp.where(lane < N_GROUPS, logits, neg)
    gmax = jnp.max(gl, axis=-1, keepdims=True)
    gsel = jnp.min(jnp.where(gl == gmax, lane, LANES), axis=-1, keepdims=True)
    gate_g = 1.0 / jnp.sum(jnp.exp(gl - gmax), axis=-1, keepdims=True)
    lo = N_GROUPS + gsel * EXPERTS_PER_GROUP
    el = jnp.where(jnp.logical_and(lane >= lo, lane < lo + EXPERTS_PER_GROUP), logits, neg)
    v1 = jnp.max(el, axis=-1, keepdims=True)
    i1 = jnp.min(jnp.where(el == v1, lane, LANES), axis=-1, keepdims=True)
    el2 = jnp.where(lane == i1, neg, el)
    v2 = jnp.max(el2, axis=-1, keepdims=True)
    i2 = jnp.min(jnp.where(el2 == v2, lane, LANES), axis=-1, keepdims=True)
    e21 = jnp.exp(v2 - v1)
    w1 = gate_g / (1.0 + e21)
    w2 = w1 * e21
    ew_ref[...] = jnp.where(lane == 0, w1, jnp.where(lane == 1, w2, 0.0))

    oh1 = lane == i1
    oh2 = lane == i2
    above = (lax.broadcasted_iota(I32, (tm, tm), 1) < lax.broadcasted_iota(I32, (tm, tm), 0)).astype(BF16)
    pre1 = jnp.dot(above, oh1.astype(BF16), preferred_element_type=F32)
    pre2 = jnp.dot(above, oh2.astype(BF16), preferred_element_type=F32)
    tot1 = jnp.sum(oh1.astype(F32), axis=0, keepdims=True)
    tot2 = jnp.sum(oh2.astype(F32), axis=0, keepdims=True)
    cnt = cnt_s[...]
    rank1 = jnp.sum(jnp.where(oh1, cnt + pre1, 0.0), axis=-1, keepdims=True).astype(I32)
    rank2 = jnp.sum(jnp.where(oh2, cnt + tot1 + pre2, 0.0), axis=-1, keepdims=True).astype(I32)
    cnt = cnt + tot1 + tot2
    cnt_s[...] = cnt
    cnt_ref[...] = jnp.broadcast_to(cnt, cnt_ref.shape)
    rt_ref[...] = jnp.where(lane == 0, i1 - N_GROUPS, jnp.where(lane == 1, i2 - N_GROUPS,
                            jnp.where(lane == 2, rank1, jnp.where(lane == 3, rank2, 0))))


def _router(x, g, mod, wr, br, nrows, s, b):
    d = x.shape[-1]
    row = lambda w: pl.BlockSpec((TR, w), lambda i: (i, 0))
    wr_hi = wr.astype(BF16)
    wr_lo = (wr - wr_hi.astype(F32)).astype(BF16)
    return pl.pallas_call(
        _router_kernel,
        grid=(nrows // TR,),
        in_specs=[row(d), pl.BlockSpec((1, d), lambda i: (0, 0)), _mod_spec(d, s // TR, b),
                  pl.BlockSpec((d, LANES), lambda i: (0, 0)), pl.BlockSpec((d, LANES), lambda i: (0, 0)),
                  pl.BlockSpec((1, LANES), lambda i: (0, 0))],
        out_specs=[row(d // 2), row(LANES), row(LANES), pl.BlockSpec((SUBLANES, LANES), lambda i: (0, 0))],
        out_shape=[jax.ShapeDtypeStruct((nrows, d // 2), I32), jax.ShapeDtypeStruct((nrows, LANES), I32),
                   jax.ShapeDtypeStruct((nrows, LANES), F32), jax.ShapeDtypeStruct((SUBLANES, LANES), F32)],
        scratch_shapes=[pltpu.VMEM((1, LANES), F32)],
        compiler_params=_cp("arbitrary"),
        name="moe_router",
    )(x, g, mod, wr_hi, wr_lo, br)


def _dispatch_plan(rt, cnt, bm):
    n_tok = rt.shape[0]
    counts = cnt[0, N_GROUPS:N_GROUPS + N_EXPERTS].astype(I32)
    padded = (counts + bm - 1) // bm * bm
    pad_end = jnp.cumsum(padded)
    pad_start = pad_end - padded
    experts = jnp.arange(N_EXPERTS, dtype=I32)
    start_of = jnp.sum(jnp.where(rt[:, 0:2, None] == experts, pad_start, 0), axis=-1)
    pos = (start_of + rt[:, 2:4]).astype(I32)
    n_rows = 2 * n_tok + N_EXPERTS * bm
    n_used = (pad_end[-1] // bm).astype(I32)
    blk = jnp.arange(n_rows // bm, dtype=I32)
    first_row = jnp.minimum(blk, n_used - 1) * bm
    blk_e = jnp.sum((pad_end[None, :] <= first_row[:, None]).astype(I32), axis=1)
    blk_e = jnp.minimum(blk_e, N_EXPERTS - 1).astype(I32)
    run_first = jnp.logical_and(blk < n_used, jnp.concatenate([jnp.ones((1,), bool), blk_e[1:] != blk_e[:-1]]))
    run_slot = (jnp.cumsum(run_first.astype(I32)) - 1) % 2
    later = jnp.where(counts > 0, experts, N_EXPERTS)
    next_present = lax.cummin(jnp.concatenate([later[1:], jnp.full((1,), N_EXPERTS, I32)]), reverse=True)
    next_e = jnp.where(next_present < N_EXPERTS, next_present, -1)[blk_e]
    plan = (blk_e, n_used.reshape(1), run_first.astype(I32), run_slot.astype(I32), next_e.astype(I32))
    return pos[:, 0], pos[:, 1], plan, n_rows


def _sc_mesh():
    return plsc.VectorSubcoreMesh(core_axis_name="c", subcore_axis_name="s")


def _sc_worker_base(per_worker):
    return (lax.axis_index("s") * SC_CORES + lax.axis_index("c")) * per_worker


def _sc_dispatch(hf, pos0, pos1, n_rows):
    t, d = hf.shape
    per_w = t // SC_WORKERS
    assert per_w * SC_WORKERS == t and per_w % SC_CHUNK == 0

    @functools.partial(
        pl.kernel, mesh=_sc_mesh(), out_type=jax.ShapeDtypeStruct((n_rows, d), hf.dtype),
        scratch_types=[pltpu.VMEM((SC_CHUNK,), I32), pltpu.VMEM((SC_CHUNK,), I32),
                       pltpu.VMEM((SC_CHUNK, d), hf.dtype)])
    def dispatch(hf_hbm, p0_hbm, p1_hbm, out_hbm, i0_v, i1_v, rows_v):
        base = _sc_worker_base(per_w)

        @pl.loop(0, per_w // SC_CHUNK)
        def _(ck):
            off = pl.multiple_of(base + ck * SC_CHUNK, SUBLANES)
            pltpu.sync_copy(hf_hbm.at[pl.ds(off, SC_CHUNK)], rows_v)
            pltpu.sync_copy(p0_hbm.at[pl.ds(off, SC_CHUNK)], i0_v)
            pltpu.sync_copy(p1_hbm.at[pl.ds(off, SC_CHUNK)], i1_v)
            pltpu.sync_copy(rows_v, out_hbm.at[i0_v])
            pltpu.sync_copy(rows_v, out_hbm.at[i1_v])

    return dispatch(hf, pos0, pos1)


def _sc_gather(rows, idx):
    n = idx.shape[0]
    d = rows.shape[1]
    per_w = n // SC_WORKERS
    assert per_w * SC_WORKERS == n and per_w % SC_CHUNK == 0

    @functools.partial(
        pl.kernel, mesh=_sc_mesh(), out_type=jax.ShapeDtypeStruct((n, d), rows.dtype),
        scratch_types=[pltpu.VMEM((SC_CHUNK,), I32), pltpu.VMEM((SC_CHUNK, d), rows.dtype)])
    def gather(rows_hbm, i_hbm, out_hbm, i_v, rows_v):
        base = _sc_worker_base(per_w)

        @pl.loop(0, per_w // SC_CHUNK)
        def _(ck):
            off = pl.multiple_of(base + ck * SC_CHUNK, SUBLANES)
            pltpu.sync_copy(i_hbm.at[pl.ds(off, SC_CHUNK)], i_v)
            pltpu.sync_copy(rows_hbm.at[i_v], rows_v)
            pltpu.sync_copy(rows_v, out_hbm.at[pl.ds(off, SC_CHUNK)])

    return gather(rows, idx)


def _expert_kernel(blk_e_ref, n_used_ref, first_ref, slot_ref, next_ref, x_ref, wg_hbm, wu_hbm, wd_hbm, y_ref,
                   wgf, wuf, wdf, wgb, wub, wdb, sem, *, e_base):
    i = pl.program_id(0)
    n_used = n_used_ref[0]

    def weight_copies(e, slot):
        return (pltpu.make_async_copy(wg_hbm.at[e_base + e], wgf.at[slot], sem.at[slot, 0]),
                pltpu.make_async_copy(wu_hbm.at[e_base + e], wuf.at[slot], sem.at[slot, 1]),
                pltpu.make_async_copy(wd_hbm.at[e_base + e], wdf.at[slot], sem.at[slot, 2]))

    @pl.when(i == 0)
    def _():
        for cp in weight_copies(blk_e_ref[0], 0):
            cp.start()

    @pl.when(jnp.logical_and(i < n_used, first_ref[i] == 1))
    def _():
        slot = slot_ref[i]
        for cp in weight_copies(blk_e_ref[i], slot):
            cp.wait()
        nxt = next_ref[i]

        @pl.when(nxt >= 0)
        def _():
            for cp in weight_copies(nxt, 1 - slot):
                cp.start()

        wgb[...] = wgf[slot].astype(BF16)
        wub[...] = wuf[slot].astype(BF16)
        wdb[...] = wdf[slot].astype(BF16)

    @pl.when(i < n_used)
    def _():
        x_hi, x_lo = _unpack_bf16_pairs(x_ref[...])
        xb = jnp.concatenate([x_hi.astype(BF16), x_lo.astype(BF16)], axis=1)
        gt = jnp.dot(xb, wgb[...], preferred_element_type=F32)
        up = jnp.dot(xb, wub[...], preferred_element_type=F32)
        act = (gt * _sigmoid(gt) * up).astype(BF16)
        y_ref[...] = _pack_bf16_pairs(jnp.dot(act, wdb[...], preferred_element_type=F32))

    @pl.when(i >= n_used)
    def _():
        y_ref[...] = jnp.zeros_like(y_ref)


def _experts(x_rows, plan, w_gate, w_up, w_down, layer):
    n_rows, dp = x_rows.shape
    depth, n_e, d, de = w_gate.shape
    bm = MOE_BM
    any_spec = pl.BlockSpec(memory_space=pl.ANY)
    grid_spec = pltpu.PrefetchScalarGridSpec(
        num_scalar_prefetch=5,
        grid=(n_rows // bm,),
        in_specs=[pl.BlockSpec((bm, dp), lambda i, be, nu, *_: (jnp.minimum(i, nu[0] - 1), 0)),
                  any_spec, any_spec, any_spec],
        out_specs=pl.BlockSpec((bm, dp), lambda i, *_: (i, 0)),
        scratch_shapes=[pltpu.VMEM((2, d, de), F32), pltpu.VMEM((2, d, de), F32), pltpu.VMEM((2, de, d), F32),
                        pltpu.VMEM((d, de), BF16), pltpu.VMEM((d, de), BF16), pltpu.VMEM((de, d), BF16),
                        pltpu.SemaphoreType.DMA((2, 3))],
    )
    return pl.pallas_call(
        functools.partial(_expert_kernel, e_base=layer * n_e),
        grid_spec=grid_spec,
        out_shape=jax.ShapeDtypeStruct((n_rows, dp), I32),
        compiler_params=_cp("arbitrary"),
        name="moe_experts",
    )(*plan, x_rows, w_gate.reshape(depth * n_e, d, de), w_up.reshape(depth * n_e, d, de),
      w_down.reshape(depth * n_e, de, d))


def _combine_kernel(x_ref, y0_ref, y1_ref, ew_ref, mod_ref, o_ref):
    d = x_ref.shape[1]
    h = d // 2
    ew = ew_ref[...]
    gate = mod_ref[0][:, 5 * d:6 * d]
    y0_hi, y0_lo = _unpack_bf16_pairs(y0_ref[...])
    y1_hi, y1_lo = _unpack_bf16_pairs(y1_ref[...])
    o_ref[:, :h] = x_ref[:, :h] + gate[:, :h] * (ew[:, 0:1] * y0_hi + ew[:, 1:2] * y1_hi)
    o_ref[:, h:] = x_ref[:, h:] + gate[:, h:] * (ew[:, 0:1] * y0_lo + ew[:, 1:2] * y1_lo)


def _combine(y01, x, ew, mod, nrows, s, b):
    d = x.shape[-1]
    nb = nrows // TR
    return pl.pallas_call(
        _combine_kernel,
        grid=(nb,),
        in_specs=[pl.BlockSpec((TR, d), lambda i: (i, 0)),
                  pl.BlockSpec((TR, d // 2), lambda i: (i, 0)),
                  pl.BlockSpec((TR, d // 2), lambda i: (i + nb, 0)),
                  pl.BlockSpec((TR, LANES), lambda i: (i, 0)),
                  _mod_spec(d, s // TR, b)],
        out_specs=pl.BlockSpec((TR, d), lambda i: (i, 0)),
        out_shape=jax.ShapeDtypeStruct((nrows, d), F32),
        compiler_params=_cp("parallel"),
        name="moe_combine",
    )(x, y01, y01, ew, mod)


def _hier_moe(x, g, mod, w_group, b_group, w_router, b_router, w_gate, w_up, w_down, layer, nrows, s, b):
    d = x.shape[-1]
    pad = LANES - N_GROUPS - N_EXPERTS
    wr = jnp.concatenate([w_group, w_router.reshape(d, N_EXPERTS), jnp.zeros((d, pad), F32)], axis=1)
    br = jnp.concatenate([b_group, b_router.reshape(N_EXPERTS), jnp.zeros((pad,), F32)]).reshape(1, LANES)
    hf, rt, ew, cnt = _router(x, g, mod, wr, br, nrows, s, b)
    pos0, pos1, plan, n_rows = _dispatch_plan(rt, cnt, MOE_BM)
    x_rows = _sc_dispatch(hf, pos0, pos1, n_rows)
    y_rows = _experts(x_rows, plan, w_gate, w_up, w_down, layer)
    y01 = _sc_gather(y_rows, jnp.concatenate([pos0, pos1]))
    return _combine(y01, x, ew, mod, nrows, s, b)


def kernel(x, c, ctx, c_ctx, ada_w, ada_b, norm_mix_g, norm_ffn_g, rg_w_in, rg_conv_w, rg_conv_b, rg_wa, rg_ba, rg_wi, rg_bi, rg_lambda, rg_w_out, at_w_qkv, at_q_g, at_k_g, at_w_o, cm_w_in, cm_ln_g, cm_ln_b, cm_w_s, cm_b_s, cm_w_out, moe_w_group, moe_b_group, moe_w_router, moe_b_router, moe_w_gate, moe_w_up, moe_w_down):
    b, s, d = x.shape
    cl = ctx.shape[1]
    depth = ada_w.shape[0]
    n_lat = b * s
    assert b < SUBLANES and s % TR == 0 and cl % TM == 0 and (b * cl) % TR == 0 and d == RG_BLOCKS * LANES

    cin = jnp.concatenate([c, c_ctx[None, :], jnp.zeros((SUBLANES - b - 1, d), F32)], axis=0)
    mod_all = _ada_table(cin, ada_w, ada_b).reshape(depth, SUBLANES, 1, N_MOD * d)
    tok = jnp.concatenate([x.reshape(n_lat, d), ctx.reshape(b * cl, d)], axis=0)

    for l in range(depth):
        kind = l % 3
        j = l // 3
        last = l == depth - 1
        mod = mod_all[l]
        g_mix = norm_mix_g[l].reshape(1, d)
        nrows = n_lat if last else tok.shape[0]
        if kind == 0:
            tok_mix = _rglru_mixer(tok, g_mix, mod, rg_w_in[j], rg_conv_w[j], rg_conv_b[j], rg_wa[j], rg_ba[j],
                                   rg_wi[j], rg_bi[j], rg_lambda[j], rg_w_out[j], s, cl, b, nrows)
        elif kind == 1:
            tok_mix = _attention_mixer(tok, g_mix, mod, at_w_qkv[j], at_q_g[j], at_k_g[j], at_w_o[j], s, cl, b)
        else:
            tok_mix = _gmlp_mixer(tok, g_mix, mod, cm_w_in[j], cm_ln_g[j], cm_ln_b[j], cm_w_s[j], cm_b_s[j],
                                  cm_w_out[j], s, b)
        tok = _hier_moe(tok_mix, norm_ffn_g[l].reshape(1, d), mod, moe_w_group[l], moe_b_group[l],
                        moe_w_router[l], moe_b_router[l], moe_w_gate, moe_w_up, moe_w_down, l,
                        nrows, s, b)
    return tok[:n_lat].reshape(b, s, d)
```

```python
import functools

import jax
import jax.numpy as jnp
from jax import lax
from jax.experimental import pallas as pl
from jax.experimental.pallas import tpu as pltpu
from jax.experimental.pallas import tpu_sc as plsc

F32 = jnp.float32
BF16 = jnp.bfloat16
I32 = jnp.int32
U32 = jnp.uint32

NORM_EPS = 1e-6
N_MOD = 6
GRID_W = 64
RG_BLOCKS = 8
CONV_W = 4
RG_C = 8.0
HEAD_DIM = 128
N_KV_HEADS = 2
GQA_GROUP = 4
ROPE_THETA = 10000.0
CHUNK = 128
CM_GROUPS = 8
N_GROUPS = 4
EXPERTS_PER_GROUP = 8
N_EXPERTS = N_GROUPS * EXPERTS_PER_GROUP

LANES = 128
SUBLANES = 8
TM = 256
TR = 512
TG = 512
TL = 256
HALO = 8
ATT_KC = 512
LOG2E = 1.4426950408889634
MOE_BM = 256
SC_CORES = 2
SC_WORKERS = 32
SC_CHUNK = 32
VMEM_LIMIT = 52 * 2**20


def _cp(*sem):
    return pltpu.CompilerParams(dimension_semantics=sem, vmem_limit_bytes=VMEM_LIMIT)


def _norm_mod(x, g, shift, scale):
    ms = jnp.mean(x * x, axis=-1, keepdims=True)
    y = x * lax.rsqrt(ms + NORM_EPS) * g
    return y * (1.0 + scale) + shift


def _sigmoid(x):
    return 0.5 * jnp.tanh(0.5 * x) + 0.5


def _pack_bf16_pairs(x):
    h = x.shape[-1] // 2
    hi = lax.bitcast_convert_type(x[:, :h].astype(BF16).astype(F32), U32)
    lo = lax.bitcast_convert_type(x[:, h:].astype(BF16).astype(F32), U32)
    return lax.bitcast_convert_type(hi | (lo >> 16), I32)


def _unpack_bf16_pairs(w):
    u = lax.bitcast_convert_type(w, U32)
    hi = lax.bitcast_convert_type(u & jnp.uint32(0xFFFF0000), F32)
    lo = lax.bitcast_convert_type(u << 16, F32)
    return hi, lo


def _mod_spec(d, rows_per_sample, n_samples):
    return pl.BlockSpec((1, 1, N_MOD * d),
                        lambda i, *_: (jnp.minimum(i // rows_per_sample, n_samples), 0, 0))


def _ada_kernel(c_ref, w_ref, b_ref, o_ref):
    cin = c_ref[...]
    act = cin * jax.nn.sigmoid(cin)
    w = w_ref[0]
    w_hi = w.astype(BF16)
    w_lo = (w - w_hi.astype(F32)).astype(BF16)
    a_hi = act.astype(BF16)
    a_lo = (act - a_hi.astype(F32)).astype(BF16)
    o_ref[0] = (jnp.dot(a_hi, w_hi, preferred_element_type=F32) + jnp.dot(a_lo, w_hi, preferred_element_type=F32)
                + jnp.dot(a_hi, w_lo, preferred_element_type=F32)) + b_ref[0]


def _ada_table(cin, ada_w, ada_b):
    depth, d, n = ada_w.shape
    tn = 2 * d
    return pl.pallas_call(
        _ada_kernel,
        grid=(depth, n // tn),
        in_specs=[pl.BlockSpec((SUBLANES, d), lambda l, j: (0, 0)),
                  pl.BlockSpec((1, d, tn), lambda l, j: (l, 0, j)),
                  pl.BlockSpec((1, 1, tn), lambda l, j: (l, 0, j))],
        out_specs=pl.BlockSpec((1, SUBLANES, tn), lambda l, j: (l, 0, j)),
        out_shape=jax.ShapeDtypeStruct((depth, SUBLANES, n), F32),
        compiler_params=_cp("parallel", "parallel"),
        name="ada_table",
    )(cin, ada_w, ada_b.reshape(depth, 1, n))


def _out_kernel(y_ref, x_ref, mod_ref, w_ref, o_ref, *, gate_idx):
    d = x_ref.shape[-1]
    gate = mod_ref[0][:, gate_idx * d:(gate_idx + 1) * d]
    y = jnp.dot(y_ref[...].astype(BF16), w_ref[...], preferred_element_type=F32)
    o_ref[...] = x_ref[...] + gate * y


def _out_proj(y, x, mod, w, nrows, s, b, gate_idx):
    d = x.shape[-1]
    k = y.shape[-1]
    return pl.pallas_call(
        functools.partial(_out_kernel, gate_idx=gate_idx),
        grid=(nrows // TR,),
        in_specs=[pl.BlockSpec((TR, k), lambda i: (i, 0)),
                  pl.BlockSpec((TR, d), lambda i: (i, 0)),
                  _mod_spec(d, s // TR, b),
                  pl.BlockSpec((k, d), lambda i: (0, 0))],
        out_specs=pl.BlockSpec((TR, d), lambda i: (i, 0)),
        out_shape=jax.ShapeDtypeStruct((nrows, d), F32),
        compiler_params=_cp("parallel"),
        name="out_proj",
    )(y, x, mod, w)


def _rg_in_kernel(x_ref, g_ref, mod_ref, w_ref, gg_ref, xin_ref):
    d = x_ref.shape[-1]
    m = mod_ref[0]
    h = _norm_mod(x_ref[...], g_ref[...], m[:, 0:d], m[:, d:2 * d])
    z = jnp.dot(h.astype(BF16), w_ref[...], preferred_element_type=F32)
    tm = x_ref.shape[0]
    for n in range(d // LANES):
        cols = slice(n * LANES, (n + 1) * LANES)
        gg_ref[pl.ds(n, tm, stride=SUBLANES), :] = jax.nn.gelu(z[:, cols])
        xin_ref[pl.ds(n, tm, stride=SUBLANES), :] = z[:, d + n * LANES:d + (n + 1) * LANES]


def _rg_in(x, g, mod, w, s, b):
    t, d = x.shape
    assert d == SUBLANES * LANES
    return pl.pallas_call(
        _rg_in_kernel,
        grid=(t // TM,),
        in_specs=[pl.BlockSpec((TM, d), lambda i: (i, 0)),
                  pl.BlockSpec((1, d), lambda i: (0, 0)),
                  _mod_spec(d, s // TM, b),
                  pl.BlockSpec((d, 2 * d), lambda i: (0, 0))],
        out_specs=[pl.BlockSpec((TM * SUBLANES, LANES), lambda i: (i, 0)),
                   pl.BlockSpec((TM * SUBLANES, LANES), lambda i: (i, 0))],
        out_shape=[jax.ShapeDtypeStruct((t * SUBLANES, LANES), F32),
                   jax.ShapeDtypeStruct((t * SUBLANES, LANES), F32)],
        compiler_params=_cp("parallel"),
        name="rg_in",
    )(x, g, mod, w)


def _rg_scan_kernel(xm_ref, xprev_ref, xnext_ref, cw_ref, cb_ref, wa_ref, wi_ref,
                    ba_ref, bi_ref, lam_ref, *rest, reverse, nlat):
    if reverse:
        hf_ref, gg_ref, out_ref, xpad, xc, a_s, b_s, h_s, hcar = rest
    else:
        out_ref, xpad, xc, a_s, b_s, hcar = rest
    rows = TL * SUBLANES
    hrows = HALO * SUBLANES
    j = pl.program_id(1)
    m = (nlat - j) if reverse else (j - 1)
    has_prev = jnp.logical_and(j >= 1, m > 0)
    has_next = jnp.logical_and(j >= 1, m < nlat - 1)

    @pl.when(j == 0)
    def _():
        hcar[...] = jnp.zeros_like(hcar)

    xpad[0:hrows, :] = jnp.where(has_prev, xprev_ref[...], 0.0)
    xpad[hrows:hrows + rows, :] = xm_ref[...]
    xpad[hrows + rows:2 * hrows + rows, :] = jnp.where(has_next, xnext_ref[...], 0.0)
    acc = jnp.broadcast_to(cb_ref[...][None], (TL, SUBLANES, LANES))
    for k in range(CONV_W):
        off = (HALO + k - CONV_W // 2) * SUBLANES
        tap = xpad[off:off + rows, :].reshape(TL, SUBLANES, LANES)
        acc = acc + tap * cw_ref[k][None]
    xc[...] = acc.reshape(rows, LANES)

    for n in range(RG_BLOCKS):
        cols = slice(n * LANES, (n + 1) * LANES)
        xn = xc[pl.ds(n, TL, stride=SUBLANES), :]
        xb = xn.astype(BF16)
        ta = jnp.tanh(jnp.dot(xb, wa_ref[n], preferred_element_type=F32) + ba_ref[:, cols])
        ti = jnp.tanh(jnp.dot(xb, wi_ref[n], preferred_element_type=F32) + bi_ref[:, cols])
        k = (-0.5 * RG_C * LOG2E) * jax.nn.softplus(-lam_ref[:, cols])
        a = jnp.exp2(k * ta + k)
        om = 1.0 - a * a
        root = jnp.where(om > 0.0, om * lax.rsqrt(om), 0.0)
        a_s[pl.ds(n, TL, stride=SUBLANES), :] = a
        b_s[pl.ds(n, TL, stride=SUBLANES), :] = root * (0.5 * xn) * (ti + 1.0)

    h_dst = h_s if reverse else out_ref

    def step(s, h):
        t = (TL - 1 - s) if reverse else s
        r0 = pl.multiple_of(t * SUBLANES, SUBLANES)
        h = a_s[pl.ds(r0, SUBLANES), :] * h + b_s[pl.ds(r0, SUBLANES), :]
        h_dst[pl.ds(r0, SUBLANES), :] = h
        return h

    hcar[...] = lax.fori_loop(0, TL, step, hcar[...], unroll=8)
    if reverse:
        h_s[...] = gg_ref[...] * (hf_ref[...] + h_s[...])
        for n in range(RG_BLOCKS):
            out_ref[:, n * LANES:(n + 1) * LANES] = h_s[pl.ds(n, TL, stride=SUBLANES), :].astype(BF16)


def _rg_scan(xin8, conv_w, conv_b, wa, wi, ba, bi, lam, s, c, b, reverse, hf8=None, gg8=None):
    assert c == TL and s % TL == 0
    rows = TL * SUBLANES
    hrows = HALO * SUBLANES
    nlat = s // TL
    t = xin8.shape[0] // SUBLANES
    n_halo = t // HALO

    def chunk(bi_, j):
        lat = bi_ * nlat + ((nlat - j) if reverse else (j - 1))
        return jnp.where(j == 0, (b * s) // TL + bi_, lat)

    main = pl.BlockSpec((rows, LANES), lambda bi_, j: (chunk(bi_, j), 0))
    prev = pl.BlockSpec((hrows, LANES),
                        lambda bi_, j: (jnp.maximum(chunk(bi_, j) * (TL // HALO) - 1, 0), 0))
    nxt = pl.BlockSpec((hrows, LANES),
                       lambda bi_, j: (jnp.minimum((chunk(bi_, j) + 1) * (TL // HALO), n_halo - 1), 0))
    full = lambda shape: pl.BlockSpec(shape, lambda bi_, j: (0,) * len(shape))
    d = RG_BLOCKS * LANES
    in_specs = [main, prev, nxt, full((CONV_W, SUBLANES, LANES)), full((SUBLANES, LANES)),
                full((RG_BLOCKS, LANES, LANES)), full((RG_BLOCKS, LANES, LANES)),
                full((1, d)), full((1, d)), full((1, d))]
    args = [xin8, xin8, xin8, conv_w.reshape(CONV_W, SUBLANES, LANES), conv_b.reshape(SUBLANES, LANES),
            (0.5 * wa).astype(BF16), (0.5 * wi).astype(BF16), 0.5 * ba.reshape(1, d), 0.5 * bi.reshape(1, d),
            lam.reshape(1, d)]
    scratch = [pltpu.VMEM((rows + 2 * hrows, LANES), F32), pltpu.VMEM((rows, LANES), F32),
               pltpu.VMEM((rows, LANES), F32), pltpu.VMEM((rows, LANES), F32)]
    if reverse:
        in_specs += [main, main]
        args += [hf8, gg8]
        scratch.append(pltpu.VMEM((rows, LANES), F32))
    scratch.append(pltpu.VMEM((SUBLANES, LANES), F32))
    return pl.pallas_call(
        functools.partial(_rg_scan_kernel, reverse=reverse, nlat=nlat),
        grid=(b, nlat + 1),
        in_specs=in_specs,
        out_specs=pl.BlockSpec((TL, d), lambda bi_, j: (chunk(bi_, j), 0)) if reverse else main,
        out_shape=jax.ShapeDtypeStruct((t, d), BF16) if reverse else jax.ShapeDtypeStruct(xin8.shape, F32),
        scratch_shapes=scratch,
        compiler_params=_cp("parallel", "arbitrary"),
        name="rg_scan_bwd" if reverse else "rg_scan_fwd",
    )(*args)


def _rglru_mixer(tok, g, mod, w_in, conv_w, conv_b, wa, ba, wi, bi, lam, w_out, s, c, b, nrows_out):
    gg8, xin8 = _rg_in(tok, g, mod, w_in.astype(BF16), s, b)
    hf8 = _rg_scan(xin8, conv_w, conv_b, wa[0], wi[0], ba[0], bi[0], lam[0], s, c, b, False)
    y = _rg_scan(xin8, conv_w, conv_b, wa[1], wi[1], ba[1], bi[1], lam[1], s, c, b, True, hf8, gg8)
    return _out_proj(y, tok, mod, w_out.astype(BF16), nrows_out, s, b, 2)


def _rope_tables(s):
    pos = jnp.arange(s, dtype=F32)
    row = jnp.floor(pos / GRID_W)
    col = pos - row * GRID_W
    n_freq = HEAD_DIM // 4
    inv = ROPE_THETA ** (-jnp.arange(n_freq, dtype=F32) * 2.0 / (HEAD_DIM // 2))
    ar = row[:, None] * inv
    ac = col[:, None] * inv
    cos = jnp.concatenate([jnp.cos(ar), jnp.cos(ar), jnp.cos(ac), jnp.cos(ac)], axis=1)
    sin = jnp.concatenate([-jnp.sin(ar), jnp.sin(ar), -jnp.sin(ac), jnp.sin(ac)], axis=1)
    cos = jnp.concatenate([cos, jnp.ones((TM, HEAD_DIM), F32)], axis=0)
    sin = jnp.concatenate([sin, jnp.zeros((TM, HEAD_DIM), F32)], axis=0)
    return cos, sin


def _qkv_kernel(x_ref, g_ref, mod_ref, w_ref, qg_ref, kg_ref, cos_ref, sin_ref, q_ref, k_ref, v_ref):
    d = x_ref.shape[-1]
    m = mod_ref[0]
    h = _norm_mod(x_ref[...], g_ref[...], m[:, 0:d], m[:, d:2 * d])
    z = jnp.dot(h.astype(BF16), w_ref[...], preferred_element_type=F32)
    cos = cos_ref[...]
    sin = sin_ref[...]
    lane = lax.broadcasted_iota(I32, cos.shape, 1)
    first_half = (lane % (HEAD_DIM // 2)) < (HEAD_DIM // 4)

    def head(zc, gain):
        ms = jnp.mean(zc * zc, axis=-1, keepdims=True)
        y = zc * lax.rsqrt(ms + NORM_EPS) * gain
        partner = jnp.where(first_half, pltpu.roll(y, HEAD_DIM - HEAD_DIM // 4, 1),
                            pltpu.roll(y, HEAD_DIM // 4, 1))
        return y * cos + partner * sin

    nq = q_ref.shape[-1] // HEAD_DIM
    nk = k_ref.shape[-1] // HEAD_DIM
    for j in range(nq):
        q_ref[:, j * HEAD_DIM:(j + 1) * HEAD_DIM] = (
            head(z[:, j * HEAD_DIM:(j + 1) * HEAD_DIM], qg_ref[...]) * (HEAD_DIM ** -0.5 * LOG2E)).astype(BF16)
    for j in range(nk):
        c0 = (nq + j) * HEAD_DIM
        k_ref[:, j * HEAD_DIM:(j + 1) * HEAD_DIM] = head(z[:, c0:c0 + HEAD_DIM], kg_ref[...]).astype(BF16)
    v_ref[...] = z[:, (nq + nk) * HEAD_DIM:].astype(BF16)


def _qkv(x, g, mod, w, qg, kg, cos, sin, s, b):
    t, d = x.shape
    nkv = N_KV_HEADS * HEAD_DIM
    n_pos = s // TM
    return pl.pallas_call(
        _qkv_kernel,
        grid=(t // TM,),
        in_specs=[pl.BlockSpec((TM, d), lambda i: (i, 0)),
                  pl.BlockSpec((1, d), lambda i: (0, 0)),
                  _mod_spec(d, s // TM, b),
                  pl.BlockSpec(w.shape, lambda i: (0, 0)),
                  pl.BlockSpec((1, HEAD_DIM), lambda i: (0, 0)),
                  pl.BlockSpec((1, HEAD_DIM), lambda i: (0, 0)),
                  pl.BlockSpec((TM, HEAD_DIM), lambda i: (jnp.where(i < b * n_pos, i % n_pos, n_pos), 0)),
                  pl.BlockSpec((TM, HEAD_DIM), lambda i: (jnp.where(i < b * n_pos, i % n_pos, n_pos), 0))],
        out_specs=[pl.BlockSpec((TM, d), lambda i: (i, 0)),
                   pl.BlockSpec((TM, nkv), lambda i: (i, 0)),
                   pl.BlockSpec((TM, nkv), lambda i: (i, 0))],
        out_shape=[jax.ShapeDtypeStruct((t, d), BF16), jax.ShapeDtypeStruct((t, nkv), BF16),
                   jax.ShapeDtypeStruct((t, nkv), BF16)],
        compiler_params=_cp("parallel"),
        name="qkv_proj",
    )(x, g, mod, w, qg, kg, cos, sin)


def _attn_kernel(q_ref, kc_ref, vc_ref, *rest, n_lat):
    if n_lat:
        kl_ref, vl_ref, o_ref, s_scr, vaug = rest
    else:
        o_ref, s_scr, vaug = rest
    n_ctx = kc_ref.shape[0]
    tq = q_ref.shape[0]

    def fill_values():
        vaug[:, HEAD_DIM:] = jnp.ones((n_ctx + n_lat, HEAD_DIM), BF16)
        vaug[0:n_ctx, 0:HEAD_DIM] = vc_ref[...]
        if n_lat:
            vaug[n_ctx:, 0:HEAD_DIM] = vl_ref[...]

    if n_lat:
        pl.when(pl.program_id(2) == 0)(fill_values)
    else:
        fill_values()

    chunks = [(0, n_ctx)] + [(n_ctx + j, ATT_KC) for j in range(0, n_lat, ATT_KC)]
    nt = (((1,), (1,)), ((), ()))
    q_all = jnp.concatenate([q_ref[:, g * HEAD_DIM:(g + 1) * HEAD_DIM] for g in range(GQA_GROUP)], axis=0)
    m_part = jnp.full((GQA_GROUP * tq, LANES), -jnp.inf, F32)
    for off, size in chunks:
        keys = kc_ref[...] if off == 0 else kl_ref[off - n_ctx:off - n_ctx + size, :]
        sc = lax.dot_general(q_all, keys, nt, preferred_element_type=F32)
        s_scr[:, off:off + size] = sc
        for j in range(0, size, LANES):
            m_part = jnp.maximum(m_part, sc[:, j:j + LANES])
    m_row = jnp.max(m_part, axis=-1, keepdims=True)
    hr = GQA_GROUP * tq // 2
    acc = [jnp.zeros((hr, 2 * HEAD_DIM), F32), jnp.zeros((hr, 2 * HEAD_DIM), F32)]
    for off, size in chunks:
        for r in range(2):
            rows = slice(r * hr, (r + 1) * hr)
            p = jnp.exp2((s_scr[rows, off:off + size] - m_row[rows]).astype(BF16))
            acc[r] = acc[r] + jnp.dot(p, vaug[off:off + size, :], preferred_element_type=F32)
    for r in range(2):
        out = (acc[r][:, :HEAD_DIM] / acc[r][:, HEAD_DIM:]).astype(BF16)
        for j in range(GQA_GROUP // 2):
            g = r * (GQA_GROUP // 2) + j
            o_ref[:, g * HEAD_DIM:(g + 1) * HEAD_DIM] = out[j * tq:(j + 1) * tq]


def _attn_ctx_kernel(q_ref, kc_ref, vc_ref, o_all_ref, o_ref, s_scr, vaug):
    del o_all_ref
    _attn_kernel(q_ref, kc_ref, vc_ref, o_ref, s_scr, vaug, n_lat=0)


def _attention(q, k, v, s, c, b):
    t, d = q.shape
    gw = GQA_GROUP * HEAD_DIM
    tq = TM
    nq = s // tq
    assert s % ATT_KC == 0
    ctx_blk = lambda bi, h, *_: ((b * s) // c + bi, h)
    o_lat = pl.pallas_call(
        functools.partial(_attn_kernel, n_lat=s),
        grid=(b, N_KV_HEADS, nq),
        in_specs=[pl.BlockSpec((tq, gw), lambda bi, h, i: (bi * nq + i, h)),
                  pl.BlockSpec((c, HEAD_DIM), ctx_blk),
                  pl.BlockSpec((c, HEAD_DIM), ctx_blk),
                  pl.BlockSpec((s, HEAD_DIM), lambda bi, h, i: (bi, h)),
                  pl.BlockSpec((s, HEAD_DIM), lambda bi, h, i: (bi, h))],
        out_specs=pl.BlockSpec((tq, gw), lambda bi, h, i: (bi * nq + i, h)),
        out_shape=jax.ShapeDtypeStruct((t, d), BF16),
        scratch_shapes=[pltpu.VMEM((GQA_GROUP * tq, c + s), F32), pltpu.VMEM((c + s, 2 * HEAD_DIM), BF16)],
        compiler_params=_cp("parallel", "parallel", "arbitrary"),
        name="attn_lat",
    )(q, k, v, k, v)
    return pl.pallas_call(
        _attn_ctx_kernel,
        grid=(b, N_KV_HEADS),
        in_specs=[pl.BlockSpec((c, gw), ctx_blk),
                  pl.BlockSpec((c, HEAD_DIM), ctx_blk),
                  pl.BlockSpec((c, HEAD_DIM), ctx_blk),
                  pl.BlockSpec(memory_space=pl.ANY)],
        out_specs=pl.BlockSpec((c, gw), ctx_blk),
        out_shape=jax.ShapeDtypeStruct((t, d), BF16),
        scratch_shapes=[pltpu.VMEM((GQA_GROUP * c, c), F32), pltpu.VMEM((c, 2 * HEAD_DIM), BF16)],
        input_output_aliases={3: 0},
        compiler_params=_cp("parallel", "parallel"),
        name="attn_ctx",
    )(q, k, v, o_lat)


def _attention_mixer(tok, g, mod, w_qkv, qg, kg, w_o, s, c, b):
    cos, sin = _rope_tables(s)
    q, k, v = _qkv(tok, g, mod, w_qkv.astype(BF16), qg.reshape(1, -1), kg.reshape(1, -1), cos, sin, s, b)
    o = _attention(q, k, v, s, c, b)
    return _out_proj(o, tok, mod, w_o.astype(BF16), tok.shape[0], s, b, 2)


def _gmlp_kernel(x_ref, g_ref, mod_ref, w_in_ref, lng_ref, lnb_ref, ws_ref, bs_ref, w_out_ref, o_ref, uv_ref):
    d = x_ref.shape[-1]
    dcm = lng_ref.shape[-1]
    gw = dcm // CM_GROUPS
    x = x_ref[...]
    m = mod_ref[0]
    h = _norm_mod(x, g_ref[...], m[:, 0:d], m[:, d:2 * d])
    z = jax.nn.gelu(jnp.dot(h.astype(BF16), w_in_ref[...], preferred_element_type=F32))
    u = z[:, :dcm]
    v = z[:, dcm:]
    mu = jnp.mean(v, axis=-1, keepdims=True)
    vc = v - mu
    var = jnp.mean(vc * vc, axis=-1, keepdims=True)
    vn = (vc * lax.rsqrt(var + NORM_EPS) * lng_ref[...] + lnb_ref[...]).astype(BF16)
    for ck in range(x.shape[0] // CHUNK):
        rows = slice(ck * CHUNK, (ck + 1) * CHUNK)
        for gi in range(CM_GROUPS):
            cols = slice(gi * gw, (gi + 1) * gw)
            mix = jnp.dot(ws_ref[gi], vn[rows, cols], preferred_element_type=F32) + bs_ref[:, gi:gi + 1]
            uv_ref[rows, cols] = (u[rows, cols] * mix).astype(BF16)
    y = jnp.dot(uv_ref[...], w_out_ref[...], preferred_element_type=F32)
    o_ref[...] = x + m[:, 2 * d:3 * d] * y


def _gmlp_mixer(tok, g, mod, w_in, ln_g, ln_b, w_s, b_s, w_out, s, b):
    t, d = tok.shape
    dcm = ln_g.shape[-1]
    full = lambda shape: pl.BlockSpec(shape, lambda i: (0,) * len(shape))
    return pl.pallas_call(
        _gmlp_kernel,
        grid=(t // TG,),
        in_specs=[pl.BlockSpec((TG, d), lambda i: (i, 0)),
                  full((1, d)),
                  _mod_spec(d, s // TG, b),
                  full((d, 2 * dcm)), full((1, dcm)), full((1, dcm)),
                  full((CM_GROUPS, CHUNK, CHUNK)), full((CHUNK, CM_GROUPS)), full((dcm, d))],
        out_specs=pl.BlockSpec((TG, d), lambda i: (i, 0)),
        out_shape=jax.ShapeDtypeStruct((t, d), F32),
        scratch_shapes=[pltpu.VMEM((TG, dcm), BF16)],
        compiler_params=_cp("parallel"),
        name="gmlp",
    )(tok, g, mod, w_in.astype(BF16), ln_g.reshape(1, dcm), ln_b.reshape(1, dcm),
      w_s.astype(BF16), b_s.T, w_out.astype(BF16))


def _router_kernel(x_ref, g_ref, mod_ref, wrh_ref, wrl_ref, br_ref, hf_ref, rt_ref, ew_ref, cnt_ref, cnt_s):
    d = x_ref.shape[-1]
    tm = x_ref.shape[0]

    @pl.when(pl.program_id(0) == 0)
    def _():
        cnt_s[...] = jnp.zeros_like(cnt_s)

    m = mod_ref[0]
    hf = _norm_mod(x_ref[...], g_ref[...], m[:, 3 * d:4 * d], m[:, 4 * d:5 * d])
    hf_ref[...] = _pack_bf16_pairs(hf)
    hf_hi = hf.astype(BF16)
    hf_lo = (hf - hf_hi.astype(F32)).astype(BF16)
    logits = (jnp.dot(hf_hi, wrh_ref[...], preferred_element_type=F32)
              + jnp.dot(hf_lo, wrh_ref[...], preferred_element_type=F32)
              + jnp.dot(hf_hi, wrl_ref[...], preferred_element_type=F32)) + br_ref[...]
    lane = lax.broadcasted_iota(I32, logits.shape, 1)
    neg = -jnp.inf
    gl = jnp.where(lane < N_GROUPS, logits, neg)
    gmax = jnp.max(gl, axis=-1, keepdims=True)
    gsel = jnp.min(jnp.where(gl == gmax, lane, LANES), axis=-1, keepdims=True)
    gate_g = 1.0 / jnp.sum(jnp.exp(gl - gmax), axis=-1, keepdims=True)
    lo = N_GROUPS + gsel * EXPERTS_PER_GROUP
    el = jnp.where(jnp.logical_and(lane >= lo, lane < lo + EXPERTS_PER_GROUP), logits, neg)
    v1 = jnp.max(el, axis=-1, keepdims=True)
    i1 = jnp.min(jnp.where(el == v1, lane, LANES), axis=-1, keepdims=True)
    el2 = jnp.where(lane == i1, neg, el)
    v2 = jnp.max(el2, axis=-1, keepdims=True)
    i2 = jnp.min(jnp.where(el2 == v2, lane, LANES), axis=-1, keepdims=True)
    e21 = jnp.exp(v2 - v1)
    w1 = gate_g / (1.0 + e21)
    w2 = w1 * e21
    ew_ref[...] = jnp.where(lane == 0, w1, jnp.where(lane == 1, w2, 0.0))

    oh1 = lane == i1
    oh2 = lane == i2
    above = (lax.broadcasted_iota(I32, (tm, tm), 1) < lax.broadcasted_iota(I32, (tm, tm), 0)).astype(BF16)
    pre1 = jnp.dot(above, oh1.astype(BF16), preferred_element_type=F32)
    pre2 = jnp.dot(above, oh2.astype(BF16), preferred_element_type=F32)
    tot1 = jnp.sum(oh1.astype(F32), axis=0, keepdims=True)
    tot2 = jnp.sum(oh2.astype(F32), axis=0, keepdims=True)
    cnt = cnt_s[...]
    rank1 = jnp.sum(jnp.where(oh1, cnt + pre1, 0.0), axis=-1, keepdims=True).astype(I32)
    rank2 = jnp.sum(jnp.where(oh2, cnt + tot1 + pre2, 0.0), axis=-1, keepdims=True).astype(I32)
    cnt = cnt + tot1 + tot2
    cnt_s[...] = cnt
    cnt_ref[...] = jnp.broadcast_to(cnt, cnt_ref.shape)
    rt = jnp.where(lane == 0, i1 - N_GROUPS, jnp.where(lane == 1, i2 - N_GROUPS,
                   jnp.where(lane == 2, rank1, jnp.where(lane == 3, rank2, 0))))
    rt_ref[...] = rt.T[:SUBLANES]


def _router(x, g, mod, wr, br, nrows, s, b):
    d = x.shape[-1]
    row = lambda w: pl.BlockSpec((TR, w), lambda i: (i, 0))
    wr_hi = wr.astype(BF16)
    wr_lo = (wr - wr_hi.astype(F32)).astype(BF16)
    return pl.pallas_call(
        _router_kernel,
        grid=(nrows // TR,),
        in_specs=[row(d), pl.BlockSpec((1, d), lambda i: (0, 0)), _mod_spec(d, s // TR, b),
                  pl.BlockSpec((d, LANES), lambda i: (0, 0)), pl.BlockSpec((d, LANES), lambda i: (0, 0)),
                  pl.BlockSpec((1, LANES), lambda i: (0, 0))],
        out_specs=[row(d // 2), pl.BlockSpec((SUBLANES, TR), lambda i: (0, i)), row(LANES),
                   pl.BlockSpec((SUBLANES, LANES), lambda i: (0, 0))],
        out_shape=[jax.ShapeDtypeStruct((nrows, d // 2), I32), jax.ShapeDtypeStruct((SUBLANES, nrows), I32),
                   jax.ShapeDtypeStruct((nrows, LANES), F32), jax.ShapeDtypeStruct((SUBLANES, LANES), F32)],
        scratch_shapes=[pltpu.VMEM((1, LANES), F32)],
        compiler_params=_cp("arbitrary"),
        name="moe_router",
    )(x, g, mod, wr_hi, wr_lo, br)


def _dispatch_plan(rt, cnt, bm):
    n_tok = rt.shape[1]
    counts = cnt[0, N_GROUPS:N_GROUPS + N_EXPERTS].astype(I32)
    padded = (counts + bm - 1) // bm * bm
    pad_end = jnp.cumsum(padded)
    pad_start = pad_end - padded
    experts = jnp.arange(N_EXPERTS, dtype=I32)
    start_of = jnp.sum(jnp.where(rt[0:2, None, :] == experts[None, :, None], pad_start[None, :, None], 0), axis=1)
    pos = (start_of + rt[2:4]).astype(I32)
    n_rows = 2 * n_tok + N_EXPERTS * bm
    n_used = (pad_end[-1] // bm).astype(I32)
    blk = jnp.arange(n_rows // bm, dtype=I32)
    first_row = jnp.minimum(blk, n_used - 1) * bm
    blk_e = jnp.sum((pad_end[None, :] <= first_row[:, None]).astype(I32), axis=1)
    blk_e = jnp.minimum(blk_e, N_EXPERTS - 1).astype(I32)
    run_first = jnp.logical_and(blk < n_used, jnp.concatenate([jnp.ones((1,), bool), blk_e[1:] != blk_e[:-1]]))
    run_slot = (jnp.cumsum(run_first.astype(I32)) - 1) % 2
    later = jnp.where(counts > 0, experts, N_EXPERTS)
    next_present = lax.cummin(jnp.concatenate([later[1:], jnp.full((1,), N_EXPERTS, I32)]), reverse=True)
    next_e = jnp.where(next_present < N_EXPERTS, next_present, -1)[blk_e]
    plan = (blk_e, n_used.reshape(1), run_first.astype(I32), run_slot.astype(I32), next_e.astype(I32))
    return pos, plan, n_rows


def _sc_mesh():
    return plsc.VectorSubcoreMesh(core_axis_name="c", subcore_axis_name="s")


def _sc_worker_base(per_worker):
    return (lax.axis_index("s") * SC_CORES + lax.axis_index("c")) * per_worker


def _sc_dispatch(hf, pos0, pos1, n_rows):
    t, d = hf.shape
    per_w = t // SC_WORKERS
    assert per_w * SC_WORKERS == t and per_w % SC_CHUNK == 0

    @functools.partial(
        pl.kernel, mesh=_sc_mesh(), out_type=jax.ShapeDtypeStruct((n_rows, d), hf.dtype),
        scratch_types=[pltpu.VMEM((SC_CHUNK,), I32), pltpu.VMEM((SC_CHUNK,), I32),
                       pltpu.VMEM((SC_CHUNK, d), hf.dtype)])
    def dispatch(hf_hbm, p0_hbm, p1_hbm, out_hbm, i0_v, i1_v, rows_v):
        base = _sc_worker_base(per_w)

        @pl.loop(0, per_w // SC_CHUNK)
        def _(ck):
            off = pl.multiple_of(base + ck * SC_CHUNK, SUBLANES)
            pltpu.sync_copy(hf_hbm.at[pl.ds(off, SC_CHUNK)], rows_v)
            pltpu.sync_copy(p0_hbm.at[pl.ds(off, SC_CHUNK)], i0_v)
            pltpu.sync_copy(p1_hbm.at[pl.ds(off, SC_CHUNK)], i1_v)
            pltpu.sync_copy(rows_v, out_hbm.at[i0_v])
            pltpu.sync_copy(rows_v, out_hbm.at[i1_v])

    return dispatch(hf, pos0, pos1)


def _sc_gather(rows, idx):
    n = idx.shape[0]
    d = rows.shape[1]
    per_w = n // SC_WORKERS
    assert per_w * SC_WORKERS == n and per_w % SC_CHUNK == 0

    @functools.partial(
        pl.kernel, mesh=_sc_mesh(), out_type=jax.ShapeDtypeStruct((n, d), rows.dtype),
        scratch_types=[pltpu.VMEM((SC_CHUNK,), I32), pltpu.VMEM((SC_CHUNK, d), rows.dtype)])
    def gather(rows_hbm, i_hbm, out_hbm, i_v, rows_v):
        base = _sc_worker_base(per_w)

        @pl.loop(0, per_w // SC_CHUNK)
        def _(ck):
            off = pl.multiple_of(base + ck * SC_CHUNK, SUBLANES)
            pltpu.sync_copy(i_hbm.at[pl.ds(off, SC_CHUNK)], i_v)
            pltpu.sync_copy(rows_hbm.at[i_v], rows_v)
            pltpu.sync_copy(rows_v, out_hbm.at[pl.ds(off, SC_CHUNK)])

    return gather(rows, idx)


def _expert_kernel(blk_e_ref, n_used_ref, first_ref, slot_ref, next_ref, x_ref, wg_hbm, wu_hbm, wd_hbm, y_ref,
                   wgf, wuf, wdf, wgb, wub, wdb, sem, *, e_base):
    i = pl.program_id(0)
    n_used = n_used_ref[0]

    def weight_copies(e, slot):
        return (pltpu.make_async_copy(wg_hbm.at[e_base + e], wgf.at[slot], sem.at[slot, 0]),
                pltpu.make_async_copy(wu_hbm.at[e_base + e], wuf.at[slot], sem.at[slot, 1]),
                pltpu.make_async_copy(wd_hbm.at[e_base + e], wdf.at[slot], sem.at[slot, 2]))

    @pl.when(i == 0)
    def _():
        for cp in weight_copies(blk_e_ref[0], 0):
            cp.start()

    @pl.when(jnp.logical_and(i < n_used, first_ref[i] == 1))
    def _():
        slot = slot_ref[i]
        for cp in weight_copies(blk_e_ref[i], slot):
            cp.wait()
        nxt = next_ref[i]

        @pl.when(nxt >= 0)
        def _():
            for cp in weight_copies(nxt, 1 - slot):
                cp.start()

        wgb[...] = wgf[slot].astype(BF16)
        wub[...] = wuf[slot].astype(BF16)
        wdb[...] = wdf[slot].astype(BF16)

    @pl.when(i < n_used)
    def _():
        x_hi, x_lo = _unpack_bf16_pairs(x_ref[...])
        xb = jnp.concatenate([x_hi.astype(BF16), x_lo.astype(BF16)], axis=1)
        gt = jnp.dot(xb, wgb[...], preferred_element_type=F32)
        up = jnp.dot(xb, wub[...], preferred_element_type=F32)
        act = (gt * _sigmoid(gt) * up).astype(BF16)
        y_ref[...] = _pack_bf16_pairs(jnp.dot(act, wdb[...], preferred_element_type=F32))


def _experts(x_rows, plan, w_gate, w_up, w_down, layer):
    n_rows, dp = x_rows.shape
    depth, n_e, d, de = w_gate.shape
    bm = MOE_BM
    any_spec = pl.BlockSpec(memory_space=pl.ANY)
    grid_spec = pltpu.PrefetchScalarGridSpec(
        num_scalar_prefetch=5,
        grid=(n_rows // bm,),
        in_specs=[pl.BlockSpec((bm, dp), lambda i, be, nu, *_: (jnp.minimum(i, nu[0] - 1), 0)),
                  any_spec, any_spec, any_spec],
        out_specs=pl.BlockSpec((bm, dp), lambda i, be, nu, *_: (jnp.minimum(i, nu[0] - 1), 0)),
        scratch_shapes=[pltpu.VMEM((2, d, de), F32), pltpu.VMEM((2, d, de), F32), pltpu.VMEM((2, de, d), F32),
                        pltpu.VMEM((d, de), BF16), pltpu.VMEM((d, de), BF16), pltpu.VMEM((de, d), BF16),
                        pltpu.SemaphoreType.DMA((2, 3))],
    )
    return pl.pallas_call(
        functools.partial(_expert_kernel, e_base=layer * n_e),
        grid_spec=grid_spec,
        out_shape=jax.ShapeDtypeStruct((n_rows, dp), I32),
        compiler_params=_cp("arbitrary"),
        name="moe_experts",
    )(*plan, x_rows, w_gate.reshape(depth * n_e, d, de), w_up.reshape(depth * n_e, d, de),
      w_down.reshape(depth * n_e, de, d))


def _combine_kernel(x_ref, y0_ref, y1_ref, ew_ref, mod_ref, o_ref):
    d = x_ref.shape[1]
    h = d // 2
    ew = ew_ref[...]
    gate = mod_ref[0][:, 5 * d:6 * d]
    y0_hi, y0_lo = _unpack_bf16_pairs(y0_ref[...])
    y1_hi, y1_lo = _unpack_bf16_pairs(y1_ref[...])
    o_ref[:, :h] = x_ref[:, :h] + gate[:, :h] * (ew[:, 0:1] * y0_hi + ew[:, 1:2] * y1_hi)
    o_ref[:, h:] = x_ref[:, h:] + gate[:, h:] * (ew[:, 0:1] * y0_lo + ew[:, 1:2] * y1_lo)


def _combine(y01, x, ew, mod, nrows, s, b):
    d = x.shape[-1]
    nb = nrows // TR
    return pl.pallas_call(
        _combine_kernel,
        grid=(nb,),
        in_specs=[pl.BlockSpec((TR, d), lambda i: (i, 0)),
                  pl.BlockSpec((TR, d // 2), lambda i: (i, 0)),
                  pl.BlockSpec((TR, d // 2), lambda i: (i + nb, 0)),
                  pl.BlockSpec((TR, LANES), lambda i: (i, 0)),
                  _mod_spec(d, s // TR, b)],
        out_specs=pl.BlockSpec((TR, d), lambda i: (i, 0)),
        out_shape=jax.ShapeDtypeStruct((nrows, d), F32),
        compiler_params=_cp("parallel"),
        name="moe_combine",
    )(x, y01, y01, ew, mod)


def _hier_moe(x, g, mod, w_group, b_group, w_router, b_router, w_gate, w_up, w_down, layer, nrows, s, b):
    d = x.shape[-1]
    pad = LANES - N_GROUPS - N_EXPERTS
    wr = jnp.concatenate([w_group, w_router.reshape(d, N_EXPERTS), jnp.zeros((d, pad), F32)], axis=1)
    br = jnp.concatenate([b_group, b_router.reshape(N_EXPERTS), jnp.zeros((pad,), F32)]).reshape(1, LANES)
    hf, rt, ew, cnt = _router(x, g, mod, wr, br, nrows, s, b)
    pos, plan, n_rows = _dispatch_plan(rt, cnt, MOE_BM)
    x_rows = _sc_dispatch(hf, pos[0], pos[1], n_rows)
    y_rows = _experts(x_rows, plan, w_gate, w_up, w_down, layer)
    y01 = _sc_gather(y_rows, pos.reshape(-1))
    return _combine(y01, x, ew, mod, nrows, s, b)


def kernel(x, c, ctx, c_ctx, ada_w, ada_b, norm_mix_g, norm_ffn_g, rg_w_in, rg_conv_w, rg_conv_b, rg_wa, rg_ba, rg_wi, rg_bi, rg_lambda, rg_w_out, at_w_qkv, at_q_g, at_k_g, at_w_o, cm_w_in, cm_ln_g, cm_ln_b, cm_w_s, cm_b_s, cm_w_out, moe_w_group, moe_b_group, moe_w_router, moe_b_router, moe_w_gate, moe_w_up, moe_w_down):
    b, s, d = x.shape
    cl = ctx.shape[1]
    depth = ada_w.shape[0]
    n_lat = b * s
    assert b < SUBLANES and s % TR == 0 and cl % TM == 0 and (b * cl) % TR == 0 and d == RG_BLOCKS * LANES

    cin = jnp.concatenate([c, c_ctx[None, :], jnp.zeros((SUBLANES - b - 1, d), F32)], axis=0)
    mod_all = _ada_table(cin, ada_w, ada_b).reshape(depth, SUBLANES, 1, N_MOD * d)
    tok = jnp.concatenate([x.reshape(n_lat, d), ctx.reshape(b * cl, d)], axis=0)

    for l in range(depth):
        kind = l % 3
        j = l // 3
        last = l == depth - 1
        mod = mod_all[l]
        g_mix = norm_mix_g[l].reshape(1, d)
        nrows = n_lat if last else tok.shape[0]
        if kind == 0:
            tok_mix = _rglru_mixer(tok, g_mix, mod, rg_w_in[j], rg_conv_w[j], rg_conv_b[j], rg_wa[j], rg_ba[j],
                                   rg_wi[j], rg_bi[j], rg_lambda[j], rg_w_out[j], s, cl, b, nrows)
        elif kind == 1:
            tok_mix = _attention_mixer(tok, g_mix, mod, at_w_qkv[j], at_q_g[j], at_k_g[j], at_w_o[j], s, cl, b)
        else:
            tok_mix = _gmlp_mixer(tok, g_mix, mod, cm_w_in[j], cm_ln_g[j], cm_ln_b[j], cm_w_s[j], cm_b_s[j],
                                  cm_w_out[j], s, b)
        tok = _hier_moe(tok_mix, norm_ffn_g[l].reshape(1, d), mod, moe_w_group[l], moe_b_group[l],
                        moe_w_router[l], moe_b_router[l], moe_w_gate, moe_w_up, moe_w_down, l,
                        nrows, s, b)
    return tok[:n_lat].reshape(b, s, d)
```

```python
import functools

import jax
import jax.numpy as jnp
from jax import lax
from jax.experimental import pallas as pl
from jax.experimental.pallas import tpu as pltpu
from jax.experimental.pallas import tpu_sc as plsc

F32 = jnp.float32
BF16 = jnp.bfloat16
I32 = jnp.int32
U32 = jnp.uint32

NORM_EPS = 1e-6
N_MOD = 6
GRID_W = 64
RG_BLOCKS = 8
CONV_W = 4
RG_C = 8.0
HEAD_DIM = 128
N_KV_HEADS = 2
GQA_GROUP = 4
ROPE_THETA = 10000.0
CHUNK = 128
CM_GROUPS = 8
N_GROUPS = 4
EXPERTS_PER_GROUP = 8
N_EXPERTS = N_GROUPS * EXPERTS_PER_GROUP

LANES = 128
SUBLANES = 8
TM = 256
TR = 512
TG = 512
TL = 256
HALO = 8
ATT_KC = 512
LOG2E = 1.4426950408889634
MOE_BM = 256
MOE_SUB = 2
SC_CORES = 2
SC_WORKERS = 32
SC_CHUNK = 32
VMEM_LIMIT = 52 * 2**20


def _cp(*sem):
    return pltpu.CompilerParams(dimension_semantics=sem, vmem_limit_bytes=VMEM_LIMIT)


def _norm_mod(x, g, shift, scale):
    ms = jnp.mean(x * x, axis=-1, keepdims=True)
    y = x * lax.rsqrt(ms + NORM_EPS) * g
    return y * (1.0 + scale) + shift


def _sigmoid(x):
    return 0.5 * jnp.tanh(0.5 * x) + 0.5


def _pack_bf16_pairs(x):
    h = x.shape[-1] // 2
    hi = lax.bitcast_convert_type(x[:, :h].astype(BF16).astype(F32), U32)
    lo = lax.bitcast_convert_type(x[:, h:].astype(BF16).astype(F32), U32)
    return lax.bitcast_convert_type(hi | (lo >> 16), I32)


def _unpack_bf16_pairs(w):
    u = lax.bitcast_convert_type(w, U32)
    hi = lax.bitcast_convert_type(u & jnp.uint32(0xFFFF0000), F32)
    lo = lax.bitcast_convert_type(u << 16, F32)
    return hi, lo


def _mod_spec(d, rows_per_sample, n_samples):
    return pl.BlockSpec((1, 1, N_MOD * d),
                        lambda i, *_: (jnp.minimum(i // rows_per_sample, n_samples), 0, 0))


def _ada_kernel(c_ref, w_ref, b_ref, o_ref):
    cin = c_ref[...]
    act = cin * jax.nn.sigmoid(cin)
    w = w_ref[0]
    w_hi = w.astype(BF16)
    w_lo = (w - w_hi.astype(F32)).astype(BF16)
    a_hi = act.astype(BF16)
    a_lo = (act - a_hi.astype(F32)).astype(BF16)
    o_ref[0] = (jnp.dot(a_hi, w_hi, preferred_element_type=F32) + jnp.dot(a_lo, w_hi, preferred_element_type=F32)
                + jnp.dot(a_hi, w_lo, preferred_element_type=F32)) + b_ref[0]


def _ada_table(cin, ada_w, ada_b):
    depth, d, n = ada_w.shape
    tn = 2 * d
    return pl.pallas_call(
        _ada_kernel,
        grid=(depth, n // tn),
        in_specs=[pl.BlockSpec((SUBLANES, d), lambda l, j: (0, 0)),
                  pl.BlockSpec((1, d, tn), lambda l, j: (l, 0, j)),
                  pl.BlockSpec((1, 1, tn), lambda l, j: (l, 0, j))],
        out_specs=pl.BlockSpec((1, SUBLANES, tn), lambda l, j: (l, 0, j)),
        out_shape=jax.ShapeDtypeStruct((depth, SUBLANES, n), F32),
        compiler_params=_cp("parallel", "parallel"),
        name="ada_table",
    )(cin, ada_w, ada_b.reshape(depth, 1, n))


def _out_kernel(y_ref, x_ref, mod_ref, w_ref, o_ref, *, gate_idx):
    d = x_ref.shape[-1]
    gate = mod_ref[0][:, gate_idx * d:(gate_idx + 1) * d]
    y = jnp.dot(y_ref[...].astype(BF16), w_ref[...], preferred_element_type=F32)
    o_ref[...] = x_ref[...] + gate * y


def _out_proj(y, x, mod, w, nrows, s, b, gate_idx):
    d = x.shape[-1]
    k = y.shape[-1]
    return pl.pallas_call(
        functools.partial(_out_kernel, gate_idx=gate_idx),
        grid=(nrows // TR,),
        in_specs=[pl.BlockSpec((TR, k), lambda i: (i, 0)),
                  pl.BlockSpec((TR, d), lambda i: (i, 0)),
                  _mod_spec(d, s // TR, b),
                  pl.BlockSpec((k, d), lambda i: (0, 0))],
        out_specs=pl.BlockSpec((TR, d), lambda i: (i, 0)),
        out_shape=jax.ShapeDtypeStruct((nrows, d), F32),
        compiler_params=_cp("parallel"),
        name="out_proj",
    )(y, x, mod, w)


def _rg_in_kernel(x_ref, g_ref, mod_ref, w_ref, gg_ref, xin_ref):
    d = x_ref.shape[-1]
    m = mod_ref[0]
    h = _norm_mod(x_ref[...], g_ref[...], m[:, 0:d], m[:, d:2 * d])
    z = jnp.dot(h.astype(BF16), w_ref[...], preferred_element_type=F32)
    tm = x_ref.shape[0]
    for n in range(d // LANES):
        cols = slice(n * LANES, (n + 1) * LANES)
        gg_ref[pl.ds(n, tm, stride=SUBLANES), :] = jax.nn.gelu(z[:, cols])
        xin_ref[pl.ds(n, tm, stride=SUBLANES), :] = z[:, d + n * LANES:d + (n + 1) * LANES]


def _rg_in(x, g, mod, w, s, b):
    t, d = x.shape
    assert d == SUBLANES * LANES
    return pl.pallas_call(
        _rg_in_kernel,
        grid=(t // TM,),
        in_specs=[pl.BlockSpec((TM, d), lambda i: (i, 0)),
                  pl.BlockSpec((1, d), lambda i: (0, 0)),
                  _mod_spec(d, s // TM, b),
                  pl.BlockSpec((d, 2 * d), lambda i: (0, 0))],
        out_specs=[pl.BlockSpec((TM * SUBLANES, LANES), lambda i: (i, 0)),
                   pl.BlockSpec((TM * SUBLANES, LANES), lambda i: (i, 0))],
        out_shape=[jax.ShapeDtypeStruct((t * SUBLANES, LANES), F32),
                   jax.ShapeDtypeStruct((t * SUBLANES, LANES), F32)],
        compiler_params=_cp("parallel"),
        name="rg_in",
    )(x, g, mod, w)


def _rg_scan_kernel(xm_ref, xprev_ref, xnext_ref, cw_ref, cb_ref, wa_ref, wi_ref,
                    ba_ref, bi_ref, lam_ref, *rest, reverse, nlat):
    if reverse:
        hf_ref, gg_ref, out_ref, xpad, xc, a_s, b_s, h_s, hcar = rest
    else:
        out_ref, xpad, xc, a_s, b_s, hcar = rest
    rows = TL * SUBLANES
    hrows = HALO * SUBLANES
    j = pl.program_id(1)
    m = (nlat - j) if reverse else (j - 1)
    has_prev = jnp.logical_and(j >= 1, m > 0)
    has_next = jnp.logical_and(j >= 1, m < nlat - 1)

    @pl.when(j == 0)
    def _():
        hcar[...] = jnp.zeros_like(hcar)

    xpad[0:hrows, :] = jnp.where(has_prev, xprev_ref[...], 0.0)
    xpad[hrows:hrows + rows, :] = xm_ref[...]
    xpad[hrows + rows:2 * hrows + rows, :] = jnp.where(has_next, xnext_ref[...], 0.0)
    acc = jnp.broadcast_to(cb_ref[...][None], (TL, SUBLANES, LANES))
    for k in range(CONV_W):
        off = (HALO + k - CONV_W // 2) * SUBLANES
        tap = xpad[off:off + rows, :].reshape(TL, SUBLANES, LANES)
        acc = acc + tap * cw_ref[k][None]
    xc[...] = acc.reshape(rows, LANES)

    for n in range(RG_BLOCKS):
        cols = slice(n * LANES, (n + 1) * LANES)
        xn = xc[pl.ds(n, TL, stride=SUBLANES), :]
        xb = xn.astype(BF16)
        ta = jnp.tanh(jnp.dot(xb, wa_ref[n], preferred_element_type=F32) + ba_ref[:, cols])
        ti = jnp.tanh(jnp.dot(xb, wi_ref[n], preferred_element_type=F32) + bi_ref[:, cols])
        k = (-0.5 * RG_C * LOG2E) * jax.nn.softplus(-lam_ref[:, cols])
        a = jnp.exp2(k * ta + k)
        om = 1.0 - a * a
        root = jnp.where(om > 0.0, om * lax.rsqrt(om), 0.0)
        a_s[pl.ds(n, TL, stride=SUBLANES), :] = a
        b_s[pl.ds(n, TL, stride=SUBLANES), :] = root * (0.5 * xn) * (ti + 1.0)

    h_dst = h_s if reverse else out_ref

    def two_steps(p, h):
        t0 = (TL - 1 - 2 * p) if reverse else 2 * p
        t1 = (t0 - 1) if reverse else (t0 + 1)
        r0 = pl.multiple_of(t0 * SUBLANES, SUBLANES)
        r1 = pl.multiple_of(t1 * SUBLANES, SUBLANES)
        a0 = a_s[pl.ds(r0, SUBLANES), :]
        b0 = b_s[pl.ds(r0, SUBLANES), :]
        a1 = a_s[pl.ds(r1, SUBLANES), :]
        b1 = b_s[pl.ds(r1, SUBLANES), :]
        h_dst[pl.ds(r0, SUBLANES), :] = a0 * h + b0
        h2 = (a1 * a0) * h + (a1 * b0 + b1)
        h_dst[pl.ds(r1, SUBLANES), :] = h2
        return h2

    hcar[...] = lax.fori_loop(0, TL // 2, two_steps, hcar[...], unroll=8)
    if reverse:
        h_s[...] = gg_ref[...] * (hf_ref[...] + h_s[...])
        for n in range(RG_BLOCKS):
            out_ref[:, n * LANES:(n + 1) * LANES] = h_s[pl.ds(n, TL, stride=SUBLANES), :].astype(BF16)


def _rg_scan(xin8, conv_w, conv_b, wa, wi, ba, bi, lam, s, c, b, reverse, hf8=None, gg8=None):
    assert c == TL and s % TL == 0
    rows = TL * SUBLANES
    hrows = HALO * SUBLANES
    nlat = s // TL
    t = xin8.shape[0] // SUBLANES
    n_halo = t // HALO

    def chunk(bi_, j):
        lat = bi_ * nlat + ((nlat - j) if reverse else (j - 1))
        return jnp.where(j == 0, (b * s) // TL + bi_, lat)

    main = pl.BlockSpec((rows, LANES), lambda bi_, j: (chunk(bi_, j), 0))
    prev = pl.BlockSpec((hrows, LANES),
                        lambda bi_, j: (jnp.maximum(chunk(bi_, j) * (TL // HALO) - 1, 0), 0))
    nxt = pl.BlockSpec((hrows, LANES),
                       lambda bi_, j: (jnp.minimum((chunk(bi_, j) + 1) * (TL // HALO), n_halo - 1), 0))
    full = lambda shape: pl.BlockSpec(shape, lambda bi_, j: (0,) * len(shape))
    d = RG_BLOCKS * LANES
    in_specs = [main, prev, nxt, full((CONV_W, SUBLANES, LANES)), full((SUBLANES, LANES)),
                full((RG_BLOCKS, LANES, LANES)), full((RG_BLOCKS, LANES, LANES)),
                full((1, d)), full((1, d)), full((1, d))]
    args = [xin8, xin8, xin8, conv_w.reshape(CONV_W, SUBLANES, LANES), conv_b.reshape(SUBLANES, LANES),
            (0.5 * wa).astype(BF16), (0.5 * wi).astype(BF16), 0.5 * ba.reshape(1, d), 0.5 * bi.reshape(1, d),
            lam.reshape(1, d)]
    scratch = [pltpu.VMEM((rows + 2 * hrows, LANES), F32), pltpu.VMEM((rows, LANES), F32),
               pltpu.VMEM((rows, LANES), F32), pltpu.VMEM((rows, LANES), F32)]
    if reverse:
        in_specs += [main, main]
        args += [hf8, gg8]
        scratch.append(pltpu.VMEM((rows, LANES), F32))
    scratch.append(pltpu.VMEM((SUBLANES, LANES), F32))
    return pl.pallas_call(
        functools.partial(_rg_scan_kernel, reverse=reverse, nlat=nlat),
        grid=(b, nlat + 1),
        in_specs=in_specs,
        out_specs=pl.BlockSpec((TL, d), lambda bi_, j: (chunk(bi_, j), 0)) if reverse else main,
        out_shape=jax.ShapeDtypeStruct((t, d), BF16) if reverse else jax.ShapeDtypeStruct(xin8.shape, F32),
        scratch_shapes=scratch,
        compiler_params=_cp("parallel", "arbitrary"),
        name="rg_scan_bwd" if reverse else "rg_scan_fwd",
    )(*args)


def _rglru_mixer(tok, g, mod, w_in, conv_w, conv_b, wa, ba, wi, bi, lam, w_out, s, c, b, nrows_out):
    gg8, xin8 = _rg_in(tok, g, mod, w_in.astype(BF16), s, b)
    hf8 = _rg_scan(xin8, conv_w, conv_b, wa[0], wi[0], ba[0], bi[0], lam[0], s, c, b, False)
    y = _rg_scan(xin8, conv_w, conv_b, wa[1], wi[1], ba[1], bi[1], lam[1], s, c, b, True, hf8, gg8)
    return _out_proj(y, tok, mod, w_out.astype(BF16), nrows_out, s, b, 2)


def _rope_tables(s):
    pos = jnp.arange(s, dtype=F32)
    row = jnp.floor(pos / GRID_W)
    col = pos - row * GRID_W
    n_freq = HEAD_DIM // 4
    inv = ROPE_THETA ** (-jnp.arange(n_freq, dtype=F32) * 2.0 / (HEAD_DIM // 2))
    ar = row[:, None] * inv
    ac = col[:, None] * inv
    cos = jnp.concatenate([jnp.cos(ar), jnp.cos(ar), jnp.cos(ac), jnp.cos(ac)], axis=1)
    sin = jnp.concatenate([-jnp.sin(ar), jnp.sin(ar), -jnp.sin(ac), jnp.sin(ac)], axis=1)
    cos = jnp.concatenate([cos, jnp.ones((TM, HEAD_DIM), F32)], axis=0)
    sin = jnp.concatenate([sin, jnp.zeros((TM, HEAD_DIM), F32)], axis=0)
    return cos, sin


def _qkv_kernel(x_ref, g_ref, mod_ref, w_ref, qg_ref, kg_ref, cos_ref, sin_ref, q_ref, k_ref, v_ref):
    d = x_ref.shape[-1]
    m = mod_ref[0]
    h = _norm_mod(x_ref[...], g_ref[...], m[:, 0:d], m[:, d:2 * d])
    z = jnp.dot(h.astype(BF16), w_ref[...], preferred_element_type=F32)
    cos = cos_ref[...]
    sin = sin_ref[...]
    lane = lax.broadcasted_iota(I32, cos.shape, 1)
    first_half = (lane % (HEAD_DIM // 2)) < (HEAD_DIM // 4)

    def head(zc, gain):
        ms = jnp.mean(zc * zc, axis=-1, keepdims=True)
        y = zc * lax.rsqrt(ms + NORM_EPS) * gain
        partner = jnp.where(first_half, pltpu.roll(y, HEAD_DIM - HEAD_DIM // 4, 1),
                            pltpu.roll(y, HEAD_DIM // 4, 1))
        return y * cos + partner * sin

    nq = q_ref.shape[-1] // HEAD_DIM
    nk = k_ref.shape[-1] // HEAD_DIM
    for j in range(nq):
        q_ref[:, j * HEAD_DIM:(j + 1) * HEAD_DIM] = (
            head(z[:, j * HEAD_DIM:(j + 1) * HEAD_DIM], qg_ref[...]) * (HEAD_DIM ** -0.5 * LOG2E)).astype(BF16)
    for j in range(nk):
        c0 = (nq + j) * HEAD_DIM
        k_ref[:, j * HEAD_DIM:(j + 1) * HEAD_DIM] = head(z[:, c0:c0 + HEAD_DIM], kg_ref[...]).astype(BF16)
    v_ref[...] = z[:, (nq + nk) * HEAD_DIM:].astype(BF16)


def _qkv(x, g, mod, w, qg, kg, cos, sin, s, b):
    t, d = x.shape
    nkv = N_KV_HEADS * HEAD_DIM
    n_pos = s // TM
    return pl.pallas_call(
        _qkv_kernel,
        grid=(t // TM,),
        in_specs=[pl.BlockSpec((TM, d), lambda i: (i, 0)),
                  pl.BlockSpec((1, d), lambda i: (0, 0)),
                  _mod_spec(d, s // TM, b),
                  pl.BlockSpec(w.shape, lambda i: (0, 0)),
                  pl.BlockSpec((1, HEAD_DIM), lambda i: (0, 0)),
                  pl.BlockSpec((1, HEAD_DIM), lambda i: (0, 0)),
                  pl.BlockSpec((TM, HEAD_DIM), lambda i: (jnp.where(i < b * n_pos, i % n_pos, n_pos), 0)),
                  pl.BlockSpec((TM, HEAD_DIM), lambda i: (jnp.where(i < b * n_pos, i % n_pos, n_pos), 0))],
        out_specs=[pl.BlockSpec((TM, d), lambda i: (i, 0)),
                   pl.BlockSpec((TM, nkv), lambda i: (i, 0)),
                   pl.BlockSpec((TM, nkv), lambda i: (i, 0))],
        out_shape=[jax.ShapeDtypeStruct((t, d), BF16), jax.ShapeDtypeStruct((t, nkv), BF16),
                   jax.ShapeDtypeStruct((t, nkv), BF16)],
        compiler_params=_cp("parallel"),
        name="qkv_proj",
    )(x, g, mod, w, qg, kg, cos, sin)


def _attn_kernel(q_ref, kc_ref, vc_ref, *rest, n_lat):
    if n_lat:
        kl_ref, vl_ref, o_ref, s_scr, vaug = rest
    else:
        o_ref, s_scr, vaug = rest
    n_ctx = kc_ref.shape[0]
    tq = q_ref.shape[0]

    def fill_values():
        vaug[:, HEAD_DIM:] = jnp.ones((n_ctx + n_lat, HEAD_DIM), BF16)
        vaug[0:n_ctx, 0:HEAD_DIM] = vc_ref[...]
        if n_lat:
            vaug[n_ctx:, 0:HEAD_DIM] = vl_ref[...]

    if n_lat:
        pl.when(pl.program_id(2) == 0)(fill_values)
    else:
        fill_values()

    chunks = [(0, n_ctx)] + [(n_ctx + j, ATT_KC) for j in range(0, n_lat, ATT_KC)]
    nt = (((1,), (1,)), ((), ()))
    q_all = jnp.concatenate([q_ref[:, g * HEAD_DIM:(g + 1) * HEAD_DIM] for g in range(GQA_GROUP)], axis=0)
    m_part = jnp.full((GQA_GROUP * tq, LANES), -jnp.inf, F32)
    for off, size in chunks:
        keys = kc_ref[...] if off == 0 else kl_ref[off - n_ctx:off - n_ctx + size, :]
        sc = lax.dot_general(q_all, keys, nt, preferred_element_type=F32)
        s_scr[:, off:off + size] = sc
        for j in range(0, size, LANES):
            m_part = jnp.maximum(m_part, sc[:, j:j + LANES])
    m_row = jnp.max(m_part, axis=-1, keepdims=True)
    hr = GQA_GROUP * tq // 2
    acc = [jnp.zeros((hr, 2 * HEAD_DIM), F32), jnp.zeros((hr, 2 * HEAD_DIM), F32)]
    for off, size in chunks:
        for r in range(2):
            rows = slice(r * hr, (r + 1) * hr)
            p = jnp.exp2((s_scr[rows, off:off + size] - m_row[rows]).astype(BF16))
            acc[r] = acc[r] + jnp.dot(p, vaug[off:off + size, :], preferred_element_type=F32)
    for r in range(2):
        out = (acc[r][:, :HEAD_DIM] / acc[r][:, HEAD_DIM:]).astype(BF16)
        for j in range(GQA_GROUP // 2):
            g = r * (GQA_GROUP // 2) + j
            o_ref[:, g * HEAD_DIM:(g + 1) * HEAD_DIM] = out[j * tq:(j + 1) * tq]


def _attn_ctx_kernel(q_ref, kc_ref, vc_ref, o_all_ref, o_ref, s_scr, vaug):
    del o_all_ref
    _attn_kernel(q_ref, kc_ref, vc_ref, o_ref, s_scr, vaug, n_lat=0)


def _attention(q, k, v, s, c, b):
    t, d = q.shape
    gw = GQA_GROUP * HEAD_DIM
    tq = TM
    nq = s // tq
    assert s % ATT_KC == 0
    ctx_blk = lambda bi, h, *_: ((b * s) // c + bi, h)
    o_lat = pl.pallas_call(
        functools.partial(_attn_kernel, n_lat=s),
        grid=(b, N_KV_HEADS, nq),
        in_specs=[pl.BlockSpec((tq, gw), lambda bi, h, i: (bi * nq + i, h)),
                  pl.BlockSpec((c, HEAD_DIM), ctx_blk),
                  pl.BlockSpec((c, HEAD_DIM), ctx_blk),
                  pl.BlockSpec((s, HEAD_DIM), lambda bi, h, i: (bi, h)),
                  pl.BlockSpec((s, HEAD_DIM), lambda bi, h, i: (bi, h))],
        out_specs=pl.BlockSpec((tq, gw), lambda bi, h, i: (bi * nq + i, h)),
        out_shape=jax.ShapeDtypeStruct((t, d), BF16),
        scratch_shapes=[pltpu.VMEM((GQA_GROUP * tq, c + s), F32), pltpu.VMEM((c + s, 2 * HEAD_DIM), BF16)],
        compiler_params=_cp("parallel", "parallel", "arbitrary"),
        name="attn_lat",
    )(q, k, v, k, v)
    return pl.pallas_call(
        _attn_ctx_kernel,
        grid=(b, N_KV_HEADS),
        in_specs=[pl.BlockSpec((c, gw), ctx_blk),
                  pl.BlockSpec((c, HEAD_DIM), ctx_blk),
                  pl.BlockSpec((c, HEAD_DIM), ctx_blk),
                  pl.BlockSpec(memory_space=pl.ANY)],
        out_specs=pl.BlockSpec((c, gw), ctx_blk),
        out_shape=jax.ShapeDtypeStruct((t, d), BF16),
        scratch_shapes=[pltpu.VMEM((GQA_GROUP * c, c), F32), pltpu.VMEM((c, 2 * HEAD_DIM), BF16)],
        input_output_aliases={3: 0},
        compiler_params=_cp("parallel", "parallel"),
        name="attn_ctx",
    )(q, k, v, o_lat)


def _attention_mixer(tok, g, mod, w_qkv, qg, kg, w_o, s, c, b):
    cos, sin = _rope_tables(s)
    q, k, v = _qkv(tok, g, mod, w_qkv.astype(BF16), qg.reshape(1, -1), kg.reshape(1, -1), cos, sin, s, b)
    o = _attention(q, k, v, s, c, b)
    return _out_proj(o, tok, mod, w_o.astype(BF16), tok.shape[0], s, b, 2)


def _gmlp_kernel(x_ref, g_ref, mod_ref, w_in_ref, lng_ref, lnb_ref, ws_ref, bs_ref, w_out_ref, o_ref, uv_ref):
    d = x_ref.shape[-1]
    dcm = lng_ref.shape[-1]
    gw = dcm // CM_GROUPS
    x = x_ref[...]
    m = mod_ref[0]
    h = _norm_mod(x, g_ref[...], m[:, 0:d], m[:, d:2 * d])
    z = jax.nn.gelu(jnp.dot(h.astype(BF16), w_in_ref[...], preferred_element_type=F32))
    u = z[:, :dcm]
    v = z[:, dcm:]
    mu = jnp.mean(v, axis=-1, keepdims=True)
    vc = v - mu
    var = jnp.mean(vc * vc, axis=-1, keepdims=True)
    vn = (vc * lax.rsqrt(var + NORM_EPS) * lng_ref[...] + lnb_ref[...]).astype(BF16)
    for ck in range(x.shape[0] // CHUNK):
        rows = slice(ck * CHUNK, (ck + 1) * CHUNK)
        for gi in range(CM_GROUPS):
            cols = slice(gi * gw, (gi + 1) * gw)
            mix = jnp.dot(ws_ref[gi], vn[rows, cols], preferred_element_type=F32) + bs_ref[:, gi:gi + 1]
            uv_ref[rows, cols] = (u[rows, cols] * mix).astype(BF16)
    y = jnp.dot(uv_ref[...], w_out_ref[...], preferred_element_type=F32)
    o_ref[...] = x + m[:, 2 * d:3 * d] * y


def _gmlp_mixer(tok, g, mod, w_in, ln_g, ln_b, w_s, b_s, w_out, s, b):
    t, d = tok.shape
    dcm = ln_g.shape[-1]
    full = lambda shape: pl.BlockSpec(shape, lambda i: (0,) * len(shape))
    return pl.pallas_call(
        _gmlp_kernel,
        grid=(t // TG,),
        in_specs=[pl.BlockSpec((TG, d), lambda i: (i, 0)),
                  full((1, d)),
                  _mod_spec(d, s // TG, b),
                  full((d, 2 * dcm)), full((1, dcm)), full((1, dcm)),
                  full((CM_GROUPS, CHUNK, CHUNK)), full((CHUNK, CM_GROUPS)), full((dcm, d))],
        out_specs=pl.BlockSpec((TG, d), lambda i: (i, 0)),
        out_shape=jax.ShapeDtypeStruct((t, d), F32),
        scratch_shapes=[pltpu.VMEM((TG, dcm), BF16)],
        compiler_params=_cp("parallel"),
        name="gmlp",
    )(tok, g, mod, w_in.astype(BF16), ln_g.reshape(1, dcm), ln_b.reshape(1, dcm),
      w_s.astype(BF16), b_s.T, w_out.astype(BF16))


def _router_kernel(x_ref, g_ref, mod_ref, wrh_ref, wrl_ref, br_ref, hf_ref, rt_ref, ew_ref, cnt_ref, cnt_s):
    d = x_ref.shape[-1]
    tm = x_ref.shape[0]

    @pl.when(pl.program_id(0) == 0)
    def _():
        cnt_s[...] = jnp.zeros_like(cnt_s)

    m = mod_ref[0]
    hf = _norm_mod(x_ref[...], g_ref[...], m[:, 3 * d:4 * d], m[:, 4 * d:5 * d])
    hf_ref[...] = _pack_bf16_pairs(hf)
    hf_hi = hf.astype(BF16)
    hf_lo = (hf - hf_hi.astype(F32)).astype(BF16)
    logits = (jnp.dot(hf_hi, wrh_ref[...], preferred_element_type=F32)
              + jnp.dot(hf_lo, wrh_ref[...], preferred_element_type=F32)
              + jnp.dot(hf_hi, wrl_ref[...], preferred_element_type=F32)) + br_ref[...]
    lane = lax.broadcasted_iota(I32, logits.shape, 1)
    neg = -jnp.inf
    gl = jnp.where(lane < N_GROUPS, logits, neg)
    gmax = jnp.max(gl, axis=-1, keepdims=True)
    gsel = jnp.min(jnp.where(gl == gmax, lane, LANES), axis=-1, keepdims=True)
    gate_g = 1.0 / jnp.sum(jnp.exp(gl - gmax), axis=-1, keepdims=True)
    lo = N_GROUPS + gsel * EXPERTS_PER_GROUP
    el = jnp.where(jnp.logical_and(lane >= lo, lane < lo + EXPERTS_PER_GROUP), logits, neg)
    v1 = jnp.max(el, axis=-1, keepdims=True)
    i1 = jnp.min(jnp.where(el == v1, lane, LANES), axis=-1, keepdims=True)
    el2 = jnp.where(lane == i1, neg, el)
    v2 = jnp.max(el2, axis=-1, keepdims=True)
    i2 = jnp.min(jnp.where(el2 == v2, lane, LANES), axis=-1, keepdims=True)
    e21 = jnp.exp(v2 - v1)
    w1 = gate_g / (1.0 + e21)
    w2 = w1 * e21
    ew_ref[...] = jnp.where(lane == 0, w1, jnp.where(lane == 1, w2, 0.0))

    oh1 = lane == i1
    oh2 = lane == i2
    above = (lax.broadcasted_iota(I32, (tm, tm), 1) < lax.broadcasted_iota(I32, (tm, tm), 0)).astype(BF16)
    pre1 = jnp.dot(above, oh1.astype(BF16), preferred_element_type=F32)
    pre2 = jnp.dot(above, oh2.astype(BF16), preferred_element_type=F32)
    tot1 = jnp.sum(oh1.astype(F32), axis=0, keepdims=True)
    tot2 = jnp.sum(oh2.astype(F32), axis=0, keepdims=True)
    cnt = cnt_s[...]
    rank1 = jnp.sum(jnp.where(oh1, cnt + pre1, 0.0), axis=-1, keepdims=True).astype(I32)
    rank2 = jnp.sum(jnp.where(oh2, cnt + tot1 + pre2, 0.0), axis=-1, keepdims=True).astype(I32)
    cnt = cnt + tot1 + tot2
    cnt_s[...] = cnt
    cnt_ref[...] = jnp.broadcast_to(cnt, cnt_ref.shape)
    rt = jnp.where(lane == 0, i1 - N_GROUPS, jnp.where(lane == 1, i2 - N_GROUPS,
                   jnp.where(lane == 2, rank1, jnp.where(lane == 3, rank2, 0))))
    rt_ref[...] = rt.T[:SUBLANES]


def _router(x, g, mod, wr, br, nrows, s, b):
    d = x.shape[-1]
    row = lambda w: pl.BlockSpec((TR, w), lambda i: (i, 0))
    wr_hi = wr.astype(BF16)
    wr_lo = (wr - wr_hi.astype(F32)).astype(BF16)
    return pl.pallas_call(
        _router_kernel,
        grid=(nrows // TR,),
        in_specs=[row(d), pl.BlockSpec((1, d), lambda i: (0, 0)), _mod_spec(d, s // TR, b),
                  pl.BlockSpec((d, LANES), lambda i: (0, 0)), pl.BlockSpec((d, LANES), lambda i: (0, 0)),
                  pl.BlockSpec((1, LANES), lambda i: (0, 0))],
        out_specs=[row(d // 2), pl.BlockSpec((SUBLANES, TR), lambda i: (0, i)), row(LANES),
                   pl.BlockSpec((SUBLANES, LANES), lambda i: (0, 0))],
        out_shape=[jax.ShapeDtypeStruct((nrows, d // 2), I32), jax.ShapeDtypeStruct((SUBLANES, nrows), I32),
                   jax.ShapeDtypeStruct((nrows, LANES), F32), jax.ShapeDtypeStruct((SUBLANES, LANES), F32)],
        scratch_shapes=[pltpu.VMEM((1, LANES), F32)],
        compiler_params=_cp("arbitrary"),
        name="moe_router",
    )(x, g, mod, wr_hi, wr_lo, br)


def _dispatch_plan(rt, cnt, bm):
    n_tok = rt.shape[1]
    counts = cnt[0, N_GROUPS:N_GROUPS + N_EXPERTS].astype(I32)
    padded = (counts + bm - 1) // bm * bm
    pad_end = jnp.cumsum(padded)
    pad_start = pad_end - padded
    experts = jnp.arange(N_EXPERTS, dtype=I32)
    start_of = jnp.sum(jnp.where(rt[0:2, None, :] == experts[None, :, None], pad_start[None, :, None], 0), axis=1)
    pos = (start_of + rt[2:4]).astype(I32)
    n_rows = 2 * n_tok + N_EXPERTS * bm
    n_used = (pad_end[-1] // bm).astype(I32)
    blk = jnp.arange(n_rows // bm, dtype=I32)
    first_row = jnp.minimum(blk, n_used - 1) * bm
    blk_e = jnp.sum((pad_end[None, :] <= first_row[:, None]).astype(I32), axis=1)
    blk_e = jnp.minimum(blk_e, N_EXPERTS - 1).astype(I32)
    run_first = jnp.logical_and(blk < n_used, jnp.concatenate([jnp.ones((1,), bool), blk_e[1:] != blk_e[:-1]]))
    run_slot = (jnp.cumsum(run_first.astype(I32)) - 1) % 2
    later = jnp.where(counts > 0, experts, N_EXPERTS)
    next_present = lax.cummin(jnp.concatenate([later[1:], jnp.full((1,), N_EXPERTS, I32)]), reverse=True)
    next_e = jnp.where(next_present < N_EXPERTS, next_present, -1)[blk_e]
    plan = (blk_e, n_used.reshape(1), run_first.astype(I32), run_slot.astype(I32), next_e.astype(I32))
    return pos, plan, n_rows


def _sc_mesh():
    return plsc.VectorSubcoreMesh(core_axis_name="c", subcore_axis_name="s")


def _sc_worker_base(per_worker):
    return (lax.axis_index("s") * SC_CORES + lax.axis_index("c")) * per_worker


def _sc_dispatch(hf, pos0, pos1, n_rows):
    t, d = hf.shape
    per_w = t // SC_WORKERS
    assert per_w * SC_WORKERS == t and per_w % SC_CHUNK == 0

    @functools.partial(
        pl.kernel, mesh=_sc_mesh(), out_type=jax.ShapeDtypeStruct((n_rows, d), hf.dtype),
        scratch_types=[pltpu.VMEM((SC_CHUNK,), I32), pltpu.VMEM((SC_CHUNK,), I32),
                       pltpu.VMEM((SC_CHUNK, d), hf.dtype)])
    def dispatch(hf_hbm, p0_hbm, p1_hbm, out_hbm, i0_v, i1_v, rows_v):
        base = _sc_worker_base(per_w)

        @pl.loop(0, per_w // SC_CHUNK)
        def _(ck):
            off = pl.multiple_of(base + ck * SC_CHUNK, SUBLANES)
            pltpu.sync_copy(hf_hbm.at[pl.ds(off, SC_CHUNK)], rows_v)
            pltpu.sync_copy(p0_hbm.at[pl.ds(off, SC_CHUNK)], i0_v)
            pltpu.sync_copy(p1_hbm.at[pl.ds(off, SC_CHUNK)], i1_v)
            pltpu.sync_copy(rows_v, out_hbm.at[i0_v])
            pltpu.sync_copy(rows_v, out_hbm.at[i1_v])

    return dispatch(hf, pos0, pos1)


def _sc_gather(rows, idx):
    n = idx.shape[0]
    d = rows.shape[1]
    per_w = n // SC_WORKERS
    assert per_w * SC_WORKERS == n and per_w % SC_CHUNK == 0

    @functools.partial(
        pl.kernel, mesh=_sc_mesh(), out_type=jax.ShapeDtypeStruct((n, d), rows.dtype),
        scratch_types=[pltpu.VMEM((SC_CHUNK,), I32), pltpu.VMEM((SC_CHUNK, d), rows.dtype)])
    def gather(rows_hbm, i_hbm, out_hbm, i_v, rows_v):
        base = _sc_worker_base(per_w)

        @pl.loop(0, per_w // SC_CHUNK)
        def _(ck):
            off = pl.multiple_of(base + ck * SC_CHUNK, SUBLANES)
            pltpu.sync_copy(i_hbm.at[pl.ds(off, SC_CHUNK)], i_v)
            pltpu.sync_copy(rows_hbm.at[i_v], rows_v)
            pltpu.sync_copy(rows_v, out_hbm.at[pl.ds(off, SC_CHUNK)])

    return gather(rows, idx)


def _expert_kernel(blk_e_ref, n_used_ref, first_ref, slot_ref, next_ref, x_ref, wg_hbm, wu_hbm, wd_hbm, y_ref,
                   wgf, wuf, wdf, wgb, wub, wdb, sem, *, e_base):
    n_used = n_used_ref[0]
    bm = x_ref.shape[0] // MOE_SUB

    def weight_copies(e, slot):
        return (pltpu.make_async_copy(wg_hbm.at[e_base + e], wgf.at[slot], sem.at[slot, 0]),
                pltpu.make_async_copy(wu_hbm.at[e_base + e], wuf.at[slot], sem.at[slot, 1]),
                pltpu.make_async_copy(wd_hbm.at[e_base + e], wdf.at[slot], sem.at[slot, 2]))

    @pl.when(pl.program_id(0) == 0)
    def _():
        for cp in weight_copies(blk_e_ref[0], 0):
            cp.start()

    for j in range(MOE_SUB):
        blk = pl.program_id(0) * MOE_SUB + j
        rows = slice(j * bm, (j + 1) * bm)

        @pl.when(jnp.logical_and(blk < n_used, first_ref[blk] == 1))
        def _():
            slot = slot_ref[blk]
            for cp in weight_copies(blk_e_ref[blk], slot):
                cp.wait()
            nxt = next_ref[blk]

            @pl.when(nxt >= 0)
            def _():
                for cp in weight_copies(nxt, 1 - slot):
                    cp.start()

            wgb[...] = wgf[slot].astype(BF16)
            wub[...] = wuf[slot].astype(BF16)
            wdb[...] = wdf[slot].astype(BF16)

        @pl.when(blk < n_used)
        def _():
            x_hi, x_lo = _unpack_bf16_pairs(x_ref[rows, :])
            xb = jnp.concatenate([x_hi.astype(BF16), x_lo.astype(BF16)], axis=1)
            gt = jnp.dot(xb, wgb[...], preferred_element_type=F32)
            up = jnp.dot(xb, wub[...], preferred_element_type=F32)
            act = (gt * _sigmoid(gt) * up).astype(BF16)
            y_ref[rows, :] = _pack_bf16_pairs(jnp.dot(act, wdb[...], preferred_element_type=F32))


def _experts(x_rows, plan, w_gate, w_up, w_down, layer):
    n_rows, dp = x_rows.shape
    depth, n_e, d, de = w_gate.shape
    step_rows = MOE_SUB * MOE_BM
    assert n_rows % step_rows == 0
    any_spec = pl.BlockSpec(memory_space=pl.ANY)
    last_used = lambda i, be, nu, *_: (jnp.minimum(i, (nu[0] - 1) // MOE_SUB), 0)
    grid_spec = pltpu.PrefetchScalarGridSpec(
        num_scalar_prefetch=5,
        grid=(n_rows // step_rows,),
        in_specs=[pl.BlockSpec((step_rows, dp), last_used), any_spec, any_spec, any_spec],
        out_specs=pl.BlockSpec((step_rows, dp), last_used),
        scratch_shapes=[pltpu.VMEM((2, d, de), F32), pltpu.VMEM((2, d, de), F32), pltpu.VMEM((2, de, d), F32),
                        pltpu.VMEM((d, de), BF16), pltpu.VMEM((d, de), BF16), pltpu.VMEM((de, d), BF16),
                        pltpu.SemaphoreType.DMA((2, 3))],
    )
    return pl.pallas_call(
        functools.partial(_expert_kernel, e_base=layer * n_e),
        grid_spec=grid_spec,
        out_shape=jax.ShapeDtypeStruct((n_rows, dp), I32),
        compiler_params=_cp("arbitrary"),
        name="moe_experts",
    )(*plan, x_rows, w_gate.reshape(depth * n_e, d, de), w_up.reshape(depth * n_e, d, de),
      w_down.reshape(depth * n_e, de, d))


def _combine_kernel(x_ref, y0_ref, y1_ref, ew_ref, mod_ref, o_ref):
    d = x_ref.shape[1]
    h = d // 2
    ew = ew_ref[...]
    gate = mod_ref[0][:, 5 * d:6 * d]
    y0_hi, y0_lo = _unpack_bf16_pairs(y0_ref[...])
    y1_hi, y1_lo = _unpack_bf16_pairs(y1_ref[...])
    o_ref[:, :h] = x_ref[:, :h] + gate[:, :h] * (ew[:, 0:1] * y0_hi + ew[:, 1:2] * y1_hi)
    o_ref[:, h:] = x_ref[:, h:] + gate[:, h:] * (ew[:, 0:1] * y0_lo + ew[:, 1:2] * y1_lo)


def _combine(y01, x, ew, mod, nrows, s, b):
    d = x.shape[-1]
    nb = nrows // TR
    return pl.pallas_call(
        _combine_kernel,
        grid=(nb,),
        in_specs=[pl.BlockSpec((TR, d), lambda i: (i, 0)),
                  pl.BlockSpec((TR, d // 2), lambda i: (i, 0)),
                  pl.BlockSpec((TR, d // 2), lambda i: (i + nb, 0)),
                  pl.BlockSpec((TR, LANES), lambda i: (i, 0)),
                  _mod_spec(d, s // TR, b)],
        out_specs=pl.BlockSpec((TR, d), lambda i: (i, 0)),
        out_shape=jax.ShapeDtypeStruct((nrows, d), F32),
        compiler_params=_cp("parallel"),
        name="moe_combine",
    )(x, y01, y01, ew, mod)


def _hier_moe(x, g, mod, w_group, b_group, w_router, b_router, w_gate, w_up, w_down, layer, nrows, s, b):
    d = x.shape[-1]
    pad = LANES - N_GROUPS - N_EXPERTS
    wr = jnp.concatenate([w_group, w_router.reshape(d, N_EXPERTS), jnp.zeros((d, pad), F32)], axis=1)
    br = jnp.concatenate([b_group, b_router.reshape(N_EXPERTS), jnp.zeros((pad,), F32)]).reshape(1, LANES)
    hf, rt, ew, cnt = _router(x, g, mod, wr, br, nrows, s, b)
    pos, plan, n_rows = _dispatch_plan(rt, cnt, MOE_BM)
    x_rows = _sc_dispatch(hf, pos[0], pos[1], n_rows)
    y_rows = _experts(x_rows, plan, w_gate, w_up, w_down, layer)
    y01 = _sc_gather(y_rows, pos.reshape(-1))
    return _combine(y01, x, ew, mod, nrows, s, b)


def kernel(x, c, ctx, c_ctx, ada_w, ada_b, norm_mix_g, norm_ffn_g, rg_w_in, rg_conv_w, rg_conv_b, rg_wa, rg_ba, rg_wi, rg_bi, rg_lambda, rg_w_out, at_w_qkv, at_q_g, at_k_g, at_w_o, cm_w_in, cm_ln_g, cm_ln_b, cm_w_s, cm_b_s, cm_w_out, moe_w_group, moe_b_group, moe_w_router, moe_b_router, moe_w_gate, moe_w_up, moe_w_down):
    b, s, d = x.shape
    cl = ctx.shape[1]
    depth = ada_w.shape[0]
    n_lat = b * s
    assert b < SUBLANES and s % TR == 0 and cl % TM == 0 and (b * cl) % TR == 0 and d == RG_BLOCKS * LANES

    cin = jnp.concatenate([c, c_ctx[None, :], jnp.zeros((SUBLANES - b - 1, d), F32)], axis=0)
    mod_all = _ada_table(cin, ada_w, ada_b).reshape(depth, SUBLANES, 1, N_MOD * d)
    tok = jnp.concatenate([x.reshape(n_lat, d), ctx.reshape(b * cl, d)], axis=0)

    for l in range(depth):
        kind = l % 3
        j = l // 3
        last = l == depth - 1
        mod = mod_all[l]
        g_mix = norm_mix_g[l].reshape(1, d)
        nrows = n_lat if last else tok.shape[0]
        if kind == 0:
            tok_mix = _rglru_mixer(tok, g_mix, mod, rg_w_in[j], rg_conv_w[j], rg_conv_b[j], rg_wa[j], rg_ba[j],
                                   rg_wi[j], rg_bi[j], rg_lambda[j], rg_w_out[j], s, cl, b, nrows)
        elif kind == 1:
            tok_mix = _attention_mixer(tok, g_mix, mod, at_w_qkv[j], at_q_g[j], at_k_g[j], at_w_o[j], s, cl, b)
        else:
            tok_mix = _gmlp_mixer(tok, g_mix, mod, cm_w_in[j], cm_ln_g[j], cm_ln_b[j], cm_w_s[j], cm_b_s[j],
                                  cm_w_out[j], s, b)
        tok = _hier_moe(tok_mix, norm_ffn_g[l].reshape(1, d), mod, moe_w_group[l], moe_b_group[l],
                        moe_w_router[l], moe_b_router[l], moe_w_gate, moe_w_up, moe_w_down, l,
                        nrows, s, b)
    return tok[:n_lat].reshape(b, s, d)
```

```python
import functools

import jax
import jax.numpy as jnp
from jax import lax
from jax.experimental import pallas as pl
from jax.experimental.pallas import tpu as pltpu
from jax.experimental.pallas import tpu_sc as plsc

F32 = jnp.float32
BF16 = jnp.bfloat16
I32 = jnp.int32
U32 = jnp.uint32

NORM_EPS = 1e-6
N_MOD = 6
GRID_W = 64
RG_BLOCKS = 8
CONV_W = 4
RG_C = 8.0
HEAD_DIM = 128
N_KV_HEADS = 2
GQA_GROUP = 4
ROPE_THETA = 10000.0
CHUNK = 128
CM_GROUPS = 8
N_GROUPS = 4
EXPERTS_PER_GROUP = 8
N_EXPERTS = N_GROUPS * EXPERTS_PER_GROUP

LANES = 128
SUBLANES = 8
TM = 256
TR = 512
TG = 512
TL = 256
HALO = 8
ATT_KC = 512
LOG2E = 1.4426950408889634
MOE_BM = 256
MOE_SUB = 2
SC_CORES = 2
SC_WORKERS = 32
SC_CHUNK = 32
VMEM_LIMIT = 52 * 2**20


def _cp(*sem):
    return pltpu.CompilerParams(dimension_semantics=sem, vmem_limit_bytes=VMEM_LIMIT)


def _norm_mod(x, g, shift, scale):
    ms = jnp.mean(x * x, axis=-1, keepdims=True)
    y = x * lax.rsqrt(ms + NORM_EPS) * g
    return y * (1.0 + scale) + shift


def _sigmoid(x):
    return 0.5 * jnp.tanh(0.5 * x) + 0.5


def _pack_bf16_pairs(x):
    h = x.shape[-1] // 2
    hi = lax.bitcast_convert_type(x[:, :h].astype(BF16).astype(F32), U32)
    lo = lax.bitcast_convert_type(x[:, h:].astype(BF16).astype(F32), U32)
    return lax.bitcast_convert_type(hi | (lo >> 16), I32)


def _unpack_bf16_pairs(w):
    u = lax.bitcast_convert_type(w, U32)
    hi = lax.bitcast_convert_type(u & jnp.uint32(0xFFFF0000), F32)
    lo = lax.bitcast_convert_type(u << 16, F32)
    return hi, lo


def _mod_spec(d, rows_per_sample, n_samples):
    return pl.BlockSpec((1, 1, N_MOD * d),
                        lambda i, *_: (jnp.minimum(i // rows_per_sample, n_samples), 0, 0))


def _ada_kernel(c_ref, w_ref, b_ref, o_ref):
    cin = c_ref[...]
    act = cin * jax.nn.sigmoid(cin)
    w = w_ref[0]
    w_hi = w.astype(BF16)
    w_lo = (w - w_hi.astype(F32)).astype(BF16)
    a_hi = act.astype(BF16)
    a_lo = (act - a_hi.astype(F32)).astype(BF16)
    o_ref[0] = (jnp.dot(a_hi, w_hi, preferred_element_type=F32) + jnp.dot(a_lo, w_hi, preferred_element_type=F32)
                + jnp.dot(a_hi, w_lo, preferred_element_type=F32)) + b_ref[0]


def _ada_table(cin, ada_w, ada_b):
    depth, d, n = ada_w.shape
    tn = 2 * d
    return pl.pallas_call(
        _ada_kernel,
        grid=(depth, n // tn),
        in_specs=[pl.BlockSpec((SUBLANES, d), lambda l, j: (0, 0)),
                  pl.BlockSpec((1, d, tn), lambda l, j: (l, 0, j)),
                  pl.BlockSpec((1, 1, tn), lambda l, j: (l, 0, j))],
        out_specs=pl.BlockSpec((1, SUBLANES, tn), lambda l, j: (l, 0, j)),
        out_shape=jax.ShapeDtypeStruct((depth, SUBLANES, n), F32),
        compiler_params=_cp("parallel", "parallel"),
        name="ada_table",
    )(cin, ada_w, ada_b.reshape(depth, 1, n))


def _out_kernel(y_ref, x_ref, mod_ref, w_ref, o_ref, *, gate_idx):
    d = x_ref.shape[-1]
    gate = mod_ref[0][:, gate_idx * d:(gate_idx + 1) * d]
    y = jnp.dot(y_ref[...].astype(BF16), w_ref[...], preferred_element_type=F32)
    o_ref[...] = x_ref[...] + gate * y


def _out_proj(y, x, mod, w, nrows, s, b, gate_idx):
    d = x.shape[-1]
    k = y.shape[-1]
    return pl.pallas_call(
        functools.partial(_out_kernel, gate_idx=gate_idx),
        grid=(nrows // TR,),
        in_specs=[pl.BlockSpec((TR, k), lambda i: (i, 0)),
                  pl.BlockSpec((TR, d), lambda i: (i, 0)),
                  _mod_spec(d, s // TR, b),
                  pl.BlockSpec((k, d), lambda i: (0, 0))],
        out_specs=pl.BlockSpec((TR, d), lambda i: (i, 0)),
        out_shape=jax.ShapeDtypeStruct((nrows, d), F32),
        compiler_params=_cp("parallel"),
        name="out_proj",
    )(y, x, mod, w)


def _rg_in_kernel(x_ref, g_ref, mod_ref, w_ref, gg_ref, xin_ref):
    d = x_ref.shape[-1]
    m = mod_ref[0]
    h = _norm_mod(x_ref[...], g_ref[...], m[:, 0:d], m[:, d:2 * d])
    z = jnp.dot(h.astype(BF16), w_ref[...], preferred_element_type=F32)
    tm = x_ref.shape[0]
    for n in range(d // LANES):
        cols = slice(n * LANES, (n + 1) * LANES)
        gg_ref[pl.ds(n, tm, stride=SUBLANES), :] = jax.nn.gelu(z[:, cols])
        xin_ref[pl.ds(n, tm, stride=SUBLANES), :] = z[:, d + n * LANES:d + (n + 1) * LANES]


def _rg_in(x, g, mod, w, s, b):
    t, d = x.shape
    assert d == SUBLANES * LANES
    return pl.pallas_call(
        _rg_in_kernel,
        grid=(t // TM,),
        in_specs=[pl.BlockSpec((TM, d), lambda i: (i, 0)),
                  pl.BlockSpec((1, d), lambda i: (0, 0)),
                  _mod_spec(d, s // TM, b),
                  pl.BlockSpec((d, 2 * d), lambda i: (0, 0))],
        out_specs=[pl.BlockSpec((TM * SUBLANES, LANES), lambda i: (i, 0)),
                   pl.BlockSpec((TM * SUBLANES, LANES), lambda i: (i, 0))],
        out_shape=[jax.ShapeDtypeStruct((t * SUBLANES, LANES), F32),
                   jax.ShapeDtypeStruct((t * SUBLANES, LANES), F32)],
        compiler_params=_cp("parallel"),
        name="rg_in",
    )(x, g, mod, w)


def _rg_scan_kernel(xm_ref, xprev_ref, xnext_ref, cw_ref, cb_ref, wa_ref, wi_ref,
                    ba_ref, bi_ref, lam_ref, *rest, reverse, nlat):
    if reverse:
        hf_ref, gg_ref, out_ref, xpad, xc, a_s, b_s, h_s, hcar = rest
    else:
        out_ref, xpad, xc, a_s, b_s, hcar = rest
    rows = TL * SUBLANES
    hrows = HALO * SUBLANES
    j = pl.program_id(1)
    m = (nlat - j) if reverse else (j - 1)
    has_prev = jnp.logical_and(j >= 1, m > 0)
    has_next = jnp.logical_and(j >= 1, m < nlat - 1)

    @pl.when(j == 0)
    def _():
        hcar[...] = jnp.zeros_like(hcar)

    xpad[0:hrows, :] = jnp.where(has_prev, xprev_ref[...], 0.0)
    xpad[hrows:hrows + rows, :] = xm_ref[...]
    xpad[hrows + rows:2 * hrows + rows, :] = jnp.where(has_next, xnext_ref[...], 0.0)
    acc = jnp.broadcast_to(cb_ref[...][None], (TL, SUBLANES, LANES))
    for k in range(CONV_W):
        off = (HALO + k - CONV_W // 2) * SUBLANES
        tap = xpad[off:off + rows, :].reshape(TL, SUBLANES, LANES)
        acc = acc + tap * cw_ref[k][None]
    xc[...] = acc.reshape(rows, LANES)

    for n in range(RG_BLOCKS):
        cols = slice(n * LANES, (n + 1) * LANES)
        xn = xc[pl.ds(n, TL, stride=SUBLANES), :]
        xb = xn.astype(BF16)
        ta = jnp.tanh(jnp.dot(xb, wa_ref[n], preferred_element_type=F32) + ba_ref[:, cols])
        ti = jnp.tanh(jnp.dot(xb, wi_ref[n], preferred_element_type=F32) + bi_ref[:, cols])
        k = (-0.5 * RG_C * LOG2E) * jax.nn.softplus(-lam_ref[:, cols])
        a = jnp.exp2(k * ta + k)
        om = 1.0 - a * a
        root = jnp.where(om > 0.0, om * lax.rsqrt(om), 0.0)
        a_s[pl.ds(n, TL, stride=SUBLANES), :] = a
        b_s[pl.ds(n, TL, stride=SUBLANES), :] = root * (0.5 * xn) * (ti + 1.0)

    h_dst = h_s if reverse else out_ref

    def two_steps(p, h):
        t0 = (TL - 1 - 2 * p) if reverse else 2 * p
        t1 = (t0 - 1) if reverse else (t0 + 1)
        r0 = pl.multiple_of(t0 * SUBLANES, SUBLANES)
        r1 = pl.multiple_of(t1 * SUBLANES, SUBLANES)
        a0 = a_s[pl.ds(r0, SUBLANES), :]
        b0 = b_s[pl.ds(r0, SUBLANES), :]
        a1 = a_s[pl.ds(r1, SUBLANES), :]
        b1 = b_s[pl.ds(r1, SUBLANES), :]
        h_dst[pl.ds(r0, SUBLANES), :] = a0 * h + b0
        h2 = (a1 * a0) * h + (a1 * b0 + b1)
        h_dst[pl.ds(r1, SUBLANES), :] = h2
        return h2

    hcar[...] = lax.fori_loop(0, TL // 2, two_steps, hcar[...], unroll=8)
    if reverse:
        h_s[...] = gg_ref[...] * (hf_ref[...] + h_s[...])
        for n in range(RG_BLOCKS):
            out_ref[:, n * LANES:(n + 1) * LANES] = h_s[pl.ds(n, TL, stride=SUBLANES), :].astype(BF16)


def _rg_scan(xin8, conv_w, conv_b, wa, wi, ba, bi, lam, s, c, b, reverse, hf8=None, gg8=None):
    assert c == TL and s % TL == 0
    rows = TL * SUBLANES
    hrows = HALO * SUBLANES
    nlat = s // TL
    t = xin8.shape[0] // SUBLANES
    n_halo = t // HALO

    def chunk(bi_, j):
        lat = bi_ * nlat + ((nlat - j) if reverse else (j - 1))
        return jnp.where(j == 0, (b * s) // TL + bi_, lat)

    main = pl.BlockSpec((rows, LANES), lambda bi_, j: (chunk(bi_, j), 0))
    prev = pl.BlockSpec((hrows, LANES),
                        lambda bi_, j: (jnp.maximum(chunk(bi_, j) * (TL // HALO) - 1, 0), 0))
    nxt = pl.BlockSpec((hrows, LANES),
                       lambda bi_, j: (jnp.minimum((chunk(bi_, j) + 1) * (TL // HALO), n_halo - 1), 0))
    full = lambda shape: pl.BlockSpec(shape, lambda bi_, j: (0,) * len(shape))
    d = RG_BLOCKS * LANES
    in_specs = [main, prev, nxt, full((CONV_W, SUBLANES, LANES)), full((SUBLANES, LANES)),
                full((RG_BLOCKS, LANES, LANES)), full((RG_BLOCKS, LANES, LANES)),
                full((1, d)), full((1, d)), full((1, d))]
    args = [xin8, xin8, xin8, conv_w.reshape(CONV_W, SUBLANES, LANES), conv_b.reshape(SUBLANES, LANES),
            (0.5 * wa).astype(BF16), (0.5 * wi).astype(BF16), 0.5 * ba.reshape(1, d), 0.5 * bi.reshape(1, d),
            lam.reshape(1, d)]
    scratch = [pltpu.VMEM((rows + 2 * hrows, LANES), F32), pltpu.VMEM((rows, LANES), F32),
               pltpu.VMEM((rows, LANES), F32), pltpu.VMEM((rows, LANES), F32)]
    if reverse:
        in_specs += [main, main]
        args += [hf8, gg8]
        scratch.append(pltpu.VMEM((rows, LANES), F32))
    scratch.append(pltpu.VMEM((SUBLANES, LANES), F32))
    return pl.pallas_call(
        functools.partial(_rg_scan_kernel, reverse=reverse, nlat=nlat),
        grid=(b, nlat + 1),
        in_specs=in_specs,
        out_specs=pl.BlockSpec((TL, d), lambda bi_, j: (chunk(bi_, j), 0)) if reverse else main,
        out_shape=jax.ShapeDtypeStruct((t, d), BF16) if reverse else jax.ShapeDtypeStruct(xin8.shape, F32),
        scratch_shapes=scratch,
        compiler_params=_cp("parallel", "arbitrary"),
        name="rg_scan_bwd" if reverse else "rg_scan_fwd",
    )(*args)


def _rglru_mixer(tok, g, mod, w_in, conv_w, conv_b, wa, ba, wi, bi, lam, w_out, s, c, b, nrows_out):
    gg8, xin8 = _rg_in(tok, g, mod, w_in.astype(BF16), s, b)
    hf8 = _rg_scan(xin8, conv_w, conv_b, wa[0], wi[0], ba[0], bi[0], lam[0], s, c, b, False)
    y = _rg_scan(xin8, conv_w, conv_b, wa[1], wi[1], ba[1], bi[1], lam[1], s, c, b, True, hf8, gg8)
    return _out_proj(y, tok, mod, w_out.astype(BF16), nrows_out, s, b, 2)


def _rope_tables(s):
    pos = jnp.arange(s, dtype=F32)
    row = jnp.floor(pos / GRID_W)
    col = pos - row * GRID_W
    n_freq = HEAD_DIM // 4
    inv = ROPE_THETA ** (-jnp.arange(n_freq, dtype=F32) * 2.0 / (HEAD_DIM // 2))
    ar = row[:, None] * inv
    ac = col[:, None] * inv
    cos = jnp.concatenate([jnp.cos(ar), jnp.cos(ar), jnp.cos(ac), jnp.cos(ac)], axis=1)
    sin = jnp.concatenate([-jnp.sin(ar), jnp.sin(ar), -jnp.sin(ac), jnp.sin(ac)], axis=1)
    cos = jnp.concatenate([cos, jnp.ones((TM, HEAD_DIM), F32)], axis=0)
    sin = jnp.concatenate([sin, jnp.zeros((TM, HEAD_DIM), F32)], axis=0)
    return cos, sin


def _qkv_kernel(x_ref, g_ref, mod_ref, w_ref, qg_ref, kg_ref, cos_ref, sin_ref, q_ref, k_ref, v_ref):
    d = x_ref.shape[-1]
    m = mod_ref[0]
    h = _norm_mod(x_ref[...], g_ref[...], m[:, 0:d], m[:, d:2 * d])
    z = jnp.dot(h.astype(BF16), w_ref[...], preferred_element_type=F32)
    cos = cos_ref[...]
    sin = sin_ref[...]
    lane = lax.broadcasted_iota(I32, cos.shape, 1)
    first_half = (lane % (HEAD_DIM // 2)) < (HEAD_DIM // 4)

    def head(zc, gain):
        ms = jnp.mean(zc * zc, axis=-1, keepdims=True)
        y = zc * lax.rsqrt(ms + NORM_EPS) * gain
        partner = jnp.where(first_half, pltpu.roll(y, HEAD_DIM - HEAD_DIM // 4, 1),
                            pltpu.roll(y, HEAD_DIM // 4, 1))
        return y * cos + partner * sin

    nq = q_ref.shape[-1] // HEAD_DIM
    nk = k_ref.shape[-1] // HEAD_DIM
    for j in range(nq):
        q_ref[:, j * HEAD_DIM:(j + 1) * HEAD_DIM] = (
            head(z[:, j * HEAD_DIM:(j + 1) * HEAD_DIM], qg_ref[...]) * (HEAD_DIM ** -0.5 * LOG2E)).astype(BF16)
    for j in range(nk):
        c0 = (nq + j) * HEAD_DIM
        k_ref[:, j * HEAD_DIM:(j + 1) * HEAD_DIM] = head(z[:, c0:c0 + HEAD_DIM], kg_ref[...]).astype(BF16)
    v_ref[...] = z[:, (nq + nk) * HEAD_DIM:].astype(BF16)


def _qkv(x, g, mod, w, qg, kg, cos, sin, s, b):
    t, d = x.shape
    nkv = N_KV_HEADS * HEAD_DIM
    n_pos = s // TM
    return pl.pallas_call(
        _qkv_kernel,
        grid=(t // TM,),
        in_specs=[pl.BlockSpec((TM, d), lambda i: (i, 0)),
                  pl.BlockSpec((1, d), lambda i: (0, 0)),
                  _mod_spec(d, s // TM, b),
                  pl.BlockSpec(w.shape, lambda i: (0, 0)),
                  pl.BlockSpec((1, HEAD_DIM), lambda i: (0, 0)),
                  pl.BlockSpec((1, HEAD_DIM), lambda i: (0, 0)),
                  pl.BlockSpec((TM, HEAD_DIM), lambda i: (jnp.where(i < b * n_pos, i % n_pos, n_pos), 0)),
                  pl.BlockSpec((TM, HEAD_DIM), lambda i: (jnp.where(i < b * n_pos, i % n_pos, n_pos), 0))],
        out_specs=[pl.BlockSpec((TM, d), lambda i: (i, 0)),
                   pl.BlockSpec((TM, nkv), lambda i: (i, 0)),
                   pl.BlockSpec((TM, nkv), lambda i: (i, 0))],
        out_shape=[jax.ShapeDtypeStruct((t, d), BF16), jax.ShapeDtypeStruct((t, nkv), BF16),
                   jax.ShapeDtypeStruct((t, nkv), BF16)],
        compiler_params=_cp("parallel"),
        name="qkv_proj",
    )(x, g, mod, w, qg, kg, cos, sin)


def _attn_kernel(q_ref, kc_ref, vc_ref, *rest, n_lat):
    if n_lat:
        kl_ref, vl_ref, o_ref, s_scr, vaug = rest
    else:
        o_ref, s_scr, vaug = rest
    n_ctx = kc_ref.shape[0]
    tq = q_ref.shape[0]

    def fill_values():
        vaug[:, HEAD_DIM:] = jnp.ones((n_ctx + n_lat, HEAD_DIM), BF16)
        vaug[0:n_ctx, 0:HEAD_DIM] = vc_ref[...]
        if n_lat:
            vaug[n_ctx:, 0:HEAD_DIM] = vl_ref[...]

    if n_lat:
        pl.when(pl.program_id(2) == 0)(fill_values)
    else:
        fill_values()

    chunks = [(0, n_ctx)] + [(n_ctx + j, ATT_KC) for j in range(0, n_lat, ATT_KC)]
    nt = (((1,), (1,)), ((), ()))
    q_all = jnp.concatenate([q_ref[:, g * HEAD_DIM:(g + 1) * HEAD_DIM] for g in range(GQA_GROUP)], axis=0)
    m_part = jnp.full((GQA_GROUP * tq, LANES), -jnp.inf, F32)
    for off, size in chunks:
        keys = kc_ref[...] if off == 0 else kl_ref[off - n_ctx:off - n_ctx + size, :]
        sc = lax.dot_general(q_all, keys, nt, preferred_element_type=F32)
        s_scr[:, off:off + size] = sc
        for j in range(0, size, LANES):
            m_part = jnp.maximum(m_part, sc[:, j:j + LANES])
    m_row = jnp.max(m_part, axis=-1, keepdims=True)
    hr = GQA_GROUP * tq // 2
    acc = [jnp.zeros((hr, 2 * HEAD_DIM), F32), jnp.zeros((hr, 2 * HEAD_DIM), F32)]
    for off, size in chunks:
        for r in range(2):
            rows = slice(r * hr, (r + 1) * hr)
            p = jnp.exp2((s_scr[rows, off:off + size] - m_row[rows]).astype(BF16))
            acc[r] = acc[r] + jnp.dot(p, vaug[off:off + size, :], preferred_element_type=F32)
    for r in range(2):
        out = (acc[r][:, :HEAD_DIM] / acc[r][:, HEAD_DIM:]).astype(BF16)
        for j in range(GQA_GROUP // 2):
            g = r * (GQA_GROUP // 2) + j
            o_ref[:, g * HEAD_DIM:(g + 1) * HEAD_DIM] = out[j * tq:(j + 1) * tq]


def _attn_ctx_kernel(q_ref, kc_ref, vc_ref, o_all_ref, o_ref, s_scr, vaug):
    del o_all_ref
    _attn_kernel(q_ref, kc_ref, vc_ref, o_ref, s_scr, vaug, n_lat=0)


def _attention(q, k, v, s, c, b):
    t, d = q.shape
    gw = GQA_GROUP * HEAD_DIM
    tq = TM
    nq = s // tq
    assert s % ATT_KC == 0
    ctx_blk = lambda bi, h, *_: ((b * s) // c + bi, h)
    o_lat = pl.pallas_call(
        functools.partial(_attn_kernel, n_lat=s),
        grid=(b, N_KV_HEADS, nq),
        in_specs=[pl.BlockSpec((tq, gw), lambda bi, h, i: (bi * nq + i, h)),
                  pl.BlockSpec((c, HEAD_DIM), ctx_blk),
                  pl.BlockSpec((c, HEAD_DIM), ctx_blk),
                  pl.BlockSpec((s, HEAD_DIM), lambda bi, h, i: (bi, h)),
                  pl.BlockSpec((s, HEAD_DIM), lambda bi, h, i: (bi, h))],
        out_specs=pl.BlockSpec((tq, gw), lambda bi, h, i: (bi * nq + i, h)),
        out_shape=jax.ShapeDtypeStruct((t, d), BF16),
        scratch_shapes=[pltpu.VMEM((GQA_GROUP * tq, c + s), F32), pltpu.VMEM((c + s, 2 * HEAD_DIM), BF16)],
        compiler_params=_cp("parallel", "parallel", "arbitrary"),
        name="attn_lat",
    )(q, k, v, k, v)
    return pl.pallas_call(
        _attn_ctx_kernel,
        grid=(b, N_KV_HEADS),
        in_specs=[pl.BlockSpec((c, gw), ctx_blk),
                  pl.BlockSpec((c, HEAD_DIM), ctx_blk),
                  pl.BlockSpec((c, HEAD_DIM), ctx_blk),
                  pl.BlockSpec(memory_space=pl.ANY)],
        out_specs=pl.BlockSpec((c, gw), ctx_blk),
        out_shape=jax.ShapeDtypeStruct((t, d), BF16),
        scratch_shapes=[pltpu.VMEM((GQA_GROUP * c, c), F32), pltpu.VMEM((c, 2 * HEAD_DIM), BF16)],
        input_output_aliases={3: 0},
        compiler_params=_cp("parallel", "parallel"),
        name="attn_ctx",
    )(q, k, v, o_lat)


def _attention_mixer(tok, g, mod, w_qkv, qg, kg, w_o, s, c, b):
    cos, sin = _rope_tables(s)
    q, k, v = _qkv(tok, g, mod, w_qkv.astype(BF16), qg.reshape(1, -1), kg.reshape(1, -1), cos, sin, s, b)
    o = _attention(q, k, v, s, c, b)
    return _out_proj(o, tok, mod, w_o.astype(BF16), tok.shape[0], s, b, 2)


def _gmlp_kernel(x_ref, g_ref, mod_ref, w_in_ref, lng_ref, lnb_ref, ws_ref, bs_ref, w_out_ref, o_ref, uv_ref):
    d = x_ref.shape[-1]
    dcm = lng_ref.shape[-1]
    gw = dcm // CM_GROUPS
    x = x_ref[...]
    m = mod_ref[0]
    h = _norm_mod(x, g_ref[...], m[:, 0:d], m[:, d:2 * d])
    z = jax.nn.gelu(jnp.dot(h.astype(BF16), w_in_ref[...], preferred_element_type=F32))
    u = z[:, :dcm]
    v = z[:, dcm:]
    mu = jnp.mean(v, axis=-1, keepdims=True)
    vc = v - mu
    var = jnp.mean(vc * vc, axis=-1, keepdims=True)
    vn = (vc * lax.rsqrt(var + NORM_EPS) * lng_ref[...] + lnb_ref[...]).astype(BF16)
    for ck in range(x.shape[0] // CHUNK):
        rows = slice(ck * CHUNK, (ck + 1) * CHUNK)
        for gi in range(CM_GROUPS):
            cols = slice(gi * gw, (gi + 1) * gw)
            mix = jnp.dot(ws_ref[gi], vn[rows, cols], preferred_element_type=F32) + bs_ref[:, gi:gi + 1]
            uv_ref[rows, cols] = (u[rows, cols] * mix).astype(BF16)
    y = jnp.dot(uv_ref[...], w_out_ref[...], preferred_element_type=F32)
    o_ref[...] = x + m[:, 2 * d:3 * d] * y


def _gmlp_mixer(tok, g, mod, w_in, ln_g, ln_b, w_s, b_s, w_out, s, b):
    t, d = tok.shape
    dcm = ln_g.shape[-1]
    full = lambda shape: pl.BlockSpec(shape, lambda i: (0,) * len(shape))
    return pl.pallas_call(
        _gmlp_kernel,
        grid=(t // TG,),
        in_specs=[pl.BlockSpec((TG, d), lambda i: (i, 0)),
                  full((1, d)),
                  _mod_spec(d, s // TG, b),
                  full((d, 2 * dcm)), full((1, dcm)), full((1, dcm)),
                  full((CM_GROUPS, CHUNK, CHUNK)), full((CHUNK, CM_GROUPS)), full((dcm, d))],
        out_specs=pl.BlockSpec((TG, d), lambda i: (i, 0)),
        out_shape=jax.ShapeDtypeStruct((t, d), F32),
        scratch_shapes=[pltpu.VMEM((TG, dcm), BF16)],
        compiler_params=_cp("parallel"),
        name="gmlp",
    )(tok, g, mod, w_in.astype(BF16), ln_g.reshape(1, dcm), ln_b.reshape(1, dcm),
      w_s.astype(BF16), b_s.T, w_out.astype(BF16))


def _router_kernel(x_ref, g_ref, mod_ref, wrh_ref, wrl_ref, br_ref, hf_ref, rt_ref, ew_ref, cnt_ref, cnt_s):
    d = x_ref.shape[-1]
    tm = x_ref.shape[0]

    @pl.when(pl.program_id(0) == 0)
    def _():
        cnt_s[...] = jnp.zeros_like(cnt_s)

    m = mod_ref[0]
    hf = _norm_mod(x_ref[...], g_ref[...], m[:, 3 * d:4 * d], m[:, 4 * d:5 * d])
    hf_ref[...] = _pack_bf16_pairs(hf)
    hf_hi = hf.astype(BF16)
    hf_lo = (hf - hf_hi.astype(F32)).astype(BF16)
    logits = (jnp.dot(hf_hi, wrh_ref[...], preferred_element_type=F32)
              + jnp.dot(hf_lo, wrh_ref[...], preferred_element_type=F32)
              + jnp.dot(hf_hi, wrl_ref[...], preferred_element_type=F32)) + br_ref[...]
    lane = lax.broadcasted_iota(I32, logits.shape, 1)
    neg = -jnp.inf
    gl = jnp.where(lane < N_GROUPS, logits, neg)
    gmax = jnp.max(gl, axis=-1, keepdims=True)
    gsel = jnp.min(jnp.where(gl == gmax, lane, LANES), axis=-1, keepdims=True)
    gate_g = 1.0 / jnp.sum(jnp.exp(gl - gmax), axis=-1, keepdims=True)
    lo = N_GROUPS + gsel * EXPERTS_PER_GROUP
    el = jnp.where(jnp.logical_and(lane >= lo, lane < lo + EXPERTS_PER_GROUP), logits, neg)
    v1 = jnp.max(el, axis=-1, keepdims=True)
    i1 = jnp.min(jnp.where(el == v1, lane, LANES), axis=-1, keepdims=True)
    el2 = jnp.where(lane == i1, neg, el)
    v2 = jnp.max(el2, axis=-1, keepdims=True)
    i2 = jnp.min(jnp.where(el2 == v2, lane, LANES), axis=-1, keepdims=True)
    e21 = jnp.exp(v2 - v1)
    w1 = gate_g / (1.0 + e21)
    w2 = w1 * e21
    ew_ref[...] = jnp.where(lane == 0, w1, jnp.where(lane == 1, w2, 0.0))

    oh1 = lane == i1
    oh2 = lane == i2
    above = (lax.broadcasted_iota(I32, (tm, tm), 1) < lax.broadcasted_iota(I32, (tm, tm), 0)).astype(BF16)
    pre1 = jnp.dot(above, oh1.astype(BF16), preferred_element_type=F32)
    pre2 = jnp.dot(above, oh2.astype(BF16), preferred_element_type=F32)
    tot1 = jnp.sum(oh1.astype(F32), axis=0, keepdims=True)
    tot2 = jnp.sum(oh2.astype(F32), axis=0, keepdims=True)
    cnt = cnt_s[...]
    rank1 = jnp.sum(jnp.where(oh1, cnt + pre1, 0.0), axis=-1, keepdims=True).astype(I32)
    rank2 = jnp.sum(jnp.where(oh2, cnt + tot1 + pre2, 0.0), axis=-1, keepdims=True).astype(I32)
    cnt = cnt + tot1 + tot2
    cnt_s[...] = cnt
    cnt_ref[...] = jnp.broadcast_to(cnt, cnt_ref.shape)
    rt = jnp.where(lane == 0, i1 - N_GROUPS, jnp.where(lane == 1, i2 - N_GROUPS,
                   jnp.where(lane == 2, rank1, jnp.where(lane == 3, rank2, 0))))
    rt_ref[...] = rt.T[:SUBLANES]


def _router(x, g, mod, wr, br, nrows, s, b):
    d = x.shape[-1]
    row = lambda w: pl.BlockSpec((TR, w), lambda i: (i, 0))
    wr_hi = wr.astype(BF16)
    wr_lo = (wr - wr_hi.astype(F32)).astype(BF16)
    return pl.pallas_call(
        _router_kernel,
        grid=(nrows // TR,),
        in_specs=[row(d), pl.BlockSpec((1, d), lambda i: (0, 0)), _mod_spec(d, s // TR, b),
                  pl.BlockSpec((d, LANES), lambda i: (0, 0)), pl.BlockSpec((d, LANES), lambda i: (0, 0)),
                  pl.BlockSpec((1, LANES), lambda i: (0, 0))],
        out_specs=[row(d // 2), pl.BlockSpec((SUBLANES, TR), lambda i: (0, i)), row(LANES),
                   pl.BlockSpec((SUBLANES, LANES), lambda i: (0, 0))],
        out_shape=[jax.ShapeDtypeStruct((nrows, d // 2), I32), jax.ShapeDtypeStruct((SUBLANES, nrows), I32),
                   jax.ShapeDtypeStruct((nrows, LANES), F32), jax.ShapeDtypeStruct((SUBLANES, LANES), F32)],
        scratch_shapes=[pltpu.VMEM((1, LANES), F32)],
        compiler_params=_cp("arbitrary"),
        name="moe_router",
    )(x, g, mod, wr_hi, wr_lo, br)


def _dispatch_plan(rt, cnt, bm):
    n_tok = rt.shape[1]
    counts = cnt[0, N_GROUPS:N_GROUPS + N_EXPERTS].astype(I32)
    padded = (counts + bm - 1) // bm * bm
    pad_end = jnp.cumsum(padded)
    pad_start = pad_end - padded
    experts = jnp.arange(N_EXPERTS, dtype=I32)
    start_of = jnp.sum(jnp.where(rt[0:2, None, :] == experts[None, :, None], pad_start[None, :, None], 0), axis=1)
    pos = (start_of + rt[2:4]).astype(I32)
    n_rows = 2 * n_tok + N_EXPERTS * bm
    n_used = (pad_end[-1] // bm).astype(I32)
    blk = jnp.arange(n_rows // bm, dtype=I32)
    first_row = jnp.minimum(blk, n_used - 1) * bm
    blk_e = jnp.sum((pad_end[None, :] <= first_row[:, None]).astype(I32), axis=1)
    blk_e = jnp.minimum(blk_e, N_EXPERTS - 1).astype(I32)
    run_first = jnp.logical_and(blk < n_used, jnp.concatenate([jnp.ones((1,), bool), blk_e[1:] != blk_e[:-1]]))
    run_slot = (jnp.cumsum(run_first.astype(I32)) - 1) % 2
    later = jnp.where(counts > 0, experts, N_EXPERTS)
    next_present = lax.cummin(jnp.concatenate([later[1:], jnp.full((1,), N_EXPERTS, I32)]), reverse=True)
    next_e = jnp.where(next_present < N_EXPERTS, next_present, -1)[blk_e]
    plan = (blk_e, n_used.reshape(1), run_first.astype(I32), run_slot.astype(I32), next_e.astype(I32))
    return pos, plan, n_rows


def _sc_mesh():
    return plsc.VectorSubcoreMesh(core_axis_name="c", subcore_axis_name="s")


def _sc_worker_base(per_worker):
    return (lax.axis_index("s") * SC_CORES + lax.axis_index("c")) * per_worker


def _sc_dispatch(hf, pos0, pos1, n_rows):
    t, d = hf.shape
    per_w = t // SC_WORKERS
    ch = SC_CHUNK
    n_ck = per_w // ch
    assert per_w * SC_WORKERS == t and n_ck * ch == per_w

    @functools.partial(
        pl.kernel, mesh=_sc_mesh(), out_type=jax.ShapeDtypeStruct((n_rows, d), hf.dtype),
        scratch_types=[pltpu.VMEM((per_w,), I32), pltpu.VMEM((per_w,), I32), pltpu.VMEM((2, ch, d), hf.dtype),
                       pltpu.SemaphoreType.DMA((2,)), pltpu.SemaphoreType.DMA((2,)), pltpu.SemaphoreType.DMA((2,))])
    def dispatch(hf_hbm, p0_hbm, p1_hbm, out_hbm, i0_v, i1_v, rows_v, sem_in, sem_s0, sem_s1):
        base = pl.multiple_of(_sc_worker_base(per_w), SUBLANES)
        pltpu.sync_copy(p0_hbm.at[pl.ds(base, per_w)], i0_v)
        pltpu.sync_copy(p1_hbm.at[pl.ds(base, per_w)], i1_v)

        def load(ck):
            return pltpu.make_async_copy(hf_hbm.at[pl.ds(base + ck * ch, ch)], rows_v.at[ck % 2], sem_in.at[ck % 2])

        def scatters(ck):
            src = rows_v.at[ck % 2]
            return (pltpu.make_async_copy(src, out_hbm.at[i0_v.at[pl.ds(ck * ch, ch)]], sem_s0.at[ck % 2]),
                    pltpu.make_async_copy(src, out_hbm.at[i1_v.at[pl.ds(ck * ch, ch)]], sem_s1.at[ck % 2]))

        load(0).start()
        for ck in range(n_ck):
            load(ck).wait()
            if ck + 1 < n_ck:
                if ck >= 1:
                    for cp in scatters(ck - 1):
                        cp.wait()
                load(ck + 1).start()
            for cp in scatters(ck):
                cp.start()
        for ck in range(max(n_ck - 2, 0), n_ck):
            for cp in scatters(ck):
                cp.wait()

    return dispatch(hf, pos0, pos1)


def _sc_gather(rows, idx):
    n = idx.shape[0]
    d = rows.shape[1]
    per_w = n // SC_WORKERS
    ch = 2 * SC_CHUNK
    n_ck = per_w // ch
    assert per_w * SC_WORKERS == n and n_ck * ch == per_w

    @functools.partial(
        pl.kernel, mesh=_sc_mesh(), out_type=jax.ShapeDtypeStruct((n, d), rows.dtype),
        scratch_types=[pltpu.VMEM((per_w,), I32), pltpu.VMEM((2, ch, d), rows.dtype),
                       pltpu.SemaphoreType.DMA((2,)), pltpu.SemaphoreType.DMA((2,))])
    def gather(rows_hbm, i_hbm, out_hbm, i_v, buf, sem_g, sem_w):
        base = pl.multiple_of(_sc_worker_base(per_w), SUBLANES)
        pltpu.sync_copy(i_hbm.at[pl.ds(base, per_w)], i_v)

        def fetch(ck):
            return pltpu.make_async_copy(rows_hbm.at[i_v.at[pl.ds(ck * ch, ch)]], buf.at[ck % 2], sem_g.at[ck % 2])

        def write(ck):
            return pltpu.make_async_copy(buf.at[ck % 2], out_hbm.at[pl.ds(base + ck * ch, ch)], sem_w.at[ck % 2])

        fetch(0).start()
        for ck in range(n_ck):
            fetch(ck).wait()
            if ck + 1 < n_ck:
                if ck >= 1:
                    write(ck - 1).wait()
                fetch(ck + 1).start()
            write(ck).start()
        for ck in range(max(n_ck - 2, 0), n_ck):
            write(ck).wait()

    return gather(rows, idx)


def _expert_kernel(blk_e_ref, n_used_ref, first_ref, slot_ref, next_ref, x_ref, wg_hbm, wu_hbm, wd_hbm, y_ref,
                   wgf, wuf, wdf, wgb, wub, wdb, sem, *, e_base):
    n_used = n_used_ref[0]
    bm = x_ref.shape[0] // MOE_SUB

    def weight_copies(e, slot):
        return (pltpu.make_async_copy(wg_hbm.at[e_base + e], wgf.at[slot], sem.at[slot, 0]),
                pltpu.make_async_copy(wu_hbm.at[e_base + e], wuf.at[slot], sem.at[slot, 1]),
                pltpu.make_async_copy(wd_hbm.at[e_base + e], wdf.at[slot], sem.at[slot, 2]))

    @pl.when(pl.program_id(0) == 0)
    def _():
        for cp in weight_copies(blk_e_ref[0], 0):
            cp.start()

    for j in range(MOE_SUB):
        blk = pl.program_id(0) * MOE_SUB + j
        rows = slice(j * bm, (j + 1) * bm)

        @pl.when(jnp.logical_and(blk < n_used, first_ref[blk] == 1))
        def _():
            slot = slot_ref[blk]
            for cp in weight_copies(blk_e_ref[blk], slot):
                cp.wait()
            nxt = next_ref[blk]

            @pl.when(nxt >= 0)
            def _():
                for cp in weight_copies(nxt, 1 - slot):
                    cp.start()

            wgb[...] = wgf[slot].astype(BF16)
            wub[...] = wuf[slot].astype(BF16)
            wdb[...] = wdf[slot].astype(BF16)

        @pl.when(blk < n_used)
        def _():
            x_hi, x_lo = _unpack_bf16_pairs(x_ref[rows, :])
            xb = jnp.concatenate([x_hi.astype(BF16), x_lo.astype(BF16)], axis=1)
            gt = jnp.dot(xb, wgb[...], preferred_element_type=F32)
            up = jnp.dot(xb, wub[...], preferred_element_type=F32)
            act = (gt * _sigmoid(gt) * up).astype(BF16)
            y_ref[rows, :] = _pack_bf16_pairs(jnp.dot(act, wdb[...], preferred_element_type=F32))


def _experts(x_rows, plan, w_gate, w_up, w_down, layer):
    n_rows, dp = x_rows.shape
    depth, n_e, d, de = w_gate.shape
    step_rows = MOE_SUB * MOE_BM
    assert n_rows % step_rows == 0
    any_spec = pl.BlockSpec(memory_space=pl.ANY)
    last_used = lambda i, be, nu, *_: (jnp.minimum(i, (nu[0] - 1) // MOE_SUB), 0)
    grid_spec = pltpu.PrefetchScalarGridSpec(
        num_scalar_prefetch=5,
        grid=(n_rows // step_rows,),
        in_specs=[pl.BlockSpec((step_rows, dp), last_used), any_spec, any_spec, any_spec],
        out_specs=pl.BlockSpec((step_rows, dp), last_used),
        scratch_shapes=[pltpu.VMEM((2, d, de), F32), pltpu.VMEM((2, d, de), F32), pltpu.VMEM((2, de, d), F32),
                        pltpu.VMEM((d, de), BF16), pltpu.VMEM((d, de), BF16), pltpu.VMEM((de, d), BF16),
                        pltpu.SemaphoreType.DMA((2, 3))],
    )
    return pl.pallas_call(
        functools.partial(_expert_kernel, e_base=layer * n_e),
        grid_spec=grid_spec,
        out_shape=jax.ShapeDtypeStruct((n_rows, dp), I32),
        compiler_params=_cp("arbitrary"),
        name="moe_experts",
    )(*plan, x_rows, w_gate.reshape(depth * n_e, d, de), w_up.reshape(depth * n_e, d, de),
      w_down.reshape(depth * n_e, de, d))


def _combine_kernel(x_ref, y0_ref, y1_ref, ew_ref, mod_ref, o_ref):
    d = x_ref.shape[1]
    h = d // 2
    ew = ew_ref[...]
    gate = mod_ref[0][:, 5 * d:6 * d]
    y0_hi, y0_lo = _unpack_bf16_pairs(y0_ref[...])
    y1_hi, y1_lo = _unpack_bf16_pairs(y1_ref[...])
    o_ref[:, :h] = x_ref[:, :h] + gate[:, :h] * (ew[:, 0:1] * y0_hi + ew[:, 1:2] * y1_hi)
    o_ref[:, h:] = x_ref[:, h:] + gate[:, h:] * (ew[:, 0:1] * y0_lo + ew[:, 1:2] * y1_lo)


def _combine(y01, x, ew, mod, nrows, s, b):
    d = x.shape[-1]
    nb = nrows // TR
    return pl.pallas_call(
        _combine_kernel,
        grid=(nb,),
        in_specs=[pl.BlockSpec((TR, d), lambda i: (i, 0)),
                  pl.BlockSpec((TR, d // 2), lambda i: (i, 0)),
                  pl.BlockSpec((TR, d // 2), lambda i: (i + nb, 0)),
                  pl.BlockSpec((TR, LANES), lambda i: (i, 0)),
                  _mod_spec(d, s // TR, b)],
        out_specs=pl.BlockSpec((TR, d), lambda i: (i, 0)),
        out_shape=jax.ShapeDtypeStruct((nrows, d), F32),
        compiler_params=_cp("parallel"),
        name="moe_combine",
    )(x, y01, y01, ew, mod)


def _hier_moe(x, g, mod, w_group, b_group, w_router, b_router, w_gate, w_up, w_down, layer, nrows, s, b):
    d = x.shape[-1]
    pad = LANES - N_GROUPS - N_EXPERTS
    wr = jnp.concatenate([w_group, w_router.reshape(d, N_EXPERTS), jnp.zeros((d, pad), F32)], axis=1)
    br = jnp.concatenate([b_group, b_router.reshape(N_EXPERTS), jnp.zeros((pad,), F32)]).reshape(1, LANES)
    hf, rt, ew, cnt = _router(x, g, mod, wr, br, nrows, s, b)
    pos, plan, n_rows = _dispatch_plan(rt, cnt, MOE_BM)
    x_rows = _sc_dispatch(hf, pos[0], pos[1], n_rows)
    y_rows = _experts(x_rows, plan, w_gate, w_up, w_down, layer)
    y01 = _sc_gather(y_rows, pos.reshape(-1))
    return _combine(y01, x, ew, mod, nrows, s, b)


def kernel(x, c, ctx, c_ctx, ada_w, ada_b, norm_mix_g, norm_ffn_g, rg_w_in, rg_conv_w, rg_conv_b, rg_wa, rg_ba, rg_wi, rg_bi, rg_lambda, rg_w_out, at_w_qkv, at_q_g, at_k_g, at_w_o, cm_w_in, cm_ln_g, cm_ln_b, cm_w_s, cm_b_s, cm_w_out, moe_w_group, moe_b_group, moe_w_router, moe_b_router, moe_w_gate, moe_w_up, moe_w_down):
    b, s, d = x.shape
    cl = ctx.shape[1]
    depth = ada_w.shape[0]
    n_lat = b * s
    assert b < SUBLANES and s % TR == 0 and cl % TM == 0 and (b * cl) % TR == 0 and d == RG_BLOCKS * LANES

    cin = jnp.concatenate([c, c_ctx[None, :], jnp.zeros((SUBLANES - b - 1, d), F32)], axis=0)
    mod_all = _ada_table(cin, ada_w, ada_b).reshape(depth, SUBLANES, 1, N_MOD * d)
    tok = jnp.concatenate([x.reshape(n_lat, d), ctx.reshape(b * cl, d)], axis=0)

    for l in range(depth):
        kind = l % 3
        j = l // 3
        last = l == depth - 1
        mod = mod_all[l]
        g_mix = norm_mix_g[l].reshape(1, d)
        nrows = n_lat if last else tok.shape[0]
        if kind == 0:
            tok_mix = _rglru_mixer(tok, g_mix, mod, rg_w_in[j], rg_conv_w[j], rg_conv_b[j], rg_wa[j], rg_ba[j],
                                   rg_wi[j], rg_bi[j], rg_lambda[j], rg_w_out[j], s, cl, b, nrows)
        elif kind == 1:
            tok_mix = _attention_mixer(tok, g_mix, mod, at_w_qkv[j], at_q_g[j], at_k_g[j], at_w_o[j], s, cl, b)
        else:
            tok_mix = _gmlp_mixer(tok, g_mix, mod, cm_w_in[j], cm_ln_g[j], cm_ln_b[j], cm_w_s[j], cm_b_s[j],
                                  cm_w_out[j], s, b)
        tok = _hier_moe(tok_mix, norm_ffn_g[l].reshape(1, d), mod, moe_w_group[l], moe_b_group[l],
                        moe_w_router[l], moe_b_router[l], moe_w_gate, moe_w_up, moe_w_down, l,
                        nrows, s, b)
    return tok[:n_lat].reshape(b, s, d)
```

```python
import functools

import jax
import jax.numpy as jnp
from jax import lax
from jax.experimental import pallas as pl
from jax.experimental.pallas import tpu as pltpu
from jax.experimental.pallas import tpu_sc as plsc

F32 = jnp.float32
BF16 = jnp.bfloat16
I32 = jnp.int32
U32 = jnp.uint32

NORM_EPS = 1e-6
N_MOD = 6
GRID_W = 64
RG_BLOCKS = 8
CONV_W = 4
RG_C = 8.0
HEAD_DIM = 128
N_KV_HEADS = 2
GQA_GROUP = 4
ROPE_THETA = 10000.0
CHUNK = 128
CM_GROUPS = 8
N_GROUPS = 4
EXPERTS_PER_GROUP = 8
N_EXPERTS = N_GROUPS * EXPERTS_PER_GROUP

LANES = 128
SUBLANES = 8
TM = 256
TR = 512
TG = 512
TL = 256
HALO = 8
ATT_KC = 512
LOG2E = 1.4426950408889634
MOE_BM = 256
MOE_SUB = 2
SC_CORES = 2
SC_WORKERS = 32
SC_CHUNK = 32
VMEM_LIMIT = 52 * 2**20


def _cp(*sem):
    return pltpu.CompilerParams(dimension_semantics=sem, vmem_limit_bytes=VMEM_LIMIT)


def _norm_mod(x, g, shift, scale):
    ms = jnp.mean(x * x, axis=-1, keepdims=True)
    y = x * lax.rsqrt(ms + NORM_EPS) * g
    return y * (1.0 + scale) + shift


def _sigmoid(x):
    return 0.5 * jnp.tanh(0.5 * x) + 0.5


def _pack_bf16_pairs(x):
    h = x.shape[-1] // 2
    hi = lax.bitcast_convert_type(x[:, :h].astype(BF16).astype(F32), U32)
    lo = lax.bitcast_convert_type(x[:, h:].astype(BF16).astype(F32), U32)
    return lax.bitcast_convert_type(hi | (lo >> 16), I32)


def _unpack_bf16_pairs(w):
    u = lax.bitcast_convert_type(w, U32)
    hi = lax.bitcast_convert_type(u & jnp.uint32(0xFFFF0000), F32)
    lo = lax.bitcast_convert_type(u << 16, F32)
    return hi, lo


def _combined_rows(x_ref, pre_refs):
    if not pre_refs:
        return x_ref[...]
    y0_ref, y1_ref, ew_ref, modp_ref = pre_refs
    d = x_ref.shape[1]
    ew = ew_ref[...]
    y0_hi, y0_lo = _unpack_bf16_pairs(y0_ref[...])
    y1_hi, y1_lo = _unpack_bf16_pairs(y1_ref[...])
    y = jnp.concatenate([ew[:, 0:1] * y0_hi + ew[:, 1:2] * y1_hi, ew[:, 0:1] * y0_lo + ew[:, 1:2] * y1_lo], axis=1)
    return x_ref[...] + modp_ref[0][:, 5 * d:6 * d] * y


def _pre_io(pre, d, tr, s, b):
    if pre is None:
        return [], []
    y01, ew, mod_prev = pre
    nb = y01.shape[0] // 2 // tr
    specs = [pl.BlockSpec((tr, d // 2), lambda i: (i, 0)), pl.BlockSpec((tr, d // 2), lambda i: (i + nb, 0)),
             pl.BlockSpec((tr, LANES), lambda i: (i, 0)), _mod_spec(d, s // tr, b)]
    return specs, [y01, y01, ew, mod_prev]


def _split_refs(refs, has_pre):
    return (refs[0], refs[1:5], refs[5:]) if has_pre else (refs[0], (), refs[1:])


def _mod_spec(d, rows_per_sample, n_samples):
    return pl.BlockSpec((1, 1, N_MOD * d),
                        lambda i, *_: (jnp.minimum(i // rows_per_sample, n_samples), 0, 0))


def _ada_kernel(c_ref, w_ref, b_ref, o_ref):
    cin = c_ref[...]
    act = cin * jax.nn.sigmoid(cin)
    w = w_ref[0]
    w_hi = w.astype(BF16)
    w_lo = (w - w_hi.astype(F32)).astype(BF16)
    a_hi = act.astype(BF16)
    a_lo = (act - a_hi.astype(F32)).astype(BF16)
    o_ref[0] = (jnp.dot(a_hi, w_hi, preferred_element_type=F32) + jnp.dot(a_lo, w_hi, preferred_element_type=F32)
                + jnp.dot(a_hi, w_lo, preferred_element_type=F32)) + b_ref[0]


def _ada_table(cin, ada_w, ada_b):
    depth, d, n = ada_w.shape
    tn = 2 * d
    return pl.pallas_call(
        _ada_kernel,
        grid=(depth, n // tn),
        in_specs=[pl.BlockSpec((SUBLANES, d), lambda l, j: (0, 0)),
                  pl.BlockSpec((1, d, tn), lambda l, j: (l, 0, j)),
                  pl.BlockSpec((1, 1, tn), lambda l, j: (l, 0, j))],
        out_specs=pl.BlockSpec((1, SUBLANES, tn), lambda l, j: (l, 0, j)),
        out_shape=jax.ShapeDtypeStruct((depth, SUBLANES, n), F32),
        compiler_params=_cp("parallel", "parallel"),
        name="ada_table",
    )(cin, ada_w, ada_b.reshape(depth, 1, n))


def _out_kernel(y_ref, x_ref, mod_ref, w_ref, *rest):
    route_in, (o_ref, *route_out) = rest[:4], rest[4:]
    d = x_ref.shape[-1]
    m = mod_ref[0]
    y = jnp.dot(y_ref[...].astype(BF16), w_ref[...], preferred_element_type=F32)
    x1 = x_ref[...] + m[:, 2 * d:3 * d] * y
    o_ref[...] = x1
    _route_rows(x1, m, *route_in, *route_out)


def _out_proj(y, x, mod, w, route, nrows, s, b):
    d = x.shape[-1]
    k = y.shape[-1]
    r_in, r_out, r_shape, r_scratch = _route_io(d, nrows, TR)
    x1, *routed = pl.pallas_call(
        _out_kernel,
        grid=(nrows // TR,),
        in_specs=[pl.BlockSpec((TR, k), lambda i: (i, 0)),
                  pl.BlockSpec((TR, d), lambda i: (i, 0)),
                  _mod_spec(d, s // TR, b),
                  pl.BlockSpec((k, d), lambda i: (0, 0))] + r_in,
        out_specs=[pl.BlockSpec((TR, d), lambda i: (i, 0))] + r_out,
        out_shape=[jax.ShapeDtypeStruct((nrows, d), F32)] + r_shape,
        scratch_shapes=r_scratch,
        compiler_params=_cp("arbitrary"),
        name="out_proj",
    )(y, x, mod, w, *route)
    return x1, routed


def _rg_in_kernel(*refs, has_pre):
    x_ref, pre_refs, (g_ref, mod_ref, w_ref, *outs) = _split_refs(refs, has_pre)
    gg_ref, xin_ref = outs[-2:]
    d = x_ref.shape[-1]
    m = mod_ref[0]
    x = _combined_rows(x_ref, pre_refs)
    if has_pre:
        outs[0][...] = x
    h = _norm_mod(x, g_ref[...], m[:, 0:d], m[:, d:2 * d])
    z = jnp.dot(h.astype(BF16), w_ref[...], preferred_element_type=F32)
    tm = x_ref.shape[0]
    for n in range(d // LANES):
        cols = slice(n * LANES, (n + 1) * LANES)
        gg_ref[pl.ds(n, tm, stride=SUBLANES), :] = jax.nn.gelu(z[:, cols])
        xin_ref[pl.ds(n, tm, stride=SUBLANES), :] = z[:, d + n * LANES:d + (n + 1) * LANES]


def _rg_in(x, pre, g, mod, w, s, b):
    t, d = x.shape
    assert d == SUBLANES * LANES
    pre_specs, pre_args = _pre_io(pre, d, TM, s, b)
    row = pl.BlockSpec((TM, d), lambda i: (i, 0))
    tmajor = pl.BlockSpec((TM * SUBLANES, LANES), lambda i: (i, 0))
    outs = pl.pallas_call(
        functools.partial(_rg_in_kernel, has_pre=pre is not None),
        grid=(t // TM,),
        in_specs=[row] + pre_specs + [pl.BlockSpec((1, d), lambda i: (0, 0)), _mod_spec(d, s // TM, b),
                                      pl.BlockSpec((d, 2 * d), lambda i: (0, 0))],
        out_specs=([row] if pre else []) + [tmajor, tmajor],
        out_shape=([jax.ShapeDtypeStruct((t, d), F32)] if pre else [])
        + [jax.ShapeDtypeStruct((t * SUBLANES, LANES), F32)] * 2,
        compiler_params=_cp("parallel"),
        name="rg_in",
    )(x, *pre_args, g, mod, w)
    return (outs[0], outs[1], outs[2]) if pre else (x, outs[0], outs[1])


def _rg_scan_kernel(xm_ref, xprev_ref, xnext_ref, cw_ref, cb_ref, wa_ref, wi_ref,
                    ba_ref, bi_ref, lam_ref, *rest, reverse, nlat):
    if reverse:
        hf_ref, gg_ref, out_ref, xpad, xc, a_s, b_s, h_s, hcar = rest
    else:
        out_ref, xpad, xc, a_s, b_s, hcar = rest
    rows = TL * SUBLANES
    hrows = HALO * SUBLANES
    j = pl.program_id(1)
    m = (nlat - j) if reverse else (j - 1)
    has_prev = jnp.logical_and(j >= 1, m > 0)
    has_next = jnp.logical_and(j >= 1, m < nlat - 1)

    @pl.when(j == 0)
    def _():
        hcar[...] = jnp.zeros_like(hcar)

    xpad[0:hrows, :] = jnp.where(has_prev, xprev_ref[...], 0.0)
    xpad[hrows:hrows + rows, :] = xm_ref[...]
    xpad[hrows + rows:2 * hrows + rows, :] = jnp.where(has_next, xnext_ref[...], 0.0)
    acc = jnp.broadcast_to(cb_ref[...][None], (TL, SUBLANES, LANES))
    for k in range(CONV_W):
        off = (HALO + k - CONV_W // 2) * SUBLANES
        tap = xpad[off:off + rows, :].reshape(TL, SUBLANES, LANES)
        acc = acc + tap * cw_ref[k][None]
    xc[...] = acc.reshape(rows, LANES)

    for n in range(RG_BLOCKS):
        cols = slice(n * LANES, (n + 1) * LANES)
        xn = xc[pl.ds(n, TL, stride=SUBLANES), :]
        xb = xn.astype(BF16)
        ta = jnp.tanh(jnp.dot(xb, wa_ref[n], preferred_element_type=F32) + ba_ref[:, cols])
        ti = jnp.tanh(jnp.dot(xb, wi_ref[n], preferred_element_type=F32) + bi_ref[:, cols])
        k = (-0.5 * RG_C * LOG2E) * jax.nn.softplus(-lam_ref[:, cols])
        a = jnp.exp2(k * ta + k)
        om = 1.0 - a * a
        root = jnp.where(om > 0.0, om * lax.rsqrt(om), 0.0)
        a_s[pl.ds(n, TL, stride=SUBLANES), :] = a
        b_s[pl.ds(n, TL, stride=SUBLANES), :] = root * (0.5 * xn) * (ti + 1.0)

    h_dst = h_s if reverse else out_ref

    def two_steps(p, h):
        t0 = (TL - 1 - 2 * p) if reverse else 2 * p
        t1 = (t0 - 1) if reverse else (t0 + 1)
        r0 = pl.multiple_of(t0 * SUBLANES, SUBLANES)
        r1 = pl.multiple_of(t1 * SUBLANES, SUBLANES)
        a0 = a_s[pl.ds(r0, SUBLANES), :]
        b0 = b_s[pl.ds(r0, SUBLANES), :]
        a1 = a_s[pl.ds(r1, SUBLANES), :]
        b1 = b_s[pl.ds(r1, SUBLANES), :]
        h_dst[pl.ds(r0, SUBLANES), :] = a0 * h + b0
        h2 = (a1 * a0) * h + (a1 * b0 + b1)
        h_dst[pl.ds(r1, SUBLANES), :] = h2
        return h2

    hcar[...] = lax.fori_loop(0, TL // 2, two_steps, hcar[...], unroll=8)
    if reverse:
        h_s[...] = gg_ref[...] * (hf_ref[...] + h_s[...])
        for n in range(RG_BLOCKS):
            out_ref[:, n * LANES:(n + 1) * LANES] = h_s[pl.ds(n, TL, stride=SUBLANES), :].astype(BF16)


def _rg_scan(xin8, conv_w, conv_b, wa, wi, ba, bi, lam, s, c, b, reverse, hf8=None, gg8=None):
    assert c == TL and s % TL == 0
    rows = TL * SUBLANES
    hrows = HALO * SUBLANES
    nlat = s // TL
    t = xin8.shape[0] // SUBLANES
    n_halo = t // HALO

    def chunk(bi_, j):
        lat = bi_ * nlat + ((nlat - j) if reverse else (j - 1))
        return jnp.where(j == 0, (b * s) // TL + bi_, lat)

    main = pl.BlockSpec((rows, LANES), lambda bi_, j: (chunk(bi_, j), 0))
    prev = pl.BlockSpec((hrows, LANES),
                        lambda bi_, j: (jnp.maximum(chunk(bi_, j) * (TL // HALO) - 1, 0), 0))
    nxt = pl.BlockSpec((hrows, LANES),
                       lambda bi_, j: (jnp.minimum((chunk(bi_, j) + 1) * (TL // HALO), n_halo - 1), 0))
    full = lambda shape: pl.BlockSpec(shape, lambda bi_, j: (0,) * len(shape))
    d = RG_BLOCKS * LANES
    in_specs = [main, prev, nxt, full((CONV_W, SUBLANES, LANES)), full((SUBLANES, LANES)),
                full((RG_BLOCKS, LANES, LANES)), full((RG_BLOCKS, LANES, LANES)),
                full((1, d)), full((1, d)), full((1, d))]
    args = [xin8, xin8, xin8, conv_w.reshape(CONV_W, SUBLANES, LANES), conv_b.reshape(SUBLANES, LANES),
            (0.5 * wa).astype(BF16), (0.5 * wi).astype(BF16), 0.5 * ba.reshape(1, d), 0.5 * bi.reshape(1, d),
            lam.reshape(1, d)]
    scratch = [pltpu.VMEM((rows + 2 * hrows, LANES), F32), pltpu.VMEM((rows, LANES), F32),
               pltpu.VMEM((rows, LANES), F32), pltpu.VMEM((rows, LANES), F32)]
    if reverse:
        in_specs += [main, main]
        args += [hf8, gg8]
        scratch.append(pltpu.VMEM((rows, LANES), F32))
    scratch.append(pltpu.VMEM((SUBLANES, LANES), F32))
    return pl.pallas_call(
        functools.partial(_rg_scan_kernel, reverse=reverse, nlat=nlat),
        grid=(b, nlat + 1),
        in_specs=in_specs,
        out_specs=pl.BlockSpec((TL, d), lambda bi_, j: (chunk(bi_, j), 0)) if reverse else main,
        out_shape=jax.ShapeDtypeStruct((t, d), BF16) if reverse else jax.ShapeDtypeStruct(xin8.shape, F32),
        scratch_shapes=scratch,
        compiler_params=_cp("parallel", "arbitrary"),
        name="rg_scan_bwd" if reverse else "rg_scan_fwd",
    )(*args)


def _rglru_mixer(xa, pre, g, mod, w_in, conv_w, conv_b, wa, ba, wi, bi, lam, w_out, route, s, c, b, nrows_out):
    x, gg8, xin8 = _rg_in(xa, pre, g, mod, w_in.astype(BF16), s, b)
    hf8 = _rg_scan(xin8, conv_w, conv_b, wa[0], wi[0], ba[0], bi[0], lam[0], s, c, b, False)
    y = _rg_scan(xin8, conv_w, conv_b, wa[1], wi[1], ba[1], bi[1], lam[1], s, c, b, True, hf8, gg8)
    return _out_proj(y, x, mod, w_out.astype(BF16), route, nrows_out, s, b)


def _rope_tables(s):
    pos = jnp.arange(s, dtype=F32)
    row = jnp.floor(pos / GRID_W)
    col = pos - row * GRID_W
    n_freq = HEAD_DIM // 4
    inv = ROPE_THETA ** (-jnp.arange(n_freq, dtype=F32) * 2.0 / (HEAD_DIM // 2))
    ar = row[:, None] * inv
    ac = col[:, None] * inv
    cos = jnp.concatenate([jnp.cos(ar), jnp.cos(ar), jnp.cos(ac), jnp.cos(ac)], axis=1)
    sin = jnp.concatenate([-jnp.sin(ar), jnp.sin(ar), -jnp.sin(ac), jnp.sin(ac)], axis=1)
    cos = jnp.concatenate([cos, jnp.ones((TM, HEAD_DIM), F32)], axis=0)
    sin = jnp.concatenate([sin, jnp.zeros((TM, HEAD_DIM), F32)], axis=0)
    return cos, sin


def _qkv_kernel(*refs, has_pre):
    x_ref, pre_refs, (g_ref, mod_ref, w_ref, qg_ref, kg_ref, cos_ref, sin_ref, *outs) = _split_refs(refs, has_pre)
    q_ref, k_ref, v_ref = outs[-3:]
    d = x_ref.shape[-1]
    m = mod_ref[0]
    x = _combined_rows(x_ref, pre_refs)
    if has_pre:
        outs[0][...] = x
    h = _norm_mod(x, g_ref[...], m[:, 0:d], m[:, d:2 * d])
    z = jnp.dot(h.astype(BF16), w_ref[...], preferred_element_type=F32)
    cos = cos_ref[...]
    sin = sin_ref[...]
    lane = lax.broadcasted_iota(I32, cos.shape, 1)
    first_half = (lane % (HEAD_DIM // 2)) < (HEAD_DIM // 4)

    def head(zc, gain):
        ms = jnp.mean(zc * zc, axis=-1, keepdims=True)
        y = zc * lax.rsqrt(ms + NORM_EPS) * gain
        partner = jnp.where(first_half, pltpu.roll(y, HEAD_DIM - HEAD_DIM // 4, 1),
                            pltpu.roll(y, HEAD_DIM // 4, 1))
        return y * cos + partner * sin

    nq = q_ref.shape[-1] // HEAD_DIM
    nk = k_ref.shape[-1] // HEAD_DIM
    for j in range(nq):
        q_ref[:, j * HEAD_DIM:(j + 1) * HEAD_DIM] = (
            head(z[:, j * HEAD_DIM:(j + 1) * HEAD_DIM], qg_ref[...]) * (HEAD_DIM ** -0.5 * LOG2E)).astype(BF16)
    for j in range(nk):
        c0 = (nq + j) * HEAD_DIM
        k_ref[:, j * HEAD_DIM:(j + 1) * HEAD_DIM] = head(z[:, c0:c0 + HEAD_DIM], kg_ref[...]).astype(BF16)
    v_ref[...] = z[:, (nq + nk) * HEAD_DIM:].astype(BF16)


def _qkv(x, pre, g, mod, w, qg, kg, cos, sin, s, b):
    t, d = x.shape
    nkv = N_KV_HEADS * HEAD_DIM
    n_pos = s // TM
    pre_specs, pre_args = _pre_io(pre, d, TM, s, b)
    row = pl.BlockSpec((TM, d), lambda i: (i, 0))
    outs = pl.pallas_call(
        functools.partial(_qkv_kernel, has_pre=pre is not None),
        grid=(t // TM,),
        in_specs=[row] + pre_specs + [
            pl.BlockSpec((1, d), lambda i: (0, 0)),
            _mod_spec(d, s // TM, b),
            pl.BlockSpec(w.shape, lambda i: (0, 0)),
            pl.BlockSpec((1, HEAD_DIM), lambda i: (0, 0)),
            pl.BlockSpec((1, HEAD_DIM), lambda i: (0, 0)),
            pl.BlockSpec((TM, HEAD_DIM), lambda i: (jnp.where(i < b * n_pos, i % n_pos, n_pos), 0)),
            pl.BlockSpec((TM, HEAD_DIM), lambda i: (jnp.where(i < b * n_pos, i % n_pos, n_pos), 0))],
        out_specs=([row] if pre else []) + [pl.BlockSpec((TM, d), lambda i: (i, 0)),
                                           pl.BlockSpec((TM, nkv), lambda i: (i, 0)),
                                           pl.BlockSpec((TM, nkv), lambda i: (i, 0))],
        out_shape=([jax.ShapeDtypeStruct((t, d), F32)] if pre else [])
        + [jax.ShapeDtypeStruct((t, d), BF16), jax.ShapeDtypeStruct((t, nkv), BF16),
           jax.ShapeDtypeStruct((t, nkv), BF16)],
        compiler_params=_cp("parallel"),
        name="qkv_proj",
    )(x, *pre_args, g, mod, w, qg, kg, cos, sin)
    return tuple(outs) if pre else (x, *outs)


def _attn_kernel(q_ref, kc_ref, vc_ref, *rest, n_lat):
    if n_lat:
        kl_ref, vl_ref, o_ref, s_scr, vaug = rest
    else:
        o_ref, s_scr, vaug = rest
    n_ctx = kc_ref.shape[0]
    tq = q_ref.shape[0]

    def fill_values():
        vaug[:, HEAD_DIM:] = jnp.ones((n_ctx + n_lat, HEAD_DIM), BF16)
        vaug[0:n_ctx, 0:HEAD_DIM] = vc_ref[...]
        if n_lat:
            vaug[n_ctx:, 0:HEAD_DIM] = vl_ref[...]

    if n_lat:
        pl.when(pl.program_id(2) == 0)(fill_values)
    else:
        fill_values()

    chunks = [(0, n_ctx)] + [(n_ctx + j, ATT_KC) for j in range(0, n_lat, ATT_KC)]
    nt = (((1,), (1,)), ((), ()))
    q_all = jnp.concatenate([q_ref[:, g * HEAD_DIM:(g + 1) * HEAD_DIM] for g in range(GQA_GROUP)], axis=0)
    m_part = jnp.full((GQA_GROUP * tq, LANES), -jnp.inf, F32)
    for off, size in chunks:
        keys = kc_ref[...] if off == 0 else kl_ref[off - n_ctx:off - n_ctx + size, :]
        sc = lax.dot_general(q_all, keys, nt, preferred_element_type=F32)
        s_scr[:, off:off + size] = sc
        for j in range(0, size, LANES):
            m_part = jnp.maximum(m_part, sc[:, j:j + LANES])
    m_row = jnp.max(m_part, axis=-1, keepdims=True)
    hr = GQA_GROUP * tq // 2
    acc = [jnp.zeros((hr, 2 * HEAD_DIM), F32), jnp.zeros((hr, 2 * HEAD_DIM), F32)]
    for off, size in chunks:
        for r in range(2):
            rows = slice(r * hr, (r + 1) * hr)
            p = jnp.exp2((s_scr[rows, off:off + size] - m_row[rows]).astype(BF16))
            acc[r] = acc[r] + jnp.dot(p, vaug[off:off + size, :], preferred_element_type=F32)
    for r in range(2):
        out = (acc[r][:, :HEAD_DIM] / acc[r][:, HEAD_DIM:]).astype(BF16)
        for j in range(GQA_GROUP // 2):
            g = r * (GQA_GROUP // 2) + j
            o_ref[:, g * HEAD_DIM:(g + 1) * HEAD_DIM] = out[j * tq:(j + 1) * tq]


def _attn_ctx_kernel(q_ref, kc_ref, vc_ref, o_all_ref, o_ref, s_scr, vaug):
    del o_all_ref
    _attn_kernel(q_ref, kc_ref, vc_ref, o_ref, s_scr, vaug, n_lat=0)


def _attention(q, k, v, s, c, b):
    t, d = q.shape
    gw = GQA_GROUP * HEAD_DIM
    tq = TM
    nq = s // tq
    assert s % ATT_KC == 0
    ctx_blk = lambda bi, h, *_: ((b * s) // c + bi, h)
    o_lat = pl.pallas_call(
        functools.partial(_attn_kernel, n_lat=s),
        grid=(b, N_KV_HEADS, nq),
        in_specs=[pl.BlockSpec((tq, gw), lambda bi, h, i: (bi * nq + i, h)),
                  pl.BlockSpec((c, HEAD_DIM), ctx_blk),
                  pl.BlockSpec((c, HEAD_DIM), ctx_blk),
                  pl.BlockSpec((s, HEAD_DIM), lambda bi, h, i: (bi, h)),
                  pl.BlockSpec((s, HEAD_DIM), lambda bi, h, i: (bi, h))],
        out_specs=pl.BlockSpec((tq, gw), lambda bi, h, i: (bi * nq + i, h)),
        out_shape=jax.ShapeDtypeStruct((t, d), BF16),
        scratch_shapes=[pltpu.VMEM((GQA_GROUP * tq, c + s), F32), pltpu.VMEM((c + s, 2 * HEAD_DIM), BF16)],
        compiler_params=_cp("parallel", "parallel", "arbitrary"),
        name="attn_lat",
    )(q, k, v, k, v)
    return pl.pallas_call(
        _attn_ctx_kernel,
        grid=(b, N_KV_HEADS),
        in_specs=[pl.BlockSpec((c, gw), ctx_blk),
                  pl.BlockSpec((c, HEAD_DIM), ctx_blk),
                  pl.BlockSpec((c, HEAD_DIM), ctx_blk),
                  pl.BlockSpec(memory_space=pl.ANY)],
        out_specs=pl.BlockSpec((c, gw), ctx_blk),
        out_shape=jax.ShapeDtypeStruct((t, d), BF16),
        scratch_shapes=[pltpu.VMEM((GQA_GROUP * c, c), F32), pltpu.VMEM((c, 2 * HEAD_DIM), BF16)],
        input_output_aliases={3: 0},
        compiler_params=_cp("parallel", "parallel"),
        name="attn_ctx",
    )(q, k, v, o_lat)


def _attention_mixer(xa, pre, g, mod, w_qkv, qg, kg, w_o, route, s, c, b):
    cos, sin = _rope_tables(s)
    x, q, k, v = _qkv(xa, pre, g, mod, w_qkv.astype(BF16), qg.reshape(1, -1), kg.reshape(1, -1), cos, sin, s, b)
    o = _attention(q, k, v, s, c, b)
    return _out_proj(o, x, mod, w_o.astype(BF16), route, x.shape[0], s, b)


def _gmlp_kernel(*refs, has_pre):
    x_ref, pre_refs, rest = _split_refs(refs, has_pre)
    g_ref, mod_ref, w_in_ref, lng_ref, lnb_ref, ws_ref, bs_ref, w_out_ref = rest[:8]
    route_in, (o_ref, *route_out, uv_ref, cnt_s) = rest[8:12], rest[12:]
    d = x_ref.shape[-1]
    dcm = lng_ref.shape[-1]
    gw = dcm // CM_GROUPS
    x = _combined_rows(x_ref, pre_refs)
    m = mod_ref[0]
    h = _norm_mod(x, g_ref[...], m[:, 0:d], m[:, d:2 * d])
    z = jax.nn.gelu(jnp.dot(h.astype(BF16), w_in_ref[...], preferred_element_type=F32))
    u = z[:, :dcm]
    v = z[:, dcm:]
    mu = jnp.mean(v, axis=-1, keepdims=True)
    vc = v - mu
    var = jnp.mean(vc * vc, axis=-1, keepdims=True)
    vn = (vc * lax.rsqrt(var + NORM_EPS) * lng_ref[...] + lnb_ref[...]).astype(BF16)
    for ck in range(x.shape[0] // CHUNK):
        rows = slice(ck * CHUNK, (ck + 1) * CHUNK)
        for gi in range(CM_GROUPS):
            cols = slice(gi * gw, (gi + 1) * gw)
            mix = jnp.dot(ws_ref[gi], vn[rows, cols], preferred_element_type=F32) + bs_ref[:, gi:gi + 1]
            uv_ref[rows, cols] = (u[rows, cols] * mix).astype(BF16)
    y = jnp.dot(uv_ref[...], w_out_ref[...], preferred_element_type=F32)
    x1 = x + m[:, 2 * d:3 * d] * y
    o_ref[...] = x1
    _route_rows(x1, m, *route_in, *route_out, cnt_s)


def _gmlp_mixer(xa, pre, g, mod, w_in, ln_g, ln_b, w_s, b_s, w_out, route, s, b):
    t, d = xa.shape
    dcm = ln_g.shape[-1]
    full = lambda shape: pl.BlockSpec(shape, lambda i: (0,) * len(shape))
    pre_specs, pre_args = _pre_io(pre, d, TG, s, b)
    r_in, r_out, r_shape, r_scratch = _route_io(d, t, TG)
    x1, *routed = pl.pallas_call(
        functools.partial(_gmlp_kernel, has_pre=pre is not None),
        grid=(t // TG,),
        in_specs=[pl.BlockSpec((TG, d), lambda i: (i, 0))] + pre_specs + [
            full((1, d)),
            _mod_spec(d, s // TG, b),
            full((d, 2 * dcm)), full((1, dcm)), full((1, dcm)),
            full((CM_GROUPS, CHUNK, CHUNK)), full((CHUNK, CM_GROUPS)), full((dcm, d))] + r_in,
        out_specs=[pl.BlockSpec((TG, d), lambda i: (i, 0))] + r_out,
        out_shape=[jax.ShapeDtypeStruct((t, d), F32)] + r_shape,
        scratch_shapes=[pltpu.VMEM((TG, dcm), BF16)] + r_scratch,
        compiler_params=_cp("arbitrary"),
        name="gmlp",
    )(xa, *pre_args, g, mod, w_in.astype(BF16), ln_g.reshape(1, dcm), ln_b.reshape(1, dcm),
      w_s.astype(BF16), b_s.T, w_out.astype(BF16), *route)
    return x1, routed


def _route_rows(x1, m, g_ref, wrh_ref, wrl_ref, br_ref, hf_ref, rt_ref, ew_ref, cnt_ref, cnt_s):
    tm, d = x1.shape

    @pl.when(pl.program_id(0) == 0)
    def _():
        cnt_s[...] = jnp.zeros_like(cnt_s)

    hf = _norm_mod(x1, g_ref[...], m[:, 3 * d:4 * d], m[:, 4 * d:5 * d])
    hf_ref[...] = _pack_bf16_pairs(hf)
    hf_hi = hf.astype(BF16)
    hf_lo = (hf - hf_hi.astype(F32)).astype(BF16)
    logits = (jnp.dot(hf_hi, wrh_ref[...], preferred_element_type=F32)
              + jnp.dot(hf_lo, wrh_ref[...], preferred_element_type=F32)
              + jnp.dot(hf_hi, wrl_ref[...], preferred_element_type=F32)) + br_ref[...]
    lane = lax.broadcasted_iota(I32, logits.shape, 1)
    neg = -jnp.inf
    gl = jnp.where(lane < N_GROUPS, logits, neg)
    gmax = jnp.max(gl, axis=-1, keepdims=True)
    gsel = jnp.min(jnp.where(gl == gmax, lane, LANES), axis=-1, keepdims=True)
    gate_g = 1.0 / jnp.sum(jnp.exp(gl - gmax), axis=-1, keepdims=True)
    lo = N_GROUPS + gsel * EXPERTS_PER_GROUP
    el = jnp.where(jnp.logical_and(lane >= lo, lane < lo + EXPERTS_PER_GROUP), logits, neg)
    v1 = jnp.max(el, axis=-1, keepdims=True)
    i1 = jnp.min(jnp.where(el == v1, lane, LANES), axis=-1, keepdims=True)
    el2 = jnp.where(lane == i1, neg, el)
    v2 = jnp.max(el2, axis=-1, keepdims=True)
    i2 = jnp.min(jnp.where(el2 == v2, lane, LANES), axis=-1, keepdims=True)
    e21 = jnp.exp(v2 - v1)
    w1 = gate_g / (1.0 + e21)
    w2 = w1 * e21
    ew_ref[...] = jnp.where(lane == 0, w1, jnp.where(lane == 1, w2, 0.0))

    oh1 = lane == i1
    oh2 = lane == i2
    above = (lax.broadcasted_iota(I32, (tm, tm), 1) < lax.broadcasted_iota(I32, (tm, tm), 0)).astype(BF16)
    pre1 = jnp.dot(above, oh1.astype(BF16), preferred_element_type=F32)
    pre2 = jnp.dot(above, oh2.astype(BF16), preferred_element_type=F32)
    tot1 = jnp.sum(oh1.astype(F32), axis=0, keepdims=True)
    tot2 = jnp.sum(oh2.astype(F32), axis=0, keepdims=True)
    cnt = cnt_s[...]
    rank1 = jnp.sum(jnp.where(oh1, cnt + pre1, 0.0), axis=-1, keepdims=True).astype(I32)
    rank2 = jnp.sum(jnp.where(oh2, cnt + tot1 + pre2, 0.0), axis=-1, keepdims=True).astype(I32)
    cnt = cnt + tot1 + tot2
    cnt_s[...] = cnt
    cnt_ref[...] = jnp.broadcast_to(cnt, cnt_ref.shape)
    rt = jnp.where(lane == 0, i1 - N_GROUPS, jnp.where(lane == 1, i2 - N_GROUPS,
                   jnp.where(lane == 2, rank1, jnp.where(lane == 3, rank2, 0))))
    rt_ref[...] = rt.T[:SUBLANES]


def _route_params(g_ffn, w_group, b_group, w_router, b_router):
    d = w_group.shape[0]
    pad = LANES - N_GROUPS - N_EXPERTS
    wr = jnp.concatenate([w_group, w_router.reshape(d, N_EXPERTS), jnp.zeros((d, pad), F32)], axis=1)
    br = jnp.concatenate([b_group, b_router.reshape(N_EXPERTS), jnp.zeros((pad,), F32)]).reshape(1, LANES)
    wr_hi = wr.astype(BF16)
    wr_lo = (wr - wr_hi.astype(F32)).astype(BF16)
    return g_ffn.reshape(1, d), wr_hi, wr_lo, br


def _route_io(d, nrows, tr):
    const = lambda shape: pl.BlockSpec(shape, lambda i: (0, 0))
    row = lambda w: pl.BlockSpec((tr, w), lambda i: (i, 0))
    in_specs = [const((1, d)), const((d, LANES)), const((d, LANES)), const((1, LANES))]
    out_specs = [row(d // 2), pl.BlockSpec((SUBLANES, tr), lambda i: (0, i)), row(LANES), const((SUBLANES, LANES))]
    out_shape = [jax.ShapeDtypeStruct((nrows, d // 2), I32), jax.ShapeDtypeStruct((SUBLANES, nrows), I32),
                 jax.ShapeDtypeStruct((nrows, LANES), F32), jax.ShapeDtypeStruct((SUBLANES, LANES), F32)]
    return in_specs, out_specs, out_shape, [pltpu.VMEM((1, LANES), F32)]


def _dispatch_plan(rt, cnt, bm):
    n_tok = rt.shape[1]
    counts = cnt[0, N_GROUPS:N_GROUPS + N_EXPERTS].astype(I32)
    padded = (counts + bm - 1) // bm * bm
    pad_end = jnp.cumsum(padded)
    pad_start = pad_end - padded
    experts = jnp.arange(N_EXPERTS, dtype=I32)
    start_of = jnp.zeros_like(rt[0:2])
    for e in range(N_EXPERTS):
        start_of = jnp.where(rt[0:2] == e, pad_start[e], start_of)
    pos = (start_of + rt[2:4]).astype(I32)
    n_rows = 2 * n_tok + N_EXPERTS * bm
    n_used = (pad_end[-1] // bm).astype(I32)
    blk = jnp.arange(n_rows // bm, dtype=I32)
    first_row = jnp.minimum(blk, n_used - 1) * bm
    blk_e = jnp.sum((pad_end[None, :] <= first_row[:, None]).astype(I32), axis=1)
    blk_e = jnp.minimum(blk_e, N_EXPERTS - 1).astype(I32)
    run_first = jnp.logical_and(blk < n_used, jnp.concatenate([jnp.ones((1,), bool), blk_e[1:] != blk_e[:-1]]))
    run_slot = (jnp.cumsum(run_first.astype(I32)) - 1) % 2
    later = jnp.where(counts > 0, experts, N_EXPERTS)
    next_present = lax.cummin(jnp.concatenate([later[1:], jnp.full((1,), N_EXPERTS, I32)]), reverse=True)
    next_e = jnp.where(next_present < N_EXPERTS, next_present, -1)[blk_e]
    plan = (blk_e, n_used.reshape(1), run_first.astype(I32), run_slot.astype(I32), next_e.astype(I32))
    return pos, plan, n_rows


def _sc_mesh():
    return plsc.VectorSubcoreMesh(core_axis_name="c", subcore_axis_name="s")


def _sc_worker_base(per_worker):
    return (lax.axis_index("s") * SC_CORES + lax.axis_index("c")) * per_worker


def _sc_dispatch(hf, pos0, pos1, n_rows):
    t, d = hf.shape
    per_w = t // SC_WORKERS
    ch = SC_CHUNK
    n_ck = per_w // ch
    assert per_w * SC_WORKERS == t and n_ck * ch == per_w

    @functools.partial(
        pl.kernel, mesh=_sc_mesh(), out_type=jax.ShapeDtypeStruct((n_rows, d), hf.dtype),
        scratch_types=[pltpu.VMEM((per_w,), I32), pltpu.VMEM((per_w,), I32), pltpu.VMEM((2, ch, d), hf.dtype),
                       pltpu.SemaphoreType.DMA((2,)), pltpu.SemaphoreType.DMA((2,)), pltpu.SemaphoreType.DMA((2,))])
    def dispatch(hf_hbm, p0_hbm, p1_hbm, out_hbm, i0_v, i1_v, rows_v, sem_in, sem_s0, sem_s1):
        base = pl.multiple_of(_sc_worker_base(per_w), SUBLANES)
        pltpu.sync_copy(p0_hbm.at[pl.ds(base, per_w)], i0_v)
        pltpu.sync_copy(p1_hbm.at[pl.ds(base, per_w)], i1_v)

        def load(ck):
            return pltpu.make_async_copy(hf_hbm.at[pl.ds(base + ck * ch, ch)], rows_v.at[ck % 2], sem_in.at[ck % 2])

        def scatters(ck):
            src = rows_v.at[ck % 2]
            return (pltpu.make_async_copy(src, out_hbm.at[i0_v.at[pl.ds(ck * ch, ch)]], sem_s0.at[ck % 2]),
                    pltpu.make_async_copy(src, out_hbm.at[i1_v.at[pl.ds(ck * ch, ch)]], sem_s1.at[ck % 2]))

        load(0).start()
        for ck in range(n_ck):
            load(ck).wait()
            if ck + 1 < n_ck:
                if ck >= 1:
                    for cp in scatters(ck - 1):
                        cp.wait()
                load(ck + 1).start()
            for cp in scatters(ck):
                cp.start()
        for ck in range(max(n_ck - 2, 0), n_ck):
            for cp in scatters(ck):
                cp.wait()

    return dispatch(hf, pos0, pos1)


def _sc_gather(rows, idx):
    n = idx.shape[0]
    d = rows.shape[1]
    per_w = n // SC_WORKERS
    ch = 2 * SC_CHUNK
    n_ck = per_w // ch
    assert per_w * SC_WORKERS == n and n_ck * ch == per_w

    @functools.partial(
        pl.kernel, mesh=_sc_mesh(), out_type=jax.ShapeDtypeStruct((n, d), rows.dtype),
        scratch_types=[pltpu.VMEM((per_w,), I32), pltpu.VMEM((2, ch, d), rows.dtype),
                       pltpu.SemaphoreType.DMA((2,)), pltpu.SemaphoreType.DMA((2,))])
    def gather(rows_hbm, i_hbm, out_hbm, i_v, buf, sem_g, sem_w):
        base = pl.multiple_of(_sc_worker_base(per_w), SUBLANES)
        pltpu.sync_copy(i_hbm.at[pl.ds(base, per_w)], i_v)

        def fetch(ck):
            return pltpu.make_async_copy(rows_hbm.at[i_v.at[pl.ds(ck * ch, ch)]], buf.at[ck % 2], sem_g.at[ck % 2])

        def write(ck):
            return pltpu.make_async_copy(buf.at[ck % 2], out_hbm.at[pl.ds(base + ck * ch, ch)], sem_w.at[ck % 2])

        fetch(0).start()
        for ck in range(n_ck):
            fetch(ck).wait()
            if ck + 1 < n_ck:
                if ck >= 1:
                    write(ck - 1).wait()
                fetch(ck + 1).start()
            write(ck).start()
        for ck in range(max(n_ck - 2, 0), n_ck):
            write(ck).wait()

    return gather(rows, idx)


def _expert_kernel(blk_e_ref, n_used_ref, first_ref, slot_ref, next_ref, x_ref, wg_hbm, wu_hbm, wd_hbm, y_ref,
                   wgf, wuf, wdf, wgb, wub, wdb, sem, *, e_base):
    n_used = n_used_ref[0]
    bm = x_ref.shape[0] // MOE_SUB

    def weight_copies(e, slot):
        return (pltpu.make_async_copy(wg_hbm.at[e_base + e], wgf.at[slot], sem.at[slot, 0]),
                pltpu.make_async_copy(wu_hbm.at[e_base + e], wuf.at[slot], sem.at[slot, 1]),
                pltpu.make_async_copy(wd_hbm.at[e_base + e], wdf.at[slot], sem.at[slot, 2]))

    @pl.when(pl.program_id(0) == 0)
    def _():
        for cp in weight_copies(blk_e_ref[0], 0):
            cp.start()

    for j in range(MOE_SUB):
        blk = pl.program_id(0) * MOE_SUB + j
        rows = slice(j * bm, (j + 1) * bm)

        @pl.when(jnp.logical_and(blk < n_used, first_ref[blk] == 1))
        def _():
            slot = slot_ref[blk]
            for cp in weight_copies(blk_e_ref[blk], slot):
                cp.wait()
            nxt = next_ref[blk]

            @pl.when(nxt >= 0)
            def _():
                for cp in weight_copies(nxt, 1 - slot):
                    cp.start()

            wgb[...] = wgf[slot].astype(BF16)
            wub[...] = wuf[slot].astype(BF16)
            wdb[...] = wdf[slot].astype(BF16)

        @pl.when(blk < n_used)
        def _():
            x_hi, x_lo = _unpack_bf16_pairs(x_ref[rows, :])
            xb = jnp.concatenate([x_hi.astype(BF16), x_lo.astype(BF16)], axis=1)
            gt = jnp.dot(xb, wgb[...], preferred_element_type=F32)
            up = jnp.dot(xb, wub[...], preferred_element_type=F32)
            act = (gt * _sigmoid(gt) * up).astype(BF16)
            y_ref[rows, :] = _pack_bf16_pairs(jnp.dot(act, wdb[...], preferred_element_type=F32))


def _experts(x_rows, plan, w_gate, w_up, w_down, layer):
    n_rows, dp = x_rows.shape
    depth, n_e, d, de = w_gate.shape
    step_rows = MOE_SUB * MOE_BM
    assert n_rows % step_rows == 0
    any_spec = pl.BlockSpec(memory_space=pl.ANY)
    last_used = lambda i, be, nu, *_: (jnp.minimum(i, (nu[0] - 1) // MOE_SUB), 0)
    grid_spec = pltpu.PrefetchScalarGridSpec(
        num_scalar_prefetch=5,
        grid=(n_rows // step_rows,),
        in_specs=[pl.BlockSpec((step_rows, dp), last_used), any_spec, any_spec, any_spec],
        out_specs=pl.BlockSpec((step_rows, dp), last_used),
        scratch_shapes=[pltpu.VMEM((2, d, de), F32), pltpu.VMEM((2, d, de), F32), pltpu.VMEM((2, de, d), F32),
                        pltpu.VMEM((d, de), BF16), pltpu.VMEM((d, de), BF16), pltpu.VMEM((de, d), BF16),
                        pltpu.SemaphoreType.DMA((2, 3))],
    )
    return pl.pallas_call(
        functools.partial(_expert_kernel, e_base=layer * n_e),
        grid_spec=grid_spec,
        out_shape=jax.ShapeDtypeStruct((n_rows, dp), I32),
        compiler_params=_cp("arbitrary"),
        name="moe_experts",
    )(*plan, x_rows, w_gate.reshape(depth * n_e, d, de), w_up.reshape(depth * n_e, d, de),
      w_down.reshape(depth * n_e, de, d))


def _combine_kernel(x_ref, *rest):
    *pre_refs, o_ref = rest
    o_ref[...] = _combined_rows(x_ref, pre_refs)


def _combine(x, pre, s, b):
    nrows, d = x.shape
    pre_specs, pre_args = _pre_io(pre, d, TR, s, b)
    return pl.pallas_call(
        _combine_kernel,
        grid=(nrows // TR,),
        in_specs=[pl.BlockSpec((TR, d), lambda i: (i, 0))] + pre_specs,
        out_specs=pl.BlockSpec((TR, d), lambda i: (i, 0)),
        out_shape=jax.ShapeDtypeStruct((nrows, d), F32),
        compiler_params=_cp("parallel"),
        name="moe_combine",
    )(x, *pre_args)


def _expert_outputs(routed, w_gate, w_up, w_down, layer):
    hf, rt, _, cnt = routed
    pos, plan, n_rows = _dispatch_plan(rt, cnt, MOE_BM)
    x_rows = _sc_dispatch(hf, pos[0], pos[1], n_rows)
    y_rows = _experts(x_rows, plan, w_gate, w_up, w_down, layer)
    return _sc_gather(y_rows, pos.reshape(-1))


def kernel(x, c, ctx, c_ctx, ada_w, ada_b, norm_mix_g, norm_ffn_g, rg_w_in, rg_conv_w, rg_conv_b, rg_wa, rg_ba, rg_wi, rg_bi, rg_lambda, rg_w_out, at_w_qkv, at_q_g, at_k_g, at_w_o, cm_w_in, cm_ln_g, cm_ln_b, cm_w_s, cm_b_s, cm_w_out, moe_w_group, moe_b_group, moe_w_router, moe_b_router, moe_w_gate, moe_w_up, moe_w_down):
    b, s, d = x.shape
    cl = ctx.shape[1]
    depth = ada_w.shape[0]
    n_lat = b * s
    assert b < SUBLANES and s % TR == 0 and cl % TM == 0 and (b * cl) % TR == 0 and d == RG_BLOCKS * LANES

    cin = jnp.concatenate([c, c_ctx[None, :], jnp.zeros((SUBLANES - b - 1, d), F32)], axis=0)
    mod_all = _ada_table(cin, ada_w, ada_b).reshape(depth, SUBLANES, 1, N_MOD * d)
    tok = jnp.concatenate([x.reshape(n_lat, d), ctx.reshape(b * cl, d)], axis=0)

    xa, pre = tok, None
    for l in range(depth):
        kind = l % 3
        j = l // 3
        last = l == depth - 1
        mod = mod_all[l]
        g_mix = norm_mix_g[l].reshape(1, d)
        route = _route_params(norm_ffn_g[l], moe_w_group[l], moe_b_group[l], moe_w_router[l], moe_b_router[l])
        if kind == 0:
            x1, routed = _rglru_mixer(xa, pre, g_mix, mod, rg_w_in[j], rg_conv_w[j], rg_conv_b[j], rg_wa[j],
                                      rg_ba[j], rg_wi[j], rg_bi[j], rg_lambda[j], rg_w_out[j], route, s, cl, b,
                                      n_lat if last else xa.shape[0])
        elif kind == 1:
            x1, routed = _attention_mixer(xa, pre, g_mix, mod, at_w_qkv[j], at_q_g[j], at_k_g[j], at_w_o[j],
                                          route, s, cl, b)
        else:
            x1, routed = _gmlp_mixer(xa, pre, g_mix, mod, cm_w_in[j], cm_ln_g[j], cm_ln_b[j], cm_w_s[j],
                                     cm_b_s[j], cm_w_out[j], route, s, b)
        y01 = _expert_outputs(routed, moe_w_gate, moe_w_up, moe_w_down, l)
        xa, pre = x1, (y01, routed[2], mod)
    return _combine(xa, pre, s, b)[:n_lat].reshape(b, s, d)
```

```python
import functools

import jax
import jax.numpy as jnp
from jax import lax
from jax.experimental import pallas as pl
from jax.experimental.pallas import tpu as pltpu
from jax.experimental.pallas import tpu_sc as plsc

F32 = jnp.float32
BF16 = jnp.bfloat16
I32 = jnp.int32
U32 = jnp.uint32

NORM_EPS = 1e-6
N_MOD = 6
GRID_W = 64
RG_BLOCKS = 8
CONV_W = 4
RG_C = 8.0
HEAD_DIM = 128
N_KV_HEADS = 2
GQA_GROUP = 4
ROPE_THETA = 10000.0
CHUNK = 128
CM_GROUPS = 8
N_GROUPS = 4
EXPERTS_PER_GROUP = 8
N_EXPERTS = N_GROUPS * EXPERTS_PER_GROUP

LANES = 128
SUBLANES = 8
TM = 256
TR = 512
TG = 512
TL = 256
HALO = 8
ATT_KC = 512
LOG2E = 1.4426950408889634
MOE_BM = 256
MOE_SUB = 2
SC_CORES = 2
SC_WORKERS = 32
SC_CHUNK = 32
VMEM_LIMIT = 52 * 2**20


def _cp(*sem):
    return pltpu.CompilerParams(dimension_semantics=sem, vmem_limit_bytes=VMEM_LIMIT)


def _norm_mod(x, g, shift, scale):
    ms = jnp.mean(x * x, axis=-1, keepdims=True)
    y = x * lax.rsqrt(ms + NORM_EPS) * g
    return y * (1.0 + scale) + shift


def _sigmoid(x):
    return 0.5 * jnp.tanh(0.5 * x) + 0.5


def _pack_bf16_pairs(x):
    h = x.shape[-1] // 2
    hi = lax.bitcast_convert_type(x[:, :h].astype(BF16).astype(F32), U32)
    lo = lax.bitcast_convert_type(x[:, h:].astype(BF16).astype(F32), U32)
    return lax.bitcast_convert_type(hi | (lo >> 16), I32)


def _unpack_bf16_pairs(w):
    u = lax.bitcast_convert_type(w, U32)
    hi = lax.bitcast_convert_type(u & jnp.uint32(0xFFFF0000), F32)
    lo = lax.bitcast_convert_type(u << 16, F32)
    return hi, lo


def _combined_rows(x_ref, pre_refs):
    if not pre_refs:
        return x_ref[...]
    y0_ref, y1_ref, ew_ref, modp_ref = pre_refs
    d = x_ref.shape[1]
    ew = ew_ref[...]
    y0_hi, y0_lo = _unpack_bf16_pairs(y0_ref[...])
    y1_hi, y1_lo = _unpack_bf16_pairs(y1_ref[...])
    y = jnp.concatenate([ew[:, 0:1] * y0_hi + ew[:, 1:2] * y1_hi, ew[:, 0:1] * y0_lo + ew[:, 1:2] * y1_lo], axis=1)
    return x_ref[...] + modp_ref[0][:, 5 * d:6 * d] * y


def _pre_io(pre, d, tr, s, b):
    if pre is None:
        return [], []
    y01, ew, mod_prev = pre
    nb = y01.shape[0] // 2 // tr
    specs = [pl.BlockSpec((tr, d // 2), lambda i: (i, 0)), pl.BlockSpec((tr, d // 2), lambda i: (i + nb, 0)),
             pl.BlockSpec((tr, LANES), lambda i: (i, 0)), _mod_spec(d, s // tr, b)]
    return specs, [y01, y01, ew, mod_prev]


def _split_refs(refs, has_pre):
    return (refs[0], refs[1:5], refs[5:]) if has_pre else (refs[0], (), refs[1:])


def _mod_spec(d, rows_per_sample, n_samples):
    return pl.BlockSpec((1, 1, N_MOD * d),
                        lambda i, *_: (jnp.minimum(i // rows_per_sample, n_samples), 0, 0))


def _ada_kernel(c_ref, w_ref, b_ref, o_ref):
    cin = c_ref[...]
    act = cin * jax.nn.sigmoid(cin)
    w = w_ref[0]
    w_hi = w.astype(BF16)
    w_lo = (w - w_hi.astype(F32)).astype(BF16)
    a_hi = act.astype(BF16)
    a_lo = (act - a_hi.astype(F32)).astype(BF16)
    o_ref[0] = (jnp.dot(a_hi, w_hi, preferred_element_type=F32) + jnp.dot(a_lo, w_hi, preferred_element_type=F32)
                + jnp.dot(a_hi, w_lo, preferred_element_type=F32)) + b_ref[0]


def _ada_table(cin, ada_w, ada_b):
    depth, d, n = ada_w.shape
    tn = 2 * d
    return pl.pallas_call(
        _ada_kernel,
        grid=(depth, n // tn),
        in_specs=[pl.BlockSpec((SUBLANES, d), lambda l, j: (0, 0)),
                  pl.BlockSpec((1, d, tn), lambda l, j: (l, 0, j)),
                  pl.BlockSpec((1, 1, tn), lambda l, j: (l, 0, j))],
        out_specs=pl.BlockSpec((1, SUBLANES, tn), lambda l, j: (l, 0, j)),
        out_shape=jax.ShapeDtypeStruct((depth, SUBLANES, n), F32),
        compiler_params=_cp("parallel", "parallel"),
        name="ada_table",
    )(cin, ada_w, ada_b.reshape(depth, 1, n))


def _out_kernel(y_ref, x_ref, mod_ref, w_ref, *rest):
    route_in, (o_ref, *route_out) = rest[:4], rest[4:]
    d = x_ref.shape[-1]
    m = mod_ref[0]
    y = jnp.dot(y_ref[...].astype(BF16), w_ref[...], preferred_element_type=F32)
    x1 = x_ref[...] + m[:, 2 * d:3 * d] * y
    o_ref[...] = x1
    _route_rows(x1, m, *route_in, *route_out)


def _out_proj(y, x, mod, w, route, nrows, s, b):
    d = x.shape[-1]
    k = y.shape[-1]
    r_in, r_out, r_shape, r_scratch = _route_io(d, nrows, TR)
    x1, *routed = pl.pallas_call(
        _out_kernel,
        grid=(nrows // TR,),
        in_specs=[pl.BlockSpec((TR, k), lambda i: (i, 0)),
                  pl.BlockSpec((TR, d), lambda i: (i, 0)),
                  _mod_spec(d, s // TR, b),
                  pl.BlockSpec((k, d), lambda i: (0, 0))] + r_in,
        out_specs=[pl.BlockSpec((TR, d), lambda i: (i, 0))] + r_out,
        out_shape=[jax.ShapeDtypeStruct((nrows, d), F32)] + r_shape,
        scratch_shapes=r_scratch,
        compiler_params=_cp("arbitrary"),
        name="out_proj",
    )(y, x, mod, w, *route)
    return x1, routed


def _rg_in_kernel(*refs, has_pre):
    x_ref, pre_refs, (g_ref, mod_ref, w_ref, *outs) = _split_refs(refs, has_pre)
    gg_ref, xin_ref = outs[-2:]
    d = x_ref.shape[-1]
    m = mod_ref[0]
    x = _combined_rows(x_ref, pre_refs)
    if has_pre:
        outs[0][...] = x
    h = _norm_mod(x, g_ref[...], m[:, 0:d], m[:, d:2 * d])
    z = jnp.dot(h.astype(BF16), w_ref[...], preferred_element_type=F32)
    tm = x_ref.shape[0]
    for n in range(d // LANES):
        cols = slice(n * LANES, (n + 1) * LANES)
        gg_ref[pl.ds(n, tm, stride=SUBLANES), :] = jax.nn.gelu(z[:, cols])
        xin_ref[pl.ds(n, tm, stride=SUBLANES), :] = z[:, d + n * LANES:d + (n + 1) * LANES]


def _rg_in(x, pre, g, mod, w, s, b):
    t, d = x.shape
    assert d == SUBLANES * LANES
    pre_specs, pre_args = _pre_io(pre, d, TM, s, b)
    row = pl.BlockSpec((TM, d), lambda i: (i, 0))
    tmajor = pl.BlockSpec((TM * SUBLANES, LANES), lambda i: (i, 0))
    outs = pl.pallas_call(
        functools.partial(_rg_in_kernel, has_pre=pre is not None),
        grid=(t // TM,),
        in_specs=[row] + pre_specs + [pl.BlockSpec((1, d), lambda i: (0, 0)), _mod_spec(d, s // TM, b),
                                      pl.BlockSpec((d, 2 * d), lambda i: (0, 0))],
        out_specs=([row] if pre else []) + [tmajor, tmajor],
        out_shape=([jax.ShapeDtypeStruct((t, d), F32)] if pre else [])
        + [jax.ShapeDtypeStruct((t * SUBLANES, LANES), F32)] * 2,
        compiler_params=_cp("parallel"),
        name="rg_in",
    )(x, *pre_args, g, mod, w)
    return (outs[0], outs[1], outs[2]) if pre else (x, outs[0], outs[1])


def _rg_scan_kernel(xm_ref, xprev_ref, xnext_ref, cw_ref, cb_ref, wa_ref, wi_ref,
                    ba_ref, bi_ref, lam_ref, *rest, reverse, nlat):
    if reverse:
        hf_ref, gg_ref, out_ref, xpad, xc, a_s, b_s, h_s, hcar = rest
    else:
        out_ref, xpad, xc, a_s, b_s, hcar = rest
    rows = TL * SUBLANES
    hrows = HALO * SUBLANES
    j = pl.program_id(1)
    m = (nlat - j) if reverse else (j - 1)
    has_prev = jnp.logical_and(j >= 1, m > 0)
    has_next = jnp.logical_and(j >= 1, m < nlat - 1)

    @pl.when(j == 0)
    def _():
        hcar[...] = jnp.zeros_like(hcar)

    xpad[0:hrows, :] = jnp.where(has_prev, xprev_ref[...], 0.0)
    xpad[hrows:hrows + rows, :] = xm_ref[...]
    xpad[hrows + rows:2 * hrows + rows, :] = jnp.where(has_next, xnext_ref[...], 0.0)
    acc = jnp.broadcast_to(cb_ref[...][None], (TL, SUBLANES, LANES))
    for k in range(CONV_W):
        off = (HALO + k - CONV_W // 2) * SUBLANES
        tap = xpad[off:off + rows, :].reshape(TL, SUBLANES, LANES)
        acc = acc + tap * cw_ref[k][None]
    xc[...] = acc.reshape(rows, LANES)

    for n in range(RG_BLOCKS):
        cols = slice(n * LANES, (n + 1) * LANES)
        xn = xc[pl.ds(n, TL, stride=SUBLANES), :]
        xb = xn.astype(BF16)
        ta = jnp.tanh(jnp.dot(xb, wa_ref[n], preferred_element_type=F32) + ba_ref[:, cols])
        ti = jnp.tanh(jnp.dot(xb, wi_ref[n], preferred_element_type=F32) + bi_ref[:, cols])
        k = (-0.5 * RG_C * LOG2E) * jax.nn.softplus(-lam_ref[:, cols])
        a = jnp.exp2(k * ta + k)
        om = 1.0 - a * a
        root = jnp.where(om > 0.0, om * lax.rsqrt(om), 0.0)
        a_s[pl.ds(n, TL, stride=SUBLANES), :] = a
        b_s[pl.ds(n, TL, stride=SUBLANES), :] = root * (0.5 * xn) * (ti + 1.0)

    h_dst = h_s if reverse else out_ref

    def two_steps(p, h):
        t0 = (TL - 1 - 2 * p) if reverse else 2 * p
        t1 = (t0 - 1) if reverse else (t0 + 1)
        r0 = pl.multiple_of(t0 * SUBLANES, SUBLANES)
        r1 = pl.multiple_of(t1 * SUBLANES, SUBLANES)
        a0 = a_s[pl.ds(r0, SUBLANES), :]
        b0 = b_s[pl.ds(r0, SUBLANES), :]
        a1 = a_s[pl.ds(r1, SUBLANES), :]
        b1 = b_s[pl.ds(r1, SUBLANES), :]
        h_dst[pl.ds(r0, SUBLANES), :] = a0 * h + b0
        h2 = (a1 * a0) * h + (a1 * b0 + b1)
        h_dst[pl.ds(r1, SUBLANES), :] = h2
        return h2

    hcar[...] = lax.fori_loop(0, TL // 2, two_steps, hcar[...], unroll=8)
    if reverse:
        h_s[...] = gg_ref[...] * (hf_ref[...] + h_s[...])
        for n in range(RG_BLOCKS):
            out_ref[:, n * LANES:(n + 1) * LANES] = h_s[pl.ds(n, TL, stride=SUBLANES), :].astype(BF16)


def _rg_scan(xin8, conv_w, conv_b, wa, wi, ba, bi, lam, s, c, b, reverse, hf8=None, gg8=None):
    assert c == TL and s % TL == 0
    rows = TL * SUBLANES
    hrows = HALO * SUBLANES
    nlat = s // TL
    t = xin8.shape[0] // SUBLANES
    n_halo = t // HALO

    def chunk(bi_, j):
        lat = bi_ * nlat + ((nlat - j) if reverse else (j - 1))
        return jnp.where(j == 0, (b * s) // TL + bi_, lat)

    main = pl.BlockSpec((rows, LANES), lambda bi_, j: (chunk(bi_, j), 0))
    prev = pl.BlockSpec((hrows, LANES),
                        lambda bi_, j: (jnp.maximum(chunk(bi_, j) * (TL // HALO) - 1, 0), 0))
    nxt = pl.BlockSpec((hrows, LANES),
                       lambda bi_, j: (jnp.minimum((chunk(bi_, j) + 1) * (TL // HALO), n_halo - 1), 0))
    full = lambda shape: pl.BlockSpec(shape, lambda bi_, j: (0,) * len(shape))
    d = RG_BLOCKS * LANES
    in_specs = [main, prev, nxt, full((CONV_W, SUBLANES, LANES)), full((SUBLANES, LANES)),
                full((RG_BLOCKS, LANES, LANES)), full((RG_BLOCKS, LANES, LANES)),
                full((1, d)), full((1, d)), full((1, d))]
    args = [xin8, xin8, xin8, conv_w.reshape(CONV_W, SUBLANES, LANES), conv_b.reshape(SUBLANES, LANES),
            (0.5 * wa).astype(BF16), (0.5 * wi).astype(BF16), 0.5 * ba.reshape(1, d), 0.5 * bi.reshape(1, d),
            lam.reshape(1, d)]
    scratch = [pltpu.VMEM((rows + 2 * hrows, LANES), F32), pltpu.VMEM((rows, LANES), F32),
               pltpu.VMEM((rows, LANES), F32), pltpu.VMEM((rows, LANES), F32)]
    if reverse:
        in_specs += [main, main]
        args += [hf8, gg8]
        scratch.append(pltpu.VMEM((rows, LANES), F32))
    scratch.append(pltpu.VMEM((SUBLANES, LANES), F32))
    return pl.pallas_call(
        functools.partial(_rg_scan_kernel, reverse=reverse, nlat=nlat),
        grid=(b, nlat + 1),
        in_specs=in_specs,
        out_specs=pl.BlockSpec((TL, d), lambda bi_, j: (chunk(bi_, j), 0)) if reverse else main,
        out_shape=jax.ShapeDtypeStruct((t, d), BF16) if reverse else jax.ShapeDtypeStruct(xin8.shape, F32),
        scratch_shapes=scratch,
        compiler_params=_cp("parallel", "arbitrary"),
        name="rg_scan_bwd" if reverse else "rg_scan_fwd",
    )(*args)


def _rglru_mixer(xa, pre, g, mod, w_in, conv_w, conv_b, wa, ba, wi, bi, lam, w_out, route, s, c, b, nrows_out):
    x, gg8, xin8 = _rg_in(xa, pre, g, mod, w_in.astype(BF16), s, b)
    hf8 = _rg_scan(xin8, conv_w, conv_b, wa[0], wi[0], ba[0], bi[0], lam[0], s, c, b, False)
    y = _rg_scan(xin8, conv_w, conv_b, wa[1], wi[1], ba[1], bi[1], lam[1], s, c, b, True, hf8, gg8)
    return _out_proj(y, x, mod, w_out.astype(BF16), route, nrows_out, s, b)


def _rope_tables(s):
    pos = jnp.arange(s, dtype=F32)
    row = jnp.floor(pos / GRID_W)
    col = pos - row * GRID_W
    n_freq = HEAD_DIM // 4
    inv = ROPE_THETA ** (-jnp.arange(n_freq, dtype=F32) * 2.0 / (HEAD_DIM // 2))
    ar = row[:, None] * inv
    ac = col[:, None] * inv
    cos = jnp.concatenate([jnp.cos(ar), jnp.cos(ar), jnp.cos(ac), jnp.cos(ac)], axis=1)
    sin = jnp.concatenate([-jnp.sin(ar), jnp.sin(ar), -jnp.sin(ac), jnp.sin(ac)], axis=1)
    cos = jnp.concatenate([cos, jnp.ones((TM, HEAD_DIM), F32)], axis=0)
    sin = jnp.concatenate([sin, jnp.zeros((TM, HEAD_DIM), F32)], axis=0)
    return cos, sin


def _qkv_kernel(*refs, has_pre):
    x_ref, pre_refs, (g_ref, mod_ref, w_ref, qg_ref, kg_ref, cos_ref, sin_ref, *outs) = _split_refs(refs, has_pre)
    q_ref, k_ref, v_ref = outs[-3:]
    d = x_ref.shape[-1]
    m = mod_ref[0]
    x = _combined_rows(x_ref, pre_refs)
    if has_pre:
        outs[0][...] = x
    h = _norm_mod(x, g_ref[...], m[:, 0:d], m[:, d:2 * d])
    z = jnp.dot(h.astype(BF16), w_ref[...], preferred_element_type=F32)
    cos = cos_ref[...]
    sin = sin_ref[...]
    lane = lax.broadcasted_iota(I32, cos.shape, 1)
    first_half = (lane % (HEAD_DIM // 2)) < (HEAD_DIM // 4)

    def head(zc, gain):
        ms = jnp.mean(zc * zc, axis=-1, keepdims=True)
        y = zc * lax.rsqrt(ms + NORM_EPS) * gain
        partner = jnp.where(first_half, pltpu.roll(y, HEAD_DIM - HEAD_DIM // 4, 1),
                            pltpu.roll(y, HEAD_DIM // 4, 1))
        return y * cos + partner * sin

    nq = q_ref.shape[-1] // HEAD_DIM
    nk = k_ref.shape[-1] // HEAD_DIM
    for j in range(nq):
        q_ref[:, j * HEAD_DIM:(j + 1) * HEAD_DIM] = (
            head(z[:, j * HEAD_DIM:(j + 1) * HEAD_DIM], qg_ref[...]) * (HEAD_DIM ** -0.5 * LOG2E)).astype(BF16)
    for j in range(nk):
        c0 = (nq + j) * HEAD_DIM
        k_ref[:, j * HEAD_DIM:(j + 1) * HEAD_DIM] = head(z[:, c0:c0 + HEAD_DIM], kg_ref[...]).astype(BF16)
    v_ref[...] = z[:, (nq + nk) * HEAD_DIM:].astype(BF16)


def _qkv(x, pre, g, mod, w, qg, kg, cos, sin, s, b):
    t, d = x.shape
    nkv = N_KV_HEADS * HEAD_DIM
    n_pos = s // TM
    pre_specs, pre_args = _pre_io(pre, d, TM, s, b)
    row = pl.BlockSpec((TM, d), lambda i: (i, 0))
    outs = pl.pallas_call(
        functools.partial(_qkv_kernel, has_pre=pre is not None),
        grid=(t // TM,),
        in_specs=[row] + pre_specs + [
            pl.BlockSpec((1, d), lambda i: (0, 0)),
            _mod_spec(d, s // TM, b),
            pl.BlockSpec(w.shape, lambda i: (0, 0)),
            pl.BlockSpec((1, HEAD_DIM), lambda i: (0, 0)),
            pl.BlockSpec((1, HEAD_DIM), lambda i: (0, 0)),
            pl.BlockSpec((TM, HEAD_DIM), lambda i: (jnp.where(i < b * n_pos, i % n_pos, n_pos), 0)),
            pl.BlockSpec((TM, HEAD_DIM), lambda i: (jnp.where(i < b * n_pos, i % n_pos, n_pos), 0))],
        out_specs=([row] if pre else []) + [pl.BlockSpec((TM, d), lambda i: (i, 0)),
                                           pl.BlockSpec((TM, nkv), lambda i: (i, 0)),
                                           pl.BlockSpec((TM, nkv), lambda i: (i, 0))],
        out_shape=([jax.ShapeDtypeStruct((t, d), F32)] if pre else [])
        + [jax.ShapeDtypeStruct((t, d), BF16), jax.ShapeDtypeStruct((t, nkv), BF16),
           jax.ShapeDtypeStruct((t, nkv), BF16)],
        compiler_params=_cp("parallel"),
        name="qkv_proj",
    )(x, *pre_args, g, mod, w, qg, kg, cos, sin)
    return tuple(outs) if pre else (x, *outs)


def _attn_kernel(q_ref, kc_ref, vc_ref, *rest, n_lat):
    if n_lat:
        kl_ref, vl_ref, o_ref, s_scr, vaug = rest
    else:
        o_ref, s_scr, vaug = rest
    n_ctx = kc_ref.shape[0]
    tq = q_ref.shape[0]

    def fill_values():
        vaug[:, HEAD_DIM:] = jnp.ones((n_ctx + n_lat, HEAD_DIM), BF16)
        vaug[0:n_ctx, 0:HEAD_DIM] = vc_ref[...]
        if n_lat:
            vaug[n_ctx:, 0:HEAD_DIM] = vl_ref[...]

    if n_lat:
        pl.when(pl.program_id(2) == 0)(fill_values)
    else:
        fill_values()

    chunks = [(0, n_ctx)] + [(n_ctx + j, ATT_KC) for j in range(0, n_lat, ATT_KC)]
    nt = (((1,), (1,)), ((), ()))
    q_all = jnp.concatenate([q_ref[:, g * HEAD_DIM:(g + 1) * HEAD_DIM] for g in range(GQA_GROUP)], axis=0)
    m_part = jnp.full((GQA_GROUP * tq, LANES), -jnp.inf, F32)
    for off, size in chunks:
        keys = kc_ref[...] if off == 0 else kl_ref[off - n_ctx:off - n_ctx + size, :]
        sc = lax.dot_general(q_all, keys, nt, preferred_element_type=F32)
        s_scr[:, off:off + size] = sc
        for j in range(0, size, LANES):
            m_part = jnp.maximum(m_part, sc[:, j:j + LANES])
    m_row = jnp.max(m_part, axis=-1, keepdims=True)
    hr = GQA_GROUP * tq // 2
    acc = [jnp.zeros((hr, 2 * HEAD_DIM), F32), jnp.zeros((hr, 2 * HEAD_DIM), F32)]
    for off, size in chunks:
        for r in range(2):
            rows = slice(r * hr, (r + 1) * hr)
            p = jnp.exp2((s_scr[rows, off:off + size] - m_row[rows]).astype(BF16))
            acc[r] = acc[r] + jnp.dot(p, vaug[off:off + size, :], preferred_element_type=F32)
    for r in range(2):
        out = (acc[r][:, :HEAD_DIM] / acc[r][:, HEAD_DIM:]).astype(BF16)
        for j in range(GQA_GROUP // 2):
            g = r * (GQA_GROUP // 2) + j
            o_ref[:, g * HEAD_DIM:(g + 1) * HEAD_DIM] = out[j * tq:(j + 1) * tq]


def _attn_ctx_kernel(q_ref, kc_ref, vc_ref, o_all_ref, o_ref, s_scr, vaug):
    del o_all_ref
    _attn_kernel(q_ref, kc_ref, vc_ref, o_ref, s_scr, vaug, n_lat=0)


def _attention(q, k, v, s, c, b):
    t, d = q.shape
    gw = GQA_GROUP * HEAD_DIM
    tq = TM
    nq = s // tq
    assert s % ATT_KC == 0
    ctx_blk = lambda bi, h, *_: ((b * s) // c + bi, h)
    o_lat = pl.pallas_call(
        functools.partial(_attn_kernel, n_lat=s),
        grid=(b, N_KV_HEADS, nq),
        in_specs=[pl.BlockSpec((tq, gw), lambda bi, h, i: (bi * nq + i, h)),
                  pl.BlockSpec((c, HEAD_DIM), ctx_blk),
                  pl.BlockSpec((c, HEAD_DIM), ctx_blk),
                  pl.BlockSpec((s, HEAD_DIM), lambda bi, h, i: (bi, h)),
                  pl.BlockSpec((s, HEAD_DIM), lambda bi, h, i: (bi, h))],
        out_specs=pl.BlockSpec((tq, gw), lambda bi, h, i: (bi * nq + i, h)),
        out_shape=jax.ShapeDtypeStruct((t, d), BF16),
        scratch_shapes=[pltpu.VMEM((GQA_GROUP * tq, c + s), F32), pltpu.VMEM((c + s, 2 * HEAD_DIM), BF16)],
        compiler_params=_cp("parallel", "parallel", "arbitrary"),
        name="attn_lat",
    )(q, k, v, k, v)
    return pl.pallas_call(
        _attn_ctx_kernel,
        grid=(b, N_KV_HEADS),
        in_specs=[pl.BlockSpec((c, gw), ctx_blk),
                  pl.BlockSpec((c, HEAD_DIM), ctx_blk),
                  pl.BlockSpec((c, HEAD_DIM), ctx_blk),
                  pl.BlockSpec(memory_space=pl.ANY)],
        out_specs=pl.BlockSpec((c, gw), ctx_blk),
        out_shape=jax.ShapeDtypeStruct((t, d), BF16),
        scratch_shapes=[pltpu.VMEM((GQA_GROUP * c, c), F32), pltpu.VMEM((c, 2 * HEAD_DIM), BF16)],
        input_output_aliases={3: 0},
        compiler_params=_cp("parallel", "parallel"),
        name="attn_ctx",
    )(q, k, v, o_lat)


def _attention_mixer(xa, pre, g, mod, w_qkv, qg, kg, w_o, route, s, c, b):
    cos, sin = _rope_tables(s)
    x, q, k, v = _qkv(xa, pre, g, mod, w_qkv.astype(BF16), qg.reshape(1, -1), kg.reshape(1, -1), cos, sin, s, b)
    o = _attention(q, k, v, s, c, b)
    return _out_proj(o, x, mod, w_o.astype(BF16), route, x.shape[0], s, b)


def _gmlp_kernel(*refs, has_pre):
    x_ref, pre_refs, rest = _split_refs(refs, has_pre)
    g_ref, mod_ref, w_in_ref, lng_ref, lnb_ref, ws_ref, bs_ref, w_out_ref = rest[:8]
    route_in, (o_ref, *route_out, uv_ref, cnt_s) = rest[8:12], rest[12:]
    d = x_ref.shape[-1]
    dcm = lng_ref.shape[-1]
    gw = dcm // CM_GROUPS
    x = _combined_rows(x_ref, pre_refs)
    m = mod_ref[0]
    h = _norm_mod(x, g_ref[...], m[:, 0:d], m[:, d:2 * d])
    z = jax.nn.gelu(jnp.dot(h.astype(BF16), w_in_ref[...], preferred_element_type=F32))
    u = z[:, :dcm]
    v = z[:, dcm:]
    mu = jnp.mean(v, axis=-1, keepdims=True)
    vc = v - mu
    var = jnp.mean(vc * vc, axis=-1, keepdims=True)
    vn = (vc * lax.rsqrt(var + NORM_EPS) * lng_ref[...] + lnb_ref[...]).astype(BF16)
    for ck in range(x.shape[0] // CHUNK):
        rows = slice(ck * CHUNK, (ck + 1) * CHUNK)
        for gi in range(CM_GROUPS):
            cols = slice(gi * gw, (gi + 1) * gw)
            mix = jnp.dot(ws_ref[gi], vn[rows, cols], preferred_element_type=F32) + bs_ref[:, gi:gi + 1]
            uv_ref[rows, cols] = (u[rows, cols] * mix).astype(BF16)
    y = jnp.dot(uv_ref[...], w_out_ref[...], preferred_element_type=F32)
    x1 = x + m[:, 2 * d:3 * d] * y
    o_ref[...] = x1
    _route_rows(x1, m, *route_in, *route_out, cnt_s)


def _gmlp_mixer(xa, pre, g, mod, w_in, ln_g, ln_b, w_s, b_s, w_out, route, s, b):
    t, d = xa.shape
    dcm = ln_g.shape[-1]
    full = lambda shape: pl.BlockSpec(shape, lambda i: (0,) * len(shape))
    pre_specs, pre_args = _pre_io(pre, d, TG, s, b)
    r_in, r_out, r_shape, r_scratch = _route_io(d, t, TG)
    x1, *routed = pl.pallas_call(
        functools.partial(_gmlp_kernel, has_pre=pre is not None),
        grid=(t // TG,),
        in_specs=[pl.BlockSpec((TG, d), lambda i: (i, 0))] + pre_specs + [
            full((1, d)),
            _mod_spec(d, s // TG, b),
            full((d, 2 * dcm)), full((1, dcm)), full((1, dcm)),
            full((CM_GROUPS, CHUNK, CHUNK)), full((CHUNK, CM_GROUPS)), full((dcm, d))] + r_in,
        out_specs=[pl.BlockSpec((TG, d), lambda i: (i, 0))] + r_out,
        out_shape=[jax.ShapeDtypeStruct((t, d), F32)] + r_shape,
        scratch_shapes=[pltpu.VMEM((TG, dcm), BF16)] + r_scratch,
        compiler_params=_cp("arbitrary"),
        name="gmlp",
    )(xa, *pre_args, g, mod, w_in.astype(BF16), ln_g.reshape(1, dcm), ln_b.reshape(1, dcm),
      w_s.astype(BF16), b_s.T, w_out.astype(BF16), *route)
    return x1, routed


def _route_rows(x1, m, g_ref, wrh_ref, wrl_ref, br_ref, hf_ref, rt_ref, ew_ref, cnt_ref, cnt_s):
    tm, d = x1.shape

    @pl.when(pl.program_id(0) == 0)
    def _():
        cnt_s[...] = jnp.zeros_like(cnt_s)

    hf = _norm_mod(x1, g_ref[...], m[:, 3 * d:4 * d], m[:, 4 * d:5 * d])
    hf_ref[...] = _pack_bf16_pairs(hf)
    hf_hi = hf.astype(BF16)
    hf_lo = (hf - hf_hi.astype(F32)).astype(BF16)
    logits = (jnp.dot(hf_hi, wrh_ref[...], preferred_element_type=F32)
              + jnp.dot(hf_lo, wrh_ref[...], preferred_element_type=F32)
              + jnp.dot(hf_hi, wrl_ref[...], preferred_element_type=F32)) + br_ref[...]
    lane = lax.broadcasted_iota(I32, logits.shape, 1)
    neg = -jnp.inf
    gl = jnp.where(lane < N_GROUPS, logits, neg)
    gmax = jnp.max(gl, axis=-1, keepdims=True)
    gsel = jnp.min(jnp.where(gl == gmax, lane, LANES), axis=-1, keepdims=True)
    gate_g = 1.0 / jnp.sum(jnp.exp(gl - gmax), axis=-1, keepdims=True)
    lo = N_GROUPS + gsel * EXPERTS_PER_GROUP
    el = jnp.where(jnp.logical_and(lane >= lo, lane < lo + EXPERTS_PER_GROUP), logits, neg)
    v1 = jnp.max(el, axis=-1, keepdims=True)
    i1 = jnp.min(jnp.where(el == v1, lane, LANES), axis=-1, keepdims=True)
    el2 = jnp.where(lane == i1, neg, el)
    v2 = jnp.max(el2, axis=-1, keepdims=True)
    i2 = jnp.min(jnp.where(el2 == v2, lane, LANES), axis=-1, keepdims=True)
    e21 = jnp.exp(v2 - v1)
    w1 = gate_g / (1.0 + e21)
    w2 = w1 * e21
    ew_ref[...] = jnp.where(lane == 0, w1, jnp.where(lane == 1, w2, 0.0))

    oh1 = lane == i1
    oh2 = lane == i2
    above = (lax.broadcasted_iota(I32, (tm, tm), 1) < lax.broadcasted_iota(I32, (tm, tm), 0)).astype(BF16)
    pre1 = jnp.dot(above, oh1.astype(BF16), preferred_element_type=F32)
    pre2 = jnp.dot(above, oh2.astype(BF16), preferred_element_type=F32)
    tot1 = jnp.sum(oh1.astype(F32), axis=0, keepdims=True)
    tot2 = jnp.sum(oh2.astype(F32), axis=0, keepdims=True)
    cnt = cnt_s[...]
    rank1 = jnp.sum(jnp.where(oh1, cnt + pre1, 0.0), axis=-1, keepdims=True).astype(I32)
    rank2 = jnp.sum(jnp.where(oh2, cnt + tot1 + pre2, 0.0), axis=-1, keepdims=True).astype(I32)
    cnt = cnt + tot1 + tot2
    cnt_s[...] = cnt
    cnt_ref[...] = jnp.broadcast_to(cnt, cnt_ref.shape).astype(I32)
    rt = jnp.where(lane == 0, i1 - N_GROUPS, jnp.where(lane == 1, i2 - N_GROUPS,
                   jnp.where(lane == 2, rank1, jnp.where(lane == 3, rank2, 0))))
    rt_ref[...] = rt.T[:SUBLANES]


def _route_params(g_ffn, w_group, b_group, w_router, b_router):
    d = w_group.shape[0]
    pad = LANES - N_GROUPS - N_EXPERTS
    wr = jnp.concatenate([w_group, w_router.reshape(d, N_EXPERTS), jnp.zeros((d, pad), F32)], axis=1)
    br = jnp.concatenate([b_group, b_router.reshape(N_EXPERTS), jnp.zeros((pad,), F32)]).reshape(1, LANES)
    wr_hi = wr.astype(BF16)
    wr_lo = (wr - wr_hi.astype(F32)).astype(BF16)
    return g_ffn.reshape(1, d), wr_hi, wr_lo, br


def _route_io(d, nrows, tr):
    const = lambda shape: pl.BlockSpec(shape, lambda i: (0, 0))
    row = lambda w: pl.BlockSpec((tr, w), lambda i: (i, 0))
    in_specs = [const((1, d)), const((d, LANES)), const((d, LANES)), const((1, LANES))]
    out_specs = [row(d // 2), pl.BlockSpec((SUBLANES, tr), lambda i: (0, i)), row(LANES), const((SUBLANES, LANES))]
    out_shape = [jax.ShapeDtypeStruct((nrows, d // 2), I32), jax.ShapeDtypeStruct((SUBLANES, nrows), I32),
                 jax.ShapeDtypeStruct((nrows, LANES), F32), jax.ShapeDtypeStruct((SUBLANES, LANES), I32)]
    return in_specs, out_specs, out_shape, [pltpu.VMEM((1, LANES), F32)]


def _plan_kernel(cnt_ref, rt_ref, pos_ref, blk_e_ref, n_used_ref, first_ref, slot_ref, next_ref,
                 start_s, end_s, nxt_s, *, bm):
    n_blk = blk_e_ref.shape[0]
    acc = jnp.int32(0)
    for e in range(N_EXPERTS):
        start_s[e] = acc
        acc = acc + (cnt_ref[0, N_GROUPS + e] + (bm - 1)) // bm * bm
        end_s[e] = acc
    n_used = acc // bm
    n_used_ref[0] = n_used
    nxt = jnp.int32(-1)
    for e in reversed(range(N_EXPERTS)):
        nxt_s[e] = nxt
        nxt = jnp.where(cnt_ref[0, N_GROUPS + e] > 0, e, nxt)

    def block(i, carry):
        prev_e, runs = carry
        row = jnp.minimum(i, n_used - 1) * bm
        e = lax.while_loop(lambda v: jnp.logical_and(v < N_EXPERTS - 1, end_s[v] <= row), lambda v: v + 1,
                           jnp.maximum(prev_e, 0))
        first = jnp.logical_and(i < n_used, prev_e != e)
        runs = runs + first.astype(I32)
        blk_e_ref[i] = e
        first_ref[i] = first.astype(I32)
        slot_ref[i] = (runs - 1) % 2
        next_ref[i] = nxt_s[e]
        return e, runs

    lax.fori_loop(0, n_blk, block, (jnp.int32(-1), jnp.int32(0)))

    rt = rt_ref[...]
    start_of = jnp.zeros_like(rt)
    for e in range(N_EXPERTS):
        start_of = jnp.where(rt == e, start_s[e], start_of)
    pos_ref[...] = start_of + pltpu.roll(rt, SUBLANES - 2, 0)


def _dispatch_plan(rt, cnt, bm):
    n_tok = rt.shape[1]
    n_rows = 2 * n_tok + N_EXPERTS * bm
    n_blk = n_rows // bm
    smem = pl.BlockSpec(memory_space=pltpu.SMEM)
    vec = lambda n: jax.ShapeDtypeStruct((n,), I32)
    pos, blk_e, n_used, first, slot, nxt = pl.pallas_call(
        functools.partial(_plan_kernel, bm=bm),
        in_specs=[smem, pl.BlockSpec(memory_space=pltpu.VMEM)],
        out_specs=[pl.BlockSpec(memory_space=pltpu.VMEM), smem, smem, smem, smem, smem],
        out_shape=[jax.ShapeDtypeStruct(rt.shape, I32), vec(n_blk), vec(1), vec(n_blk), vec(n_blk), vec(n_blk)],
        scratch_shapes=[pltpu.SMEM((N_EXPERTS,), I32)] * 3,
        name="moe_plan",
    )(cnt, rt)
    return pos, (blk_e, n_used, first, slot, nxt), n_rows


def _sc_mesh():
    return plsc.VectorSubcoreMesh(core_axis_name="c", subcore_axis_name="s")


def _sc_worker_base(per_worker):
    return (lax.axis_index("s") * SC_CORES + lax.axis_index("c")) * per_worker


def _sc_dispatch(hf, pos, n_rows):
    t, d = hf.shape
    per_w = t // SC_WORKERS
    ch = SC_CHUNK
    n_ck = per_w // ch
    assert per_w * SC_WORKERS == t and n_ck * ch == per_w

    @functools.partial(
        pl.kernel, mesh=_sc_mesh(), out_type=jax.ShapeDtypeStruct((n_rows, d), hf.dtype),
        scratch_types=[pltpu.VMEM((per_w,), I32), pltpu.VMEM((per_w,), I32), pltpu.VMEM((2, ch, d), hf.dtype),
                       pltpu.SemaphoreType.DMA((2,)), pltpu.SemaphoreType.DMA((2,)), pltpu.SemaphoreType.DMA((2,))])
    def dispatch(hf_hbm, p_hbm, out_hbm, i0_v, i1_v, rows_v, sem_in, sem_s0, sem_s1):
        base = pl.multiple_of(_sc_worker_base(per_w), SUBLANES)
        pltpu.sync_copy(p_hbm.at[pl.ds(base, per_w)], i0_v)
        pltpu.sync_copy(p_hbm.at[pl.ds(pl.multiple_of(t + base, SUBLANES), per_w)], i1_v)

        def load(ck):
            return pltpu.make_async_copy(hf_hbm.at[pl.ds(base + ck * ch, ch)], rows_v.at[ck % 2], sem_in.at[ck % 2])

        def scatters(ck):
            src = rows_v.at[ck % 2]
            return (pltpu.make_async_copy(src, out_hbm.at[i0_v.at[pl.ds(ck * ch, ch)]], sem_s0.at[ck % 2]),
                    pltpu.make_async_copy(src, out_hbm.at[i1_v.at[pl.ds(ck * ch, ch)]], sem_s1.at[ck % 2]))

        load(0).start()
        for ck in range(n_ck):
            load(ck).wait()
            if ck + 1 < n_ck:
                if ck >= 1:
                    for cp in scatters(ck - 1):
                        cp.wait()
                load(ck + 1).start()
            for cp in scatters(ck):
                cp.start()
        for ck in range(max(n_ck - 2, 0), n_ck):
            for cp in scatters(ck):
                cp.wait()

    return dispatch(hf, pos)


def _sc_gather(rows, idx):
    n = idx.shape[0]
    d = rows.shape[1]
    per_w = n // SC_WORKERS
    ch = 2 * SC_CHUNK
    n_ck = per_w // ch
    assert per_w * SC_WORKERS == n and n_ck * ch == per_w

    @functools.partial(
        pl.kernel, mesh=_sc_mesh(), out_type=jax.ShapeDtypeStruct((n, d), rows.dtype),
        scratch_types=[pltpu.VMEM((per_w,), I32), pltpu.VMEM((2, ch, d), rows.dtype),
                       pltpu.SemaphoreType.DMA((2,)), pltpu.SemaphoreType.DMA((2,))])
    def gather(rows_hbm, i_hbm, out_hbm, i_v, buf, sem_g, sem_w):
        base = pl.multiple_of(_sc_worker_base(per_w), SUBLANES)
        pltpu.sync_copy(i_hbm.at[pl.ds(base, per_w)], i_v)

        def fetch(ck):
            return pltpu.make_async_copy(rows_hbm.at[i_v.at[pl.ds(ck * ch, ch)]], buf.at[ck % 2], sem_g.at[ck % 2])

        def write(ck):
            return pltpu.make_async_copy(buf.at[ck % 2], out_hbm.at[pl.ds(base + ck * ch, ch)], sem_w.at[ck % 2])

        fetch(0).start()
        for ck in range(n_ck):
            fetch(ck).wait()
            if ck + 1 < n_ck:
                if ck >= 1:
                    write(ck - 1).wait()
                fetch(ck + 1).start()
            write(ck).start()
        for ck in range(max(n_ck - 2, 0), n_ck):
            write(ck).wait()

    return gather(rows, idx)


def _expert_kernel(blk_e_ref, n_used_ref, first_ref, slot_ref, next_ref, x_ref, wg_hbm, wu_hbm, wd_hbm, y_ref,
                   wgf, wuf, wdf, wgb, wub, wdb, sem, *, e_base):
    n_used = n_used_ref[0]
    bm = x_ref.shape[0] // MOE_SUB

    def weight_copies(e, slot):
        return (pltpu.make_async_copy(wg_hbm.at[e_base + e], wgf.at[slot], sem.at[slot, 0]),
                pltpu.make_async_copy(wu_hbm.at[e_base + e], wuf.at[slot], sem.at[slot, 1]),
                pltpu.make_async_copy(wd_hbm.at[e_base + e], wdf.at[slot], sem.at[slot, 2]))

    @pl.when(pl.program_id(0) == 0)
    def _():
        for cp in weight_copies(blk_e_ref[0], 0):
            cp.start()

    for j in range(MOE_SUB):
        blk = pl.program_id(0) * MOE_SUB + j
        rows = slice(j * bm, (j + 1) * bm)

        @pl.when(jnp.logical_and(blk < n_used, first_ref[blk] == 1))
        def _():
            slot = slot_ref[blk]
            for cp in weight_copies(blk_e_ref[blk], slot):
                cp.wait()
            nxt = next_ref[blk]

            @pl.when(nxt >= 0)
            def _():
                for cp in weight_copies(nxt, 1 - slot):
                    cp.start()

            wgb[...] = wgf[slot].astype(BF16)
            wub[...] = wuf[slot].astype(BF16)
            wdb[...] = wdf[slot].astype(BF16)

        @pl.when(blk < n_used)
        def _():
            x_hi, x_lo = _unpack_bf16_pairs(x_ref[rows, :])
            xb = jnp.concatenate([x_hi.astype(BF16), x_lo.astype(BF16)], axis=1)
            gt = jnp.dot(xb, wgb[...], preferred_element_type=F32)
            up = jnp.dot(xb, wub[...], preferred_element_type=F32)
            act = (gt * _sigmoid(gt) * up).astype(BF16)
            y_ref[rows, :] = _pack_bf16_pairs(jnp.dot(act, wdb[...], preferred_element_type=F32))


def _experts(x_rows, plan, w_gate, w_up, w_down, layer):
    n_rows, dp = x_rows.shape
    depth, n_e, d, de = w_gate.shape
    step_rows = MOE_SUB * MOE_BM
    assert n_rows % step_rows == 0
    any_spec = pl.BlockSpec(memory_space=pl.ANY)
    last_used = lambda i, be, nu, *_: (jnp.minimum(i, (nu[0] - 1) // MOE_SUB), 0)
    grid_spec = pltpu.PrefetchScalarGridSpec(
        num_scalar_prefetch=5,
        grid=(n_rows // step_rows,),
        in_specs=[pl.BlockSpec((step_rows, dp), last_used), any_spec, any_spec, any_spec],
        out_specs=pl.BlockSpec((step_rows, dp), last_used),
        scratch_shapes=[pltpu.VMEM((2, d, de), F32), pltpu.VMEM((2, d, de), F32), pltpu.VMEM((2, de, d), F32),
                        pltpu.VMEM((d, de), BF16), pltpu.VMEM((d, de), BF16), pltpu.VMEM((de, d), BF16),
                        pltpu.SemaphoreType.DMA((2, 3))],
    )
    return pl.pallas_call(
        functools.partial(_expert_kernel, e_base=layer * n_e),
        grid_spec=grid_spec,
        out_shape=jax.ShapeDtypeStruct((n_rows, dp), I32),
        compiler_params=_cp("arbitrary"),
        name="moe_experts",
    )(*plan, x_rows, w_gate.reshape(depth * n_e, d, de), w_up.reshape(depth * n_e, d, de),
      w_down.reshape(depth * n_e, de, d))


def _combine_kernel(x_ref, *rest):
    *pre_refs, o_ref = rest
    o_ref[...] = _combined_rows(x_ref, pre_refs)


def _combine(x, pre, s, b):
    nrows, d = x.shape
    pre_specs, pre_args = _pre_io(pre, d, TR, s, b)
    return pl.pallas_call(
        _combine_kernel,
        grid=(nrows // TR,),
        in_specs=[pl.BlockSpec((TR, d), lambda i: (i, 0))] + pre_specs,
        out_specs=pl.BlockSpec((TR, d), lambda i: (i, 0)),
        out_shape=jax.ShapeDtypeStruct((nrows, d), F32),
        compiler_params=_cp("parallel"),
        name="moe_combine",
    )(x, *pre_args)


def _expert_outputs(routed, w_gate, w_up, w_down, layer):
    hf, rt, _, cnt = routed
    pos, plan, n_rows = _dispatch_plan(rt, cnt, MOE_BM)
    pos = pos[0:2].reshape(-1)
    x_rows = _sc_dispatch(hf, pos, n_rows)
    y_rows = _experts(x_rows, plan, w_gate, w_up, w_down, layer)
    return _sc_gather(y_rows, pos)


def kernel(x, c, ctx, c_ctx, ada_w, ada_b, norm_mix_g, norm_ffn_g, rg_w_in, rg_conv_w, rg_conv_b, rg_wa, rg_ba, rg_wi, rg_bi, rg_lambda, rg_w_out, at_w_qkv, at_q_g, at_k_g, at_w_o, cm_w_in, cm_ln_g, cm_ln_b, cm_w_s, cm_b_s, cm_w_out, moe_w_group, moe_b_group, moe_w_router, moe_b_router, moe_w_gate, moe_w_up, moe_w_down):
    b, s, d = x.shape
    cl = ctx.shape[1]
    depth = ada_w.shape[0]
    n_lat = b * s
    assert b < SUBLANES and s % TR == 0 and cl % TM == 0 and (b * cl) % TR == 0 and d == RG_BLOCKS * LANES

    cin = jnp.concatenate([c, c_ctx[None, :], jnp.zeros((SUBLANES - b - 1, d), F32)], axis=0)
    mod_all = _ada_table(cin, ada_w, ada_b).reshape(depth, SUBLANES, 1, N_MOD * d)
    tok = jnp.concatenate([x.reshape(n_lat, d), ctx.reshape(b * cl, d)], axis=0)

    xa, pre = tok, None
    for l in range(depth):
        kind = l % 3
        j = l // 3
        last = l == depth - 1
        mod = mod_all[l]
        g_mix = norm_mix_g[l].reshape(1, d)
        route = _route_params(norm_ffn_g[l], moe_w_group[l], moe_b_group[l], moe_w_router[l], moe_b_router[l])
        if kind == 0:
            x1, routed = _rglru_mixer(xa, pre, g_mix, mod, rg_w_in[j], rg_conv_w[j], rg_conv_b[j], rg_wa[j],
                                      rg_ba[j], rg_wi[j], rg_bi[j], rg_lambda[j], rg_w_out[j], route, s, cl, b,
                                      n_lat if last else xa.shape[0])
        elif kind == 1:
            x1, routed = _attention_mixer(xa, pre, g_mix, mod, at_w_qkv[j], at_q_g[j], at_k_g[j], at_w_o[j],
                                          route, s, cl, b)
        else:
            x1, routed = _gmlp_mixer(xa, pre, g_mix, mod, cm_w_in[j], cm_ln_g[j], cm_ln_b[j], cm_w_s[j],
                                     cm_b_s[j], cm_w_out[j], route, s, b)
        y01 = _expert_outputs(routed, moe_w_gate, moe_w_up, moe_w_down, l)
        xa, pre = x1, (y01, routed[2], mod)
    return _combine(xa, pre, s, b)[:n_lat].reshape(b, s, d)
```

```python
import functools

import jax
import jax.numpy as jnp
from jax import lax
from jax.experimental import pallas as pl
from jax.experimental.pallas import tpu as pltpu
from jax.experimental.pallas import tpu_sc as plsc

F32 = jnp.float32
BF16 = jnp.bfloat16
I32 = jnp.int32
U32 = jnp.uint32

NORM_EPS = 1e-6
N_MOD = 6
GRID_W = 64
RG_BLOCKS = 8
CONV_W = 4
RG_C = 8.0
HEAD_DIM = 128
N_KV_HEADS = 2
GQA_GROUP = 4
ROPE_THETA = 10000.0
CHUNK = 128
CM_GROUPS = 8
N_GROUPS = 4
EXPERTS_PER_GROUP = 8
N_EXPERTS = N_GROUPS * EXPERTS_PER_GROUP

LANES = 128
SUBLANES = 8
TM = 256
TI = 512
TR = 512
TG = 512
TL = 256
HALO = 8
ATT_KC = 512
LOG2E = 1.4426950408889634
MOE_BM = 256
MOE_SUB = 4
SC_CORES = 2
SC_WORKERS = 32
SC_CHUNK = 32
VMEM_LIMIT = 52 * 2**20


def _cp(*sem):
    return pltpu.CompilerParams(dimension_semantics=sem, vmem_limit_bytes=VMEM_LIMIT)


def _norm_mod(x, g, shift, scale):
    ms = jnp.mean(x * x, axis=-1, keepdims=True)
    y = x * lax.rsqrt(ms + NORM_EPS) * g
    return y * (1.0 + scale) + shift


def _sigmoid(x):
    return 0.5 * jnp.tanh(0.5 * x) + 0.5


def _pack_bf16_pairs(x):
    h = x.shape[-1] // 2
    hi = lax.bitcast_convert_type(x[:, :h].astype(BF16).astype(F32), U32)
    lo = lax.bitcast_convert_type(x[:, h:].astype(BF16).astype(F32), U32)
    return lax.bitcast_convert_type(hi | (lo >> 16), I32)


def _unpack_bf16_pairs(w):
    u = lax.bitcast_convert_type(w, U32)
    hi = lax.bitcast_convert_type(u & jnp.uint32(0xFFFF0000), F32)
    lo = lax.bitcast_convert_type(u << 16, F32)
    return hi, lo


def _combined_rows(x_ref, pre_refs):
    if not pre_refs:
        return x_ref[...]
    y0_ref, y1_ref, ew_ref, modp_ref = pre_refs
    d = x_ref.shape[1]
    ew = ew_ref[...]
    y0_hi, y0_lo = _unpack_bf16_pairs(y0_ref[...])
    y1_hi, y1_lo = _unpack_bf16_pairs(y1_ref[...])
    y = jnp.concatenate([ew[:, 0:1] * y0_hi + ew[:, 1:2] * y1_hi, ew[:, 0:1] * y0_lo + ew[:, 1:2] * y1_lo], axis=1)
    return x_ref[...] + modp_ref[0][:, 5 * d:6 * d] * y


def _pre_io(pre, d, tr, s, b):
    if pre is None:
        return [], []
    y01, ew, mod_prev = pre
    nb = y01.shape[0] // 2 // tr
    specs = [pl.BlockSpec((tr, d // 2), lambda i: (i, 0)), pl.BlockSpec((tr, d // 2), lambda i: (i + nb, 0)),
             pl.BlockSpec((tr, LANES), lambda i: (i, 0)), _mod_spec(d, s // tr, b)]
    return specs, [y01, y01, ew, mod_prev]


def _split_refs(refs, has_pre):
    return (refs[0], refs[1:5], refs[5:]) if has_pre else (refs[0], (), refs[1:])


def _mod_spec(d, rows_per_sample, n_samples):
    return pl.BlockSpec((1, 1, N_MOD * d),
                        lambda i, *_: (jnp.minimum(i // rows_per_sample, n_samples), 0, 0))


def _ada_kernel(c_ref, w_ref, b_ref, o_ref):
    cin = c_ref[...]
    act = cin * jax.nn.sigmoid(cin)
    w = w_ref[0]
    w_hi = w.astype(BF16)
    w_lo = (w - w_hi.astype(F32)).astype(BF16)
    a_hi = act.astype(BF16)
    a_lo = (act - a_hi.astype(F32)).astype(BF16)
    o_ref[0] = (jnp.dot(a_hi, w_hi, preferred_element_type=F32) + jnp.dot(a_lo, w_hi, preferred_element_type=F32)
                + jnp.dot(a_hi, w_lo, preferred_element_type=F32)) + b_ref[0]


def _ada_table(cin, ada_w, ada_b):
    depth, d, n = ada_w.shape
    tn = 2 * d
    return pl.pallas_call(
        _ada_kernel,
        grid=(depth, n // tn),
        in_specs=[pl.BlockSpec((SUBLANES, d), lambda l, j: (0, 0)),
                  pl.BlockSpec((1, d, tn), lambda l, j: (l, 0, j)),
                  pl.BlockSpec((1, 1, tn), lambda l, j: (l, 0, j))],
        out_specs=pl.BlockSpec((1, SUBLANES, tn), lambda l, j: (l, 0, j)),
        out_shape=jax.ShapeDtypeStruct((depth, SUBLANES, n), F32),
        compiler_params=_cp("parallel", "parallel"),
        name="ada_table",
    )(cin, ada_w, ada_b.reshape(depth, 1, n))


def _out_kernel(y_ref, x_ref, mod_ref, w_ref, *rest):
    route_in, (o_ref, *route_out) = rest[:4], rest[4:]
    d = x_ref.shape[-1]
    m = mod_ref[0]
    y = jnp.dot(y_ref[...].astype(BF16), w_ref[...], preferred_element_type=F32)
    x1 = x_ref[...] + m[:, 2 * d:3 * d] * y
    o_ref[...] = x1
    _route_rows(x1, m, *route_in, *route_out)


def _out_proj(y, x, mod, w, route, nrows, s, b):
    d = x.shape[-1]
    k = y.shape[-1]
    r_in, r_out, r_shape, r_scratch = _route_io(d, nrows, TR)
    x1, *routed = pl.pallas_call(
        _out_kernel,
        grid=(nrows // TR,),
        in_specs=[pl.BlockSpec((TR, k), lambda i: (i, 0)),
                  pl.BlockSpec((TR, d), lambda i: (i, 0)),
                  _mod_spec(d, s // TR, b),
                  pl.BlockSpec((k, d), lambda i: (0, 0))] + r_in,
        out_specs=[pl.BlockSpec((TR, d), lambda i: (i, 0))] + r_out,
        out_shape=[jax.ShapeDtypeStruct((nrows, d), F32)] + r_shape,
        scratch_shapes=r_scratch,
        compiler_params=_cp("arbitrary"),
        name="out_proj",
    )(y, x, mod, w, *route)
    return x1, routed


def _rg_in_kernel(*refs, has_pre):
    x_ref, pre_refs, (g_ref, mod_ref, w_ref, *outs) = _split_refs(refs, has_pre)
    gg_ref, xin_ref = outs[-2:]
    d = x_ref.shape[-1]
    m = mod_ref[0]
    x = _combined_rows(x_ref, pre_refs)
    if has_pre:
        outs[0][...] = x
    h = _norm_mod(x, g_ref[...], m[:, 0:d], m[:, d:2 * d])
    z = jnp.dot(h.astype(BF16), w_ref[...], preferred_element_type=F32)
    tm = x_ref.shape[0]
    for n in range(d // LANES):
        cols = slice(n * LANES, (n + 1) * LANES)
        gg_ref[pl.ds(n, tm, stride=SUBLANES), :] = jax.nn.gelu(z[:, cols])
        xin_ref[pl.ds(n, tm, stride=SUBLANES), :] = z[:, d + n * LANES:d + (n + 1) * LANES]


def _rg_in(x, pre, g, mod, w, s, b):
    t, d = x.shape
    assert d == SUBLANES * LANES
    pre_specs, pre_args = _pre_io(pre, d, TI, s, b)
    row = pl.BlockSpec((TI, d), lambda i: (i, 0))
    tmajor = pl.BlockSpec((TI * SUBLANES, LANES), lambda i: (i, 0))
    outs = pl.pallas_call(
        functools.partial(_rg_in_kernel, has_pre=pre is not None),
        grid=(t // TI,),
        in_specs=[row] + pre_specs + [pl.BlockSpec((1, d), lambda i: (0, 0)), _mod_spec(d, s // TI, b),
                                      pl.BlockSpec((d, 2 * d), lambda i: (0, 0))],
        out_specs=([row] if pre else []) + [tmajor, tmajor],
        out_shape=([jax.ShapeDtypeStruct((t, d), F32)] if pre else [])
        + [jax.ShapeDtypeStruct((t * SUBLANES, LANES), F32)] * 2,
        compiler_params=_cp("parallel"),
        name="rg_in",
    )(x, *pre_args, g, mod, w)
    return (outs[0], outs[1], outs[2]) if pre else (x, outs[0], outs[1])


def _rg_gates_and_scan(xc, wa_ref, wi_ref, ba_ref, bi_ref, lam_ref, a_s, b_s, h_dst, hcar, reverse):
    @pl.when(pl.program_id(1) == 0)
    def _():
        hcar[...] = jnp.zeros_like(hcar)

    for n in range(RG_BLOCKS):
        cols = slice(n * LANES, (n + 1) * LANES)
        xn = xc[pl.ds(n, TL, stride=SUBLANES), :]
        xb = xn.astype(BF16)
        ta = jnp.tanh(jnp.dot(xb, wa_ref[n], preferred_element_type=F32) + ba_ref[:, cols])
        ti = jnp.tanh(jnp.dot(xb, wi_ref[n], preferred_element_type=F32) + bi_ref[:, cols])
        k = (-0.5 * RG_C * LOG2E) * jax.nn.softplus(-lam_ref[:, cols])
        a = jnp.exp2(k * ta + k)
        om = 1.0 - a * a
        root = jnp.where(om > 0.0, om * lax.rsqrt(om), 0.0)
        a_s[pl.ds(n, TL, stride=SUBLANES), :] = a
        b_s[pl.ds(n, TL, stride=SUBLANES), :] = root * (0.5 * xn) * (ti + 1.0)

    def two_steps(p, h):
        t0 = (TL - 1 - 2 * p) if reverse else 2 * p
        t1 = (t0 - 1) if reverse else (t0 + 1)
        r0 = pl.multiple_of(t0 * SUBLANES, SUBLANES)
        r1 = pl.multiple_of(t1 * SUBLANES, SUBLANES)
        a0 = a_s[pl.ds(r0, SUBLANES), :]
        b0 = b_s[pl.ds(r0, SUBLANES), :]
        a1 = a_s[pl.ds(r1, SUBLANES), :]
        b1 = b_s[pl.ds(r1, SUBLANES), :]
        h_dst[pl.ds(r0, SUBLANES), :] = a0 * h + b0
        h2 = (a1 * a0) * h + (a1 * b0 + b1)
        h_dst[pl.ds(r1, SUBLANES), :] = h2
        return h2

    hcar[...] = lax.fori_loop(0, TL // 2, two_steps, hcar[...], unroll=8)


def _rg_fwd_kernel(xm_ref, xprev_ref, xnext_ref, cw_ref, cb_ref, wa_ref, wi_ref, ba_ref, bi_ref, lam_ref,
                   hf_ref, xc_ref, xpad, a_s, b_s, hcar, *, nlat):
    rows = TL * SUBLANES
    hrows = HALO * SUBLANES
    j = pl.program_id(1)
    has_prev = j >= 2
    has_next = jnp.logical_and(j >= 1, j < nlat)
    xpad[0:hrows, :] = jnp.where(has_prev, xprev_ref[...], 0.0)
    xpad[hrows:hrows + rows, :] = xm_ref[...]
    xpad[hrows + rows:2 * hrows + rows, :] = jnp.where(has_next, xnext_ref[...], 0.0)
    acc = jnp.broadcast_to(cb_ref[...][None], (TL, SUBLANES, LANES))
    for k in range(CONV_W):
        off = (HALO + k - CONV_W // 2) * SUBLANES
        tap = xpad[off:off + rows, :].reshape(TL, SUBLANES, LANES)
        acc = acc + tap * cw_ref[k][None]
    xc_ref[...] = acc.reshape(rows, LANES)
    _rg_gates_and_scan(xc_ref, wa_ref, wi_ref, ba_ref, bi_ref, lam_ref, a_s, b_s, hf_ref, hcar, False)


def _rg_bwd_kernel(xc_ref, wa_ref, wi_ref, ba_ref, bi_ref, lam_ref, hf_ref, gg_ref, out_ref, a_s, b_s, h_s, hcar):
    _rg_gates_and_scan(xc_ref, wa_ref, wi_ref, ba_ref, bi_ref, lam_ref, a_s, b_s, h_s, hcar, True)
    h_s[...] = gg_ref[...] * (hf_ref[...] + h_s[...])
    for n in range(RG_BLOCKS):
        out_ref[:, n * LANES:(n + 1) * LANES] = h_s[pl.ds(n, TL, stride=SUBLANES), :].astype(BF16)


def _rg_scans(xin8, gg8, conv_w, conv_b, wa, wi, ba, bi, lam, s, c, b):
    assert c == TL and s % TL == 0
    rows = TL * SUBLANES
    hrows = HALO * SUBLANES
    nlat = s // TL
    t = xin8.shape[0] // SUBLANES
    n_halo = t // HALO
    d = RG_BLOCKS * LANES

    def chunk(reverse):
        return lambda bi_, j: jnp.where(j == 0, (b * s) // TL + bi_,
                                        bi_ * nlat + ((nlat - j) if reverse else (j - 1)))

    fwd, bwd = chunk(False), chunk(True)
    main_f = pl.BlockSpec((rows, LANES), lambda bi_, j: (fwd(bi_, j), 0))
    main_b = pl.BlockSpec((rows, LANES), lambda bi_, j: (bwd(bi_, j), 0))
    prev = pl.BlockSpec((hrows, LANES), lambda bi_, j: (jnp.maximum(fwd(bi_, j) * (TL // HALO) - 1, 0), 0))
    nxt = pl.BlockSpec((hrows, LANES),
                       lambda bi_, j: (jnp.minimum((fwd(bi_, j) + 1) * (TL // HALO), n_halo - 1), 0))
    full = lambda shape: pl.BlockSpec(shape, lambda bi_, j: (0,) * len(shape))
    gate_specs = [full((RG_BLOCKS, LANES, LANES)), full((RG_BLOCKS, LANES, LANES)),
                  full((1, d)), full((1, d)), full((1, d))]
    gate_args = lambda k: [(0.5 * wa[k]).astype(BF16), (0.5 * wi[k]).astype(BF16), 0.5 * ba[k].reshape(1, d),
                           0.5 * bi[k].reshape(1, d), lam[k].reshape(1, d)]
    tmajor = jax.ShapeDtypeStruct(xin8.shape, F32)
    buf = pltpu.VMEM((rows, LANES), F32)
    hf8, xc8 = pl.pallas_call(
        functools.partial(_rg_fwd_kernel, nlat=nlat),
        grid=(b, nlat + 1),
        in_specs=[main_f, prev, nxt, full((CONV_W, SUBLANES, LANES)), full((SUBLANES, LANES))] + gate_specs,
        out_specs=[main_f, main_f],
        out_shape=[tmajor, tmajor],
        scratch_shapes=[pltpu.VMEM((rows + 2 * hrows, LANES), F32), buf, buf, pltpu.VMEM((SUBLANES, LANES), F32)],
        compiler_params=_cp("parallel", "arbitrary"),
        name="rg_scan_fwd",
    )(xin8, xin8, xin8, conv_w.reshape(CONV_W, SUBLANES, LANES), conv_b.reshape(SUBLANES, LANES), *gate_args(0))
    return pl.pallas_call(
        _rg_bwd_kernel,
        grid=(b, nlat + 1),
        in_specs=[main_b] + gate_specs + [main_b, main_b],
        out_specs=pl.BlockSpec((TL, d), lambda bi_, j: (bwd(bi_, j), 0)),
        out_shape=jax.ShapeDtypeStruct((t, d), BF16),
        scratch_shapes=[buf, buf, buf, pltpu.VMEM((SUBLANES, LANES), F32)],
        compiler_params=_cp("parallel", "arbitrary"),
        name="rg_scan_bwd",
    )(xc8, *gate_args(1), hf8, gg8)


def _rglru_mixer(xa, pre, g, mod, w_in, conv_w, conv_b, wa, ba, wi, bi, lam, w_out, route, s, c, b, nrows_out):
    x, gg8, xin8 = _rg_in(xa, pre, g, mod, w_in.astype(BF16), s, b)
    y = _rg_scans(xin8, gg8, conv_w, conv_b, wa, wi, ba, bi, lam, s, c, b)
    return _out_proj(y, x, mod, w_out.astype(BF16), route, nrows_out, s, b)


def _rope_tables(s):
    pos = jnp.arange(s, dtype=F32)
    row = jnp.floor(pos / GRID_W)
    col = pos - row * GRID_W
    n_freq = HEAD_DIM // 4
    inv = ROPE_THETA ** (-jnp.arange(n_freq, dtype=F32) * 2.0 / (HEAD_DIM // 2))
    ar = row[:, None] * inv
    ac = col[:, None] * inv
    cos = jnp.concatenate([jnp.cos(ar), jnp.cos(ar), jnp.cos(ac), jnp.cos(ac)], axis=1)
    sin = jnp.concatenate([-jnp.sin(ar), jnp.sin(ar), -jnp.sin(ac), jnp.sin(ac)], axis=1)
    cos = jnp.concatenate([cos, jnp.ones((TM, HEAD_DIM), F32)], axis=0)
    sin = jnp.concatenate([sin, jnp.zeros((TM, HEAD_DIM), F32)], axis=0)
    return cos, sin


def _qkv_kernel(*refs, has_pre):
    x_ref, pre_refs, (g_ref, mod_ref, w_ref, qg_ref, kg_ref, cos_ref, sin_ref, *outs) = _split_refs(refs, has_pre)
    q_ref, k_ref, v_ref = outs[-3:]
    d = x_ref.shape[-1]
    m = mod_ref[0]
    x = _combined_rows(x_ref, pre_refs)
    if has_pre:
        outs[0][...] = x
    h = _norm_mod(x, g_ref[...], m[:, 0:d], m[:, d:2 * d])
    z = jnp.dot(h.astype(BF16), w_ref[...], preferred_element_type=F32)
    cos = cos_ref[...]
    sin = sin_ref[...]
    lane = lax.broadcasted_iota(I32, cos.shape, 1)
    first_half = (lane % (HEAD_DIM // 2)) < (HEAD_DIM // 4)

    def head(zc, gain):
        ms = jnp.mean(zc * zc, axis=-1, keepdims=True)
        y = zc * lax.rsqrt(ms + NORM_EPS) * gain
        partner = jnp.where(first_half, pltpu.roll(y, HEAD_DIM - HEAD_DIM // 4, 1),
                            pltpu.roll(y, HEAD_DIM // 4, 1))
        return y * cos + partner * sin

    nq = q_ref.shape[-1] // HEAD_DIM
    nk = k_ref.shape[-1] // HEAD_DIM
    for j in range(nq):
        q_ref[:, j * HEAD_DIM:(j + 1) * HEAD_DIM] = (
            head(z[:, j * HEAD_DIM:(j + 1) * HEAD_DIM], qg_ref[...]) * (HEAD_DIM ** -0.5 * LOG2E)).astype(BF16)
    for j in range(nk):
        c0 = (nq + j) * HEAD_DIM
        k_ref[:, j * HEAD_DIM:(j + 1) * HEAD_DIM] = head(z[:, c0:c0 + HEAD_DIM], kg_ref[...]).astype(BF16)
    v_ref[...] = z[:, (nq + nk) * HEAD_DIM:].astype(BF16)


def _qkv(x, pre, g, mod, w, qg, kg, cos, sin, s, b):
    t, d = x.shape
    nkv = N_KV_HEADS * HEAD_DIM
    n_pos = s // TM
    pre_specs, pre_args = _pre_io(pre, d, TM, s, b)
    row = pl.BlockSpec((TM, d), lambda i: (i, 0))
    outs = pl.pallas_call(
        functools.partial(_qkv_kernel, has_pre=pre is not None),
        grid=(t // TM,),
        in_specs=[row] + pre_specs + [
            pl.BlockSpec((1, d), lambda i: (0, 0)),
            _mod_spec(d, s // TM, b),
            pl.BlockSpec(w.shape, lambda i: (0, 0)),
            pl.BlockSpec((1, HEAD_DIM), lambda i: (0, 0)),
            pl.BlockSpec((1, HEAD_DIM), lambda i: (0, 0)),
            pl.BlockSpec((TM, HEAD_DIM), lambda i: (jnp.where(i < b * n_pos, i % n_pos, n_pos), 0)),
            pl.BlockSpec((TM, HEAD_DIM), lambda i: (jnp.where(i < b * n_pos, i % n_pos, n_pos), 0))],
        out_specs=([row] if pre else []) + [pl.BlockSpec((TM, d), lambda i: (i, 0)),
                                           pl.BlockSpec((TM, nkv), lambda i: (i, 0)),
                                           pl.BlockSpec((TM, nkv), lambda i: (i, 0))],
        out_shape=([jax.ShapeDtypeStruct((t, d), F32)] if pre else [])
        + [jax.ShapeDtypeStruct((t, d), BF16), jax.ShapeDtypeStruct((t, nkv), BF16),
           jax.ShapeDtypeStruct((t, nkv), BF16)],
        compiler_params=_cp("parallel"),
        name="qkv_proj",
    )(x, *pre_args, g, mod, w, qg, kg, cos, sin)
    return tuple(outs) if pre else (x, *outs)


def _attn_kernel(q_ref, kc_ref, vc_ref, *rest, n_lat):
    if n_lat:
        kl_ref, vl_ref, o_ref, s_scr, vaug = rest
    else:
        o_ref, s_scr, vaug = rest
    n_ctx = kc_ref.shape[0]
    tq = q_ref.shape[0]

    def fill_values():
        vaug[:, HEAD_DIM:] = jnp.ones((n_ctx + n_lat, HEAD_DIM), BF16)
        vaug[0:n_ctx, 0:HEAD_DIM] = vc_ref[...]
        if n_lat:
            vaug[n_ctx:, 0:HEAD_DIM] = vl_ref[...]

    if n_lat:
        pl.when(pl.program_id(2) == 0)(fill_values)
    else:
        fill_values()

    chunks = [(0, n_ctx)] + [(n_ctx + j, ATT_KC) for j in range(0, n_lat, ATT_KC)]
    nt = (((1,), (1,)), ((), ()))
    q_all = jnp.concatenate([q_ref[:, g * HEAD_DIM:(g + 1) * HEAD_DIM] for g in range(GQA_GROUP)], axis=0)
    m_part = jnp.full((GQA_GROUP * tq, LANES), -jnp.inf, F32)
    for off, size in chunks:
        keys = kc_ref[...] if off == 0 else kl_ref[off - n_ctx:off - n_ctx + size, :]
        sc = lax.dot_general(q_all, keys, nt, preferred_element_type=F32)
        s_scr[:, off:off + size] = sc
        for j in range(0, size, LANES):
            m_part = jnp.maximum(m_part, sc[:, j:j + LANES])
    m_row = jnp.max(m_part, axis=-1, keepdims=True)
    hr = GQA_GROUP * tq // 2
    acc = [jnp.zeros((hr, 2 * HEAD_DIM), F32), jnp.zeros((hr, 2 * HEAD_DIM), F32)]
    for off, size in chunks:
        for r in range(2):
            rows = slice(r * hr, (r + 1) * hr)
            p = jnp.exp2((s_scr[rows, off:off + size] - m_row[rows]).astype(BF16))
            acc[r] = acc[r] + jnp.dot(p, vaug[off:off + size, :], preferred_element_type=F32)
    for r in range(2):
        out = (acc[r][:, :HEAD_DIM] / acc[r][:, HEAD_DIM:]).astype(BF16)
        for j in range(GQA_GROUP // 2):
            g = r * (GQA_GROUP // 2) + j
            o_ref[:, g * HEAD_DIM:(g + 1) * HEAD_DIM] = out[j * tq:(j + 1) * tq]


def _attn_ctx_kernel(q_ref, kc_ref, vc_ref, o_all_ref, o_ref, s_scr, vaug):
    del o_all_ref
    _attn_kernel(q_ref, kc_ref, vc_ref, o_ref, s_scr, vaug, n_lat=0)


def _attention(q, k, v, s, c, b):
    t, d = q.shape
    gw = GQA_GROUP * HEAD_DIM
    tq = TM
    nq = s // tq
    assert s % ATT_KC == 0
    ctx_blk = lambda bi, h, *_: ((b * s) // c + bi, h)
    o_lat = pl.pallas_call(
        functools.partial(_attn_kernel, n_lat=s),
        grid=(b, N_KV_HEADS, nq),
        in_specs=[pl.BlockSpec((tq, gw), lambda bi, h, i: (bi * nq + i, h)),
                  pl.BlockSpec((c, HEAD_DIM), ctx_blk),
                  pl.BlockSpec((c, HEAD_DIM), ctx_blk),
                  pl.BlockSpec((s, HEAD_DIM), lambda bi, h, i: (bi, h)),
                  pl.BlockSpec((s, HEAD_DIM), lambda bi, h, i: (bi, h))],
        out_specs=pl.BlockSpec((tq, gw), lambda bi, h, i: (bi * nq + i, h)),
        out_shape=jax.ShapeDtypeStruct((t, d), BF16),
        scratch_shapes=[pltpu.VMEM((GQA_GROUP * tq, c + s), F32), pltpu.VMEM((c + s, 2 * HEAD_DIM), BF16)],
        compiler_params=_cp("parallel", "parallel", "arbitrary"),
        name="attn_lat",
    )(q, k, v, k, v)
    return pl.pallas_call(
        _attn_ctx_kernel,
        grid=(b, N_KV_HEADS),
        in_specs=[pl.BlockSpec((c, gw), ctx_blk),
                  pl.BlockSpec((c, HEAD_DIM), ctx_blk),
                  pl.BlockSpec((c, HEAD_DIM), ctx_blk),
                  pl.BlockSpec(memory_space=pl.ANY)],
        out_specs=pl.BlockSpec((c, gw), ctx_blk),
        out_shape=jax.ShapeDtypeStruct((t, d), BF16),
        scratch_shapes=[pltpu.VMEM((GQA_GROUP * c, c), F32), pltpu.VMEM((c, 2 * HEAD_DIM), BF16)],
        input_output_aliases={3: 0},
        compiler_params=_cp("parallel", "parallel"),
        name="attn_ctx",
    )(q, k, v, o_lat)


def _attention_mixer(xa, pre, g, mod, w_qkv, qg, kg, w_o, route, s, c, b):
    cos, sin = _rope_tables(s)
    x, q, k, v = _qkv(xa, pre, g, mod, w_qkv.astype(BF16), qg.reshape(1, -1), kg.reshape(1, -1), cos, sin, s, b)
    o = _attention(q, k, v, s, c, b)
    return _out_proj(o, x, mod, w_o.astype(BF16), route, x.shape[0], s, b)


def _gmlp_kernel(*refs, has_pre):
    x_ref, pre_refs, rest = _split_refs(refs, has_pre)
    g_ref, mod_ref, w_in_ref, lng_ref, lnb_ref, ws_ref, bs_ref, w_out_ref = rest[:8]
    route_in, (o_ref, *route_out, uv_ref, cnt_s) = rest[8:12], rest[12:]
    d = x_ref.shape[-1]
    dcm = lng_ref.shape[-1]
    gw = dcm // CM_GROUPS
    x = _combined_rows(x_ref, pre_refs)
    m = mod_ref[0]
    h = _norm_mod(x, g_ref[...], m[:, 0:d], m[:, d:2 * d])
    z = jax.nn.gelu(jnp.dot(h.astype(BF16), w_in_ref[...], preferred_element_type=F32))
    u = z[:, :dcm]
    v = z[:, dcm:]
    mu = jnp.mean(v, axis=-1, keepdims=True)
    vc = v - mu
    var = jnp.mean(vc * vc, axis=-1, keepdims=True)
    vn = (vc * lax.rsqrt(var + NORM_EPS) * lng_ref[...] + lnb_ref[...]).astype(BF16)
    for ck in range(x.shape[0] // CHUNK):
        rows = slice(ck * CHUNK, (ck + 1) * CHUNK)
        for gi in range(CM_GROUPS):
            cols = slice(gi * gw, (gi + 1) * gw)
            mix = jnp.dot(ws_ref[gi], vn[rows, cols], preferred_element_type=F32) + bs_ref[:, gi:gi + 1]
            uv_ref[rows, cols] = (u[rows, cols] * mix).astype(BF16)
    y = jnp.dot(uv_ref[...], w_out_ref[...], preferred_element_type=F32)
    x1 = x + m[:, 2 * d:3 * d] * y
    o_ref[...] = x1
    _route_rows(x1, m, *route_in, *route_out, cnt_s)


def _gmlp_mixer(xa, pre, g, mod, w_in, ln_g, ln_b, w_s, b_s, w_out, route, s, b):
    t, d = xa.shape
    dcm = ln_g.shape[-1]
    full = lambda shape: pl.BlockSpec(shape, lambda i: (0,) * len(shape))
    pre_specs, pre_args = _pre_io(pre, d, TG, s, b)
    r_in, r_out, r_shape, r_scratch = _route_io(d, t, TG)
    x1, *routed = pl.pallas_call(
        functools.partial(_gmlp_kernel, has_pre=pre is not None),
        grid=(t // TG,),
        in_specs=[pl.BlockSpec((TG, d), lambda i: (i, 0))] + pre_specs + [
            full((1, d)),
            _mod_spec(d, s // TG, b),
            full((d, 2 * dcm)), full((1, dcm)), full((1, dcm)),
            full((CM_GROUPS, CHUNK, CHUNK)), full((CHUNK, CM_GROUPS)), full((dcm, d))] + r_in,
        out_specs=[pl.BlockSpec((TG, d), lambda i: (i, 0))] + r_out,
        out_shape=[jax.ShapeDtypeStruct((t, d), F32)] + r_shape,
        scratch_shapes=[pltpu.VMEM((TG, dcm), BF16)] + r_scratch,
        compiler_params=_cp("arbitrary"),
        name="gmlp",
    )(xa, *pre_args, g, mod, w_in.astype(BF16), ln_g.reshape(1, dcm), ln_b.reshape(1, dcm),
      w_s.astype(BF16), b_s.T, w_out.astype(BF16), *route)
    return x1, routed


def _route_rows(x1, m, g_ref, wrh_ref, wrl_ref, br_ref, hf_ref, rt_ref, ew_ref, cnt_ref, cnt_s):
    tm, d = x1.shape

    @pl.when(pl.program_id(0) == 0)
    def _():
        cnt_s[...] = jnp.zeros_like(cnt_s)

    hf = _norm_mod(x1, g_ref[...], m[:, 3 * d:4 * d], m[:, 4 * d:5 * d])
    hf_ref[...] = _pack_bf16_pairs(hf)
    hf_hi = hf.astype(BF16)
    hf_lo = (hf - hf_hi.astype(F32)).astype(BF16)
    logits = (jnp.dot(hf_hi, wrh_ref[...], preferred_element_type=F32)
              + jnp.dot(hf_lo, wrh_ref[...], preferred_element_type=F32)
              + jnp.dot(hf_hi, wrl_ref[...], preferred_element_type=F32)) + br_ref[...]
    lane = lax.broadcasted_iota(I32, logits.shape, 1)
    neg = -jnp.inf
    gl = jnp.where(lane < N_GROUPS, logits, neg)
    gmax = jnp.max(gl, axis=-1, keepdims=True)
    gsel = jnp.min(jnp.where(gl == gmax, lane, LANES), axis=-1, keepdims=True)
    gate_g = 1.0 / jnp.sum(jnp.exp(gl - gmax), axis=-1, keepdims=True)
    lo = N_GROUPS + gsel * EXPERTS_PER_GROUP
    el = jnp.where(jnp.logical_and(lane >= lo, lane < lo + EXPERTS_PER_GROUP), logits, neg)
    v1 = jnp.max(el, axis=-1, keepdims=True)
    i1 = jnp.min(jnp.where(el == v1, lane, LANES), axis=-1, keepdims=True)
    el2 = jnp.where(lane == i1, neg, el)
    v2 = jnp.max(el2, axis=-1, keepdims=True)
    i2 = jnp.min(jnp.where(el2 == v2, lane, LANES), axis=-1, keepdims=True)
    e21 = jnp.exp(v2 - v1)
    w1 = gate_g / (1.0 + e21)
    w2 = w1 * e21
    ew_ref[...] = jnp.where(lane == 0, w1, jnp.where(lane == 1, w2, 0.0))

    oh1 = lane == i1
    oh2 = lane == i2
    above = (lax.broadcasted_iota(I32, (tm, tm), 1) < lax.broadcasted_iota(I32, (tm, tm), 0)).astype(BF16)
    pre1 = jnp.dot(above, oh1.astype(BF16), preferred_element_type=F32)
    pre2 = jnp.dot(above, oh2.astype(BF16), preferred_element_type=F32)
    tot1 = jnp.sum(oh1.astype(F32), axis=0, keepdims=True)
    tot2 = jnp.sum(oh2.astype(F32), axis=0, keepdims=True)
    cnt = cnt_s[...]
    rank1 = jnp.sum(jnp.where(oh1, cnt + pre1, 0.0), axis=-1, keepdims=True).astype(I32)
    rank2 = jnp.sum(jnp.where(oh2, cnt + tot1 + pre2, 0.0), axis=-1, keepdims=True).astype(I32)
    cnt = cnt + tot1 + tot2
    cnt_s[...] = cnt
    cnt_ref[...] = jnp.broadcast_to(cnt, cnt_ref.shape).astype(I32)
    rt = jnp.where(lane == 0, i1 - N_GROUPS, jnp.where(lane == 1, i2 - N_GROUPS,
                   jnp.where(lane == 2, rank1, jnp.where(lane == 3, rank2, 0))))
    rt_ref[...] = rt.T[:SUBLANES]


def _route_params(g_ffn, w_group, b_group, w_router, b_router):
    d = w_group.shape[0]
    pad = LANES - N_GROUPS - N_EXPERTS
    wr = jnp.concatenate([w_group, w_router.reshape(d, N_EXPERTS), jnp.zeros((d, pad), F32)], axis=1)
    br = jnp.concatenate([b_group, b_router.reshape(N_EXPERTS), jnp.zeros((pad,), F32)]).reshape(1, LANES)
    wr_hi = wr.astype(BF16)
    wr_lo = (wr - wr_hi.astype(F32)).astype(BF16)
    return g_ffn.reshape(1, d), wr_hi, wr_lo, br


def _route_io(d, nrows, tr):
    const = lambda shape: pl.BlockSpec(shape, lambda i: (0, 0))
    row = lambda w: pl.BlockSpec((tr, w), lambda i: (i, 0))
    in_specs = [const((1, d)), const((d, LANES)), const((d, LANES)), const((1, LANES))]
    out_specs = [row(d // 2), pl.BlockSpec((SUBLANES, tr), lambda i: (0, i)), row(LANES), const((SUBLANES, LANES))]
    out_shape = [jax.ShapeDtypeStruct((nrows, d // 2), I32), jax.ShapeDtypeStruct((SUBLANES, nrows), I32),
                 jax.ShapeDtypeStruct((nrows, LANES), F32), jax.ShapeDtypeStruct((SUBLANES, LANES), I32)]
    return in_specs, out_specs, out_shape, [pltpu.VMEM((1, LANES), F32)]


def _plan_kernel(cnt_ref, rt_ref, pos_ref, blk_e_ref, n_used_ref, first_ref, slot_ref, next_ref,
                 start_s, end_s, nxt_s, *, bm):
    n_blk = blk_e_ref.shape[0]
    acc = jnp.int32(0)
    for e in range(N_EXPERTS):
        start_s[e] = acc
        acc = acc + (cnt_ref[0, N_GROUPS + e] + (bm - 1)) // bm * bm
        end_s[e] = acc
    n_used = acc // bm
    n_used_ref[0] = n_used
    nxt = jnp.int32(-1)
    for e in reversed(range(N_EXPERTS)):
        nxt_s[e] = nxt
        nxt = jnp.where(cnt_ref[0, N_GROUPS + e] > 0, e, nxt)

    def block(i, carry):
        prev_e, runs = carry
        row = jnp.minimum(i, n_used - 1) * bm
        e = lax.while_loop(lambda v: jnp.logical_and(v < N_EXPERTS - 1, end_s[v] <= row), lambda v: v + 1,
                           jnp.maximum(prev_e, 0))
        first = jnp.logical_and(i < n_used, prev_e != e)
        runs = runs + first.astype(I32)
        blk_e_ref[i] = e
        first_ref[i] = first.astype(I32)
        slot_ref[i] = (runs - 1) % 2
        next_ref[i] = nxt_s[e]
        return e, runs

    lax.fori_loop(0, n_blk, block, (jnp.int32(-1), jnp.int32(0)))

    rt = rt_ref[...]
    start_of = jnp.zeros_like(rt)
    for e in range(N_EXPERTS):
        start_of = jnp.where(rt == e, start_s[e], start_of)
    pos_ref[...] = start_of + pltpu.roll(rt, SUBLANES - 2, 0)


def _dispatch_plan(rt, cnt, bm):
    n_tok = rt.shape[1]
    n_rows = 2 * n_tok + N_EXPERTS * bm
    n_blk = n_rows // bm
    smem = pl.BlockSpec(memory_space=pltpu.SMEM)
    vec = lambda n: jax.ShapeDtypeStruct((n,), I32)
    pos, blk_e, n_used, first, slot, nxt = pl.pallas_call(
        functools.partial(_plan_kernel, bm=bm),
        in_specs=[smem, pl.BlockSpec(memory_space=pltpu.VMEM)],
        out_specs=[pl.BlockSpec(memory_space=pltpu.VMEM), smem, smem, smem, smem, smem],
        out_shape=[jax.ShapeDtypeStruct(rt.shape, I32), vec(n_blk), vec(1), vec(n_blk), vec(n_blk), vec(n_blk)],
        scratch_shapes=[pltpu.SMEM((N_EXPERTS,), I32)] * 3,
        name="moe_plan",
    )(cnt, rt)
    return pos, (blk_e, n_used, first, slot, nxt), n_rows


def _sc_mesh():
    return plsc.VectorSubcoreMesh(core_axis_name="c", subcore_axis_name="s")


def _sc_worker_base(per_worker):
    return (lax.axis_index("s") * SC_CORES + lax.axis_index("c")) * per_worker


def _sc_dispatch(hf, pos, n_rows):
    t, d = hf.shape
    per_w = t // SC_WORKERS
    ch = SC_CHUNK
    n_ck = per_w // ch
    assert per_w * SC_WORKERS == t and n_ck * ch == per_w

    @functools.partial(
        pl.kernel, mesh=_sc_mesh(), out_type=jax.ShapeDtypeStruct((n_rows, d), hf.dtype),
        scratch_types=[pltpu.VMEM((per_w,), I32), pltpu.VMEM((per_w,), I32), pltpu.VMEM((2, ch, d), hf.dtype),
                       pltpu.SemaphoreType.DMA((2,)), pltpu.SemaphoreType.DMA((2,)), pltpu.SemaphoreType.DMA((2,))])
    def dispatch(hf_hbm, p_hbm, out_hbm, i0_v, i1_v, rows_v, sem_in, sem_s0, sem_s1):
        base = pl.multiple_of(_sc_worker_base(per_w), SUBLANES)
        pltpu.sync_copy(p_hbm.at[pl.ds(base, per_w)], i0_v)
        pltpu.sync_copy(p_hbm.at[pl.ds(pl.multiple_of(t + base, SUBLANES), per_w)], i1_v)

        def load(ck):
            return pltpu.make_async_copy(hf_hbm.at[pl.ds(base + ck * ch, ch)], rows_v.at[ck % 2], sem_in.at[ck % 2])

        def scatters(ck):
            src = rows_v.at[ck % 2]
            return (pltpu.make_async_copy(src, out_hbm.at[i0_v.at[pl.ds(ck * ch, ch)]], sem_s0.at[ck % 2]),
                    pltpu.make_async_copy(src, out_hbm.at[i1_v.at[pl.ds(ck * ch, ch)]], sem_s1.at[ck % 2]))

        load(0).start()
        for ck in range(n_ck):
            load(ck).wait()
            if ck + 1 < n_ck:
                if ck >= 1:
                    for cp in scatters(ck - 1):
                        cp.wait()
                load(ck + 1).start()
            for cp in scatters(ck):
                cp.start()
        for ck in range(max(n_ck - 2, 0), n_ck):
            for cp in scatters(ck):
                cp.wait()

    return dispatch(hf, pos)


def _sc_gather(rows, idx):
    n = idx.shape[0]
    d = rows.shape[1]
    per_w = n // SC_WORKERS
    ch = 2 * SC_CHUNK
    n_ck = per_w // ch
    assert per_w * SC_WORKERS == n and n_ck * ch == per_w

    @functools.partial(
        pl.kernel, mesh=_sc_mesh(), out_type=jax.ShapeDtypeStruct((n, d), rows.dtype),
        scratch_types=[pltpu.VMEM((per_w,), I32), pltpu.VMEM((2, ch, d), rows.dtype),
                       pltpu.SemaphoreType.DMA((2,)), pltpu.SemaphoreType.DMA((2,))])
    def gather(rows_hbm, i_hbm, out_hbm, i_v, buf, sem_g, sem_w):
        base = pl.multiple_of(_sc_worker_base(per_w), SUBLANES)
        pltpu.sync_copy(i_hbm.at[pl.ds(base, per_w)], i_v)

        def fetch(ck):
            return pltpu.make_async_copy(rows_hbm.at[i_v.at[pl.ds(ck * ch, ch)]], buf.at[ck % 2], sem_g.at[ck % 2])

        def write(ck):
            return pltpu.make_async_copy(buf.at[ck % 2], out_hbm.at[pl.ds(base + ck * ch, ch)], sem_w.at[ck % 2])

        fetch(0).start()
        for ck in range(n_ck):
            fetch(ck).wait()
            if ck + 1 < n_ck:
                if ck >= 1:
                    write(ck - 1).wait()
                fetch(ck + 1).start()
            write(ck).start()
        for ck in range(max(n_ck - 2, 0), n_ck):
            write(ck).wait()

    return gather(rows, idx)


def _expert_kernel(blk_e_ref, n_used_ref, first_ref, slot_ref, next_ref, x_ref, wg_hbm, wu_hbm, wd_hbm, y_ref,
                   wgf, wuf, wdf, wgb, wub, wdb, sem, *, e_base):
    n_used = n_used_ref[0]
    bm = x_ref.shape[0] // MOE_SUB

    def weight_copies(e, slot):
        return (pltpu.make_async_copy(wg_hbm.at[e_base + e], wgf.at[slot], sem.at[slot, 0]),
                pltpu.make_async_copy(wu_hbm.at[e_base + e], wuf.at[slot], sem.at[slot, 1]),
                pltpu.make_async_copy(wd_hbm.at[e_base + e], wdf.at[slot], sem.at[slot, 2]))

    @pl.when(pl.program_id(0) == 0)
    def _():
        for cp in weight_copies(blk_e_ref[0], 0):
            cp.start()

    for j in range(MOE_SUB):
        blk = pl.program_id(0) * MOE_SUB + j
        rows = slice(j * bm, (j + 1) * bm)

        @pl.when(jnp.logical_and(blk < n_used, first_ref[blk] == 1))
        def _():
            slot = slot_ref[blk]
            for cp in weight_copies(blk_e_ref[blk], slot):
                cp.wait()
            nxt = next_ref[blk]

            @pl.when(nxt >= 0)
            def _():
                for cp in weight_copies(nxt, 1 - slot):
                    cp.start()

            wgb[...] = wgf[slot].astype(BF16)
            wub[...] = wuf[slot].astype(BF16)
            wdb[...] = wdf[slot].astype(BF16)

        @pl.when(blk < n_used)
        def _():
            x_hi, x_lo = _unpack_bf16_pairs(x_ref[rows, :])
            xb = jnp.concatenate([x_hi.astype(BF16), x_lo.astype(BF16)], axis=1)
            gt = jnp.dot(xb, wgb[...], preferred_element_type=F32)
            up = jnp.dot(xb, wub[...], preferred_element_type=F32)
            act = (gt * _sigmoid(gt) * up).astype(BF16)
            y_ref[rows, :] = _pack_bf16_pairs(jnp.dot(act, wdb[...], preferred_element_type=F32))


def _experts(x_rows, plan, w_gate, w_up, w_down, layer):
    n_rows, dp = x_rows.shape
    depth, n_e, d, de = w_gate.shape
    step_rows = MOE_SUB * MOE_BM
    assert n_rows % step_rows == 0
    any_spec = pl.BlockSpec(memory_space=pl.ANY)
    last_used = lambda i, be, nu, *_: (jnp.minimum(i, (nu[0] - 1) // MOE_SUB), 0)
    grid_spec = pltpu.PrefetchScalarGridSpec(
        num_scalar_prefetch=5,
        grid=(n_rows // step_rows,),
        in_specs=[pl.BlockSpec((step_rows, dp), last_used), any_spec, any_spec, any_spec],
        out_specs=pl.BlockSpec((step_rows, dp), last_used),
        scratch_shapes=[pltpu.VMEM((2, d, de), F32), pltpu.VMEM((2, d, de), F32), pltpu.VMEM((2, de, d), F32),
                        pltpu.VMEM((d, de), BF16), pltpu.VMEM((d, de), BF16), pltpu.VMEM((de, d), BF16),
                        pltpu.SemaphoreType.DMA((2, 3))],
    )
    return pl.pallas_call(
        functools.partial(_expert_kernel, e_base=layer * n_e),
        grid_spec=grid_spec,
        out_shape=jax.ShapeDtypeStruct((n_rows, dp), I32),
        compiler_params=_cp("arbitrary"),
        name="moe_experts",
    )(*plan, x_rows, w_gate.reshape(depth * n_e, d, de), w_up.reshape(depth * n_e, d, de),
      w_down.reshape(depth * n_e, de, d))


def _combine_kernel(x_ref, *rest):
    *pre_refs, o_ref = rest
    o_ref[...] = _combined_rows(x_ref, pre_refs)


def _combine(x, pre, s, b):
    nrows, d = x.shape
    pre_specs, pre_args = _pre_io(pre, d, TR, s, b)
    return pl.pallas_call(
        _combine_kernel,
        grid=(nrows // TR,),
        in_specs=[pl.BlockSpec((TR, d), lambda i: (i, 0))] + pre_specs,
        out_specs=pl.BlockSpec((TR, d), lambda i: (i, 0)),
        out_shape=jax.ShapeDtypeStruct((nrows, d), F32),
        compiler_params=_cp("parallel"),
        name="moe_combine",
    )(x, *pre_args)


def _expert_outputs(routed, w_gate, w_up, w_down, layer):
    hf, rt, _, cnt = routed
    pos, plan, n_rows = _dispatch_plan(rt, cnt, MOE_BM)
    pos = pos[0:2].reshape(-1)
    x_rows = _sc_dispatch(hf, pos, n_rows)
    y_rows = _experts(x_rows, plan, w_gate, w_up, w_down, layer)
    return _sc_gather(y_rows, pos)


def kernel(x, c, ctx, c_ctx, ada_w, ada_b, norm_mix_g, norm_ffn_g, rg_w_in, rg_conv_w, rg_conv_b, rg_wa, rg_ba, rg_wi, rg_bi, rg_lambda, rg_w_out, at_w_qkv, at_q_g, at_k_g, at_w_o, cm_w_in, cm_ln_g, cm_ln_b, cm_w_s, cm_b_s, cm_w_out, moe_w_group, moe_b_group, moe_w_router, moe_b_router, moe_w_gate, moe_w_up, moe_w_down):
    b, s, d = x.shape
    cl = ctx.shape[1]
    depth = ada_w.shape[0]
    n_lat = b * s
    assert b < SUBLANES and s % max(TI, TR, TG) == 0 and (b * cl) % max(TI, TR, TG) == 0 and cl % TM == 0
    assert d == RG_BLOCKS * LANES

    cin = jnp.concatenate([c, c_ctx[None, :], jnp.zeros((SUBLANES - b - 1, d), F32)], axis=0)
    mod_all = _ada_table(cin, ada_w, ada_b).reshape(depth, SUBLANES, 1, N_MOD * d)
    tok = jnp.concatenate([x.reshape(n_lat, d), ctx.reshape(b * cl, d)], axis=0)

    xa, pre = tok, None
    for l in range(depth):
        kind = l % 3
        j = l // 3
        last = l == depth - 1
        mod = mod_all[l]
        g_mix = norm_mix_g[l].reshape(1, d)
        route = _route_params(norm_ffn_g[l], moe_w_group[l], moe_b_group[l], moe_w_router[l], moe_b_router[l])
        if kind == 0:
            x1, routed = _rglru_mixer(xa, pre, g_mix, mod, rg_w_in[j], rg_conv_w[j], rg_conv_b[j], rg_wa[j],
                                      rg_ba[j], rg_wi[j], rg_bi[j], rg_lambda[j], rg_w_out[j], route, s, cl, b,
                                      n_lat if last else xa.shape[0])
        elif kind == 1:
            x1, routed = _attention_mixer(xa, pre, g_mix, mod, at_w_qkv[j], at_q_g[j], at_k_g[j], at_w_o[j],
                                          route, s, cl, b)
        else:
            x1, routed = _gmlp_mixer(xa, pre, g_mix, mod, cm_w_in[j], cm_ln_g[j], cm_ln_b[j], cm_w_s[j],
                                     cm_b_s[j], cm_w_out[j], route, s, b)
        y01 = _expert_outputs(routed, moe_w_gate, moe_w_up, moe_w_down, l)
        xa, pre = x1, (y01, routed[2], mod)
    return _combine(xa, pre, s, b)[:n_lat].reshape(b, s, d)
```

```python
import functools

import jax
import jax.numpy as jnp
from jax import lax
from jax.experimental import pallas as pl
from jax.experimental.pallas import tpu as pltpu
from jax.experimental.pallas import tpu_sc as plsc

F32 = jnp.float32
BF16 = jnp.bfloat16
I32 = jnp.int32
U32 = jnp.uint32

NORM_EPS = 1e-6
N_MOD = 6
GRID_W = 64
RG_BLOCKS = 8
CONV_W = 4
RG_C = 8.0
HEAD_DIM = 128
N_KV_HEADS = 2
GQA_GROUP = 4
ROPE_THETA = 10000.0
CHUNK = 128
CM_GROUPS = 8
N_GROUPS = 4
EXPERTS_PER_GROUP = 8
N_EXPERTS = N_GROUPS * EXPERTS_PER_GROUP

LANES = 128
SUBLANES = 8
TM = 256
TI = 512
TR = 1024
TG = 512
TL = 256
HALO = 8
ATT_KC = 512
LOG2E = 1.4426950408889634
MOE_BM = 256
MOE_SUB = 4
SC_CORES = 2
SC_WORKERS = 32
SC_CHUNK = 32
VMEM_LIMIT = 52 * 2**20


def _cp(*sem):
    return pltpu.CompilerParams(dimension_semantics=sem, vmem_limit_bytes=VMEM_LIMIT)


def _norm_mod(x, g, shift, scale):
    ms = jnp.mean(x * x, axis=-1, keepdims=True)
    y = x * lax.rsqrt(ms + NORM_EPS) * g
    return y * (1.0 + scale) + shift


def _sigmoid(x):
    return 0.5 * jnp.tanh(0.5 * x) + 0.5


def _pack_bf16_pairs(x):
    h = x.shape[-1] // 2
    hi = lax.bitcast_convert_type(x[:, :h].astype(BF16).astype(F32), U32)
    lo = lax.bitcast_convert_type(x[:, h:].astype(BF16).astype(F32), U32)
    return lax.bitcast_convert_type(hi | (lo >> 16), I32)


def _unpack_bf16_pairs(w):
    u = lax.bitcast_convert_type(w, U32)
    hi = lax.bitcast_convert_type(u & jnp.uint32(0xFFFF0000), F32)
    lo = lax.bitcast_convert_type(u << 16, F32)
    return hi, lo


def _combined_rows(x_ref, pre_refs):
    if not pre_refs:
        return x_ref[...]
    y0_ref, y1_ref, ew_ref, modp_ref = pre_refs
    d = x_ref.shape[1]
    ew = ew_ref[...]
    y0_hi, y0_lo = _unpack_bf16_pairs(y0_ref[...])
    y1_hi, y1_lo = _unpack_bf16_pairs(y1_ref[...])
    y = jnp.concatenate([ew[:, 0:1] * y0_hi + ew[:, 1:2] * y1_hi, ew[:, 0:1] * y0_lo + ew[:, 1:2] * y1_lo], axis=1)
    return x_ref[...] + modp_ref[0][:, 5 * d:6 * d] * y


def _pre_io(pre, d, tr, s, b):
    if pre is None:
        return [], []
    y01, ew, mod_prev = pre
    nb = y01.shape[0] // 2 // tr
    specs = [pl.BlockSpec((tr, d // 2), lambda i: (i, 0)), pl.BlockSpec((tr, d // 2), lambda i: (i + nb, 0)),
             pl.BlockSpec((tr, LANES), lambda i: (i, 0)), _mod_spec(d, s // tr, b)]
    return specs, [y01, y01, ew, mod_prev]


def _split_refs(refs, has_pre):
    return (refs[0], refs[1:5], refs[5:]) if has_pre else (refs[0], (), refs[1:])


def _mod_spec(d, rows_per_sample, n_samples):
    return pl.BlockSpec((1, 1, N_MOD * d),
                        lambda i, *_: (jnp.minimum(i // rows_per_sample, n_samples), 0, 0))


def _ada_kernel(c_ref, w_ref, b_ref, o_ref):
    cin = c_ref[...]
    act = cin * jax.nn.sigmoid(cin)
    w = w_ref[0]
    w_hi = w.astype(BF16)
    w_lo = (w - w_hi.astype(F32)).astype(BF16)
    a_hi = act.astype(BF16)
    a_lo = (act - a_hi.astype(F32)).astype(BF16)
    o_ref[0] = (jnp.dot(a_hi, w_hi, preferred_element_type=F32) + jnp.dot(a_lo, w_hi, preferred_element_type=F32)
                + jnp.dot(a_hi, w_lo, preferred_element_type=F32)) + b_ref[0]


def _ada_table(cin, ada_w, ada_b):
    depth, d, n = ada_w.shape
    tn = 2 * d
    return pl.pallas_call(
        _ada_kernel,
        grid=(depth, n // tn),
        in_specs=[pl.BlockSpec((SUBLANES, d), lambda l, j: (0, 0)),
                  pl.BlockSpec((1, d, tn), lambda l, j: (l, 0, j)),
                  pl.BlockSpec((1, 1, tn), lambda l, j: (l, 0, j))],
        out_specs=pl.BlockSpec((1, SUBLANES, tn), lambda l, j: (l, 0, j)),
        out_shape=jax.ShapeDtypeStruct((depth, SUBLANES, n), F32),
        compiler_params=_cp("parallel", "parallel"),
        name="ada_table",
    )(cin, ada_w, ada_b.reshape(depth, 1, n))


def _out_kernel(y_ref, x_ref, mod_ref, w_ref, *rest):
    route_in, (o_ref, *route_out) = rest[:4], rest[4:]
    d = x_ref.shape[-1]
    m = mod_ref[0]
    y = jnp.dot(y_ref[...].astype(BF16), w_ref[...], preferred_element_type=F32)
    x1 = x_ref[...] + m[:, 2 * d:3 * d] * y
    o_ref[...] = x1
    _route_rows(x1, m, *route_in, *route_out)


def _out_proj(y, x, mod, w, route, nrows, s, b):
    d = x.shape[-1]
    k = y.shape[-1]
    r_in, r_out, r_shape, r_scratch = _route_io(d, nrows, TR)
    x1, *routed = pl.pallas_call(
        _out_kernel,
        grid=(nrows // TR,),
        in_specs=[pl.BlockSpec((TR, k), lambda i: (i, 0)),
                  pl.BlockSpec((TR, d), lambda i: (i, 0)),
                  _mod_spec(d, s // TR, b),
                  pl.BlockSpec((k, d), lambda i: (0, 0))] + r_in,
        out_specs=[pl.BlockSpec((TR, d), lambda i: (i, 0))] + r_out,
        out_shape=[jax.ShapeDtypeStruct((nrows, d), F32)] + r_shape,
        scratch_shapes=r_scratch,
        compiler_params=_cp("arbitrary"),
        name="out_proj",
    )(y, x, mod, w, *route)
    return x1, routed


def _rg_in_kernel(*refs, has_pre):
    x_ref, pre_refs, (g_ref, mod_ref, w_ref, *outs) = _split_refs(refs, has_pre)
    gg_ref, xin_ref = outs[-2:]
    d = x_ref.shape[-1]
    m = mod_ref[0]
    x = _combined_rows(x_ref, pre_refs)
    if has_pre:
        outs[0][...] = x
    h = _norm_mod(x, g_ref[...], m[:, 0:d], m[:, d:2 * d])
    z = jnp.dot(h.astype(BF16), w_ref[...], preferred_element_type=F32)
    tm = x_ref.shape[0]
    for n in range(d // LANES):
        cols = slice(n * LANES, (n + 1) * LANES)
        gg_ref[pl.ds(n, tm, stride=SUBLANES), :] = jax.nn.gelu(z[:, cols])
        xin_ref[pl.ds(n, tm, stride=SUBLANES), :] = z[:, d + n * LANES:d + (n + 1) * LANES]


def _rg_in(x, pre, g, mod, w, s, b):
    t, d = x.shape
    assert d == SUBLANES * LANES
    pre_specs, pre_args = _pre_io(pre, d, TI, s, b)
    row = pl.BlockSpec((TI, d), lambda i: (i, 0))
    tmajor = pl.BlockSpec((TI * SUBLANES, LANES), lambda i: (i, 0))
    outs = pl.pallas_call(
        functools.partial(_rg_in_kernel, has_pre=pre is not None),
        grid=(t // TI,),
        in_specs=[row] + pre_specs + [pl.BlockSpec((1, d), lambda i: (0, 0)), _mod_spec(d, s // TI, b),
                                      pl.BlockSpec((d, 2 * d), lambda i: (0, 0))],
        out_specs=([row] if pre else []) + [tmajor, tmajor],
        out_shape=([jax.ShapeDtypeStruct((t, d), F32)] if pre else [])
        + [jax.ShapeDtypeStruct((t * SUBLANES, LANES), F32)] * 2,
        compiler_params=_cp("parallel"),
        name="rg_in",
    )(x, *pre_args, g, mod, w)
    return (outs[0], outs[1], outs[2]) if pre else (x, outs[0], outs[1])


def _rg_gates_and_scan(xc, wa_ref, wi_ref, ba_ref, bi_ref, lam_ref, a_s, b_s, h_dst, hcar, reverse):
    @pl.when(pl.program_id(1) == 0)
    def _():
        hcar[...] = jnp.zeros_like(hcar)

    for n in range(RG_BLOCKS):
        cols = slice(n * LANES, (n + 1) * LANES)
        xn = xc[pl.ds(n, TL, stride=SUBLANES), :]
        xb = xn.astype(BF16)
        ta = jnp.tanh(jnp.dot(xb, wa_ref[n], preferred_element_type=F32) + ba_ref[:, cols])
        ti = jnp.tanh(jnp.dot(xb, wi_ref[n], preferred_element_type=F32) + bi_ref[:, cols])
        k = (-0.5 * RG_C * LOG2E) * jax.nn.softplus(-lam_ref[:, cols])
        a = jnp.exp2(k * ta + k)
        om = 1.0 - a * a
        root = jnp.where(om > 0.0, om * lax.rsqrt(om), 0.0)
        a_s[pl.ds(n, TL, stride=SUBLANES), :] = a
        b_s[pl.ds(n, TL, stride=SUBLANES), :] = root * (0.5 * xn) * (ti + 1.0)

    def two_steps(p, h):
        t0 = (TL - 1 - 2 * p) if reverse else 2 * p
        t1 = (t0 - 1) if reverse else (t0 + 1)
        r0 = pl.multiple_of(t0 * SUBLANES, SUBLANES)
        r1 = pl.multiple_of(t1 * SUBLANES, SUBLANES)
        a0 = a_s[pl.ds(r0, SUBLANES), :]
        b0 = b_s[pl.ds(r0, SUBLANES), :]
        a1 = a_s[pl.ds(r1, SUBLANES), :]
        b1 = b_s[pl.ds(r1, SUBLANES), :]
        h_dst[pl.ds(r0, SUBLANES), :] = a0 * h + b0
        h2 = (a1 * a0) * h + (a1 * b0 + b1)
        h_dst[pl.ds(r1, SUBLANES), :] = h2
        return h2

    hcar[...] = lax.fori_loop(0, TL // 2, two_steps, hcar[...], unroll=8)


def _rg_fwd_kernel(xm_ref, xprev_ref, xnext_ref, cw_ref, cb_ref, wa_ref, wi_ref, ba_ref, bi_ref, lam_ref,
                   hf_ref, xc_ref, xpad, a_s, b_s, hcar, *, nlat):
    rows = TL * SUBLANES
    hrows = HALO * SUBLANES
    j = pl.program_id(1)
    has_prev = j >= 2
    has_next = jnp.logical_and(j >= 1, j < nlat)
    xpad[0:hrows, :] = jnp.where(has_prev, xprev_ref[...], 0.0)
    xpad[hrows:hrows + rows, :] = xm_ref[...]
    xpad[hrows + rows:2 * hrows + rows, :] = jnp.where(has_next, xnext_ref[...], 0.0)
    acc = jnp.broadcast_to(cb_ref[...][None], (TL, SUBLANES, LANES))
    for k in range(CONV_W):
        off = (HALO + k - CONV_W // 2) * SUBLANES
        tap = xpad[off:off + rows, :].reshape(TL, SUBLANES, LANES)
        acc = acc + tap * cw_ref[k][None]
    xc_ref[...] = acc.reshape(rows, LANES)
    _rg_gates_and_scan(xc_ref, wa_ref, wi_ref, ba_ref, bi_ref, lam_ref, a_s, b_s, hf_ref, hcar, False)


def _rg_bwd_kernel(xc_ref, wa_ref, wi_ref, ba_ref, bi_ref, lam_ref, hf_ref, gg_ref, out_ref, a_s, b_s, h_s, hcar):
    _rg_gates_and_scan(xc_ref, wa_ref, wi_ref, ba_ref, bi_ref, lam_ref, a_s, b_s, h_s, hcar, True)
    h_s[...] = gg_ref[...] * (hf_ref[...] + h_s[...])
    for n in range(RG_BLOCKS):
        out_ref[:, n * LANES:(n + 1) * LANES] = h_s[pl.ds(n, TL, stride=SUBLANES), :].astype(BF16)


def _rg_scans(xin8, gg8, conv_w, conv_b, wa, wi, ba, bi, lam, s, c, b):
    assert c == TL and s % TL == 0
    rows = TL * SUBLANES
    hrows = HALO * SUBLANES
    nlat = s // TL
    t = xin8.shape[0] // SUBLANES
    n_halo = t // HALO
    d = RG_BLOCKS * LANES

    def chunk(reverse):
        return lambda bi_, j: jnp.where(j == 0, (b * s) // TL + bi_,
                                        bi_ * nlat + ((nlat - j) if reverse else (j - 1)))

    fwd, bwd = chunk(False), chunk(True)
    main_f = pl.BlockSpec((rows, LANES), lambda bi_, j: (fwd(bi_, j), 0))
    main_b = pl.BlockSpec((rows, LANES), lambda bi_, j: (bwd(bi_, j), 0))
    prev = pl.BlockSpec((hrows, LANES), lambda bi_, j: (jnp.maximum(fwd(bi_, j) * (TL // HALO) - 1, 0), 0))
    nxt = pl.BlockSpec((hrows, LANES),
                       lambda bi_, j: (jnp.minimum((fwd(bi_, j) + 1) * (TL // HALO), n_halo - 1), 0))
    full = lambda shape: pl.BlockSpec(shape, lambda bi_, j: (0,) * len(shape))
    gate_specs = [full((RG_BLOCKS, LANES, LANES)), full((RG_BLOCKS, LANES, LANES)),
                  full((1, d)), full((1, d)), full((1, d))]
    gate_args = lambda k: [(0.5 * wa[k]).astype(BF16), (0.5 * wi[k]).astype(BF16), 0.5 * ba[k].reshape(1, d),
                           0.5 * bi[k].reshape(1, d), lam[k].reshape(1, d)]
    tmajor = jax.ShapeDtypeStruct(xin8.shape, F32)
    buf = pltpu.VMEM((rows, LANES), F32)
    hf8, xc8 = pl.pallas_call(
        functools.partial(_rg_fwd_kernel, nlat=nlat),
        grid=(b, nlat + 1),
        in_specs=[main_f, prev, nxt, full((CONV_W, SUBLANES, LANES)), full((SUBLANES, LANES))] + gate_specs,
        out_specs=[main_f, main_f],
        out_shape=[tmajor, tmajor],
        scratch_shapes=[pltpu.VMEM((rows + 2 * hrows, LANES), F32), buf, buf, pltpu.VMEM((SUBLANES, LANES), F32)],
        compiler_params=_cp("parallel", "arbitrary"),
        name="rg_scan_fwd",
    )(xin8, xin8, xin8, conv_w.reshape(CONV_W, SUBLANES, LANES), conv_b.reshape(SUBLANES, LANES), *gate_args(0))
    return pl.pallas_call(
        _rg_bwd_kernel,
        grid=(b, nlat + 1),
        in_specs=[main_b] + gate_specs + [main_b, main_b],
        out_specs=pl.BlockSpec((TL, d), lambda bi_, j: (bwd(bi_, j), 0)),
        out_shape=jax.ShapeDtypeStruct((t, d), BF16),
        scratch_shapes=[buf, buf, buf, pltpu.VMEM((SUBLANES, LANES), F32)],
        compiler_params=_cp("parallel", "arbitrary"),
        name="rg_scan_bwd",
    )(xc8, *gate_args(1), hf8, gg8)


def _rglru_mixer(xa, pre, g, mod, w_in, conv_w, conv_b, wa, ba, wi, bi, lam, w_out, route, s, c, b, nrows_out):
    x, gg8, xin8 = _rg_in(xa, pre, g, mod, w_in.astype(BF16), s, b)
    y = _rg_scans(xin8, gg8, conv_w, conv_b, wa, wi, ba, bi, lam, s, c, b)
    return _out_proj(y, x, mod, w_out.astype(BF16), route, nrows_out, s, b)


def _rope_tables(s):
    pos = jnp.arange(s, dtype=F32)
    row = jnp.floor(pos / GRID_W)
    col = pos - row * GRID_W
    n_freq = HEAD_DIM // 4
    inv = ROPE_THETA ** (-jnp.arange(n_freq, dtype=F32) * 2.0 / (HEAD_DIM // 2))
    ar = row[:, None] * inv
    ac = col[:, None] * inv
    cos = jnp.concatenate([jnp.cos(ar), jnp.cos(ar), jnp.cos(ac), jnp.cos(ac)], axis=1)
    sin = jnp.concatenate([-jnp.sin(ar), jnp.sin(ar), -jnp.sin(ac), jnp.sin(ac)], axis=1)
    cos = jnp.concatenate([cos, jnp.ones((TM, HEAD_DIM), F32)], axis=0)
    sin = jnp.concatenate([sin, jnp.zeros((TM, HEAD_DIM), F32)], axis=0)
    return cos, sin


def _qkv_kernel(*refs, has_pre):
    x_ref, pre_refs, (g_ref, mod_ref, w_ref, qg_ref, kg_ref, cos_ref, sin_ref, *outs) = _split_refs(refs, has_pre)
    q_ref, k_ref, v_ref = outs[-3:]
    d = x_ref.shape[-1]
    m = mod_ref[0]
    x = _combined_rows(x_ref, pre_refs)
    if has_pre:
        outs[0][...] = x
    h = _norm_mod(x, g_ref[...], m[:, 0:d], m[:, d:2 * d])
    z = jnp.dot(h.astype(BF16), w_ref[...], preferred_element_type=F32)
    cos = cos_ref[...]
    sin = sin_ref[...]
    src = lax.broadcasted_iota(I32, (HEAD_DIM, HEAD_DIM), 0)
    dst = lax.broadcasted_iota(I32, (HEAD_DIM, HEAD_DIM), 1)
    quarter = HEAD_DIM // 4
    partner_of = jnp.where((dst % (2 * quarter)) < quarter, dst + quarter, dst - quarter)
    swap = jnp.where(src == partner_of, 1.0, 0.0).astype(BF16)

    def head(zc, gain):
        ms = jnp.mean(zc * zc, axis=-1, keepdims=True)
        y = zc * lax.rsqrt(ms + NORM_EPS) * gain
        partner = jnp.dot(y.astype(BF16), swap, preferred_element_type=F32)
        return y * cos + partner * sin

    nq = q_ref.shape[-1] // HEAD_DIM
    nk = k_ref.shape[-1] // HEAD_DIM
    for j in range(nq):
        q_ref[:, j * HEAD_DIM:(j + 1) * HEAD_DIM] = (
            head(z[:, j * HEAD_DIM:(j + 1) * HEAD_DIM], qg_ref[...]) * (HEAD_DIM ** -0.5 * LOG2E)).astype(BF16)
    for j in range(nk):
        c0 = (nq + j) * HEAD_DIM
        k_ref[:, j * HEAD_DIM:(j + 1) * HEAD_DIM] = head(z[:, c0:c0 + HEAD_DIM], kg_ref[...]).astype(BF16)
    v_ref[...] = z[:, (nq + nk) * HEAD_DIM:].astype(BF16)


def _qkv(x, pre, g, mod, w, qg, kg, cos, sin, s, b):
    t, d = x.shape
    nkv = N_KV_HEADS * HEAD_DIM
    n_pos = s // TM
    pre_specs, pre_args = _pre_io(pre, d, TM, s, b)
    row = pl.BlockSpec((TM, d), lambda i: (i, 0))
    outs = pl.pallas_call(
        functools.partial(_qkv_kernel, has_pre=pre is not None),
        grid=(t // TM,),
        in_specs=[row] + pre_specs + [
            pl.BlockSpec((1, d), lambda i: (0, 0)),
            _mod_spec(d, s // TM, b),
            pl.BlockSpec(w.shape, lambda i: (0, 0)),
            pl.BlockSpec((1, HEAD_DIM), lambda i: (0, 0)),
            pl.BlockSpec((1, HEAD_DIM), lambda i: (0, 0)),
            pl.BlockSpec((TM, HEAD_DIM), lambda i: (jnp.where(i < b * n_pos, i % n_pos, n_pos), 0)),
            pl.BlockSpec((TM, HEAD_DIM), lambda i: (jnp.where(i < b * n_pos, i % n_pos, n_pos), 0))],
        out_specs=([row] if pre else []) + [pl.BlockSpec((TM, d), lambda i: (i, 0)),
                                           pl.BlockSpec((TM, nkv), lambda i: (i, 0)),
                                           pl.BlockSpec((TM, nkv), lambda i: (i, 0))],
        out_shape=([jax.ShapeDtypeStruct((t, d), F32)] if pre else [])
        + [jax.ShapeDtypeStruct((t, d), BF16), jax.ShapeDtypeStruct((t, nkv), BF16),
           jax.ShapeDtypeStruct((t, nkv), BF16)],
        compiler_params=_cp("parallel"),
        name="qkv_proj",
    )(x, *pre_args, g, mod, w, qg, kg, cos, sin)
    return tuple(outs) if pre else (x, *outs)


def _attn_kernel(q_ref, kc_ref, vc_ref, *rest, n_lat):
    if n_lat:
        kl_ref, vl_ref, o_ref, s_scr, vaug = rest
    else:
        o_ref, s_scr, vaug = rest
    n_ctx = kc_ref.shape[0]
    tq = q_ref.shape[0]

    def fill_values():
        vaug[:, HEAD_DIM:] = jnp.ones((n_ctx + n_lat, HEAD_DIM), BF16)
        vaug[0:n_ctx, 0:HEAD_DIM] = vc_ref[...]
        if n_lat:
            vaug[n_ctx:, 0:HEAD_DIM] = vl_ref[...]

    if n_lat:
        pl.when(pl.program_id(2) == 0)(fill_values)
    else:
        fill_values()

    chunks = [(0, n_ctx)] + [(n_ctx + j, ATT_KC) for j in range(0, n_lat, ATT_KC)]
    nt = (((1,), (1,)), ((), ()))
    q_all = jnp.concatenate([q_ref[:, g * HEAD_DIM:(g + 1) * HEAD_DIM] for g in range(GQA_GROUP)], axis=0)
    m_part = jnp.full((GQA_GROUP * tq, LANES), -jnp.inf, F32)
    for off, size in chunks:
        keys = kc_ref[...] if off == 0 else kl_ref[off - n_ctx:off - n_ctx + size, :]
        sc = lax.dot_general(q_all, keys, nt, preferred_element_type=F32)
        s_scr[:, off:off + size] = sc
        for j in range(0, size, LANES):
            m_part = jnp.maximum(m_part, sc[:, j:j + LANES])
    m_row = jnp.max(m_part, axis=-1, keepdims=True)
    hr = GQA_GROUP * tq // 2
    acc = [jnp.zeros((hr, 2 * HEAD_DIM), F32), jnp.zeros((hr, 2 * HEAD_DIM), F32)]
    for off, size in chunks:
        for r in range(2):
            rows = slice(r * hr, (r + 1) * hr)
            p = jnp.exp2((s_scr[rows, off:off + size] - m_row[rows]).astype(BF16))
            acc[r] = acc[r] + jnp.dot(p, vaug[off:off + size, :], preferred_element_type=F32)
    for r in range(2):
        out = (acc[r][:, :HEAD_DIM] / acc[r][:, HEAD_DIM:]).astype(BF16)
        for j in range(GQA_GROUP // 2):
            g = r * (GQA_GROUP // 2) + j
            o_ref[:, g * HEAD_DIM:(g + 1) * HEAD_DIM] = out[j * tq:(j + 1) * tq]


def _attn_ctx_kernel(q_ref, kc_ref, vc_ref, o_all_ref, o_ref, s_scr, vaug):
    del o_all_ref
    _attn_kernel(q_ref, kc_ref, vc_ref, o_ref, s_scr, vaug, n_lat=0)


def _attention(q, k, v, s, c, b):
    t, d = q.shape
    gw = GQA_GROUP * HEAD_DIM
    tq = TM
    nq = s // tq
    assert s % ATT_KC == 0
    ctx_blk = lambda bi, h, *_: ((b * s) // c + bi, h)
    o_lat = pl.pallas_call(
        functools.partial(_attn_kernel, n_lat=s),
        grid=(b, N_KV_HEADS, nq),
        in_specs=[pl.BlockSpec((tq, gw), lambda bi, h, i: (bi * nq + i, h)),
                  pl.BlockSpec((c, HEAD_DIM), ctx_blk),
                  pl.BlockSpec((c, HEAD_DIM), ctx_blk),
                  pl.BlockSpec((s, HEAD_DIM), lambda bi, h, i: (bi, h)),
                  pl.BlockSpec((s, HEAD_DIM), lambda bi, h, i: (bi, h))],
        out_specs=pl.BlockSpec((tq, gw), lambda bi, h, i: (bi * nq + i, h)),
        out_shape=jax.ShapeDtypeStruct((t, d), BF16),
        scratch_shapes=[pltpu.VMEM((GQA_GROUP * tq, c + s), F32), pltpu.VMEM((c + s, 2 * HEAD_DIM), BF16)],
        compiler_params=_cp("parallel", "parallel", "arbitrary"),
        name="attn_lat",
    )(q, k, v, k, v)
    return pl.pallas_call(
        _attn_ctx_kernel,
        grid=(b, N_KV_HEADS),
        in_specs=[pl.BlockSpec((c, gw), ctx_blk),
                  pl.BlockSpec((c, HEAD_DIM), ctx_blk),
                  pl.BlockSpec((c, HEAD_DIM), ctx_blk),
                  pl.BlockSpec(memory_space=pl.ANY)],
        out_specs=pl.BlockSpec((c, gw), ctx_blk),
        out_shape=jax.ShapeDtypeStruct((t, d), BF16),
        scratch_shapes=[pltpu.VMEM((GQA_GROUP * c, c), F32), pltpu.VMEM((c, 2 * HEAD_DIM), BF16)],
        input_output_aliases={3: 0},
        compiler_params=_cp("parallel", "parallel"),
        name="attn_ctx",
    )(q, k, v, o_lat)


def _attention_mixer(xa, pre, g, mod, w_qkv, qg, kg, w_o, route, s, c, b):
    cos, sin = _rope_tables(s)
    x, q, k, v = _qkv(xa, pre, g, mod, w_qkv.astype(BF16), qg.reshape(1, -1), kg.reshape(1, -1), cos, sin, s, b)
    o = _attention(q, k, v, s, c, b)
    return _out_proj(o, x, mod, w_o.astype(BF16), route, x.shape[0], s, b)


def _gmlp_kernel(*refs, has_pre):
    x_ref, pre_refs, rest = _split_refs(refs, has_pre)
    g_ref, mod_ref, w_in_ref, lng_ref, lnb_ref, ws_ref, bs_ref, w_out_ref = rest[:8]
    route_in, (o_ref, *route_out, uv_ref, cnt_s) = rest[8:12], rest[12:]
    d = x_ref.shape[-1]
    dcm = lng_ref.shape[-1]
    gw = dcm // CM_GROUPS
    x = _combined_rows(x_ref, pre_refs)
    m = mod_ref[0]
    h = _norm_mod(x, g_ref[...], m[:, 0:d], m[:, d:2 * d])
    z = jax.nn.gelu(jnp.dot(h.astype(BF16), w_in_ref[...], preferred_element_type=F32))
    u = z[:, :dcm]
    v = z[:, dcm:]
    mu = jnp.mean(v, axis=-1, keepdims=True)
    vc = v - mu
    var = jnp.mean(vc * vc, axis=-1, keepdims=True)
    vn = (vc * lax.rsqrt(var + NORM_EPS) * lng_ref[...] + lnb_ref[...]).astype(BF16)
    for ck in range(x.shape[0] // CHUNK):
        rows = slice(ck * CHUNK, (ck + 1) * CHUNK)
        for gi in range(CM_GROUPS):
            cols = slice(gi * gw, (gi + 1) * gw)
            mix = jnp.dot(ws_ref[gi], vn[rows, cols], preferred_element_type=F32) + bs_ref[:, gi:gi + 1]
            uv_ref[rows, cols] = (u[rows, cols] * mix).astype(BF16)
    y = jnp.dot(uv_ref[...], w_out_ref[...], preferred_element_type=F32)
    x1 = x + m[:, 2 * d:3 * d] * y
    o_ref[...] = x1
    _route_rows(x1, m, *route_in, *route_out, cnt_s)


def _gmlp_mixer(xa, pre, g, mod, w_in, ln_g, ln_b, w_s, b_s, w_out, route, s, b):
    t, d = xa.shape
    dcm = ln_g.shape[-1]
    full = lambda shape: pl.BlockSpec(shape, lambda i: (0,) * len(shape))
    pre_specs, pre_args = _pre_io(pre, d, TG, s, b)
    r_in, r_out, r_shape, r_scratch = _route_io(d, t, TG)
    x1, *routed = pl.pallas_call(
        functools.partial(_gmlp_kernel, has_pre=pre is not None),
        grid=(t // TG,),
        in_specs=[pl.BlockSpec((TG, d), lambda i: (i, 0))] + pre_specs + [
            full((1, d)),
            _mod_spec(d, s // TG, b),
            full((d, 2 * dcm)), full((1, dcm)), full((1, dcm)),
            full((CM_GROUPS, CHUNK, CHUNK)), full((CHUNK, CM_GROUPS)), full((dcm, d))] + r_in,
        out_specs=[pl.BlockSpec((TG, d), lambda i: (i, 0))] + r_out,
        out_shape=[jax.ShapeDtypeStruct((t, d), F32)] + r_shape,
        scratch_shapes=[pltpu.VMEM((TG, dcm), BF16)] + r_scratch,
        compiler_params=_cp("arbitrary"),
        name="gmlp",
    )(xa, *pre_args, g, mod, w_in.astype(BF16), ln_g.reshape(1, dcm), ln_b.reshape(1, dcm),
      w_s.astype(BF16), b_s.T, w_out.astype(BF16), *route)
    return x1, routed


def _route_rows(x1, m, g_ref, wrh_ref, wrl_ref, br_ref, hf_ref, rt_ref, ew_ref, cnt_ref, cnt_s):
    tm, d = x1.shape

    @pl.when(pl.program_id(0) == 0)
    def _():
        cnt_s[...] = jnp.zeros_like(cnt_s)

    hf = _norm_mod(x1, g_ref[...], m[:, 3 * d:4 * d], m[:, 4 * d:5 * d])
    hf_ref[...] = _pack_bf16_pairs(hf)
    hf_hi = hf.astype(BF16)
    hf_lo = (hf - hf_hi.astype(F32)).astype(BF16)
    logits = (jnp.dot(hf_hi, wrh_ref[...], preferred_element_type=F32)
              + jnp.dot(hf_lo, wrh_ref[...], preferred_element_type=F32)
              + jnp.dot(hf_hi, wrl_ref[...], preferred_element_type=F32)) + br_ref[...]
    lane = lax.broadcasted_iota(I32, logits.shape, 1)
    neg = -jnp.inf
    gl = jnp.where(lane < N_GROUPS, logits, neg)
    gmax = jnp.max(gl, axis=-1, keepdims=True)
    gsel = jnp.min(jnp.where(gl == gmax, lane, LANES), axis=-1, keepdims=True)
    gate_g = 1.0 / jnp.sum(jnp.exp(gl - gmax), axis=-1, keepdims=True)
    lo = N_GROUPS + gsel * EXPERTS_PER_GROUP
    el = jnp.where(jnp.logical_and(lane >= lo, lane < lo + EXPERTS_PER_GROUP), logits, neg)
    v1 = jnp.max(el, axis=-1, keepdims=True)
    i1 = jnp.min(jnp.where(el == v1, lane, LANES), axis=-1, keepdims=True)
    el2 = jnp.where(lane == i1, neg, el)
    v2 = jnp.max(el2, axis=-1, keepdims=True)
    i2 = jnp.min(jnp.where(el2 == v2, lane, LANES), axis=-1, keepdims=True)
    e21 = jnp.exp(v2 - v1)
    w1 = gate_g / (1.0 + e21)
    w2 = w1 * e21
    ew_ref[...] = jnp.where(lane == 0, w1, jnp.where(lane == 1, w2, 0.0))

    oh1 = lane == i1
    oh2 = lane == i2
    above = (lax.broadcasted_iota(I32, (tm, tm), 1) < lax.broadcasted_iota(I32, (tm, tm), 0)).astype(BF16)
    pre1 = jnp.dot(above, oh1.astype(BF16), preferred_element_type=F32)
    pre2 = jnp.dot(above, oh2.astype(BF16), preferred_element_type=F32)
    tot1 = jnp.sum(oh1.astype(F32), axis=0, keepdims=True)
    tot2 = jnp.sum(oh2.astype(F32), axis=0, keepdims=True)
    cnt = cnt_s[...]
    rank1 = jnp.sum(jnp.where(oh1, cnt + pre1, 0.0), axis=-1, keepdims=True).astype(I32)
    rank2 = jnp.sum(jnp.where(oh2, cnt + tot1 + pre2, 0.0), axis=-1, keepdims=True).astype(I32)
    cnt = cnt + tot1 + tot2
    cnt_s[...] = cnt
    cnt_ref[...] = jnp.broadcast_to(cnt, cnt_ref.shape).astype(I32)
    rt = jnp.where(lane == 0, i1 - N_GROUPS, jnp.where(lane == 1, i2 - N_GROUPS,
                   jnp.where(lane == 2, rank1, jnp.where(lane == 3, rank2, 0))))
    rt_ref[...] = rt.T[:SUBLANES]


def _route_params(g_ffn, w_group, b_group, w_router, b_router):
    d = w_group.shape[0]
    pad = LANES - N_GROUPS - N_EXPERTS
    wr = jnp.concatenate([w_group, w_router.reshape(d, N_EXPERTS), jnp.zeros((d, pad), F32)], axis=1)
    br = jnp.concatenate([b_group, b_router.reshape(N_EXPERTS), jnp.zeros((pad,), F32)]).reshape(1, LANES)
    wr_hi = wr.astype(BF16)
    wr_lo = (wr - wr_hi.astype(F32)).astype(BF16)
    return g_ffn.reshape(1, d), wr_hi, wr_lo, br


def _route_io(d, nrows, tr):
    const = lambda shape: pl.BlockSpec(shape, lambda i: (0, 0))
    row = lambda w: pl.BlockSpec((tr, w), lambda i: (i, 0))
    in_specs = [const((1, d)), const((d, LANES)), const((d, LANES)), const((1, LANES))]
    out_specs = [row(d // 2), pl.BlockSpec((SUBLANES, tr), lambda i: (0, i)), row(LANES), const((SUBLANES, LANES))]
    out_shape = [jax.ShapeDtypeStruct((nrows, d // 2), I32), jax.ShapeDtypeStruct((SUBLANES, nrows), I32),
                 jax.ShapeDtypeStruct((nrows, LANES), F32), jax.ShapeDtypeStruct((SUBLANES, LANES), I32)]
    return in_specs, out_specs, out_shape, [pltpu.VMEM((1, LANES), F32)]


def _plan_kernel(cnt_ref, rt_ref, pos_ref, blk_e_ref, n_used_ref, first_ref, slot_ref, next_ref,
                 start_s, end_s, nxt_s, *, bm):
    n_blk = blk_e_ref.shape[0]
    acc = jnp.int32(0)
    for e in range(N_EXPERTS):
        start_s[e] = acc
        acc = acc + (cnt_ref[0, N_GROUPS + e] + (bm - 1)) // bm * bm
        end_s[e] = acc
    n_used = acc // bm
    n_used_ref[0] = n_used
    nxt = jnp.int32(-1)
    for e in reversed(range(N_EXPERTS)):
        nxt_s[e] = nxt
        nxt = jnp.where(cnt_ref[0, N_GROUPS + e] > 0, e, nxt)

    def block(i, carry):
        prev_e, runs = carry
        row = jnp.minimum(i, n_used - 1) * bm
        e = lax.while_loop(lambda v: jnp.logical_and(v < N_EXPERTS - 1, end_s[v] <= row), lambda v: v + 1,
                           jnp.maximum(prev_e, 0))
        first = jnp.logical_and(i < n_used, prev_e != e)
        runs = runs + first.astype(I32)
        blk_e_ref[i] = e
        first_ref[i] = first.astype(I32)
        slot_ref[i] = (runs - 1) % 2
        next_ref[i] = nxt_s[e]
        return e, runs

    lax.fori_loop(0, n_blk, block, (jnp.int32(-1), jnp.int32(0)))

    rt = rt_ref[...]
    start_of = jnp.zeros_like(rt)
    for e in range(N_EXPERTS):
        start_of = jnp.where(rt == e, start_s[e], start_of)
    pos_ref[...] = start_of + pltpu.roll(rt, SUBLANES - 2, 0)


def _dispatch_plan(rt, cnt, bm):
    n_tok = rt.shape[1]
    n_rows = 2 * n_tok + N_EXPERTS * bm
    n_blk = n_rows // bm
    smem = pl.BlockSpec(memory_space=pltpu.SMEM)
    vec = lambda n: jax.ShapeDtypeStruct((n,), I32)
    pos, blk_e, n_used, first, slot, nxt = pl.pallas_call(
        functools.partial(_plan_kernel, bm=bm),
        in_specs=[smem, pl.BlockSpec(memory_space=pltpu.VMEM)],
        out_specs=[pl.BlockSpec(memory_space=pltpu.VMEM), smem, smem, smem, smem, smem],
        out_shape=[jax.ShapeDtypeStruct(rt.shape, I32), vec(n_blk), vec(1), vec(n_blk), vec(n_blk), vec(n_blk)],
        scratch_shapes=[pltpu.SMEM((N_EXPERTS,), I32)] * 3,
        name="moe_plan",
    )(cnt, rt)
    return pos, (blk_e, n_used, first, slot, nxt), n_rows


def _sc_mesh():
    return plsc.VectorSubcoreMesh(core_axis_name="c", subcore_axis_name="s")


def _sc_worker_base(per_worker):
    return (lax.axis_index("s") * SC_CORES + lax.axis_index("c")) * per_worker


def _sc_dispatch(hf, pos, n_rows):
    t, d = hf.shape
    per_w = t // SC_WORKERS
    ch = SC_CHUNK
    n_ck = per_w // ch
    assert per_w * SC_WORKERS == t and n_ck * ch == per_w

    @functools.partial(
        pl.kernel, mesh=_sc_mesh(), out_type=jax.ShapeDtypeStruct((n_rows, d), hf.dtype),
        scratch_types=[pltpu.VMEM((per_w,), I32), pltpu.VMEM((per_w,), I32), pltpu.VMEM((2, ch, d), hf.dtype),
                       pltpu.SemaphoreType.DMA((2,)), pltpu.SemaphoreType.DMA((2,)), pltpu.SemaphoreType.DMA((2,))])
    def dispatch(hf_hbm, p_hbm, out_hbm, i0_v, i1_v, rows_v, sem_in, sem_s0, sem_s1):
        base = pl.multiple_of(_sc_worker_base(per_w), SUBLANES)
        pltpu.sync_copy(p_hbm.at[pl.ds(base, per_w)], i0_v)
        pltpu.sync_copy(p_hbm.at[pl.ds(pl.multiple_of(t + base, SUBLANES), per_w)], i1_v)

        def load(ck):
            return pltpu.make_async_copy(hf_hbm.at[pl.ds(base + ck * ch, ch)], rows_v.at[ck % 2], sem_in.at[ck % 2])

        def scatters(ck):
            src = rows_v.at[ck % 2]
            return (pltpu.make_async_copy(src, out_hbm.at[i0_v.at[pl.ds(ck * ch, ch)]], sem_s0.at[ck % 2]),
                    pltpu.make_async_copy(src, out_hbm.at[i1_v.at[pl.ds(ck * ch, ch)]], sem_s1.at[ck % 2]))

        load(0).start()
        for ck in range(n_ck):
            load(ck).wait()
            if ck + 1 < n_ck:
                if ck >= 1:
                    for cp in scatters(ck - 1):
                        cp.wait()
                load(ck + 1).start()
            for cp in scatters(ck):
                cp.start()
        for ck in range(max(n_ck - 2, 0), n_ck):
            for cp in scatters(ck):
                cp.wait()

    return dispatch(hf, pos)


def _sc_gather(rows, idx):
    n = idx.shape[0]
    d = rows.shape[1]
    per_w = n // SC_WORKERS
    ch = 2 * SC_CHUNK
    n_ck = per_w // ch
    assert per_w * SC_WORKERS == n and n_ck * ch == per_w

    @functools.partial(
        pl.kernel, mesh=_sc_mesh(), out_type=jax.ShapeDtypeStruct((n, d), rows.dtype),
        scratch_types=[pltpu.VMEM((per_w,), I32), pltpu.VMEM((2, ch, d), rows.dtype),
                       pltpu.SemaphoreType.DMA((2,)), pltpu.SemaphoreType.DMA((2,))])
    def gather(rows_hbm, i_hbm, out_hbm, i_v, buf, sem_g, sem_w):
        base = pl.multiple_of(_sc_worker_base(per_w), SUBLANES)
        pltpu.sync_copy(i_hbm.at[pl.ds(base, per_w)], i_v)

        def fetch(ck):
            return pltpu.make_async_copy(rows_hbm.at[i_v.at[pl.ds(ck * ch, ch)]], buf.at[ck % 2], sem_g.at[ck % 2])

        def write(ck):
            return pltpu.make_async_copy(buf.at[ck % 2], out_hbm.at[pl.ds(base + ck * ch, ch)], sem_w.at[ck % 2])

        fetch(0).start()
        for ck in range(n_ck):
            fetch(ck).wait()
            if ck + 1 < n_ck:
                if ck >= 1:
                    write(ck - 1).wait()
                fetch(ck + 1).start()
            write(ck).start()
        for ck in range(max(n_ck - 2, 0), n_ck):
            write(ck).wait()

    return gather(rows, idx)


def _expert_kernel(blk_e_ref, n_used_ref, first_ref, slot_ref, next_ref, x_ref, wg_hbm, wu_hbm, wd_hbm, y_ref,
                   wgf, wuf, wdf, wgb, wub, wdb, sem, *, e_base):
    n_used = n_used_ref[0]
    bm = x_ref.shape[0] // MOE_SUB

    def weight_copies(e, slot):
        return (pltpu.make_async_copy(wg_hbm.at[e_base + e], wgf.at[slot], sem.at[slot, 0]),
                pltpu.make_async_copy(wu_hbm.at[e_base + e], wuf.at[slot], sem.at[slot, 1]),
                pltpu.make_async_copy(wd_hbm.at[e_base + e], wdf.at[slot], sem.at[slot, 2]))

    @pl.when(pl.program_id(0) == 0)
    def _():
        for cp in weight_copies(blk_e_ref[0], 0):
            cp.start()

    for j in range(MOE_SUB):
        blk = pl.program_id(0) * MOE_SUB + j
        rows = slice(j * bm, (j + 1) * bm)

        @pl.when(jnp.logical_and(blk < n_used, first_ref[blk] == 1))
        def _():
            slot = slot_ref[blk]
            for cp in weight_copies(blk_e_ref[blk], slot):
                cp.wait()
            nxt = next_ref[blk]

            @pl.when(nxt >= 0)
            def _():
                for cp in weight_copies(nxt, 1 - slot):
                    cp.start()

            wgb[...] = wgf[slot].astype(BF16)
            wub[...] = wuf[slot].astype(BF16)
            wdb[...] = wdf[slot].astype(BF16)

        @pl.when(blk < n_used)
        def _():
            x_hi, x_lo = _unpack_bf16_pairs(x_ref[rows, :])
            xb = jnp.concatenate([x_hi.astype(BF16), x_lo.astype(BF16)], axis=1)
            gt = jnp.dot(xb, wgb[...], preferred_element_type=F32)
            up = jnp.dot(xb, wub[...], preferred_element_type=F32)
            act = (gt * _sigmoid(gt) * up).astype(BF16)
            y_ref[rows, :] = _pack_bf16_pairs(jnp.dot(act, wdb[...], preferred_element_type=F32))


def _experts(x_rows, plan, w_gate, w_up, w_down, layer):
    n_rows, dp = x_rows.shape
    depth, n_e, d, de = w_gate.shape
    step_rows = MOE_SUB * MOE_BM
    assert n_rows % step_rows == 0
    any_spec = pl.BlockSpec(memory_space=pl.ANY)
    last_used = lambda i, be, nu, *_: (jnp.minimum(i, (nu[0] - 1) // MOE_SUB), 0)
    grid_spec = pltpu.PrefetchScalarGridSpec(
        num_scalar_prefetch=5,
        grid=(n_rows // step_rows,),
        in_specs=[pl.BlockSpec((step_rows, dp), last_used), any_spec, any_spec, any_spec],
        out_specs=pl.BlockSpec((step_rows, dp), last_used),
        scratch_shapes=[pltpu.VMEM((2, d, de), F32), pltpu.VMEM((2, d, de), F32), pltpu.VMEM((2, de, d), F32),
                        pltpu.VMEM((d, de), BF16), pltpu.VMEM((d, de), BF16), pltpu.VMEM((de, d), BF16),
                        pltpu.SemaphoreType.DMA((2, 3))],
    )
    return pl.pallas_call(
        functools.partial(_expert_kernel, e_base=layer * n_e),
        grid_spec=grid_spec,
        out_shape=jax.ShapeDtypeStruct((n_rows, dp), I32),
        compiler_params=_cp("arbitrary"),
        name="moe_experts",
    )(*plan, x_rows, w_gate.reshape(depth * n_e, d, de), w_up.reshape(depth * n_e, d, de),
      w_down.reshape(depth * n_e, de, d))


def _combine_kernel(x_ref, *rest):
    *pre_refs, o_ref = rest
    o_ref[...] = _combined_rows(x_ref, pre_refs)


def _combine(x, pre, s, b):
    nrows, d = x.shape
    pre_specs, pre_args = _pre_io(pre, d, TR, s, b)
    return pl.pallas_call(
        _combine_kernel,
        grid=(nrows // TR,),
        in_specs=[pl.BlockSpec((TR, d), lambda i: (i, 0))] + pre_specs,
        out_specs=pl.BlockSpec((TR, d), lambda i: (i, 0)),
        out_shape=jax.ShapeDtypeStruct((nrows, d), F32),
        compiler_params=_cp("parallel"),
        name="moe_combine",
    )(x, *pre_args)


def _expert_outputs(routed, w_gate, w_up, w_down, layer):
    hf, rt, _, cnt = routed
    pos, plan, n_rows = _dispatch_plan(rt, cnt, MOE_BM)
    pos = pos[0:2].reshape(-1)
    x_rows = _sc_dispatch(hf, pos, n_rows)
    y_rows = _experts(x_rows, plan, w_gate, w_up, w_down, layer)
    return _sc_gather(y_rows, pos)


def kernel(x, c, ctx, c_ctx, ada_w, ada_b, norm_mix_g, norm_ffn_g, rg_w_in, rg_conv_w, rg_conv_b, rg_wa, rg_ba, rg_wi, rg_bi, rg_lambda, rg_w_out, at_w_qkv, at_q_g, at_k_g, at_w_o, cm_w_in, cm_ln_g, cm_ln_b, cm_w_s, cm_b_s, cm_w_out, moe_w_group, moe_b_group, moe_w_router, moe_b_router, moe_w_gate, moe_w_up, moe_w_down):
    b, s, d = x.shape
    cl = ctx.shape[1]
    depth = ada_w.shape[0]
    n_lat = b * s
    assert b < SUBLANES and s % max(TI, TR, TG) == 0 and (b * cl) % max(TI, TR, TG) == 0 and cl % TM == 0
    assert d == RG_BLOCKS * LANES

    cin = jnp.concatenate([c, c_ctx[None, :], jnp.zeros((SUBLANES - b - 1, d), F32)], axis=0)
    mod_all = _ada_table(cin, ada_w, ada_b).reshape(depth, SUBLANES, 1, N_MOD * d)
    tok = jnp.concatenate([x.reshape(n_lat, d), ctx.reshape(b * cl, d)], axis=0)

    xa, pre = tok, None
    for l in range(depth):
        kind = l % 3
        j = l // 3
        last = l == depth - 1
        mod = mod_all[l]
        g_mix = norm_mix_g[l].reshape(1, d)
        route = _route_params(norm_ffn_g[l], moe_w_group[l], moe_b_group[l], moe_w_router[l], moe_b_router[l])
        if kind == 0:
            x1, routed = _rglru_mixer(xa, pre, g_mix, mod, rg_w_in[j], rg_conv_w[j], rg_conv_b[j], rg_wa[j],
                                      rg_ba[j], rg_wi[j], rg_bi[j], rg_lambda[j], rg_w_out[j], route, s, cl, b,
                                      n_lat if last else xa.shape[0])
        elif kind == 1:
            x1, routed = _attention_mixer(xa, pre, g_mix, mod, at_w_qkv[j], at_q_g[j], at_k_g[j], at_w_o[j],
                                          route, s, cl, b)
        else:
            x1, routed = _gmlp_mixer(xa, pre, g_mix, mod, cm_w_in[j], cm_ln_g[j], cm_ln_b[j], cm_w_s[j],
                                     cm_b_s[j], cm_w_out[j], route, s, b)
        y01 = _expert_outputs(routed, moe_w_gate, moe_w_up, moe_w_down, l)
        xa, pre = x1, (y01, routed[2], mod)
    return _combine(xa, pre, s, b)[:n_lat].reshape(b, s, d)
```

```python
import functools

import jax
import jax.numpy as jnp
from jax import lax
from jax.experimental import pallas as pl
from jax.experimental.pallas import tpu as pltpu
from jax.experimental.pallas import tpu_sc as plsc

F32 = jnp.float32
BF16 = jnp.bfloat16
I32 = jnp.int32
U32 = jnp.uint32

NORM_EPS = 1e-6
N_MOD = 6
GRID_W = 64
RG_BLOCKS = 8
CONV_W = 4
RG_C = 8.0
HEAD_DIM = 128
N_KV_HEADS = 2
GQA_GROUP = 4
ROPE_THETA = 10000.0
CHUNK = 128
CM_GROUPS = 8
N_GROUPS = 4
EXPERTS_PER_GROUP = 8
N_EXPERTS = N_GROUPS * EXPERTS_PER_GROUP

LANES = 128
SUBLANES = 8
TM = 256
TI = 512
TR = 1024
TG = 512
TL = 256
HALO = 8
ATT_KC = 512
LOG2E = 1.4426950408889634
MOE_BM = 256
MOE_SUB = 4
SC_CORES = 2
SC_WORKERS = 32
SC_CHUNK = 32
VMEM_LIMIT = 52 * 2**20


def _cp(*sem):
    return pltpu.CompilerParams(dimension_semantics=sem, vmem_limit_bytes=VMEM_LIMIT)


def _norm_mod(x, g, shift, scale):
    ms = jnp.mean(x * x, axis=-1, keepdims=True)
    y = x * lax.rsqrt(ms + NORM_EPS) * g
    return y * (1.0 + scale) + shift


def _sigmoid(x):
    return 0.5 * jnp.tanh(0.5 * x) + 0.5


def _pack_bf16_pairs(x):
    h = x.shape[-1] // 2
    hi = lax.bitcast_convert_type(x[:, :h].astype(BF16).astype(F32), U32)
    lo = lax.bitcast_convert_type(x[:, h:].astype(BF16).astype(F32), U32)
    return lax.bitcast_convert_type(hi | (lo >> 16), I32)


def _unpack_bf16_pairs(w):
    u = lax.bitcast_convert_type(w, U32)
    hi = lax.bitcast_convert_type(u & jnp.uint32(0xFFFF0000), F32)
    lo = lax.bitcast_convert_type(u << 16, F32)
    return hi, lo


def _resident_bf16(w_ref, wb_ref):
    @pl.when(pl.program_id(0) == 0)
    def _():
        wb_ref[...] = w_ref[...].astype(BF16)

    return wb_ref[...]


def _combined_rows(x_ref, pre_refs):
    if not pre_refs:
        return x_ref[...]
    y0_ref, y1_ref, ew_ref, modp_ref = pre_refs
    d = x_ref.shape[1]
    ew = ew_ref[...]
    y0_hi, y0_lo = _unpack_bf16_pairs(y0_ref[...])
    y1_hi, y1_lo = _unpack_bf16_pairs(y1_ref[...])
    y = jnp.concatenate([ew[:, 0:1] * y0_hi + ew[:, 1:2] * y1_hi, ew[:, 0:1] * y0_lo + ew[:, 1:2] * y1_lo], axis=1)
    return x_ref[...] + modp_ref[0][:, 5 * d:6 * d] * y


def _pre_io(pre, d, tr, s, b):
    if pre is None:
        return [], []
    y01, ew, mod_prev = pre
    nb = y01.shape[0] // 2 // tr
    specs = [pl.BlockSpec((tr, d // 2), lambda i: (i, 0)), pl.BlockSpec((tr, d // 2), lambda i: (i + nb, 0)),
             pl.BlockSpec((tr, LANES), lambda i: (i, 0)), _mod_spec(d, s // tr, b)]
    return specs, [y01, y01, ew, mod_prev]


def _split_refs(refs, has_pre):
    return (refs[0], refs[1:5], refs[5:]) if has_pre else (refs[0], (), refs[1:])


def _mod_spec(d, rows_per_sample, n_samples):
    return pl.BlockSpec((1, 1, N_MOD * d),
                        lambda i, *_: (jnp.minimum(i // rows_per_sample, n_samples), 0, 0))


def _ada_kernel(c_ref, w_ref, b_ref, o_ref):
    cin = c_ref[...]
    act = cin * jax.nn.sigmoid(cin)
    w = w_ref[0]
    w_hi = w.astype(BF16)
    w_lo = (w - w_hi.astype(F32)).astype(BF16)
    a_hi = act.astype(BF16)
    a_lo = (act - a_hi.astype(F32)).astype(BF16)
    o_ref[0] = (jnp.dot(a_hi, w_hi, preferred_element_type=F32) + jnp.dot(a_lo, w_hi, preferred_element_type=F32)
                + jnp.dot(a_hi, w_lo, preferred_element_type=F32)) + b_ref[0]


def _ada_table(cin, ada_w, ada_b):
    depth, d, n = ada_w.shape
    tn = 2 * d
    return pl.pallas_call(
        _ada_kernel,
        grid=(depth, n // tn),
        in_specs=[pl.BlockSpec((SUBLANES, d), lambda l, j: (0, 0)),
                  pl.BlockSpec((1, d, tn), lambda l, j: (l, 0, j)),
                  pl.BlockSpec((1, 1, tn), lambda l, j: (l, 0, j))],
        out_specs=pl.BlockSpec((1, SUBLANES, tn), lambda l, j: (l, 0, j)),
        out_shape=jax.ShapeDtypeStruct((depth, SUBLANES, n), F32),
        compiler_params=_cp("parallel", "parallel"),
        name="ada_table",
    )(cin, ada_w, ada_b.reshape(depth, 1, n))


def _out_kernel(y_ref, x_ref, mod_ref, w_ref, *rest):
    route_in, (o_ref, *route_out, wb_ref) = rest[:4], rest[4:]
    d = x_ref.shape[-1]
    m = mod_ref[0]
    y = jnp.dot(y_ref[...].astype(BF16), _resident_bf16(w_ref, wb_ref), preferred_element_type=F32)
    x1 = x_ref[...] + m[:, 2 * d:3 * d] * y
    o_ref[...] = x1
    _route_rows(x1, m, *route_in, *route_out)


def _out_proj(y, x, mod, w, route, nrows, s, b):
    d = x.shape[-1]
    k = y.shape[-1]
    r_in, r_out, r_shape, r_scratch = _route_io(d, nrows, TR)
    x1, *routed = pl.pallas_call(
        _out_kernel,
        grid=(nrows // TR,),
        in_specs=[pl.BlockSpec((TR, k), lambda i: (i, 0)),
                  pl.BlockSpec((TR, d), lambda i: (i, 0)),
                  _mod_spec(d, s // TR, b),
                  pl.BlockSpec((k, d), lambda i: (0, 0))] + r_in,
        out_specs=[pl.BlockSpec((TR, d), lambda i: (i, 0))] + r_out,
        out_shape=[jax.ShapeDtypeStruct((nrows, d), F32)] + r_shape,
        scratch_shapes=r_scratch + [pltpu.VMEM((k, d), BF16)],
        compiler_params=_cp("arbitrary"),
        name="out_proj",
    )(y, x, mod, w, *route)
    return x1, routed


def _rg_in_kernel(*refs, has_pre):
    x_ref, pre_refs, (g_ref, mod_ref, w_ref, *outs, wb_ref) = _split_refs(refs, has_pre)
    gg_ref, xin_ref = outs[-2:]
    d = x_ref.shape[-1]
    m = mod_ref[0]
    x = _combined_rows(x_ref, pre_refs)
    if has_pre:
        outs[0][...] = x
    h = _norm_mod(x, g_ref[...], m[:, 0:d], m[:, d:2 * d])
    z = jnp.dot(h.astype(BF16), _resident_bf16(w_ref, wb_ref), preferred_element_type=F32)
    tm = x_ref.shape[0]
    for n in range(d // LANES):
        cols = slice(n * LANES, (n + 1) * LANES)
        gg_ref[pl.ds(n, tm, stride=SUBLANES), :] = jax.nn.gelu(z[:, cols])
        xin_ref[pl.ds(n, tm, stride=SUBLANES), :] = z[:, d + n * LANES:d + (n + 1) * LANES]


def _rg_in(x, pre, g, mod, w, s, b):
    t, d = x.shape
    assert d == SUBLANES * LANES
    pre_specs, pre_args = _pre_io(pre, d, TI, s, b)
    row = pl.BlockSpec((TI, d), lambda i: (i, 0))
    tmajor = pl.BlockSpec((TI * SUBLANES, LANES), lambda i: (i, 0))
    outs = pl.pallas_call(
        functools.partial(_rg_in_kernel, has_pre=pre is not None),
        grid=(t // TI,),
        in_specs=[row] + pre_specs + [pl.BlockSpec((1, d), lambda i: (0, 0)), _mod_spec(d, s // TI, b),
                                      pl.BlockSpec((d, 2 * d), lambda i: (0, 0))],
        out_specs=([row] if pre else []) + [tmajor, tmajor],
        out_shape=([jax.ShapeDtypeStruct((t, d), F32)] if pre else [])
        + [jax.ShapeDtypeStruct((t * SUBLANES, LANES), F32)] * 2,
        scratch_shapes=[pltpu.VMEM(w.shape, BF16)],
        compiler_params=_cp("arbitrary"),
        name="rg_in",
    )(x, *pre_args, g, mod, w)
    return (outs[0], outs[1], outs[2]) if pre else (x, outs[0], outs[1])


def _rg_gates_and_scan(xc, wa_ref, wi_ref, ba_ref, bi_ref, lam_ref, a_s, b_s, h_dst, hcar, reverse):
    @pl.when(pl.program_id(1) == 0)
    def _():
        hcar[...] = jnp.zeros_like(hcar)

    for n in range(RG_BLOCKS):
        cols = slice(n * LANES, (n + 1) * LANES)
        xn = xc[pl.ds(n, TL, stride=SUBLANES), :]
        xb = xn.astype(BF16)
        ta = jnp.tanh(jnp.dot(xb, wa_ref[n], preferred_element_type=F32) + ba_ref[:, cols])
        ti = jnp.tanh(jnp.dot(xb, wi_ref[n], preferred_element_type=F32) + bi_ref[:, cols])
        k = (-0.5 * RG_C * LOG2E) * jax.nn.softplus(-lam_ref[:, cols])
        a = jnp.exp2(k * ta + k)
        om = 1.0 - a * a
        root = jnp.where(om > 0.0, om * lax.rsqrt(om), 0.0)
        a_s[pl.ds(n, TL, stride=SUBLANES), :] = a
        b_s[pl.ds(n, TL, stride=SUBLANES), :] = root * (0.5 * xn) * (ti + 1.0)

    def two_steps(p, h):
        t0 = (TL - 1 - 2 * p) if reverse else 2 * p
        t1 = (t0 - 1) if reverse else (t0 + 1)
        r0 = pl.multiple_of(t0 * SUBLANES, SUBLANES)
        r1 = pl.multiple_of(t1 * SUBLANES, SUBLANES)
        a0 = a_s[pl.ds(r0, SUBLANES), :]
        b0 = b_s[pl.ds(r0, SUBLANES), :]
        a1 = a_s[pl.ds(r1, SUBLANES), :]
        b1 = b_s[pl.ds(r1, SUBLANES), :]
        h_dst[pl.ds(r0, SUBLANES), :] = a0 * h + b0
        h2 = (a1 * a0) * h + (a1 * b0 + b1)
        h_dst[pl.ds(r1, SUBLANES), :] = h2
        return h2

    hcar[...] = lax.fori_loop(0, TL // 2, two_steps, hcar[...], unroll=8)


def _rg_fwd_kernel(xm_ref, xprev_ref, xnext_ref, cw_ref, cb_ref, wa_ref, wi_ref, ba_ref, bi_ref, lam_ref,
                   hf_ref, xc_ref, xpad, a_s, b_s, hcar, *, nlat):
    rows = TL * SUBLANES
    hrows = HALO * SUBLANES
    j = pl.program_id(1)
    has_prev = j >= 2
    has_next = jnp.logical_and(j >= 1, j < nlat)
    xpad[0:hrows, :] = jnp.where(has_prev, xprev_ref[...], 0.0)
    xpad[hrows:hrows + rows, :] = xm_ref[...]
    xpad[hrows + rows:2 * hrows + rows, :] = jnp.where(has_next, xnext_ref[...], 0.0)
    acc = jnp.broadcast_to(cb_ref[...][None], (TL, SUBLANES, LANES))
    for k in range(CONV_W):
        off = (HALO + k - CONV_W // 2) * SUBLANES
        tap = xpad[off:off + rows, :].reshape(TL, SUBLANES, LANES)
        acc = acc + tap * cw_ref[k][None]
    xc_ref[...] = acc.reshape(rows, LANES)
    _rg_gates_and_scan(xc_ref, wa_ref, wi_ref, ba_ref, bi_ref, lam_ref, a_s, b_s, hf_ref, hcar, False)


def _rg_bwd_kernel(xc_ref, wa_ref, wi_ref, ba_ref, bi_ref, lam_ref, hf_ref, gg_ref, out_ref, a_s, b_s, h_s, hcar):
    _rg_gates_and_scan(xc_ref, wa_ref, wi_ref, ba_ref, bi_ref, lam_ref, a_s, b_s, h_s, hcar, True)
    h_s[...] = gg_ref[...] * (hf_ref[...] + h_s[...])
    for n in range(RG_BLOCKS):
        out_ref[:, n * LANES:(n + 1) * LANES] = h_s[pl.ds(n, TL, stride=SUBLANES), :].astype(BF16)


def _rg_scans(xin8, gg8, conv_w, conv_b, wa, wi, ba, bi, lam, s, c, b):
    assert c == TL and s % TL == 0
    rows = TL * SUBLANES
    hrows = HALO * SUBLANES
    nlat = s // TL
    t = xin8.shape[0] // SUBLANES
    n_halo = t // HALO
    d = RG_BLOCKS * LANES

    def chunk(reverse):
        return lambda bi_, j: jnp.where(j == 0, (b * s) // TL + bi_,
                                        bi_ * nlat + ((nlat - j) if reverse else (j - 1)))

    fwd, bwd = chunk(False), chunk(True)
    main_f = pl.BlockSpec((rows, LANES), lambda bi_, j: (fwd(bi_, j), 0))
    main_b = pl.BlockSpec((rows, LANES), lambda bi_, j: (bwd(bi_, j), 0))
    prev = pl.BlockSpec((hrows, LANES), lambda bi_, j: (jnp.maximum(fwd(bi_, j) * (TL // HALO) - 1, 0), 0))
    nxt = pl.BlockSpec((hrows, LANES),
                       lambda bi_, j: (jnp.minimum((fwd(bi_, j) + 1) * (TL // HALO), n_halo - 1), 0))
    full = lambda shape: pl.BlockSpec(shape, lambda bi_, j: (0,) * len(shape))
    gate_specs = [full((RG_BLOCKS, LANES, LANES)), full((RG_BLOCKS, LANES, LANES)),
                  full((1, d)), full((1, d)), full((1, d))]
    gate_args = lambda k: [(0.5 * wa[k]).astype(BF16), (0.5 * wi[k]).astype(BF16), 0.5 * ba[k].reshape(1, d),
                           0.5 * bi[k].reshape(1, d), lam[k].reshape(1, d)]
    tmajor = jax.ShapeDtypeStruct(xin8.shape, F32)
    buf = pltpu.VMEM((rows, LANES), F32)
    hf8, xc8 = pl.pallas_call(
        functools.partial(_rg_fwd_kernel, nlat=nlat),
        grid=(b, nlat + 1),
        in_specs=[main_f, prev, nxt, full((CONV_W, SUBLANES, LANES)), full((SUBLANES, LANES))] + gate_specs,
        out_specs=[main_f, main_f],
        out_shape=[tmajor, tmajor],
        scratch_shapes=[pltpu.VMEM((rows + 2 * hrows, LANES), F32), buf, buf, pltpu.VMEM((SUBLANES, LANES), F32)],
        compiler_params=_cp("parallel", "arbitrary"),
        name="rg_scan_fwd",
    )(xin8, xin8, xin8, conv_w.reshape(CONV_W, SUBLANES, LANES), conv_b.reshape(SUBLANES, LANES), *gate_args(0))
    return pl.pallas_call(
        _rg_bwd_kernel,
        grid=(b, nlat + 1),
        in_specs=[main_b] + gate_specs + [main_b, main_b],
        out_specs=pl.BlockSpec((TL, d), lambda bi_, j: (bwd(bi_, j), 0)),
        out_shape=jax.ShapeDtypeStruct((t, d), BF16),
        scratch_shapes=[buf, buf, buf, pltpu.VMEM((SUBLANES, LANES), F32)],
        compiler_params=_cp("parallel", "arbitrary"),
        name="rg_scan_bwd",
    )(xc8, *gate_args(1), hf8, gg8)


def _rglru_mixer(xa, pre, g, mod, w_in, conv_w, conv_b, wa, ba, wi, bi, lam, w_out, route, s, c, b, nrows_out):
    x, gg8, xin8 = _rg_in(xa, pre, g, mod, w_in, s, b)
    y = _rg_scans(xin8, gg8, conv_w, conv_b, wa, wi, ba, bi, lam, s, c, b)
    return _out_proj(y, x, mod, w_out, route, nrows_out, s, b)


def _rope_tables(s):
    pos = jnp.arange(s, dtype=F32)
    row = jnp.floor(pos / GRID_W)
    col = pos - row * GRID_W
    n_freq = HEAD_DIM // 4
    inv = ROPE_THETA ** (-jnp.arange(n_freq, dtype=F32) * 2.0 / (HEAD_DIM // 2))
    ar = row[:, None] * inv
    ac = col[:, None] * inv
    cos = jnp.concatenate([jnp.cos(ar), jnp.cos(ar), jnp.cos(ac), jnp.cos(ac)], axis=1)
    sin = jnp.concatenate([-jnp.sin(ar), jnp.sin(ar), -jnp.sin(ac), jnp.sin(ac)], axis=1)
    cos = jnp.concatenate([cos, jnp.ones((TM, HEAD_DIM), F32)], axis=0)
    sin = jnp.concatenate([sin, jnp.zeros((TM, HEAD_DIM), F32)], axis=0)
    return cos, sin


def _qkv_kernel(*refs, has_pre):
    x_ref, pre_refs, rest = _split_refs(refs, has_pre)
    g_ref, mod_ref, w_ref, qg_ref, kg_ref, cos_ref, sin_ref, *outs, wb_ref = rest
    q_ref, k_ref, v_ref = outs[-3:]
    d = x_ref.shape[-1]
    m = mod_ref[0]
    x = _combined_rows(x_ref, pre_refs)
    if has_pre:
        outs[0][...] = x
    h = _norm_mod(x, g_ref[...], m[:, 0:d], m[:, d:2 * d])
    z = jnp.dot(h.astype(BF16), _resident_bf16(w_ref, wb_ref), preferred_element_type=F32)
    cos = cos_ref[...]
    sin = sin_ref[...]
    src = lax.broadcasted_iota(I32, (HEAD_DIM, HEAD_DIM), 0)
    dst = lax.broadcasted_iota(I32, (HEAD_DIM, HEAD_DIM), 1)
    quarter = HEAD_DIM // 4
    partner_of = jnp.where((dst % (2 * quarter)) < quarter, dst + quarter, dst - quarter)
    swap = jnp.where(src == partner_of, 1.0, 0.0).astype(BF16)

    def head(zc, gain):
        ms = jnp.mean(zc * zc, axis=-1, keepdims=True)
        y = zc * lax.rsqrt(ms + NORM_EPS) * gain
        partner = jnp.dot(y.astype(BF16), swap, preferred_element_type=F32)
        return y * cos + partner * sin

    nq = q_ref.shape[-1] // HEAD_DIM
    nk = k_ref.shape[-1] // HEAD_DIM
    for j in range(nq):
        q_ref[:, j * HEAD_DIM:(j + 1) * HEAD_DIM] = (
            head(z[:, j * HEAD_DIM:(j + 1) * HEAD_DIM], qg_ref[...]) * (HEAD_DIM ** -0.5 * LOG2E)).astype(BF16)
    for j in range(nk):
        c0 = (nq + j) * HEAD_DIM
        k_ref[:, j * HEAD_DIM:(j + 1) * HEAD_DIM] = head(z[:, c0:c0 + HEAD_DIM], kg_ref[...]).astype(BF16)
    v_ref[...] = z[:, (nq + nk) * HEAD_DIM:].astype(BF16)


def _qkv(x, pre, g, mod, w, qg, kg, cos, sin, s, b):
    t, d = x.shape
    nkv = N_KV_HEADS * HEAD_DIM
    n_pos = s // TM
    pre_specs, pre_args = _pre_io(pre, d, TM, s, b)
    row = pl.BlockSpec((TM, d), lambda i: (i, 0))
    outs = pl.pallas_call(
        functools.partial(_qkv_kernel, has_pre=pre is not None),
        grid=(t // TM,),
        in_specs=[row] + pre_specs + [
            pl.BlockSpec((1, d), lambda i: (0, 0)),
            _mod_spec(d, s // TM, b),
            pl.BlockSpec(w.shape, lambda i: (0, 0)),
            pl.BlockSpec((1, HEAD_DIM), lambda i: (0, 0)),
            pl.BlockSpec((1, HEAD_DIM), lambda i: (0, 0)),
            pl.BlockSpec((TM, HEAD_DIM), lambda i: (jnp.where(i < b * n_pos, i % n_pos, n_pos), 0)),
            pl.BlockSpec((TM, HEAD_DIM), lambda i: (jnp.where(i < b * n_pos, i % n_pos, n_pos), 0))],
        out_specs=([row] if pre else []) + [pl.BlockSpec((TM, d), lambda i: (i, 0)),
                                           pl.BlockSpec((TM, nkv), lambda i: (i, 0)),
                                           pl.BlockSpec((TM, nkv), lambda i: (i, 0))],
        out_shape=([jax.ShapeDtypeStruct((t, d), F32)] if pre else [])
        + [jax.ShapeDtypeStruct((t, d), BF16), jax.ShapeDtypeStruct((t, nkv), BF16),
           jax.ShapeDtypeStruct((t, nkv), BF16)],
        scratch_shapes=[pltpu.VMEM(w.shape, BF16)],
        compiler_params=_cp("arbitrary"),
        name="qkv_proj",
    )(x, *pre_args, g, mod, w, qg, kg, cos, sin)
    return tuple(outs) if pre else (x, *outs)


def _attn_kernel(q_ref, kc_ref, vc_ref, *rest, n_lat):
    if n_lat:
        kl_ref, vl_ref, o_ref, s_scr, vaug = rest
    else:
        o_ref, s_scr, vaug = rest
    n_ctx = kc_ref.shape[0]
    tq = q_ref.shape[0]

    def fill_values():
        vaug[:, HEAD_DIM:] = jnp.ones((n_ctx + n_lat, HEAD_DIM), BF16)
        vaug[0:n_ctx, 0:HEAD_DIM] = vc_ref[...]
        if n_lat:
            vaug[n_ctx:, 0:HEAD_DIM] = vl_ref[...]

    if n_lat:
        pl.when(pl.program_id(2) == 0)(fill_values)
    else:
        fill_values()

    chunks = [(0, n_ctx)] + [(n_ctx + j, ATT_KC) for j in range(0, n_lat, ATT_KC)]
    nt = (((1,), (1,)), ((), ()))
    q_all = jnp.concatenate([q_ref[:, g * HEAD_DIM:(g + 1) * HEAD_DIM] for g in range(GQA_GROUP)], axis=0)
    m_part = jnp.full((GQA_GROUP * tq, LANES), -jnp.inf, F32)
    for off, size in chunks:
        keys = kc_ref[...] if off == 0 else kl_ref[off - n_ctx:off - n_ctx + size, :]
        sc = lax.dot_general(q_all, keys, nt, preferred_element_type=F32)
        s_scr[:, off:off + size] = sc
        for j in range(0, size, LANES):
            m_part = jnp.maximum(m_part, sc[:, j:j + LANES])
    m_row = jnp.max(m_part, axis=-1, keepdims=True)
    hr = GQA_GROUP * tq // 2
    acc = [jnp.zeros((hr, 2 * HEAD_DIM), F32), jnp.zeros((hr, 2 * HEAD_DIM), F32)]
    for off, size in chunks:
        for r in range(2):
            rows = slice(r * hr, (r + 1) * hr)
            p = jnp.exp2((s_scr[rows, off:off + size] - m_row[rows]).astype(BF16))
            acc[r] = acc[r] + jnp.dot(p, vaug[off:off + size, :], preferred_element_type=F32)
    for r in range(2):
        out = (acc[r][:, :HEAD_DIM] / acc[r][:, HEAD_DIM:]).astype(BF16)
        for j in range(GQA_GROUP // 2):
            g = r * (GQA_GROUP // 2) + j
            o_ref[:, g * HEAD_DIM:(g + 1) * HEAD_DIM] = out[j * tq:(j + 1) * tq]


def _attn_ctx_kernel(q_ref, kc_ref, vc_ref, o_all_ref, o_ref, s_scr, vaug):
    del o_all_ref
    _attn_kernel(q_ref, kc_ref, vc_ref, o_ref, s_scr, vaug, n_lat=0)


def _attention(q, k, v, s, c, b):
    t, d = q.shape
    gw = GQA_GROUP * HEAD_DIM
    tq = TM
    nq = s // tq
    assert s % ATT_KC == 0
    ctx_blk = lambda bi, h, *_: ((b * s) // c + bi, h)
    o_lat = pl.pallas_call(
        functools.partial(_attn_kernel, n_lat=s),
        grid=(b, N_KV_HEADS, nq),
        in_specs=[pl.BlockSpec((tq, gw), lambda bi, h, i: (bi * nq + i, h)),
                  pl.BlockSpec((c, HEAD_DIM), ctx_blk),
                  pl.BlockSpec((c, HEAD_DIM), ctx_blk),
                  pl.BlockSpec((s, HEAD_DIM), lambda bi, h, i: (bi, h)),
                  pl.BlockSpec((s, HEAD_DIM), lambda bi, h, i: (bi, h))],
        out_specs=pl.BlockSpec((tq, gw), lambda bi, h, i: (bi * nq + i, h)),
        out_shape=jax.ShapeDtypeStruct((t, d), BF16),
        scratch_shapes=[pltpu.VMEM((GQA_GROUP * tq, c + s), F32), pltpu.VMEM((c + s, 2 * HEAD_DIM), BF16)],
        compiler_params=_cp("parallel", "parallel", "arbitrary"),
        name="attn_lat",
    )(q, k, v, k, v)
    return pl.pallas_call(
        _attn_ctx_kernel,
        grid=(b, N_KV_HEADS),
        in_specs=[pl.BlockSpec((c, gw), ctx_blk),
                  pl.BlockSpec((c, HEAD_DIM), ctx_blk),
                  pl.BlockSpec((c, HEAD_DIM), ctx_blk),
                  pl.BlockSpec(memory_space=pl.ANY)],
        out_specs=pl.BlockSpec((c, gw), ctx_blk),
        out_shape=jax.ShapeDtypeStruct((t, d), BF16),
        scratch_shapes=[pltpu.VMEM((GQA_GROUP * c, c), F32), pltpu.VMEM((c, 2 * HEAD_DIM), BF16)],
        input_output_aliases={3: 0},
        compiler_params=_cp("parallel", "parallel"),
        name="attn_ctx",
    )(q, k, v, o_lat)


def _attention_mixer(xa, pre, g, mod, w_qkv, qg, kg, w_o, route, s, c, b):
    cos, sin = _rope_tables(s)
    x, q, k, v = _qkv(xa, pre, g, mod, w_qkv, qg.reshape(1, -1), kg.reshape(1, -1), cos, sin, s, b)
    o = _attention(q, k, v, s, c, b)
    return _out_proj(o, x, mod, w_o, route, x.shape[0], s, b)


def _gmlp_kernel(*refs, has_pre):
    x_ref, pre_refs, rest = _split_refs(refs, has_pre)
    g_ref, mod_ref, w_in_ref, lng_ref, lnb_ref, ws_ref, bs_ref, w_out_ref = rest[:8]
    route_in, (o_ref, *route_out, uv_ref, cnt_s) = rest[8:12], rest[12:]
    d = x_ref.shape[-1]
    dcm = lng_ref.shape[-1]
    gw = dcm // CM_GROUPS
    x = _combined_rows(x_ref, pre_refs)
    m = mod_ref[0]
    h = _norm_mod(x, g_ref[...], m[:, 0:d], m[:, d:2 * d])
    z = jax.nn.gelu(jnp.dot(h.astype(BF16), w_in_ref[...], preferred_element_type=F32))
    u = z[:, :dcm]
    v = z[:, dcm:]
    mu = jnp.mean(v, axis=-1, keepdims=True)
    vc = v - mu
    var = jnp.mean(vc * vc, axis=-1, keepdims=True)
    vn = (vc * lax.rsqrt(var + NORM_EPS) * lng_ref[...] + lnb_ref[...]).astype(BF16)
    for ck in range(x.shape[0] // CHUNK):
        rows = slice(ck * CHUNK, (ck + 1) * CHUNK)
        for gi in range(CM_GROUPS):
            cols = slice(gi * gw, (gi + 1) * gw)
            mix = jnp.dot(ws_ref[gi], vn[rows, cols], preferred_element_type=F32) + bs_ref[:, gi:gi + 1]
            uv_ref[rows, cols] = (u[rows, cols] * mix).astype(BF16)
    y = jnp.dot(uv_ref[...], w_out_ref[...], preferred_element_type=F32)
    x1 = x + m[:, 2 * d:3 * d] * y
    o_ref[...] = x1
    _route_rows(x1, m, *route_in, *route_out, cnt_s)


def _gmlp_mixer(xa, pre, g, mod, w_in, ln_g, ln_b, w_s, b_s, w_out, route, s, b):
    t, d = xa.shape
    dcm = ln_g.shape[-1]
    full = lambda shape: pl.BlockSpec(shape, lambda i: (0,) * len(shape))
    pre_specs, pre_args = _pre_io(pre, d, TG, s, b)
    r_in, r_out, r_shape, r_scratch = _route_io(d, t, TG)
    x1, *routed = pl.pallas_call(
        functools.partial(_gmlp_kernel, has_pre=pre is not None),
        grid=(t // TG,),
        in_specs=[pl.BlockSpec((TG, d), lambda i: (i, 0))] + pre_specs + [
            full((1, d)),
            _mod_spec(d, s // TG, b),
            full((d, 2 * dcm)), full((1, dcm)), full((1, dcm)),
            full((CM_GROUPS, CHUNK, CHUNK)), full((CHUNK, CM_GROUPS)), full((dcm, d))] + r_in,
        out_specs=[pl.BlockSpec((TG, d), lambda i: (i, 0))] + r_out,
        out_shape=[jax.ShapeDtypeStruct((t, d), F32)] + r_shape,
        scratch_shapes=[pltpu.VMEM((TG, dcm), BF16)] + r_scratch,
        compiler_params=_cp("arbitrary"),
        name="gmlp",
    )(xa, *pre_args, g, mod, w_in.astype(BF16), ln_g.reshape(1, dcm), ln_b.reshape(1, dcm),
      w_s.astype(BF16), b_s.T, w_out.astype(BF16), *route)
    return x1, routed


def _route_rows(x1, m, g_ref, wrh_ref, wrl_ref, br_ref, hf_ref, rt_ref, ew_ref, cnt_ref, cnt_s):
    tm, d = x1.shape

    @pl.when(pl.program_id(0) == 0)
    def _():
        cnt_s[...] = jnp.zeros_like(cnt_s)

    hf = _norm_mod(x1, g_ref[...], m[:, 3 * d:4 * d], m[:, 4 * d:5 * d])
    hf_ref[...] = _pack_bf16_pairs(hf)
    hf_hi = hf.astype(BF16)
    hf_lo = (hf - hf_hi.astype(F32)).astype(BF16)
    logits = (jnp.dot(hf_hi, wrh_ref[...], preferred_element_type=F32)
              + jnp.dot(hf_lo, wrh_ref[...], preferred_element_type=F32)
              + jnp.dot(hf_hi, wrl_ref[...], preferred_element_type=F32)) + br_ref[...]
    lane = lax.broadcasted_iota(I32, logits.shape, 1)
    neg = -jnp.inf
    gl = jnp.where(lane < N_GROUPS, logits, neg)
    gmax = jnp.max(gl, axis=-1, keepdims=True)
    gsel = jnp.min(jnp.where(gl == gmax, lane, LANES), axis=-1, keepdims=True)
    gate_g = 1.0 / jnp.sum(jnp.exp(gl - gmax), axis=-1, keepdims=True)
    lo = N_GROUPS + gsel * EXPERTS_PER_GROUP
    el = jnp.where(jnp.logical_and(lane >= lo, lane < lo + EXPERTS_PER_GROUP), logits, neg)
    v1 = jnp.max(el, axis=-1, keepdims=True)
    i1 = jnp.min(jnp.where(el == v1, lane, LANES), axis=-1, keepdims=True)
    el2 = jnp.where(lane == i1, neg, el)
    v2 = jnp.max(el2, axis=-1, keepdims=True)
    i2 = jnp.min(jnp.where(el2 == v2, lane, LANES), axis=-1, keepdims=True)
    e21 = jnp.exp(v2 - v1)
    w1 = gate_g / (1.0 + e21)
    w2 = w1 * e21
    ew_ref[...] = jnp.where(lane == 0, w1, jnp.where(lane == 1, w2, 0.0))

    oh1 = lane == i1
    oh2 = lane == i2
    above = (lax.broadcasted_iota(I32, (tm, tm), 1) < lax.broadcasted_iota(I32, (tm, tm), 0)).astype(BF16)
    pre1 = jnp.dot(above, oh1.astype(BF16), preferred_element_type=F32)
    pre2 = jnp.dot(above, oh2.astype(BF16), preferred_element_type=F32)
    tot1 = jnp.sum(oh1.astype(F32), axis=0, keepdims=True)
    tot2 = jnp.sum(oh2.astype(F32), axis=0, keepdims=True)
    cnt = cnt_s[...]
    rank1 = jnp.sum(jnp.where(oh1, cnt + pre1, 0.0), axis=-1, keepdims=True).astype(I32)
    rank2 = jnp.sum(jnp.where(oh2, cnt + tot1 + pre2, 0.0), axis=-1, keepdims=True).astype(I32)
    cnt = cnt + tot1 + tot2
    cnt_s[...] = cnt
    cnt_ref[...] = jnp.broadcast_to(cnt, cnt_ref.shape).astype(I32)
    rt = jnp.where(lane == 0, i1 - N_GROUPS, jnp.where(lane == 1, i2 - N_GROUPS,
                   jnp.where(lane == 2, rank1, jnp.where(lane == 3, rank2, 0))))
    rt_ref[...] = rt.T[:SUBLANES]


def _route_params(g_ffn, w_group, b_group, w_router, b_router):
    d = w_group.shape[0]
    pad = LANES - N_GROUPS - N_EXPERTS
    wr = jnp.concatenate([w_group, w_router.reshape(d, N_EXPERTS), jnp.zeros((d, pad), F32)], axis=1)
    br = jnp.concatenate([b_group, b_router.reshape(N_EXPERTS), jnp.zeros((pad,), F32)]).reshape(1, LANES)
    wr_hi = wr.astype(BF16)
    wr_lo = (wr - wr_hi.astype(F32)).astype(BF16)
    return g_ffn.reshape(1, d), wr_hi, wr_lo, br


def _route_io(d, nrows, tr):
    const = lambda shape: pl.BlockSpec(shape, lambda i: (0, 0))
    row = lambda w: pl.BlockSpec((tr, w), lambda i: (i, 0))
    in_specs = [const((1, d)), const((d, LANES)), const((d, LANES)), const((1, LANES))]
    out_specs = [row(d // 2), pl.BlockSpec((SUBLANES, tr), lambda i: (0, i)), row(LANES), const((SUBLANES, LANES))]
    out_shape = [jax.ShapeDtypeStruct((nrows, d // 2), I32), jax.ShapeDtypeStruct((SUBLANES, nrows), I32),
                 jax.ShapeDtypeStruct((nrows, LANES), F32), jax.ShapeDtypeStruct((SUBLANES, LANES), I32)]
    return in_specs, out_specs, out_shape, [pltpu.VMEM((1, LANES), F32)]


def _plan_kernel(cnt_ref, rt_ref, pos_ref, blk_e_ref, n_used_ref, first_ref, slot_ref, next_ref,
                 start_s, end_s, nxt_s, *, bm):
    n_blk = blk_e_ref.shape[0]
    acc = jnp.int32(0)
    for e in range(N_EXPERTS):
        start_s[e] = acc
        acc = acc + (cnt_ref[0, N_GROUPS + e] + (bm - 1)) // bm * bm
        end_s[e] = acc
    n_used = acc // bm
    n_used_ref[0] = n_used
    nxt = jnp.int32(-1)
    for e in reversed(range(N_EXPERTS)):
        nxt_s[e] = nxt
        nxt = jnp.where(cnt_ref[0, N_GROUPS + e] > 0, e, nxt)

    def block(i, carry):
        prev_e, runs = carry
        row = jnp.minimum(i, n_used - 1) * bm
        e = lax.while_loop(lambda v: jnp.logical_and(v < N_EXPERTS - 1, end_s[v] <= row), lambda v: v + 1,
                           jnp.maximum(prev_e, 0))
        first = jnp.logical_and(i < n_used, prev_e != e)
        runs = runs + first.astype(I32)
        blk_e_ref[i] = e
        first_ref[i] = first.astype(I32)
        slot_ref[i] = (runs - 1) % 2
        next_ref[i] = nxt_s[e]
        return e, runs

    lax.fori_loop(0, n_blk, block, (jnp.int32(-1), jnp.int32(0)))

    rt = rt_ref[...]
    start_of = jnp.zeros_like(rt)
    for e in range(N_EXPERTS):
        start_of = jnp.where(rt == e, start_s[e], start_of)
    pos_ref[...] = start_of + pltpu.roll(rt, SUBLANES - 2, 0)


def _dispatch_plan(rt, cnt, bm):
    n_tok = rt.shape[1]
    n_rows = 2 * n_tok + N_EXPERTS * bm
    n_blk = n_rows // bm
    smem = pl.BlockSpec(memory_space=pltpu.SMEM)
    vec = lambda n: jax.ShapeDtypeStruct((n,), I32)
    pos, blk_e, n_used, first, slot, nxt = pl.pallas_call(
        functools.partial(_plan_kernel, bm=bm),
        in_specs=[smem, pl.BlockSpec(memory_space=pltpu.VMEM)],
        out_specs=[pl.BlockSpec(memory_space=pltpu.VMEM), smem, smem, smem, smem, smem],
        out_shape=[jax.ShapeDtypeStruct(rt.shape, I32), vec(n_blk), vec(1), vec(n_blk), vec(n_blk), vec(n_blk)],
        scratch_shapes=[pltpu.SMEM((N_EXPERTS,), I32)] * 3,
        name="moe_plan",
    )(cnt, rt)
    return pos, (blk_e, n_used, first, slot, nxt), n_rows


def _sc_mesh():
    return plsc.VectorSubcoreMesh(core_axis_name="c", subcore_axis_name="s")


def _sc_worker_base(per_worker):
    return (lax.axis_index("s") * SC_CORES + lax.axis_index("c")) * per_worker


def _sc_dispatch(hf, pos, n_rows):
    t, d = hf.shape
    per_w = t // SC_WORKERS
    ch = SC_CHUNK
    n_ck = per_w // ch
    assert per_w * SC_WORKERS == t and n_ck * ch == per_w

    @functools.partial(
        pl.kernel, mesh=_sc_mesh(), out_type=jax.ShapeDtypeStruct((n_rows, d), hf.dtype),
        scratch_types=[pltpu.VMEM((per_w,), I32), pltpu.VMEM((per_w,), I32), pltpu.VMEM((2, ch, d), hf.dtype),
                       pltpu.SemaphoreType.DMA((2,)), pltpu.SemaphoreType.DMA((2,)), pltpu.SemaphoreType.DMA((2,))])
    def dispatch(hf_hbm, p_hbm, out_hbm, i0_v, i1_v, rows_v, sem_in, sem_s0, sem_s1):
        base = pl.multiple_of(_sc_worker_base(per_w), SUBLANES)
        pltpu.sync_copy(p_hbm.at[pl.ds(base, per_w)], i0_v)
        pltpu.sync_copy(p_hbm.at[pl.ds(pl.multiple_of(t + base, SUBLANES), per_w)], i1_v)

        def load(ck):
            return pltpu.make_async_copy(hf_hbm.at[pl.ds(base + ck * ch, ch)], rows_v.at[ck % 2], sem_in.at[ck % 2])

        def scatters(ck):
            src = rows_v.at[ck % 2]
            return (pltpu.make_async_copy(src, out_hbm.at[i0_v.at[pl.ds(ck * ch, ch)]], sem_s0.at[ck % 2]),
                    pltpu.make_async_copy(src, out_hbm.at[i1_v.at[pl.ds(ck * ch, ch)]], sem_s1.at[ck % 2]))

        load(0).start()
        for ck in range(n_ck):
            load(ck).wait()
            if ck + 1 < n_ck:
                if ck >= 1:
                    for cp in scatters(ck - 1):
                        cp.wait()
                load(ck + 1).start()
            for cp in scatters(ck):
                cp.start()
        for ck in range(max(n_ck - 2, 0), n_ck):
            for cp in scatters(ck):
                cp.wait()

    return dispatch(hf, pos)


def _sc_gather(rows, idx):
    n = idx.shape[0]
    d = rows.shape[1]
    per_w = n // SC_WORKERS
    ch = 2 * SC_CHUNK
    n_ck = per_w // ch
    assert per_w * SC_WORKERS == n and n_ck * ch == per_w

    @functools.partial(
        pl.kernel, mesh=_sc_mesh(), out_type=jax.ShapeDtypeStruct((n, d), rows.dtype),
        scratch_types=[pltpu.VMEM((per_w,), I32), pltpu.VMEM((2, ch, d), rows.dtype),
                       pltpu.SemaphoreType.DMA((2,)), pltpu.SemaphoreType.DMA((2,))])
    def gather(rows_hbm, i_hbm, out_hbm, i_v, buf, sem_g, sem_w):
        base = pl.multiple_of(_sc_worker_base(per_w), SUBLANES)
        pltpu.sync_copy(i_hbm.at[pl.ds(base, per_w)], i_v)

        def fetch(ck):
            return pltpu.make_async_copy(rows_hbm.at[i_v.at[pl.ds(ck * ch, ch)]], buf.at[ck % 2], sem_g.at[ck % 2])

        def write(ck):
            return pltpu.make_async_copy(buf.at[ck % 2], out_hbm.at[pl.ds(base + ck * ch, ch)], sem_w.at[ck % 2])

        fetch(0).start()
        for ck in range(n_ck):
            fetch(ck).wait()
            if ck + 1 < n_ck:
                if ck >= 1:
                    write(ck - 1).wait()
                fetch(ck + 1).start()
            write(ck).start()
        for ck in range(max(n_ck - 2, 0), n_ck):
            write(ck).wait()

    return gather(rows, idx)


def _expert_kernel(blk_e_ref, n_used_ref, first_ref, slot_ref, next_ref, x_ref, wg_hbm, wu_hbm, wd_hbm, y_ref,
                   wgf, wuf, wdf, wgb, wub, wdb, sem, *, e_base):
    n_used = n_used_ref[0]
    bm = x_ref.shape[0] // MOE_SUB

    def weight_copies(e, slot):
        return (pltpu.make_async_copy(wg_hbm.at[e_base + e], wgf.at[slot], sem.at[slot, 0]),
                pltpu.make_async_copy(wu_hbm.at[e_base + e], wuf.at[slot], sem.at[slot, 1]),
                pltpu.make_async_copy(wd_hbm.at[e_base + e], wdf.at[slot], sem.at[slot, 2]))

    @pl.when(pl.program_id(0) == 0)
    def _():
        for cp in weight_copies(blk_e_ref[0], 0):
            cp.start()

    for j in range(MOE_SUB):
        blk = pl.program_id(0) * MOE_SUB + j
        rows = slice(j * bm, (j + 1) * bm)

        @pl.when(jnp.logical_and(blk < n_used, first_ref[blk] == 1))
        def _():
            slot = slot_ref[blk]
            for cp in weight_copies(blk_e_ref[blk], slot):
                cp.wait()
            nxt = next_ref[blk]

            @pl.when(nxt >= 0)
            def _():
                for cp in weight_copies(nxt, 1 - slot):
                    cp.start()

            wgb[...] = wgf[slot].astype(BF16)
            wub[...] = wuf[slot].astype(BF16)
            wdb[...] = wdf[slot].astype(BF16)

        @pl.when(blk < n_used)
        def _():
            x_hi, x_lo = _unpack_bf16_pairs(x_ref[rows, :])
            xb = jnp.concatenate([x_hi.astype(BF16), x_lo.astype(BF16)], axis=1)
            gt = jnp.dot(xb, wgb[...], preferred_element_type=F32)
            up = jnp.dot(xb, wub[...], preferred_element_type=F32)
            act = (gt * _sigmoid(gt) * up).astype(BF16)
            y_ref[rows, :] = _pack_bf16_pairs(jnp.dot(act, wdb[...], preferred_element_type=F32))


def _experts(x_rows, plan, w_gate, w_up, w_down, layer):
    n_rows, dp = x_rows.shape
    depth, n_e, d, de = w_gate.shape
    step_rows = MOE_SUB * MOE_BM
    assert n_rows % step_rows == 0
    any_spec = pl.BlockSpec(memory_space=pl.ANY)
    last_used = lambda i, be, nu, *_: (jnp.minimum(i, (nu[0] - 1) // MOE_SUB), 0)
    grid_spec = pltpu.PrefetchScalarGridSpec(
        num_scalar_prefetch=5,
        grid=(n_rows // step_rows,),
        in_specs=[pl.BlockSpec((step_rows, dp), last_used), any_spec, any_spec, any_spec],
        out_specs=pl.BlockSpec((step_rows, dp), last_used),
        scratch_shapes=[pltpu.VMEM((2, d, de), F32), pltpu.VMEM((2, d, de), F32), pltpu.VMEM((2, de, d), F32),
                        pltpu.VMEM((d, de), BF16), pltpu.VMEM((d, de), BF16), pltpu.VMEM((de, d), BF16),
                        pltpu.SemaphoreType.DMA((2, 3))],
    )
    return pl.pallas_call(
        functools.partial(_expert_kernel, e_base=layer * n_e),
        grid_spec=grid_spec,
        out_shape=jax.ShapeDtypeStruct((n_rows, dp), I32),
        compiler_params=_cp("arbitrary"),
        name="moe_experts",
    )(*plan, x_rows, w_gate.reshape(depth * n_e, d, de), w_up.reshape(depth * n_e, d, de),
      w_down.reshape(depth * n_e, de, d))


def _combine_kernel(x_ref, *rest):
    *pre_refs, o_ref = rest
    o_ref[...] = _combined_rows(x_ref, pre_refs)


def _combine(x, pre, s, b):
    nrows, d = x.shape
    pre_specs, pre_args = _pre_io(pre, d, TR, s, b)
    return pl.pallas_call(
        _combine_kernel,
        grid=(nrows // TR,),
        in_specs=[pl.BlockSpec((TR, d), lambda i: (i, 0))] + pre_specs,
        out_specs=pl.BlockSpec((TR, d), lambda i: (i, 0)),
        out_shape=jax.ShapeDtypeStruct((nrows, d), F32),
        compiler_params=_cp("parallel"),
        name="moe_combine",
    )(x, *pre_args)


def _expert_outputs(routed, w_gate, w_up, w_down, layer):
    hf, rt, _, cnt = routed
    pos, plan, n_rows = _dispatch_plan(rt, cnt, MOE_BM)
    pos = pos[0:2].reshape(-1)
    x_rows = _sc_dispatch(hf, pos, n_rows)
    y_rows = _experts(x_rows, plan, w_gate, w_up, w_down, layer)
    return _sc_gather(y_rows, pos)


def kernel(x, c, ctx, c_ctx, ada_w, ada_b, norm_mix_g, norm_ffn_g, rg_w_in, rg_conv_w, rg_conv_b, rg_wa, rg_ba, rg_wi, rg_bi, rg_lambda, rg_w_out, at_w_qkv, at_q_g, at_k_g, at_w_o, cm_w_in, cm_ln_g, cm_ln_b, cm_w_s, cm_b_s, cm_w_out, moe_w_group, moe_b_group, moe_w_router, moe_b_router, moe_w_gate, moe_w_up, moe_w_down):
    b, s, d = x.shape
    cl = ctx.shape[1]
    depth = ada_w.shape[0]
    n_lat = b * s
    assert b < SUBLANES and s % max(TI, TR, TG) == 0 and (b * cl) % max(TI, TR, TG) == 0 and cl % TM == 0
    assert d == RG_BLOCKS * LANES

    cin = jnp.concatenate([c, c_ctx[None, :], jnp.zeros((SUBLANES - b - 1, d), F32)], axis=0)
    mod_all = _ada_table(cin, ada_w, ada_b).reshape(depth, SUBLANES, 1, N_MOD * d)
    tok = jnp.concatenate([x.reshape(n_lat, d), ctx.reshape(b * cl, d)], axis=0)

    xa, pre = tok, None
    for l in range(depth):
        kind = l % 3
        j = l // 3
        last = l == depth - 1
        mod = mod_all[l]
        g_mix = norm_mix_g[l].reshape(1, d)
        route = _route_params(norm_ffn_g[l], moe_w_group[l], moe_b_group[l], moe_w_router[l], moe_b_router[l])
        if kind == 0:
            x1, routed = _rglru_mixer(xa, pre, g_mix, mod, rg_w_in[j], rg_conv_w[j], rg_conv_b[j], rg_wa[j],
                                      rg_ba[j], rg_wi[j], rg_bi[j], rg_lambda[j], rg_w_out[j], route, s, cl, b,
                                      n_lat if last else xa.shape[0])
        elif kind == 1:
            x1, routed = _attention_mixer(xa, pre, g_mix, mod, at_w_qkv[j], at_q_g[j], at_k_g[j], at_w_o[j],
                                          route, s, cl, b)
        else:
            x1, routed = _gmlp_mixer(xa, pre, g_mix, mod, cm_w_in[j], cm_ln_g[j], cm_ln_b[j], cm_w_s[j],
                                     cm_b_s[j], cm_w_out[j], route, s, b)
        y01 = _expert_outputs(routed, moe_w_gate, moe_w_up, moe_w_down, l)
        xa, pre = x1, (y01, routed[2], mod)
    return _combine(xa, pre, s, b)[:n_lat].reshape(b, s, d)
```

```python
import functools

import jax
import jax.numpy as jnp
from jax import lax
from jax.experimental import pallas as pl
from jax.experimental.pallas import tpu as pltpu
from jax.experimental.pallas import tpu_sc as plsc

F32 = jnp.float32
BF16 = jnp.bfloat16
I32 = jnp.int32
U32 = jnp.uint32

NORM_EPS = 1e-6
N_MOD = 6
GRID_W = 64
RG_BLOCKS = 8
CONV_W = 4
RG_C = 8.0
HEAD_DIM = 128
N_KV_HEADS = 2
GQA_GROUP = 4
ROPE_THETA = 10000.0
CHUNK = 128
CM_GROUPS = 8
N_GROUPS = 4
EXPERTS_PER_GROUP = 8
N_EXPERTS = N_GROUPS * EXPERTS_PER_GROUP

LANES = 128
SUBLANES = 8
TM = 256
TQ = 512
TI = 1024
TR = 1024
TG = 512
TL = 256
HALO = 8
ATT_KC = 512
LOG2E = 1.4426950408889634
MOE_BM = 256
MOE_SUB = 4
SC_CORES = 2
SC_WORKERS = 32
SC_CHUNK = 32
VMEM_LIMIT = 52 * 2**20


def _cp(*sem):
    return pltpu.CompilerParams(dimension_semantics=sem, vmem_limit_bytes=VMEM_LIMIT)


def _norm_mod(x, g, shift, scale):
    ms = jnp.mean(x * x, axis=-1, keepdims=True)
    y = x * lax.rsqrt(ms + NORM_EPS) * g
    return y * (1.0 + scale) + shift


def _sigmoid(x):
    return 0.5 * jnp.tanh(0.5 * x) + 0.5


def _pack_bf16_pairs(x):
    h = x.shape[-1] // 2
    hi = lax.bitcast_convert_type(x[:, :h].astype(BF16).astype(F32), U32)
    lo = lax.bitcast_convert_type(x[:, h:].astype(BF16).astype(F32), U32)
    return lax.bitcast_convert_type(hi | (lo >> 16), I32)


def _unpack_bf16_pairs(w):
    u = lax.bitcast_convert_type(w, U32)
    hi = lax.bitcast_convert_type(u & jnp.uint32(0xFFFF0000), F32)
    lo = lax.bitcast_convert_type(u << 16, F32)
    return hi, lo


def _combined_rows(x_ref, pre_refs):
    if not pre_refs:
        return x_ref[...]
    y0_ref, y1_ref, ew_ref, modp_ref = pre_refs
    d = x_ref.shape[1]
    ew = ew_ref[...]
    y0_hi, y0_lo = _unpack_bf16_pairs(y0_ref[...])
    y1_hi, y1_lo = _unpack_bf16_pairs(y1_ref[...])
    y = jnp.concatenate([ew[:, 0:1] * y0_hi + ew[:, 1:2] * y1_hi, ew[:, 0:1] * y0_lo + ew[:, 1:2] * y1_lo], axis=1)
    return x_ref[...] + modp_ref[0][:, 5 * d:6 * d] * y


def _pre_io(pre, d, tr, s, b):
    if pre is None:
        return [], []
    y01, ew, mod_prev = pre
    nb = y01.shape[0] // 2 // tr
    specs = [pl.BlockSpec((tr, d // 2), lambda i: (i, 0)), pl.BlockSpec((tr, d // 2), lambda i: (i + nb, 0)),
             pl.BlockSpec((tr, LANES), lambda i: (i, 0)), _mod_spec(d, s // tr, b)]
    return specs, [y01, y01, ew, mod_prev]


def _split_refs(refs, has_pre):
    return (refs[0], refs[1:5], refs[5:]) if has_pre else (refs[0], (), refs[1:])


def _mod_spec(d, rows_per_sample, n_samples):
    return pl.BlockSpec((1, 1, N_MOD * d),
                        lambda i, *_: (jnp.minimum(i // rows_per_sample, n_samples), 0, 0))


def _ada_kernel(c_ref, w_ref, b_ref, o_ref):
    cin = c_ref[...]
    act = cin * jax.nn.sigmoid(cin)
    w = w_ref[0]
    w_hi = w.astype(BF16)
    w_lo = (w - w_hi.astype(F32)).astype(BF16)
    a_hi = act.astype(BF16)
    a_lo = (act - a_hi.astype(F32)).astype(BF16)
    o_ref[0] = (jnp.dot(a_hi, w_hi, preferred_element_type=F32) + jnp.dot(a_lo, w_hi, preferred_element_type=F32)
                + jnp.dot(a_hi, w_lo, preferred_element_type=F32)) + b_ref[0]


def _ada_table(cin, ada_w, ada_b):
    depth, d, n = ada_w.shape
    tn = 2 * d
    return pl.pallas_call(
        _ada_kernel,
        grid=(depth, n // tn),
        in_specs=[pl.BlockSpec((SUBLANES, d), lambda l, j: (0, 0)),
                  pl.BlockSpec((1, d, tn), lambda l, j: (l, 0, j)),
                  pl.BlockSpec((1, 1, tn), lambda l, j: (l, 0, j))],
        out_specs=pl.BlockSpec((1, SUBLANES, tn), lambda l, j: (l, 0, j)),
        out_shape=jax.ShapeDtypeStruct((depth, SUBLANES, n), F32),
        compiler_params=_cp("parallel", "parallel"),
        name="ada_table",
    )(cin, ada_w, ada_b.reshape(depth, 1, n))


def _out_kernel(y_ref, x_ref, mod_ref, w_ref, *rest):
    route_in, (o_ref, *route_out) = rest[:4], rest[4:]
    d = x_ref.shape[-1]
    m = mod_ref[0]
    y = jnp.dot(y_ref[...].astype(BF16), w_ref[...], preferred_element_type=F32)
    x1 = x_ref[...] + m[:, 2 * d:3 * d] * y
    o_ref[...] = x1
    _route_rows(x1, m, *route_in, *route_out)


def _out_proj(y, x, mod, w, route, nrows, s, b):
    d = x.shape[-1]
    k = y.shape[-1]
    r_in, r_out, r_shape, r_scratch = _route_io(d, nrows, TR)
    x1, *routed = pl.pallas_call(
        _out_kernel,
        grid=(nrows // TR,),
        in_specs=[pl.BlockSpec((TR, k), lambda i: (i, 0)),
                  pl.BlockSpec((TR, d), lambda i: (i, 0)),
                  _mod_spec(d, s // TR, b),
                  pl.BlockSpec((k, d), lambda i: (0, 0))] + r_in,
        out_specs=[pl.BlockSpec((TR, d), lambda i: (i, 0))] + r_out,
        out_shape=[jax.ShapeDtypeStruct((nrows, d), F32)] + r_shape,
        scratch_shapes=r_scratch,
        compiler_params=_cp("arbitrary"),
        name="out_proj",
    )(y, x, mod, w, *route)
    return x1, routed


def _rg_in_kernel(*refs, has_pre):
    x_ref, pre_refs, (g_ref, mod_ref, w_ref, *outs) = _split_refs(refs, has_pre)
    gg_ref, xin_ref = outs[-2:]
    d = x_ref.shape[-1]
    m = mod_ref[0]
    x = _combined_rows(x_ref, pre_refs)
    if has_pre:
        outs[0][...] = x
    h = _norm_mod(x, g_ref[...], m[:, 0:d], m[:, d:2 * d])
    z = jnp.dot(h.astype(BF16), w_ref[...], preferred_element_type=F32)
    tm = x_ref.shape[0]
    for n in range(d // LANES):
        cols = slice(n * LANES, (n + 1) * LANES)
        gg_ref[pl.ds(n, tm, stride=SUBLANES), :] = jax.nn.gelu(z[:, cols])
        xin_ref[pl.ds(n, tm, stride=SUBLANES), :] = z[:, d + n * LANES:d + (n + 1) * LANES]


def _rg_in(x, pre, g, mod, w, s, b):
    t, d = x.shape
    assert d == SUBLANES * LANES
    pre_specs, pre_args = _pre_io(pre, d, TI, s, b)
    row = pl.BlockSpec((TI, d), lambda i: (i, 0))
    tmajor = pl.BlockSpec((TI * SUBLANES, LANES), lambda i: (i, 0))
    outs = pl.pallas_call(
        functools.partial(_rg_in_kernel, has_pre=pre is not None),
        grid=(t // TI,),
        in_specs=[row] + pre_specs + [pl.BlockSpec((1, d), lambda i: (0, 0)), _mod_spec(d, s // TI, b),
                                      pl.BlockSpec((d, 2 * d), lambda i: (0, 0))],
        out_specs=([row] if pre else []) + [tmajor, tmajor],
        out_shape=([jax.ShapeDtypeStruct((t, d), F32)] if pre else [])
        + [jax.ShapeDtypeStruct((t * SUBLANES, LANES), F32)] * 2,
        compiler_params=_cp("parallel"),
        name="rg_in",
    )(x, *pre_args, g, mod, w)
    return (outs[0], outs[1], outs[2]) if pre else (x, outs[0], outs[1])


def _rg_gates_and_scan(xc, wa_ref, wi_ref, ba_ref, bi_ref, lam_ref, a_s, b_s, h_dst, hcar, reverse):
    @pl.when(pl.program_id(1) == 0)
    def _():
        hcar[...] = jnp.zeros_like(hcar)

    for n in range(RG_BLOCKS):
        cols = slice(n * LANES, (n + 1) * LANES)
        xn = xc[pl.ds(n, TL, stride=SUBLANES), :]
        xb = xn.astype(BF16)
        ta = jnp.tanh(jnp.dot(xb, wa_ref[n], preferred_element_type=F32) + ba_ref[:, cols])
        ti = jnp.tanh(jnp.dot(xb, wi_ref[n], preferred_element_type=F32) + bi_ref[:, cols])
        k = (-0.5 * RG_C * LOG2E) * jax.nn.softplus(-lam_ref[:, cols])
        a = jnp.exp2(k * ta + k)
        om = 1.0 - a * a
        root = jnp.where(om > 0.0, om * lax.rsqrt(om), 0.0)
        a_s[pl.ds(n, TL, stride=SUBLANES), :] = a
        b_s[pl.ds(n, TL, stride=SUBLANES), :] = root * (0.5 * xn) * (ti + 1.0)

    def two_steps(p, h):
        t0 = (TL - 1 - 2 * p) if reverse else 2 * p
        t1 = (t0 - 1) if reverse else (t0 + 1)
        r0 = pl.multiple_of(t0 * SUBLANES, SUBLANES)
        r1 = pl.multiple_of(t1 * SUBLANES, SUBLANES)
        a0 = a_s[pl.ds(r0, SUBLANES), :]
        b0 = b_s[pl.ds(r0, SUBLANES), :]
        a1 = a_s[pl.ds(r1, SUBLANES), :]
        b1 = b_s[pl.ds(r1, SUBLANES), :]
        h_dst[pl.ds(r0, SUBLANES), :] = a0 * h + b0
        h2 = (a1 * a0) * h + (a1 * b0 + b1)
        h_dst[pl.ds(r1, SUBLANES), :] = h2
        return h2

    hcar[...] = lax.fori_loop(0, TL // 2, two_steps, hcar[...], unroll=8)


def _rg_fwd_kernel(xm_ref, xprev_ref, xnext_ref, cw_ref, cb_ref, wa_ref, wi_ref, ba_ref, bi_ref, lam_ref,
                   hf_ref, xc_ref, xpad, a_s, b_s, hcar, *, nlat):
    rows = TL * SUBLANES
    hrows = HALO * SUBLANES
    j = pl.program_id(1)
    has_prev = j >= 2
    has_next = jnp.logical_and(j >= 1, j < nlat)
    xpad[0:hrows, :] = jnp.where(has_prev, xprev_ref[...], 0.0)
    xpad[hrows:hrows + rows, :] = xm_ref[...]
    xpad[hrows + rows:2 * hrows + rows, :] = jnp.where(has_next, xnext_ref[...], 0.0)
    acc = jnp.broadcast_to(cb_ref[...][None], (TL, SUBLANES, LANES))
    for k in range(CONV_W):
        off = (HALO + k - CONV_W // 2) * SUBLANES
        tap = xpad[off:off + rows, :].reshape(TL, SUBLANES, LANES)
        acc = acc + tap * cw_ref[k][None]
    xc_ref[...] = acc.reshape(rows, LANES)
    _rg_gates_and_scan(xc_ref, wa_ref, wi_ref, ba_ref, bi_ref, lam_ref, a_s, b_s, hf_ref, hcar, False)


def _rg_bwd_kernel(xc_ref, wa_ref, wi_ref, ba_ref, bi_ref, lam_ref, hf_ref, gg_ref, out_ref, a_s, b_s, h_s, hcar):
    _rg_gates_and_scan(xc_ref, wa_ref, wi_ref, ba_ref, bi_ref, lam_ref, a_s, b_s, h_s, hcar, True)
    h_s[...] = gg_ref[...] * (hf_ref[...] + h_s[...])
    for n in range(RG_BLOCKS):
        out_ref[:, n * LANES:(n + 1) * LANES] = h_s[pl.ds(n, TL, stride=SUBLANES), :].astype(BF16)


def _rg_scans(xin8, gg8, conv_w, conv_b, wa, wi, ba, bi, lam, s, c, b):
    assert c == TL and s % TL == 0
    rows = TL * SUBLANES
    hrows = HALO * SUBLANES
    nlat = s // TL
    t = xin8.shape[0] // SUBLANES
    n_halo = t // HALO
    d = RG_BLOCKS * LANES

    def chunk(reverse):
        return lambda bi_, j: jnp.where(j == 0, (b * s) // TL + bi_,
                                        bi_ * nlat + ((nlat - j) if reverse else (j - 1)))

    fwd, bwd = chunk(False), chunk(True)
    main_f = pl.BlockSpec((rows, LANES), lambda bi_, j: (fwd(bi_, j), 0))
    main_b = pl.BlockSpec((rows, LANES), lambda bi_, j: (bwd(bi_, j), 0))
    prev = pl.BlockSpec((hrows, LANES), lambda bi_, j: (jnp.maximum(fwd(bi_, j) * (TL // HALO) - 1, 0), 0))
    nxt = pl.BlockSpec((hrows, LANES),
                       lambda bi_, j: (jnp.minimum((fwd(bi_, j) + 1) * (TL // HALO), n_halo - 1), 0))
    full = lambda shape: pl.BlockSpec(shape, lambda bi_, j: (0,) * len(shape))
    gate_specs = [full((RG_BLOCKS, LANES, LANES)), full((RG_BLOCKS, LANES, LANES)),
                  full((1, d)), full((1, d)), full((1, d))]
    gate_args = lambda k: [(0.5 * wa[k]).astype(BF16), (0.5 * wi[k]).astype(BF16), 0.5 * ba[k].reshape(1, d),
                           0.5 * bi[k].reshape(1, d), lam[k].reshape(1, d)]
    tmajor = jax.ShapeDtypeStruct(xin8.shape, F32)
    buf = pltpu.VMEM((rows, LANES), F32)
    hf8, xc8 = pl.pallas_call(
        functools.partial(_rg_fwd_kernel, nlat=nlat),
        grid=(b, nlat + 1),
        in_specs=[main_f, prev, nxt, full((CONV_W, SUBLANES, LANES)), full((SUBLANES, LANES))] + gate_specs,
        out_specs=[main_f, main_f],
        out_shape=[tmajor, tmajor],
        scratch_shapes=[pltpu.VMEM((rows + 2 * hrows, LANES), F32), buf, buf, pltpu.VMEM((SUBLANES, LANES), F32)],
        compiler_params=_cp("parallel", "arbitrary"),
        name="rg_scan_fwd",
    )(xin8, xin8, xin8, conv_w.reshape(CONV_W, SUBLANES, LANES), conv_b.reshape(SUBLANES, LANES), *gate_args(0))
    return pl.pallas_call(
        _rg_bwd_kernel,
        grid=(b, nlat + 1),
        in_specs=[main_b] + gate_specs + [main_b, main_b],
        out_specs=pl.BlockSpec((TL, d), lambda bi_, j: (bwd(bi_, j), 0)),
        out_shape=jax.ShapeDtypeStruct((t, d), BF16),
        scratch_shapes=[buf, buf, buf, pltpu.VMEM((SUBLANES, LANES), F32)],
        compiler_params=_cp("parallel", "arbitrary"),
        name="rg_scan_bwd",
    )(xc8, *gate_args(1), hf8, gg8)


def _rglru_mixer(xa, pre, g, mod, w_in, conv_w, conv_b, wa, ba, wi, bi, lam, w_out, route, s, c, b, nrows_out):
    x, gg8, xin8 = _rg_in(xa, pre, g, mod, w_in.astype(BF16), s, b)
    y = _rg_scans(xin8, gg8, conv_w, conv_b, wa, wi, ba, bi, lam, s, c, b)
    return _out_proj(y, x, mod, w_out.astype(BF16), route, nrows_out, s, b)


def _rope_tables(s):
    pos = jnp.arange(s, dtype=F32)
    row = jnp.floor(pos / GRID_W)
    col = pos - row * GRID_W
    n_freq = HEAD_DIM // 4
    inv = ROPE_THETA ** (-jnp.arange(n_freq, dtype=F32) * 2.0 / (HEAD_DIM // 2))
    ar = row[:, None] * inv
    ac = col[:, None] * inv
    cos = jnp.concatenate([jnp.cos(ar), jnp.cos(ar), jnp.cos(ac), jnp.cos(ac)], axis=1)
    sin = jnp.concatenate([-jnp.sin(ar), jnp.sin(ar), -jnp.sin(ac), jnp.sin(ac)], axis=1)
    cos = jnp.concatenate([cos, jnp.ones((TM, HEAD_DIM), F32)], axis=0)
    sin = jnp.concatenate([sin, jnp.zeros((TM, HEAD_DIM), F32)], axis=0)
    return cos, sin


def _qkv_kernel(*refs, has_pre):
    x_ref, pre_refs, (g_ref, mod_ref, w_ref, qg_ref, kg_ref, cos_ref, sin_ref, *outs) = _split_refs(refs, has_pre)
    q_ref, k_ref, v_ref = outs[-3:]
    d = x_ref.shape[-1]
    m = mod_ref[0]
    x = _combined_rows(x_ref, pre_refs)
    if has_pre:
        outs[0][...] = x
    h = _norm_mod(x, g_ref[...], m[:, 0:d], m[:, d:2 * d])
    z = jnp.dot(h.astype(BF16), w_ref[...], preferred_element_type=F32)
    cos = cos_ref[...]
    sin = sin_ref[...]
    src = lax.broadcasted_iota(I32, (HEAD_DIM, HEAD_DIM), 0)
    dst = lax.broadcasted_iota(I32, (HEAD_DIM, HEAD_DIM), 1)
    quarter = HEAD_DIM // 4
    partner_of = jnp.where((dst % (2 * quarter)) < quarter, dst + quarter, dst - quarter)
    swap = jnp.where(src == partner_of, 1.0, 0.0).astype(BF16)

    def head(zc, gain):
        ms = jnp.mean(zc * zc, axis=-1, keepdims=True)
        y = zc * lax.rsqrt(ms + NORM_EPS) * gain
        partner = jnp.dot(y.astype(BF16), swap, preferred_element_type=F32)
        return y * cos + partner * sin

    nq = q_ref.shape[-1] // HEAD_DIM
    nk = k_ref.shape[-1] // HEAD_DIM
    for j in range(nq):
        q_ref[:, j * HEAD_DIM:(j + 1) * HEAD_DIM] = (
            head(z[:, j * HEAD_DIM:(j + 1) * HEAD_DIM], qg_ref[...]) * (HEAD_DIM ** -0.5 * LOG2E)).astype(BF16)
    for j in range(nk):
        c0 = (nq + j) * HEAD_DIM
        k_ref[:, j * HEAD_DIM:(j + 1) * HEAD_DIM] = head(z[:, c0:c0 + HEAD_DIM], kg_ref[...]).astype(BF16)
    v_ref[...] = z[:, (nq + nk) * HEAD_DIM:].astype(BF16)


def _qkv(x, pre, g, mod, w, qg, kg, cos, sin, s, b):
    t, d = x.shape
    nkv = N_KV_HEADS * HEAD_DIM
    n_pos = s // TM
    pre_specs, pre_args = _pre_io(pre, d, TM, s, b)
    row = pl.BlockSpec((TM, d), lambda i: (i, 0))
    outs = pl.pallas_call(
        functools.partial(_qkv_kernel, has_pre=pre is not None),
        grid=(t // TM,),
        in_specs=[row] + pre_specs + [
            pl.BlockSpec((1, d), lambda i: (0, 0)),
            _mod_spec(d, s // TM, b),
            pl.BlockSpec(w.shape, lambda i: (0, 0)),
            pl.BlockSpec((1, HEAD_DIM), lambda i: (0, 0)),
            pl.BlockSpec((1, HEAD_DIM), lambda i: (0, 0)),
            pl.BlockSpec((TM, HEAD_DIM), lambda i: (jnp.where(i < b * n_pos, i % n_pos, n_pos), 0)),
            pl.BlockSpec((TM, HEAD_DIM), lambda i: (jnp.where(i < b * n_pos, i % n_pos, n_pos), 0))],
        out_specs=([row] if pre else []) + [pl.BlockSpec((TM, d), lambda i: (i, 0)),
                                           pl.BlockSpec((TM, nkv), lambda i: (i, 0)),
                                           pl.BlockSpec((TM, nkv), lambda i: (i, 0))],
        out_shape=([jax.ShapeDtypeStruct((t, d), F32)] if pre else [])
        + [jax.ShapeDtypeStruct((t, d), BF16), jax.ShapeDtypeStruct((t, nkv), BF16),
           jax.ShapeDtypeStruct((t, nkv), BF16)],
        compiler_params=_cp("parallel"),
        name="qkv_proj",
    )(x, *pre_args, g, mod, w, qg, kg, cos, sin)
    return tuple(outs) if pre else (x, *outs)


def _attn_kernel(q_ref, kc_ref, vc_ref, *rest, n_lat):
    if n_lat:
        kl_ref, vl_ref, o_ref, s_scr, vaug = rest
    else:
        o_ref, s_scr, vaug = rest
    n_ctx = kc_ref.shape[0]
    tq = q_ref.shape[0]

    def fill_values():
        vaug[:, HEAD_DIM:] = jnp.ones((n_ctx + n_lat, HEAD_DIM), BF16)
        vaug[0:n_ctx, 0:HEAD_DIM] = vc_ref[...]
        if n_lat:
            vaug[n_ctx:, 0:HEAD_DIM] = vl_ref[...]

    if n_lat:
        pl.when(pl.program_id(2) == 0)(fill_values)
    else:
        fill_values()

    chunks = [(0, n_ctx)] + [(n_ctx + j, ATT_KC) for j in range(0, n_lat, ATT_KC)]
    nt = (((1,), (1,)), ((), ()))
    q_all = jnp.concatenate([q_ref[:, g * HEAD_DIM:(g + 1) * HEAD_DIM] for g in range(GQA_GROUP)], axis=0)
    m_part = jnp.full((GQA_GROUP * tq, LANES), -jnp.inf, F32)
    for off, size in chunks:
        keys = kc_ref[...] if off == 0 else kl_ref[off - n_ctx:off - n_ctx + size, :]
        sc = lax.dot_general(q_all, keys, nt, preferred_element_type=F32)
        s_scr[:, off:off + size] = sc
        for j in range(0, size, LANES):
            m_part = jnp.maximum(m_part, sc[:, j:j + LANES])
    m_row = jnp.max(m_part, axis=-1, keepdims=True)
    hr = GQA_GROUP * tq // 2
    acc = [jnp.zeros((hr, 2 * HEAD_DIM), F32), jnp.zeros((hr, 2 * HEAD_DIM), F32)]
    for off, size in chunks:
        for r in range(2):
            rows = slice(r * hr, (r + 1) * hr)
            p = jnp.exp2((s_scr[rows, off:off + size] - m_row[rows]).astype(BF16))
            acc[r] = acc[r] + jnp.dot(p, vaug[off:off + size, :], preferred_element_type=F32)
    for r in range(2):
        out = (acc[r][:, :HEAD_DIM] / acc[r][:, HEAD_DIM:]).astype(BF16)
        for j in range(GQA_GROUP // 2):
            g = r * (GQA_GROUP // 2) + j
            o_ref[:, g * HEAD_DIM:(g + 1) * HEAD_DIM] = out[j * tq:(j + 1) * tq]


def _attn_ctx_kernel(q_ref, kc_ref, vc_ref, o_all_ref, o_ref, s_scr, vaug):
    del o_all_ref
    _attn_kernel(q_ref, kc_ref, vc_ref, o_ref, s_scr, vaug, n_lat=0)


def _attention(q, k, v, s, c, b):
    t, d = q.shape
    gw = GQA_GROUP * HEAD_DIM
    tq = TQ
    nq = s // tq
    assert s % ATT_KC == 0
    ctx_blk = lambda bi, h, *_: ((b * s) // c + bi, h)
    o_lat = pl.pallas_call(
        functools.partial(_attn_kernel, n_lat=s),
        grid=(b, N_KV_HEADS, nq),
        in_specs=[pl.BlockSpec((tq, gw), lambda bi, h, i: (bi * nq + i, h)),
                  pl.BlockSpec((c, HEAD_DIM), ctx_blk),
                  pl.BlockSpec((c, HEAD_DIM), ctx_blk),
                  pl.BlockSpec((s, HEAD_DIM), lambda bi, h, i: (bi, h)),
                  pl.BlockSpec((s, HEAD_DIM), lambda bi, h, i: (bi, h))],
        out_specs=pl.BlockSpec((tq, gw), lambda bi, h, i: (bi * nq + i, h)),
        out_shape=jax.ShapeDtypeStruct((t, d), BF16),
        scratch_shapes=[pltpu.VMEM((GQA_GROUP * tq, c + s), F32), pltpu.VMEM((c + s, 2 * HEAD_DIM), BF16)],
        compiler_params=_cp("parallel", "parallel", "arbitrary"),
        name="attn_lat",
    )(q, k, v, k, v)
    return pl.pallas_call(
        _attn_ctx_kernel,
        grid=(b, N_KV_HEADS),
        in_specs=[pl.BlockSpec((c, gw), ctx_blk),
                  pl.BlockSpec((c, HEAD_DIM), ctx_blk),
                  pl.BlockSpec((c, HEAD_DIM), ctx_blk),
                  pl.BlockSpec(memory_space=pl.ANY)],
        out_specs=pl.BlockSpec((c, gw), ctx_blk),
        out_shape=jax.ShapeDtypeStruct((t, d), BF16),
        scratch_shapes=[pltpu.VMEM((GQA_GROUP * c, c), F32), pltpu.VMEM((c, 2 * HEAD_DIM), BF16)],
        input_output_aliases={3: 0},
        compiler_params=_cp("parallel", "parallel"),
        name="attn_ctx",
    )(q, k, v, o_lat)


def _attention_mixer(xa, pre, g, mod, w_qkv, qg, kg, w_o, route, s, c, b):
    cos, sin = _rope_tables(s)
    x, q, k, v = _qkv(xa, pre, g, mod, w_qkv.astype(BF16), qg.reshape(1, -1), kg.reshape(1, -1), cos, sin, s, b)
    o = _attention(q, k, v, s, c, b)
    return _out_proj(o, x, mod, w_o.astype(BF16), route, x.shape[0], s, b)


def _gmlp_kernel(*refs, has_pre):
    x_ref, pre_refs, rest = _split_refs(refs, has_pre)
    g_ref, mod_ref, w_in_ref, lng_ref, lnb_ref, ws_ref, bs_ref, w_out_ref = rest[:8]
    route_in, (o_ref, *route_out, uv_ref, cnt_s) = rest[8:12], rest[12:]
    d = x_ref.shape[-1]
    dcm = lng_ref.shape[-1]
    gw = dcm // CM_GROUPS
    x = _combined_rows(x_ref, pre_refs)
    m = mod_ref[0]
    h = _norm_mod(x, g_ref[...], m[:, 0:d], m[:, d:2 * d])
    z = jax.nn.gelu(jnp.dot(h.astype(BF16), w_in_ref[...], preferred_element_type=F32))
    u = z[:, :dcm]
    v = z[:, dcm:]
    mu = jnp.mean(v, axis=-1, keepdims=True)
    vc = v - mu
    var = jnp.mean(vc * vc, axis=-1, keepdims=True)
    vn = (vc * lax.rsqrt(var + NORM_EPS) * lng_ref[...] + lnb_ref[...]).astype(BF16)
    for ck in range(x.shape[0] // CHUNK):
        rows = slice(ck * CHUNK, (ck + 1) * CHUNK)
        for gi in range(CM_GROUPS):
            cols = slice(gi * gw, (gi + 1) * gw)
            mix = jnp.dot(ws_ref[gi], vn[rows, cols], preferred_element_type=F32) + bs_ref[:, gi:gi + 1]
            uv_ref[rows, cols] = (u[rows, cols] * mix).astype(BF16)
    y = jnp.dot(uv_ref[...], w_out_ref[...], preferred_element_type=F32)
    x1 = x + m[:, 2 * d:3 * d] * y
    o_ref[...] = x1
    _route_rows(x1, m, *route_in, *route_out, cnt_s)


def _gmlp_mixer(xa, pre, g, mod, w_in, ln_g, ln_b, w_s, b_s, w_out, route, s, b):
    t, d = xa.shape
    dcm = ln_g.shape[-1]
    full = lambda shape: pl.BlockSpec(shape, lambda i: (0,) * len(shape))
    pre_specs, pre_args = _pre_io(pre, d, TG, s, b)
    r_in, r_out, r_shape, r_scratch = _route_io(d, t, TG)
    x1, *routed = pl.pallas_call(
        functools.partial(_gmlp_kernel, has_pre=pre is not None),
        grid=(t // TG,),
        in_specs=[pl.BlockSpec((TG, d), lambda i: (i, 0))] + pre_specs + [
            full((1, d)),
            _mod_spec(d, s // TG, b),
            full((d, 2 * dcm)), full((1, dcm)), full((1, dcm)),
            full((CM_GROUPS, CHUNK, CHUNK)), full((CHUNK, CM_GROUPS)), full((dcm, d))] + r_in,
        out_specs=[pl.BlockSpec((TG, d), lambda i: (i, 0))] + r_out,
        out_shape=[jax.ShapeDtypeStruct((t, d), F32)] + r_shape,
        scratch_shapes=[pltpu.VMEM((TG, dcm), BF16)] + r_scratch,
        compiler_params=_cp("arbitrary"),
        name="gmlp",
    )(xa, *pre_args, g, mod, w_in.astype(BF16), ln_g.reshape(1, dcm), ln_b.reshape(1, dcm),
      w_s.astype(BF16), b_s.T, w_out.astype(BF16), *route)
    return x1, routed


def _route_rows(x1, m, g_ref, wrh_ref, wrl_ref, br_ref, hf_ref, rt_ref, ew_ref, cnt_ref, cnt_s):
    tm, d = x1.shape

    @pl.when(pl.program_id(0) == 0)
    def _():
        cnt_s[...] = jnp.zeros_like(cnt_s)

    hf = _norm_mod(x1, g_ref[...], m[:, 3 * d:4 * d], m[:, 4 * d:5 * d])
    hf_ref[...] = _pack_bf16_pairs(hf)
    hf_hi = hf.astype(BF16)
    hf_lo = (hf - hf_hi.astype(F32)).astype(BF16)
    logits = (jnp.dot(hf_hi, wrh_ref[...], preferred_element_type=F32)
              + jnp.dot(hf_lo, wrh_ref[...], preferred_element_type=F32)
              + jnp.dot(hf_hi, wrl_ref[...], preferred_element_type=F32)) + br_ref[...]
    lane = lax.broadcasted_iota(I32, logits.shape, 1)
    neg = -jnp.inf
    gl = jnp.where(lane < N_GROUPS, logits, neg)
    gmax = jnp.max(gl, axis=-1, keepdims=True)
    gsel = jnp.min(jnp.where(gl == gmax, lane, LANES), axis=-1, keepdims=True)
    gate_g = 1.0 / jnp.sum(jnp.exp(gl - gmax), axis=-1, keepdims=True)
    lo = N_GROUPS + gsel * EXPERTS_PER_GROUP
    el = jnp.where(jnp.logical_and(lane >= lo, lane < lo + EXPERTS_PER_GROUP), logits, neg)
    v1 = jnp.max(el, axis=-1, keepdims=True)
    i1 = jnp.min(jnp.where(el == v1, lane, LANES), axis=-1, keepdims=True)
    el2 = jnp.where(lane == i1, neg, el)
    v2 = jnp.max(el2, axis=-1, keepdims=True)
    i2 = jnp.min(jnp.where(el2 == v2, lane, LANES), axis=-1, keepdims=True)
    e21 = jnp.exp(v2 - v1)
    w1 = gate_g / (1.0 + e21)
    w2 = w1 * e21
    ew_ref[...] = jnp.where(lane == 0, w1, jnp.where(lane == 1, w2, 0.0))

    oh1 = lane == i1
    oh2 = lane == i2
    above = (lax.broadcasted_iota(I32, (tm, tm), 1) < lax.broadcasted_iota(I32, (tm, tm), 0)).astype(BF16)
    pre1 = jnp.dot(above, oh1.astype(BF16), preferred_element_type=F32)
    pre2 = jnp.dot(above, oh2.astype(BF16), preferred_element_type=F32)
    tot1 = jnp.sum(oh1.astype(F32), axis=0, keepdims=True)
    tot2 = jnp.sum(oh2.astype(F32), axis=0, keepdims=True)
    cnt = cnt_s[...]
    rank1 = jnp.sum(jnp.where(oh1, cnt + pre1, 0.0), axis=-1, keepdims=True).astype(I32)
    rank2 = jnp.sum(jnp.where(oh2, cnt + tot1 + pre2, 0.0), axis=-1, keepdims=True).astype(I32)
    cnt = cnt + tot1 + tot2
    cnt_s[...] = cnt
    cnt_ref[...] = jnp.broadcast_to(cnt, cnt_ref.shape).astype(I32)
    rt = jnp.where(lane == 0, i1 - N_GROUPS, jnp.where(lane == 1, i2 - N_GROUPS,
                   jnp.where(lane == 2, rank1, jnp.where(lane == 3, rank2, 0))))
    rt_ref[...] = rt.T[:SUBLANES]


def _route_params(g_ffn, w_group, b_group, w_router, b_router):
    d = w_group.shape[0]
    pad = LANES - N_GROUPS - N_EXPERTS
    wr = jnp.concatenate([w_group, w_router.reshape(d, N_EXPERTS), jnp.zeros((d, pad), F32)], axis=1)
    br = jnp.concatenate([b_group, b_router.reshape(N_EXPERTS), jnp.zeros((pad,), F32)]).reshape(1, LANES)
    wr_hi = wr.astype(BF16)
    wr_lo = (wr - wr_hi.astype(F32)).astype(BF16)
    return g_ffn.reshape(1, d), wr_hi, wr_lo, br


def _route_io(d, nrows, tr):
    const = lambda shape: pl.BlockSpec(shape, lambda i: (0, 0))
    row = lambda w: pl.BlockSpec((tr, w), lambda i: (i, 0))
    in_specs = [const((1, d)), const((d, LANES)), const((d, LANES)), const((1, LANES))]
    out_specs = [row(d // 2), pl.BlockSpec((SUBLANES, tr), lambda i: (0, i)), row(LANES), const((SUBLANES, LANES))]
    out_shape = [jax.ShapeDtypeStruct((nrows, d // 2), I32), jax.ShapeDtypeStruct((SUBLANES, nrows), I32),
                 jax.ShapeDtypeStruct((nrows, LANES), F32), jax.ShapeDtypeStruct((SUBLANES, LANES), I32)]
    return in_specs, out_specs, out_shape, [pltpu.VMEM((1, LANES), F32)]


def _plan_kernel(cnt_ref, rt_ref, pos_ref, blk_e_ref, n_used_ref, first_ref, slot_ref, next_ref,
                 start_s, end_s, nxt_s, *, bm):
    n_blk = blk_e_ref.shape[0]
    acc = jnp.int32(0)
    for e in range(N_EXPERTS):
        start_s[e] = acc
        acc = acc + (cnt_ref[0, N_GROUPS + e] + (bm - 1)) // bm * bm
        end_s[e] = acc
    n_used = acc // bm
    n_used_ref[0] = n_used
    nxt = jnp.int32(-1)
    for e in reversed(range(N_EXPERTS)):
        nxt_s[e] = nxt
        nxt = jnp.where(cnt_ref[0, N_GROUPS + e] > 0, e, nxt)

    def block(i, carry):
        prev_e, runs = carry
        row = jnp.minimum(i, n_used - 1) * bm
        e = lax.while_loop(lambda v: jnp.logical_and(v < N_EXPERTS - 1, end_s[v] <= row), lambda v: v + 1,
                           jnp.maximum(prev_e, 0))
        first = jnp.logical_and(i < n_used, prev_e != e)
        runs = runs + first.astype(I32)
        blk_e_ref[i] = e
        first_ref[i] = first.astype(I32)
        slot_ref[i] = (runs - 1) % 2
        next_ref[i] = nxt_s[e]
        return e, runs

    lax.fori_loop(0, n_blk, block, (jnp.int32(-1), jnp.int32(0)))

    rt = rt_ref[...]
    start_of = jnp.zeros_like(rt)
    for e in range(N_EXPERTS):
        start_of = jnp.where(rt == e, start_s[e], start_of)
    pos_ref[...] = start_of + pltpu.roll(rt, SUBLANES - 2, 0)


def _dispatch_plan(rt, cnt, bm):
    n_tok = rt.shape[1]
    n_rows = 2 * n_tok + N_EXPERTS * bm
    n_blk = n_rows // bm
    smem = pl.BlockSpec(memory_space=pltpu.SMEM)
    vec = lambda n: jax.ShapeDtypeStruct((n,), I32)
    pos, blk_e, n_used, first, slot, nxt = pl.pallas_call(
        functools.partial(_plan_kernel, bm=bm),
        in_specs=[smem, pl.BlockSpec(memory_space=pltpu.VMEM)],
        out_specs=[pl.BlockSpec(memory_space=pltpu.VMEM), smem, smem, smem, smem, smem],
        out_shape=[jax.ShapeDtypeStruct(rt.shape, I32), vec(n_blk), vec(1), vec(n_blk), vec(n_blk), vec(n_blk)],
        scratch_shapes=[pltpu.SMEM((N_EXPERTS,), I32)] * 3,
        name="moe_plan",
    )(cnt, rt)
    return pos, (blk_e, n_used, first, slot, nxt), n_rows


def _sc_mesh():
    return plsc.VectorSubcoreMesh(core_axis_name="c", subcore_axis_name="s")


def _sc_worker_base(per_worker):
    return (lax.axis_index("s") * SC_CORES + lax.axis_index("c")) * per_worker


def _sc_dispatch(hf, pos, n_rows):
    t, d = hf.shape
    per_w = t // SC_WORKERS
    ch = SC_CHUNK
    n_ck = per_w // ch
    assert per_w * SC_WORKERS == t and n_ck * ch == per_w

    @functools.partial(
        pl.kernel, mesh=_sc_mesh(), out_type=jax.ShapeDtypeStruct((n_rows, d), hf.dtype),
        scratch_types=[pltpu.VMEM((per_w,), I32), pltpu.VMEM((per_w,), I32), pltpu.VMEM((2, ch, d), hf.dtype),
                       pltpu.SemaphoreType.DMA((2,)), pltpu.SemaphoreType.DMA((2,)), pltpu.SemaphoreType.DMA((2,))])
    def dispatch(hf_hbm, p_hbm, out_hbm, i0_v, i1_v, rows_v, sem_in, sem_s0, sem_s1):
        base = pl.multiple_of(_sc_worker_base(per_w), SUBLANES)
        pltpu.sync_copy(p_hbm.at[pl.ds(base, per_w)], i0_v)
        pltpu.sync_copy(p_hbm.at[pl.ds(pl.multiple_of(t + base, SUBLANES), per_w)], i1_v)

        def load(ck):
            return pltpu.make_async_copy(hf_hbm.at[pl.ds(base + ck * ch, ch)], rows_v.at[ck % 2], sem_in.at[ck % 2])

        def scatters(ck):
            src = rows_v.at[ck % 2]
            return (pltpu.make_async_copy(src, out_hbm.at[i0_v.at[pl.ds(ck * ch, ch)]], sem_s0.at[ck % 2]),
                    pltpu.make_async_copy(src, out_hbm.at[i1_v.at[pl.ds(ck * ch, ch)]], sem_s1.at[ck % 2]))

        load(0).start()
        for ck in range(n_ck):
            load(ck).wait()
            if ck + 1 < n_ck:
                if ck >= 1:
                    for cp in scatters(ck - 1):
                        cp.wait()
                load(ck + 1).start()
            for cp in scatters(ck):
                cp.start()
        for ck in range(max(n_ck - 2, 0), n_ck):
            for cp in scatters(ck):
                cp.wait()

    return dispatch(hf, pos)


def _sc_gather(rows, idx):
    n = idx.shape[0]
    d = rows.shape[1]
    per_w = n // SC_WORKERS
    ch = 2 * SC_CHUNK
    n_ck = per_w // ch
    assert per_w * SC_WORKERS == n and n_ck * ch == per_w

    @functools.partial(
        pl.kernel, mesh=_sc_mesh(), out_type=jax.ShapeDtypeStruct((n, d), rows.dtype),
        scratch_types=[pltpu.VMEM((per_w,), I32), pltpu.VMEM((2, ch, d), rows.dtype),
                       pltpu.SemaphoreType.DMA((2,)), pltpu.SemaphoreType.DMA((2,))])
    def gather(rows_hbm, i_hbm, out_hbm, i_v, buf, sem_g, sem_w):
        base = pl.multiple_of(_sc_worker_base(per_w), SUBLANES)
        pltpu.sync_copy(i_hbm.at[pl.ds(base, per_w)], i_v)

        def fetch(ck):
            return pltpu.make_async_copy(rows_hbm.at[i_v.at[pl.ds(ck * ch, ch)]], buf.at[ck % 2], sem_g.at[ck % 2])

        def write(ck):
            return pltpu.make_async_copy(buf.at[ck % 2], out_hbm.at[pl.ds(base + ck * ch, ch)], sem_w.at[ck % 2])

        fetch(0).start()
        for ck in range(n_ck):
            fetch(ck).wait()
            if ck + 1 < n_ck:
                if ck >= 1:
                    write(ck - 1).wait()
                fetch(ck + 1).start()
            write(ck).start()
        for ck in range(max(n_ck - 2, 0), n_ck):
            write(ck).wait()

    return gather(rows, idx)


def _expert_kernel(blk_e_ref, n_used_ref, first_ref, slot_ref, next_ref, x_ref, wg_hbm, wu_hbm, wd_hbm, y_ref,
                   wgf, wuf, wdf, wgb, wub, wdb, sem, *, e_base):
    n_used = n_used_ref[0]
    bm = x_ref.shape[0] // MOE_SUB

    def weight_copies(e, slot):
        return (pltpu.make_async_copy(wg_hbm.at[e_base + e], wgf.at[slot], sem.at[slot, 0]),
                pltpu.make_async_copy(wu_hbm.at[e_base + e], wuf.at[slot], sem.at[slot, 1]),
                pltpu.make_async_copy(wd_hbm.at[e_base + e], wdf.at[slot], sem.at[slot, 2]))

    @pl.when(pl.program_id(0) == 0)
    def _():
        for cp in weight_copies(blk_e_ref[0], 0):
            cp.start()

    for j in range(MOE_SUB):
        blk = pl.program_id(0) * MOE_SUB + j
        rows = slice(j * bm, (j + 1) * bm)

        @pl.when(jnp.logical_and(blk < n_used, first_ref[blk] == 1))
        def _():
            slot = slot_ref[blk]
            for cp in weight_copies(blk_e_ref[blk], slot):
                cp.wait()
            nxt = next_ref[blk]

            @pl.when(nxt >= 0)
            def _():
                for cp in weight_copies(nxt, 1 - slot):
                    cp.start()

            wgb[...] = wgf[slot].astype(BF16)
            wub[...] = wuf[slot].astype(BF16)
            wdb[...] = wdf[slot].astype(BF16)

        @pl.when(blk < n_used)
        def _():
            x_hi, x_lo = _unpack_bf16_pairs(x_ref[rows, :])
            xb = jnp.concatenate([x_hi.astype(BF16), x_lo.astype(BF16)], axis=1)
            gt = jnp.dot(xb, wgb[...], preferred_element_type=F32)
            up = jnp.dot(xb, wub[...], preferred_element_type=F32)
            act = (gt * _sigmoid(gt) * up).astype(BF16)
            y_ref[rows, :] = _pack_bf16_pairs(jnp.dot(act, wdb[...], preferred_element_type=F32))


def _experts(x_rows, plan, w_gate, w_up, w_down, layer):
    n_rows, dp = x_rows.shape
    depth, n_e, d, de = w_gate.shape
    step_rows = MOE_SUB * MOE_BM
    assert n_rows % step_rows == 0
    any_spec = pl.BlockSpec(memory_space=pl.ANY)
    last_used = lambda i, be, nu, *_: (jnp.minimum(i, (nu[0] - 1) // MOE_SUB), 0)
    grid_spec = pltpu.PrefetchScalarGridSpec(
        num_scalar_prefetch=5,
        grid=(n_rows // step_rows,),
        in_specs=[pl.BlockSpec((step_rows, dp), last_used), any_spec, any_spec, any_spec],
        out_specs=pl.BlockSpec((step_rows, dp), last_used),
        scratch_shapes=[pltpu.VMEM((2, d, de), F32), pltpu.VMEM((2, d, de), F32), pltpu.VMEM((2, de, d), F32),
                        pltpu.VMEM((d, de), BF16), pltpu.VMEM((d, de), BF16), pltpu.VMEM((de, d), BF16),
                        pltpu.SemaphoreType.DMA((2, 3))],
    )
    return pl.pallas_call(
        functools.partial(_expert_kernel, e_base=layer * n_e),
        grid_spec=grid_spec,
        out_shape=jax.ShapeDtypeStruct((n_rows, dp), I32),
        compiler_params=_cp("arbitrary"),
        name="moe_experts",
    )(*plan, x_rows, w_gate.reshape(depth * n_e, d, de), w_up.reshape(depth * n_e, d, de),
      w_down.reshape(depth * n_e, de, d))


def _combine_kernel(x_ref, *rest):
    *pre_refs, o_ref = rest
    o_ref[...] = _combined_rows(x_ref, pre_refs)


def _combine(x, pre, s, b):
    nrows, d = x.shape
    pre_specs, pre_args = _pre_io(pre, d, TR, s, b)
    return pl.pallas_call(
        _combine_kernel,
        grid=(nrows // TR,),
        in_specs=[pl.BlockSpec((TR, d), lambda i: (i, 0))] + pre_specs,
        out_specs=pl.BlockSpec((TR, d), lambda i: (i, 0)),
        out_shape=jax.ShapeDtypeStruct((nrows, d), F32),
        compiler_params=_cp("parallel"),
        name="moe_combine",
    )(x, *pre_args)


def _expert_outputs(routed, w_gate, w_up, w_down, layer):
    hf, rt, _, cnt = routed
    pos, plan, n_rows = _dispatch_plan(rt, cnt, MOE_BM)
    pos = pos[0:2].reshape(-1)
    x_rows = _sc_dispatch(hf, pos, n_rows)
    y_rows = _experts(x_rows, plan, w_gate, w_up, w_down, layer)
    return _sc_gather(y_rows, pos)


def kernel(x, c, ctx, c_ctx, ada_w, ada_b, norm_mix_g, norm_ffn_g, rg_w_in, rg_conv_w, rg_conv_b, rg_wa, rg_ba, rg_wi, rg_bi, rg_lambda, rg_w_out, at_w_qkv, at_q_g, at_k_g, at_w_o, cm_w_in, cm_ln_g, cm_ln_b, cm_w_s, cm_b_s, cm_w_out, moe_w_group, moe_b_group, moe_w_router, moe_b_router, moe_w_gate, moe_w_up, moe_w_down):
    b, s, d = x.shape
    cl = ctx.shape[1]
    depth = ada_w.shape[0]
    n_lat = b * s
    assert b < SUBLANES and s % max(TI, TR, TG) == 0 and (b * cl) % max(TI, TR, TG) == 0 and cl % TM == 0
    assert d == RG_BLOCKS * LANES

    cin = jnp.concatenate([c, c_ctx[None, :], jnp.zeros((SUBLANES - b - 1, d), F32)], axis=0)
    mod_all = _ada_table(cin, ada_w, ada_b).reshape(depth, SUBLANES, 1, N_MOD * d)
    tok = jnp.concatenate([x.reshape(n_lat, d), ctx.reshape(b * cl, d)], axis=0)

    xa, pre = tok, None
    for l in range(depth):
        kind = l % 3
        j = l // 3
        last = l == depth - 1
        mod = mod_all[l]
        g_mix = norm_mix_g[l].reshape(1, d)
        route = _route_params(norm_ffn_g[l], moe_w_group[l], moe_b_group[l], moe_w_router[l], moe_b_router[l])
        if kind == 0:
            x1, routed = _rglru_mixer(xa, pre, g_mix, mod, rg_w_in[j], rg_conv_w[j], rg_conv_b[j], rg_wa[j],
                                      rg_ba[j], rg_wi[j], rg_bi[j], rg_lambda[j], rg_w_out[j], route, s, cl, b,
                                      n_lat if last else xa.shape[0])
        elif kind == 1:
            x1, routed = _attention_mixer(xa, pre, g_mix, mod, at_w_qkv[j], at_q_g[j], at_k_g[j], at_w_o[j],
                                          route, s, cl, b)
        else:
            x1, routed = _gmlp_mixer(xa, pre, g_mix, mod, cm_w_in[j], cm_ln_g[j], cm_ln_b[j], cm_w_s[j],
                                     cm_b_s[j], cm_w_out[j], route, s, b)
        y01 = _expert_outputs(routed, moe_w_gate, moe_w_up, moe_w_down, l)
        xa, pre = x1, (y01, routed[2], mod)
    return _combine(xa, pre, s, b)[:n_lat].reshape(b, s, d)
```

```python
import functools

import jax
import jax.numpy as jnp
from jax import lax
from jax.experimental import pallas as pl
from jax.experimental.pallas import tpu as pltpu
from jax.experimental.pallas import tpu_sc as plsc

F32 = jnp.float32
BF16 = jnp.bfloat16
I32 = jnp.int32
U32 = jnp.uint32

NORM_EPS = 1e-6
N_MOD = 6
GRID_W = 64
RG_BLOCKS = 8
CONV_W = 4
RG_C = 8.0
HEAD_DIM = 128
N_KV_HEADS = 2
GQA_GROUP = 4
ROPE_THETA = 10000.0
CHUNK = 128
CM_GROUPS = 8
N_GROUPS = 4
EXPERTS_PER_GROUP = 8
N_EXPERTS = N_GROUPS * EXPERTS_PER_GROUP

LANES = 128
SUBLANES = 8
TM = 256
TQ = 512
TI = 1024
TR = 1024
TG = 512
TL = 256
HALO = 8
ATT_KC = 512
LOG2E = 1.4426950408889634
MOE_BM = 256
MOE_SUB = 4
SC_CORES = 2
SC_WORKERS = 32
SC_CHUNK = 32
VMEM_LIMIT = 52 * 2**20


def _cp(*sem):
    return pltpu.CompilerParams(dimension_semantics=sem, vmem_limit_bytes=VMEM_LIMIT)


def _norm_mod(x, g, shift, scale):
    ms = jnp.mean(x * x, axis=-1, keepdims=True)
    y = x * lax.rsqrt(ms + NORM_EPS) * g
    return y * (1.0 + scale) + shift


def _sigmoid(x):
    return 0.5 * jnp.tanh(0.5 * x) + 0.5


def _pack_bf16_pairs(x):
    h = x.shape[-1] // 2
    hi = lax.bitcast_convert_type(x[:, :h].astype(BF16).astype(F32), U32)
    lo = lax.bitcast_convert_type(x[:, h:].astype(BF16).astype(F32), U32)
    return lax.bitcast_convert_type(hi | (lo >> 16), I32)


def _unpack_bf16_pairs(w):
    u = lax.bitcast_convert_type(w, U32)
    hi = lax.bitcast_convert_type(u & jnp.uint32(0xFFFF0000), F32)
    lo = lax.bitcast_convert_type(u << 16, F32)
    return hi, lo


def _combined_rows(x_ref, pre_refs):
    if not pre_refs:
        return x_ref[...]
    y0_ref, y1_ref, ew_ref, modp_ref = pre_refs
    d = x_ref.shape[1]
    ew = ew_ref[...]
    y0_hi, y0_lo = _unpack_bf16_pairs(y0_ref[...])
    y1_hi, y1_lo = _unpack_bf16_pairs(y1_ref[...])
    y = jnp.concatenate([ew[:, 0:1] * y0_hi + ew[:, 1:2] * y1_hi, ew[:, 0:1] * y0_lo + ew[:, 1:2] * y1_lo], axis=1)
    return x_ref[...] + modp_ref[0][:, 5 * d:6 * d] * y


def _pre_io(pre, d, tr, s, b):
    if pre is None:
        return [], []
    y01, ew, mod_prev = pre
    nb = y01.shape[0] // 2 // tr
    specs = [pl.BlockSpec((tr, d // 2), lambda i: (i, 0)), pl.BlockSpec((tr, d // 2), lambda i: (i + nb, 0)),
             pl.BlockSpec((tr, LANES), lambda i: (i, 0)), _mod_spec(d, s // tr, b)]
    return specs, [y01, y01, ew, mod_prev]


def _split_refs(refs, has_pre):
    return (refs[0], refs[1:5], refs[5:]) if has_pre else (refs[0], (), refs[1:])


def _mod_spec(d, rows_per_sample, n_samples):
    return pl.BlockSpec((1, 1, N_MOD * d),
                        lambda i, *_: (jnp.minimum(i // rows_per_sample, n_samples), 0, 0))


def _ada_kernel(c_ref, w_ref, b_ref, o_ref):
    cin = c_ref[...]
    act = cin * jax.nn.sigmoid(cin)
    w = w_ref[0]
    w_hi = w.astype(BF16)
    w_lo = (w - w_hi.astype(F32)).astype(BF16)
    a_hi = act.astype(BF16)
    a_lo = (act - a_hi.astype(F32)).astype(BF16)
    o_ref[0] = (jnp.dot(a_hi, w_hi, preferred_element_type=F32) + jnp.dot(a_lo, w_hi, preferred_element_type=F32)
                + jnp.dot(a_hi, w_lo, preferred_element_type=F32)) + b_ref[0]


def _ada_table(cin, ada_w, ada_b):
    depth, d, n = ada_w.shape
    tn = 2 * d
    return pl.pallas_call(
        _ada_kernel,
        grid=(depth, n // tn),
        in_specs=[pl.BlockSpec((SUBLANES, d), lambda l, j: (0, 0)),
                  pl.BlockSpec((1, d, tn), lambda l, j: (l, 0, j)),
                  pl.BlockSpec((1, 1, tn), lambda l, j: (l, 0, j))],
        out_specs=pl.BlockSpec((1, SUBLANES, tn), lambda l, j: (l, 0, j)),
        out_shape=jax.ShapeDtypeStruct((depth, SUBLANES, n), F32),
        compiler_params=_cp("parallel", "parallel"),
        name="ada_table",
    )(cin, ada_w, ada_b.reshape(depth, 1, n))


def _out_kernel(y_ref, x_ref, mod_ref, w_ref, *rest):
    route_in, (o_ref, *route_out) = rest[:4], rest[4:]
    d = x_ref.shape[-1]
    m = mod_ref[0]
    y = jnp.dot(y_ref[...].astype(BF16), w_ref[...], preferred_element_type=F32)
    x1 = x_ref[...] + m[:, 2 * d:3 * d] * y
    o_ref[...] = x1
    _route_rows(x1, m, *route_in, *route_out)


def _out_proj(y, x, mod, w, route, nrows, s, b):
    d = x.shape[-1]
    k = y.shape[-1]
    r_in, r_out, r_shape, r_scratch = _route_io(d, nrows, TR)
    x1, *routed = pl.pallas_call(
        _out_kernel,
        grid=(nrows // TR,),
        in_specs=[pl.BlockSpec((TR, k), lambda i: (i, 0)),
                  pl.BlockSpec((TR, d), lambda i: (i, 0)),
                  _mod_spec(d, s // TR, b),
                  pl.BlockSpec((k, d), lambda i: (0, 0))] + r_in,
        out_specs=[pl.BlockSpec((TR, d), lambda i: (i, 0))] + r_out,
        out_shape=[jax.ShapeDtypeStruct((nrows, d), F32)] + r_shape,
        scratch_shapes=r_scratch,
        compiler_params=_cp("arbitrary"),
        name="out_proj",
    )(y, x, mod, w, *route)
    return x1, routed


def _rg_in_kernel(*refs, has_pre):
    x_ref, pre_refs, (g_ref, mod_ref, w_ref, *outs) = _split_refs(refs, has_pre)
    gg_ref, xin_ref = outs[-2:]
    d = x_ref.shape[-1]
    m = mod_ref[0]
    x = _combined_rows(x_ref, pre_refs)
    if has_pre:
        outs[0][...] = x
    h = _norm_mod(x, g_ref[...], m[:, 0:d], m[:, d:2 * d])
    z = jnp.dot(h.astype(BF16), w_ref[...], preferred_element_type=F32)
    tm = x_ref.shape[0]
    for n in range(d // LANES):
        cols = slice(n * LANES, (n + 1) * LANES)
        gg_ref[pl.ds(n, tm, stride=SUBLANES), :] = jax.nn.gelu(z[:, cols])
        xin_ref[pl.ds(n, tm, stride=SUBLANES), :] = z[:, d + n * LANES:d + (n + 1) * LANES]


def _rg_in(x, pre, g, mod, w, s, b):
    t, d = x.shape
    assert d == SUBLANES * LANES
    pre_specs, pre_args = _pre_io(pre, d, TI, s, b)
    row = pl.BlockSpec((TI, d), lambda i: (i, 0))
    tmajor = pl.BlockSpec((TI * SUBLANES, LANES), lambda i: (i, 0))
    outs = pl.pallas_call(
        functools.partial(_rg_in_kernel, has_pre=pre is not None),
        grid=(t // TI,),
        in_specs=[row] + pre_specs + [pl.BlockSpec((1, d), lambda i: (0, 0)), _mod_spec(d, s // TI, b),
                                      pl.BlockSpec((d, 2 * d), lambda i: (0, 0))],
        out_specs=([row] if pre else []) + [tmajor, tmajor],
        out_shape=([jax.ShapeDtypeStruct((t, d), F32)] if pre else [])
        + [jax.ShapeDtypeStruct((t * SUBLANES, LANES), F32)] * 2,
        compiler_params=_cp("parallel"),
        name="rg_in",
    )(x, *pre_args, g, mod, w)
    return (outs[0], outs[1], outs[2]) if pre else (x, outs[0], outs[1])


def _rg_gates_and_scan(xc, wa_ref, wi_ref, ba_ref, bi_ref, lam_ref, a_s, b_s, h_dst, hcar, reverse):
    @pl.when(pl.program_id(1) == 0)
    def _():
        hcar[...] = jnp.zeros_like(hcar)

    for n in range(RG_BLOCKS):
        cols = slice(n * LANES, (n + 1) * LANES)
        xn = xc[pl.ds(n, TL, stride=SUBLANES), :]
        xb = xn.astype(BF16)
        ta = jnp.tanh(jnp.dot(xb, wa_ref[n], preferred_element_type=F32) + ba_ref[:, cols])
        ti = jnp.tanh(jnp.dot(xb, wi_ref[n], preferred_element_type=F32) + bi_ref[:, cols])
        k = (-0.5 * RG_C * LOG2E) * jax.nn.softplus(-lam_ref[:, cols])
        a = jnp.exp2(k * ta + k)
        om = 1.0 - a * a
        root = jnp.where(om > 0.0, om * lax.rsqrt(om), 0.0)
        a_s[pl.ds(n, TL, stride=SUBLANES), :] = a
        b_s[pl.ds(n, TL, stride=SUBLANES), :] = root * (0.5 * xn) * (ti + 1.0)

    def two_steps(p, h):
        t0 = (TL - 1 - 2 * p) if reverse else 2 * p
        t1 = (t0 - 1) if reverse else (t0 + 1)
        r0 = pl.multiple_of(t0 * SUBLANES, SUBLANES)
        r1 = pl.multiple_of(t1 * SUBLANES, SUBLANES)
        a0 = a_s[pl.ds(r0, SUBLANES), :]
        b0 = b_s[pl.ds(r0, SUBLANES), :]
        a1 = a_s[pl.ds(r1, SUBLANES), :]
        b1 = b_s[pl.ds(r1, SUBLANES), :]
        h_dst[pl.ds(r0, SUBLANES), :] = a0 * h + b0
        h2 = (a1 * a0) * h + (a1 * b0 + b1)
        h_dst[pl.ds(r1, SUBLANES), :] = h2
        return h2

    hcar[...] = lax.fori_loop(0, TL // 2, two_steps, hcar[...], unroll=8)


def _rg_fwd_kernel(xm_ref, xprev_ref, xnext_ref, cw_ref, cb_ref, wa_ref, wi_ref, ba_ref, bi_ref, lam_ref,
                   hf_ref, xc_ref, xpad, a_s, b_s, hcar, *, nlat):
    rows = TL * SUBLANES
    hrows = HALO * SUBLANES
    j = pl.program_id(1)
    has_prev = j >= 2
    has_next = jnp.logical_and(j >= 1, j < nlat)
    xpad[0:hrows, :] = jnp.where(has_prev, xprev_ref[...], 0.0)
    xpad[hrows:hrows + rows, :] = xm_ref[...]
    xpad[hrows + rows:2 * hrows + rows, :] = jnp.where(has_next, xnext_ref[...], 0.0)
    acc = jnp.broadcast_to(cb_ref[...][None], (TL, SUBLANES, LANES))
    for k in range(CONV_W):
        off = (HALO + k - CONV_W // 2) * SUBLANES
        tap = xpad[off:off + rows, :].reshape(TL, SUBLANES, LANES)
        acc = acc + tap * cw_ref[k][None]
    xc_ref[...] = acc.reshape(rows, LANES)
    _rg_gates_and_scan(xc_ref, wa_ref, wi_ref, ba_ref, bi_ref, lam_ref, a_s, b_s, hf_ref, hcar, False)


def _rg_bwd_kernel(xc_ref, wa_ref, wi_ref, ba_ref, bi_ref, lam_ref, hf_ref, gg_ref, out_ref, a_s, b_s, h_s, hcar):
    _rg_gates_and_scan(xc_ref, wa_ref, wi_ref, ba_ref, bi_ref, lam_ref, a_s, b_s, h_s, hcar, True)
    h_s[...] = gg_ref[...] * (hf_ref[...] + h_s[...])
    for n in range(RG_BLOCKS):
        out_ref[:, n * LANES:(n + 1) * LANES] = h_s[pl.ds(n, TL, stride=SUBLANES), :].astype(BF16)


def _rg_scans(xin8, gg8, conv_w, conv_b, wa, wi, ba, bi, lam, s, c, b):
    assert c == TL and s % TL == 0
    rows = TL * SUBLANES
    hrows = HALO * SUBLANES
    nlat = s // TL
    t = xin8.shape[0] // SUBLANES
    n_halo = t // HALO
    d = RG_BLOCKS * LANES

    def chunk(reverse):
        return lambda bi_, j: jnp.where(j == 0, (b * s) // TL + bi_,
                                        bi_ * nlat + ((nlat - j) if reverse else (j - 1)))

    fwd, bwd = chunk(False), chunk(True)
    main_f = pl.BlockSpec((rows, LANES), lambda bi_, j: (fwd(bi_, j), 0))
    main_b = pl.BlockSpec((rows, LANES), lambda bi_, j: (bwd(bi_, j), 0))
    prev = pl.BlockSpec((hrows, LANES), lambda bi_, j: (jnp.maximum(fwd(bi_, j) * (TL // HALO) - 1, 0), 0))
    nxt = pl.BlockSpec((hrows, LANES),
                       lambda bi_, j: (jnp.minimum((fwd(bi_, j) + 1) * (TL // HALO), n_halo - 1), 0))
    full = lambda shape: pl.BlockSpec(shape, lambda bi_, j: (0,) * len(shape))
    gate_specs = [full((RG_BLOCKS, LANES, LANES)), full((RG_BLOCKS, LANES, LANES)),
                  full((1, d)), full((1, d)), full((1, d))]
    gate_args = lambda k: [(0.5 * wa[k]).astype(BF16), (0.5 * wi[k]).astype(BF16), 0.5 * ba[k].reshape(1, d),
                           0.5 * bi[k].reshape(1, d), lam[k].reshape(1, d)]
    tmajor = jax.ShapeDtypeStruct(xin8.shape, F32)
    buf = pltpu.VMEM((rows, LANES), F32)
    hf8, xc8 = pl.pallas_call(
        functools.partial(_rg_fwd_kernel, nlat=nlat),
        grid=(b, nlat + 1),
        in_specs=[main_f, prev, nxt, full((CONV_W, SUBLANES, LANES)), full((SUBLANES, LANES))] + gate_specs,
        out_specs=[main_f, main_f],
        out_shape=[tmajor, tmajor],
        scratch_shapes=[pltpu.VMEM((rows + 2 * hrows, LANES), F32), buf, buf, pltpu.VMEM((SUBLANES, LANES), F32)],
        compiler_params=_cp("parallel", "arbitrary"),
        name="rg_scan_fwd",
    )(xin8, xin8, xin8, conv_w.reshape(CONV_W, SUBLANES, LANES), conv_b.reshape(SUBLANES, LANES), *gate_args(0))
    return pl.pallas_call(
        _rg_bwd_kernel,
        grid=(b, nlat + 1),
        in_specs=[main_b] + gate_specs + [main_b, main_b],
        out_specs=pl.BlockSpec((TL, d), lambda bi_, j: (bwd(bi_, j), 0)),
        out_shape=jax.ShapeDtypeStruct((t, d), BF16),
        scratch_shapes=[buf, buf, buf, pltpu.VMEM((SUBLANES, LANES), F32)],
        compiler_params=_cp("parallel", "arbitrary"),
        name="rg_scan_bwd",
    )(xc8, *gate_args(1), hf8, gg8)


def _rglru_mixer(xa, pre, g, mod, w_in, conv_w, conv_b, wa, ba, wi, bi, lam, w_out, route, s, c, b, nrows_out):
    x, gg8, xin8 = _rg_in(xa, pre, g, mod, w_in.astype(BF16), s, b)
    y = _rg_scans(xin8, gg8, conv_w, conv_b, wa, wi, ba, bi, lam, s, c, b)
    return _out_proj(y, x, mod, w_out.astype(BF16), route, nrows_out, s, b)


def _rope_tables(s):
    pos = jnp.arange(s, dtype=F32)
    row = jnp.floor(pos / GRID_W)
    col = pos - row * GRID_W
    n_freq = HEAD_DIM // 4
    inv = ROPE_THETA ** (-jnp.arange(n_freq, dtype=F32) * 2.0 / (HEAD_DIM // 2))
    ar = row[:, None] * inv
    ac = col[:, None] * inv
    cos = jnp.concatenate([jnp.cos(ar), jnp.cos(ar), jnp.cos(ac), jnp.cos(ac)], axis=1)
    sin = jnp.concatenate([-jnp.sin(ar), jnp.sin(ar), -jnp.sin(ac), jnp.sin(ac)], axis=1)
    cos = jnp.concatenate([cos, jnp.ones((TM, HEAD_DIM), F32)], axis=0)
    sin = jnp.concatenate([sin, jnp.zeros((TM, HEAD_DIM), F32)], axis=0)
    return cos, sin


def _qkv_kernel(*refs, has_pre):
    x_ref, pre_refs, (g_ref, mod_ref, w_ref, qg_ref, kg_ref, cos_ref, sin_ref, *outs) = _split_refs(refs, has_pre)
    q_ref, k_ref, v_ref = outs[-3:]
    d = x_ref.shape[-1]
    m = mod_ref[0]
    x = _combined_rows(x_ref, pre_refs)
    if has_pre:
        outs[0][...] = x
    h = _norm_mod(x, g_ref[...], m[:, 0:d], m[:, d:2 * d])
    z = jnp.dot(h.astype(BF16), w_ref[...], preferred_element_type=F32)
    cos = cos_ref[...]
    sin = sin_ref[...]
    src = lax.broadcasted_iota(I32, (HEAD_DIM, HEAD_DIM), 0)
    dst = lax.broadcasted_iota(I32, (HEAD_DIM, HEAD_DIM), 1)
    quarter = HEAD_DIM // 4
    partner_of = jnp.where((dst % (2 * quarter)) < quarter, dst + quarter, dst - quarter)
    swap = jnp.where(src == partner_of, 1.0, 0.0).astype(BF16)

    def head(zc, gain):
        ms = jnp.mean(zc * zc, axis=-1, keepdims=True)
        y = zc * lax.rsqrt(ms + NORM_EPS) * gain
        partner = jnp.dot(y.astype(BF16), swap, preferred_element_type=F32)
        return y * cos + partner * sin

    nq = q_ref.shape[-1] // HEAD_DIM
    nk = k_ref.shape[-1] // HEAD_DIM
    for j in range(nq):
        q_ref[:, j * HEAD_DIM:(j + 1) * HEAD_DIM] = (
            head(z[:, j * HEAD_DIM:(j + 1) * HEAD_DIM], qg_ref[...]) * (HEAD_DIM ** -0.5 * LOG2E)).astype(BF16)
    for j in range(nk):
        c0 = (nq + j) * HEAD_DIM
        k_ref[:, j * HEAD_DIM:(j + 1) * HEAD_DIM] = head(z[:, c0:c0 + HEAD_DIM], kg_ref[...]).astype(BF16)
    v_ref[...] = z[:, (nq + nk) * HEAD_DIM:].astype(BF16)


def _qkv(x, pre, g, mod, w, qg, kg, cos, sin, s, b):
    t, d = x.shape
    nkv = N_KV_HEADS * HEAD_DIM
    n_pos = s // TM
    pre_specs, pre_args = _pre_io(pre, d, TM, s, b)
    row = pl.BlockSpec((TM, d), lambda i: (i, 0))
    outs = pl.pallas_call(
        functools.partial(_qkv_kernel, has_pre=pre is not None),
        grid=(t // TM,),
        in_specs=[row] + pre_specs + [
            pl.BlockSpec((1, d), lambda i: (0, 0)),
            _mod_spec(d, s // TM, b),
            pl.BlockSpec(w.shape, lambda i: (0, 0)),
            pl.BlockSpec((1, HEAD_DIM), lambda i: (0, 0)),
            pl.BlockSpec((1, HEAD_DIM), lambda i: (0, 0)),
            pl.BlockSpec((TM, HEAD_DIM), lambda i: (jnp.where(i < b * n_pos, i % n_pos, n_pos), 0)),
            pl.BlockSpec((TM, HEAD_DIM), lambda i: (jnp.where(i < b * n_pos, i % n_pos, n_pos), 0))],
        out_specs=([row] if pre else []) + [pl.BlockSpec((TM, d), lambda i: (i, 0)),
                                           pl.BlockSpec((TM, nkv), lambda i: (i, 0)),
                                           pl.BlockSpec((TM, nkv), lambda i: (i, 0))],
        out_shape=([jax.ShapeDtypeStruct((t, d), F32)] if pre else [])
        + [jax.ShapeDtypeStruct((t, d), BF16), jax.ShapeDtypeStruct((t, nkv), BF16),
           jax.ShapeDtypeStruct((t, nkv), BF16)],
        compiler_params=_cp("parallel"),
        name="qkv_proj",
    )(x, *pre_args, g, mod, w, qg, kg, cos, sin)
    return tuple(outs) if pre else (x, *outs)


def _attn_kernel(q_ref, kc_ref, vc_ref, *rest, n_lat):
    if n_lat:
        kl_ref, vl_ref, o_ref, s_scr, vaug = rest
    else:
        o_ref, s_scr, vaug = rest
    n_ctx = kc_ref.shape[0]
    tq = q_ref.shape[0]

    def fill_values():
        vaug[:, HEAD_DIM:] = jnp.ones((n_ctx + n_lat, HEAD_DIM), BF16)
        vaug[0:n_ctx, 0:HEAD_DIM] = vc_ref[...]
        if n_lat:
            vaug[n_ctx:, 0:HEAD_DIM] = vl_ref[...]

    if n_lat:
        pl.when(pl.program_id(2) == 0)(fill_values)
    else:
        fill_values()

    chunks = [(0, n_ctx)] + [(n_ctx + j, ATT_KC) for j in range(0, n_lat, ATT_KC)]
    nt = (((1,), (1,)), ((), ()))
    q_all = jnp.concatenate([q_ref[:, g * HEAD_DIM:(g + 1) * HEAD_DIM] for g in range(GQA_GROUP)], axis=0)
    m_part = jnp.full((GQA_GROUP * tq, LANES), -jnp.inf, F32)
    for off, size in chunks:
        keys = kc_ref[...] if off == 0 else kl_ref[off - n_ctx:off - n_ctx + size, :]
        sc = lax.dot_general(q_all, keys, nt, preferred_element_type=F32)
        s_scr[:, off:off + size] = sc
        for j in range(0, size, LANES):
            m_part = jnp.maximum(m_part, sc[:, j:j + LANES])
    m_row = jnp.max(m_part, axis=-1, keepdims=True)
    hr = GQA_GROUP * tq // 2
    acc = [jnp.zeros((hr, 2 * HEAD_DIM), F32), jnp.zeros((hr, 2 * HEAD_DIM), F32)]
    for off, size in chunks:
        for r in range(2):
            rows = slice(r * hr, (r + 1) * hr)
            p = jnp.exp2((s_scr[rows, off:off + size] - m_row[rows]).astype(BF16))
            acc[r] = acc[r] + jnp.dot(p, vaug[off:off + size, :], preferred_element_type=F32)
    for r in range(2):
        out = (acc[r][:, :HEAD_DIM] / acc[r][:, HEAD_DIM:]).astype(BF16)
        for j in range(GQA_GROUP // 2):
            g = r * (GQA_GROUP // 2) + j
            o_ref[:, g * HEAD_DIM:(g + 1) * HEAD_DIM] = out[j * tq:(j + 1) * tq]


def _attn_ctx_kernel(q_ref, kc_ref, vc_ref, o_all_ref, o_ref, s_scr, vaug):
    del o_all_ref
    _attn_kernel(q_ref, kc_ref, vc_ref, o_ref, s_scr, vaug, n_lat=0)


def _attention(q, k, v, s, c, b):
    t, d = q.shape
    gw = GQA_GROUP * HEAD_DIM
    tq = TQ
    nq = s // tq
    assert s % ATT_KC == 0
    ctx_blk = lambda bi, h, *_: ((b * s) // c + bi, h)
    o_lat = pl.pallas_call(
        functools.partial(_attn_kernel, n_lat=s),
        grid=(b, N_KV_HEADS, nq),
        in_specs=[pl.BlockSpec((tq, gw), lambda bi, h, i: (bi * nq + i, h)),
                  pl.BlockSpec((c, HEAD_DIM), ctx_blk),
                  pl.BlockSpec((c, HEAD_DIM), ctx_blk),
                  pl.BlockSpec((s, HEAD_DIM), lambda bi, h, i: (bi, h)),
                  pl.BlockSpec((s, HEAD_DIM), lambda bi, h, i: (bi, h))],
        out_specs=pl.BlockSpec((tq, gw), lambda bi, h, i: (bi * nq + i, h)),
        out_shape=jax.ShapeDtypeStruct((t, d), BF16),
        scratch_shapes=[pltpu.VMEM((GQA_GROUP * tq, c + s), F32), pltpu.VMEM((c + s, 2 * HEAD_DIM), BF16)],
        compiler_params=_cp("parallel", "parallel", "arbitrary"),
        name="attn_lat",
    )(q, k, v, k, v)
    return pl.pallas_call(
        _attn_ctx_kernel,
        grid=(b, N_KV_HEADS),
        in_specs=[pl.BlockSpec((c, gw), ctx_blk),
                  pl.BlockSpec((c, HEAD_DIM), ctx_blk),
                  pl.BlockSpec((c, HEAD_DIM), ctx_blk),
                  pl.BlockSpec(memory_space=pl.ANY)],
        out_specs=pl.BlockSpec((c, gw), ctx_blk),
        out_shape=jax.ShapeDtypeStruct((t, d), BF16),
        scratch_shapes=[pltpu.VMEM((GQA_GROUP * c, c), F32), pltpu.VMEM((c, 2 * HEAD_DIM), BF16)],
        input_output_aliases={3: 0},
        compiler_params=_cp("parallel", "parallel"),
        name="attn_ctx",
    )(q, k, v, o_lat)


def _attention_mixer(xa, pre, g, mod, w_qkv, qg, kg, w_o, route, s, c, b):
    cos, sin = _rope_tables(s)
    x, q, k, v = _qkv(xa, pre, g, mod, w_qkv.astype(BF16), qg.reshape(1, -1), kg.reshape(1, -1), cos, sin, s, b)
    o = _attention(q, k, v, s, c, b)
    return _out_proj(o, x, mod, w_o.astype(BF16), route, x.shape[0], s, b)


def _gmlp_kernel(*refs, has_pre):
    x_ref, pre_refs, rest = _split_refs(refs, has_pre)
    g_ref, mod_ref, w_in_ref, lng_ref, lnb_ref, ws_ref, bs_ref, w_out_ref = rest[:8]
    route_in, (o_ref, *route_out, uv_ref, cnt_s) = rest[8:12], rest[12:]
    d = x_ref.shape[-1]
    dcm = lng_ref.shape[-1]
    gw = dcm // CM_GROUPS
    x = _combined_rows(x_ref, pre_refs)
    m = mod_ref[0]
    h = _norm_mod(x, g_ref[...], m[:, 0:d], m[:, d:2 * d])
    z = jax.nn.gelu(jnp.dot(h.astype(BF16), w_in_ref[...], preferred_element_type=F32))
    u = z[:, :dcm]
    v = z[:, dcm:]
    mu = jnp.mean(v, axis=-1, keepdims=True)
    vc = v - mu
    var = jnp.mean(vc * vc, axis=-1, keepdims=True)
    vn = (vc * lax.rsqrt(var + NORM_EPS) * lng_ref[...] + lnb_ref[...]).astype(BF16)
    for ck in range(x.shape[0] // CHUNK):
        rows = slice(ck * CHUNK, (ck + 1) * CHUNK)
        for gi in range(CM_GROUPS):
            cols = slice(gi * gw, (gi + 1) * gw)
            mix = jnp.dot(ws_ref[gi], vn[rows, cols], preferred_element_type=F32) + bs_ref[:, gi:gi + 1]
            uv_ref[rows, cols] = (u[rows, cols] * mix).astype(BF16)
    y = jnp.dot(uv_ref[...], w_out_ref[...], preferred_element_type=F32)
    x1 = x + m[:, 2 * d:3 * d] * y
    o_ref[...] = x1
    _route_rows(x1, m, *route_in, *route_out, cnt_s)


def _gmlp_mixer(xa, pre, g, mod, w_in, ln_g, ln_b, w_s, b_s, w_out, route, s, b):
    t, d = xa.shape
    dcm = ln_g.shape[-1]
    full = lambda shape: pl.BlockSpec(shape, lambda i: (0,) * len(shape))
    pre_specs, pre_args = _pre_io(pre, d, TG, s, b)
    r_in, r_out, r_shape, r_scratch = _route_io(d, t, TG)
    x1, *routed = pl.pallas_call(
        functools.partial(_gmlp_kernel, has_pre=pre is not None),
        grid=(t // TG,),
        in_specs=[pl.BlockSpec((TG, d), lambda i: (i, 0))] + pre_specs + [
            full((1, d)),
            _mod_spec(d, s // TG, b),
            full((d, 2 * dcm)), full((1, dcm)), full((1, dcm)),
            full((CM_GROUPS, CHUNK, CHUNK)), full((CHUNK, CM_GROUPS)), full((dcm, d))] + r_in,
        out_specs=[pl.BlockSpec((TG, d), lambda i: (i, 0))] + r_out,
        out_shape=[jax.ShapeDtypeStruct((t, d), F32)] + r_shape,
        scratch_shapes=[pltpu.VMEM((TG, dcm), BF16)] + r_scratch,
        compiler_params=_cp("arbitrary"),
        name="gmlp",
    )(xa, *pre_args, g, mod, w_in.astype(BF16), ln_g.reshape(1, dcm), ln_b.reshape(1, dcm),
      w_s.astype(BF16), b_s.T, w_out.astype(BF16), *route)
    return x1, routed


def _route_rows(x1, m, g_ref, wrh_ref, wrl_ref, br_ref, hf_ref, rt_ref, ew_ref, cnt_ref, cnt_s):
    tm, d = x1.shape
    ng = EXPERTS_PER_GROUP

    @pl.when(pl.program_id(0) == 0)
    def _():
        cnt_s[...] = jnp.zeros_like(cnt_s)

    hf = _norm_mod(x1, g_ref[...], m[:, 3 * d:4 * d], m[:, 4 * d:5 * d])
    hf_ref[...] = _pack_bf16_pairs(hf)
    hf_hi = hf.astype(BF16)
    hf_lo = (hf - hf_hi.astype(F32)).astype(BF16)
    nt = (((1,), (1,)), ((), ()))
    logits = (lax.dot_general(wrh_ref[...], hf_hi, nt, preferred_element_type=F32)
              + lax.dot_general(wrh_ref[...], hf_lo, nt, preferred_element_type=F32)
              + lax.dot_general(wrl_ref[...], hf_hi, nt, preferred_element_type=F32)) + br_ref[...]
    neg = -jnp.inf
    row = lax.broadcasted_iota(I32, (ng, tm), 0)
    grp = logits[N_EXPERTS:N_EXPERTS + ng]
    gmax = jnp.max(grp, axis=0, keepdims=True)
    gsel = jnp.min(jnp.where(grp == gmax, row, ng), axis=0, keepdims=True)
    gate_g = 1.0 / jnp.sum(jnp.exp(grp - gmax), axis=0, keepdims=True)
    el = logits[0:ng]
    for gi in range(1, N_GROUPS):
        el = jnp.where(gsel == gi, logits[gi * ng:(gi + 1) * ng], el)
    v1 = jnp.max(el, axis=0, keepdims=True)
    i1 = jnp.min(jnp.where(el == v1, row, ng), axis=0, keepdims=True)
    el2 = jnp.where(row == i1, neg, el)
    v2 = jnp.max(el2, axis=0, keepdims=True)
    i2 = jnp.min(jnp.where(el2 == v2, row, ng), axis=0, keepdims=True)
    e21 = jnp.exp(v2 - v1)
    w1 = gate_g / (1.0 + e21)
    w2 = w1 * e21
    e1 = gsel * ng + i1
    e2 = gsel * ng + i2
    lane_row = lax.broadcasted_iota(I32, (LANES, tm), 0)
    ew_ref[...] = jnp.where(lane_row == 0, w1, jnp.where(lane_row == 1, w2, 0.0)).T

    expert = lax.broadcasted_iota(I32, (N_EXPERTS, tm), 0)
    oh1 = expert == e1
    oh2 = expert == e2
    earlier = (lax.broadcasted_iota(I32, (tm, tm), 0) < lax.broadcasted_iota(I32, (tm, tm), 1)).astype(BF16)
    pre1 = jnp.dot(jnp.where(oh1, 1.0, 0.0).astype(BF16), earlier, preferred_element_type=F32)
    pre2 = jnp.dot(jnp.where(oh2, 1.0, 0.0).astype(BF16), earlier, preferred_element_type=F32)
    tot1 = jnp.sum(jnp.where(oh1, 1.0, 0.0), axis=1, keepdims=True)
    tot2 = jnp.sum(jnp.where(oh2, 1.0, 0.0), axis=1, keepdims=True)
    cnt = cnt_s[:, 0:1]
    rank1 = jnp.sum(jnp.where(oh1, cnt + pre1, 0.0), axis=0, keepdims=True).astype(I32)
    rank2 = jnp.sum(jnp.where(oh2, cnt + tot1 + pre2, 0.0), axis=0, keepdims=True).astype(I32)
    cnt = jnp.broadcast_to(cnt + tot1 + tot2, cnt_s.shape)
    cnt_s[...] = cnt
    cnt_ref[...] = cnt.astype(I32)
    rt_ref[...] = jnp.where(row == 0, e1, jnp.where(row == 1, e2, jnp.where(row == 2, rank1,
                            jnp.where(row == 3, rank2, 0))))


def _route_params(g_ffn, w_group, b_group, w_router, b_router):
    d = w_group.shape[0]
    pad = LANES - N_EXPERTS - N_GROUPS
    wr = jnp.concatenate([w_router.reshape(d, N_EXPERTS), w_group, jnp.zeros((d, pad), F32)], axis=1).T
    br = jnp.concatenate([b_router.reshape(N_EXPERTS), b_group,
                          jnp.full((EXPERTS_PER_GROUP - N_GROUPS,), -jnp.inf, F32),
                          jnp.zeros((pad - EXPERTS_PER_GROUP + N_GROUPS,), F32)]).reshape(LANES, 1)
    wr_hi = wr.astype(BF16)
    wr_lo = (wr - wr_hi.astype(F32)).astype(BF16)
    return g_ffn.reshape(1, d), wr_hi, wr_lo, br


def _route_io(d, nrows, tr):
    const = lambda shape: pl.BlockSpec(shape, lambda i: (0, 0))
    row = lambda w: pl.BlockSpec((tr, w), lambda i: (i, 0))
    in_specs = [const((1, d)), const((LANES, d)), const((LANES, d)), const((LANES, 1))]
    out_specs = [row(d // 2), pl.BlockSpec((SUBLANES, tr), lambda i: (0, i)), row(LANES), const((N_EXPERTS, LANES))]
    out_shape = [jax.ShapeDtypeStruct((nrows, d // 2), I32), jax.ShapeDtypeStruct((SUBLANES, nrows), I32),
                 jax.ShapeDtypeStruct((nrows, LANES), F32), jax.ShapeDtypeStruct((N_EXPERTS, LANES), I32)]
    return in_specs, out_specs, out_shape, [pltpu.VMEM((N_EXPERTS, LANES), F32)]


def _plan_kernel(cnt_ref, rt_ref, pos_ref, blk_e_ref, n_used_ref, first_ref, slot_ref, next_ref,
                 start_s, end_s, nxt_s, *, bm):
    n_blk = blk_e_ref.shape[0]
    acc = jnp.int32(0)
    for e in range(N_EXPERTS):
        start_s[e] = acc
        acc = acc + (cnt_ref[e, 0] + (bm - 1)) // bm * bm
        end_s[e] = acc
    n_used = acc // bm
    n_used_ref[0] = n_used
    nxt = jnp.int32(-1)
    for e in reversed(range(N_EXPERTS)):
        nxt_s[e] = nxt
        nxt = jnp.where(cnt_ref[e, 0] > 0, e, nxt)

    def block(i, carry):
        prev_e, runs = carry
        row = jnp.minimum(i, n_used - 1) * bm
        e = lax.while_loop(lambda v: jnp.logical_and(v < N_EXPERTS - 1, end_s[v] <= row), lambda v: v + 1,
                           jnp.maximum(prev_e, 0))
        first = jnp.logical_and(i < n_used, prev_e != e)
        runs = runs + first.astype(I32)
        blk_e_ref[i] = e
        first_ref[i] = first.astype(I32)
        slot_ref[i] = (runs - 1) % 2
        next_ref[i] = nxt_s[e]
        return e, runs

    lax.fori_loop(0, n_blk, block, (jnp.int32(-1), jnp.int32(0)))

    rt = rt_ref[...]
    start_of = jnp.zeros_like(rt)
    for e in range(N_EXPERTS):
        start_of = jnp.where(rt == e, start_s[e], start_of)
    pos_ref[...] = start_of + pltpu.roll(rt, SUBLANES - 2, 0)


def _dispatch_plan(rt, cnt, bm):
    n_tok = rt.shape[1]
    n_rows = 2 * n_tok + N_EXPERTS * bm
    n_blk = n_rows // bm
    smem = pl.BlockSpec(memory_space=pltpu.SMEM)
    vec = lambda n: jax.ShapeDtypeStruct((n,), I32)
    pos, blk_e, n_used, first, slot, nxt = pl.pallas_call(
        functools.partial(_plan_kernel, bm=bm),
        in_specs=[smem, pl.BlockSpec(memory_space=pltpu.VMEM)],
        out_specs=[pl.BlockSpec(memory_space=pltpu.VMEM), smem, smem, smem, smem, smem],
        out_shape=[jax.ShapeDtypeStruct(rt.shape, I32), vec(n_blk), vec(1), vec(n_blk), vec(n_blk), vec(n_blk)],
        scratch_shapes=[pltpu.SMEM((N_EXPERTS,), I32)] * 3,
        name="moe_plan",
    )(cnt, rt)
    return pos, (blk_e, n_used, first, slot, nxt), n_rows


def _sc_mesh():
    return plsc.VectorSubcoreMesh(core_axis_name="c", subcore_axis_name="s")


def _sc_worker_base(per_worker):
    return (lax.axis_index("s") * SC_CORES + lax.axis_index("c")) * per_worker


def _sc_dispatch(hf, pos, n_rows):
    t, d = hf.shape
    per_w = t // SC_WORKERS
    ch = SC_CHUNK
    n_ck = per_w // ch
    assert per_w * SC_WORKERS == t and n_ck * ch == per_w

    @functools.partial(
        pl.kernel, mesh=_sc_mesh(), out_type=jax.ShapeDtypeStruct((n_rows, d), hf.dtype),
        scratch_types=[pltpu.VMEM((per_w,), I32), pltpu.VMEM((per_w,), I32), pltpu.VMEM((2, ch, d), hf.dtype),
                       pltpu.SemaphoreType.DMA((2,)), pltpu.SemaphoreType.DMA((2,)), pltpu.SemaphoreType.DMA((2,))])
    def dispatch(hf_hbm, p_hbm, out_hbm, i0_v, i1_v, rows_v, sem_in, sem_s0, sem_s1):
        base = pl.multiple_of(_sc_worker_base(per_w), SUBLANES)
        pltpu.sync_copy(p_hbm.at[pl.ds(base, per_w)], i0_v)
        pltpu.sync_copy(p_hbm.at[pl.ds(pl.multiple_of(t + base, SUBLANES), per_w)], i1_v)

        def load(ck):
            return pltpu.make_async_copy(hf_hbm.at[pl.ds(base + ck * ch, ch)], rows_v.at[ck % 2], sem_in.at[ck % 2])

        def scatters(ck):
            src = rows_v.at[ck % 2]
            return (pltpu.make_async_copy(src, out_hbm.at[i0_v.at[pl.ds(ck * ch, ch)]], sem_s0.at[ck % 2]),
                    pltpu.make_async_copy(src, out_hbm.at[i1_v.at[pl.ds(ck * ch, ch)]], sem_s1.at[ck % 2]))

        load(0).start()
        for ck in range(n_ck):
            load(ck).wait()
            if ck + 1 < n_ck:
                if ck >= 1:
                    for cp in scatters(ck - 1):
                        cp.wait()
                load(ck + 1).start()
            for cp in scatters(ck):
                cp.start()
        for ck in range(max(n_ck - 2, 0), n_ck):
            for cp in scatters(ck):
                cp.wait()

    return dispatch(hf, pos)


def _sc_gather(rows, idx):
    n = idx.shape[0]
    d = rows.shape[1]
    per_w = n // SC_WORKERS
    ch = 2 * SC_CHUNK
    n_ck = per_w // ch
    assert per_w * SC_WORKERS == n and n_ck * ch == per_w

    @functools.partial(
        pl.kernel, mesh=_sc_mesh(), out_type=jax.ShapeDtypeStruct((n, d), rows.dtype),
        scratch_types=[pltpu.VMEM((per_w,), I32), pltpu.VMEM((2, ch, d), rows.dtype),
                       pltpu.SemaphoreType.DMA((2,)), pltpu.SemaphoreType.DMA((2,))])
    def gather(rows_hbm, i_hbm, out_hbm, i_v, buf, sem_g, sem_w):
        base = pl.multiple_of(_sc_worker_base(per_w), SUBLANES)
        pltpu.sync_copy(i_hbm.at[pl.ds(base, per_w)], i_v)

        def fetch(ck):
            return pltpu.make_async_copy(rows_hbm.at[i_v.at[pl.ds(ck * ch, ch)]], buf.at[ck % 2], sem_g.at[ck % 2])

        def write(ck):
            return pltpu.make_async_copy(buf.at[ck % 2], out_hbm.at[pl.ds(base + ck * ch, ch)], sem_w.at[ck % 2])

        fetch(0).start()
        for ck in range(n_ck):
            fetch(ck).wait()
            if ck + 1 < n_ck:
                if ck >= 1:
                    write(ck - 1).wait()
                fetch(ck + 1).start()
            write(ck).start()
        for ck in range(max(n_ck - 2, 0), n_ck):
            write(ck).wait()

    return gather(rows, idx)


def _expert_kernel(blk_e_ref, n_used_ref, first_ref, slot_ref, next_ref, x_ref, wg_hbm, wu_hbm, wd_hbm, y_ref,
                   wgf, wuf, wdf, wgb, wub, wdb, sem, *, e_base):
    n_used = n_used_ref[0]
    bm = x_ref.shape[0] // MOE_SUB

    def weight_copies(e, slot):
        return (pltpu.make_async_copy(wg_hbm.at[e_base + e], wgf.at[slot], sem.at[slot, 0]),
                pltpu.make_async_copy(wu_hbm.at[e_base + e], wuf.at[slot], sem.at[slot, 1]),
                pltpu.make_async_copy(wd_hbm.at[e_base + e], wdf.at[slot], sem.at[slot, 2]))

    @pl.when(pl.program_id(0) == 0)
    def _():
        for cp in weight_copies(blk_e_ref[0], 0):
            cp.start()

    for j in range(MOE_SUB):
        blk = pl.program_id(0) * MOE_SUB + j
        rows = slice(j * bm, (j + 1) * bm)

        @pl.when(jnp.logical_and(blk < n_used, first_ref[blk] == 1))
        def _():
            slot = slot_ref[blk]
            for cp in weight_copies(blk_e_ref[blk], slot):
                cp.wait()
            nxt = next_ref[blk]

            @pl.when(nxt >= 0)
            def _():
                for cp in weight_copies(nxt, 1 - slot):
                    cp.start()

            wgb[...] = wgf[slot].astype(BF16)
            wub[...] = wuf[slot].astype(BF16)
            wdb[...] = wdf[slot].astype(BF16)

        @pl.when(blk < n_used)
        def _():
            x_hi, x_lo = _unpack_bf16_pairs(x_ref[rows, :])
            xb = jnp.concatenate([x_hi.astype(BF16), x_lo.astype(BF16)], axis=1)
            gt = jnp.dot(xb, wgb[...], preferred_element_type=F32)
            up = jnp.dot(xb, wub[...], preferred_element_type=F32)
            act = (gt * _sigmoid(gt) * up).astype(BF16)
            y_ref[rows, :] = _pack_bf16_pairs(jnp.dot(act, wdb[...], preferred_element_type=F32))


def _experts(x_rows, plan, w_gate, w_up, w_down, layer):
    n_rows, dp = x_rows.shape
    depth, n_e, d, de = w_gate.shape
    step_rows = MOE_SUB * MOE_BM
    assert n_rows % step_rows == 0
    any_spec = pl.BlockSpec(memory_space=pl.ANY)
    last_used = lambda i, be, nu, *_: (jnp.minimum(i, (nu[0] - 1) // MOE_SUB), 0)
    grid_spec = pltpu.PrefetchScalarGridSpec(
        num_scalar_prefetch=5,
        grid=(n_rows // step_rows,),
        in_specs=[pl.BlockSpec((step_rows, dp), last_used), any_spec, any_spec, any_spec],
        out_specs=pl.BlockSpec((step_rows, dp), last_used),
        scratch_shapes=[pltpu.VMEM((2, d, de), F32), pltpu.VMEM((2, d, de), F32), pltpu.VMEM((2, de, d), F32),
                        pltpu.VMEM((d, de), BF16), pltpu.VMEM((d, de), BF16), pltpu.VMEM((de, d), BF16),
                        pltpu.SemaphoreType.DMA((2, 3))],
    )
    return pl.pallas_call(
        functools.partial(_expert_kernel, e_base=layer * n_e),
        grid_spec=grid_spec,
        out_shape=jax.ShapeDtypeStruct((n_rows, dp), I32),
        compiler_params=_cp("arbitrary"),
        name="moe_experts",
    )(*plan, x_rows, w_gate.reshape(depth * n_e, d, de), w_up.reshape(depth * n_e, d, de),
      w_down.reshape(depth * n_e, de, d))


def _combine_kernel(x_ref, *rest):
    *pre_refs, o_ref = rest
    o_ref[...] = _combined_rows(x_ref, pre_refs)


def _combine(x, pre, s, b):
    nrows, d = x.shape
    pre_specs, pre_args = _pre_io(pre, d, TR, s, b)
    return pl.pallas_call(
        _combine_kernel,
        grid=(nrows // TR,),
        in_specs=[pl.BlockSpec((TR, d), lambda i: (i, 0))] + pre_specs,
        out_specs=pl.BlockSpec((TR, d), lambda i: (i, 0)),
        out_shape=jax.ShapeDtypeStruct((nrows, d), F32),
        compiler_params=_cp("parallel"),
        name="moe_combine",
    )(x, *pre_args)


def _expert_outputs(routed, w_gate, w_up, w_down, layer):
    hf, rt, _, cnt = routed
    pos, plan, n_rows = _dispatch_plan(rt, cnt, MOE_BM)
    pos = pos[0:2].reshape(-1)
    x_rows = _sc_dispatch(hf, pos, n_rows)
    y_rows = _experts(x_rows, plan, w_gate, w_up, w_down, layer)
    return _sc_gather(y_rows, pos)


def kernel(x, c, ctx, c_ctx, ada_w, ada_b, norm_mix_g, norm_ffn_g, rg_w_in, rg_conv_w, rg_conv_b, rg_wa, rg_ba, rg_wi, rg_bi, rg_lambda, rg_w_out, at_w_qkv, at_q_g, at_k_g, at_w_o, cm_w_in, cm_ln_g, cm_ln_b, cm_w_s, cm_b_s, cm_w_out, moe_w_group, moe_b_group, moe_w_router, moe_b_router, moe_w_gate, moe_w_up, moe_w_down):
    b, s, d = x.shape
    cl = ctx.shape[1]
    depth = ada_w.shape[0]
    n_lat = b * s
    assert b < SUBLANES and s % max(TI, TR, TG) == 0 and (b * cl) % max(TI, TR, TG) == 0 and cl % TM == 0
    assert d == RG_BLOCKS * LANES

    cin = jnp.concatenate([c, c_ctx[None, :], jnp.zeros((SUBLANES - b - 1, d), F32)], axis=0)
    mod_all = _ada_table(cin, ada_w, ada_b).reshape(depth, SUBLANES, 1, N_MOD * d)
    tok = jnp.concatenate([x.reshape(n_lat, d), ctx.reshape(b * cl, d)], axis=0)

    xa, pre = tok, None
    for l in range(depth):
        kind = l % 3
        j = l // 3
        last = l == depth - 1
        mod = mod_all[l]
        g_mix = norm_mix_g[l].reshape(1, d)
        route = _route_params(norm_ffn_g[l], moe_w_group[l], moe_b_group[l], moe_w_router[l], moe_b_router[l])
        if kind == 0:
            x1, routed = _rglru_mixer(xa, pre, g_mix, mod, rg_w_in[j], rg_conv_w[j], rg_conv_b[j], rg_wa[j],
                                      rg_ba[j], rg_wi[j], rg_bi[j], rg_lambda[j], rg_w_out[j], route, s, cl, b,
                                      n_lat if last else xa.shape[0])
        elif kind == 1:
            x1, routed = _attention_mixer(xa, pre, g_mix, mod, at_w_qkv[j], at_q_g[j], at_k_g[j], at_w_o[j],
                                          route, s, cl, b)
        else:
            x1, routed = _gmlp_mixer(xa, pre, g_mix, mod, cm_w_in[j], cm_ln_g[j], cm_ln_b[j], cm_w_s[j],
                                     cm_b_s[j], cm_w_out[j], route, s, b)
        y01 = _expert_outputs(routed, moe_w_gate, moe_w_up, moe_w_down, l)
        xa, pre = x1, (y01, routed[2], mod)
    return _combine(xa, pre, s, b)[:n_lat].reshape(b, s, d)
```

```python
import functools

import jax
import jax.numpy as jnp
from jax import lax
from jax.experimental import pallas as pl
from jax.experimental.pallas import tpu as pltpu
from jax.experimental.pallas import tpu_sc as plsc

F32 = jnp.float32
BF16 = jnp.bfloat16
I32 = jnp.int32
U32 = jnp.uint32

NORM_EPS = 1e-6
N_MOD = 6
GRID_W = 64
RG_BLOCKS = 8
CONV_W = 4
RG_C = 8.0
HEAD_DIM = 128
N_KV_HEADS = 2
GQA_GROUP = 4
ROPE_THETA = 10000.0
CHUNK = 128
CM_GROUPS = 8
N_GROUPS = 4
EXPERTS_PER_GROUP = 8
N_EXPERTS = N_GROUPS * EXPERTS_PER_GROUP

LANES = 128
SUBLANES = 8
TM = 256
TQ = 512
TI = 1024
TR = 1024
TG = 512
TL = 256
HALO = 8
ATT_KC = 512
LOG2E = 1.4426950408889634
MOE_BM = 256
MOE_SUB = 4
SC_CORES = 2
SC_WORKERS = 32
SC_CHUNK = 32
VMEM_LIMIT = 52 * 2**20


def _cp(*sem):
    return pltpu.CompilerParams(dimension_semantics=sem, vmem_limit_bytes=VMEM_LIMIT)


def _norm_mod(x, g, shift, scale):
    ms = jnp.mean(x * x, axis=-1, keepdims=True)
    y = x * lax.rsqrt(ms + NORM_EPS) * g
    return y * (1.0 + scale) + shift


def _sigmoid(x):
    return 0.5 * jnp.tanh(0.5 * x) + 0.5


def _pack_bf16_pairs(x):
    h = x.shape[-1] // 2
    hi = lax.bitcast_convert_type(x[:, :h].astype(BF16).astype(F32), U32)
    lo = lax.bitcast_convert_type(x[:, h:].astype(BF16).astype(F32), U32)
    return lax.bitcast_convert_type(hi | (lo >> 16), I32)


def _unpack_bf16_pairs(w):
    u = lax.bitcast_convert_type(w, U32)
    hi = lax.bitcast_convert_type(u & jnp.uint32(0xFFFF0000), F32)
    lo = lax.bitcast_convert_type(u << 16, F32)
    return hi, lo


def _combined_rows(x_ref, pre_refs):
    if not pre_refs:
        return x_ref[...]
    y0_ref, y1_ref, ew_ref, modp_ref = pre_refs
    d = x_ref.shape[1]
    ew = ew_ref[...]
    y0_hi, y0_lo = _unpack_bf16_pairs(y0_ref[...])
    y1_hi, y1_lo = _unpack_bf16_pairs(y1_ref[...])
    y = jnp.concatenate([ew[:, 0:1] * y0_hi + ew[:, 1:2] * y1_hi, ew[:, 0:1] * y0_lo + ew[:, 1:2] * y1_lo], axis=1)
    return x_ref[...] + modp_ref[0][:, 5 * d:6 * d] * y


def _pre_io(pre, d, tr, s, b):
    if pre is None:
        return [], []
    y01, ew, mod_prev = pre
    nb = y01.shape[0] // 2 // tr
    specs = [pl.BlockSpec((tr, d // 2), lambda i: (i, 0)), pl.BlockSpec((tr, d // 2), lambda i: (i + nb, 0)),
             pl.BlockSpec((tr, LANES), lambda i: (i, 0)), _mod_spec(d, s // tr, b)]
    return specs, [y01, y01, ew, mod_prev]


def _split_refs(refs, has_pre):
    return (refs[0], refs[1:5], refs[5:]) if has_pre else (refs[0], (), refs[1:])


def _mod_spec(d, rows_per_sample, n_samples):
    return pl.BlockSpec((1, 1, N_MOD * d),
                        lambda i, *_: (jnp.minimum(i // rows_per_sample, n_samples), 0, 0))


def _ada_kernel(c_ref, w_ref, b_ref, o_ref):
    cin = c_ref[...]
    act = cin * jax.nn.sigmoid(cin)
    w = w_ref[0]
    w_hi = w.astype(BF16)
    w_lo = (w - w_hi.astype(F32)).astype(BF16)
    a_hi = act.astype(BF16)
    a_lo = (act - a_hi.astype(F32)).astype(BF16)
    o_ref[0] = (jnp.dot(a_hi, w_hi, preferred_element_type=F32) + jnp.dot(a_lo, w_hi, preferred_element_type=F32)
                + jnp.dot(a_hi, w_lo, preferred_element_type=F32)) + b_ref[0]


def _ada_table(cin, ada_w, ada_b):
    depth, d, n = ada_w.shape
    tn = 2 * d
    return pl.pallas_call(
        _ada_kernel,
        grid=(depth, n // tn),
        in_specs=[pl.BlockSpec((SUBLANES, d), lambda l, j: (0, 0)),
                  pl.BlockSpec((1, d, tn), lambda l, j: (l, 0, j)),
                  pl.BlockSpec((1, 1, tn), lambda l, j: (l, 0, j))],
        out_specs=pl.BlockSpec((1, SUBLANES, tn), lambda l, j: (l, 0, j)),
        out_shape=jax.ShapeDtypeStruct((depth, SUBLANES, n), F32),
        compiler_params=_cp("parallel", "parallel"),
        name="ada_table",
    )(cin, ada_w, ada_b.reshape(depth, 1, n))


def _out_kernel(y_ref, x_ref, mod_ref, w_ref, *rest):
    route_in, (o_ref, *route_out) = rest[:3], rest[3:]
    d = x_ref.shape[-1]
    m = mod_ref[0]
    y = jnp.dot(y_ref[...].astype(BF16), w_ref[...], preferred_element_type=F32)
    x1 = x_ref[...] + m[:, 2 * d:3 * d] * y
    o_ref[...] = x1
    _route_rows(x1, m, *route_in, *route_out)


def _out_proj(y, x, mod, w, route, nrows, s, b):
    d = x.shape[-1]
    k = y.shape[-1]
    r_in, r_out, r_shape, r_scratch = _route_io(d, nrows, TR)
    x1, *routed = pl.pallas_call(
        _out_kernel,
        grid=(nrows // TR,),
        in_specs=[pl.BlockSpec((TR, k), lambda i: (i, 0)),
                  pl.BlockSpec((TR, d), lambda i: (i, 0)),
                  _mod_spec(d, s // TR, b),
                  pl.BlockSpec((k, d), lambda i: (0, 0))] + r_in,
        out_specs=[pl.BlockSpec((TR, d), lambda i: (i, 0))] + r_out,
        out_shape=[jax.ShapeDtypeStruct((nrows, d), F32)] + r_shape,
        scratch_shapes=r_scratch,
        compiler_params=_cp("arbitrary"),
        name="out_proj",
    )(y, x, mod, w, *route)
    return x1, routed


def _rg_in_kernel(*refs, has_pre):
    x_ref, pre_refs, (g_ref, mod_ref, w_ref, *outs) = _split_refs(refs, has_pre)
    gg_ref, xin_ref = outs[-2:]
    d = x_ref.shape[-1]
    m = mod_ref[0]
    x = _combined_rows(x_ref, pre_refs)
    if has_pre:
        outs[0][...] = x
    h = _norm_mod(x, g_ref[...], m[:, 0:d], m[:, d:2 * d])
    z = jnp.dot(h.astype(BF16), w_ref[...], preferred_element_type=F32)
    tm = x_ref.shape[0]
    for n in range(d // LANES):
        cols = slice(n * LANES, (n + 1) * LANES)
        gg_ref[pl.ds(n, tm, stride=SUBLANES), :] = jax.nn.gelu(z[:, cols])
        xin_ref[pl.ds(n, tm, stride=SUBLANES), :] = z[:, d + n * LANES:d + (n + 1) * LANES]


def _rg_in(x, pre, g, mod, w, s, b):
    t, d = x.shape
    assert d == SUBLANES * LANES
    pre_specs, pre_args = _pre_io(pre, d, TI, s, b)
    row = pl.BlockSpec((TI, d), lambda i: (i, 0))
    tmajor = pl.BlockSpec((TI * SUBLANES, LANES), lambda i: (i, 0))
    outs = pl.pallas_call(
        functools.partial(_rg_in_kernel, has_pre=pre is not None),
        grid=(t // TI,),
        in_specs=[row] + pre_specs + [pl.BlockSpec((1, d), lambda i: (0, 0)), _mod_spec(d, s // TI, b),
                                      pl.BlockSpec((d, 2 * d), lambda i: (0, 0))],
        out_specs=([row] if pre else []) + [tmajor, tmajor],
        out_shape=([jax.ShapeDtypeStruct((t, d), F32)] if pre else [])
        + [jax.ShapeDtypeStruct((t * SUBLANES, LANES), F32)] * 2,
        compiler_params=_cp("parallel"),
        name="rg_in",
    )(x, *pre_args, g, mod, w)
    return (outs[0], outs[1], outs[2]) if pre else (x, outs[0], outs[1])


def _rg_gates_and_scan(xc, wa_ref, wi_ref, ba_ref, bi_ref, lam_ref, a_s, b_s, h_dst, hcar, reverse):
    @pl.when(pl.program_id(1) == 0)
    def _():
        hcar[...] = jnp.zeros_like(hcar)

    for n in range(RG_BLOCKS):
        cols = slice(n * LANES, (n + 1) * LANES)
        xn = xc[pl.ds(n, TL, stride=SUBLANES), :]
        xb = xn.astype(BF16)
        ta = jnp.tanh(jnp.dot(xb, wa_ref[n], preferred_element_type=F32) + ba_ref[:, cols])
        ti = jnp.tanh(jnp.dot(xb, wi_ref[n], preferred_element_type=F32) + bi_ref[:, cols])
        k = (-0.5 * RG_C * LOG2E) * jax.nn.softplus(-lam_ref[:, cols])
        a = jnp.exp2(k * ta + k)
        om = 1.0 - a * a
        root = jnp.where(om > 0.0, om * lax.rsqrt(om), 0.0)
        a_s[pl.ds(n, TL, stride=SUBLANES), :] = a
        b_s[pl.ds(n, TL, stride=SUBLANES), :] = root * (0.5 * xn) * (ti + 1.0)

    def two_steps(p, h):
        t0 = (TL - 1 - 2 * p) if reverse else 2 * p
        t1 = (t0 - 1) if reverse else (t0 + 1)
        r0 = pl.multiple_of(t0 * SUBLANES, SUBLANES)
        r1 = pl.multiple_of(t1 * SUBLANES, SUBLANES)
        a0 = a_s[pl.ds(r0, SUBLANES), :]
        b0 = b_s[pl.ds(r0, SUBLANES), :]
        a1 = a_s[pl.ds(r1, SUBLANES), :]
        b1 = b_s[pl.ds(r1, SUBLANES), :]
        h_dst[pl.ds(r0, SUBLANES), :] = a0 * h + b0
        h2 = (a1 * a0) * h + (a1 * b0 + b1)
        h_dst[pl.ds(r1, SUBLANES), :] = h2
        return h2

    hcar[...] = lax.fori_loop(0, TL // 2, two_steps, hcar[...], unroll=8)


def _rg_fwd_kernel(xm_ref, xprev_ref, xnext_ref, cw_ref, cb_ref, wa_ref, wi_ref, ba_ref, bi_ref, lam_ref,
                   hf_ref, xc_ref, xpad, a_s, b_s, hcar, *, nlat):
    rows = TL * SUBLANES
    hrows = HALO * SUBLANES
    j = pl.program_id(1)
    has_prev = j >= 2
    has_next = jnp.logical_and(j >= 1, j < nlat)
    xpad[0:hrows, :] = jnp.where(has_prev, xprev_ref[...], 0.0)
    xpad[hrows:hrows + rows, :] = xm_ref[...]
    xpad[hrows + rows:2 * hrows + rows, :] = jnp.where(has_next, xnext_ref[...], 0.0)
    acc = jnp.broadcast_to(cb_ref[...][None], (TL, SUBLANES, LANES))
    for k in range(CONV_W):
        off = (HALO + k - CONV_W // 2) * SUBLANES
        tap = xpad[off:off + rows, :].reshape(TL, SUBLANES, LANES)
        acc = acc + tap * cw_ref[k][None]
    xc_ref[...] = acc.reshape(rows, LANES)
    _rg_gates_and_scan(xc_ref, wa_ref, wi_ref, ba_ref, bi_ref, lam_ref, a_s, b_s, hf_ref, hcar, False)


def _rg_bwd_kernel(xc_ref, wa_ref, wi_ref, ba_ref, bi_ref, lam_ref, hf_ref, gg_ref, out_ref, a_s, b_s, h_s, hcar):
    _rg_gates_and_scan(xc_ref, wa_ref, wi_ref, ba_ref, bi_ref, lam_ref, a_s, b_s, h_s, hcar, True)
    h_s[...] = gg_ref[...] * (hf_ref[...] + h_s[...])
    for n in range(RG_BLOCKS):
        out_ref[:, n * LANES:(n + 1) * LANES] = h_s[pl.ds(n, TL, stride=SUBLANES), :].astype(BF16)


def _rg_scans(xin8, gg8, conv_w, conv_b, wa, wi, ba, bi, lam, s, c, b):
    assert c == TL and s % TL == 0
    rows = TL * SUBLANES
    hrows = HALO * SUBLANES
    nlat = s // TL
    t = xin8.shape[0] // SUBLANES
    n_halo = t // HALO
    d = RG_BLOCKS * LANES

    def chunk(reverse):
        return lambda bi_, j: jnp.where(j == 0, (b * s) // TL + bi_,
                                        bi_ * nlat + ((nlat - j) if reverse else (j - 1)))

    fwd, bwd = chunk(False), chunk(True)
    main_f = pl.BlockSpec((rows, LANES), lambda bi_, j: (fwd(bi_, j), 0))
    main_b = pl.BlockSpec((rows, LANES), lambda bi_, j: (bwd(bi_, j), 0))
    prev = pl.BlockSpec((hrows, LANES), lambda bi_, j: (jnp.maximum(fwd(bi_, j) * (TL // HALO) - 1, 0), 0))
    nxt = pl.BlockSpec((hrows, LANES),
                       lambda bi_, j: (jnp.minimum((fwd(bi_, j) + 1) * (TL // HALO), n_halo - 1), 0))
    full = lambda shape: pl.BlockSpec(shape, lambda bi_, j: (0,) * len(shape))
    gate_specs = [full((RG_BLOCKS, LANES, LANES)), full((RG_BLOCKS, LANES, LANES)),
                  full((1, d)), full((1, d)), full((1, d))]
    gate_args = lambda k: [(0.5 * wa[k]).astype(BF16), (0.5 * wi[k]).astype(BF16), 0.5 * ba[k].reshape(1, d),
                           0.5 * bi[k].reshape(1, d), lam[k].reshape(1, d)]
    tmajor = jax.ShapeDtypeStruct(xin8.shape, F32)
    buf = pltpu.VMEM((rows, LANES), F32)
    hf8, xc8 = pl.pallas_call(
        functools.partial(_rg_fwd_kernel, nlat=nlat),
        grid=(b, nlat + 1),
        in_specs=[main_f, prev, nxt, full((CONV_W, SUBLANES, LANES)), full((SUBLANES, LANES))] + gate_specs,
        out_specs=[main_f, main_f],
        out_shape=[tmajor, tmajor],
        scratch_shapes=[pltpu.VMEM((rows + 2 * hrows, LANES), F32), buf, buf, pltpu.VMEM((SUBLANES, LANES), F32)],
        compiler_params=_cp("parallel", "arbitrary"),
        name="rg_scan_fwd",
    )(xin8, xin8, xin8, conv_w.reshape(CONV_W, SUBLANES, LANES), conv_b.reshape(SUBLANES, LANES), *gate_args(0))
    return pl.pallas_call(
        _rg_bwd_kernel,
        grid=(b, nlat + 1),
        in_specs=[main_b] + gate_specs + [main_b, main_b],
        out_specs=pl.BlockSpec((TL, d), lambda bi_, j: (bwd(bi_, j), 0)),
        out_shape=jax.ShapeDtypeStruct((t, d), BF16),
        scratch_shapes=[buf, buf, buf, pltpu.VMEM((SUBLANES, LANES), F32)],
        compiler_params=_cp("parallel", "arbitrary"),
        name="rg_scan_bwd",
    )(xc8, *gate_args(1), hf8, gg8)


def _rglru_mixer(xa, pre, g, mod, w_in, conv_w, conv_b, wa, ba, wi, bi, lam, w_out, route, s, c, b, nrows_out):
    x, gg8, xin8 = _rg_in(xa, pre, g, mod, w_in.astype(BF16), s, b)
    y = _rg_scans(xin8, gg8, conv_w, conv_b, wa, wi, ba, bi, lam, s, c, b)
    return _out_proj(y, x, mod, w_out.astype(BF16), route, nrows_out, s, b)


def _rope_tables(s):
    pos = jnp.arange(s, dtype=F32)
    row = jnp.floor(pos / GRID_W)
    col = pos - row * GRID_W
    n_freq = HEAD_DIM // 4
    inv = ROPE_THETA ** (-jnp.arange(n_freq, dtype=F32) * 2.0 / (HEAD_DIM // 2))
    ar = row[:, None] * inv
    ac = col[:, None] * inv
    cos = jnp.concatenate([jnp.cos(ar), jnp.cos(ar), jnp.cos(ac), jnp.cos(ac)], axis=1)
    sin = jnp.concatenate([-jnp.sin(ar), jnp.sin(ar), -jnp.sin(ac), jnp.sin(ac)], axis=1)
    cos = jnp.concatenate([cos, jnp.ones((TM, HEAD_DIM), F32)], axis=0)
    sin = jnp.concatenate([sin, jnp.zeros((TM, HEAD_DIM), F32)], axis=0)
    return cos, sin


def _qkv_kernel(*refs, has_pre):
    x_ref, pre_refs, (g_ref, mod_ref, w_ref, qg_ref, kg_ref, cos_ref, sin_ref, *outs) = _split_refs(refs, has_pre)
    q_ref, k_ref, v_ref = outs[-3:]
    d = x_ref.shape[-1]
    m = mod_ref[0]
    x = _combined_rows(x_ref, pre_refs)
    if has_pre:
        outs[0][...] = x
    h = _norm_mod(x, g_ref[...], m[:, 0:d], m[:, d:2 * d])
    z = jnp.dot(h.astype(BF16), w_ref[...], preferred_element_type=F32)
    cos = cos_ref[...]
    sin = sin_ref[...]
    src = lax.broadcasted_iota(I32, (HEAD_DIM, HEAD_DIM), 0)
    dst = lax.broadcasted_iota(I32, (HEAD_DIM, HEAD_DIM), 1)
    quarter = HEAD_DIM // 4
    partner_of = jnp.where((dst % (2 * quarter)) < quarter, dst + quarter, dst - quarter)
    swap = jnp.where(src == partner_of, 1.0, 0.0).astype(BF16)

    def head(zc, gain):
        ms = jnp.mean(zc * zc, axis=-1, keepdims=True)
        y = zc * lax.rsqrt(ms + NORM_EPS) * gain
        partner = jnp.dot(y.astype(BF16), swap, preferred_element_type=F32)
        return y * cos + partner * sin

    nq = q_ref.shape[-1] // HEAD_DIM
    nk = k_ref.shape[-1] // HEAD_DIM
    for j in range(nq):
        q_ref[:, j * HEAD_DIM:(j + 1) * HEAD_DIM] = (
            head(z[:, j * HEAD_DIM:(j + 1) * HEAD_DIM], qg_ref[...]) * (HEAD_DIM ** -0.5 * LOG2E)).astype(BF16)
    for j in range(nk):
        c0 = (nq + j) * HEAD_DIM
        k_ref[:, j * HEAD_DIM:(j + 1) * HEAD_DIM] = head(z[:, c0:c0 + HEAD_DIM], kg_ref[...]).astype(BF16)
    v_ref[...] = z[:, (nq + nk) * HEAD_DIM:].astype(BF16)


def _qkv(x, pre, g, mod, w, qg, kg, cos, sin, s, b):
    t, d = x.shape
    nkv = N_KV_HEADS * HEAD_DIM
    n_pos = s // TM
    pre_specs, pre_args = _pre_io(pre, d, TM, s, b)
    row = pl.BlockSpec((TM, d), lambda i: (i, 0))
    outs = pl.pallas_call(
        functools.partial(_qkv_kernel, has_pre=pre is not None),
        grid=(t // TM,),
        in_specs=[row] + pre_specs + [
            pl.BlockSpec((1, d), lambda i: (0, 0)),
            _mod_spec(d, s // TM, b),
            pl.BlockSpec(w.shape, lambda i: (0, 0)),
            pl.BlockSpec((1, HEAD_DIM), lambda i: (0, 0)),
            pl.BlockSpec((1, HEAD_DIM), lambda i: (0, 0)),
            pl.BlockSpec((TM, HEAD_DIM), lambda i: (jnp.where(i < b * n_pos, i % n_pos, n_pos), 0)),
            pl.BlockSpec((TM, HEAD_DIM), lambda i: (jnp.where(i < b * n_pos, i % n_pos, n_pos), 0))],
        out_specs=([row] if pre else []) + [pl.BlockSpec((TM, d), lambda i: (i, 0)),
                                           pl.BlockSpec((TM, nkv), lambda i: (i, 0)),
                                           pl.BlockSpec((TM, nkv), lambda i: (i, 0))],
        out_shape=([jax.ShapeDtypeStruct((t, d), F32)] if pre else [])
        + [jax.ShapeDtypeStruct((t, d), BF16), jax.ShapeDtypeStruct((t, nkv), BF16),
           jax.ShapeDtypeStruct((t, nkv), BF16)],
        compiler_params=_cp("parallel"),
        name="qkv_proj",
    )(x, *pre_args, g, mod, w, qg, kg, cos, sin)
    return tuple(outs) if pre else (x, *outs)


def _attn_kernel(q_ref, kc_ref, vc_ref, *rest, n_lat):
    if n_lat:
        kl_ref, vl_ref, o_ref, s_scr, vaug = rest
    else:
        o_ref, s_scr, vaug = rest
    n_ctx = kc_ref.shape[0]
    tq = q_ref.shape[0]

    def fill_values():
        vaug[:, HEAD_DIM:] = jnp.ones((n_ctx + n_lat, HEAD_DIM), BF16)
        vaug[0:n_ctx, 0:HEAD_DIM] = vc_ref[...]
        if n_lat:
            vaug[n_ctx:, 0:HEAD_DIM] = vl_ref[...]

    if n_lat:
        pl.when(pl.program_id(2) == 0)(fill_values)
    else:
        fill_values()

    chunks = [(0, n_ctx)] + [(n_ctx + j, ATT_KC) for j in range(0, n_lat, ATT_KC)]
    nt = (((1,), (1,)), ((), ()))
    q_all = jnp.concatenate([q_ref[:, g * HEAD_DIM:(g + 1) * HEAD_DIM] for g in range(GQA_GROUP)], axis=0)
    m_part = jnp.full((GQA_GROUP * tq, LANES), -jnp.inf, F32)
    for off, size in chunks:
        keys = kc_ref[...] if off == 0 else kl_ref[off - n_ctx:off - n_ctx + size, :]
        sc = lax.dot_general(q_all, keys, nt, preferred_element_type=F32)
        s_scr[:, off:off + size] = sc
        for j in range(0, size, LANES):
            m_part = jnp.maximum(m_part, sc[:, j:j + LANES])
    m_row = jnp.max(m_part, axis=-1, keepdims=True)
    hr = GQA_GROUP * tq // 2
    acc = [jnp.zeros((hr, 2 * HEAD_DIM), F32), jnp.zeros((hr, 2 * HEAD_DIM), F32)]
    for off, size in chunks:
        for r in range(2):
            rows = slice(r * hr, (r + 1) * hr)
            p = jnp.exp2((s_scr[rows, off:off + size] - m_row[rows]).astype(BF16))
            acc[r] = acc[r] + jnp.dot(p, vaug[off:off + size, :], preferred_element_type=F32)
    for r in range(2):
        out = (acc[r][:, :HEAD_DIM] / acc[r][:, HEAD_DIM:]).astype(BF16)
        for j in range(GQA_GROUP // 2):
            g = r * (GQA_GROUP // 2) + j
            o_ref[:, g * HEAD_DIM:(g + 1) * HEAD_DIM] = out[j * tq:(j + 1) * tq]


def _attn_ctx_kernel(q_ref, kc_ref, vc_ref, o_all_ref, o_ref, s_scr, vaug):
    del o_all_ref
    _attn_kernel(q_ref, kc_ref, vc_ref, o_ref, s_scr, vaug, n_lat=0)


def _attention(q, k, v, s, c, b):
    t, d = q.shape
    gw = GQA_GROUP * HEAD_DIM
    tq = TQ
    nq = s // tq
    assert s % ATT_KC == 0
    ctx_blk = lambda bi, h, *_: ((b * s) // c + bi, h)
    o_lat = pl.pallas_call(
        functools.partial(_attn_kernel, n_lat=s),
        grid=(b, N_KV_HEADS, nq),
        in_specs=[pl.BlockSpec((tq, gw), lambda bi, h, i: (bi * nq + i, h)),
                  pl.BlockSpec((c, HEAD_DIM), ctx_blk),
                  pl.BlockSpec((c, HEAD_DIM), ctx_blk),
                  pl.BlockSpec((s, HEAD_DIM), lambda bi, h, i: (bi, h)),
                  pl.BlockSpec((s, HEAD_DIM), lambda bi, h, i: (bi, h))],
        out_specs=pl.BlockSpec((tq, gw), lambda bi, h, i: (bi * nq + i, h)),
        out_shape=jax.ShapeDtypeStruct((t, d), BF16),
        scratch_shapes=[pltpu.VMEM((GQA_GROUP * tq, c + s), F32), pltpu.VMEM((c + s, 2 * HEAD_DIM), BF16)],
        compiler_params=_cp("parallel", "parallel", "arbitrary"),
        name="attn_lat",
    )(q, k, v, k, v)
    return pl.pallas_call(
        _attn_ctx_kernel,
        grid=(b, N_KV_HEADS),
        in_specs=[pl.BlockSpec((c, gw), ctx_blk),
                  pl.BlockSpec((c, HEAD_DIM), ctx_blk),
                  pl.BlockSpec((c, HEAD_DIM), ctx_blk),
                  pl.BlockSpec(memory_space=pl.ANY)],
        out_specs=pl.BlockSpec((c, gw), ctx_blk),
        out_shape=jax.ShapeDtypeStruct((t, d), BF16),
        scratch_shapes=[pltpu.VMEM((GQA_GROUP * c, c), F32), pltpu.VMEM((c, 2 * HEAD_DIM), BF16)],
        input_output_aliases={3: 0},
        compiler_params=_cp("parallel", "parallel"),
        name="attn_ctx",
    )(q, k, v, o_lat)


def _attention_mixer(xa, pre, g, mod, w_qkv, qg, kg, w_o, route, s, c, b):
    cos, sin = _rope_tables(s)
    x, q, k, v = _qkv(xa, pre, g, mod, w_qkv.astype(BF16), qg.reshape(1, -1), kg.reshape(1, -1), cos, sin, s, b)
    o = _attention(q, k, v, s, c, b)
    return _out_proj(o, x, mod, w_o.astype(BF16), route, x.shape[0], s, b)


def _gmlp_kernel(*refs, has_pre):
    x_ref, pre_refs, rest = _split_refs(refs, has_pre)
    g_ref, mod_ref, w_in_ref, lng_ref, lnb_ref, ws_ref, bs_ref, w_out_ref = rest[:8]
    route_in, (o_ref, *route_out, uv_ref, cnt_s) = rest[8:11], rest[11:]
    d = x_ref.shape[-1]
    dcm = lng_ref.shape[-1]
    gw = dcm // CM_GROUPS
    x = _combined_rows(x_ref, pre_refs)
    m = mod_ref[0]
    h = _norm_mod(x, g_ref[...], m[:, 0:d], m[:, d:2 * d])
    z = jax.nn.gelu(jnp.dot(h.astype(BF16), w_in_ref[...], preferred_element_type=F32))
    u = z[:, :dcm]
    v = z[:, dcm:]
    mu = jnp.mean(v, axis=-1, keepdims=True)
    vc = v - mu
    var = jnp.mean(vc * vc, axis=-1, keepdims=True)
    vn = (vc * lax.rsqrt(var + NORM_EPS) * lng_ref[...] + lnb_ref[...]).astype(BF16)
    for ck in range(x.shape[0] // CHUNK):
        rows = slice(ck * CHUNK, (ck + 1) * CHUNK)
        for gi in range(CM_GROUPS):
            cols = slice(gi * gw, (gi + 1) * gw)
            mix = jnp.dot(ws_ref[gi], vn[rows, cols], preferred_element_type=F32) + bs_ref[:, gi:gi + 1]
            uv_ref[rows, cols] = (u[rows, cols] * mix).astype(BF16)
    y = jnp.dot(uv_ref[...], w_out_ref[...], preferred_element_type=F32)
    x1 = x + m[:, 2 * d:3 * d] * y
    o_ref[...] = x1
    _route_rows(x1, m, *route_in, *route_out, cnt_s)


def _gmlp_mixer(xa, pre, g, mod, w_in, ln_g, ln_b, w_s, b_s, w_out, route, s, b):
    t, d = xa.shape
    dcm = ln_g.shape[-1]
    full = lambda shape: pl.BlockSpec(shape, lambda i: (0,) * len(shape))
    pre_specs, pre_args = _pre_io(pre, d, TG, s, b)
    r_in, r_out, r_shape, r_scratch = _route_io(d, t, TG)
    x1, *routed = pl.pallas_call(
        functools.partial(_gmlp_kernel, has_pre=pre is not None),
        grid=(t // TG,),
        in_specs=[pl.BlockSpec((TG, d), lambda i: (i, 0))] + pre_specs + [
            full((1, d)),
            _mod_spec(d, s // TG, b),
            full((d, 2 * dcm)), full((1, dcm)), full((1, dcm)),
            full((CM_GROUPS, CHUNK, CHUNK)), full((CHUNK, CM_GROUPS)), full((dcm, d))] + r_in,
        out_specs=[pl.BlockSpec((TG, d), lambda i: (i, 0))] + r_out,
        out_shape=[jax.ShapeDtypeStruct((t, d), F32)] + r_shape,
        scratch_shapes=[pltpu.VMEM((TG, dcm), BF16)] + r_scratch,
        compiler_params=_cp("arbitrary"),
        name="gmlp",
    )(xa, *pre_args, g, mod, w_in.astype(BF16), ln_g.reshape(1, dcm), ln_b.reshape(1, dcm),
      w_s.astype(BF16), b_s.T, w_out.astype(BF16), *route)
    return x1, routed


def _route_rows(x1, m, g_ref, wr_ref, br_ref, hf_ref, rt_ref, ew_ref, cnt_ref, cnt_s):
    tm, d = x1.shape
    ng = EXPERTS_PER_GROUP

    @pl.when(pl.program_id(0) == 0)
    def _():
        cnt_s[...] = jnp.zeros_like(cnt_s)

    hf = _norm_mod(x1, g_ref[...], m[:, 3 * d:4 * d], m[:, 4 * d:5 * d])
    hf_ref[...] = _pack_bf16_pairs(hf)
    hf_hi = hf.astype(BF16)
    hf_lo = (hf - hf_hi.astype(F32)).astype(BF16)
    nt = (((1,), (1,)), ((), ()))
    by_hi = lax.dot_general(wr_ref[...], hf_hi, nt, preferred_element_type=F32)
    logits = (by_hi[:LANES] + by_hi[LANES:]
              + lax.dot_general(wr_ref[0:LANES, :], hf_lo, nt, preferred_element_type=F32)) + br_ref[...]
    neg = -jnp.inf
    row = lax.broadcasted_iota(I32, (ng, tm), 0)
    grp = logits[N_EXPERTS:N_EXPERTS + ng]
    gmax = jnp.max(grp, axis=0, keepdims=True)
    gsel = jnp.min(jnp.where(grp == gmax, row, ng), axis=0, keepdims=True)
    gate_g = 1.0 / jnp.sum(jnp.exp(grp - gmax), axis=0, keepdims=True)
    el = logits[0:ng]
    for gi in range(1, N_GROUPS):
        el = jnp.where(gsel == gi, logits[gi * ng:(gi + 1) * ng], el)
    v1 = jnp.max(el, axis=0, keepdims=True)
    i1 = jnp.min(jnp.where(el == v1, row, ng), axis=0, keepdims=True)
    el2 = jnp.where(row == i1, neg, el)
    v2 = jnp.max(el2, axis=0, keepdims=True)
    i2 = jnp.min(jnp.where(el2 == v2, row, ng), axis=0, keepdims=True)
    e21 = jnp.exp(v2 - v1)
    w1 = gate_g / (1.0 + e21)
    w2 = w1 * e21
    e1 = gsel * ng + i1
    e2 = gsel * ng + i2
    lane_row = lax.broadcasted_iota(I32, (LANES, tm), 0)
    ew_ref[...] = jnp.where(lane_row == 0, w1, jnp.where(lane_row == 1, w2, 0.0)).T

    expert = lax.broadcasted_iota(I32, (N_EXPERTS, tm), 0)
    oh1 = expert == e1
    oh2 = expert == e2
    earlier = (lax.broadcasted_iota(I32, (tm, tm), 0) < lax.broadcasted_iota(I32, (tm, tm), 1)).astype(BF16)
    pre1 = jnp.dot(jnp.where(oh1, 1.0, 0.0).astype(BF16), earlier, preferred_element_type=F32)
    pre2 = jnp.dot(jnp.where(oh2, 1.0, 0.0).astype(BF16), earlier, preferred_element_type=F32)
    tot1 = jnp.sum(jnp.where(oh1, 1.0, 0.0), axis=1, keepdims=True)
    tot2 = jnp.sum(jnp.where(oh2, 1.0, 0.0), axis=1, keepdims=True)
    cnt = cnt_s[:, 0:1]
    rank1 = jnp.sum(jnp.where(oh1, cnt + pre1, 0.0), axis=0, keepdims=True).astype(I32)
    rank2 = jnp.sum(jnp.where(oh2, cnt + tot1 + pre2, 0.0), axis=0, keepdims=True).astype(I32)
    cnt = jnp.broadcast_to(cnt + tot1 + tot2, cnt_s.shape)
    cnt_s[...] = cnt
    cnt_ref[...] = cnt.astype(I32)
    rt_ref[...] = jnp.where(row == 0, e1, jnp.where(row == 1, e2, jnp.where(row == 2, rank1,
                            jnp.where(row == 3, rank2, 0))))


def _route_params(g_ffn, w_group, b_group, w_router, b_router):
    d = w_group.shape[0]
    pad = LANES - N_EXPERTS - N_GROUPS
    wr = jnp.concatenate([w_router.reshape(d, N_EXPERTS), w_group, jnp.zeros((d, pad), F32)], axis=1).T
    br = jnp.concatenate([b_router.reshape(N_EXPERTS), b_group,
                          jnp.full((EXPERTS_PER_GROUP - N_GROUPS,), -jnp.inf, F32),
                          jnp.zeros((pad - EXPERTS_PER_GROUP + N_GROUPS,), F32)]).reshape(LANES, 1)
    wr_hi = wr.astype(BF16)
    wr_lo = (wr - wr_hi.astype(F32)).astype(BF16)
    return g_ffn.reshape(1, d), jnp.concatenate([wr_hi, wr_lo], axis=0), br


def _route_io(d, nrows, tr):
    const = lambda shape: pl.BlockSpec(shape, lambda i: (0, 0))
    row = lambda w: pl.BlockSpec((tr, w), lambda i: (i, 0))
    in_specs = [const((1, d)), const((2 * LANES, d)), const((LANES, 1))]
    out_specs = [row(d // 2), pl.BlockSpec((SUBLANES, tr), lambda i: (0, i)), row(LANES), const((N_EXPERTS, LANES))]
    out_shape = [jax.ShapeDtypeStruct((nrows, d // 2), I32), jax.ShapeDtypeStruct((SUBLANES, nrows), I32),
                 jax.ShapeDtypeStruct((nrows, LANES), F32), jax.ShapeDtypeStruct((N_EXPERTS, LANES), I32)]
    return in_specs, out_specs, out_shape, [pltpu.VMEM((N_EXPERTS, LANES), F32)]


def _plan_kernel(cnt_ref, rt_ref, pos_ref, blk_e_ref, n_used_ref, first_ref, slot_ref, next_ref,
                 start_s, end_s, nxt_s, *, bm):
    n_blk = blk_e_ref.shape[0]
    acc = jnp.int32(0)
    for e in range(N_EXPERTS):
        start_s[e] = acc
        acc = acc + (cnt_ref[e, 0] + (bm - 1)) // bm * bm
        end_s[e] = acc
    n_used = acc // bm
    n_used_ref[0] = n_used
    nxt = jnp.int32(-1)
    for e in reversed(range(N_EXPERTS)):
        nxt_s[e] = nxt
        nxt = jnp.where(cnt_ref[e, 0] > 0, e, nxt)

    def block(i, carry):
        prev_e, runs = carry
        row = jnp.minimum(i, n_used - 1) * bm
        e = lax.while_loop(lambda v: jnp.logical_and(v < N_EXPERTS - 1, end_s[v] <= row), lambda v: v + 1,
                           jnp.maximum(prev_e, 0))
        first = jnp.logical_and(i < n_used, prev_e != e)
        runs = runs + first.astype(I32)
        blk_e_ref[i] = e
        first_ref[i] = first.astype(I32)
        slot_ref[i] = (runs - 1) % 2
        next_ref[i] = nxt_s[e]
        return e, runs

    lax.fori_loop(0, n_blk, block, (jnp.int32(-1), jnp.int32(0)))

    rt = rt_ref[...]
    start_of = jnp.zeros_like(rt)
    for e in range(N_EXPERTS):
        start_of = jnp.where(rt == e, start_s[e], start_of)
    pos_ref[...] = start_of + pltpu.roll(rt, SUBLANES - 2, 0)


def _dispatch_plan(rt, cnt, bm):
    n_tok = rt.shape[1]
    n_rows = 2 * n_tok + N_EXPERTS * bm
    n_blk = n_rows // bm
    smem = pl.BlockSpec(memory_space=pltpu.SMEM)
    vec = lambda n: jax.ShapeDtypeStruct((n,), I32)
    pos, blk_e, n_used, first, slot, nxt = pl.pallas_call(
        functools.partial(_plan_kernel, bm=bm),
        in_specs=[smem, pl.BlockSpec(memory_space=pltpu.VMEM)],
        out_specs=[pl.BlockSpec(memory_space=pltpu.VMEM), smem, smem, smem, smem, smem],
        out_shape=[jax.ShapeDtypeStruct(rt.shape, I32), vec(n_blk), vec(1), vec(n_blk), vec(n_blk), vec(n_blk)],
        scratch_shapes=[pltpu.SMEM((N_EXPERTS,), I32)] * 3,
        name="moe_plan",
    )(cnt, rt)
    return pos, (blk_e, n_used, first, slot, nxt), n_rows


def _sc_mesh():
    return plsc.VectorSubcoreMesh(core_axis_name="c", subcore_axis_name="s")


def _sc_worker_base(per_worker):
    return (lax.axis_index("s") * SC_CORES + lax.axis_index("c")) * per_worker


def _sc_dispatch(hf, pos, n_rows):
    t, d = hf.shape
    per_w = t // SC_WORKERS
    ch = SC_CHUNK
    n_ck = per_w // ch
    assert per_w * SC_WORKERS == t and n_ck * ch == per_w

    @functools.partial(
        pl.kernel, mesh=_sc_mesh(), out_type=jax.ShapeDtypeStruct((n_rows, d), hf.dtype),
        scratch_types=[pltpu.VMEM((per_w,), I32), pltpu.VMEM((per_w,), I32), pltpu.VMEM((2, ch, d), hf.dtype),
                       pltpu.SemaphoreType.DMA((2,)), pltpu.SemaphoreType.DMA((2,)), pltpu.SemaphoreType.DMA((2,))])
    def dispatch(hf_hbm, p_hbm, out_hbm, i0_v, i1_v, rows_v, sem_in, sem_s0, sem_s1):
        base = pl.multiple_of(_sc_worker_base(per_w), SUBLANES)
        pltpu.sync_copy(p_hbm.at[pl.ds(base, per_w)], i0_v)
        pltpu.sync_copy(p_hbm.at[pl.ds(pl.multiple_of(t + base, SUBLANES), per_w)], i1_v)

        def load(ck):
            return pltpu.make_async_copy(hf_hbm.at[pl.ds(base + ck * ch, ch)], rows_v.at[ck % 2], sem_in.at[ck % 2])

        def scatters(ck):
            src = rows_v.at[ck % 2]
            return (pltpu.make_async_copy(src, out_hbm.at[i0_v.at[pl.ds(ck * ch, ch)]], sem_s0.at[ck % 2]),
                    pltpu.make_async_copy(src, out_hbm.at[i1_v.at[pl.ds(ck * ch, ch)]], sem_s1.at[ck % 2]))

        load(0).start()
        for ck in range(n_ck):
            load(ck).wait()
            if ck + 1 < n_ck:
                if ck >= 1:
                    for cp in scatters(ck - 1):
                        cp.wait()
                load(ck + 1).start()
            for cp in scatters(ck):
                cp.start()
        for ck in range(max(n_ck - 2, 0), n_ck):
            for cp in scatters(ck):
                cp.wait()

    return dispatch(hf, pos)


def _sc_gather(rows, idx):
    n = idx.shape[0]
    d = rows.shape[1]
    per_w = n // SC_WORKERS
    ch = 2 * SC_CHUNK
    n_ck = per_w // ch
    assert per_w * SC_WORKERS == n and n_ck * ch == per_w

    @functools.partial(
        pl.kernel, mesh=_sc_mesh(), out_type=jax.ShapeDtypeStruct((n, d), rows.dtype),
        scratch_types=[pltpu.VMEM((per_w,), I32), pltpu.VMEM((2, ch, d), rows.dtype),
                       pltpu.SemaphoreType.DMA((2,)), pltpu.SemaphoreType.DMA((2,))])
    def gather(rows_hbm, i_hbm, out_hbm, i_v, buf, sem_g, sem_w):
        base = pl.multiple_of(_sc_worker_base(per_w), SUBLANES)
        pltpu.sync_copy(i_hbm.at[pl.ds(base, per_w)], i_v)

        def fetch(ck):
            return pltpu.make_async_copy(rows_hbm.at[i_v.at[pl.ds(ck * ch, ch)]], buf.at[ck % 2], sem_g.at[ck % 2])

        def write(ck):
            return pltpu.make_async_copy(buf.at[ck % 2], out_hbm.at[pl.ds(base + ck * ch, ch)], sem_w.at[ck % 2])

        fetch(0).start()
        for ck in range(n_ck):
            fetch(ck).wait()
            if ck + 1 < n_ck:
                if ck >= 1:
                    write(ck - 1).wait()
                fetch(ck + 1).start()
            write(ck).start()
        for ck in range(max(n_ck - 2, 0), n_ck):
            write(ck).wait()

    return gather(rows, idx)


def _expert_kernel(blk_e_ref, n_used_ref, first_ref, slot_ref, next_ref, x_ref, wg_hbm, wu_hbm, wd_hbm, y_ref,
                   wgf, wuf, wdf, wgb, wub, wdb, sem, *, e_base):
    n_used = n_used_ref[0]
    bm = x_ref.shape[0] // MOE_SUB

    def weight_copies(e, slot):
        return (pltpu.make_async_copy(wg_hbm.at[e_base + e], wgf.at[slot], sem.at[slot, 0]),
                pltpu.make_async_copy(wu_hbm.at[e_base + e], wuf.at[slot], sem.at[slot, 1]),
                pltpu.make_async_copy(wd_hbm.at[e_base + e], wdf.at[slot], sem.at[slot, 2]))

    @pl.when(pl.program_id(0) == 0)
    def _():
        for cp in weight_copies(blk_e_ref[0], 0):
            cp.start()

    for j in range(MOE_SUB):
        blk = pl.program_id(0) * MOE_SUB + j
        rows = slice(j * bm, (j + 1) * bm)

        @pl.when(jnp.logical_and(blk < n_used, first_ref[blk] == 1))
        def _():
            slot = slot_ref[blk]
            for cp in weight_copies(blk_e_ref[blk], slot):
                cp.wait()
            nxt = next_ref[blk]

            @pl.when(nxt >= 0)
            def _():
                for cp in weight_copies(nxt, 1 - slot):
                    cp.start()

            wgb[...] = wgf[slot].astype(BF16)
            wub[...] = wuf[slot].astype(BF16)
            wdb[...] = wdf[slot].astype(BF16)

        @pl.when(blk < n_used)
        def _():
            x_hi, x_lo = _unpack_bf16_pairs(x_ref[rows, :])
            xb = jnp.concatenate([x_hi.astype(BF16), x_lo.astype(BF16)], axis=1)
            gt = jnp.dot(xb, wgb[...], preferred_element_type=F32)
            up = jnp.dot(xb, wub[...], preferred_element_type=F32)
            act = (gt * _sigmoid(gt) * up).astype(BF16)
            y_ref[rows, :] = _pack_bf16_pairs(jnp.dot(act, wdb[...], preferred_element_type=F32))


def _experts(x_rows, plan, w_gate, w_up, w_down, layer):
    n_rows, dp = x_rows.shape
    depth, n_e, d, de = w_gate.shape
    step_rows = MOE_SUB * MOE_BM
    assert n_rows % step_rows == 0
    any_spec = pl.BlockSpec(memory_space=pl.ANY)
    last_used = lambda i, be, nu, *_: (jnp.minimum(i, (nu[0] - 1) // MOE_SUB), 0)
    grid_spec = pltpu.PrefetchScalarGridSpec(
        num_scalar_prefetch=5,
        grid=(n_rows // step_rows,),
        in_specs=[pl.BlockSpec((step_rows, dp), last_used), any_spec, any_spec, any_spec],
        out_specs=pl.BlockSpec((step_rows, dp), last_used),
        scratch_shapes=[pltpu.VMEM((2, d, de), F32), pltpu.VMEM((2, d, de), F32), pltpu.VMEM((2, de, d), F32),
                        pltpu.VMEM((d, de), BF16), pltpu.VMEM((d, de), BF16), pltpu.VMEM((de, d), BF16),
                        pltpu.SemaphoreType.DMA((2, 3))],
    )
    return pl.pallas_call(
        functools.partial(_expert_kernel, e_base=layer * n_e),
        grid_spec=grid_spec,
        out_shape=jax.ShapeDtypeStruct((n_rows, dp), I32),
        compiler_params=_cp("arbitrary"),
        name="moe_experts",
    )(*plan, x_rows, w_gate.reshape(depth * n_e, d, de), w_up.reshape(depth * n_e, d, de),
      w_down.reshape(depth * n_e, de, d))


def _combine_kernel(x_ref, *rest):
    *pre_refs, o_ref = rest
    o_ref[...] = _combined_rows(x_ref, pre_refs)


def _combine(x, pre, s, b):
    nrows, d = x.shape
    pre_specs, pre_args = _pre_io(pre, d, TR, s, b)
    return pl.pallas_call(
        _combine_kernel,
        grid=(nrows // TR,),
        in_specs=[pl.BlockSpec((TR, d), lambda i: (i, 0))] + pre_specs,
        out_specs=pl.BlockSpec((TR, d), lambda i: (i, 0)),
        out_shape=jax.ShapeDtypeStruct((nrows, d), F32),
        compiler_params=_cp("parallel"),
        name="moe_combine",
    )(x, *pre_args)


def _expert_outputs(routed, w_gate, w_up, w_down, layer):
    hf, rt, _, cnt = routed
    pos, plan, n_rows = _dispatch_plan(rt, cnt, MOE_BM)
    pos = pos[0:2].reshape(-1)
    x_rows = _sc_dispatch(hf, pos, n_rows)
    y_rows = _experts(x_rows, plan, w_gate, w_up, w_down, layer)
    return _sc_gather(y_rows, pos)


def kernel(x, c, ctx, c_ctx, ada_w, ada_b, norm_mix_g, norm_ffn_g, rg_w_in, rg_conv_w, rg_conv_b, rg_wa, rg_ba, rg_wi, rg_bi, rg_lambda, rg_w_out, at_w_qkv, at_q_g, at_k_g, at_w_o, cm_w_in, cm_ln_g, cm_ln_b, cm_w_s, cm_b_s, cm_w_out, moe_w_group, moe_b_group, moe_w_router, moe_b_router, moe_w_gate, moe_w_up, moe_w_down):
    b, s, d = x.shape
    cl = ctx.shape[1]
    depth = ada_w.shape[0]
    n_lat = b * s
    assert b < SUBLANES and s % max(TI, TR, TG) == 0 and (b * cl) % max(TI, TR, TG) == 0 and cl % TM == 0
    assert d == RG_BLOCKS * LANES

    cin = jnp.concatenate([c, c_ctx[None, :], jnp.zeros((SUBLANES - b - 1, d), F32)], axis=0)
    mod_all = _ada_table(cin, ada_w, ada_b).reshape(depth, SUBLANES, 1, N_MOD * d)
    tok = jnp.concatenate([x.reshape(n_lat, d), ctx.reshape(b * cl, d)], axis=0)

    xa, pre = tok, None
    for l in range(depth):
        kind = l % 3
        j = l // 3
        last = l == depth - 1
        mod = mod_all[l]
        g_mix = norm_mix_g[l].reshape(1, d)
        route = _route_params(norm_ffn_g[l], moe_w_group[l], moe_b_group[l], moe_w_router[l], moe_b_router[l])
        if kind == 0:
            x1, routed = _rglru_mixer(xa, pre, g_mix, mod, rg_w_in[j], rg_conv_w[j], rg_conv_b[j], rg_wa[j],
                                      rg_ba[j], rg_wi[j], rg_bi[j], rg_lambda[j], rg_w_out[j], route, s, cl, b,
                                      n_lat if last else xa.shape[0])
        elif kind == 1:
            x1, routed = _attention_mixer(xa, pre, g_mix, mod, at_w_qkv[j], at_q_g[j], at_k_g[j], at_w_o[j],
                                          route, s, cl, b)
        else:
            x1, routed = _gmlp_mixer(xa, pre, g_mix, mod, cm_w_in[j], cm_ln_g[j], cm_ln_b[j], cm_w_s[j],
                                     cm_b_s[j], cm_w_out[j], route, s, b)
        y01 = _expert_outputs(routed, moe_w_gate, moe_w_up, moe_w_down, l)
        xa, pre = x1, (y01, routed[2], mod)
    return _combine(xa, pre, s, b)[:n_lat].reshape(b, s, d)
```

```python
import functools

import jax
import jax.numpy as jnp
from jax import lax
from jax.experimental import pallas as pl
from jax.experimental.pallas import tpu as pltpu
from jax.experimental.pallas import tpu_sc as plsc

F32 = jnp.float32
BF16 = jnp.bfloat16
I32 = jnp.int32
U32 = jnp.uint32

NORM_EPS = 1e-6
N_MOD = 6
GRID_W = 64
RG_BLOCKS = 8
CONV_W = 4
RG_C = 8.0
HEAD_DIM = 128
N_KV_HEADS = 2
GQA_GROUP = 4
ROPE_THETA = 10000.0
CHUNK = 128
CM_GROUPS = 8
N_GROUPS = 4
EXPERTS_PER_GROUP = 8
N_EXPERTS = N_GROUPS * EXPERTS_PER_GROUP

LANES = 128
SUBLANES = 8
TM = 256
TQ = 512
TI = 1024
TR = 1024
TG = 512
TL = 256
HALO = 8
ATT_KC = 512
LOG2E = 1.4426950408889634
MOE_BM = 256
MOE_SUB = 4
SC_CORES = 2
SC_WORKERS = 32
SC_CHUNK = 32
VMEM_LIMIT = 52 * 2**20


def _cp(*sem):
    return pltpu.CompilerParams(dimension_semantics=sem, vmem_limit_bytes=VMEM_LIMIT)


def _norm_mod(x, g, shift, scale):
    ms = jnp.mean(x * x, axis=-1, keepdims=True)
    y = x * lax.rsqrt(ms + NORM_EPS) * g
    return y * (1.0 + scale) + shift


def _sigmoid(x):
    return 0.5 * jnp.tanh(0.5 * x) + 0.5


def _pack_bf16_pairs(x):
    h = x.shape[-1] // 2
    hi = lax.bitcast_convert_type(x[:, :h].astype(BF16).astype(F32), U32)
    lo = lax.bitcast_convert_type(x[:, h:].astype(BF16).astype(F32), U32)
    return lax.bitcast_convert_type(hi | (lo >> 16), I32)


def _unpack_bf16_pairs(w):
    u = lax.bitcast_convert_type(w, U32)
    hi = lax.bitcast_convert_type(u & jnp.uint32(0xFFFF0000), F32)
    lo = lax.bitcast_convert_type(u << 16, F32)
    return hi, lo


def _combined_rows(x_ref, pre_refs):
    if not pre_refs:
        return x_ref[...]
    y0_ref, y1_ref, ew_ref, modp_ref = pre_refs
    d = x_ref.shape[1]
    ew = ew_ref[...]
    y0_hi, y0_lo = _unpack_bf16_pairs(y0_ref[...])
    y1_hi, y1_lo = _unpack_bf16_pairs(y1_ref[...])
    y = jnp.concatenate([ew[:, 0:1] * y0_hi + ew[:, 1:2] * y1_hi, ew[:, 0:1] * y0_lo + ew[:, 1:2] * y1_lo], axis=1)
    return x_ref[...] + modp_ref[0][:, 5 * d:6 * d] * y


def _pre_io(pre, d, tr, s, b):
    if pre is None:
        return [], []
    y01, ew, mod_prev = pre
    nb = y01.shape[0] // 2 // tr
    specs = [pl.BlockSpec((tr, d // 2), lambda i: (i, 0)), pl.BlockSpec((tr, d // 2), lambda i: (i + nb, 0)),
             pl.BlockSpec((tr, LANES), lambda i: (i, 0)), _mod_spec(d, s // tr, b)]
    return specs, [y01, y01, ew, mod_prev]


def _split_refs(refs, has_pre):
    return (refs[0], refs[1:5], refs[5:]) if has_pre else (refs[0], (), refs[1:])


def _mod_spec(d, rows_per_sample, n_samples):
    return pl.BlockSpec((1, 1, N_MOD * d),
                        lambda i, *_: (jnp.minimum(i // rows_per_sample, n_samples), 0, 0))


def _ada_kernel(c_ref, w_ref, b_ref, o_ref):
    cin = c_ref[...]
    act = cin * jax.nn.sigmoid(cin)
    w = w_ref[0]
    w_hi = w.astype(BF16)
    w_lo = (w - w_hi.astype(F32)).astype(BF16)
    a_hi = act.astype(BF16)
    a_lo = (act - a_hi.astype(F32)).astype(BF16)
    o_ref[0] = (jnp.dot(a_hi, w_hi, preferred_element_type=F32) + jnp.dot(a_lo, w_hi, preferred_element_type=F32)
                + jnp.dot(a_hi, w_lo, preferred_element_type=F32)) + b_ref[0]


def _ada_table(cin, ada_w, ada_b):
    depth, d, n = ada_w.shape
    tn = 2 * d
    return pl.pallas_call(
        _ada_kernel,
        grid=(depth, n // tn),
        in_specs=[pl.BlockSpec((SUBLANES, d), lambda l, j: (0, 0)),
                  pl.BlockSpec((1, d, tn), lambda l, j: (l, 0, j)),
                  pl.BlockSpec((1, 1, tn), lambda l, j: (l, 0, j))],
        out_specs=pl.BlockSpec((1, SUBLANES, tn), lambda l, j: (l, 0, j)),
        out_shape=jax.ShapeDtypeStruct((depth, SUBLANES, n), F32),
        compiler_params=_cp("parallel", "parallel"),
        name="ada_table",
    )(cin, ada_w, ada_b.reshape(depth, 1, n))


def _out_kernel(y_ref, x_ref, mod_ref, w_ref, *rest):
    route_in, (o_ref, *route_out) = rest[:3], rest[3:]
    d = x_ref.shape[-1]
    m = mod_ref[0]
    y = jnp.dot(y_ref[...].astype(BF16), w_ref[...], preferred_element_type=F32)
    x1 = x_ref[...] + m[:, 2 * d:3 * d] * y
    o_ref[...] = x1
    _route_rows(x1, m, *route_in, *route_out)


def _out_proj(y, x, mod, w, route, nrows, s, b):
    d = x.shape[-1]
    k = y.shape[-1]
    r_in, r_out, r_shape, r_scratch = _route_io(d, nrows, TR)
    x1, *routed = pl.pallas_call(
        _out_kernel,
        grid=(nrows // TR,),
        in_specs=[pl.BlockSpec((TR, k), lambda i: (i, 0)),
                  pl.BlockSpec((TR, d), lambda i: (i, 0)),
                  _mod_spec(d, s // TR, b),
                  pl.BlockSpec((k, d), lambda i: (0, 0))] + r_in,
        out_specs=[pl.BlockSpec((TR, d), lambda i: (i, 0))] + r_out,
        out_shape=[jax.ShapeDtypeStruct((nrows, d), F32)] + r_shape,
        scratch_shapes=r_scratch,
        compiler_params=_cp("arbitrary"),
        name="out_proj",
    )(y, x, mod, w, *route)
    return x1, routed


def _rg_in_kernel(*refs, has_pre):
    x_ref, pre_refs, (g_ref, mod_ref, w_ref, *outs) = _split_refs(refs, has_pre)
    gg_ref, xin_ref = outs[-2:]
    d = x_ref.shape[-1]
    m = mod_ref[0]
    x = _combined_rows(x_ref, pre_refs)
    if has_pre:
        outs[0][...] = x
    h = _norm_mod(x, g_ref[...], m[:, 0:d], m[:, d:2 * d])
    z = jnp.dot(h.astype(BF16), w_ref[...], preferred_element_type=F32)
    tm = x_ref.shape[0]
    for n in range(d // LANES):
        cols = slice(n * LANES, (n + 1) * LANES)
        gg_ref[pl.ds(n, tm, stride=SUBLANES), :] = jax.nn.gelu(z[:, cols])
        xin_ref[pl.ds(n, tm, stride=SUBLANES), :] = z[:, d + n * LANES:d + (n + 1) * LANES]


def _rg_in(x, pre, g, mod, w, s, b):
    t, d = x.shape
    assert d == SUBLANES * LANES
    pre_specs, pre_args = _pre_io(pre, d, TI, s, b)
    row = pl.BlockSpec((TI, d), lambda i: (i, 0))
    tmajor = pl.BlockSpec((TI * SUBLANES, LANES), lambda i: (i, 0))
    outs = pl.pallas_call(
        functools.partial(_rg_in_kernel, has_pre=pre is not None),
        grid=(t // TI,),
        in_specs=[row] + pre_specs + [pl.BlockSpec((1, d), lambda i: (0, 0)), _mod_spec(d, s // TI, b),
                                      pl.BlockSpec((d, 2 * d), lambda i: (0, 0))],
        out_specs=([row] if pre else []) + [tmajor, tmajor],
        out_shape=([jax.ShapeDtypeStruct((t, d), F32)] if pre else [])
        + [jax.ShapeDtypeStruct((t * SUBLANES, LANES), F32)] * 2,
        compiler_params=_cp("parallel"),
        name="rg_in",
    )(x, *pre_args, g, mod, w)
    return (outs[0], outs[1], outs[2]) if pre else (x, outs[0], outs[1])


def _rg_gates_and_scan(xc, wa_ref, wi_ref, ba_ref, bi_ref, lam_ref, a_s, b_s, h_dst, hcar, reverse):
    @pl.when(pl.program_id(1) == 0)
    def _():
        hcar[...] = jnp.zeros_like(hcar)

    for n in range(RG_BLOCKS):
        cols = slice(n * LANES, (n + 1) * LANES)
        xn = xc[pl.ds(n, TL, stride=SUBLANES), :]
        xb = xn.astype(BF16)
        ta = jnp.tanh(jnp.dot(xb, wa_ref[n], preferred_element_type=F32) + ba_ref[:, cols])
        ti = jnp.tanh(jnp.dot(xb, wi_ref[n], preferred_element_type=F32) + bi_ref[:, cols])
        k = (-0.5 * RG_C * LOG2E) * jax.nn.softplus(-lam_ref[:, cols])
        a = jnp.exp2(k * ta + k)
        om = 1.0 - a * a
        root = jnp.where(om > 0.0, om * lax.rsqrt(om), 0.0)
        a_s[pl.ds(n, TL, stride=SUBLANES), :] = a
        b_s[pl.ds(n, TL, stride=SUBLANES), :] = root * (0.5 * xn) * (ti + 1.0)

    def two_steps(p, h):
        t0 = (TL - 1 - 2 * p) if reverse else 2 * p
        t1 = (t0 - 1) if reverse else (t0 + 1)
        r0 = pl.multiple_of(t0 * SUBLANES, SUBLANES)
        r1 = pl.multiple_of(t1 * SUBLANES, SUBLANES)
        a0 = a_s[pl.ds(r0, SUBLANES), :]
        b0 = b_s[pl.ds(r0, SUBLANES), :]
        a1 = a_s[pl.ds(r1, SUBLANES), :]
        b1 = b_s[pl.ds(r1, SUBLANES), :]
        h_dst[pl.ds(r0, SUBLANES), :] = a0 * h + b0
        h2 = (a1 * a0) * h + (a1 * b0 + b1)
        h_dst[pl.ds(r1, SUBLANES), :] = h2
        return h2

    hcar[...] = lax.fori_loop(0, TL // 2, two_steps, hcar[...], unroll=8)


def _rg_fwd_kernel(xm_ref, xprev_ref, xnext_ref, cw_ref, cb_ref, wa_ref, wi_ref, ba_ref, bi_ref, lam_ref,
                   hf_ref, xc_ref, xpad, a_s, b_s, hcar, *, nlat):
    rows = TL * SUBLANES
    hrows = HALO * SUBLANES
    j = pl.program_id(1)
    has_prev = j >= 2
    has_next = jnp.logical_and(j >= 1, j < nlat)
    xpad[0:hrows, :] = jnp.where(has_prev, xprev_ref[...], 0.0)
    xpad[hrows:hrows + rows, :] = xm_ref[...]
    xpad[hrows + rows:2 * hrows + rows, :] = jnp.where(has_next, xnext_ref[...], 0.0)
    acc = jnp.broadcast_to(cb_ref[...][None], (TL, SUBLANES, LANES))
    for k in range(CONV_W):
        off = (HALO + k - CONV_W // 2) * SUBLANES
        tap = xpad[off:off + rows, :].reshape(TL, SUBLANES, LANES)
        acc = acc + tap * cw_ref[k][None]
    xc_ref[...] = acc.reshape(rows, LANES)
    _rg_gates_and_scan(xc_ref, wa_ref, wi_ref, ba_ref, bi_ref, lam_ref, a_s, b_s, hf_ref, hcar, False)


def _rg_bwd_kernel(xc_ref, wa_ref, wi_ref, ba_ref, bi_ref, lam_ref, hf_ref, gg_ref, out_ref, a_s, b_s, h_s, hcar):
    _rg_gates_and_scan(xc_ref, wa_ref, wi_ref, ba_ref, bi_ref, lam_ref, a_s, b_s, h_s, hcar, True)
    h_s[...] = gg_ref[...] * (hf_ref[...] + h_s[...])
    for n in range(RG_BLOCKS):
        out_ref[:, n * LANES:(n + 1) * LANES] = h_s[pl.ds(n, TL, stride=SUBLANES), :].astype(BF16)


def _rg_scans(xin8, gg8, conv_w, conv_b, wa, wi, ba, bi, lam, s, c, b):
    assert c == TL and s % TL == 0
    rows = TL * SUBLANES
    hrows = HALO * SUBLANES
    nlat = s // TL
    t = xin8.shape[0] // SUBLANES
    n_halo = t // HALO
    d = RG_BLOCKS * LANES

    def chunk(reverse):
        return lambda bi_, j: jnp.where(j == 0, (b * s) // TL + bi_,
                                        bi_ * nlat + ((nlat - j) if reverse else (j - 1)))

    fwd, bwd = chunk(False), chunk(True)
    main_f = pl.BlockSpec((rows, LANES), lambda bi_, j: (fwd(bi_, j), 0))
    main_b = pl.BlockSpec((rows, LANES), lambda bi_, j: (bwd(bi_, j), 0))
    prev = pl.BlockSpec((hrows, LANES), lambda bi_, j: (jnp.maximum(fwd(bi_, j) * (TL // HALO) - 1, 0), 0))
    nxt = pl.BlockSpec((hrows, LANES),
                       lambda bi_, j: (jnp.minimum((fwd(bi_, j) + 1) * (TL // HALO), n_halo - 1), 0))
    full = lambda shape: pl.BlockSpec(shape, lambda bi_, j: (0,) * len(shape))
    gate_specs = [full((RG_BLOCKS, LANES, LANES)), full((RG_BLOCKS, LANES, LANES)),
                  full((1, d)), full((1, d)), full((1, d))]
    gate_args = lambda k: [(0.5 * wa[k]).astype(BF16), (0.5 * wi[k]).astype(BF16), 0.5 * ba[k].reshape(1, d),
                           0.5 * bi[k].reshape(1, d), lam[k].reshape(1, d)]
    tmajor = jax.ShapeDtypeStruct(xin8.shape, F32)
    buf = pltpu.VMEM((rows, LANES), F32)
    hf8, xc8 = pl.pallas_call(
        functools.partial(_rg_fwd_kernel, nlat=nlat),
        grid=(b, nlat + 1),
        in_specs=[main_f, prev, nxt, full((CONV_W, SUBLANES, LANES)), full((SUBLANES, LANES))] + gate_specs,
        out_specs=[main_f, main_f],
        out_shape=[tmajor, tmajor],
        scratch_shapes=[pltpu.VMEM((rows + 2 * hrows, LANES), F32), buf, buf, pltpu.VMEM((SUBLANES, LANES), F32)],
        compiler_params=_cp("parallel", "arbitrary"),
        name="rg_scan_fwd",
    )(xin8, xin8, xin8, conv_w.reshape(CONV_W, SUBLANES, LANES), conv_b.reshape(SUBLANES, LANES), *gate_args(0))
    return pl.pallas_call(
        _rg_bwd_kernel,
        grid=(b, nlat + 1),
        in_specs=[main_b] + gate_specs + [main_b, main_b],
        out_specs=pl.BlockSpec((TL, d), lambda bi_, j: (bwd(bi_, j), 0)),
        out_shape=jax.ShapeDtypeStruct((t, d), BF16),
        scratch_shapes=[buf, buf, buf, pltpu.VMEM((SUBLANES, LANES), F32)],
        compiler_params=_cp("parallel", "arbitrary"),
        name="rg_scan_bwd",
    )(xc8, *gate_args(1), hf8, gg8)


def _rglru_mixer(xa, pre, g, mod, w_in, conv_w, conv_b, wa, ba, wi, bi, lam, w_out, route, s, c, b, nrows_out):
    x, gg8, xin8 = _rg_in(xa, pre, g, mod, w_in.astype(BF16), s, b)
    y = _rg_scans(xin8, gg8, conv_w, conv_b, wa, wi, ba, bi, lam, s, c, b)
    return _out_proj(y, x, mod, w_out.astype(BF16), route, nrows_out, s, b)


def _rope_tables(s):
    pos = jnp.arange(s, dtype=F32)
    row = jnp.floor(pos / GRID_W)
    col = pos - row * GRID_W
    n_freq = HEAD_DIM // 4
    inv = ROPE_THETA ** (-jnp.arange(n_freq, dtype=F32) * 2.0 / (HEAD_DIM // 2))
    ar = row[:, None] * inv
    ac = col[:, None] * inv
    cos = jnp.concatenate([jnp.cos(ar), jnp.cos(ar), jnp.cos(ac), jnp.cos(ac)], axis=1)
    sin = jnp.concatenate([-jnp.sin(ar), jnp.sin(ar), -jnp.sin(ac), jnp.sin(ac)], axis=1)
    cos = jnp.concatenate([cos, jnp.ones((TM, HEAD_DIM), F32)], axis=0)
    sin = jnp.concatenate([sin, jnp.zeros((TM, HEAD_DIM), F32)], axis=0)
    return cos, sin


def _qkv_kernel(*refs, has_pre):
    x_ref, pre_refs, (g_ref, mod_ref, w_ref, qg_ref, kg_ref, cos_ref, sin_ref, *outs) = _split_refs(refs, has_pre)
    q_ref, k_ref, v_ref = outs[-3:]
    d = x_ref.shape[-1]
    m = mod_ref[0]
    x = _combined_rows(x_ref, pre_refs)
    if has_pre:
        outs[0][...] = x
    h = _norm_mod(x, g_ref[...], m[:, 0:d], m[:, d:2 * d])
    z = jnp.dot(h.astype(BF16), w_ref[...], preferred_element_type=F32)
    cos = cos_ref[...]
    sin = sin_ref[...]
    src = lax.broadcasted_iota(I32, (HEAD_DIM, HEAD_DIM), 0)
    dst = lax.broadcasted_iota(I32, (HEAD_DIM, HEAD_DIM), 1)
    quarter = HEAD_DIM // 4
    partner_of = jnp.where((dst % (2 * quarter)) < quarter, dst + quarter, dst - quarter)
    swap = jnp.where(src == partner_of, 1.0, 0.0).astype(BF16)

    def head(zc, gain):
        ms = jnp.mean(zc * zc, axis=-1, keepdims=True)
        y = zc * lax.rsqrt(ms + NORM_EPS) * gain
        partner = jnp.dot(y.astype(BF16), swap, preferred_element_type=F32)
        return y * cos + partner * sin

    nq = q_ref.shape[-1] // HEAD_DIM
    nk = k_ref.shape[-1] // HEAD_DIM
    for j in range(nq):
        q_ref[:, j * HEAD_DIM:(j + 1) * HEAD_DIM] = (
            head(z[:, j * HEAD_DIM:(j + 1) * HEAD_DIM], qg_ref[...]) * (HEAD_DIM ** -0.5 * LOG2E)).astype(BF16)
    for j in range(nk):
        c0 = (nq + j) * HEAD_DIM
        k_ref[:, j * HEAD_DIM:(j + 1) * HEAD_DIM] = head(z[:, c0:c0 + HEAD_DIM], kg_ref[...]).astype(BF16)
    v_ref[...] = z[:, (nq + nk) * HEAD_DIM:].astype(BF16)


def _qkv(x, pre, g, mod, w, qg, kg, cos, sin, s, b):
    t, d = x.shape
    nkv = N_KV_HEADS * HEAD_DIM
    n_pos = s // TM
    pre_specs, pre_args = _pre_io(pre, d, TM, s, b)
    row = pl.BlockSpec((TM, d), lambda i: (i, 0))
    outs = pl.pallas_call(
        functools.partial(_qkv_kernel, has_pre=pre is not None),
        grid=(t // TM,),
        in_specs=[row] + pre_specs + [
            pl.BlockSpec((1, d), lambda i: (0, 0)),
            _mod_spec(d, s // TM, b),
            pl.BlockSpec(w.shape, lambda i: (0, 0)),
            pl.BlockSpec((1, HEAD_DIM), lambda i: (0, 0)),
            pl.BlockSpec((1, HEAD_DIM), lambda i: (0, 0)),
            pl.BlockSpec((TM, HEAD_DIM), lambda i: (jnp.where(i < b * n_pos, i % n_pos, n_pos), 0)),
            pl.BlockSpec((TM, HEAD_DIM), lambda i: (jnp.where(i < b * n_pos, i % n_pos, n_pos), 0))],
        out_specs=([row] if pre else []) + [pl.BlockSpec((TM, d), lambda i: (i, 0)),
                                           pl.BlockSpec((TM, nkv), lambda i: (i, 0)),
                                           pl.BlockSpec((TM, nkv), lambda i: (i, 0))],
        out_shape=([jax.ShapeDtypeStruct((t, d), F32)] if pre else [])
        + [jax.ShapeDtypeStruct((t, d), BF16), jax.ShapeDtypeStruct((t, nkv), BF16),
           jax.ShapeDtypeStruct((t, nkv), BF16)],
        compiler_params=_cp("parallel"),
        name="qkv_proj",
    )(x, *pre_args, g, mod, w, qg, kg, cos, sin)
    return tuple(outs) if pre else (x, *outs)


def _attn_kernel(q_ref, kc_ref, vc_ref, *rest, n_lat):
    if n_lat:
        kl_ref, vl_ref, o_ref, s_scr, vaug = rest
    else:
        o_ref, s_scr, vaug = rest
    n_ctx = kc_ref.shape[0]
    tq = q_ref.shape[0]

    def fill_values():
        vaug[:, HEAD_DIM:] = jnp.ones((n_ctx + n_lat, HEAD_DIM), BF16)
        vaug[0:n_ctx, 0:HEAD_DIM] = vc_ref[...]
        if n_lat:
            vaug[n_ctx:, 0:HEAD_DIM] = vl_ref[...]

    if n_lat:
        pl.when(pl.program_id(2) == 0)(fill_values)
    else:
        fill_values()

    chunks = [(0, n_ctx)] + [(n_ctx + j, ATT_KC) for j in range(0, n_lat, ATT_KC)]
    nt = (((1,), (1,)), ((), ()))
    q_all = jnp.concatenate([q_ref[:, g * HEAD_DIM:(g + 1) * HEAD_DIM] for g in range(GQA_GROUP)], axis=0)
    m_part = jnp.full((GQA_GROUP * tq, LANES), -jnp.inf, F32)
    for off, size in chunks:
        keys = kc_ref[...] if off == 0 else kl_ref[off - n_ctx:off - n_ctx + size, :]
        sc = lax.dot_general(q_all, keys, nt, preferred_element_type=F32)
        s_scr[:, off:off + size] = sc
        for j in range(0, size, LANES):
            m_part = jnp.maximum(m_part, sc[:, j:j + LANES])
    m_row = jnp.max(m_part, axis=-1, keepdims=True)
    hr = GQA_GROUP * tq // 2
    acc = [jnp.zeros((hr, 2 * HEAD_DIM), F32), jnp.zeros((hr, 2 * HEAD_DIM), F32)]
    for off, size in chunks:
        for r in range(2):
            rows = slice(r * hr, (r + 1) * hr)
            p = jnp.exp2((s_scr[rows, off:off + size] - m_row[rows]).astype(BF16))
            acc[r] = acc[r] + jnp.dot(p, vaug[off:off + size, :], preferred_element_type=F32)
    for r in range(2):
        out = (acc[r][:, :HEAD_DIM] / acc[r][:, HEAD_DIM:]).astype(BF16)
        for j in range(GQA_GROUP // 2):
            g = r * (GQA_GROUP // 2) + j
            o_ref[:, g * HEAD_DIM:(g + 1) * HEAD_DIM] = out[j * tq:(j + 1) * tq]


def _attn_ctx_kernel(q_ref, kc_ref, vc_ref, o_all_ref, o_ref, s_scr, vaug):
    del o_all_ref
    _attn_kernel(q_ref, kc_ref, vc_ref, o_ref, s_scr, vaug, n_lat=0)


def _attention(q, k, v, s, c, b):
    t, d = q.shape
    gw = GQA_GROUP * HEAD_DIM
    tq = TQ
    nq = s // tq
    assert s % ATT_KC == 0
    ctx_blk = lambda bi, h, *_: ((b * s) // c + bi, h)
    o_lat = pl.pallas_call(
        functools.partial(_attn_kernel, n_lat=s),
        grid=(b, N_KV_HEADS, nq),
        in_specs=[pl.BlockSpec((tq, gw), lambda bi, h, i: (bi * nq + i, h)),
                  pl.BlockSpec((c, HEAD_DIM), ctx_blk),
                  pl.BlockSpec((c, HEAD_DIM), ctx_blk),
                  pl.BlockSpec((s, HEAD_DIM), lambda bi, h, i: (bi, h)),
                  pl.BlockSpec((s, HEAD_DIM), lambda bi, h, i: (bi, h))],
        out_specs=pl.BlockSpec((tq, gw), lambda bi, h, i: (bi * nq + i, h)),
        out_shape=jax.ShapeDtypeStruct((t, d), BF16),
        scratch_shapes=[pltpu.VMEM((GQA_GROUP * tq, c + s), F32), pltpu.VMEM((c + s, 2 * HEAD_DIM), BF16)],
        compiler_params=_cp("parallel", "parallel", "arbitrary"),
        name="attn_lat",
    )(q, k, v, k, v)
    return pl.pallas_call(
        _attn_ctx_kernel,
        grid=(b, N_KV_HEADS),
        in_specs=[pl.BlockSpec((c, gw), ctx_blk),
                  pl.BlockSpec((c, HEAD_DIM), ctx_blk),
                  pl.BlockSpec((c, HEAD_DIM), ctx_blk),
                  pl.BlockSpec(memory_space=pl.ANY)],
        out_specs=pl.BlockSpec((c, gw), ctx_blk),
        out_shape=jax.ShapeDtypeStruct((t, d), BF16),
        scratch_shapes=[pltpu.VMEM((GQA_GROUP * c, c), F32), pltpu.VMEM((c, 2 * HEAD_DIM), BF16)],
        input_output_aliases={3: 0},
        compiler_params=_cp("parallel", "parallel"),
        name="attn_ctx",
    )(q, k, v, o_lat)


def _attention_mixer(xa, pre, g, mod, w_qkv, qg, kg, w_o, route, s, c, b):
    cos, sin = _rope_tables(s)
    x, q, k, v = _qkv(xa, pre, g, mod, w_qkv.astype(BF16), qg.reshape(1, -1), kg.reshape(1, -1), cos, sin, s, b)
    o = _attention(q, k, v, s, c, b)
    return _out_proj(o, x, mod, w_o.astype(BF16), route, x.shape[0], s, b)


def _gmlp_kernel(*refs, has_pre):
    x_ref, pre_refs, rest = _split_refs(refs, has_pre)
    g_ref, mod_ref, w_in_ref, lng_ref, lnb_ref, ws_ref, bs_ref, w_out_ref = rest[:8]
    route_in, (o_ref, *route_out, uv_ref, cnt_s) = rest[8:11], rest[11:]
    d = x_ref.shape[-1]
    dcm = lng_ref.shape[-1]
    gw = dcm // CM_GROUPS
    x = _combined_rows(x_ref, pre_refs)
    m = mod_ref[0]
    h = _norm_mod(x, g_ref[...], m[:, 0:d], m[:, d:2 * d])
    z = jax.nn.gelu(jnp.dot(h.astype(BF16), w_in_ref[...], preferred_element_type=F32))
    u = z[:, :dcm]
    v = z[:, dcm:]
    mu = jnp.mean(v, axis=-1, keepdims=True)
    vc = v - mu
    var = jnp.mean(vc * vc, axis=-1, keepdims=True)
    vn = (vc * lax.rsqrt(var + NORM_EPS) * lng_ref[...] + lnb_ref[...]).astype(BF16)
    for ck in range(x.shape[0] // CHUNK):
        rows = slice(ck * CHUNK, (ck + 1) * CHUNK)
        for gi in range(CM_GROUPS):
            cols = slice(gi * gw, (gi + 1) * gw)
            mix = jnp.dot(ws_ref[gi], vn[rows, cols], preferred_element_type=F32) + bs_ref[:, gi:gi + 1]
            uv_ref[rows, cols] = (u[rows, cols] * mix).astype(BF16)
    y = jnp.dot(uv_ref[...], w_out_ref[...], preferred_element_type=F32)
    x1 = x + m[:, 2 * d:3 * d] * y
    o_ref[...] = x1
    _route_rows(x1, m, *route_in, *route_out, cnt_s)


def _gmlp_mixer(xa, pre, g, mod, w_in, ln_g, ln_b, w_s, b_s, w_out, route, s, b):
    t, d = xa.shape
    dcm = ln_g.shape[-1]
    full = lambda shape: pl.BlockSpec(shape, lambda i: (0,) * len(shape))
    pre_specs, pre_args = _pre_io(pre, d, TG, s, b)
    r_in, r_out, r_shape, r_scratch = _route_io(d, t, TG)
    x1, *routed = pl.pallas_call(
        functools.partial(_gmlp_kernel, has_pre=pre is not None),
        grid=(t // TG,),
        in_specs=[pl.BlockSpec((TG, d), lambda i: (i, 0))] + pre_specs + [
            full((1, d)),
            _mod_spec(d, s // TG, b),
            full((d, 2 * dcm)), full((1, dcm)), full((1, dcm)),
            full((CM_GROUPS, CHUNK, CHUNK)), full((CHUNK, CM_GROUPS)), full((dcm, d))] + r_in,
        out_specs=[pl.BlockSpec((TG, d), lambda i: (i, 0))] + r_out,
        out_shape=[jax.ShapeDtypeStruct((t, d), F32)] + r_shape,
        scratch_shapes=[pltpu.VMEM((TG, dcm), BF16)] + r_scratch,
        compiler_params=_cp("arbitrary"),
        name="gmlp",
    )(xa, *pre_args, g, mod, w_in.astype(BF16), ln_g.reshape(1, dcm), ln_b.reshape(1, dcm),
      w_s.astype(BF16), b_s.T, w_out.astype(BF16), *route)
    return x1, routed


def _route_rows(x1, m, g_ref, wr_ref, br_ref, hf_ref, rt_ref, ew_ref, cnt_ref, cnt_s):
    tm, d = x1.shape
    ng = EXPERTS_PER_GROUP

    @pl.when(pl.program_id(0) == 0)
    def _():
        cnt_s[...] = jnp.zeros_like(cnt_s)

    hf = _norm_mod(x1, g_ref[...], m[:, 3 * d:4 * d], m[:, 4 * d:5 * d])
    hf_ref[...] = _pack_bf16_pairs(hf)
    hf_hi = hf.astype(BF16)
    hf_lo = (hf - hf_hi.astype(F32)).astype(BF16)
    nt = (((1,), (1,)), ((), ()))
    by_hi = lax.dot_general(wr_ref[...], hf_hi, nt, preferred_element_type=F32)
    logits = (by_hi[:LANES] + by_hi[LANES:]
              + lax.dot_general(wr_ref[0:LANES, :], hf_lo, nt, preferred_element_type=F32)) + br_ref[...]
    neg = -jnp.inf
    row = lax.broadcasted_iota(I32, (ng, tm), 0)
    grp = logits[N_EXPERTS:N_EXPERTS + ng]
    gmax = jnp.max(grp, axis=0, keepdims=True)
    gsel = jnp.min(jnp.where(grp == gmax, row, ng), axis=0, keepdims=True)
    gate_g = 1.0 / jnp.sum(jnp.exp(grp - gmax), axis=0, keepdims=True)
    el = logits[0:ng]
    for gi in range(1, N_GROUPS):
        el = jnp.where(gsel == gi, logits[gi * ng:(gi + 1) * ng], el)
    v1 = jnp.max(el, axis=0, keepdims=True)
    i1 = jnp.min(jnp.where(el == v1, row, ng), axis=0, keepdims=True)
    el2 = jnp.where(row == i1, neg, el)
    v2 = jnp.max(el2, axis=0, keepdims=True)
    i2 = jnp.min(jnp.where(el2 == v2, row, ng), axis=0, keepdims=True)
    e21 = jnp.exp(v2 - v1)
    w1 = gate_g / (1.0 + e21)
    w2 = w1 * e21
    e1 = gsel * ng + i1
    e2 = gsel * ng + i2
    lane_row = lax.broadcasted_iota(I32, (LANES, tm), 0)
    ew_ref[...] = jnp.where(lane_row == 0, w1, jnp.where(lane_row == 1, w2, 0.0)).T

    expert = lax.broadcasted_iota(I32, (N_EXPERTS, tm), 0)
    oh1 = expert == e1
    oh2 = expert == e2
    earlier = (lax.broadcasted_iota(I32, (tm, tm), 0) < lax.broadcasted_iota(I32, (tm, tm), 1)).astype(BF16)
    pre1 = jnp.dot(jnp.where(oh1, 1.0, 0.0).astype(BF16), earlier, preferred_element_type=F32)
    pre2 = jnp.dot(jnp.where(oh2, 1.0, 0.0).astype(BF16), earlier, preferred_element_type=F32)
    tot1 = jnp.sum(jnp.where(oh1, 1.0, 0.0), axis=1, keepdims=True)
    tot2 = jnp.sum(jnp.where(oh2, 1.0, 0.0), axis=1, keepdims=True)
    cnt = cnt_s[:, 0:1]
    rank1 = jnp.sum(jnp.where(oh1, cnt + pre1, 0.0), axis=0, keepdims=True).astype(I32)
    rank2 = jnp.sum(jnp.where(oh2, cnt + tot1 + pre2, 0.0), axis=0, keepdims=True).astype(I32)
    cnt = jnp.broadcast_to(cnt + tot1 + tot2, cnt_s.shape)
    cnt_s[...] = cnt
    cnt_ref[...] = cnt.astype(I32)
    rt_ref[...] = jnp.where(row == 0, e1, jnp.where(row == 1, e2, jnp.where(row == 2, rank1,
                            jnp.where(row == 3, rank2, 0))))


def _route_params(g_ffn, w_group, b_group, w_router, b_router):
    d = w_group.shape[0]
    pad = LANES - N_EXPERTS - N_GROUPS
    wr = jnp.concatenate([w_router.reshape(d, N_EXPERTS), w_group, jnp.zeros((d, pad), F32)], axis=1).T
    br = jnp.concatenate([b_router.reshape(N_EXPERTS), b_group,
                          jnp.full((EXPERTS_PER_GROUP - N_GROUPS,), -jnp.inf, F32),
                          jnp.zeros((pad - EXPERTS_PER_GROUP + N_GROUPS,), F32)]).reshape(LANES, 1)
    wr_hi = wr.astype(BF16)
    wr_lo = (wr - wr_hi.astype(F32)).astype(BF16)
    return g_ffn.reshape(1, d), jnp.concatenate([wr_hi, wr_lo], axis=0), br


def _route_io(d, nrows, tr):
    const = lambda shape: pl.BlockSpec(shape, lambda i: (0, 0))
    row = lambda w: pl.BlockSpec((tr, w), lambda i: (i, 0))
    in_specs = [const((1, d)), const((2 * LANES, d)), const((LANES, 1))]
    out_specs = [row(d // 2), pl.BlockSpec((SUBLANES, tr), lambda i: (0, i)), row(LANES), const((N_EXPERTS, LANES))]
    out_shape = [jax.ShapeDtypeStruct((nrows, d // 2), I32), jax.ShapeDtypeStruct((SUBLANES, nrows), I32),
                 jax.ShapeDtypeStruct((nrows, LANES), F32), jax.ShapeDtypeStruct((N_EXPERTS, LANES), I32)]
    return in_specs, out_specs, out_shape, [pltpu.VMEM((N_EXPERTS, LANES), F32)]


def _plan_kernel(cnt_ref, rt_ref, pos_ref, blk_e_ref, n_used_ref, first_ref, slot_ref, next_ref,
                 start_s, end_s, nxt_s, *, bm):
    n_blk = blk_e_ref.shape[0]
    acc = jnp.int32(0)
    for e in range(N_EXPERTS):
        start_s[e] = acc
        acc = acc + (cnt_ref[e, 0] + (bm - 1)) // bm * bm
        end_s[e] = acc
    n_used = acc // bm
    n_used_ref[0] = n_used
    nxt = jnp.int32(-1)
    for e in reversed(range(N_EXPERTS)):
        nxt_s[e] = nxt
        nxt = jnp.where(cnt_ref[e, 0] > 0, e, nxt)

    def block(i, carry):
        prev_e, runs = carry
        row = jnp.minimum(i, n_used - 1) * bm
        e = lax.while_loop(lambda v: jnp.logical_and(v < N_EXPERTS - 1, end_s[v] <= row), lambda v: v + 1,
                           jnp.maximum(prev_e, 0))
        first = jnp.logical_and(i < n_used, prev_e != e)
        runs = runs + first.astype(I32)
        blk_e_ref[i] = e
        first_ref[i] = first.astype(I32)
        slot_ref[i] = (runs - 1) % 2
        next_ref[i] = nxt_s[e]
        return e, runs

    lax.fori_loop(0, n_blk, block, (jnp.int32(-1), jnp.int32(0)))

    rt = rt_ref[...]
    start_of = jnp.zeros_like(rt)
    for e in range(N_EXPERTS):
        start_of = jnp.where(rt == e, start_s[e], start_of)
    pos_ref[...] = start_of + pltpu.roll(rt, SUBLANES - 2, 0)


def _dispatch_plan(rt, cnt, bm):
    n_tok = rt.shape[1]
    n_rows = 2 * n_tok + N_EXPERTS * bm
    n_blk = n_rows // bm
    smem = pl.BlockSpec(memory_space=pltpu.SMEM)
    vec = lambda n: jax.ShapeDtypeStruct((n,), I32)
    pos, blk_e, n_used, first, slot, nxt = pl.pallas_call(
        functools.partial(_plan_kernel, bm=bm),
        in_specs=[smem, pl.BlockSpec(memory_space=pltpu.VMEM)],
        out_specs=[pl.BlockSpec(memory_space=pltpu.VMEM), smem, smem, smem, smem, smem],
        out_shape=[jax.ShapeDtypeStruct(rt.shape, I32), vec(n_blk), vec(1), vec(n_blk), vec(n_blk), vec(n_blk)],
        scratch_shapes=[pltpu.SMEM((N_EXPERTS,), I32)] * 3,
        name="moe_plan",
    )(cnt, rt)
    return pos, (blk_e, n_used, first, slot, nxt), n_rows


def _sc_mesh():
    return plsc.VectorSubcoreMesh(core_axis_name="c", subcore_axis_name="s")


def _sc_worker_base(per_worker):
    return (lax.axis_index("s") * SC_CORES + lax.axis_index("c")) * per_worker


def _sc_dispatch(hf, pos, n_rows):
    t, d = hf.shape
    per_w = t // SC_WORKERS
    ch = SC_CHUNK
    n_ck = per_w // ch
    assert per_w * SC_WORKERS == t and n_ck * ch == per_w

    @functools.partial(
        pl.kernel, mesh=_sc_mesh(), out_type=jax.ShapeDtypeStruct((n_rows, d), hf.dtype),
        scratch_types=[pltpu.VMEM((per_w,), I32), pltpu.VMEM((per_w,), I32), pltpu.VMEM((2, ch, d), hf.dtype),
                       pltpu.SemaphoreType.DMA((2,)), pltpu.SemaphoreType.DMA((2,)), pltpu.SemaphoreType.DMA((2,))])
    def dispatch(hf_hbm, p_hbm, out_hbm, i0_v, i1_v, rows_v, sem_in, sem_s0, sem_s1):
        base = pl.multiple_of(_sc_worker_base(per_w), SUBLANES)
        pltpu.sync_copy(p_hbm.at[pl.ds(base, per_w)], i0_v)
        pltpu.sync_copy(p_hbm.at[pl.ds(pl.multiple_of(t + base, SUBLANES), per_w)], i1_v)

        def load(ck):
            return pltpu.make_async_copy(hf_hbm.at[pl.ds(base + ck * ch, ch)], rows_v.at[ck % 2], sem_in.at[ck % 2])

        def scatters(ck):
            src = rows_v.at[ck % 2]
            return (pltpu.make_async_copy(src, out_hbm.at[i0_v.at[pl.ds(ck * ch, ch)]], sem_s0.at[ck % 2]),
                    pltpu.make_async_copy(src, out_hbm.at[i1_v.at[pl.ds(ck * ch, ch)]], sem_s1.at[ck % 2]))

        load(0).start()
        for ck in range(n_ck):
            load(ck).wait()
            if ck + 1 < n_ck:
                if ck >= 1:
                    for cp in scatters(ck - 1):
                        cp.wait()
                load(ck + 1).start()
            for cp in scatters(ck):
                cp.start()
        for ck in range(max(n_ck - 2, 0), n_ck):
            for cp in scatters(ck):
                cp.wait()

    return dispatch(hf, pos)


def _sc_gather(rows, idx):
    n = idx.shape[0]
    d = rows.shape[1]
    per_w = n // SC_WORKERS
    ch = 2 * SC_CHUNK
    n_ck = per_w // ch
    assert per_w * SC_WORKERS == n and n_ck * ch == per_w

    @functools.partial(
        pl.kernel, mesh=_sc_mesh(), out_type=jax.ShapeDtypeStruct((n, d), rows.dtype),
        scratch_types=[pltpu.VMEM((per_w,), I32), pltpu.VMEM((2, ch, d), rows.dtype),
                       pltpu.SemaphoreType.DMA((2,)), pltpu.SemaphoreType.DMA((2,))])
    def gather(rows_hbm, i_hbm, out_hbm, i_v, buf, sem_g, sem_w):
        base = pl.multiple_of(_sc_worker_base(per_w), SUBLANES)
        pltpu.sync_copy(i_hbm.at[pl.ds(base, per_w)], i_v)

        def fetch(ck):
            return pltpu.make_async_copy(rows_hbm.at[i_v.at[pl.ds(ck * ch, ch)]], buf.at[ck % 2], sem_g.at[ck % 2])

        def write(ck):
            return pltpu.make_async_copy(buf.at[ck % 2], out_hbm.at[pl.ds(base + ck * ch, ch)], sem_w.at[ck % 2])

        fetch(0).start()
        for ck in range(n_ck):
            fetch(ck).wait()
            if ck + 1 < n_ck:
                if ck >= 1:
                    write(ck - 1).wait()
                fetch(ck + 1).start()
            write(ck).start()
        for ck in range(max(n_ck - 2, 0), n_ck):
            write(ck).wait()

    return gather(rows, idx)


def _expert_kernel(blk_e_ref, n_used_ref, first_ref, slot_ref, next_ref, x_ref, wg_hbm, wu_hbm, wd_hbm, y_ref,
                   wgf, wuf, wdf, wgb, wub, wdb, sem, *, e_base):
    n_used = n_used_ref[0]
    bm = x_ref.shape[0] // MOE_SUB

    def weight_copies(e, slot):
        return (pltpu.make_async_copy(wg_hbm.at[e_base + e], wgf.at[slot], sem.at[slot, 0]),
                pltpu.make_async_copy(wu_hbm.at[e_base + e], wuf.at[slot], sem.at[slot, 1]),
                pltpu.make_async_copy(wd_hbm.at[e_base + e], wdf.at[slot], sem.at[slot, 2]))

    @pl.when(pl.program_id(0) == 0)
    def _():
        for cp in weight_copies(blk_e_ref[0], 0):
            cp.start()

    for j in range(MOE_SUB):
        blk = pl.program_id(0) * MOE_SUB + j
        rows = slice(j * bm, (j + 1) * bm)

        @pl.when(jnp.logical_and(blk < n_used, first_ref[blk] == 1))
        def _():
            slot = slot_ref[blk]
            for cp in weight_copies(blk_e_ref[blk], slot):
                cp.wait()
            nxt = next_ref[blk]

            @pl.when(nxt >= 0)
            def _():
                for cp in weight_copies(nxt, 1 - slot):
                    cp.start()

            wgb[...] = wgf[slot].astype(BF16)
            wub[...] = wuf[slot].astype(BF16)
            wdb[...] = wdf[slot].astype(BF16)

        @pl.when(blk < n_used)
        def _():
            x_hi, x_lo = _unpack_bf16_pairs(x_ref[rows, :])
            xb = jnp.concatenate([x_hi.astype(BF16), x_lo.astype(BF16)], axis=1)
            gt = jnp.dot(xb, wgb[...], preferred_element_type=F32)
            up = jnp.dot(xb, wub[...], preferred_element_type=F32)
            act = (gt * _sigmoid(gt) * up).astype(BF16)
            y_ref[rows, :] = _pack_bf16_pairs(jnp.dot(act, wdb[...], preferred_element_type=F32))


def _experts(x_rows, plan, w_gate, w_up, w_down, layer):
    n_rows, dp = x_rows.shape
    depth, n_e, d, de = w_gate.shape
    step_rows = MOE_SUB * MOE_BM
    assert n_rows % step_rows == 0
    any_spec = pl.BlockSpec(memory_space=pl.ANY)
    last_used = lambda i, be, nu, *_: (jnp.minimum(i, (nu[0] - 1) // MOE_SUB), 0)
    grid_spec = pltpu.PrefetchScalarGridSpec(
        num_scalar_prefetch=5,
        grid=(n_rows // step_rows,),
        in_specs=[pl.BlockSpec((step_rows, dp), last_used), any_spec, any_spec, any_spec],
        out_specs=pl.BlockSpec((step_rows, dp), last_used),
        scratch_shapes=[pltpu.VMEM((2, d, de), F32), pltpu.VMEM((2, d, de), F32), pltpu.VMEM((2, de, d), F32),
                        pltpu.VMEM((d, de), BF16), pltpu.VMEM((d, de), BF16), pltpu.VMEM((de, d), BF16),
                        pltpu.SemaphoreType.DMA((2, 3))],
    )
    return pl.pallas_call(
        functools.partial(_expert_kernel, e_base=layer * n_e),
        grid_spec=grid_spec,
        out_shape=jax.ShapeDtypeStruct((n_rows, dp), I32),
        compiler_params=_cp("arbitrary"),
        name="moe_experts",
    )(*plan, x_rows, w_gate.reshape(depth * n_e, d, de), w_up.reshape(depth * n_e, d, de),
      w_down.reshape(depth * n_e, de, d))


def _combine_kernel(x_ref, *rest):
    *pre_refs, o_ref = rest
    o_ref[...] = _combined_rows(x_ref, pre_refs)


def _combine(x, pre, s, b):
    nrows, d = x.shape
    pre_specs, pre_args = _pre_io(pre, d, TR, s, b)
    return pl.pallas_call(
        _combine_kernel,
        grid=(nrows // TR,),
        in_specs=[pl.BlockSpec((TR, d), lambda i: (i, 0))] + pre_specs,
        out_specs=pl.BlockSpec((TR, d), lambda i: (i, 0)),
        out_shape=jax.ShapeDtypeStruct((nrows, d), F32),
        compiler_params=_cp("parallel"),
        name="moe_combine",
    )(x, *pre_args)


def _expert_outputs(routed, w_gate, w_up, w_down, layer):
    hf, rt, _, cnt = routed
    pos, plan, n_rows = _dispatch_plan(rt, cnt, MOE_BM)
    pos = pos[0:2].reshape(-1)
    x_rows = _sc_dispatch(hf, pos, n_rows)
    y_rows = _experts(x_rows, plan, w_gate, w_up, w_down, layer)
    return _sc_gather(y_rows, pos)


def kernel(x, c, ctx, c_ctx, ada_w, ada_b, norm_mix_g, norm_ffn_g, rg_w_in, rg_conv_w, rg_conv_b, rg_wa, rg_ba, rg_wi, rg_bi, rg_lambda, rg_w_out, at_w_qkv, at_q_g, at_k_g, at_w_o, cm_w_in, cm_ln_g, cm_ln_b, cm_w_s, cm_b_s, cm_w_out, moe_w_group, moe_b_group, moe_w_router, moe_b_router, moe_w_gate, moe_w_up, moe_w_down):
    b, s, d = x.shape
    cl = ctx.shape[1]
    depth = ada_w.shape[0]
    n_lat = b * s
    assert b < SUBLANES and s % max(TI, TR, TG) == 0 and (b * cl) % max(TI, TR, TG) == 0 and cl % TM == 0
    assert d == RG_BLOCKS * LANES

    cin = jnp.concatenate([c, c_ctx[None, :], jnp.zeros((SUBLANES - b - 1, d), F32)], axis=0)
    mod_all = _ada_table(cin, ada_w, ada_b).reshape(depth, SUBLANES, 1, N_MOD * d)
    tok = jnp.concatenate([x.reshape(n_lat, d), ctx.reshape(b * cl, d)], axis=0)

    xa, pre, routed = tok, None, None
    for l in range(depth):
        kind = l % 3
        j = l // 3
        last = l == depth - 1
        mod = mod_all[l]
        g_mix = norm_mix_g[l].reshape(1, d)
        route = _route_params(norm_ffn_g[l], moe_w_group[l], moe_b_group[l], moe_w_router[l], moe_b_router[l])
        late = (lambda *ws: ws) if routed is None else (lambda *ws: lax.optimization_barrier((routed[1], *ws))[1:])
        if kind == 0:
            w_in, w_out = late(rg_w_in, rg_w_out)
            x1, routed = _rglru_mixer(xa, pre, g_mix, mod, w_in[j], rg_conv_w[j], rg_conv_b[j], rg_wa[j],
                                      rg_ba[j], rg_wi[j], rg_bi[j], rg_lambda[j], w_out[j], route, s, cl, b,
                                      n_lat if last else xa.shape[0])
        elif kind == 1:
            w_qkv, w_o = late(at_w_qkv, at_w_o)
            x1, routed = _attention_mixer(xa, pre, g_mix, mod, w_qkv[j], at_q_g[j], at_k_g[j], w_o[j],
                                          route, s, cl, b)
        else:
            w_in, w_out = late(cm_w_in, cm_w_out)
            x1, routed = _gmlp_mixer(xa, pre, g_mix, mod, w_in[j], cm_ln_g[j], cm_ln_b[j], cm_w_s[j],
                                     cm_b_s[j], w_out[j], route, s, b)
        y01 = _expert_outputs(routed, moe_w_gate, moe_w_up, moe_w_down, l)
        xa, pre = x1, (y01, routed[2], mod)
    return _combine(xa, pre, s, b)[:n_lat].reshape(b, s, d)
```

```python
import functools

import jax
import jax.numpy as jnp
from jax import lax
from jax.experimental import pallas as pl
from jax.experimental.pallas import tpu as pltpu
from jax.experimental.pallas import tpu_sc as plsc

F32 = jnp.float32
BF16 = jnp.bfloat16
I32 = jnp.int32
U32 = jnp.uint32

NORM_EPS = 1e-6
N_MOD = 6
GRID_W = 64
RG_BLOCKS = 8
CONV_W = 4
RG_C = 8.0
HEAD_DIM = 128
N_KV_HEADS = 2
GQA_GROUP = 4
ROPE_THETA = 10000.0
CHUNK = 128
CM_GROUPS = 8
N_GROUPS = 4
EXPERTS_PER_GROUP = 8
N_EXPERTS = N_GROUPS * EXPERTS_PER_GROUP

LANES = 128
SUBLANES = 8
TM = 256
TQ = 512
TI = 1024
TR = 1024
TG = 512
TL = 256
HALO = 8
ATT_KC = 512
LOG2E = 1.4426950408889634
MOE_BM = 256
MOE_SUB = 4
SC_CORES = 2
SC_WORKERS = 32
SC_CHUNK = 32
VMEM_LIMIT = 52 * 2**20


def _cp(*sem):
    return pltpu.CompilerParams(dimension_semantics=sem, vmem_limit_bytes=VMEM_LIMIT)


def _norm_mod(x, g, shift, scale):
    ms = jnp.mean(x * x, axis=-1, keepdims=True)
    y = x * lax.rsqrt(ms + NORM_EPS) * g
    return y * (1.0 + scale) + shift


def _sigmoid(x):
    return 0.5 * jnp.tanh(0.5 * x) + 0.5


def _pack_bf16_pairs(x):
    h = x.shape[-1] // 2
    hi = lax.bitcast_convert_type(x[:, :h].astype(BF16).astype(F32), U32)
    lo = lax.bitcast_convert_type(x[:, h:].astype(BF16).astype(F32), U32)
    return lax.bitcast_convert_type(hi | (lo >> 16), I32)


def _unpack_bf16_pairs(w):
    u = lax.bitcast_convert_type(w, U32)
    hi = lax.bitcast_convert_type(u & jnp.uint32(0xFFFF0000), F32)
    lo = lax.bitcast_convert_type(u << 16, F32)
    return hi, lo


def _combined_rows(x_ref, pre_refs):
    if not pre_refs:
        return x_ref[...]
    y0_ref, y1_ref, ew_ref, modp_ref = pre_refs
    d = x_ref.shape[1]
    ew = ew_ref[...]
    y0_hi, y0_lo = _unpack_bf16_pairs(y0_ref[...])
    y1_hi, y1_lo = _unpack_bf16_pairs(y1_ref[...])
    y = jnp.concatenate([ew[:, 0:1] * y0_hi + ew[:, 1:2] * y1_hi, ew[:, 0:1] * y0_lo + ew[:, 1:2] * y1_lo], axis=1)
    return x_ref[...] + modp_ref[0][:, 5 * d:6 * d] * y


def _pre_io(pre, d, tr, s, b):
    if pre is None:
        return [], []
    y01, ew, mod_prev = pre
    nb = y01.shape[0] // 2 // tr
    specs = [pl.BlockSpec((tr, d // 2), lambda i: (i, 0)), pl.BlockSpec((tr, d // 2), lambda i: (i + nb, 0)),
             pl.BlockSpec((tr, LANES), lambda i: (i, 0)), _mod_spec(d, s // tr, b)]
    return specs, [y01, y01, ew, mod_prev]


def _split_refs(refs, has_pre):
    return (refs[0], refs[1:5], refs[5:]) if has_pre else (refs[0], (), refs[1:])


def _mod_spec(d, rows_per_sample, n_samples):
    return pl.BlockSpec((1, 1, N_MOD * d),
                        lambda i, *_: (jnp.minimum(i // rows_per_sample, n_samples), 0, 0))


def _ada_kernel(c_ref, w_ref, b_ref, o_ref):
    cin = c_ref[...]
    act = cin * jax.nn.sigmoid(cin)
    w = w_ref[0]
    w_hi = w.astype(BF16)
    w_lo = (w - w_hi.astype(F32)).astype(BF16)
    a_hi = act.astype(BF16)
    a_lo = (act - a_hi.astype(F32)).astype(BF16)
    o_ref[0] = (jnp.dot(a_hi, w_hi, preferred_element_type=F32) + jnp.dot(a_lo, w_hi, preferred_element_type=F32)
                + jnp.dot(a_hi, w_lo, preferred_element_type=F32)) + b_ref[0]


def _ada_table(cin, ada_w, ada_b):
    depth, d, n = ada_w.shape
    tn = 2 * d
    return pl.pallas_call(
        _ada_kernel,
        grid=(depth, n // tn),
        in_specs=[pl.BlockSpec((SUBLANES, d), lambda l, j: (0, 0)),
                  pl.BlockSpec((1, d, tn), lambda l, j: (l, 0, j)),
                  pl.BlockSpec((1, 1, tn), lambda l, j: (l, 0, j))],
        out_specs=pl.BlockSpec((1, SUBLANES, tn), lambda l, j: (l, 0, j)),
        out_shape=jax.ShapeDtypeStruct((depth, SUBLANES, n), F32),
        compiler_params=_cp("parallel", "parallel"),
        name="ada_table",
    )(cin, ada_w, ada_b.reshape(depth, 1, n))


def _rows_io(x, tr):
    if not isinstance(x, tuple):
        return [pl.BlockSpec((tr, x.shape[1]), lambda i: (i, 0))], [x], 0
    lat, cx = x
    nl = lat.shape[0] // tr
    assert nl * tr == lat.shape[0] and cx.shape[0] % tr == 0
    specs = [pl.BlockSpec((tr, lat.shape[1]), lambda i: (jnp.minimum(i, nl - 1), 0)),
             pl.BlockSpec((tr, lat.shape[1]), lambda i: (jnp.maximum(i - nl, 0), 0))]
    return specs, [lat, cx], nl


def _rows_value(x_refs, n_lat_blocks):
    if not n_lat_blocks:
        return x_refs[0][...]
    return jnp.where(pl.program_id(0) < n_lat_blocks, x_refs[0][...], x_refs[1][...])


def _out_kernel(y_ref, *rest, n_lat_blocks):
    n_src = 2 if n_lat_blocks else 1
    x_refs, (mod_ref, w_ref, *rest) = rest[:n_src], rest[n_src:]
    route_in, (o_ref, *route_out) = rest[:3], rest[3:]
    d = o_ref.shape[-1]
    m = mod_ref[0]
    y = jnp.dot(y_ref[...].astype(BF16), w_ref[...], preferred_element_type=F32)
    x1 = _rows_value(x_refs, n_lat_blocks) + m[:, 2 * d:3 * d] * y
    o_ref[...] = x1
    _route_rows(x1, m, *route_in, *route_out)


def _out_proj(y, x, mod, w, route, nrows, s, b):
    k, d = w.shape
    r_in, r_out, r_shape, r_scratch = _route_io(d, nrows, TR)
    x_specs, x_args, n_lat_blocks = _rows_io(x, TR)
    x1, *routed = pl.pallas_call(
        functools.partial(_out_kernel, n_lat_blocks=n_lat_blocks),
        grid=(nrows // TR,),
        in_specs=[pl.BlockSpec((TR, k), lambda i: (i, 0))] + x_specs + [
            _mod_spec(d, s // TR, b), pl.BlockSpec((k, d), lambda i: (0, 0))] + r_in,
        out_specs=[pl.BlockSpec((TR, d), lambda i: (i, 0))] + r_out,
        out_shape=[jax.ShapeDtypeStruct((nrows, d), F32)] + r_shape,
        scratch_shapes=r_scratch,
        compiler_params=_cp("arbitrary"),
        name="out_proj",
    )(y, *x_args, mod, w, *route)
    return x1, routed


def _rg_in_kernel(*refs, has_pre, n_lat_blocks):
    if n_lat_blocks:
        x_ref, (g_ref, mod_ref, w_ref, *outs) = refs[0], refs[2:]
        x = _rows_value(refs[:2], n_lat_blocks)
    else:
        x_ref, pre_refs, (g_ref, mod_ref, w_ref, *outs) = _split_refs(refs, has_pre)
        x = _combined_rows(x_ref, pre_refs)
    gg_ref, xin_ref = outs[-2:]
    d = x_ref.shape[-1]
    m = mod_ref[0]
    if has_pre:
        outs[0][...] = x
    h = _norm_mod(x, g_ref[...], m[:, 0:d], m[:, d:2 * d])
    z = jnp.dot(h.astype(BF16), w_ref[...], preferred_element_type=F32)
    tm = x_ref.shape[0]
    for n in range(d // LANES):
        cols = slice(n * LANES, (n + 1) * LANES)
        gg_ref[pl.ds(n, tm, stride=SUBLANES), :] = jax.nn.gelu(z[:, cols])
        xin_ref[pl.ds(n, tm, stride=SUBLANES), :] = z[:, d + n * LANES:d + (n + 1) * LANES]


def _rg_in(x, pre, g, mod, w, s, b):
    d = w.shape[0]
    assert d == SUBLANES * LANES
    x_specs, x_args, n_lat_blocks = _rows_io(x, TI)
    assert not (n_lat_blocks and pre)
    t = sum(a.shape[0] for a in x_args)
    pre_specs, pre_args = _pre_io(pre, d, TI, s, b)
    row = pl.BlockSpec((TI, d), lambda i: (i, 0))
    tmajor = pl.BlockSpec((TI * SUBLANES, LANES), lambda i: (i, 0))
    outs = pl.pallas_call(
        functools.partial(_rg_in_kernel, has_pre=pre is not None, n_lat_blocks=n_lat_blocks),
        grid=(t // TI,),
        in_specs=x_specs + pre_specs + [pl.BlockSpec((1, d), lambda i: (0, 0)), _mod_spec(d, s // TI, b),
                                        pl.BlockSpec((d, 2 * d), lambda i: (0, 0))],
        out_specs=([row] if pre else []) + [tmajor, tmajor],
        out_shape=([jax.ShapeDtypeStruct((t, d), F32)] if pre else [])
        + [jax.ShapeDtypeStruct((t * SUBLANES, LANES), F32)] * 2,
        compiler_params=_cp("parallel"),
        name="rg_in",
    )(*x_args, *pre_args, g, mod, w)
    return (outs[0], outs[1], outs[2]) if pre else (x, outs[0], outs[1])


def _rg_gates_and_scan(xc, wa_ref, wi_ref, ba_ref, bi_ref, lam_ref, a_s, b_s, h_dst, hcar, reverse):
    @pl.when(pl.program_id(1) == 0)
    def _():
        hcar[...] = jnp.zeros_like(hcar)

    for n in range(RG_BLOCKS):
        cols = slice(n * LANES, (n + 1) * LANES)
        xn = xc[pl.ds(n, TL, stride=SUBLANES), :]
        xb = xn.astype(BF16)
        ta = jnp.tanh(jnp.dot(xb, wa_ref[n], preferred_element_type=F32) + ba_ref[:, cols])
        ti = jnp.tanh(jnp.dot(xb, wi_ref[n], preferred_element_type=F32) + bi_ref[:, cols])
        k = (-0.5 * RG_C * LOG2E) * jax.nn.softplus(-lam_ref[:, cols])
        a = jnp.exp2(k * ta + k)
        om = 1.0 - a * a
        root = jnp.where(om > 0.0, om * lax.rsqrt(om), 0.0)
        a_s[pl.ds(n, TL, stride=SUBLANES), :] = a
        b_s[pl.ds(n, TL, stride=SUBLANES), :] = root * (0.5 * xn) * (ti + 1.0)

    def two_steps(p, h):
        t0 = (TL - 1 - 2 * p) if reverse else 2 * p
        t1 = (t0 - 1) if reverse else (t0 + 1)
        r0 = pl.multiple_of(t0 * SUBLANES, SUBLANES)
        r1 = pl.multiple_of(t1 * SUBLANES, SUBLANES)
        a0 = a_s[pl.ds(r0, SUBLANES), :]
        b0 = b_s[pl.ds(r0, SUBLANES), :]
        a1 = a_s[pl.ds(r1, SUBLANES), :]
        b1 = b_s[pl.ds(r1, SUBLANES), :]
        h_dst[pl.ds(r0, SUBLANES), :] = a0 * h + b0
        h2 = (a1 * a0) * h + (a1 * b0 + b1)
        h_dst[pl.ds(r1, SUBLANES), :] = h2
        return h2

    hcar[...] = lax.fori_loop(0, TL // 2, two_steps, hcar[...], unroll=8)


def _rg_fwd_kernel(xm_ref, xprev_ref, xnext_ref, cw_ref, cb_ref, wa_ref, wi_ref, ba_ref, bi_ref, lam_ref,
                   hf_ref, xc_ref, xpad, a_s, b_s, hcar, *, nlat):
    rows = TL * SUBLANES
    hrows = HALO * SUBLANES
    j = pl.program_id(1)
    has_prev = j >= 2
    has_next = jnp.logical_and(j >= 1, j < nlat)
    xpad[0:hrows, :] = jnp.where(has_prev, xprev_ref[...], 0.0)
    xpad[hrows:hrows + rows, :] = xm_ref[...]
    xpad[hrows + rows:2 * hrows + rows, :] = jnp.where(has_next, xnext_ref[...], 0.0)
    acc = jnp.broadcast_to(cb_ref[...][None], (TL, SUBLANES, LANES))
    for k in range(CONV_W):
        off = (HALO + k - CONV_W // 2) * SUBLANES
        tap = xpad[off:off + rows, :].reshape(TL, SUBLANES, LANES)
        acc = acc + tap * cw_ref[k][None]
    xc_ref[...] = acc.reshape(rows, LANES)
    _rg_gates_and_scan(xc_ref, wa_ref, wi_ref, ba_ref, bi_ref, lam_ref, a_s, b_s, hf_ref, hcar, False)


def _rg_bwd_kernel(xc_ref, wa_ref, wi_ref, ba_ref, bi_ref, lam_ref, hf_ref, gg_ref, out_ref, a_s, b_s, h_s, hcar):
    _rg_gates_and_scan(xc_ref, wa_ref, wi_ref, ba_ref, bi_ref, lam_ref, a_s, b_s, h_s, hcar, True)
    h_s[...] = gg_ref[...] * (hf_ref[...] + h_s[...])
    for n in range(RG_BLOCKS):
        out_ref[:, n * LANES:(n + 1) * LANES] = h_s[pl.ds(n, TL, stride=SUBLANES), :].astype(BF16)


def _rg_scans(xin8, gg8, conv_w, conv_b, wa, wi, ba, bi, lam, s, c, b):
    assert c == TL and s % TL == 0
    rows = TL * SUBLANES
    hrows = HALO * SUBLANES
    nlat = s // TL
    t = xin8.shape[0] // SUBLANES
    n_halo = t // HALO
    d = RG_BLOCKS * LANES

    def chunk(reverse):
        return lambda bi_, j: jnp.where(j == 0, (b * s) // TL + bi_,
                                        bi_ * nlat + ((nlat - j) if reverse else (j - 1)))

    fwd, bwd = chunk(False), chunk(True)
    main_f = pl.BlockSpec((rows, LANES), lambda bi_, j: (fwd(bi_, j), 0))
    main_b = pl.BlockSpec((rows, LANES), lambda bi_, j: (bwd(bi_, j), 0))
    prev = pl.BlockSpec((hrows, LANES), lambda bi_, j: (jnp.maximum(fwd(bi_, j) * (TL // HALO) - 1, 0), 0))
    nxt = pl.BlockSpec((hrows, LANES),
                       lambda bi_, j: (jnp.minimum((fwd(bi_, j) + 1) * (TL // HALO), n_halo - 1), 0))
    full = lambda shape: pl.BlockSpec(shape, lambda bi_, j: (0,) * len(shape))
    gate_specs = [full((RG_BLOCKS, LANES, LANES)), full((RG_BLOCKS, LANES, LANES)),
                  full((1, d)), full((1, d)), full((1, d))]
    gate_args = lambda k: [(0.5 * wa[k]).astype(BF16), (0.5 * wi[k]).astype(BF16), 0.5 * ba[k].reshape(1, d),
                           0.5 * bi[k].reshape(1, d), lam[k].reshape(1, d)]
    tmajor = jax.ShapeDtypeStruct(xin8.shape, F32)
    buf = pltpu.VMEM((rows, LANES), F32)
    hf8, xc8 = pl.pallas_call(
        functools.partial(_rg_fwd_kernel, nlat=nlat),
        grid=(b, nlat + 1),
        in_specs=[main_f, prev, nxt, full((CONV_W, SUBLANES, LANES)), full((SUBLANES, LANES))] + gate_specs,
        out_specs=[main_f, main_f],
        out_shape=[tmajor, tmajor],
        scratch_shapes=[pltpu.VMEM((rows + 2 * hrows, LANES), F32), buf, buf, pltpu.VMEM((SUBLANES, LANES), F32)],
        compiler_params=_cp("parallel", "arbitrary"),
        name="rg_scan_fwd",
    )(xin8, xin8, xin8, conv_w.reshape(CONV_W, SUBLANES, LANES), conv_b.reshape(SUBLANES, LANES), *gate_args(0))
    return pl.pallas_call(
        _rg_bwd_kernel,
        grid=(b, nlat + 1),
        in_specs=[main_b] + gate_specs + [main_b, main_b],
        out_specs=pl.BlockSpec((TL, d), lambda bi_, j: (bwd(bi_, j), 0)),
        out_shape=jax.ShapeDtypeStruct((t, d), BF16),
        scratch_shapes=[buf, buf, buf, pltpu.VMEM((SUBLANES, LANES), F32)],
        compiler_params=_cp("parallel", "arbitrary"),
        name="rg_scan_bwd",
    )(xc8, *gate_args(1), hf8, gg8)


def _rglru_mixer(xa, pre, g, mod, w_in, conv_w, conv_b, wa, ba, wi, bi, lam, w_out, route, s, c, b, nrows_out):
    x, gg8, xin8 = _rg_in(xa, pre, g, mod, w_in.astype(BF16), s, b)
    y = _rg_scans(xin8, gg8, conv_w, conv_b, wa, wi, ba, bi, lam, s, c, b)
    return _out_proj(y, x, mod, w_out.astype(BF16), route, nrows_out, s, b)


def _rope_tables(s):
    pos = jnp.arange(s, dtype=F32)
    row = jnp.floor(pos / GRID_W)
    col = pos - row * GRID_W
    n_freq = HEAD_DIM // 4
    inv = ROPE_THETA ** (-jnp.arange(n_freq, dtype=F32) * 2.0 / (HEAD_DIM // 2))
    ar = row[:, None] * inv
    ac = col[:, None] * inv
    cos = jnp.concatenate([jnp.cos(ar), jnp.cos(ar), jnp.cos(ac), jnp.cos(ac)], axis=1)
    sin = jnp.concatenate([-jnp.sin(ar), jnp.sin(ar), -jnp.sin(ac), jnp.sin(ac)], axis=1)
    cos = jnp.concatenate([cos, jnp.ones((TM, HEAD_DIM), F32)], axis=0)
    sin = jnp.concatenate([sin, jnp.zeros((TM, HEAD_DIM), F32)], axis=0)
    return cos, sin


def _qkv_kernel(*refs, has_pre):
    x_ref, pre_refs, (g_ref, mod_ref, w_ref, qg_ref, kg_ref, cos_ref, sin_ref, *outs) = _split_refs(refs, has_pre)
    q_ref, k_ref, v_ref = outs[-3:]
    d = x_ref.shape[-1]
    m = mod_ref[0]
    x = _combined_rows(x_ref, pre_refs)
    if has_pre:
        outs[0][...] = x
    h = _norm_mod(x, g_ref[...], m[:, 0:d], m[:, d:2 * d])
    z = jnp.dot(h.astype(BF16), w_ref[...], preferred_element_type=F32)
    cos = cos_ref[...]
    sin = sin_ref[...]
    src = lax.broadcasted_iota(I32, (HEAD_DIM, HEAD_DIM), 0)
    dst = lax.broadcasted_iota(I32, (HEAD_DIM, HEAD_DIM), 1)
    quarter = HEAD_DIM // 4
    partner_of = jnp.where((dst % (2 * quarter)) < quarter, dst + quarter, dst - quarter)
    swap = jnp.where(src == partner_of, 1.0, 0.0).astype(BF16)

    def head(zc, gain):
        ms = jnp.mean(zc * zc, axis=-1, keepdims=True)
        y = zc * lax.rsqrt(ms + NORM_EPS) * gain
        partner = jnp.dot(y.astype(BF16), swap, preferred_element_type=F32)
        return y * cos + partner * sin

    nq = q_ref.shape[-1] // HEAD_DIM
    nk = k_ref.shape[-1] // HEAD_DIM
    for j in range(nq):
        q_ref[:, j * HEAD_DIM:(j + 1) * HEAD_DIM] = (
            head(z[:, j * HEAD_DIM:(j + 1) * HEAD_DIM], qg_ref[...]) * (HEAD_DIM ** -0.5 * LOG2E)).astype(BF16)
    for j in range(nk):
        c0 = (nq + j) * HEAD_DIM
        k_ref[:, j * HEAD_DIM:(j + 1) * HEAD_DIM] = head(z[:, c0:c0 + HEAD_DIM], kg_ref[...]).astype(BF16)
    v_ref[...] = z[:, (nq + nk) * HEAD_DIM:].astype(BF16)


def _qkv(x, pre, g, mod, w, qg, kg, cos, sin, s, b):
    t, d = x.shape
    nkv = N_KV_HEADS * HEAD_DIM
    n_pos = s // TM
    pre_specs, pre_args = _pre_io(pre, d, TM, s, b)
    row = pl.BlockSpec((TM, d), lambda i: (i, 0))
    outs = pl.pallas_call(
        functools.partial(_qkv_kernel, has_pre=pre is not None),
        grid=(t // TM,),
        in_specs=[row] + pre_specs + [
            pl.BlockSpec((1, d), lambda i: (0, 0)),
            _mod_spec(d, s // TM, b),
            pl.BlockSpec(w.shape, lambda i: (0, 0)),
            pl.BlockSpec((1, HEAD_DIM), lambda i: (0, 0)),
            pl.BlockSpec((1, HEAD_DIM), lambda i: (0, 0)),
            pl.BlockSpec((TM, HEAD_DIM), lambda i: (jnp.where(i < b * n_pos, i % n_pos, n_pos), 0)),
            pl.BlockSpec((TM, HEAD_DIM), lambda i: (jnp.where(i < b * n_pos, i % n_pos, n_pos), 0))],
        out_specs=([row] if pre else []) + [pl.BlockSpec((TM, d), lambda i: (i, 0)),
                                           pl.BlockSpec((TM, nkv), lambda i: (i, 0)),
                                           pl.BlockSpec((TM, nkv), lambda i: (i, 0))],
        out_shape=([jax.ShapeDtypeStruct((t, d), F32)] if pre else [])
        + [jax.ShapeDtypeStruct((t, d), BF16), jax.ShapeDtypeStruct((t, nkv), BF16),
           jax.ShapeDtypeStruct((t, nkv), BF16)],
        compiler_params=_cp("parallel"),
        name="qkv_proj",
    )(x, *pre_args, g, mod, w, qg, kg, cos, sin)
    return tuple(outs) if pre else (x, *outs)


def _attn_kernel(q_ref, kc_ref, vc_ref, *rest, n_lat):
    if n_lat:
        kl_ref, vl_ref, o_ref, s_scr, vaug = rest
    else:
        o_ref, s_scr, vaug = rest
    n_ctx = kc_ref.shape[0]
    tq = q_ref.shape[0]

    def fill_values():
        vaug[:, HEAD_DIM:] = jnp.ones((n_ctx + n_lat, HEAD_DIM), BF16)
        vaug[0:n_ctx, 0:HEAD_DIM] = vc_ref[...]
        if n_lat:
            vaug[n_ctx:, 0:HEAD_DIM] = vl_ref[...]

    if n_lat:
        pl.when(pl.program_id(2) == 0)(fill_values)
    else:
        fill_values()

    chunks = [(0, n_ctx)] + [(n_ctx + j, ATT_KC) for j in range(0, n_lat, ATT_KC)]
    nt = (((1,), (1,)), ((), ()))
    q_all = jnp.concatenate([q_ref[:, g * HEAD_DIM:(g + 1) * HEAD_DIM] for g in range(GQA_GROUP)], axis=0)
    m_part = jnp.full((GQA_GROUP * tq, LANES), -jnp.inf, F32)
    for off, size in chunks:
        keys = kc_ref[...] if off == 0 else kl_ref[off - n_ctx:off - n_ctx + size, :]
        sc = lax.dot_general(q_all, keys, nt, preferred_element_type=F32)
        s_scr[:, off:off + size] = sc
        for j in range(0, size, LANES):
            m_part = jnp.maximum(m_part, sc[:, j:j + LANES])
    m_row = jnp.max(m_part, axis=-1, keepdims=True)
    hr = GQA_GROUP * tq // 2
    acc = [jnp.zeros((hr, 2 * HEAD_DIM), F32), jnp.zeros((hr, 2 * HEAD_DIM), F32)]
    for off, size in chunks:
        for r in range(2):
            rows = slice(r * hr, (r + 1) * hr)
            p = jnp.exp2((s_scr[rows, off:off + size] - m_row[rows]).astype(BF16))
            acc[r] = acc[r] + jnp.dot(p, vaug[off:off + size, :], preferred_element_type=F32)
    for r in range(2):
        out = (acc[r][:, :HEAD_DIM] / acc[r][:, HEAD_DIM:]).astype(BF16)
        for j in range(GQA_GROUP // 2):
            g = r * (GQA_GROUP // 2) + j
            o_ref[:, g * HEAD_DIM:(g + 1) * HEAD_DIM] = out[j * tq:(j + 1) * tq]


def _attn_ctx_kernel(q_ref, kc_ref, vc_ref, o_all_ref, o_ref, s_scr, vaug):
    del o_all_ref
    _attn_kernel(q_ref, kc_ref, vc_ref, o_ref, s_scr, vaug, n_lat=0)


def _attention(q, k, v, s, c, b):
    t, d = q.shape
    gw = GQA_GROUP * HEAD_DIM
    tq = TQ
    nq = s // tq
    assert s % ATT_KC == 0
    ctx_blk = lambda bi, h, *_: ((b * s) // c + bi, h)
    o_lat = pl.pallas_call(
        functools.partial(_attn_kernel, n_lat=s),
        grid=(b, N_KV_HEADS, nq),
        in_specs=[pl.BlockSpec((tq, gw), lambda bi, h, i: (bi * nq + i, h)),
                  pl.BlockSpec((c, HEAD_DIM), ctx_blk),
                  pl.BlockSpec((c, HEAD_DIM), ctx_blk),
                  pl.BlockSpec((s, HEAD_DIM), lambda bi, h, i: (bi, h)),
                  pl.BlockSpec((s, HEAD_DIM), lambda bi, h, i: (bi, h))],
        out_specs=pl.BlockSpec((tq, gw), lambda bi, h, i: (bi * nq + i, h)),
        out_shape=jax.ShapeDtypeStruct((t, d), BF16),
        scratch_shapes=[pltpu.VMEM((GQA_GROUP * tq, c + s), F32), pltpu.VMEM((c + s, 2 * HEAD_DIM), BF16)],
        compiler_params=_cp("parallel", "parallel", "arbitrary"),
        name="attn_lat",
    )(q, k, v, k, v)
    return pl.pallas_call(
        _attn_ctx_kernel,
        grid=(b, N_KV_HEADS),
        in_specs=[pl.BlockSpec((c, gw), ctx_blk),
                  pl.BlockSpec((c, HEAD_DIM), ctx_blk),
                  pl.BlockSpec((c, HEAD_DIM), ctx_blk),
                  pl.BlockSpec(memory_space=pl.ANY)],
        out_specs=pl.BlockSpec((c, gw), ctx_blk),
        out_shape=jax.ShapeDtypeStruct((t, d), BF16),
        scratch_shapes=[pltpu.VMEM((GQA_GROUP * c, c), F32), pltpu.VMEM((c, 2 * HEAD_DIM), BF16)],
        input_output_aliases={3: 0},
        compiler_params=_cp("parallel", "parallel"),
        name="attn_ctx",
    )(q, k, v, o_lat)


def _attention_mixer(xa, pre, g, mod, w_qkv, qg, kg, w_o, route, s, c, b):
    cos, sin = _rope_tables(s)
    x, q, k, v = _qkv(xa, pre, g, mod, w_qkv.astype(BF16), qg.reshape(1, -1), kg.reshape(1, -1), cos, sin, s, b)
    o = _attention(q, k, v, s, c, b)
    return _out_proj(o, x, mod, w_o.astype(BF16), route, x.shape[0], s, b)


def _gmlp_kernel(*refs, has_pre):
    x_ref, pre_refs, rest = _split_refs(refs, has_pre)
    g_ref, mod_ref, w_in_ref, lng_ref, lnb_ref, ws_ref, bs_ref, w_out_ref = rest[:8]
    route_in, (o_ref, *route_out, uv_ref, cnt_s) = rest[8:11], rest[11:]
    d = x_ref.shape[-1]
    dcm = lng_ref.shape[-1]
    gw = dcm // CM_GROUPS
    x = _combined_rows(x_ref, pre_refs)
    m = mod_ref[0]
    h = _norm_mod(x, g_ref[...], m[:, 0:d], m[:, d:2 * d])
    z = jax.nn.gelu(jnp.dot(h.astype(BF16), w_in_ref[...], preferred_element_type=F32))
    u = z[:, :dcm]
    v = z[:, dcm:]
    mu = jnp.mean(v, axis=-1, keepdims=True)
    vc = v - mu
    var = jnp.mean(vc * vc, axis=-1, keepdims=True)
    vn = (vc * lax.rsqrt(var + NORM_EPS) * lng_ref[...] + lnb_ref[...]).astype(BF16)
    for ck in range(x.shape[0] // CHUNK):
        rows = slice(ck * CHUNK, (ck + 1) * CHUNK)
        for gi in range(CM_GROUPS):
            cols = slice(gi * gw, (gi + 1) * gw)
            mix = jnp.dot(ws_ref[gi], vn[rows, cols], preferred_element_type=F32) + bs_ref[:, gi:gi + 1]
            uv_ref[rows, cols] = (u[rows, cols] * mix).astype(BF16)
    y = jnp.dot(uv_ref[...], w_out_ref[...], preferred_element_type=F32)
    x1 = x + m[:, 2 * d:3 * d] * y
    o_ref[...] = x1
    _route_rows(x1, m, *route_in, *route_out, cnt_s)


def _gmlp_mixer(xa, pre, g, mod, w_in, ln_g, ln_b, w_s, b_s, w_out, route, s, b):
    t, d = xa.shape
    dcm = ln_g.shape[-1]
    full = lambda shape: pl.BlockSpec(shape, lambda i: (0,) * len(shape))
    pre_specs, pre_args = _pre_io(pre, d, TG, s, b)
    r_in, r_out, r_shape, r_scratch = _route_io(d, t, TG)
    x1, *routed = pl.pallas_call(
        functools.partial(_gmlp_kernel, has_pre=pre is not None),
        grid=(t // TG,),
        in_specs=[pl.BlockSpec((TG, d), lambda i: (i, 0))] + pre_specs + [
            full((1, d)),
            _mod_spec(d, s // TG, b),
            full((d, 2 * dcm)), full((1, dcm)), full((1, dcm)),
            full((CM_GROUPS, CHUNK, CHUNK)), full((CHUNK, CM_GROUPS)), full((dcm, d))] + r_in,
        out_specs=[pl.BlockSpec((TG, d), lambda i: (i, 0))] + r_out,
        out_shape=[jax.ShapeDtypeStruct((t, d), F32)] + r_shape,
        scratch_shapes=[pltpu.VMEM((TG, dcm), BF16)] + r_scratch,
        compiler_params=_cp("arbitrary"),
        name="gmlp",
    )(xa, *pre_args, g, mod, w_in.astype(BF16), ln_g.reshape(1, dcm), ln_b.reshape(1, dcm),
      w_s.astype(BF16), b_s.T, w_out.astype(BF16), *route)
    return x1, routed


def _route_rows(x1, m, g_ref, wr_ref, br_ref, hf_ref, rt_ref, ew_ref, cnt_ref, cnt_s):
    tm, d = x1.shape
    ng = EXPERTS_PER_GROUP

    @pl.when(pl.program_id(0) == 0)
    def _():
        cnt_s[...] = jnp.zeros_like(cnt_s)

    hf = _norm_mod(x1, g_ref[...], m[:, 3 * d:4 * d], m[:, 4 * d:5 * d])
    hf_ref[...] = _pack_bf16_pairs(hf)
    hf_hi = hf.astype(BF16)
    hf_lo = (hf - hf_hi.astype(F32)).astype(BF16)
    nt = (((1,), (1,)), ((), ()))
    by_hi = lax.dot_general(wr_ref[...], hf_hi, nt, preferred_element_type=F32)
    logits = (by_hi[:LANES] + by_hi[LANES:]
              + lax.dot_general(wr_ref[0:LANES, :], hf_lo, nt, preferred_element_type=F32)) + br_ref[...]
    neg = -jnp.inf
    row = lax.broadcasted_iota(I32, (ng, tm), 0)
    grp = logits[N_EXPERTS:N_EXPERTS + ng]
    gmax = jnp.max(grp, axis=0, keepdims=True)
    gsel = jnp.min(jnp.where(grp == gmax, row, ng), axis=0, keepdims=True)
    gate_g = 1.0 / jnp.sum(jnp.exp(grp - gmax), axis=0, keepdims=True)
    el = logits[0:ng]
    for gi in range(1, N_GROUPS):
        el = jnp.where(gsel == gi, logits[gi * ng:(gi + 1) * ng], el)
    v1 = jnp.max(el, axis=0, keepdims=True)
    i1 = jnp.min(jnp.where(el == v1, row, ng), axis=0, keepdims=True)
    el2 = jnp.where(row == i1, neg, el)
    v2 = jnp.max(el2, axis=0, keepdims=True)
    i2 = jnp.min(jnp.where(el2 == v2, row, ng), axis=0, keepdims=True)
    e21 = jnp.exp(v2 - v1)
    w1 = gate_g / (1.0 + e21)
    w2 = w1 * e21
    e1 = gsel * ng + i1
    e2 = gsel * ng + i2
    lane_row = lax.broadcasted_iota(I32, (LANES, tm), 0)
    ew_ref[...] = jnp.where(lane_row == 0, w1, jnp.where(lane_row == 1, w2, 0.0)).T

    expert = lax.broadcasted_iota(I32, (N_EXPERTS, tm), 0)
    oh1 = expert == e1
    oh2 = expert == e2
    earlier = (lax.broadcasted_iota(I32, (tm, tm), 0) < lax.broadcasted_iota(I32, (tm, tm), 1)).astype(BF16)
    pre1 = jnp.dot(jnp.where(oh1, 1.0, 0.0).astype(BF16), earlier, preferred_element_type=F32)
    pre2 = jnp.dot(jnp.where(oh2, 1.0, 0.0).astype(BF16), earlier, preferred_element_type=F32)
    tot1 = jnp.sum(jnp.where(oh1, 1.0, 0.0), axis=1, keepdims=True)
    tot2 = jnp.sum(jnp.where(oh2, 1.0, 0.0), axis=1, keepdims=True)
    cnt = cnt_s[:, 0:1]
    rank1 = jnp.sum(jnp.where(oh1, cnt + pre1, 0.0), axis=0, keepdims=True).astype(I32)
    rank2 = jnp.sum(jnp.where(oh2, cnt + tot1 + pre2, 0.0), axis=0, keepdims=True).astype(I32)
    cnt = jnp.broadcast_to(cnt + tot1 + tot2, cnt_s.shape)
    cnt_s[...] = cnt
    cnt_ref[...] = cnt.astype(I32)
    rt_ref[...] = jnp.where(row == 0, e1, jnp.where(row == 1, e2, jnp.where(row == 2, rank1,
                            jnp.where(row == 3, rank2, 0))))


def _route_params(g_ffn, w_group, b_group, w_router, b_router):
    d = w_group.shape[0]
    pad = LANES - N_EXPERTS - N_GROUPS
    wr = jnp.concatenate([w_router.reshape(d, N_EXPERTS), w_group, jnp.zeros((d, pad), F32)], axis=1).T
    br = jnp.concatenate([b_router.reshape(N_EXPERTS), b_group,
                          jnp.full((EXPERTS_PER_GROUP - N_GROUPS,), -jnp.inf, F32),
                          jnp.zeros((pad - EXPERTS_PER_GROUP + N_GROUPS,), F32)]).reshape(LANES, 1)
    wr_hi = wr.astype(BF16)
    wr_lo = (wr - wr_hi.astype(F32)).astype(BF16)
    return g_ffn.reshape(1, d), jnp.concatenate([wr_hi, wr_lo], axis=0), br


def _route_io(d, nrows, tr):
    const = lambda shape: pl.BlockSpec(shape, lambda i: (0, 0))
    row = lambda w: pl.BlockSpec((tr, w), lambda i: (i, 0))
    in_specs = [const((1, d)), const((2 * LANES, d)), const((LANES, 1))]
    out_specs = [row(d // 2), pl.BlockSpec((SUBLANES, tr), lambda i: (0, i)), row(LANES), const((N_EXPERTS, LANES))]
    out_shape = [jax.ShapeDtypeStruct((nrows, d // 2), I32), jax.ShapeDtypeStruct((SUBLANES, nrows), I32),
                 jax.ShapeDtypeStruct((nrows, LANES), F32), jax.ShapeDtypeStruct((N_EXPERTS, LANES), I32)]
    return in_specs, out_specs, out_shape, [pltpu.VMEM((N_EXPERTS, LANES), F32)]


def _plan_kernel(cnt_ref, rt_ref, pos_ref, blk_e_ref, n_used_ref, first_ref, slot_ref, next_ref,
                 start_s, end_s, nxt_s, *, bm):
    n_blk = blk_e_ref.shape[0]
    acc = jnp.int32(0)
    for e in range(N_EXPERTS):
        start_s[e] = acc
        acc = acc + (cnt_ref[e, 0] + (bm - 1)) // bm * bm
        end_s[e] = acc
    n_used = acc // bm
    n_used_ref[0] = n_used
    nxt = jnp.int32(-1)
    for e in reversed(range(N_EXPERTS)):
        nxt_s[e] = nxt
        nxt = jnp.where(cnt_ref[e, 0] > 0, e, nxt)

    def block(i, carry):
        prev_e, runs = carry
        row = jnp.minimum(i, n_used - 1) * bm
        e = lax.while_loop(lambda v: jnp.logical_and(v < N_EXPERTS - 1, end_s[v] <= row), lambda v: v + 1,
                           jnp.maximum(prev_e, 0))
        first = jnp.logical_and(i < n_used, prev_e != e)
        runs = runs + first.astype(I32)
        blk_e_ref[i] = e
        first_ref[i] = first.astype(I32)
        slot_ref[i] = (runs - 1) % 2
        next_ref[i] = nxt_s[e]
        return e, runs

    lax.fori_loop(0, n_blk, block, (jnp.int32(-1), jnp.int32(0)))

    rt = rt_ref[...]
    start_of = jnp.zeros_like(rt)
    for e in range(N_EXPERTS):
        start_of = jnp.where(rt == e, start_s[e], start_of)
    pos_ref[...] = start_of + pltpu.roll(rt, SUBLANES - 2, 0)


def _dispatch_plan(rt, cnt, bm):
    n_tok = rt.shape[1]
    n_rows = 2 * n_tok + N_EXPERTS * bm
    n_blk = n_rows // bm
    smem = pl.BlockSpec(memory_space=pltpu.SMEM)
    vec = lambda n: jax.ShapeDtypeStruct((n,), I32)
    pos, blk_e, n_used, first, slot, nxt = pl.pallas_call(
        functools.partial(_plan_kernel, bm=bm),
        in_specs=[smem, pl.BlockSpec(memory_space=pltpu.VMEM)],
        out_specs=[pl.BlockSpec(memory_space=pltpu.VMEM), smem, smem, smem, smem, smem],
        out_shape=[jax.ShapeDtypeStruct(rt.shape, I32), vec(n_blk), vec(1), vec(n_blk), vec(n_blk), vec(n_blk)],
        scratch_shapes=[pltpu.SMEM((N_EXPERTS,), I32)] * 3,
        name="moe_plan",
    )(cnt, rt)
    return pos, (blk_e, n_used, first, slot, nxt), n_rows


def _sc_mesh():
    return plsc.VectorSubcoreMesh(core_axis_name="c", subcore_axis_name="s")


def _sc_worker_base(per_worker):
    return (lax.axis_index("s") * SC_CORES + lax.axis_index("c")) * per_worker


def _sc_dispatch(hf, pos, n_rows):
    t, d = hf.shape
    per_w = t // SC_WORKERS
    ch = SC_CHUNK
    n_ck = per_w // ch
    assert per_w * SC_WORKERS == t and n_ck * ch == per_w

    @functools.partial(
        pl.kernel, mesh=_sc_mesh(), out_type=jax.ShapeDtypeStruct((n_rows, d), hf.dtype),
        scratch_types=[pltpu.VMEM((per_w,), I32), pltpu.VMEM((per_w,), I32), pltpu.VMEM((2, ch, d), hf.dtype),
                       pltpu.SemaphoreType.DMA((2,)), pltpu.SemaphoreType.DMA((2,)), pltpu.SemaphoreType.DMA((2,))])
    def dispatch(hf_hbm, p_hbm, out_hbm, i0_v, i1_v, rows_v, sem_in, sem_s0, sem_s1):
        base = pl.multiple_of(_sc_worker_base(per_w), SUBLANES)
        pltpu.sync_copy(p_hbm.at[pl.ds(base, per_w)], i0_v)
        pltpu.sync_copy(p_hbm.at[pl.ds(pl.multiple_of(t + base, SUBLANES), per_w)], i1_v)

        def load(ck):
            return pltpu.make_async_copy(hf_hbm.at[pl.ds(base + ck * ch, ch)], rows_v.at[ck % 2], sem_in.at[ck % 2])

        def scatters(ck):
            src = rows_v.at[ck % 2]
            return (pltpu.make_async_copy(src, out_hbm.at[i0_v.at[pl.ds(ck * ch, ch)]], sem_s0.at[ck % 2]),
                    pltpu.make_async_copy(src, out_hbm.at[i1_v.at[pl.ds(ck * ch, ch)]], sem_s1.at[ck % 2]))

        load(0).start()
        for ck in range(n_ck):
            load(ck).wait()
            if ck + 1 < n_ck:
                if ck >= 1:
                    for cp in scatters(ck - 1):
                        cp.wait()
                load(ck + 1).start()
            for cp in scatters(ck):
                cp.start()
        for ck in range(max(n_ck - 2, 0), n_ck):
            for cp in scatters(ck):
                cp.wait()

    return dispatch(hf, pos)


def _sc_gather(rows, idx):
    n = idx.shape[0]
    d = rows.shape[1]
    per_w = n // SC_WORKERS
    ch = 2 * SC_CHUNK
    n_ck = per_w // ch
    assert per_w * SC_WORKERS == n and n_ck * ch == per_w

    @functools.partial(
        pl.kernel, mesh=_sc_mesh(), out_type=jax.ShapeDtypeStruct((n, d), rows.dtype),
        scratch_types=[pltpu.VMEM((per_w,), I32), pltpu.VMEM((2, ch, d), rows.dtype),
                       pltpu.SemaphoreType.DMA((2,)), pltpu.SemaphoreType.DMA((2,))])
    def gather(rows_hbm, i_hbm, out_hbm, i_v, buf, sem_g, sem_w):
        base = pl.multiple_of(_sc_worker_base(per_w), SUBLANES)
        pltpu.sync_copy(i_hbm.at[pl.ds(base, per_w)], i_v)

        def fetch(ck):
            return pltpu.make_async_copy(rows_hbm.at[i_v.at[pl.ds(ck * ch, ch)]], buf.at[ck % 2], sem_g.at[ck % 2])

        def write(ck):
            return pltpu.make_async_copy(buf.at[ck % 2], out_hbm.at[pl.ds(base + ck * ch, ch)], sem_w.at[ck % 2])

        fetch(0).start()
        for ck in range(n_ck):
            fetch(ck).wait()
            if ck + 1 < n_ck:
                if ck >= 1:
                    write(ck - 1).wait()
                fetch(ck + 1).start()
            write(ck).start()
        for ck in range(max(n_ck - 2, 0), n_ck):
            write(ck).wait()

    return gather(rows, idx)


def _expert_kernel(blk_e_ref, n_used_ref, first_ref, slot_ref, next_ref, x_ref, wg_hbm, wu_hbm, wd_hbm, y_ref,
                   wgf, wuf, wdf, wgb, wub, wdb, sem, *, e_base):
    n_used = n_used_ref[0]
    bm = x_ref.shape[0] // MOE_SUB

    def weight_copies(e, slot):
        return (pltpu.make_async_copy(wg_hbm.at[e_base + e], wgf.at[slot], sem.at[slot, 0]),
                pltpu.make_async_copy(wu_hbm.at[e_base + e], wuf.at[slot], sem.at[slot, 1]),
                pltpu.make_async_copy(wd_hbm.at[e_base + e], wdf.at[slot], sem.at[slot, 2]))

    @pl.when(pl.program_id(0) == 0)
    def _():
        for cp in weight_copies(blk_e_ref[0], 0):
            cp.start()

    for j in range(MOE_SUB):
        blk = pl.program_id(0) * MOE_SUB + j
        rows = slice(j * bm, (j + 1) * bm)

        @pl.when(jnp.logical_and(blk < n_used, first_ref[blk] == 1))
        def _():
            slot = slot_ref[blk]
            for cp in weight_copies(blk_e_ref[blk], slot):
                cp.wait()
            nxt = next_ref[blk]

            @pl.when(nxt >= 0)
            def _():
                for cp in weight_copies(nxt, 1 - slot):
                    cp.start()

            wgb[...] = wgf[slot].astype(BF16)
            wub[...] = wuf[slot].astype(BF16)
            wdb[...] = wdf[slot].astype(BF16)

        @pl.when(blk < n_used)
        def _():
            x_hi, x_lo = _unpack_bf16_pairs(x_ref[rows, :])
            xb = jnp.concatenate([x_hi.astype(BF16), x_lo.astype(BF16)], axis=1)
            gt = jnp.dot(xb, wgb[...], preferred_element_type=F32)
            up = jnp.dot(xb, wub[...], preferred_element_type=F32)
            act = (gt * _sigmoid(gt) * up).astype(BF16)
            y_ref[rows, :] = _pack_bf16_pairs(jnp.dot(act, wdb[...], preferred_element_type=F32))


def _experts(x_rows, plan, w_gate, w_up, w_down, layer):
    n_rows, dp = x_rows.shape
    depth, n_e, d, de = w_gate.shape
    step_rows = MOE_SUB * MOE_BM
    assert n_rows % step_rows == 0
    any_spec = pl.BlockSpec(memory_space=pl.ANY)
    last_used = lambda i, be, nu, *_: (jnp.minimum(i, (nu[0] - 1) // MOE_SUB), 0)
    grid_spec = pltpu.PrefetchScalarGridSpec(
        num_scalar_prefetch=5,
        grid=(n_rows // step_rows,),
        in_specs=[pl.BlockSpec((step_rows, dp), last_used), any_spec, any_spec, any_spec],
        out_specs=pl.BlockSpec((step_rows, dp), last_used),
        scratch_shapes=[pltpu.VMEM((2, d, de), F32), pltpu.VMEM((2, d, de), F32), pltpu.VMEM((2, de, d), F32),
                        pltpu.VMEM((d, de), BF16), pltpu.VMEM((d, de), BF16), pltpu.VMEM((de, d), BF16),
                        pltpu.SemaphoreType.DMA((2, 3))],
    )
    return pl.pallas_call(
        functools.partial(_expert_kernel, e_base=layer * n_e),
        grid_spec=grid_spec,
        out_shape=jax.ShapeDtypeStruct((n_rows, dp), I32),
        compiler_params=_cp("arbitrary"),
        name="moe_experts",
    )(*plan, x_rows, w_gate.reshape(depth * n_e, d, de), w_up.reshape(depth * n_e, d, de),
      w_down.reshape(depth * n_e, de, d))


def _combine_kernel(x_ref, *rest):
    *pre_refs, o_ref = rest
    o_ref[...] = _combined_rows(x_ref, pre_refs)


def _combine(x, pre, s, b):
    nrows, d = x.shape
    pre_specs, pre_args = _pre_io(pre, d, TR, s, b)
    return pl.pallas_call(
        _combine_kernel,
        grid=(nrows // TR,),
        in_specs=[pl.BlockSpec((TR, d), lambda i: (i, 0))] + pre_specs,
        out_specs=pl.BlockSpec((TR, d), lambda i: (i, 0)),
        out_shape=jax.ShapeDtypeStruct((nrows, d), F32),
        compiler_params=_cp("parallel"),
        name="moe_combine",
    )(x, *pre_args)


def _expert_outputs(routed, w_gate, w_up, w_down, layer):
    hf, rt, _, cnt = routed
    pos, plan, n_rows = _dispatch_plan(rt, cnt, MOE_BM)
    pos = pos[0:2].reshape(-1)
    x_rows = _sc_dispatch(hf, pos, n_rows)
    y_rows = _experts(x_rows, plan, w_gate, w_up, w_down, layer)
    return _sc_gather(y_rows, pos)


def kernel(x, c, ctx, c_ctx, ada_w, ada_b, norm_mix_g, norm_ffn_g, rg_w_in, rg_conv_w, rg_conv_b, rg_wa, rg_ba, rg_wi, rg_bi, rg_lambda, rg_w_out, at_w_qkv, at_q_g, at_k_g, at_w_o, cm_w_in, cm_ln_g, cm_ln_b, cm_w_s, cm_b_s, cm_w_out, moe_w_group, moe_b_group, moe_w_router, moe_b_router, moe_w_gate, moe_w_up, moe_w_down):
    b, s, d = x.shape
    cl = ctx.shape[1]
    depth = ada_w.shape[0]
    n_lat = b * s
    assert b < SUBLANES and s % max(TI, TR, TG) == 0 and (b * cl) % max(TI, TR, TG) == 0 and cl % TM == 0
    assert d == RG_BLOCKS * LANES

    cin = jnp.concatenate([c, c_ctx[None, :], jnp.zeros((SUBLANES - b - 1, d), F32)], axis=0)
    mod_all = _ada_table(cin, ada_w, ada_b).reshape(depth, SUBLANES, 1, N_MOD * d)
    n_tok = n_lat + b * cl

    xa, pre = (x.reshape(n_lat, d), ctx.reshape(b * cl, d)), None
    for l in range(depth):
        kind = l % 3
        j = l // 3
        last = l == depth - 1
        mod = mod_all[l]
        g_mix = norm_mix_g[l].reshape(1, d)
        route = _route_params(norm_ffn_g[l], moe_w_group[l], moe_b_group[l], moe_w_router[l], moe_b_router[l])
        if kind == 0:
            x1, routed = _rglru_mixer(xa, pre, g_mix, mod, rg_w_in[j], rg_conv_w[j], rg_conv_b[j], rg_wa[j],
                                      rg_ba[j], rg_wi[j], rg_bi[j], rg_lambda[j], rg_w_out[j], route, s, cl, b,
                                      n_lat if last else n_tok)
        elif kind == 1:
            x1, routed = _attention_mixer(xa, pre, g_mix, mod, at_w_qkv[j], at_q_g[j], at_k_g[j], at_w_o[j],
                                          route, s, cl, b)
        else:
            x1, routed = _gmlp_mixer(xa, pre, g_mix, mod, cm_w_in[j], cm_ln_g[j], cm_ln_b[j], cm_w_s[j],
                                     cm_b_s[j], cm_w_out[j], route, s, b)
        y01 = _expert_outputs(routed, moe_w_gate, moe_w_up, moe_w_down, l)
        xa, pre = x1, (y01, routed[2], mod)
    return _combine(xa, pre, s, b)[:n_lat].reshape(b, s, d)
```

```python
import functools

import jax
import jax.numpy as jnp
from jax import lax
from jax.experimental import pallas as pl
from jax.experimental.pallas import tpu as pltpu
from jax.experimental.pallas import tpu_sc as plsc

F32 = jnp.float32
BF16 = jnp.bfloat16
I32 = jnp.int32
U32 = jnp.uint32

NORM_EPS = 1e-6
N_MOD = 6
GRID_W = 64
RG_BLOCKS = 8
CONV_W = 4
RG_C = 8.0
HEAD_DIM = 128
N_KV_HEADS = 2
GQA_GROUP = 4
ROPE_THETA = 10000.0
CHUNK = 128
CM_GROUPS = 8
N_GROUPS = 4
EXPERTS_PER_GROUP = 8
N_EXPERTS = N_GROUPS * EXPERTS_PER_GROUP

LANES = 128
SUBLANES = 8
TM = 256
TQ = 512
TI = 1024
TR = 1024
TG = 512
CAST_ROWS = 256
TL = 256
HALO = 8
ATT_KC = 512
LOG2E = 1.4426950408889634
MOE_BM = 256
MOE_SUB = 4
SC_CORES = 2
SC_WORKERS = 32
SC_CHUNK = 32
VMEM_LIMIT = 52 * 2**20


def _cp(*sem):
    return pltpu.CompilerParams(dimension_semantics=sem, vmem_limit_bytes=VMEM_LIMIT)


def _norm_mod(x, g, shift, scale):
    ms = jnp.mean(x * x, axis=-1, keepdims=True)
    y = x * lax.rsqrt(ms + NORM_EPS) * g
    return y * (1.0 + scale) + shift


def _sigmoid(x):
    return 0.5 * jnp.tanh(0.5 * x) + 0.5


def _pack_bf16_pairs(x):
    h = x.shape[-1] // 2
    hi = lax.bitcast_convert_type(x[:, :h].astype(BF16).astype(F32), U32)
    lo = lax.bitcast_convert_type(x[:, h:].astype(BF16).astype(F32), U32)
    return lax.bitcast_convert_type(hi | (lo >> 16), I32)


def _unpack_bf16_pairs(w):
    u = lax.bitcast_convert_type(w, U32)
    hi = lax.bitcast_convert_type(u & jnp.uint32(0xFFFF0000), F32)
    lo = lax.bitcast_convert_type(u << 16, F32)
    return hi, lo


def _combined_rows(x_ref, pre_refs):
    if not pre_refs:
        return x_ref[...]
    y0_ref, y1_ref, ew_ref, modp_ref = pre_refs
    d = x_ref.shape[1]
    ew = ew_ref[...]
    y0_hi, y0_lo = _unpack_bf16_pairs(y0_ref[...])
    y1_hi, y1_lo = _unpack_bf16_pairs(y1_ref[...])
    y = jnp.concatenate([ew[:, 0:1] * y0_hi + ew[:, 1:2] * y1_hi, ew[:, 0:1] * y0_lo + ew[:, 1:2] * y1_lo], axis=1)
    return x_ref[...] + modp_ref[0][:, 5 * d:6 * d] * y


def _pre_io(pre, d, tr, s, b):
    if pre is None:
        return [], []
    y01, ew, mod_prev = pre
    nb = y01.shape[0] // 2 // tr
    specs = [pl.BlockSpec((tr, d // 2), lambda i: (i, 0)), pl.BlockSpec((tr, d // 2), lambda i: (i + nb, 0)),
             pl.BlockSpec((tr, LANES), lambda i: (i, 0)), _mod_spec(d, s // tr, b)]
    return specs, [y01, y01, ew, mod_prev]


def _split_refs(refs, has_pre):
    return (refs[0], refs[1:5], refs[5:]) if has_pre else (refs[0], (), refs[1:])


def _mod_spec(d, rows_per_sample, n_samples):
    return pl.BlockSpec((1, 1, N_MOD * d),
                        lambda i, *_: (jnp.minimum(i // rows_per_sample, n_samples), 0, 0))


def _cast_kernel(w_ref, o_ref):
    o_ref[...] = w_ref[0].astype(BF16)


def _bf16_weight(w_stack, j):
    _, r, c = w_stack.shape
    return pl.pallas_call(
        _cast_kernel,
        grid=(r // CAST_ROWS,),
        in_specs=[pl.BlockSpec((1, CAST_ROWS, c), lambda i: (j, i, 0))],
        out_specs=pl.BlockSpec((CAST_ROWS, c), lambda i: (i, 0)),
        out_shape=jax.ShapeDtypeStruct((r, c), BF16),
        compiler_params=_cp("parallel"),
        name="weight_bf16",
    )(w_stack)


def _ada_kernel(c_ref, w_ref, b_ref, o_ref):
    cin = c_ref[...]
    act = cin * jax.nn.sigmoid(cin)
    w = w_ref[0]
    w_hi = w.astype(BF16)
    w_lo = (w - w_hi.astype(F32)).astype(BF16)
    a_hi = act.astype(BF16)
    a_lo = (act - a_hi.astype(F32)).astype(BF16)
    o_ref[0] = (jnp.dot(a_hi, w_hi, preferred_element_type=F32) + jnp.dot(a_lo, w_hi, preferred_element_type=F32)
                + jnp.dot(a_hi, w_lo, preferred_element_type=F32)) + b_ref[0]


def _ada_table(cin, ada_w, ada_b):
    depth, d, n = ada_w.shape
    tn = 2 * d
    return pl.pallas_call(
        _ada_kernel,
        grid=(depth, n // tn),
        in_specs=[pl.BlockSpec((SUBLANES, d), lambda l, j: (0, 0)),
                  pl.BlockSpec((1, d, tn), lambda l, j: (l, 0, j)),
                  pl.BlockSpec((1, 1, tn), lambda l, j: (l, 0, j))],
        out_specs=pl.BlockSpec((1, SUBLANES, tn), lambda l, j: (l, 0, j)),
        out_shape=jax.ShapeDtypeStruct((depth, SUBLANES, n), F32),
        compiler_params=_cp("parallel", "parallel"),
        name="ada_table",
    )(cin, ada_w, ada_b.reshape(depth, 1, n))


def _rows_io(x, tr):
    if not isinstance(x, tuple):
        return [pl.BlockSpec((tr, x.shape[1]), lambda i: (i, 0))], [x], 0
    lat, cx = x
    nl = lat.shape[0] // tr
    assert nl * tr == lat.shape[0] and cx.shape[0] % tr == 0
    specs = [pl.BlockSpec((tr, lat.shape[1]), lambda i: (jnp.minimum(i, nl - 1), 0)),
             pl.BlockSpec((tr, lat.shape[1]), lambda i: (jnp.maximum(i - nl, 0), 0))]
    return specs, [lat, cx], nl


def _rows_value(x_refs, n_lat_blocks):
    if not n_lat_blocks:
        return x_refs[0][...]
    return jnp.where(pl.program_id(0) < n_lat_blocks, x_refs[0][...], x_refs[1][...])


def _out_kernel(y_ref, *rest, n_lat_blocks):
    n_src = 2 if n_lat_blocks else 1
    x_refs, (mod_ref, w_ref, *rest) = rest[:n_src], rest[n_src:]
    route_in, (o_ref, *route_out) = rest[:3], rest[3:]
    d = o_ref.shape[-1]
    m = mod_ref[0]
    y = jnp.dot(y_ref[...].astype(BF16), w_ref[...], preferred_element_type=F32)
    x1 = _rows_value(x_refs, n_lat_blocks) + m[:, 2 * d:3 * d] * y
    o_ref[...] = x1
    _route_rows(x1, m, *route_in, *route_out)


def _out_proj(y, x, mod, w, route, nrows, s, b):
    k, d = w.shape
    r_in, r_out, r_shape, r_scratch = _route_io(d, nrows, TR)
    x_specs, x_args, n_lat_blocks = _rows_io(x, TR)
    x1, *routed = pl.pallas_call(
        functools.partial(_out_kernel, n_lat_blocks=n_lat_blocks),
        grid=(nrows // TR,),
        in_specs=[pl.BlockSpec((TR, k), lambda i: (i, 0))] + x_specs + [
            _mod_spec(d, s // TR, b), pl.BlockSpec((k, d), lambda i: (0, 0))] + r_in,
        out_specs=[pl.BlockSpec((TR, d), lambda i: (i, 0))] + r_out,
        out_shape=[jax.ShapeDtypeStruct((nrows, d), F32)] + r_shape,
        scratch_shapes=r_scratch,
        compiler_params=_cp("arbitrary"),
        name="out_proj",
    )(y, *x_args, mod, w, *route)
    return x1, routed


def _rg_in_kernel(*refs, has_pre, n_lat_blocks):
    if n_lat_blocks:
        x_ref, (g_ref, mod_ref, w_ref, *outs) = refs[0], refs[2:]
        x = _rows_value(refs[:2], n_lat_blocks)
    else:
        x_ref, pre_refs, (g_ref, mod_ref, w_ref, *outs) = _split_refs(refs, has_pre)
        x = _combined_rows(x_ref, pre_refs)
    gg_ref, xin_ref = outs[-2:]
    d = x_ref.shape[-1]
    m = mod_ref[0]
    if has_pre:
        outs[0][...] = x
    h = _norm_mod(x, g_ref[...], m[:, 0:d], m[:, d:2 * d])
    z = jnp.dot(h.astype(BF16), w_ref[...], preferred_element_type=F32)
    tm = x_ref.shape[0]
    for n in range(d // LANES):
        cols = slice(n * LANES, (n + 1) * LANES)
        gg_ref[pl.ds(n, tm, stride=SUBLANES), :] = jax.nn.gelu(z[:, cols])
        xin_ref[pl.ds(n, tm, stride=SUBLANES), :] = z[:, d + n * LANES:d + (n + 1) * LANES]


def _rg_in(x, pre, g, mod, w, s, b):
    d = w.shape[0]
    assert d == SUBLANES * LANES
    x_specs, x_args, n_lat_blocks = _rows_io(x, TI)
    assert not (n_lat_blocks and pre)
    t = sum(a.shape[0] for a in x_args)
    pre_specs, pre_args = _pre_io(pre, d, TI, s, b)
    row = pl.BlockSpec((TI, d), lambda i: (i, 0))
    tmajor = pl.BlockSpec((TI * SUBLANES, LANES), lambda i: (i, 0))
    outs = pl.pallas_call(
        functools.partial(_rg_in_kernel, has_pre=pre is not None, n_lat_blocks=n_lat_blocks),
        grid=(t // TI,),
        in_specs=x_specs + pre_specs + [pl.BlockSpec((1, d), lambda i: (0, 0)), _mod_spec(d, s // TI, b),
                                        pl.BlockSpec((d, 2 * d), lambda i: (0, 0))],
        out_specs=([row] if pre else []) + [tmajor, tmajor],
        out_shape=([jax.ShapeDtypeStruct((t, d), F32)] if pre else [])
        + [jax.ShapeDtypeStruct((t * SUBLANES, LANES), F32)] * 2,
        compiler_params=_cp("parallel"),
        name="rg_in",
    )(*x_args, *pre_args, g, mod, w)
    return (outs[0], outs[1], outs[2]) if pre else (x, outs[0], outs[1])


def _rg_gates_and_scan(xc, wa_ref, wi_ref, ba_ref, bi_ref, lam_ref, a_s, b_s, h_dst, hcar, reverse):
    @pl.when(pl.program_id(1) == 0)
    def _():
        hcar[...] = jnp.zeros_like(hcar)

    for n in range(RG_BLOCKS):
        cols = slice(n * LANES, (n + 1) * LANES)
        xn = xc[pl.ds(n, TL, stride=SUBLANES), :]
        xb = xn.astype(BF16)
        ta = jnp.tanh(jnp.dot(xb, wa_ref[n], preferred_element_type=F32) + ba_ref[:, cols])
        ti = jnp.tanh(jnp.dot(xb, wi_ref[n], preferred_element_type=F32) + bi_ref[:, cols])
        k = (-0.5 * RG_C * LOG2E) * jax.nn.softplus(-lam_ref[:, cols])
        a = jnp.exp2(k * ta + k)
        om = 1.0 - a * a
        root = jnp.where(om > 0.0, om * lax.rsqrt(om), 0.0)
        a_s[pl.ds(n, TL, stride=SUBLANES), :] = a
        b_s[pl.ds(n, TL, stride=SUBLANES), :] = root * (0.5 * xn) * (ti + 1.0)

    def two_steps(p, h):
        t0 = (TL - 1 - 2 * p) if reverse else 2 * p
        t1 = (t0 - 1) if reverse else (t0 + 1)
        r0 = pl.multiple_of(t0 * SUBLANES, SUBLANES)
        r1 = pl.multiple_of(t1 * SUBLANES, SUBLANES)
        a0 = a_s[pl.ds(r0, SUBLANES), :]
        b0 = b_s[pl.ds(r0, SUBLANES), :]
        a1 = a_s[pl.ds(r1, SUBLANES), :]
        b1 = b_s[pl.ds(r1, SUBLANES), :]
        h_dst[pl.ds(r0, SUBLANES), :] = a0 * h + b0
        h2 = (a1 * a0) * h + (a1 * b0 + b1)
        h_dst[pl.ds(r1, SUBLANES), :] = h2
        return h2

    hcar[...] = lax.fori_loop(0, TL // 2, two_steps, hcar[...], unroll=8)


def _rg_fwd_kernel(xm_ref, xprev_ref, xnext_ref, cw_ref, cb_ref, wa_ref, wi_ref, ba_ref, bi_ref, lam_ref,
                   hf_ref, xc_ref, xpad, a_s, b_s, hcar, *, nlat):
    rows = TL * SUBLANES
    hrows = HALO * SUBLANES
    j = pl.program_id(1)
    has_prev = j >= 2
    has_next = jnp.logical_and(j >= 1, j < nlat)
    xpad[0:hrows, :] = jnp.where(has_prev, xprev_ref[...], 0.0)
    xpad[hrows:hrows + rows, :] = xm_ref[...]
    xpad[hrows + rows:2 * hrows + rows, :] = jnp.where(has_next, xnext_ref[...], 0.0)
    acc = jnp.broadcast_to(cb_ref[...][None], (TL, SUBLANES, LANES))
    for k in range(CONV_W):
        off = (HALO + k - CONV_W // 2) * SUBLANES
        tap = xpad[off:off + rows, :].reshape(TL, SUBLANES, LANES)
        acc = acc + tap * cw_ref[k][None]
    xc_ref[...] = acc.reshape(rows, LANES)
    _rg_gates_and_scan(xc_ref, wa_ref, wi_ref, ba_ref, bi_ref, lam_ref, a_s, b_s, hf_ref, hcar, False)


def _rg_bwd_kernel(xc_ref, wa_ref, wi_ref, ba_ref, bi_ref, lam_ref, hf_ref, gg_ref, out_ref, a_s, b_s, h_s, hcar):
    _rg_gates_and_scan(xc_ref, wa_ref, wi_ref, ba_ref, bi_ref, lam_ref, a_s, b_s, h_s, hcar, True)
    h_s[...] = gg_ref[...] * (hf_ref[...] + h_s[...])
    for n in range(RG_BLOCKS):
        out_ref[:, n * LANES:(n + 1) * LANES] = h_s[pl.ds(n, TL, stride=SUBLANES), :].astype(BF16)


def _rg_scans(xin8, gg8, conv_w, conv_b, wa, wi, ba, bi, lam, s, c, b):
    assert c == TL and s % TL == 0
    rows = TL * SUBLANES
    hrows = HALO * SUBLANES
    nlat = s // TL
    t = xin8.shape[0] // SUBLANES
    n_halo = t // HALO
    d = RG_BLOCKS * LANES

    def chunk(reverse):
        return lambda bi_, j: jnp.where(j == 0, (b * s) // TL + bi_,
                                        bi_ * nlat + ((nlat - j) if reverse else (j - 1)))

    fwd, bwd = chunk(False), chunk(True)
    main_f = pl.BlockSpec((rows, LANES), lambda bi_, j: (fwd(bi_, j), 0))
    main_b = pl.BlockSpec((rows, LANES), lambda bi_, j: (bwd(bi_, j), 0))
    prev = pl.BlockSpec((hrows, LANES), lambda bi_, j: (jnp.maximum(fwd(bi_, j) * (TL // HALO) - 1, 0), 0))
    nxt = pl.BlockSpec((hrows, LANES),
                       lambda bi_, j: (jnp.minimum((fwd(bi_, j) + 1) * (TL // HALO), n_halo - 1), 0))
    full = lambda shape: pl.BlockSpec(shape, lambda bi_, j: (0,) * len(shape))
    gate_specs = [full((RG_BLOCKS, LANES, LANES)), full((RG_BLOCKS, LANES, LANES)),
                  full((1, d)), full((1, d)), full((1, d))]
    gate_args = lambda k: [(0.5 * wa[k]).astype(BF16), (0.5 * wi[k]).astype(BF16), 0.5 * ba[k].reshape(1, d),
                           0.5 * bi[k].reshape(1, d), lam[k].reshape(1, d)]
    tmajor = jax.ShapeDtypeStruct(xin8.shape, F32)
    buf = pltpu.VMEM((rows, LANES), F32)
    hf8, xc8 = pl.pallas_call(
        functools.partial(_rg_fwd_kernel, nlat=nlat),
        grid=(b, nlat + 1),
        in_specs=[main_f, prev, nxt, full((CONV_W, SUBLANES, LANES)), full((SUBLANES, LANES))] + gate_specs,
        out_specs=[main_f, main_f],
        out_shape=[tmajor, tmajor],
        scratch_shapes=[pltpu.VMEM((rows + 2 * hrows, LANES), F32), buf, buf, pltpu.VMEM((SUBLANES, LANES), F32)],
        compiler_params=_cp("parallel", "arbitrary"),
        name="rg_scan_fwd",
    )(xin8, xin8, xin8, conv_w.reshape(CONV_W, SUBLANES, LANES), conv_b.reshape(SUBLANES, LANES), *gate_args(0))
    return pl.pallas_call(
        _rg_bwd_kernel,
        grid=(b, nlat + 1),
        in_specs=[main_b] + gate_specs + [main_b, main_b],
        out_specs=pl.BlockSpec((TL, d), lambda bi_, j: (bwd(bi_, j), 0)),
        out_shape=jax.ShapeDtypeStruct((t, d), BF16),
        scratch_shapes=[buf, buf, buf, pltpu.VMEM((SUBLANES, LANES), F32)],
        compiler_params=_cp("parallel", "arbitrary"),
        name="rg_scan_bwd",
    )(xc8, *gate_args(1), hf8, gg8)


def _rglru_mixer(xa, pre, g, mod, w_in, conv_w, conv_b, wa, ba, wi, bi, lam, w_out, route, s, c, b, nrows_out):
    x, gg8, xin8 = _rg_in(xa, pre, g, mod, w_in, s, b)
    y = _rg_scans(xin8, gg8, conv_w, conv_b, wa, wi, ba, bi, lam, s, c, b)
    return _out_proj(y, x, mod, w_out, route, nrows_out, s, b)


def _rope_tables(s):
    pos = jnp.arange(s, dtype=F32)
    row = jnp.floor(pos / GRID_W)
    col = pos - row * GRID_W
    n_freq = HEAD_DIM // 4
    inv = ROPE_THETA ** (-jnp.arange(n_freq, dtype=F32) * 2.0 / (HEAD_DIM // 2))
    ar = row[:, None] * inv
    ac = col[:, None] * inv
    cos = jnp.concatenate([jnp.cos(ar), jnp.cos(ar), jnp.cos(ac), jnp.cos(ac)], axis=1)
    sin = jnp.concatenate([-jnp.sin(ar), jnp.sin(ar), -jnp.sin(ac), jnp.sin(ac)], axis=1)
    cos = jnp.concatenate([cos, jnp.ones((TM, HEAD_DIM), F32)], axis=0)
    sin = jnp.concatenate([sin, jnp.zeros((TM, HEAD_DIM), F32)], axis=0)
    return cos, sin


def _qkv_kernel(*refs, has_pre):
    x_ref, pre_refs, (g_ref, mod_ref, w_ref, qg_ref, kg_ref, cos_ref, sin_ref, *outs) = _split_refs(refs, has_pre)
    q_ref, k_ref, v_ref = outs[-3:]
    d = x_ref.shape[-1]
    m = mod_ref[0]
    x = _combined_rows(x_ref, pre_refs)
    if has_pre:
        outs[0][...] = x
    h = _norm_mod(x, g_ref[...], m[:, 0:d], m[:, d:2 * d])
    z = jnp.dot(h.astype(BF16), w_ref[...], preferred_element_type=F32)
    cos = cos_ref[...]
    sin = sin_ref[...]
    src = lax.broadcasted_iota(I32, (HEAD_DIM, HEAD_DIM), 0)
    dst = lax.broadcasted_iota(I32, (HEAD_DIM, HEAD_DIM), 1)
    quarter = HEAD_DIM // 4
    partner_of = jnp.where((dst % (2 * quarter)) < quarter, dst + quarter, dst - quarter)
    swap = jnp.where(src == partner_of, 1.0, 0.0).astype(BF16)

    def head(zc, gain):
        ms = jnp.mean(zc * zc, axis=-1, keepdims=True)
        y = zc * lax.rsqrt(ms + NORM_EPS) * gain
        partner = jnp.dot(y.astype(BF16), swap, preferred_element_type=F32)
        return y * cos + partner * sin

    nq = q_ref.shape[-1] // HEAD_DIM
    nk = k_ref.shape[-1] // HEAD_DIM
    for j in range(nq):
        q_ref[:, j * HEAD_DIM:(j + 1) * HEAD_DIM] = (
            head(z[:, j * HEAD_DIM:(j + 1) * HEAD_DIM], qg_ref[...]) * (HEAD_DIM ** -0.5 * LOG2E)).astype(BF16)
    for j in range(nk):
        c0 = (nq + j) * HEAD_DIM
        k_ref[:, j * HEAD_DIM:(j + 1) * HEAD_DIM] = head(z[:, c0:c0 + HEAD_DIM], kg_ref[...]).astype(BF16)
    v_ref[...] = z[:, (nq + nk) * HEAD_DIM:].astype(BF16)


def _qkv(x, pre, g, mod, w, qg, kg, cos, sin, s, b):
    t, d = x.shape
    nkv = N_KV_HEADS * HEAD_DIM
    n_pos = s // TM
    pre_specs, pre_args = _pre_io(pre, d, TM, s, b)
    row = pl.BlockSpec((TM, d), lambda i: (i, 0))
    outs = pl.pallas_call(
        functools.partial(_qkv_kernel, has_pre=pre is not None),
        grid=(t // TM,),
        in_specs=[row] + pre_specs + [
            pl.BlockSpec((1, d), lambda i: (0, 0)),
            _mod_spec(d, s // TM, b),
            pl.BlockSpec(w.shape, lambda i: (0, 0)),
            pl.BlockSpec((1, HEAD_DIM), lambda i: (0, 0)),
            pl.BlockSpec((1, HEAD_DIM), lambda i: (0, 0)),
            pl.BlockSpec((TM, HEAD_DIM), lambda i: (jnp.where(i < b * n_pos, i % n_pos, n_pos), 0)),
            pl.BlockSpec((TM, HEAD_DIM), lambda i: (jnp.where(i < b * n_pos, i % n_pos, n_pos), 0))],
        out_specs=([row] if pre else []) + [pl.BlockSpec((TM, d), lambda i: (i, 0)),
                                           pl.BlockSpec((TM, nkv), lambda i: (i, 0)),
                                           pl.BlockSpec((TM, nkv), lambda i: (i, 0))],
        out_shape=([jax.ShapeDtypeStruct((t, d), F32)] if pre else [])
        + [jax.ShapeDtypeStruct((t, d), BF16), jax.ShapeDtypeStruct((t, nkv), BF16),
           jax.ShapeDtypeStruct((t, nkv), BF16)],
        compiler_params=_cp("parallel"),
        name="qkv_proj",
    )(x, *pre_args, g, mod, w, qg, kg, cos, sin)
    return tuple(outs) if pre else (x, *outs)


def _attn_kernel(q_ref, kc_ref, vc_ref, *rest, n_lat):
    if n_lat:
        kl_ref, vl_ref, o_ref, s_scr, vaug = rest
    else:
        o_ref, s_scr, vaug = rest
    n_ctx = kc_ref.shape[0]
    tq = q_ref.shape[0]

    def fill_values():
        vaug[:, HEAD_DIM:] = jnp.ones((n_ctx + n_lat, HEAD_DIM), BF16)
        vaug[0:n_ctx, 0:HEAD_DIM] = vc_ref[...]
        if n_lat:
            vaug[n_ctx:, 0:HEAD_DIM] = vl_ref[...]

    if n_lat:
        pl.when(pl.program_id(2) == 0)(fill_values)
    else:
        fill_values()

    chunks = [(0, n_ctx)] + [(n_ctx + j, ATT_KC) for j in range(0, n_lat, ATT_KC)]
    nt = (((1,), (1,)), ((), ()))
    q_all = jnp.concatenate([q_ref[:, g * HEAD_DIM:(g + 1) * HEAD_DIM] for g in range(GQA_GROUP)], axis=0)
    m_part = jnp.full((GQA_GROUP * tq, LANES), -jnp.inf, F32)
    for off, size in chunks:
        keys = kc_ref[...] if off == 0 else kl_ref[off - n_ctx:off - n_ctx + size, :]
        sc = lax.dot_general(q_all, keys, nt, preferred_element_type=F32)
        s_scr[:, off:off + size] = sc
        for j in range(0, size, LANES):
            m_part = jnp.maximum(m_part, sc[:, j:j + LANES])
    m_row = jnp.max(m_part, axis=-1, keepdims=True)
    hr = GQA_GROUP * tq // 2
    acc = [jnp.zeros((hr, 2 * HEAD_DIM), F32), jnp.zeros((hr, 2 * HEAD_DIM), F32)]
    for off, size in chunks:
        for r in range(2):
            rows = slice(r * hr, (r + 1) * hr)
            p = jnp.exp2((s_scr[rows, off:off + size] - m_row[rows]).astype(BF16))
            acc[r] = acc[r] + jnp.dot(p, vaug[off:off + size, :], preferred_element_type=F32)
    for r in range(2):
        out = (acc[r][:, :HEAD_DIM] / acc[r][:, HEAD_DIM:]).astype(BF16)
        for j in range(GQA_GROUP // 2):
            g = r * (GQA_GROUP // 2) + j
            o_ref[:, g * HEAD_DIM:(g + 1) * HEAD_DIM] = out[j * tq:(j + 1) * tq]


def _attn_ctx_kernel(q_ref, kc_ref, vc_ref, o_all_ref, o_ref, s_scr, vaug):
    del o_all_ref
    _attn_kernel(q_ref, kc_ref, vc_ref, o_ref, s_scr, vaug, n_lat=0)


def _attention(q, k, v, s, c, b):
    t, d = q.shape
    gw = GQA_GROUP * HEAD_DIM
    tq = TQ
    nq = s // tq
    assert s % ATT_KC == 0
    ctx_blk = lambda bi, h, *_: ((b * s) // c + bi, h)
    o_lat = pl.pallas_call(
        functools.partial(_attn_kernel, n_lat=s),
        grid=(b, N_KV_HEADS, nq),
        in_specs=[pl.BlockSpec((tq, gw), lambda bi, h, i: (bi * nq + i, h)),
                  pl.BlockSpec((c, HEAD_DIM), ctx_blk),
                  pl.BlockSpec((c, HEAD_DIM), ctx_blk),
                  pl.BlockSpec((s, HEAD_DIM), lambda bi, h, i: (bi, h)),
                  pl.BlockSpec((s, HEAD_DIM), lambda bi, h, i: (bi, h))],
        out_specs=pl.BlockSpec((tq, gw), lambda bi, h, i: (bi * nq + i, h)),
        out_shape=jax.ShapeDtypeStruct((t, d), BF16),
        scratch_shapes=[pltpu.VMEM((GQA_GROUP * tq, c + s), F32), pltpu.VMEM((c + s, 2 * HEAD_DIM), BF16)],
        compiler_params=_cp("parallel", "parallel", "arbitrary"),
        name="attn_lat",
    )(q, k, v, k, v)
    return pl.pallas_call(
        _attn_ctx_kernel,
        grid=(b, N_KV_HEADS),
        in_specs=[pl.BlockSpec((c, gw), ctx_blk),
                  pl.BlockSpec((c, HEAD_DIM), ctx_blk),
                  pl.BlockSpec((c, HEAD_DIM), ctx_blk),
                  pl.BlockSpec(memory_space=pl.ANY)],
        out_specs=pl.BlockSpec((c, gw), ctx_blk),
        out_shape=jax.ShapeDtypeStruct((t, d), BF16),
        scratch_shapes=[pltpu.VMEM((GQA_GROUP * c, c), F32), pltpu.VMEM((c, 2 * HEAD_DIM), BF16)],
        input_output_aliases={3: 0},
        compiler_params=_cp("parallel", "parallel"),
        name="attn_ctx",
    )(q, k, v, o_lat)


def _attention_mixer(xa, pre, g, mod, w_qkv, qg, kg, w_o, route, s, c, b):
    cos, sin = _rope_tables(s)
    x, q, k, v = _qkv(xa, pre, g, mod, w_qkv, qg.reshape(1, -1), kg.reshape(1, -1), cos, sin, s, b)
    o = _attention(q, k, v, s, c, b)
    return _out_proj(o, x, mod, w_o, route, x.shape[0], s, b)


def _gmlp_kernel(*refs, has_pre):
    x_ref, pre_refs, rest = _split_refs(refs, has_pre)
    g_ref, mod_ref, w_in_ref, lng_ref, lnb_ref, ws_ref, bs_ref, w_out_ref = rest[:8]
    route_in, (o_ref, *route_out, uv_ref, cnt_s) = rest[8:11], rest[11:]
    d = x_ref.shape[-1]
    dcm = lng_ref.shape[-1]
    gw = dcm // CM_GROUPS
    x = _combined_rows(x_ref, pre_refs)
    m = mod_ref[0]
    h = _norm_mod(x, g_ref[...], m[:, 0:d], m[:, d:2 * d])
    z = jax.nn.gelu(jnp.dot(h.astype(BF16), w_in_ref[...], preferred_element_type=F32))
    u = z[:, :dcm]
    v = z[:, dcm:]
    mu = jnp.mean(v, axis=-1, keepdims=True)
    vc = v - mu
    var = jnp.mean(vc * vc, axis=-1, keepdims=True)
    vn = (vc * lax.rsqrt(var + NORM_EPS) * lng_ref[...] + lnb_ref[...]).astype(BF16)
    for ck in range(x.shape[0] // CHUNK):
        rows = slice(ck * CHUNK, (ck + 1) * CHUNK)
        for gi in range(CM_GROUPS):
            cols = slice(gi * gw, (gi + 1) * gw)
            mix = jnp.dot(ws_ref[gi], vn[rows, cols], preferred_element_type=F32) + bs_ref[:, gi:gi + 1]
            uv_ref[rows, cols] = (u[rows, cols] * mix).astype(BF16)
    y = jnp.dot(uv_ref[...], w_out_ref[...], preferred_element_type=F32)
    x1 = x + m[:, 2 * d:3 * d] * y
    o_ref[...] = x1
    _route_rows(x1, m, *route_in, *route_out, cnt_s)


def _gmlp_mixer(xa, pre, g, mod, w_in, ln_g, ln_b, w_s, b_s, w_out, route, s, b):
    t, d = xa.shape
    dcm = ln_g.shape[-1]
    full = lambda shape: pl.BlockSpec(shape, lambda i: (0,) * len(shape))
    pre_specs, pre_args = _pre_io(pre, d, TG, s, b)
    r_in, r_out, r_shape, r_scratch = _route_io(d, t, TG)
    x1, *routed = pl.pallas_call(
        functools.partial(_gmlp_kernel, has_pre=pre is not None),
        grid=(t // TG,),
        in_specs=[pl.BlockSpec((TG, d), lambda i: (i, 0))] + pre_specs + [
            full((1, d)),
            _mod_spec(d, s // TG, b),
            full((d, 2 * dcm)), full((1, dcm)), full((1, dcm)),
            full((CM_GROUPS, CHUNK, CHUNK)), full((CHUNK, CM_GROUPS)), full((dcm, d))] + r_in,
        out_specs=[pl.BlockSpec((TG, d), lambda i: (i, 0))] + r_out,
        out_shape=[jax.ShapeDtypeStruct((t, d), F32)] + r_shape,
        scratch_shapes=[pltpu.VMEM((TG, dcm), BF16)] + r_scratch,
        compiler_params=_cp("arbitrary"),
        name="gmlp",
    )(xa, *pre_args, g, mod, w_in, ln_g.reshape(1, dcm), ln_b.reshape(1, dcm),
      w_s.astype(BF16), b_s.T, w_out, *route)
    return x1, routed


def _route_rows(x1, m, g_ref, wr_ref, br_ref, hf_ref, rt_ref, ew_ref, cnt_ref, cnt_s):
    tm, d = x1.shape
    ng = EXPERTS_PER_GROUP

    @pl.when(pl.program_id(0) == 0)
    def _():
        cnt_s[...] = jnp.zeros_like(cnt_s)

    hf = _norm_mod(x1, g_ref[...], m[:, 3 * d:4 * d], m[:, 4 * d:5 * d])
    hf_ref[...] = _pack_bf16_pairs(hf)
    hf_hi = hf.astype(BF16)
    hf_lo = (hf - hf_hi.astype(F32)).astype(BF16)
    nt = (((1,), (1,)), ((), ()))
    by_hi = lax.dot_general(wr_ref[...], hf_hi, nt, preferred_element_type=F32)
    logits = (by_hi[:LANES] + by_hi[LANES:]
              + lax.dot_general(wr_ref[0:LANES, :], hf_lo, nt, preferred_element_type=F32)) + br_ref[...]
    neg = -jnp.inf
    row = lax.broadcasted_iota(I32, (ng, tm), 0)
    grp = logits[N_EXPERTS:N_EXPERTS + ng]
    gmax = jnp.max(grp, axis=0, keepdims=True)
    gsel = jnp.min(jnp.where(grp == gmax, row, ng), axis=0, keepdims=True)
    gate_g = 1.0 / jnp.sum(jnp.exp(grp - gmax), axis=0, keepdims=True)
    el = logits[0:ng]
    for gi in range(1, N_GROUPS):
        el = jnp.where(gsel == gi, logits[gi * ng:(gi + 1) * ng], el)
    v1 = jnp.max(el, axis=0, keepdims=True)
    i1 = jnp.min(jnp.where(el == v1, row, ng), axis=0, keepdims=True)
    el2 = jnp.where(row == i1, neg, el)
    v2 = jnp.max(el2, axis=0, keepdims=True)
    i2 = jnp.min(jnp.where(el2 == v2, row, ng), axis=0, keepdims=True)
    e21 = jnp.exp(v2 - v1)
    w1 = gate_g / (1.0 + e21)
    w2 = w1 * e21
    e1 = gsel * ng + i1
    e2 = gsel * ng + i2
    lane_row = lax.broadcasted_iota(I32, (LANES, tm), 0)
    ew_ref[...] = jnp.where(lane_row == 0, w1, jnp.where(lane_row == 1, w2, 0.0)).T

    expert = lax.broadcasted_iota(I32, (N_EXPERTS, tm), 0)
    oh1 = expert == e1
    oh2 = expert == e2
    earlier = (lax.broadcasted_iota(I32, (tm, tm), 0) < lax.broadcasted_iota(I32, (tm, tm), 1)).astype(BF16)
    pre1 = jnp.dot(jnp.where(oh1, 1.0, 0.0).astype(BF16), earlier, preferred_element_type=F32)
    pre2 = jnp.dot(jnp.where(oh2, 1.0, 0.0).astype(BF16), earlier, preferred_element_type=F32)
    tot1 = jnp.sum(jnp.where(oh1, 1.0, 0.0), axis=1, keepdims=True)
    tot2 = jnp.sum(jnp.where(oh2, 1.0, 0.0), axis=1, keepdims=True)
    cnt = cnt_s[:, 0:1]
    rank1 = jnp.sum(jnp.where(oh1, cnt + pre1, 0.0), axis=0, keepdims=True).astype(I32)
    rank2 = jnp.sum(jnp.where(oh2, cnt + tot1 + pre2, 0.0), axis=0, keepdims=True).astype(I32)
    cnt = jnp.broadcast_to(cnt + tot1 + tot2, cnt_s.shape)
    cnt_s[...] = cnt
    cnt_ref[...] = cnt.astype(I32)
    rt_ref[...] = jnp.where(row == 0, e1, jnp.where(row == 1, e2, jnp.where(row == 2, rank1,
                            jnp.where(row == 3, rank2, 0))))


def _route_params(g_ffn, w_group, b_group, w_router, b_router):
    d = w_group.shape[0]
    pad = LANES - N_EXPERTS - N_GROUPS
    wr = jnp.concatenate([w_router.reshape(d, N_EXPERTS), w_group, jnp.zeros((d, pad), F32)], axis=1).T
    br = jnp.concatenate([b_router.reshape(N_EXPERTS), b_group,
                          jnp.full((EXPERTS_PER_GROUP - N_GROUPS,), -jnp.inf, F32),
                          jnp.zeros((pad - EXPERTS_PER_GROUP + N_GROUPS,), F32)]).reshape(LANES, 1)
    wr_hi = wr.astype(BF16)
    wr_lo = (wr - wr_hi.astype(F32)).astype(BF16)
    return g_ffn.reshape(1, d), jnp.concatenate([wr_hi, wr_lo], axis=0), br


def _route_io(d, nrows, tr):
    const = lambda shape: pl.BlockSpec(shape, lambda i: (0, 0))
    row = lambda w: pl.BlockSpec((tr, w), lambda i: (i, 0))
    in_specs = [const((1, d)), const((2 * LANES, d)), const((LANES, 1))]
    out_specs = [row(d // 2), pl.BlockSpec((SUBLANES, tr), lambda i: (0, i)), row(LANES), const((N_EXPERTS, LANES))]
    out_shape = [jax.ShapeDtypeStruct((nrows, d // 2), I32), jax.ShapeDtypeStruct((SUBLANES, nrows), I32),
                 jax.ShapeDtypeStruct((nrows, LANES), F32), jax.ShapeDtypeStruct((N_EXPERTS, LANES), I32)]
    return in_specs, out_specs, out_shape, [pltpu.VMEM((N_EXPERTS, LANES), F32)]


def _plan_kernel(cnt_ref, rt_ref, pos_ref, blk_e_ref, n_used_ref, first_ref, slot_ref, next_ref,
                 start_s, end_s, nxt_s, *, bm):
    n_blk = blk_e_ref.shape[0]
    acc = jnp.int32(0)
    for e in range(N_EXPERTS):
        start_s[e] = acc
        acc = acc + (cnt_ref[e, 0] + (bm - 1)) // bm * bm
        end_s[e] = acc
    n_used = acc // bm
    n_used_ref[0] = n_used
    nxt = jnp.int32(-1)
    for e in reversed(range(N_EXPERTS)):
        nxt_s[e] = nxt
        nxt = jnp.where(cnt_ref[e, 0] > 0, e, nxt)

    def block(i, carry):
        prev_e, runs = carry
        row = jnp.minimum(i, n_used - 1) * bm
        e = lax.while_loop(lambda v: jnp.logical_and(v < N_EXPERTS - 1, end_s[v] <= row), lambda v: v + 1,
                           jnp.maximum(prev_e, 0))
        first = jnp.logical_and(i < n_used, prev_e != e)
        runs = runs + first.astype(I32)
        blk_e_ref[i] = e
        first_ref[i] = first.astype(I32)
        slot_ref[i] = (runs - 1) % 2
        next_ref[i] = nxt_s[e]
        return e, runs

    lax.fori_loop(0, n_blk, block, (jnp.int32(-1), jnp.int32(0)))

    rt = rt_ref[...]
    start_of = jnp.zeros_like(rt)
    for e in range(N_EXPERTS):
        start_of = jnp.where(rt == e, start_s[e], start_of)
    pos_ref[...] = start_of + pltpu.roll(rt, SUBLANES - 2, 0)


def _dispatch_plan(rt, cnt, bm):
    n_tok = rt.shape[1]
    n_rows = 2 * n_tok + N_EXPERTS * bm
    n_blk = n_rows // bm
    smem = pl.BlockSpec(memory_space=pltpu.SMEM)
    vec = lambda n: jax.ShapeDtypeStruct((n,), I32)
    pos, blk_e, n_used, first, slot, nxt = pl.pallas_call(
        functools.partial(_plan_kernel, bm=bm),
        in_specs=[smem, pl.BlockSpec(memory_space=pltpu.VMEM)],
        out_specs=[pl.BlockSpec(memory_space=pltpu.VMEM), smem, smem, smem, smem, smem],
        out_shape=[jax.ShapeDtypeStruct(rt.shape, I32), vec(n_blk), vec(1), vec(n_blk), vec(n_blk), vec(n_blk)],
        scratch_shapes=[pltpu.SMEM((N_EXPERTS,), I32)] * 3,
        name="moe_plan",
    )(cnt, rt)
    return pos, (blk_e, n_used, first, slot, nxt), n_rows


def _sc_mesh():
    return plsc.VectorSubcoreMesh(core_axis_name="c", subcore_axis_name="s")


def _sc_worker_base(per_worker):
    return (lax.axis_index("s") * SC_CORES + lax.axis_index("c")) * per_worker


def _sc_dispatch(hf, pos, n_rows):
    t, d = hf.shape
    per_w = t // SC_WORKERS
    ch = SC_CHUNK
    n_ck = per_w // ch
    assert per_w * SC_WORKERS == t and n_ck * ch == per_w

    @functools.partial(
        pl.kernel, mesh=_sc_mesh(), out_type=jax.ShapeDtypeStruct((n_rows, d), hf.dtype),
        scratch_types=[pltpu.VMEM((per_w,), I32), pltpu.VMEM((per_w,), I32), pltpu.VMEM((2, ch, d), hf.dtype),
                       pltpu.SemaphoreType.DMA((2,)), pltpu.SemaphoreType.DMA((2,)), pltpu.SemaphoreType.DMA((2,))])
    def dispatch(hf_hbm, p_hbm, out_hbm, i0_v, i1_v, rows_v, sem_in, sem_s0, sem_s1):
        base = pl.multiple_of(_sc_worker_base(per_w), SUBLANES)
        pltpu.sync_copy(p_hbm.at[pl.ds(base, per_w)], i0_v)
        pltpu.sync_copy(p_hbm.at[pl.ds(pl.multiple_of(t + base, SUBLANES), per_w)], i1_v)

        def load(ck):
            return pltpu.make_async_copy(hf_hbm.at[pl.ds(base + ck * ch, ch)], rows_v.at[ck % 2], sem_in.at[ck % 2])

        def scatters(ck):
            src = rows_v.at[ck % 2]
            return (pltpu.make_async_copy(src, out_hbm.at[i0_v.at[pl.ds(ck * ch, ch)]], sem_s0.at[ck % 2]),
                    pltpu.make_async_copy(src, out_hbm.at[i1_v.at[pl.ds(ck * ch, ch)]], sem_s1.at[ck % 2]))

        load(0).start()
        for ck in range(n_ck):
            load(ck).wait()
            if ck + 1 < n_ck:
                if ck >= 1:
                    for cp in scatters(ck - 1):
                        cp.wait()
                load(ck + 1).start()
            for cp in scatters(ck):
                cp.start()
        for ck in range(max(n_ck - 2, 0), n_ck):
            for cp in scatters(ck):
                cp.wait()

    return dispatch(hf, pos)


def _sc_gather(rows, idx):
    n = idx.shape[0]
    d = rows.shape[1]
    per_w = n // SC_WORKERS
    ch = 2 * SC_CHUNK
    n_ck = per_w // ch
    assert per_w * SC_WORKERS == n and n_ck * ch == per_w

    @functools.partial(
        pl.kernel, mesh=_sc_mesh(), out_type=jax.ShapeDtypeStruct((n, d), rows.dtype),
        scratch_types=[pltpu.VMEM((per_w,), I32), pltpu.VMEM((2, ch, d), rows.dtype),
                       pltpu.SemaphoreType.DMA((2,)), pltpu.SemaphoreType.DMA((2,))])
    def gather(rows_hbm, i_hbm, out_hbm, i_v, buf, sem_g, sem_w):
        base = pl.multiple_of(_sc_worker_base(per_w), SUBLANES)
        pltpu.sync_copy(i_hbm.at[pl.ds(base, per_w)], i_v)

        def fetch(ck):
            return pltpu.make_async_copy(rows_hbm.at[i_v.at[pl.ds(ck * ch, ch)]], buf.at[ck % 2], sem_g.at[ck % 2])

        def write(ck):
            return pltpu.make_async_copy(buf.at[ck % 2], out_hbm.at[pl.ds(base + ck * ch, ch)], sem_w.at[ck % 2])

        fetch(0).start()
        for ck in range(n_ck):
            fetch(ck).wait()
            if ck + 1 < n_ck:
                if ck >= 1:
                    write(ck - 1).wait()
                fetch(ck + 1).start()
            write(ck).start()
        for ck in range(max(n_ck - 2, 0), n_ck):
            write(ck).wait()

    return gather(rows, idx)


def _expert_kernel(blk_e_ref, n_used_ref, first_ref, slot_ref, next_ref, x_ref, wg_hbm, wu_hbm, wd_hbm, y_ref,
                   wgf, wuf, wdf, wgb, wub, wdb, sem, *, e_base):
    n_used = n_used_ref[0]
    bm = x_ref.shape[0] // MOE_SUB

    def weight_copies(e, slot):
        return (pltpu.make_async_copy(wg_hbm.at[e_base + e], wgf.at[slot], sem.at[slot, 0]),
                pltpu.make_async_copy(wu_hbm.at[e_base + e], wuf.at[slot], sem.at[slot, 1]),
                pltpu.make_async_copy(wd_hbm.at[e_base + e], wdf.at[slot], sem.at[slot, 2]))

    @pl.when(pl.program_id(0) == 0)
    def _():
        for cp in weight_copies(blk_e_ref[0], 0):
            cp.start()

    for j in range(MOE_SUB):
        blk = pl.program_id(0) * MOE_SUB + j
        rows = slice(j * bm, (j + 1) * bm)

        @pl.when(jnp.logical_and(blk < n_used, first_ref[blk] == 1))
        def _():
            slot = slot_ref[blk]
            for cp in weight_copies(blk_e_ref[blk], slot):
                cp.wait()
            nxt = next_ref[blk]

            @pl.when(nxt >= 0)
            def _():
                for cp in weight_copies(nxt, 1 - slot):
                    cp.start()

            wgb[...] = wgf[slot].astype(BF16)
            wub[...] = wuf[slot].astype(BF16)
            wdb[...] = wdf[slot].astype(BF16)

        @pl.when(blk < n_used)
        def _():
            x_hi, x_lo = _unpack_bf16_pairs(x_ref[rows, :])
            xb = jnp.concatenate([x_hi.astype(BF16), x_lo.astype(BF16)], axis=1)
            gt = jnp.dot(xb, wgb[...], preferred_element_type=F32)
            up = jnp.dot(xb, wub[...], preferred_element_type=F32)
            act = (gt * _sigmoid(gt) * up).astype(BF16)
            y_ref[rows, :] = _pack_bf16_pairs(jnp.dot(act, wdb[...], preferred_element_type=F32))


def _experts(x_rows, plan, w_gate, w_up, w_down, layer):
    n_rows, dp = x_rows.shape
    depth, n_e, d, de = w_gate.shape
    step_rows = MOE_SUB * MOE_BM
    assert n_rows % step_rows == 0
    any_spec = pl.BlockSpec(memory_space=pl.ANY)
    last_used = lambda i, be, nu, *_: (jnp.minimum(i, (nu[0] - 1) // MOE_SUB), 0)
    grid_spec = pltpu.PrefetchScalarGridSpec(
        num_scalar_prefetch=5,
        grid=(n_rows // step_rows,),
        in_specs=[pl.BlockSpec((step_rows, dp), last_used), any_spec, any_spec, any_spec],
        out_specs=pl.BlockSpec((step_rows, dp), last_used),
        scratch_shapes=[pltpu.VMEM((2, d, de), F32), pltpu.VMEM((2, d, de), F32), pltpu.VMEM((2, de, d), F32),
                        pltpu.VMEM((d, de), BF16), pltpu.VMEM((d, de), BF16), pltpu.VMEM((de, d), BF16),
                        pltpu.SemaphoreType.DMA((2, 3))],
    )
    return pl.pallas_call(
        functools.partial(_expert_kernel, e_base=layer * n_e),
        grid_spec=grid_spec,
        out_shape=jax.ShapeDtypeStruct((n_rows, dp), I32),
        compiler_params=_cp("arbitrary"),
        name="moe_experts",
    )(*plan, x_rows, w_gate.reshape(depth * n_e, d, de), w_up.reshape(depth * n_e, d, de),
      w_down.reshape(depth * n_e, de, d))


def _combine_kernel(x_ref, *rest):
    *pre_refs, o_ref = rest
    o_ref[...] = _combined_rows(x_ref, pre_refs)


def _combine(x, pre, s, b):
    nrows, d = x.shape
    pre_specs, pre_args = _pre_io(pre, d, TR, s, b)
    return pl.pallas_call(
        _combine_kernel,
        grid=(nrows // TR,),
        in_specs=[pl.BlockSpec((TR, d), lambda i: (i, 0))] + pre_specs,
        out_specs=pl.BlockSpec((TR, d), lambda i: (i, 0)),
        out_shape=jax.ShapeDtypeStruct((nrows, d), F32),
        compiler_params=_cp("parallel"),
        name="moe_combine",
    )(x, *pre_args)


def _expert_outputs(routed, w_gate, w_up, w_down, layer):
    hf, rt, _, cnt = routed
    pos, plan, n_rows = _dispatch_plan(rt, cnt, MOE_BM)
    pos = pos[0:2].reshape(-1)
    x_rows = _sc_dispatch(hf, pos, n_rows)
    y_rows = _experts(x_rows, plan, w_gate, w_up, w_down, layer)
    return _sc_gather(y_rows, pos)


def kernel(x, c, ctx, c_ctx, ada_w, ada_b, norm_mix_g, norm_ffn_g, rg_w_in, rg_conv_w, rg_conv_b, rg_wa, rg_ba, rg_wi, rg_bi, rg_lambda, rg_w_out, at_w_qkv, at_q_g, at_k_g, at_w_o, cm_w_in, cm_ln_g, cm_ln_b, cm_w_s, cm_b_s, cm_w_out, moe_w_group, moe_b_group, moe_w_router, moe_b_router, moe_w_gate, moe_w_up, moe_w_down):
    b, s, d = x.shape
    cl = ctx.shape[1]
    depth = ada_w.shape[0]
    n_lat = b * s
    assert b < SUBLANES and s % max(TI, TR, TG) == 0 and (b * cl) % max(TI, TR, TG) == 0 and cl % TM == 0
    assert d == RG_BLOCKS * LANES

    cin = jnp.concatenate([c, c_ctx[None, :], jnp.zeros((SUBLANES - b - 1, d), F32)], axis=0)
    mod_all = _ada_table(cin, ada_w, ada_b).reshape(depth, SUBLANES, 1, N_MOD * d)
    n_tok = n_lat + b * cl

    xa, pre = (x.reshape(n_lat, d), ctx.reshape(b * cl, d)), None
    for l in range(depth):
        kind = l % 3
        j = l // 3
        last = l == depth - 1
        mod = mod_all[l]
        g_mix = norm_mix_g[l].reshape(1, d)
        route = _route_params(norm_ffn_g[l], moe_w_group[l], moe_b_group[l], moe_w_router[l], moe_b_router[l])
        if kind == 0:
            x1, routed = _rglru_mixer(xa, pre, g_mix, mod, _bf16_weight(rg_w_in, j), rg_conv_w[j], rg_conv_b[j],
                                      rg_wa[j], rg_ba[j], rg_wi[j], rg_bi[j], rg_lambda[j],
                                      _bf16_weight(rg_w_out, j), route, s, cl, b, n_lat if last else n_tok)
        elif kind == 1:
            x1, routed = _attention_mixer(xa, pre, g_mix, mod, _bf16_weight(at_w_qkv, j), at_q_g[j], at_k_g[j],
                                          _bf16_weight(at_w_o, j), route, s, cl, b)
        else:
            x1, routed = _gmlp_mixer(xa, pre, g_mix, mod, _bf16_weight(cm_w_in, j), cm_ln_g[j], cm_ln_b[j],
                                     cm_w_s[j], cm_b_s[j], _bf16_weight(cm_w_out, j), route, s, b)
        y01 = _expert_outputs(routed, moe_w_gate, moe_w_up, moe_w_down, l)
        xa, pre = x1, (y01, routed[2], mod)
    return _combine(xa, pre, s, b)[:n_lat].reshape(b, s, d)
```

```python
import functools

import jax
import jax.numpy as jnp
from jax import lax
from jax.experimental import pallas as pl
from jax.experimental.pallas import tpu as pltpu
from jax.experimental.pallas import tpu_sc as plsc

F32 = jnp.float32
BF16 = jnp.bfloat16
I32 = jnp.int32
U32 = jnp.uint32

NORM_EPS = 1e-6
N_MOD = 6
GRID_W = 64
RG_BLOCKS = 8
CONV_W = 4
RG_C = 8.0
HEAD_DIM = 128
N_KV_HEADS = 2
GQA_GROUP = 4
ROPE_THETA = 10000.0
CHUNK = 128
CM_GROUPS = 8
N_GROUPS = 4
EXPERTS_PER_GROUP = 8
N_EXPERTS = N_GROUPS * EXPERTS_PER_GROUP

LANES = 128
SUBLANES = 8
TM = 256
TQ = 512
TI = 1024
TR = 1024
TG = 512
CAST_ROWS = 512
TL = 256
HALO = 8
ATT_KC = 512
LOG2E = 1.4426950408889634
MOE_BM = 256
MOE_SUB = 4
SC_CORES = 2
SC_WORKERS = 32
SC_CHUNK = 32
VMEM_LIMIT = 52 * 2**20


def _cp(*sem):
    return pltpu.CompilerParams(dimension_semantics=sem, vmem_limit_bytes=VMEM_LIMIT)


def _norm_mod(x, g, shift, scale):
    ms = jnp.mean(x * x, axis=-1, keepdims=True)
    y = x * lax.rsqrt(ms + NORM_EPS) * g
    return y * (1.0 + scale) + shift


def _sigmoid(x):
    return 0.5 * jnp.tanh(0.5 * x) + 0.5


def _pack_bf16_pairs(x):
    h = x.shape[-1] // 2
    hi = lax.bitcast_convert_type(x[:, :h].astype(BF16).astype(F32), U32)
    lo = lax.bitcast_convert_type(x[:, h:].astype(BF16).astype(F32), U32)
    return lax.bitcast_convert_type(hi | (lo >> 16), I32)


def _unpack_bf16_pairs(w):
    u = lax.bitcast_convert_type(w, U32)
    hi = lax.bitcast_convert_type(u & jnp.uint32(0xFFFF0000), F32)
    lo = lax.bitcast_convert_type(u << 16, F32)
    return hi, lo


def _combined_rows(x_ref, pre_refs):
    if not pre_refs:
        return x_ref[...]
    y0_ref, y1_ref, ew_ref, modp_ref = pre_refs
    d = x_ref.shape[1]
    ew = ew_ref[...]
    y0_hi, y0_lo = _unpack_bf16_pairs(y0_ref[...])
    y1_hi, y1_lo = _unpack_bf16_pairs(y1_ref[...])
    y = jnp.concatenate([ew[:, 0:1] * y0_hi + ew[:, 1:2] * y1_hi, ew[:, 0:1] * y0_lo + ew[:, 1:2] * y1_lo], axis=1)
    return x_ref[...] + modp_ref[0][:, 5 * d:6 * d] * y


def _pre_io(pre, d, tr, s, b):
    if pre is None:
        return [], []
    y01, ew, mod_prev = pre
    nb = y01.shape[0] // 2 // tr
    specs = [pl.BlockSpec((tr, d // 2), lambda i: (i, 0)), pl.BlockSpec((tr, d // 2), lambda i: (i + nb, 0)),
             pl.BlockSpec((tr, LANES), lambda i: (i, 0)), _mod_spec(d, s // tr, b)]
    return specs, [y01, y01, ew, mod_prev]


def _split_refs(refs, has_pre):
    return (refs[0], refs[1:5], refs[5:]) if has_pre else (refs[0], (), refs[1:])


def _mod_spec(d, rows_per_sample, n_samples):
    return pl.BlockSpec((1, 1, N_MOD * d),
                        lambda i, *_: (jnp.minimum(i // rows_per_sample, n_samples), 0, 0))


def _cast_kernel(w_ref, o_ref):
    o_ref[...] = w_ref[0].astype(BF16)


def _bf16_weight(w_stack, j):
    _, r, c = w_stack.shape
    return pl.pallas_call(
        _cast_kernel,
        grid=(r // CAST_ROWS,),
        in_specs=[pl.BlockSpec((1, CAST_ROWS, c), lambda i: (j, i, 0))],
        out_specs=pl.BlockSpec((CAST_ROWS, c), lambda i: (i, 0)),
        out_shape=jax.ShapeDtypeStruct((r, c), BF16),
        compiler_params=_cp("parallel"),
        name="weight_bf16",
    )(w_stack)


def _ada_kernel(c_ref, w_ref, b_ref, o_ref):
    cin = c_ref[...]
    act = cin * jax.nn.sigmoid(cin)
    w = w_ref[0]
    w_hi = w.astype(BF16)
    w_lo = (w - w_hi.astype(F32)).astype(BF16)
    a_hi = act.astype(BF16)
    a_lo = (act - a_hi.astype(F32)).astype(BF16)
    o_ref[0] = (jnp.dot(a_hi, w_hi, preferred_element_type=F32) + jnp.dot(a_lo, w_hi, preferred_element_type=F32)
                + jnp.dot(a_hi, w_lo, preferred_element_type=F32)) + b_ref[0]


def _ada_table(cin, ada_w, ada_b):
    depth, d, n = ada_w.shape
    tn = 2 * d
    return pl.pallas_call(
        _ada_kernel,
        grid=(depth, n // tn),
        in_specs=[pl.BlockSpec((SUBLANES, d), lambda l, j: (0, 0)),
                  pl.BlockSpec((1, d, tn), lambda l, j: (l, 0, j)),
                  pl.BlockSpec((1, 1, tn), lambda l, j: (l, 0, j))],
        out_specs=pl.BlockSpec((1, SUBLANES, tn), lambda l, j: (l, 0, j)),
        out_shape=jax.ShapeDtypeStruct((depth, SUBLANES, n), F32),
        compiler_params=_cp("parallel", "parallel"),
        name="ada_table",
    )(cin, ada_w, ada_b.reshape(depth, 1, n))


def _rows_io(x, tr):
    if not isinstance(x, tuple):
        return [pl.BlockSpec((tr, x.shape[1]), lambda i: (i, 0))], [x], 0
    lat, cx = x
    nl = lat.shape[0] // tr
    assert nl * tr == lat.shape[0] and cx.shape[0] % tr == 0
    specs = [pl.BlockSpec((tr, lat.shape[1]), lambda i: (jnp.minimum(i, nl - 1), 0)),
             pl.BlockSpec((tr, lat.shape[1]), lambda i: (jnp.maximum(i - nl, 0), 0))]
    return specs, [lat, cx], nl


def _rows_value(x_refs, n_lat_blocks):
    if not n_lat_blocks:
        return x_refs[0][...]
    return jnp.where(pl.program_id(0) < n_lat_blocks, x_refs[0][...], x_refs[1][...])


def _out_kernel(y_ref, *rest, n_lat_blocks):
    n_src = 2 if n_lat_blocks else 1
    x_refs, (mod_ref, w_ref, *rest) = rest[:n_src], rest[n_src:]
    route_in, (o_ref, *route_out) = rest[:3], rest[3:]
    d = o_ref.shape[-1]
    m = mod_ref[0]
    y = jnp.dot(y_ref[...].astype(BF16), w_ref[...], preferred_element_type=F32)
    x1 = _rows_value(x_refs, n_lat_blocks) + m[:, 2 * d:3 * d] * y
    o_ref[...] = x1
    _route_rows(x1, m, *route_in, *route_out)


def _out_proj(y, x, mod, w, route, nrows, s, b):
    k, d = w.shape
    r_in, r_out, r_shape, r_scratch = _route_io(d, nrows, TR)
    x_specs, x_args, n_lat_blocks = _rows_io(x, TR)
    x1, *routed = pl.pallas_call(
        functools.partial(_out_kernel, n_lat_blocks=n_lat_blocks),
        grid=(nrows // TR,),
        in_specs=[pl.BlockSpec((TR, k), lambda i: (i, 0))] + x_specs + [
            _mod_spec(d, s // TR, b), pl.BlockSpec((k, d), lambda i: (0, 0))] + r_in,
        out_specs=[pl.BlockSpec((TR, d), lambda i: (i, 0))] + r_out,
        out_shape=[jax.ShapeDtypeStruct((nrows, d), F32)] + r_shape,
        scratch_shapes=r_scratch,
        compiler_params=_cp("arbitrary"),
        name="out_proj",
    )(y, *x_args, mod, w, *route)
    return x1, routed


def _rg_in_kernel(*refs, has_pre, n_lat_blocks):
    if n_lat_blocks:
        x_ref, (g_ref, mod_ref, w_ref, *outs) = refs[0], refs[2:]
        x = _rows_value(refs[:2], n_lat_blocks)
    else:
        x_ref, pre_refs, (g_ref, mod_ref, w_ref, *outs) = _split_refs(refs, has_pre)
        x = _combined_rows(x_ref, pre_refs)
    gg_ref, xin_ref = outs[-2:]
    d = x_ref.shape[-1]
    m = mod_ref[0]
    if has_pre:
        outs[0][...] = x
    h = _norm_mod(x, g_ref[...], m[:, 0:d], m[:, d:2 * d])
    z = jnp.dot(h.astype(BF16), w_ref[...], preferred_element_type=F32)
    tm = x_ref.shape[0]
    for n in range(d // LANES):
        cols = slice(n * LANES, (n + 1) * LANES)
        gg_ref[pl.ds(n, tm, stride=SUBLANES), :] = jax.nn.gelu(z[:, cols])
        xin_ref[pl.ds(n, tm, stride=SUBLANES), :] = z[:, d + n * LANES:d + (n + 1) * LANES]


def _rg_in(x, pre, g, mod, w, s, b):
    d = w.shape[0]
    assert d == SUBLANES * LANES
    x_specs, x_args, n_lat_blocks = _rows_io(x, TI)
    assert not (n_lat_blocks and pre)
    t = sum(a.shape[0] for a in x_args)
    pre_specs, pre_args = _pre_io(pre, d, TI, s, b)
    row = pl.BlockSpec((TI, d), lambda i: (i, 0))
    tmajor = pl.BlockSpec((TI * SUBLANES, LANES), lambda i: (i, 0))
    outs = pl.pallas_call(
        functools.partial(_rg_in_kernel, has_pre=pre is not None, n_lat_blocks=n_lat_blocks),
        grid=(t // TI,),
        in_specs=x_specs + pre_specs + [pl.BlockSpec((1, d), lambda i: (0, 0)), _mod_spec(d, s // TI, b),
                                        pl.BlockSpec((d, 2 * d), lambda i: (0, 0))],
        out_specs=([row] if pre else []) + [tmajor, tmajor],
        out_shape=([jax.ShapeDtypeStruct((t, d), F32)] if pre else [])
        + [jax.ShapeDtypeStruct((t * SUBLANES, LANES), F32)] * 2,
        compiler_params=_cp("parallel"),
        name="rg_in",
    )(*x_args, *pre_args, g, mod, w)
    return (outs[0], outs[1], outs[2]) if pre else (x, outs[0], outs[1])


def _rg_gates_and_scan(xc, wa_ref, wi_ref, ba_ref, bi_ref, lam_ref, a_s, b_s, h_dst, hcar, reverse):
    @pl.when(pl.program_id(1) == 0)
    def _():
        hcar[...] = jnp.zeros_like(hcar)

    for n in range(RG_BLOCKS):
        cols = slice(n * LANES, (n + 1) * LANES)
        xn = xc[pl.ds(n, TL, stride=SUBLANES), :]
        xb = xn.astype(BF16)
        ta = jnp.tanh(jnp.dot(xb, wa_ref[n], preferred_element_type=F32) + ba_ref[:, cols])
        ti = jnp.tanh(jnp.dot(xb, wi_ref[n], preferred_element_type=F32) + bi_ref[:, cols])
        k = (-0.5 * RG_C * LOG2E) * jax.nn.softplus(-lam_ref[:, cols])
        a = jnp.exp2(k * ta + k)
        om = 1.0 - a * a
        root = jnp.where(om > 0.0, om * lax.rsqrt(om), 0.0)
        a_s[pl.ds(n, TL, stride=SUBLANES), :] = a
        b_s[pl.ds(n, TL, stride=SUBLANES), :] = root * (0.5 * xn) * (ti + 1.0)

    def two_steps(p, h):
        t0 = (TL - 1 - 2 * p) if reverse else 2 * p
        t1 = (t0 - 1) if reverse else (t0 + 1)
        r0 = pl.multiple_of(t0 * SUBLANES, SUBLANES)
        r1 = pl.multiple_of(t1 * SUBLANES, SUBLANES)
        a0 = a_s[pl.ds(r0, SUBLANES), :]
        b0 = b_s[pl.ds(r0, SUBLANES), :]
        a1 = a_s[pl.ds(r1, SUBLANES), :]
        b1 = b_s[pl.ds(r1, SUBLANES), :]
        h_dst[pl.ds(r0, SUBLANES), :] = a0 * h + b0
        h2 = (a1 * a0) * h + (a1 * b0 + b1)
        h_dst[pl.ds(r1, SUBLANES), :] = h2
        return h2

    hcar[...] = lax.fori_loop(0, TL // 2, two_steps, hcar[...], unroll=8)


def _rg_fwd_kernel(xm_ref, xprev_ref, xnext_ref, cw_ref, cb_ref, wa_ref, wi_ref, ba_ref, bi_ref, lam_ref,
                   hf_ref, xc_ref, xpad, a_s, b_s, hcar, *, nlat):
    rows = TL * SUBLANES
    hrows = HALO * SUBLANES
    j = pl.program_id(1)
    has_prev = j >= 2
    has_next = jnp.logical_and(j >= 1, j < nlat)
    xpad[0:hrows, :] = jnp.where(has_prev, xprev_ref[...], 0.0)
    xpad[hrows:hrows + rows, :] = xm_ref[...]
    xpad[hrows + rows:2 * hrows + rows, :] = jnp.where(has_next, xnext_ref[...], 0.0)
    acc = jnp.broadcast_to(cb_ref[...][None], (TL, SUBLANES, LANES))
    for k in range(CONV_W):
        off = (HALO + k - CONV_W // 2) * SUBLANES
        tap = xpad[off:off + rows, :].reshape(TL, SUBLANES, LANES)
        acc = acc + tap * cw_ref[k][None]
    xc_ref[...] = acc.reshape(rows, LANES)
    _rg_gates_and_scan(xc_ref, wa_ref, wi_ref, ba_ref, bi_ref, lam_ref, a_s, b_s, hf_ref, hcar, False)


def _rg_bwd_kernel(xc_ref, wa_ref, wi_ref, ba_ref, bi_ref, lam_ref, hf_ref, gg_ref, out_ref, a_s, b_s, h_s, hcar):
    _rg_gates_and_scan(xc_ref, wa_ref, wi_ref, ba_ref, bi_ref, lam_ref, a_s, b_s, h_s, hcar, True)
    h_s[...] = gg_ref[...] * (hf_ref[...] + h_s[...])
    for n in range(RG_BLOCKS):
        out_ref[:, n * LANES:(n + 1) * LANES] = h_s[pl.ds(n, TL, stride=SUBLANES), :].astype(BF16)


def _rg_scans(xin8, gg8, conv_w, conv_b, wa, wi, ba, bi, lam, s, c, b):
    assert c == TL and s % TL == 0
    rows = TL * SUBLANES
    hrows = HALO * SUBLANES
    nlat = s // TL
    t = xin8.shape[0] // SUBLANES
    n_halo = t // HALO
    d = RG_BLOCKS * LANES

    def chunk(reverse):
        return lambda bi_, j: jnp.where(j == 0, (b * s) // TL + bi_,
                                        bi_ * nlat + ((nlat - j) if reverse else (j - 1)))

    fwd, bwd = chunk(False), chunk(True)
    main_f = pl.BlockSpec((rows, LANES), lambda bi_, j: (fwd(bi_, j), 0))
    main_b = pl.BlockSpec((rows, LANES), lambda bi_, j: (bwd(bi_, j), 0))
    prev = pl.BlockSpec((hrows, LANES), lambda bi_, j: (jnp.maximum(fwd(bi_, j) * (TL // HALO) - 1, 0), 0))
    nxt = pl.BlockSpec((hrows, LANES),
                       lambda bi_, j: (jnp.minimum((fwd(bi_, j) + 1) * (TL // HALO), n_halo - 1), 0))
    full = lambda shape: pl.BlockSpec(shape, lambda bi_, j: (0,) * len(shape))
    gate_specs = [full((RG_BLOCKS, LANES, LANES)), full((RG_BLOCKS, LANES, LANES)),
                  full((1, d)), full((1, d)), full((1, d))]
    gate_args = lambda k: [(0.5 * wa[k]).astype(BF16), (0.5 * wi[k]).astype(BF16), 0.5 * ba[k].reshape(1, d),
                           0.5 * bi[k].reshape(1, d), lam[k].reshape(1, d)]
    tmajor = jax.ShapeDtypeStruct(xin8.shape, F32)
    buf = pltpu.VMEM((rows, LANES), F32)
    hf8, xc8 = pl.pallas_call(
        functools.partial(_rg_fwd_kernel, nlat=nlat),
        grid=(b, nlat + 1),
        in_specs=[main_f, prev, nxt, full((CONV_W, SUBLANES, LANES)), full((SUBLANES, LANES))] + gate_specs,
        out_specs=[main_f, main_f],
        out_shape=[tmajor, tmajor],
        scratch_shapes=[pltpu.VMEM((rows + 2 * hrows, LANES), F32), buf, buf, pltpu.VMEM((SUBLANES, LANES), F32)],
        compiler_params=_cp("parallel", "arbitrary"),
        name="rg_scan_fwd",
    )(xin8, xin8, xin8, conv_w.reshape(CONV_W, SUBLANES, LANES), conv_b.reshape(SUBLANES, LANES), *gate_args(0))
    return pl.pallas_call(
        _rg_bwd_kernel,
        grid=(b, nlat + 1),
        in_specs=[main_b] + gate_specs + [main_b, main_b],
        out_specs=pl.BlockSpec((TL, d), lambda bi_, j: (bwd(bi_, j), 0)),
        out_shape=jax.ShapeDtypeStruct((t, d), BF16),
        scratch_shapes=[buf, buf, buf, pltpu.VMEM((SUBLANES, LANES), F32)],
        compiler_params=_cp("parallel", "arbitrary"),
        name="rg_scan_bwd",
    )(xc8, *gate_args(1), hf8, gg8)


def _rglru_mixer(xa, pre, g, mod, w_in, conv_w, conv_b, wa, ba, wi, bi, lam, w_out, route, s, c, b, nrows_out):
    x, gg8, xin8 = _rg_in(xa, pre, g, mod, w_in, s, b)
    y = _rg_scans(xin8, gg8, conv_w, conv_b, wa, wi, ba, bi, lam, s, c, b)
    return _out_proj(y, x, mod, w_out, route, nrows_out, s, b)


def _rope_tables(s):
    pos = jnp.arange(s, dtype=F32)
    row = jnp.floor(pos / GRID_W)
    col = pos - row * GRID_W
    n_freq = HEAD_DIM // 4
    inv = ROPE_THETA ** (-jnp.arange(n_freq, dtype=F32) * 2.0 / (HEAD_DIM // 2))
    ar = row[:, None] * inv
    ac = col[:, None] * inv
    cos = jnp.concatenate([jnp.cos(ar), jnp.cos(ar), jnp.cos(ac), jnp.cos(ac)], axis=1)
    sin = jnp.concatenate([-jnp.sin(ar), jnp.sin(ar), -jnp.sin(ac), jnp.sin(ac)], axis=1)
    cos = jnp.concatenate([cos, jnp.ones((TM, HEAD_DIM), F32)], axis=0)
    sin = jnp.concatenate([sin, jnp.zeros((TM, HEAD_DIM), F32)], axis=0)
    return cos, sin


def _qkv_kernel(*refs, has_pre):
    x_ref, pre_refs, (g_ref, mod_ref, w_ref, qg_ref, kg_ref, cos_ref, sin_ref, *outs) = _split_refs(refs, has_pre)
    q_ref, k_ref, v_ref = outs[-3:]
    d = x_ref.shape[-1]
    m = mod_ref[0]
    x = _combined_rows(x_ref, pre_refs)
    if has_pre:
        outs[0][...] = x
    h = _norm_mod(x, g_ref[...], m[:, 0:d], m[:, d:2 * d])
    z = jnp.dot(h.astype(BF16), w_ref[...], preferred_element_type=F32)
    cos = cos_ref[...]
    sin = sin_ref[...]
    src = lax.broadcasted_iota(I32, (HEAD_DIM, HEAD_DIM), 0)
    dst = lax.broadcasted_iota(I32, (HEAD_DIM, HEAD_DIM), 1)
    quarter = HEAD_DIM // 4
    partner_of = jnp.where((dst % (2 * quarter)) < quarter, dst + quarter, dst - quarter)
    swap = jnp.where(src == partner_of, 1.0, 0.0).astype(BF16)

    def head(zc, gain):
        ms = jnp.mean(zc * zc, axis=-1, keepdims=True)
        y = zc * lax.rsqrt(ms + NORM_EPS) * gain
        partner = jnp.dot(y.astype(BF16), swap, preferred_element_type=F32)
        return y * cos + partner * sin

    nq = q_ref.shape[-1] // HEAD_DIM
    nk = k_ref.shape[-1] // HEAD_DIM
    for j in range(nq):
        q_ref[:, j * HEAD_DIM:(j + 1) * HEAD_DIM] = (
            head(z[:, j * HEAD_DIM:(j + 1) * HEAD_DIM], qg_ref[...]) * (HEAD_DIM ** -0.5 * LOG2E)).astype(BF16)
    for j in range(nk):
        c0 = (nq + j) * HEAD_DIM
        k_ref[:, j * HEAD_DIM:(j + 1) * HEAD_DIM] = head(z[:, c0:c0 + HEAD_DIM], kg_ref[...]).astype(BF16)
    v_ref[...] = z[:, (nq + nk) * HEAD_DIM:].astype(BF16)


def _qkv(x, pre, g, mod, w, qg, kg, cos, sin, s, b):
    t, d = x.shape
    nkv = N_KV_HEADS * HEAD_DIM
    n_pos = s // TM
    pre_specs, pre_args = _pre_io(pre, d, TM, s, b)
    row = pl.BlockSpec((TM, d), lambda i: (i, 0))
    outs = pl.pallas_call(
        functools.partial(_qkv_kernel, has_pre=pre is not None),
        grid=(t // TM,),
        in_specs=[row] + pre_specs + [
            pl.BlockSpec((1, d), lambda i: (0, 0)),
            _mod_spec(d, s // TM, b),
            pl.BlockSpec(w.shape, lambda i: (0, 0)),
            pl.BlockSpec((1, HEAD_DIM), lambda i: (0, 0)),
            pl.BlockSpec((1, HEAD_DIM), lambda i: (0, 0)),
            pl.BlockSpec((TM, HEAD_DIM), lambda i: (jnp.where(i < b * n_pos, i % n_pos, n_pos), 0)),
            pl.BlockSpec((TM, HEAD_DIM), lambda i: (jnp.where(i < b * n_pos, i % n_pos, n_pos), 0))],
        out_specs=([row] if pre else []) + [pl.BlockSpec((TM, d), lambda i: (i, 0)),
                                           pl.BlockSpec((TM, nkv), lambda i: (i, 0)),
                                           pl.BlockSpec((TM, nkv), lambda i: (i, 0))],
        out_shape=([jax.ShapeDtypeStruct((t, d), F32)] if pre else [])
        + [jax.ShapeDtypeStruct((t, d), BF16), jax.ShapeDtypeStruct((t, nkv), BF16),
           jax.ShapeDtypeStruct((t, nkv), BF16)],
        compiler_params=_cp("parallel"),
        name="qkv_proj",
    )(x, *pre_args, g, mod, w, qg, kg, cos, sin)
    return tuple(outs) if pre else (x, *outs)


def _attn_kernel(q_ref, kc_ref, vc_ref, *rest, n_lat):
    if n_lat:
        kl_ref, vl_ref, o_ref, s_scr, vaug = rest
    else:
        o_ref, s_scr, vaug = rest
    n_ctx = kc_ref.shape[0]
    tq = q_ref.shape[0]

    def fill_values():
        vaug[:, HEAD_DIM:] = jnp.ones((n_ctx + n_lat, HEAD_DIM), BF16)
        vaug[0:n_ctx, 0:HEAD_DIM] = vc_ref[...]
        if n_lat:
            vaug[n_ctx:, 0:HEAD_DIM] = vl_ref[...]

    if n_lat:
        pl.when(pl.program_id(2) == 0)(fill_values)
    else:
        fill_values()

    chunks = [(0, n_ctx)] + [(n_ctx + j, ATT_KC) for j in range(0, n_lat, ATT_KC)]
    nt = (((1,), (1,)), ((), ()))
    q_all = jnp.concatenate([q_ref[:, g * HEAD_DIM:(g + 1) * HEAD_DIM] for g in range(GQA_GROUP)], axis=0)
    m_part = jnp.full((GQA_GROUP * tq, LANES), -jnp.inf, F32)
    for off, size in chunks:
        keys = kc_ref[...] if off == 0 else kl_ref[off - n_ctx:off - n_ctx + size, :]
        sc = lax.dot_general(q_all, keys, nt, preferred_element_type=F32)
        s_scr[:, off:off + size] = sc
        for j in range(0, size, LANES):
            m_part = jnp.maximum(m_part, sc[:, j:j + LANES])
    m_row = jnp.max(m_part, axis=-1, keepdims=True)
    hr = GQA_GROUP * tq // 2
    acc = [jnp.zeros((hr, 2 * HEAD_DIM), F32), jnp.zeros((hr, 2 * HEAD_DIM), F32)]
    for off, size in chunks:
        for r in range(2):
            rows = slice(r * hr, (r + 1) * hr)
            p = jnp.exp2((s_scr[rows, off:off + size] - m_row[rows]).astype(BF16))
            acc[r] = acc[r] + jnp.dot(p, vaug[off:off + size, :], preferred_element_type=F32)
    for r in range(2):
        out = (acc[r][:, :HEAD_DIM] / acc[r][:, HEAD_DIM:]).astype(BF16)
        for j in range(GQA_GROUP // 2):
            g = r * (GQA_GROUP // 2) + j
            o_ref[:, g * HEAD_DIM:(g + 1) * HEAD_DIM] = out[j * tq:(j + 1) * tq]


def _attn_ctx_kernel(q_ref, kc_ref, vc_ref, o_all_ref, o_ref, s_scr, vaug):
    del o_all_ref
    _attn_kernel(q_ref, kc_ref, vc_ref, o_ref, s_scr, vaug, n_lat=0)


def _attention(q, k, v, s, c, b):
    t, d = q.shape
    gw = GQA_GROUP * HEAD_DIM
    tq = TQ
    nq = s // tq
    assert s % ATT_KC == 0
    ctx_blk = lambda bi, h, *_: ((b * s) // c + bi, h)
    o_lat = pl.pallas_call(
        functools.partial(_attn_kernel, n_lat=s),
        grid=(b, N_KV_HEADS, nq),
        in_specs=[pl.BlockSpec((tq, gw), lambda bi, h, i: (bi * nq + i, h)),
                  pl.BlockSpec((c, HEAD_DIM), ctx_blk),
                  pl.BlockSpec((c, HEAD_DIM), ctx_blk),
                  pl.BlockSpec((s, HEAD_DIM), lambda bi, h, i: (bi, h)),
                  pl.BlockSpec((s, HEAD_DIM), lambda bi, h, i: (bi, h))],
        out_specs=pl.BlockSpec((tq, gw), lambda bi, h, i: (bi * nq + i, h)),
        out_shape=jax.ShapeDtypeStruct((t, d), BF16),
        scratch_shapes=[pltpu.VMEM((GQA_GROUP * tq, c + s), F32), pltpu.VMEM((c + s, 2 * HEAD_DIM), BF16)],
        compiler_params=_cp("parallel", "parallel", "arbitrary"),
        name="attn_lat",
    )(q, k, v, k, v)
    return pl.pallas_call(
        _attn_ctx_kernel,
        grid=(b, N_KV_HEADS),
        in_specs=[pl.BlockSpec((c, gw), ctx_blk),
                  pl.BlockSpec((c, HEAD_DIM), ctx_blk),
                  pl.BlockSpec((c, HEAD_DIM), ctx_blk),
                  pl.BlockSpec(memory_space=pl.ANY)],
        out_specs=pl.BlockSpec((c, gw), ctx_blk),
        out_shape=jax.ShapeDtypeStruct((t, d), BF16),
        scratch_shapes=[pltpu.VMEM((GQA_GROUP * c, c), F32), pltpu.VMEM((c, 2 * HEAD_DIM), BF16)],
        input_output_aliases={3: 0},
        compiler_params=_cp("parallel", "parallel"),
        name="attn_ctx",
    )(q, k, v, o_lat)


def _attention_mixer(xa, pre, g, mod, w_qkv, qg, kg, w_o, route, s, c, b):
    cos, sin = _rope_tables(s)
    x, q, k, v = _qkv(xa, pre, g, mod, w_qkv, qg.reshape(1, -1), kg.reshape(1, -1), cos, sin, s, b)
    o = _attention(q, k, v, s, c, b)
    return _out_proj(o, x, mod, w_o, route, x.shape[0], s, b)


def _gmlp_kernel(*refs, has_pre):
    x_ref, pre_refs, rest = _split_refs(refs, has_pre)
    g_ref, mod_ref, w_in_ref, lng_ref, lnb_ref, ws_ref, bs_ref, w_out_ref = rest[:8]
    route_in, (o_ref, *route_out, uv_ref, cnt_s) = rest[8:11], rest[11:]
    d = x_ref.shape[-1]
    dcm = lng_ref.shape[-1]
    gw = dcm // CM_GROUPS
    x = _combined_rows(x_ref, pre_refs)
    m = mod_ref[0]
    h = _norm_mod(x, g_ref[...], m[:, 0:d], m[:, d:2 * d])
    z = jax.nn.gelu(jnp.dot(h.astype(BF16), w_in_ref[...], preferred_element_type=F32))
    u = z[:, :dcm]
    v = z[:, dcm:]
    mu = jnp.mean(v, axis=-1, keepdims=True)
    vc = v - mu
    var = jnp.mean(vc * vc, axis=-1, keepdims=True)
    vn = (vc * lax.rsqrt(var + NORM_EPS) * lng_ref[...] + lnb_ref[...]).astype(BF16)
    for ck in range(x.shape[0] // CHUNK):
        rows = slice(ck * CHUNK, (ck + 1) * CHUNK)
        for gi in range(CM_GROUPS):
            cols = slice(gi * gw, (gi + 1) * gw)
            mix = jnp.dot(ws_ref[gi], vn[rows, cols], preferred_element_type=F32) + bs_ref[:, gi:gi + 1]
            uv_ref[rows, cols] = (u[rows, cols] * mix).astype(BF16)
    y = jnp.dot(uv_ref[...], w_out_ref[...], preferred_element_type=F32)
    x1 = x + m[:, 2 * d:3 * d] * y
    o_ref[...] = x1
    _route_rows(x1, m, *route_in, *route_out, cnt_s)


def _gmlp_mixer(xa, pre, g, mod, w_in, ln_g, ln_b, w_s, b_s, w_out, route, s, b):
    t, d = xa.shape
    dcm = ln_g.shape[-1]
    full = lambda shape: pl.BlockSpec(shape, lambda i: (0,) * len(shape))
    pre_specs, pre_args = _pre_io(pre, d, TG, s, b)
    r_in, r_out, r_shape, r_scratch = _route_io(d, t, TG)
    x1, *routed = pl.pallas_call(
        functools.partial(_gmlp_kernel, has_pre=pre is not None),
        grid=(t // TG,),
        in_specs=[pl.BlockSpec((TG, d), lambda i: (i, 0))] + pre_specs + [
            full((1, d)),
            _mod_spec(d, s // TG, b),
            full((d, 2 * dcm)), full((1, dcm)), full((1, dcm)),
            full((CM_GROUPS, CHUNK, CHUNK)), full((CHUNK, CM_GROUPS)), full((dcm, d))] + r_in,
        out_specs=[pl.BlockSpec((TG, d), lambda i: (i, 0))] + r_out,
        out_shape=[jax.ShapeDtypeStruct((t, d), F32)] + r_shape,
        scratch_shapes=[pltpu.VMEM((TG, dcm), BF16)] + r_scratch,
        compiler_params=_cp("arbitrary"),
        name="gmlp",
    )(xa, *pre_args, g, mod, w_in, ln_g.reshape(1, dcm), ln_b.reshape(1, dcm),
      w_s.astype(BF16), b_s.T, w_out, *route)
    return x1, routed


def _route_rows(x1, m, g_ref, wr_ref, br_ref, hf_ref, rt_ref, ew_ref, cnt_ref, cnt_s):
    tm, d = x1.shape
    ng = EXPERTS_PER_GROUP

    @pl.when(pl.program_id(0) == 0)
    def _():
        cnt_s[...] = jnp.zeros_like(cnt_s)

    hf = _norm_mod(x1, g_ref[...], m[:, 3 * d:4 * d], m[:, 4 * d:5 * d])
    hf_ref[...] = _pack_bf16_pairs(hf)
    hf_hi = hf.astype(BF16)
    hf_lo = (hf - hf_hi.astype(F32)).astype(BF16)
    nt = (((1,), (1,)), ((), ()))
    by_hi = lax.dot_general(wr_ref[...], hf_hi, nt, preferred_element_type=F32)
    logits = (by_hi[:LANES] + by_hi[LANES:]
              + lax.dot_general(wr_ref[0:LANES, :], hf_lo, nt, preferred_element_type=F32)) + br_ref[...]
    neg = -jnp.inf
    row = lax.broadcasted_iota(I32, (ng, tm), 0)
    grp = logits[N_EXPERTS:N_EXPERTS + ng]
    gmax = jnp.max(grp, axis=0, keepdims=True)
    gsel = jnp.min(jnp.where(grp == gmax, row, ng), axis=0, keepdims=True)
    gate_g = 1.0 / jnp.sum(jnp.exp(grp - gmax), axis=0, keepdims=True)
    el = logits[0:ng]
    for gi in range(1, N_GROUPS):
        el = jnp.where(gsel == gi, logits[gi * ng:(gi + 1) * ng], el)
    v1 = jnp.max(el, axis=0, keepdims=True)
    i1 = jnp.min(jnp.where(el == v1, row, ng), axis=0, keepdims=True)
    el2 = jnp.where(row == i1, neg, el)
    v2 = jnp.max(el2, axis=0, keepdims=True)
    i2 = jnp.min(jnp.where(el2 == v2, row, ng), axis=0, keepdims=True)
    e21 = jnp.exp(v2 - v1)
    w1 = gate_g / (1.0 + e21)
    w2 = w1 * e21
    e1 = gsel * ng + i1
    e2 = gsel * ng + i2
    lane_row = lax.broadcasted_iota(I32, (LANES, tm), 0)
    ew_ref[...] = jnp.where(lane_row == 0, w1, jnp.where(lane_row == 1, w2, 0.0)).T

    expert = lax.broadcasted_iota(I32, (N_EXPERTS, tm), 0)
    oh1 = expert == e1
    oh2 = expert == e2
    earlier = (lax.broadcasted_iota(I32, (tm, tm), 0) < lax.broadcasted_iota(I32, (tm, tm), 1)).astype(BF16)
    pre1 = jnp.dot(jnp.where(oh1, 1.0, 0.0).astype(BF16), earlier, preferred_element_type=F32)
    pre2 = jnp.dot(jnp.where(oh2, 1.0, 0.0).astype(BF16), earlier, preferred_element_type=F32)
    tot1 = jnp.sum(jnp.where(oh1, 1.0, 0.0), axis=1, keepdims=True)
    tot2 = jnp.sum(jnp.where(oh2, 1.0, 0.0), axis=1, keepdims=True)
    cnt = cnt_s[:, 0:1]
    rank1 = jnp.sum(jnp.where(oh1, cnt + pre1, 0.0), axis=0, keepdims=True).astype(I32)
    rank2 = jnp.sum(jnp.where(oh2, cnt + tot1 + pre2, 0.0), axis=0, keepdims=True).astype(I32)
    cnt = jnp.broadcast_to(cnt + tot1 + tot2, cnt_s.shape)
    cnt_s[...] = cnt
    cnt_ref[...] = cnt.astype(I32)
    rt_ref[...] = jnp.where(row == 0, e1, jnp.where(row == 1, e2, jnp.where(row == 2, rank1,
                            jnp.where(row == 3, rank2, 0))))


def _route_params(g_ffn, w_group, b_group, w_router, b_router):
    d = w_group.shape[0]
    pad = LANES - N_EXPERTS - N_GROUPS
    wr = jnp.concatenate([w_router.reshape(d, N_EXPERTS), w_group, jnp.zeros((d, pad), F32)], axis=1).T
    br = jnp.concatenate([b_router.reshape(N_EXPERTS), b_group,
                          jnp.full((EXPERTS_PER_GROUP - N_GROUPS,), -jnp.inf, F32),
                          jnp.zeros((pad - EXPERTS_PER_GROUP + N_GROUPS,), F32)]).reshape(LANES, 1)
    wr_hi = wr.astype(BF16)
    wr_lo = (wr - wr_hi.astype(F32)).astype(BF16)
    return g_ffn.reshape(1, d), jnp.concatenate([wr_hi, wr_lo], axis=0), br


def _route_io(d, nrows, tr):
    const = lambda shape: pl.BlockSpec(shape, lambda i: (0, 0))
    row = lambda w: pl.BlockSpec((tr, w), lambda i: (i, 0))
    in_specs = [const((1, d)), const((2 * LANES, d)), const((LANES, 1))]
    out_specs = [row(d // 2), pl.BlockSpec((SUBLANES, tr), lambda i: (0, i)), row(LANES), const((N_EXPERTS, LANES))]
    out_shape = [jax.ShapeDtypeStruct((nrows, d // 2), I32), jax.ShapeDtypeStruct((SUBLANES, nrows), I32),
                 jax.ShapeDtypeStruct((nrows, LANES), F32), jax.ShapeDtypeStruct((N_EXPERTS, LANES), I32)]
    return in_specs, out_specs, out_shape, [pltpu.VMEM((N_EXPERTS, LANES), F32)]


def _plan_kernel(cnt_ref, rt_ref, pos_ref, blk_e_ref, n_used_ref, first_ref, slot_ref, next_ref,
                 start_s, end_s, nxt_s, *, bm):
    n_blk = blk_e_ref.shape[0]
    acc = jnp.int32(0)
    for e in range(N_EXPERTS):
        start_s[e] = acc
        acc = acc + (cnt_ref[e, 0] + (bm - 1)) // bm * bm
        end_s[e] = acc
    n_used = acc // bm
    n_used_ref[0] = n_used
    nxt = jnp.int32(-1)
    for e in reversed(range(N_EXPERTS)):
        nxt_s[e] = nxt
        nxt = jnp.where(cnt_ref[e, 0] > 0, e, nxt)

    def block(i, carry):
        prev_e, runs = carry
        row = jnp.minimum(i, n_used - 1) * bm
        e = lax.while_loop(lambda v: jnp.logical_and(v < N_EXPERTS - 1, end_s[v] <= row), lambda v: v + 1,
                           jnp.maximum(prev_e, 0))
        first = jnp.logical_and(i < n_used, prev_e != e)
        runs = runs + first.astype(I32)
        blk_e_ref[i] = e
        first_ref[i] = first.astype(I32)
        slot_ref[i] = (runs - 1) % 2
        next_ref[i] = nxt_s[e]
        return e, runs

    lax.fori_loop(0, n_blk, block, (jnp.int32(-1), jnp.int32(0)))

    rt = rt_ref[...]
    start_of = jnp.zeros_like(rt)
    for e in range(N_EXPERTS):
        start_of = jnp.where(rt == e, start_s[e], start_of)
    pos_ref[...] = start_of + pltpu.roll(rt, SUBLANES - 2, 0)


def _dispatch_plan(rt, cnt, bm):
    n_tok = rt.shape[1]
    n_rows = 2 * n_tok + N_EXPERTS * bm
    n_blk = n_rows // bm
    smem = pl.BlockSpec(memory_space=pltpu.SMEM)
    vec = lambda n: jax.ShapeDtypeStruct((n,), I32)
    pos, blk_e, n_used, first, slot, nxt = pl.pallas_call(
        functools.partial(_plan_kernel, bm=bm),
        in_specs=[smem, pl.BlockSpec(memory_space=pltpu.VMEM)],
        out_specs=[pl.BlockSpec(memory_space=pltpu.VMEM), smem, smem, smem, smem, smem],
        out_shape=[jax.ShapeDtypeStruct(rt.shape, I32), vec(n_blk), vec(1), vec(n_blk), vec(n_blk), vec(n_blk)],
        scratch_shapes=[pltpu.SMEM((N_EXPERTS,), I32)] * 3,
        name="moe_plan",
    )(cnt, rt)
    return pos, (blk_e, n_used, first, slot, nxt), n_rows


def _sc_mesh():
    return plsc.VectorSubcoreMesh(core_axis_name="c", subcore_axis_name="s")


def _sc_worker_base(per_worker):
    return (lax.axis_index("s") * SC_CORES + lax.axis_index("c")) * per_worker


def _sc_dispatch(hf, pos, n_rows):
    t, d = hf.shape
    per_w = t // SC_WORKERS
    ch = SC_CHUNK
    n_ck = per_w // ch
    assert per_w * SC_WORKERS == t and n_ck * ch == per_w

    @functools.partial(
        pl.kernel, mesh=_sc_mesh(), out_type=jax.ShapeDtypeStruct((n_rows, d), hf.dtype),
        scratch_types=[pltpu.VMEM((per_w,), I32), pltpu.VMEM((per_w,), I32), pltpu.VMEM((2, ch, d), hf.dtype),
                       pltpu.SemaphoreType.DMA((2,)), pltpu.SemaphoreType.DMA((2,)), pltpu.SemaphoreType.DMA((2,))])
    def dispatch(hf_hbm, p_hbm, out_hbm, i0_v, i1_v, rows_v, sem_in, sem_s0, sem_s1):
        base = pl.multiple_of(_sc_worker_base(per_w), SUBLANES)
        pltpu.sync_copy(p_hbm.at[pl.ds(base, per_w)], i0_v)
        pltpu.sync_copy(p_hbm.at[pl.ds(pl.multiple_of(t + base, SUBLANES), per_w)], i1_v)

        def load(ck):
            return pltpu.make_async_copy(hf_hbm.at[pl.ds(base + ck * ch, ch)], rows_v.at[ck % 2], sem_in.at[ck % 2])

        def scatters(ck):
            src = rows_v.at[ck % 2]
            return (pltpu.make_async_copy(src, out_hbm.at[i0_v.at[pl.ds(ck * ch, ch)]], sem_s0.at[ck % 2]),
                    pltpu.make_async_copy(src, out_hbm.at[i1_v.at[pl.ds(ck * ch, ch)]], sem_s1.at[ck % 2]))

        load(0).start()
        for ck in range(n_ck):
            load(ck).wait()
            if ck + 1 < n_ck:
                if ck >= 1:
                    for cp in scatters(ck - 1):
                        cp.wait()
                load(ck + 1).start()
            for cp in scatters(ck):
                cp.start()
        for ck in range(max(n_ck - 2, 0), n_ck):
            for cp in scatters(ck):
                cp.wait()

    return dispatch(hf, pos)


def _sc_gather(rows, idx):
    n = idx.shape[0]
    d = rows.shape[1]
    per_w = n // SC_WORKERS
    ch = 2 * SC_CHUNK
    n_ck = per_w // ch
    assert per_w * SC_WORKERS == n and n_ck * ch == per_w

    @functools.partial(
        pl.kernel, mesh=_sc_mesh(), out_type=jax.ShapeDtypeStruct((n, d), rows.dtype),
        scratch_types=[pltpu.VMEM((per_w,), I32), pltpu.VMEM((2, ch, d), rows.dtype),
                       pltpu.SemaphoreType.DMA((2,)), pltpu.SemaphoreType.DMA((2,))])
    def gather(rows_hbm, i_hbm, out_hbm, i_v, buf, sem_g, sem_w):
        base = pl.multiple_of(_sc_worker_base(per_w), SUBLANES)
        pltpu.sync_copy(i_hbm.at[pl.ds(base, per_w)], i_v)

        def fetch(ck):
            return pltpu.make_async_copy(rows_hbm.at[i_v.at[pl.ds(ck * ch, ch)]], buf.at[ck % 2], sem_g.at[ck % 2])

        def write(ck):
            return pltpu.make_async_copy(buf.at[ck % 2], out_hbm.at[pl.ds(base + ck * ch, ch)], sem_w.at[ck % 2])

        fetch(0).start()
        for ck in range(n_ck):
            fetch(ck).wait()
            if ck + 1 < n_ck:
                if ck >= 1:
                    write(ck - 1).wait()
                fetch(ck + 1).start()
            write(ck).start()
        for ck in range(max(n_ck - 2, 0), n_ck):
            write(ck).wait()

    return gather(rows, idx)


def _expert_kernel(blk_e_ref, n_used_ref, first_ref, slot_ref, next_ref, x_ref, wg_hbm, wu_hbm, wd_hbm, y_ref,
                   wgf, wuf, wdf, wgb, wub, wdb, sem, *, e_base):
    n_used = n_used_ref[0]
    bm = x_ref.shape[0] // MOE_SUB

    def weight_copies(e, slot):
        return (pltpu.make_async_copy(wg_hbm.at[e_base + e], wgf.at[slot], sem.at[slot, 0]),
                pltpu.make_async_copy(wu_hbm.at[e_base + e], wuf.at[slot], sem.at[slot, 1]),
                pltpu.make_async_copy(wd_hbm.at[e_base + e], wdf.at[slot], sem.at[slot, 2]))

    @pl.when(pl.program_id(0) == 0)
    def _():
        for cp in weight_copies(blk_e_ref[0], 0):
            cp.start()

    for j in range(MOE_SUB):
        blk = pl.program_id(0) * MOE_SUB + j
        rows = slice(j * bm, (j + 1) * bm)

        @pl.when(jnp.logical_and(blk < n_used, first_ref[blk] == 1))
        def _():
            slot = slot_ref[blk]
            for cp in weight_copies(blk_e_ref[blk], slot):
                cp.wait()
            nxt = next_ref[blk]

            @pl.when(nxt >= 0)
            def _():
                for cp in weight_copies(nxt, 1 - slot):
                    cp.start()

            wgb[...] = wgf[slot].astype(BF16)
            wub[...] = wuf[slot].astype(BF16)
            wdb[...] = wdf[slot].astype(BF16)

        @pl.when(blk < n_used)
        def _():
            x_hi, x_lo = _unpack_bf16_pairs(x_ref[rows, :])
            xb = jnp.concatenate([x_hi.astype(BF16), x_lo.astype(BF16)], axis=1)
            gt = jnp.dot(xb, wgb[...], preferred_element_type=F32)
            up = jnp.dot(xb, wub[...], preferred_element_type=F32)
            act = (gt * _sigmoid(gt) * up).astype(BF16)
            y_ref[rows, :] = _pack_bf16_pairs(jnp.dot(act, wdb[...], preferred_element_type=F32))


def _experts(x_rows, plan, w_gate, w_up, w_down, layer):
    n_rows, dp = x_rows.shape
    depth, n_e, d, de = w_gate.shape
    step_rows = MOE_SUB * MOE_BM
    assert n_rows % step_rows == 0
    any_spec = pl.BlockSpec(memory_space=pl.ANY)
    last_used = lambda i, be, nu, *_: (jnp.minimum(i, (nu[0] - 1) // MOE_SUB), 0)
    grid_spec = pltpu.PrefetchScalarGridSpec(
        num_scalar_prefetch=5,
        grid=(n_rows // step_rows,),
        in_specs=[pl.BlockSpec((step_rows, dp), last_used), any_spec, any_spec, any_spec],
        out_specs=pl.BlockSpec((step_rows, dp), last_used),
        scratch_shapes=[pltpu.VMEM((2, d, de), F32), pltpu.VMEM((2, d, de), F32), pltpu.VMEM((2, de, d), F32),
                        pltpu.VMEM((d, de), BF16), pltpu.VMEM((d, de), BF16), pltpu.VMEM((de, d), BF16),
                        pltpu.SemaphoreType.DMA((2, 3))],
    )
    return pl.pallas_call(
        functools.partial(_expert_kernel, e_base=layer * n_e),
        grid_spec=grid_spec,
        out_shape=jax.ShapeDtypeStruct((n_rows, dp), I32),
        compiler_params=_cp("arbitrary"),
        name="moe_experts",
    )(*plan, x_rows, w_gate.reshape(depth * n_e, d, de), w_up.reshape(depth * n_e, d, de),
      w_down.reshape(depth * n_e, de, d))


def _combine_kernel(x_ref, *rest):
    *pre_refs, o_ref = rest
    o_ref[...] = _combined_rows(x_ref, pre_refs)


def _combine(x, pre, s, b):
    nrows, d = x.shape
    pre_specs, pre_args = _pre_io(pre, d, TR, s, b)
    return pl.pallas_call(
        _combine_kernel,
        grid=(nrows // TR,),
        in_specs=[pl.BlockSpec((TR, d), lambda i: (i, 0))] + pre_specs,
        out_specs=pl.BlockSpec((TR, d), lambda i: (i, 0)),
        out_shape=jax.ShapeDtypeStruct((nrows, d), F32),
        compiler_params=_cp("parallel"),
        name="moe_combine",
    )(x, *pre_args)


def _expert_outputs(routed, w_gate, w_up, w_down, layer):
    hf, rt, _, cnt = routed
    pos, plan, n_rows = _dispatch_plan(rt, cnt, MOE_BM)
    pos = pos[0:2].reshape(-1)
    x_rows = _sc_dispatch(hf, pos, n_rows)
    y_rows = _experts(x_rows, plan, w_gate, w_up, w_down, layer)
    return _sc_gather(y_rows, pos)


def kernel(x, c, ctx, c_ctx, ada_w, ada_b, norm_mix_g, norm_ffn_g, rg_w_in, rg_conv_w, rg_conv_b, rg_wa, rg_ba, rg_wi, rg_bi, rg_lambda, rg_w_out, at_w_qkv, at_q_g, at_k_g, at_w_o, cm_w_in, cm_ln_g, cm_ln_b, cm_w_s, cm_b_s, cm_w_out, moe_w_group, moe_b_group, moe_w_router, moe_b_router, moe_w_gate, moe_w_up, moe_w_down):
    b, s, d = x.shape
    cl = ctx.shape[1]
    depth = ada_w.shape[0]
    n_lat = b * s
    assert b < SUBLANES and s % max(TI, TR, TG) == 0 and (b * cl) % max(TI, TR, TG) == 0 and cl % TM == 0
    assert d == RG_BLOCKS * LANES

    cin = jnp.concatenate([c, c_ctx[None, :], jnp.zeros((SUBLANES - b - 1, d), F32)], axis=0)
    mod_all = _ada_table(cin, ada_w, ada_b).reshape(depth, SUBLANES, 1, N_MOD * d)
    n_tok = n_lat + b * cl

    xa, pre = (x.reshape(n_lat, d), ctx.reshape(b * cl, d)), None
    for l in range(depth):
        kind = l % 3
        j = l // 3
        last = l == depth - 1
        mod = mod_all[l]
        g_mix = norm_mix_g[l].reshape(1, d)
        route = _route_params(norm_ffn_g[l], moe_w_group[l], moe_b_group[l], moe_w_router[l], moe_b_router[l])
        if kind == 0:
            x1, routed = _rglru_mixer(xa, pre, g_mix, mod, _bf16_weight(rg_w_in, j), rg_conv_w[j], rg_conv_b[j],
                                      rg_wa[j], rg_ba[j], rg_wi[j], rg_bi[j], rg_lambda[j],
                                      _bf16_weight(rg_w_out, j), route, s, cl, b, n_lat if last else n_tok)
        elif kind == 1:
            x1, routed = _attention_mixer(xa, pre, g_mix, mod, _bf16_weight(at_w_qkv, j), at_q_g[j], at_k_g[j],
                                          _bf16_weight(at_w_o, j), route, s, cl, b)
        else:
            x1, routed = _gmlp_mixer(xa, pre, g_mix, mod, _bf16_weight(cm_w_in, j), cm_ln_g[j], cm_ln_b[j],
                                     cm_w_s[j], cm_b_s[j], _bf16_weight(cm_w_out, j), route, s, b)
        y01 = _expert_outputs(routed, moe_w_gate, moe_w_up, moe_w_down, l)
        xa, pre = x1, (y01, routed[2], mod)
    return _combine(xa, pre, s, b)[:n_lat].reshape(b, s, d)
```

```python
import functools

import jax
import jax.numpy as jnp
from jax import lax
from jax.experimental import pallas as pl
from jax.experimental.pallas import tpu as pltpu
from jax.experimental.pallas import tpu_sc as plsc

F32 = jnp.float32
BF16 = jnp.bfloat16
I32 = jnp.int32
U32 = jnp.uint32

NORM_EPS = 1e-6
N_MOD = 6
GRID_W = 64
RG_BLOCKS = 8
CONV_W = 4
RG_C = 8.0
HEAD_DIM = 128
N_KV_HEADS = 2
GQA_GROUP = 4
ROPE_THETA = 10000.0
CHUNK = 128
CM_GROUPS = 8
N_GROUPS = 4
EXPERTS_PER_GROUP = 8
N_EXPERTS = N_GROUPS * EXPERTS_PER_GROUP

LANES = 128
SUBLANES = 8
TM = 256
TQ = 512
TI = 1024
TR = 1024
TG = 512
CAST_ROWS = 512
TL = 256
HALO = 8
ATT_KC = 512
LOG2E = 1.4426950408889634
MOE_BM = 256
MOE_SUB = 4
SC_CORES = 2
SC_WORKERS = 32
SC_CHUNK = 32
VMEM_LIMIT = 52 * 2**20


def _cp(*sem):
    return pltpu.CompilerParams(dimension_semantics=sem, vmem_limit_bytes=VMEM_LIMIT)


def _norm_mod(x, g, shift, scale):
    ms = jnp.mean(x * x, axis=-1, keepdims=True)
    y = x * lax.rsqrt(ms + NORM_EPS) * g
    return y * (1.0 + scale) + shift


def _sigmoid(x):
    return 0.5 * jnp.tanh(0.5 * x) + 0.5


def _pack_bf16_pairs(x):
    h = x.shape[-1] // 2
    hi = lax.bitcast_convert_type(x[:, :h].astype(BF16).astype(F32), U32)
    lo = lax.bitcast_convert_type(x[:, h:].astype(BF16).astype(F32), U32)
    return lax.bitcast_convert_type(hi | (lo >> 16), I32)


def _unpack_bf16_pairs(w):
    u = lax.bitcast_convert_type(w, U32)
    hi = lax.bitcast_convert_type(u & jnp.uint32(0xFFFF0000), F32)
    lo = lax.bitcast_convert_type(u << 16, F32)
    return hi, lo


def _combined_rows(x_ref, pre_refs):
    if not pre_refs:
        return x_ref[...]
    y0_ref, y1_ref, ew_ref, modp_ref = pre_refs
    d = x_ref.shape[1]
    ew = ew_ref[...]
    y0_hi, y0_lo = _unpack_bf16_pairs(y0_ref[...])
    y1_hi, y1_lo = _unpack_bf16_pairs(y1_ref[...])
    y = jnp.concatenate([ew[:, 0:1] * y0_hi + ew[:, 1:2] * y1_hi, ew[:, 0:1] * y0_lo + ew[:, 1:2] * y1_lo], axis=1)
    return x_ref[...] + modp_ref[0][:, 5 * d:6 * d] * y


def _pre_io(pre, d, tr, s, b):
    if pre is None:
        return [], []
    y01, ew, mod_prev = pre
    nb = y01.shape[0] // 2 // tr
    specs = [pl.BlockSpec((tr, d // 2), lambda i: (i, 0)), pl.BlockSpec((tr, d // 2), lambda i: (i + nb, 0)),
             pl.BlockSpec((tr, LANES), lambda i: (i, 0)), _mod_spec(d, s // tr, b)]
    return specs, [y01, y01, ew, mod_prev]


def _split_refs(refs, has_pre):
    return (refs[0], refs[1:5], refs[5:]) if has_pre else (refs[0], (), refs[1:])


def _mod_spec(d, rows_per_sample, n_samples):
    return pl.BlockSpec((1, 1, N_MOD * d),
                        lambda i, *_: (jnp.minimum(i // rows_per_sample, n_samples), 0, 0))


def _cast_kernel(w_ref, o_ref):
    o_ref[...] = w_ref[0].astype(BF16)


def _bf16_weight(w_stack, j):
    _, r, c = w_stack.shape
    return pl.pallas_call(
        _cast_kernel,
        grid=(r // CAST_ROWS,),
        in_specs=[pl.BlockSpec((1, CAST_ROWS, c), lambda i: (j, i, 0))],
        out_specs=pl.BlockSpec((CAST_ROWS, c), lambda i: (i, 0)),
        out_shape=jax.ShapeDtypeStruct((r, c), BF16),
        compiler_params=_cp("parallel"),
        name="weight_bf16",
    )(w_stack)


def _ada_kernel(c_ref, w_ref, b_ref, o_ref):
    cin = c_ref[...]
    act = cin * jax.nn.sigmoid(cin)
    w = w_ref[0]
    w_hi = w.astype(BF16)
    w_lo = (w - w_hi.astype(F32)).astype(BF16)
    a_hi = act.astype(BF16)
    a_lo = (act - a_hi.astype(F32)).astype(BF16)
    o_ref[0] = (jnp.dot(a_hi, w_hi, preferred_element_type=F32) + jnp.dot(a_lo, w_hi, preferred_element_type=F32)
                + jnp.dot(a_hi, w_lo, preferred_element_type=F32)) + b_ref[0]


def _ada_table(cin, ada_w, ada_b):
    depth, d, n = ada_w.shape
    tn = 2 * d
    return pl.pallas_call(
        _ada_kernel,
        grid=(depth, n // tn),
        in_specs=[pl.BlockSpec((SUBLANES, d), lambda l, j: (0, 0)),
                  pl.BlockSpec((1, d, tn), lambda l, j: (l, 0, j)),
                  pl.BlockSpec((1, 1, tn), lambda l, j: (l, 0, j))],
        out_specs=pl.BlockSpec((1, SUBLANES, tn), lambda l, j: (l, 0, j)),
        out_shape=jax.ShapeDtypeStruct((depth, SUBLANES, n), F32),
        compiler_params=_cp("parallel", "parallel"),
        name="ada_table",
    )(cin, ada_w, ada_b.reshape(depth, 1, n))


def _rows_io(x, tr):
    if not isinstance(x, tuple):
        return [pl.BlockSpec((tr, x.shape[1]), lambda i: (i, 0))], [x], 0
    lat, cx = x
    nl = lat.shape[0] // tr
    assert nl * tr == lat.shape[0] and cx.shape[0] % tr == 0
    specs = [pl.BlockSpec((tr, lat.shape[1]), lambda i: (jnp.minimum(i, nl - 1), 0)),
             pl.BlockSpec((tr, lat.shape[1]), lambda i: (jnp.maximum(i - nl, 0), 0))]
    return specs, [lat, cx], nl


def _rows_value(x_refs, n_lat_blocks):
    if not n_lat_blocks:
        return x_refs[0][...]
    return jnp.where(pl.program_id(0) < n_lat_blocks, x_refs[0][...], x_refs[1][...])


def _out_kernel(y_ref, *rest, n_lat_blocks):
    n_src = 2 if n_lat_blocks else 1
    x_refs, (mod_ref, w_ref, *rest) = rest[:n_src], rest[n_src:]
    route_in, (o_ref, *route_out) = rest[:3], rest[3:]
    d = o_ref.shape[-1]
    m = mod_ref[0]
    y = jnp.dot(y_ref[...].astype(BF16), w_ref[...], preferred_element_type=F32)
    x1 = _rows_value(x_refs, n_lat_blocks) + m[:, 2 * d:3 * d] * y
    o_ref[...] = x1
    _route_rows(x1, m, *route_in, *route_out)


def _out_proj(y, x, mod, w, route, nrows, s, b):
    k, d = w.shape
    r_in, r_out, r_shape, r_scratch = _route_io(d, nrows, TR)
    x_specs, x_args, n_lat_blocks = _rows_io(x, TR)
    x1, *routed = pl.pallas_call(
        functools.partial(_out_kernel, n_lat_blocks=n_lat_blocks),
        grid=(nrows // TR,),
        in_specs=[pl.BlockSpec((TR, k), lambda i: (i, 0))] + x_specs + [
            _mod_spec(d, s // TR, b), pl.BlockSpec((k, d), lambda i: (0, 0))] + r_in,
        out_specs=[pl.BlockSpec((TR, d), lambda i: (i, 0))] + r_out,
        out_shape=[jax.ShapeDtypeStruct((nrows, d), F32)] + r_shape,
        scratch_shapes=r_scratch,
        compiler_params=_cp("arbitrary"),
        name="out_proj",
    )(y, *x_args, mod, w, *route)
    return x1, routed


def _rg_in_kernel(*refs, has_pre, n_lat_blocks):
    if n_lat_blocks:
        x_ref, (g_ref, mod_ref, w_ref, *outs) = refs[0], refs[2:]
        x = _rows_value(refs[:2], n_lat_blocks)
    else:
        x_ref, pre_refs, (g_ref, mod_ref, w_ref, *outs) = _split_refs(refs, has_pre)
        x = _combined_rows(x_ref, pre_refs)
    gg_ref, xin_ref = outs[-2:]
    d = x_ref.shape[-1]
    m = mod_ref[0]
    if has_pre:
        outs[0][...] = x
    h = _norm_mod(x, g_ref[...], m[:, 0:d], m[:, d:2 * d])
    z = jnp.dot(h.astype(BF16), w_ref[...], preferred_element_type=F32)
    tm = x_ref.shape[0]
    for n in range(d // LANES):
        cols = slice(n * LANES, (n + 1) * LANES)
        gg_ref[pl.ds(n, tm, stride=SUBLANES), :] = jax.nn.gelu(z[:, cols])
        xin_ref[pl.ds(n, tm, stride=SUBLANES), :] = z[:, d + n * LANES:d + (n + 1) * LANES]


def _rg_in(x, pre, g, mod, w, s, b):
    d = w.shape[0]
    assert d == SUBLANES * LANES
    x_specs, x_args, n_lat_blocks = _rows_io(x, TI)
    assert not (n_lat_blocks and pre)
    t = sum(a.shape[0] for a in x_args)
    pre_specs, pre_args = _pre_io(pre, d, TI, s, b)
    row = pl.BlockSpec((TI, d), lambda i: (i, 0))
    tmajor = pl.BlockSpec((TI * SUBLANES, LANES), lambda i: (i, 0))
    outs = pl.pallas_call(
        functools.partial(_rg_in_kernel, has_pre=pre is not None, n_lat_blocks=n_lat_blocks),
        grid=(t // TI,),
        in_specs=x_specs + pre_specs + [pl.BlockSpec((1, d), lambda i: (0, 0)), _mod_spec(d, s // TI, b),
                                        pl.BlockSpec((d, 2 * d), lambda i: (0, 0))],
        out_specs=([row] if pre else []) + [tmajor, tmajor],
        out_shape=([jax.ShapeDtypeStruct((t, d), F32)] if pre else [])
        + [jax.ShapeDtypeStruct((t * SUBLANES, LANES), F32)] * 2,
        compiler_params=_cp("parallel"),
        name="rg_in",
    )(*x_args, *pre_args, g, mod, w)
    return (outs[0], outs[1], outs[2]) if pre else (x, outs[0], outs[1])


def _rg_gates_and_scan(xc, wa_ref, wi_ref, ba_ref, bi_ref, lam_ref, a_s, b_s, h_dst, hcar, reverse):
    @pl.when(pl.program_id(1) == 0)
    def _():
        hcar[...] = jnp.zeros_like(hcar)

    for n in range(RG_BLOCKS):
        cols = slice(n * LANES, (n + 1) * LANES)
        xn = xc[pl.ds(n, TL, stride=SUBLANES), :]
        xb = xn.astype(BF16)
        ta = jnp.tanh(jnp.dot(xb, wa_ref[n], preferred_element_type=F32) + ba_ref[:, cols])
        ti = jnp.tanh(jnp.dot(xb, wi_ref[n], preferred_element_type=F32) + bi_ref[:, cols])
        k = (-0.5 * RG_C * LOG2E) * jax.nn.softplus(-lam_ref[:, cols])
        a = jnp.exp2(k * ta + k)
        om = 1.0 - a * a
        root = jnp.where(om > 0.0, om * lax.rsqrt(om), 0.0)
        a_s[pl.ds(n, TL, stride=SUBLANES), :] = a
        b_s[pl.ds(n, TL, stride=SUBLANES), :] = root * (0.5 * xn) * (ti + 1.0)

    def two_steps(p, h):
        t0 = (TL - 1 - 2 * p) if reverse else 2 * p
        t1 = (t0 - 1) if reverse else (t0 + 1)
        r0 = pl.multiple_of(t0 * SUBLANES, SUBLANES)
        r1 = pl.multiple_of(t1 * SUBLANES, SUBLANES)
        a0 = a_s[pl.ds(r0, SUBLANES), :]
        b0 = b_s[pl.ds(r0, SUBLANES), :]
        a1 = a_s[pl.ds(r1, SUBLANES), :]
        b1 = b_s[pl.ds(r1, SUBLANES), :]
        h_dst[pl.ds(r0, SUBLANES), :] = a0 * h + b0
        h2 = (a1 * a0) * h + (a1 * b0 + b1)
        h_dst[pl.ds(r1, SUBLANES), :] = h2
        return h2

    hcar[...] = lax.fori_loop(0, TL // 2, two_steps, hcar[...], unroll=8)


def _rg_fwd_kernel(xm_ref, xprev_ref, xnext_ref, cw_ref, cb_ref, wa_ref, wi_ref, ba_ref, bi_ref, lam_ref,
                   hf_ref, xc_ref, xpad, a_s, b_s, hcar, *, nlat):
    rows = TL * SUBLANES
    hrows = HALO * SUBLANES
    j = pl.program_id(1)
    has_prev = j >= 2
    has_next = jnp.logical_and(j >= 1, j < nlat)
    xpad[0:hrows, :] = jnp.where(has_prev, xprev_ref[...], 0.0)
    xpad[hrows:hrows + rows, :] = xm_ref[...]
    xpad[hrows + rows:2 * hrows + rows, :] = jnp.where(has_next, xnext_ref[...], 0.0)
    acc = jnp.broadcast_to(cb_ref[...][None], (TL, SUBLANES, LANES))
    for k in range(CONV_W):
        off = (HALO + k - CONV_W // 2) * SUBLANES
        tap = xpad[off:off + rows, :].reshape(TL, SUBLANES, LANES)
        acc = acc + tap * cw_ref[k][None]
    xc_ref[...] = acc.reshape(rows, LANES)
    _rg_gates_and_scan(xc_ref, wa_ref, wi_ref, ba_ref, bi_ref, lam_ref, a_s, b_s, hf_ref, hcar, False)


def _rg_bwd_kernel(xc_ref, wa_ref, wi_ref, ba_ref, bi_ref, lam_ref, hf_ref, gg_ref, out_ref, a_s, b_s, h_s, hcar):
    _rg_gates_and_scan(xc_ref, wa_ref, wi_ref, ba_ref, bi_ref, lam_ref, a_s, b_s, h_s, hcar, True)
    h_s[...] = gg_ref[...] * (hf_ref[...] + h_s[...])
    for n in range(RG_BLOCKS):
        out_ref[:, n * LANES:(n + 1) * LANES] = h_s[pl.ds(n, TL, stride=SUBLANES), :].astype(BF16)


def _rg_scans(xin8, gg8, conv_w, conv_b, wa, wi, ba, bi, lam, s, c, b):
    assert c == TL and s % TL == 0
    rows = TL * SUBLANES
    hrows = HALO * SUBLANES
    nlat = s // TL
    t = xin8.shape[0] // SUBLANES
    n_halo = t // HALO
    d = RG_BLOCKS * LANES

    def chunk(reverse):
        return lambda bi_, j: jnp.where(j == 0, (b * s) // TL + bi_,
                                        bi_ * nlat + ((nlat - j) if reverse else (j - 1)))

    fwd, bwd = chunk(False), chunk(True)
    main_f = pl.BlockSpec((rows, LANES), lambda bi_, j: (fwd(bi_, j), 0))
    main_b = pl.BlockSpec((rows, LANES), lambda bi_, j: (bwd(bi_, j), 0))
    prev = pl.BlockSpec((hrows, LANES), lambda bi_, j: (jnp.maximum(fwd(bi_, j) * (TL // HALO) - 1, 0), 0))
    nxt = pl.BlockSpec((hrows, LANES),
                       lambda bi_, j: (jnp.minimum((fwd(bi_, j) + 1) * (TL // HALO), n_halo - 1), 0))
    full = lambda shape: pl.BlockSpec(shape, lambda bi_, j: (0,) * len(shape))
    gate_specs = [full((RG_BLOCKS, LANES, LANES)), full((RG_BLOCKS, LANES, LANES)),
                  full((1, d)), full((1, d)), full((1, d))]
    gate_args = lambda k: [(0.5 * wa[k]).astype(BF16), (0.5 * wi[k]).astype(BF16), 0.5 * ba[k].reshape(1, d),
                           0.5 * bi[k].reshape(1, d), lam[k].reshape(1, d)]
    tmajor = jax.ShapeDtypeStruct(xin8.shape, F32)
    buf = pltpu.VMEM((rows, LANES), F32)
    hf8, xc8 = pl.pallas_call(
        functools.partial(_rg_fwd_kernel, nlat=nlat),
        grid=(b, nlat + 1),
        in_specs=[main_f, prev, nxt, full((CONV_W, SUBLANES, LANES)), full((SUBLANES, LANES))] + gate_specs,
        out_specs=[main_f, main_f],
        out_shape=[tmajor, tmajor],
        scratch_shapes=[pltpu.VMEM((rows + 2 * hrows, LANES), F32), buf, buf, pltpu.VMEM((SUBLANES, LANES), F32)],
        compiler_params=_cp("parallel", "arbitrary"),
        name="rg_scan_fwd",
    )(xin8, xin8, xin8, conv_w.reshape(CONV_W, SUBLANES, LANES), conv_b.reshape(SUBLANES, LANES), *gate_args(0))
    return pl.pallas_call(
        _rg_bwd_kernel,
        grid=(b, nlat + 1),
        in_specs=[main_b] + gate_specs + [main_b, main_b],
        out_specs=pl.BlockSpec((TL, d), lambda bi_, j: (bwd(bi_, j), 0)),
        out_shape=jax.ShapeDtypeStruct((t, d), BF16),
        scratch_shapes=[buf, buf, buf, pltpu.VMEM((SUBLANES, LANES), F32)],
        compiler_params=_cp("parallel", "arbitrary"),
        name="rg_scan_bwd",
    )(xc8, *gate_args(1), hf8, gg8)


def _rglru_mixer(xa, pre, g, mod, w_in, conv_w, conv_b, wa, ba, wi, bi, lam, w_out, route, s, c, b, nrows_out):
    x, gg8, xin8 = _rg_in(xa, pre, g, mod, w_in, s, b)
    y = _rg_scans(xin8, gg8, conv_w, conv_b, wa, wi, ba, bi, lam, s, c, b)
    return _out_proj(y, x, mod, w_out, route, nrows_out, s, b)


def _rope_tables(s):
    pos = jnp.arange(s, dtype=F32)
    row = jnp.floor(pos / GRID_W)
    col = pos - row * GRID_W
    n_freq = HEAD_DIM // 4
    inv = ROPE_THETA ** (-jnp.arange(n_freq, dtype=F32) * 2.0 / (HEAD_DIM // 2))
    ar = row[:, None] * inv
    ac = col[:, None] * inv
    cos = jnp.concatenate([jnp.cos(ar), jnp.cos(ar), jnp.cos(ac), jnp.cos(ac)], axis=1)
    sin = jnp.concatenate([-jnp.sin(ar), jnp.sin(ar), -jnp.sin(ac), jnp.sin(ac)], axis=1)
    cos = jnp.concatenate([cos, jnp.ones((TM, HEAD_DIM), F32)], axis=0)
    sin = jnp.concatenate([sin, jnp.zeros((TM, HEAD_DIM), F32)], axis=0)
    return cos, sin


def _qkv_kernel(*refs, has_pre):
    x_ref, pre_refs, (g_ref, mod_ref, w_ref, qg_ref, kg_ref, cos_ref, sin_ref, *outs) = _split_refs(refs, has_pre)
    q_ref, k_ref, v_ref = outs[-3:]
    d = x_ref.shape[-1]
    m = mod_ref[0]
    x = _combined_rows(x_ref, pre_refs)
    if has_pre:
        outs[0][...] = x
    h = _norm_mod(x, g_ref[...], m[:, 0:d], m[:, d:2 * d])
    z = jnp.dot(h.astype(BF16), w_ref[...], preferred_element_type=F32)
    cos = cos_ref[...]
    sin = sin_ref[...]
    src = lax.broadcasted_iota(I32, (HEAD_DIM, HEAD_DIM), 0)
    dst = lax.broadcasted_iota(I32, (HEAD_DIM, HEAD_DIM), 1)
    quarter = HEAD_DIM // 4
    partner_of = jnp.where((dst % (2 * quarter)) < quarter, dst + quarter, dst - quarter)
    swap = jnp.where(src == partner_of, 1.0, 0.0).astype(BF16)

    def head(zc, gain):
        ms = jnp.mean(zc * zc, axis=-1, keepdims=True)
        y = zc * lax.rsqrt(ms + NORM_EPS) * gain
        partner = jnp.dot(y.astype(BF16), swap, preferred_element_type=F32)
        return y * cos + partner * sin

    nq = q_ref.shape[-1] // HEAD_DIM
    nk = k_ref.shape[-1] // HEAD_DIM
    for j in range(nq):
        q_ref[:, j * HEAD_DIM:(j + 1) * HEAD_DIM] = (
            head(z[:, j * HEAD_DIM:(j + 1) * HEAD_DIM], qg_ref[...]) * (HEAD_DIM ** -0.5 * LOG2E)).astype(BF16)
    for j in range(nk):
        c0 = (nq + j) * HEAD_DIM
        k_ref[:, j * HEAD_DIM:(j + 1) * HEAD_DIM] = head(z[:, c0:c0 + HEAD_DIM], kg_ref[...]).astype(BF16)
    v_ref[...] = z[:, (nq + nk) * HEAD_DIM:].astype(BF16)


def _qkv(x, pre, g, mod, w, qg, kg, cos, sin, s, b):
    t, d = x.shape
    nkv = N_KV_HEADS * HEAD_DIM
    n_pos = s // TM
    pre_specs, pre_args = _pre_io(pre, d, TM, s, b)
    row = pl.BlockSpec((TM, d), lambda i: (i, 0))
    outs = pl.pallas_call(
        functools.partial(_qkv_kernel, has_pre=pre is not None),
        grid=(t // TM,),
        in_specs=[row] + pre_specs + [
            pl.BlockSpec((1, d), lambda i: (0, 0)),
            _mod_spec(d, s // TM, b),
            pl.BlockSpec(w.shape, lambda i: (0, 0)),
            pl.BlockSpec((1, HEAD_DIM), lambda i: (0, 0)),
            pl.BlockSpec((1, HEAD_DIM), lambda i: (0, 0)),
            pl.BlockSpec((TM, HEAD_DIM), lambda i: (jnp.where(i < b * n_pos, i % n_pos, n_pos), 0)),
            pl.BlockSpec((TM, HEAD_DIM), lambda i: (jnp.where(i < b * n_pos, i % n_pos, n_pos), 0))],
        out_specs=([row] if pre else []) + [pl.BlockSpec((TM, d), lambda i: (i, 0)),
                                           pl.BlockSpec((TM, nkv), lambda i: (i, 0)),
                                           pl.BlockSpec((TM, nkv), lambda i: (i, 0))],
        out_shape=([jax.ShapeDtypeStruct((t, d), F32)] if pre else [])
        + [jax.ShapeDtypeStruct((t, d), BF16), jax.ShapeDtypeStruct((t, nkv), BF16),
           jax.ShapeDtypeStruct((t, nkv), BF16)],
        compiler_params=_cp("parallel"),
        name="qkv_proj",
    )(x, *pre_args, g, mod, w, qg, kg, cos, sin)
    return tuple(outs) if pre else (x, *outs)


def _attn_kernel(q_ref, kc_ref, vc_ref, *rest, n_lat):
    if n_lat:
        kl_ref, vl_ref, o_ref, s_scr, vaug = rest
    else:
        o_ref, s_scr, vaug = rest
    n_ctx = kc_ref.shape[0]
    tq = q_ref.shape[0]

    def fill_values():
        vaug[:, HEAD_DIM:] = jnp.ones((n_ctx + n_lat, HEAD_DIM), BF16)
        vaug[0:n_ctx, 0:HEAD_DIM] = vc_ref[...]
        if n_lat:
            vaug[n_ctx:, 0:HEAD_DIM] = vl_ref[...]

    if n_lat:
        pl.when(pl.program_id(2) == 0)(fill_values)
    else:
        fill_values()

    chunks = [(0, n_ctx)] + [(n_ctx + j, ATT_KC) for j in range(0, n_lat, ATT_KC)]
    nt = (((1,), (1,)), ((), ()))
    q_all = jnp.concatenate([q_ref[:, g * HEAD_DIM:(g + 1) * HEAD_DIM] for g in range(GQA_GROUP)], axis=0)
    m_part = jnp.full((GQA_GROUP * tq, LANES), -jnp.inf, F32)
    for off, size in chunks:
        keys = kc_ref[...] if off == 0 else kl_ref[off - n_ctx:off - n_ctx + size, :]
        sc = lax.dot_general(q_all, keys, nt, preferred_element_type=F32)
        s_scr[:, off:off + size] = sc
        for j in range(0, size, LANES):
            m_part = jnp.maximum(m_part, sc[:, j:j + LANES])
    m_row = jnp.max(m_part, axis=-1, keepdims=True)
    hr = GQA_GROUP * tq // 2
    acc = [jnp.zeros((hr, 2 * HEAD_DIM), F32), jnp.zeros((hr, 2 * HEAD_DIM), F32)]
    for off, size in chunks:
        for r in range(2):
            rows = slice(r * hr, (r + 1) * hr)
            p = jnp.exp2((s_scr[rows, off:off + size] - m_row[rows]).astype(BF16))
            acc[r] = acc[r] + jnp.dot(p, vaug[off:off + size, :], preferred_element_type=F32)
    for r in range(2):
        out = (acc[r][:, :HEAD_DIM] / acc[r][:, HEAD_DIM:]).astype(BF16)
        for j in range(GQA_GROUP // 2):
            g = r * (GQA_GROUP // 2) + j
            o_ref[:, g * HEAD_DIM:(g + 1) * HEAD_DIM] = out[j * tq:(j + 1) * tq]


def _attn_ctx_kernel(q_ref, kc_ref, vc_ref, o_all_ref, o_ref, s_scr, vaug):
    del o_all_ref
    _attn_kernel(q_ref, kc_ref, vc_ref, o_ref, s_scr, vaug, n_lat=0)


def _attention(q, k, v, s, c, b):
    t, d = q.shape
    gw = GQA_GROUP * HEAD_DIM
    tq = TQ
    nq = s // tq
    assert s % ATT_KC == 0
    ctx_blk = lambda bi, h, *_: ((b * s) // c + bi, h)
    o_lat = pl.pallas_call(
        functools.partial(_attn_kernel, n_lat=s),
        grid=(b, N_KV_HEADS, nq),
        in_specs=[pl.BlockSpec((tq, gw), lambda bi, h, i: (bi * nq + i, h)),
                  pl.BlockSpec((c, HEAD_DIM), ctx_blk),
                  pl.BlockSpec((c, HEAD_DIM), ctx_blk),
                  pl.BlockSpec((s, HEAD_DIM), lambda bi, h, i: (bi, h)),
                  pl.BlockSpec((s, HEAD_DIM), lambda bi, h, i: (bi, h))],
        out_specs=pl.BlockSpec((tq, gw), lambda bi, h, i: (bi * nq + i, h)),
        out_shape=jax.ShapeDtypeStruct((t, d), BF16),
        scratch_shapes=[pltpu.VMEM((GQA_GROUP * tq, c + s), F32), pltpu.VMEM((c + s, 2 * HEAD_DIM), BF16)],
        compiler_params=_cp("parallel", "parallel", "arbitrary"),
        name="attn_lat",
    )(q, k, v, k, v)
    return pl.pallas_call(
        _attn_ctx_kernel,
        grid=(b, N_KV_HEADS),
        in_specs=[pl.BlockSpec((c, gw), ctx_blk),
                  pl.BlockSpec((c, HEAD_DIM), ctx_blk),
                  pl.BlockSpec((c, HEAD_DIM), ctx_blk),
                  pl.BlockSpec(memory_space=pl.ANY)],
        out_specs=pl.BlockSpec((c, gw), ctx_blk),
        out_shape=jax.ShapeDtypeStruct((t, d), BF16),
        scratch_shapes=[pltpu.VMEM((GQA_GROUP * c, c), F32), pltpu.VMEM((c, 2 * HEAD_DIM), BF16)],
        input_output_aliases={3: 0},
        compiler_params=_cp("parallel", "parallel"),
        name="attn_ctx",
    )(q, k, v, o_lat)


def _attention_mixer(xa, pre, g, mod, w_qkv, qg, kg, w_o, route, s, c, b):
    cos, sin = _rope_tables(s)
    x, q, k, v = _qkv(xa, pre, g, mod, w_qkv, qg.reshape(1, -1), kg.reshape(1, -1), cos, sin, s, b)
    o = _attention(q, k, v, s, c, b)
    return _out_proj(o, x, mod, w_o, route, x.shape[0], s, b)


def _gmlp_kernel(*refs, has_pre):
    x_ref, pre_refs, rest = _split_refs(refs, has_pre)
    g_ref, mod_ref, w_in_ref, lng_ref, lnb_ref, ws_ref, bs_ref, w_out_ref = rest[:8]
    route_in, (o_ref, *route_out, uv_ref, cnt_s) = rest[8:11], rest[11:]
    d = x_ref.shape[-1]
    dcm = lng_ref.shape[-1]
    gw = dcm // CM_GROUPS
    x = _combined_rows(x_ref, pre_refs)
    m = mod_ref[0]
    h = _norm_mod(x, g_ref[...], m[:, 0:d], m[:, d:2 * d])
    z = jax.nn.gelu(jnp.dot(h.astype(BF16), w_in_ref[...], preferred_element_type=F32))
    u = z[:, :dcm]
    v = z[:, dcm:]
    mu = jnp.mean(v, axis=-1, keepdims=True)
    vc = v - mu
    var = jnp.mean(vc * vc, axis=-1, keepdims=True)
    vn = (vc * lax.rsqrt(var + NORM_EPS) * lng_ref[...] + lnb_ref[...]).astype(BF16)
    for ck in range(x.shape[0] // CHUNK):
        rows = slice(ck * CHUNK, (ck + 1) * CHUNK)
        for gi in range(CM_GROUPS):
            cols = slice(gi * gw, (gi + 1) * gw)
            mix = jnp.dot(ws_ref[gi], vn[rows, cols], preferred_element_type=F32) + bs_ref[:, gi:gi + 1]
            uv_ref[rows, cols] = (u[rows, cols] * mix).astype(BF16)
    y = jnp.dot(uv_ref[...], w_out_ref[...], preferred_element_type=F32)
    x1 = x + m[:, 2 * d:3 * d] * y
    o_ref[...] = x1
    _route_rows(x1, m, *route_in, *route_out, cnt_s)


def _gmlp_mixer(xa, pre, g, mod, w_in, ln_g, ln_b, w_s, b_s, w_out, route, s, b):
    t, d = xa.shape
    dcm = ln_g.shape[-1]
    full = lambda shape: pl.BlockSpec(shape, lambda i: (0,) * len(shape))
    pre_specs, pre_args = _pre_io(pre, d, TG, s, b)
    r_in, r_out, r_shape, r_scratch = _route_io(d, t, TG)
    x1, *routed = pl.pallas_call(
        functools.partial(_gmlp_kernel, has_pre=pre is not None),
        grid=(t // TG,),
        in_specs=[pl.BlockSpec((TG, d), lambda i: (i, 0))] + pre_specs + [
            full((1, d)),
            _mod_spec(d, s // TG, b),
            full((d, 2 * dcm)), full((1, dcm)), full((1, dcm)),
            full((CM_GROUPS, CHUNK, CHUNK)), full((CHUNK, CM_GROUPS)), full((dcm, d))] + r_in,
        out_specs=[pl.BlockSpec((TG, d), lambda i: (i, 0))] + r_out,
        out_shape=[jax.ShapeDtypeStruct((t, d), F32)] + r_shape,
        scratch_shapes=[pltpu.VMEM((TG, dcm), BF16)] + r_scratch,
        compiler_params=_cp("arbitrary"),
        name="gmlp",
    )(xa, *pre_args, g, mod, w_in, ln_g.reshape(1, dcm), ln_b.reshape(1, dcm),
      w_s.astype(BF16), b_s.T, w_out, *route)
    return x1, routed


def _route_rows(x1, m, g_ref, wr_ref, br_ref, hf_ref, rt_ref, ew_ref, cnt_ref, cnt_s):
    tm, d = x1.shape
    ng = EXPERTS_PER_GROUP

    @pl.when(pl.program_id(0) == 0)
    def _():
        cnt_s[...] = jnp.zeros_like(cnt_s)

    hf = _norm_mod(x1, g_ref[...], m[:, 3 * d:4 * d], m[:, 4 * d:5 * d])
    hf_ref[...] = _pack_bf16_pairs(hf)
    hf_hi = hf.astype(BF16)
    hf_lo = (hf - hf_hi.astype(F32)).astype(BF16)
    nt = (((1,), (1,)), ((), ()))
    by_hi = lax.dot_general(wr_ref[...], hf_hi, nt, preferred_element_type=F32)
    logits = (by_hi[:LANES] + by_hi[LANES:]
              + lax.dot_general(wr_ref[0:LANES, :], hf_lo, nt, preferred_element_type=F32)) + br_ref[...]
    neg = -jnp.inf
    row = lax.broadcasted_iota(I32, (ng, tm), 0)
    grp = logits[N_EXPERTS:N_EXPERTS + ng]
    gmax = jnp.max(grp, axis=0, keepdims=True)
    gsel = jnp.min(jnp.where(grp == gmax, row, ng), axis=0, keepdims=True)
    gate_g = 1.0 / jnp.sum(jnp.exp(grp - gmax), axis=0, keepdims=True)
    el = logits[0:ng]
    for gi in range(1, N_GROUPS):
        el = jnp.where(gsel == gi, logits[gi * ng:(gi + 1) * ng], el)
    v1 = jnp.max(el, axis=0, keepdims=True)
    i1 = jnp.min(jnp.where(el == v1, row, ng), axis=0, keepdims=True)
    el2 = jnp.where(row == i1, neg, el)
    v2 = jnp.max(el2, axis=0, keepdims=True)
    i2 = jnp.min(jnp.where(el2 == v2, row, ng), axis=0, keepdims=True)
    e21 = jnp.exp(v2 - v1)
    w1 = gate_g / (1.0 + e21)
    w2 = w1 * e21
    e1 = gsel * ng + i1
    e2 = gsel * ng + i2
    lane_row = lax.broadcasted_iota(I32, (LANES, tm), 0)
    ew_ref[...] = jnp.where(lane_row == 0, w1, jnp.where(lane_row == 1, w2, 0.0)).T

    expert = lax.broadcasted_iota(I32, (N_EXPERTS, tm), 0)
    oh1 = expert == e1
    oh2 = expert == e2
    earlier = (lax.broadcasted_iota(I32, (tm, tm), 0) < lax.broadcasted_iota(I32, (tm, tm), 1)).astype(BF16)
    pre1 = jnp.dot(jnp.where(oh1, 1.0, 0.0).astype(BF16), earlier, preferred_element_type=F32)
    pre2 = jnp.dot(jnp.where(oh2, 1.0, 0.0).astype(BF16), earlier, preferred_element_type=F32)
    tot1 = jnp.sum(jnp.where(oh1, 1.0, 0.0), axis=1, keepdims=True)
    tot2 = jnp.sum(jnp.where(oh2, 1.0, 0.0), axis=1, keepdims=True)
    cnt = cnt_s[:, 0:1]
    rank1 = jnp.sum(jnp.where(oh1, cnt + pre1, 0.0), axis=0, keepdims=True).astype(I32)
    rank2 = jnp.sum(jnp.where(oh2, cnt + tot1 + pre2, 0.0), axis=0, keepdims=True).astype(I32)
    cnt = jnp.broadcast_to(cnt + tot1 + tot2, cnt_s.shape)
    cnt_s[...] = cnt
    cnt_ref[...] = cnt.astype(I32)
    rt_ref[...] = jnp.where(row == 0, e1, jnp.where(row == 1, e2, jnp.where(row == 2, rank1,
                            jnp.where(row == 3, rank2, 0))))


def _route_params(g_ffn, w_group, b_group, w_router, b_router):
    d = w_group.shape[0]
    pad = LANES - N_EXPERTS - N_GROUPS
    wr = jnp.concatenate([w_router.reshape(d, N_EXPERTS), w_group, jnp.zeros((d, pad), F32)], axis=1).T
    br = jnp.concatenate([b_router.reshape(N_EXPERTS), b_group,
                          jnp.full((EXPERTS_PER_GROUP - N_GROUPS,), -jnp.inf, F32),
                          jnp.zeros((pad - EXPERTS_PER_GROUP + N_GROUPS,), F32)]).reshape(LANES, 1)
    wr_hi = wr.astype(BF16)
    wr_lo = (wr - wr_hi.astype(F32)).astype(BF16)
    return g_ffn.reshape(1, d), jnp.concatenate([wr_hi, wr_lo], axis=0), br


def _route_io(d, nrows, tr):
    const = lambda shape: pl.BlockSpec(shape, lambda i: (0, 0))
    row = lambda w: pl.BlockSpec((tr, w), lambda i: (i, 0))
    in_specs = [const((1, d)), const((2 * LANES, d)), const((LANES, 1))]
    out_specs = [row(d // 2), pl.BlockSpec((SUBLANES, tr), lambda i: (0, i)), row(LANES), const((N_EXPERTS, LANES))]
    out_shape = [jax.ShapeDtypeStruct((nrows, d // 2), I32), jax.ShapeDtypeStruct((SUBLANES, nrows), I32),
                 jax.ShapeDtypeStruct((nrows, LANES), F32), jax.ShapeDtypeStruct((N_EXPERTS, LANES), I32)]
    return in_specs, out_specs, out_shape, [pltpu.VMEM((N_EXPERTS, LANES), F32)]


def _plan_kernel(cnt_ref, rt_ref, pos_ref, blk_e_ref, n_used_ref, first_ref, slot_ref, next_ref,
                 start_s, end_s, nxt_s, *, bm):
    n_blk = blk_e_ref.shape[0]
    acc = jnp.int32(0)
    for e in range(N_EXPERTS):
        start_s[e] = acc
        acc = acc + (cnt_ref[e, 0] + (bm - 1)) // bm * bm
        end_s[e] = acc
    n_used = acc // bm
    n_used_ref[0] = n_used
    nxt = jnp.int32(-1)
    for e in reversed(range(N_EXPERTS)):
        nxt_s[e] = nxt
        nxt = jnp.where(cnt_ref[e, 0] > 0, e, nxt)

    def block(i, carry):
        prev_e, runs = carry
        row = jnp.minimum(i, n_used - 1) * bm
        e = lax.while_loop(lambda v: jnp.logical_and(v < N_EXPERTS - 1, end_s[v] <= row), lambda v: v + 1,
                           jnp.maximum(prev_e, 0))
        first = jnp.logical_and(i < n_used, prev_e != e)
        runs = runs + first.astype(I32)
        blk_e_ref[i] = e
        first_ref[i] = first.astype(I32)
        slot_ref[i] = (runs - 1) % 2
        next_ref[i] = nxt_s[e]
        return e, runs

    lax.fori_loop(0, n_blk, block, (jnp.int32(-1), jnp.int32(0)))

    rt = rt_ref[...]
    start_of = jnp.zeros_like(rt)
    for e in range(N_EXPERTS):
        start_of = jnp.where(rt == e, start_s[e], start_of)
    pos_ref[...] = start_of + pltpu.roll(rt, SUBLANES - 2, 0)


def _dispatch_plan(rt, cnt, bm):
    n_tok = rt.shape[1]
    n_rows = 2 * n_tok + N_EXPERTS * bm
    n_blk = n_rows // bm
    smem = pl.BlockSpec(memory_space=pltpu.SMEM)
    vec = lambda n: jax.ShapeDtypeStruct((n,), I32)
    pos, blk_e, n_used, first, slot, nxt = pl.pallas_call(
        functools.partial(_plan_kernel, bm=bm),
        in_specs=[smem, pl.BlockSpec(memory_space=pltpu.VMEM)],
        out_specs=[pl.BlockSpec(memory_space=pltpu.VMEM), smem, smem, smem, smem, smem],
        out_shape=[jax.ShapeDtypeStruct(rt.shape, I32), vec(n_blk), vec(1), vec(n_blk), vec(n_blk), vec(n_blk)],
        scratch_shapes=[pltpu.SMEM((N_EXPERTS,), I32)] * 3,
        name="moe_plan",
    )(cnt, rt)
    return pos, (blk_e, n_used, first, slot, nxt), n_rows


def _sc_mesh():
    return plsc.VectorSubcoreMesh(core_axis_name="c", subcore_axis_name="s")


def _sc_worker_base(per_worker):
    return (lax.axis_index("s") * SC_CORES + lax.axis_index("c")) * per_worker


def _sc_dispatch(hf, pos, n_rows):
    t, d = hf.shape
    per_w = t // SC_WORKERS
    ch = SC_CHUNK
    n_ck = per_w // ch
    assert per_w * SC_WORKERS == t and n_ck * ch == per_w

    @functools.partial(
        pl.kernel, mesh=_sc_mesh(), out_type=jax.ShapeDtypeStruct((n_rows, d), hf.dtype),
        scratch_types=[pltpu.VMEM((per_w,), I32), pltpu.VMEM((per_w,), I32), pltpu.VMEM((2, ch, d), hf.dtype),
                       pltpu.SemaphoreType.DMA((2,)), pltpu.SemaphoreType.DMA((2,)), pltpu.SemaphoreType.DMA((2,))])
    def dispatch(hf_hbm, p_hbm, out_hbm, i0_v, i1_v, rows_v, sem_in, sem_s0, sem_s1):
        base = pl.multiple_of(_sc_worker_base(per_w), SUBLANES)
        pltpu.sync_copy(p_hbm.at[pl.ds(base, per_w)], i0_v)
        pltpu.sync_copy(p_hbm.at[pl.ds(pl.multiple_of(t + base, SUBLANES), per_w)], i1_v)

        def load(ck):
            return pltpu.make_async_copy(hf_hbm.at[pl.ds(base + ck * ch, ch)], rows_v.at[ck % 2], sem_in.at[ck % 2])

        def scatters(ck):
            src = rows_v.at[ck % 2]
            return (pltpu.make_async_copy(src, out_hbm.at[i0_v.at[pl.ds(ck * ch, ch)]], sem_s0.at[ck % 2]),
                    pltpu.make_async_copy(src, out_hbm.at[i1_v.at[pl.ds(ck * ch, ch)]], sem_s1.at[ck % 2]))

        load(0).start()
        for ck in range(n_ck):
            load(ck).wait()
            if ck + 1 < n_ck:
                if ck >= 1:
                    for cp in scatters(ck - 1):
                        cp.wait()
                load(ck + 1).start()
            for cp in scatters(ck):
                cp.start()
        for ck in range(max(n_ck - 2, 0), n_ck):
            for cp in scatters(ck):
                cp.wait()

    return dispatch(hf, pos)


def _sc_gather(rows, idx):
    n = idx.shape[0]
    d = rows.shape[1]
    per_w = n // SC_WORKERS
    ch = 2 * SC_CHUNK
    n_ck = per_w // ch
    assert per_w * SC_WORKERS == n and n_ck * ch == per_w

    @functools.partial(
        pl.kernel, mesh=_sc_mesh(), out_type=jax.ShapeDtypeStruct((n, d), rows.dtype),
        scratch_types=[pltpu.VMEM((per_w,), I32), pltpu.VMEM((2, ch, d), rows.dtype),
                       pltpu.SemaphoreType.DMA((2,)), pltpu.SemaphoreType.DMA((2,))])
    def gather(rows_hbm, i_hbm, out_hbm, i_v, buf, sem_g, sem_w):
        base = pl.multiple_of(_sc_worker_base(per_w), SUBLANES)
        pltpu.sync_copy(i_hbm.at[pl.ds(base, per_w)], i_v)

        def fetch(ck):
            return pltpu.make_async_copy(rows_hbm.at[i_v.at[pl.ds(ck * ch, ch)]], buf.at[ck % 2], sem_g.at[ck % 2])

        def write(ck):
            return pltpu.make_async_copy(buf.at[ck % 2], out_hbm.at[pl.ds(base + ck * ch, ch)], sem_w.at[ck % 2])

        fetch(0).start()
        for ck in range(n_ck):
            fetch(ck).wait()
            if ck + 1 < n_ck:
                if ck >= 1:
                    write(ck - 1).wait()
                fetch(ck + 1).start()
            write(ck).start()
        for ck in range(max(n_ck - 2, 0), n_ck):
            write(ck).wait()

    return gather(rows, idx)


def _expert_kernel(blk_e_ref, n_used_ref, first_ref, slot_ref, next_ref, x_ref, wg_hbm, wu_hbm, wd_hbm, y_ref,
                   wgf, wuf, wdf, wgb, wub, wdb, sem, *, e_base):
    n_used = n_used_ref[0]
    bm = x_ref.shape[0] // MOE_SUB

    def weight_copies(e, slot):
        return (pltpu.make_async_copy(wg_hbm.at[e_base + e], wgf.at[slot], sem.at[slot, 0]),
                pltpu.make_async_copy(wu_hbm.at[e_base + e], wuf.at[slot], sem.at[slot, 1]),
                pltpu.make_async_copy(wd_hbm.at[e_base + e], wdf.at[slot], sem.at[slot, 2]))

    @pl.when(pl.program_id(0) == 0)
    def _():
        for cp in weight_copies(blk_e_ref[0], 0):
            cp.start()

    for j in range(MOE_SUB):
        blk = pl.program_id(0) * MOE_SUB + j
        rows = slice(j * bm, (j + 1) * bm)

        @pl.when(jnp.logical_and(blk < n_used, first_ref[blk] == 1))
        def _():
            slot = slot_ref[blk]
            for cp in weight_copies(blk_e_ref[blk], slot):
                cp.wait()
            nxt = next_ref[blk]

            @pl.when(nxt >= 0)
            def _():
                for cp in weight_copies(nxt, 1 - slot):
                    cp.start()

            wgb[...] = wgf[slot].astype(BF16)
            wub[...] = wuf[slot].astype(BF16)
            wdb[...] = wdf[slot].astype(BF16)

        @pl.when(blk < n_used)
        def _():
            x_hi, x_lo = _unpack_bf16_pairs(x_ref[rows, :])
            xb = jnp.concatenate([x_hi.astype(BF16), x_lo.astype(BF16)], axis=1)
            gt = jnp.dot(xb, wgb[...], preferred_element_type=F32)
            up = jnp.dot(xb, wub[...], preferred_element_type=F32)
            act = (gt * _sigmoid(gt) * up).astype(BF16)
            y_ref[rows, :] = _pack_bf16_pairs(jnp.dot(act, wdb[...], preferred_element_type=F32))


def _experts(x_rows, plan, w_gate, w_up, w_down, layer):
    n_rows, dp = x_rows.shape
    depth, n_e, d, de = w_gate.shape
    step_rows = MOE_SUB * MOE_BM
    assert n_rows % step_rows == 0
    any_spec = pl.BlockSpec(memory_space=pl.ANY)
    last_used = lambda i, be, nu, *_: (jnp.minimum(i, (nu[0] - 1) // MOE_SUB), 0)
    grid_spec = pltpu.PrefetchScalarGridSpec(
        num_scalar_prefetch=5,
        grid=(n_rows // step_rows,),
        in_specs=[pl.BlockSpec((step_rows, dp), last_used), any_spec, any_spec, any_spec],
        out_specs=pl.BlockSpec((step_rows, dp), last_used),
        scratch_shapes=[pltpu.VMEM((2, d, de), F32), pltpu.VMEM((2, d, de), F32), pltpu.VMEM((2, de, d), F32),
                        pltpu.VMEM((d, de), BF16), pltpu.VMEM((d, de), BF16), pltpu.VMEM((de, d), BF16),
                        pltpu.SemaphoreType.DMA((2, 3))],
    )
    return pl.pallas_call(
        functools.partial(_expert_kernel, e_base=layer * n_e),
        grid_spec=grid_spec,
        out_shape=jax.ShapeDtypeStruct((n_rows, dp), I32),
        compiler_params=_cp("arbitrary"),
        name="moe_experts",
    )(*plan, x_rows, w_gate.reshape(depth * n_e, d, de), w_up.reshape(depth * n_e, d, de),
      w_down.reshape(depth * n_e, de, d))


def _combine_kernel(x_ref, *rest):
    *pre_refs, o_ref = rest
    o_ref[...] = _combined_rows(x_ref, pre_refs)


def _combine(x, pre, s, b):
    nrows, d = x.shape
    pre_specs, pre_args = _pre_io(pre, d, TR, s, b)
    return pl.pallas_call(
        _combine_kernel,
        grid=(nrows // TR,),
        in_specs=[pl.BlockSpec((TR, d), lambda i: (i, 0))] + pre_specs,
        out_specs=pl.BlockSpec((TR, d), lambda i: (i, 0)),
        out_shape=jax.ShapeDtypeStruct((nrows, d), F32),
        compiler_params=_cp("parallel"),
        name="moe_combine",
    )(x, *pre_args)


def _expert_outputs(routed, w_gate, w_up, w_down, layer):
    hf, rt, _, cnt = routed
    pos, plan, n_rows = _dispatch_plan(rt, cnt, MOE_BM)
    pos = pos[0:2].reshape(-1)
    x_rows = _sc_dispatch(hf, pos, n_rows)
    y_rows = _experts(x_rows, plan, w_gate, w_up, w_down, layer)
    return _sc_gather(y_rows, pos)


def _combine_half_kernel(x_ref, y0_ref, y1_ref, ew_ref, mod_ref, *rest):
    rest[-1][...] = _combined_rows(x_ref, (y0_ref, y1_ref, ew_ref, mod_ref))


def _last_layer_outputs(x1, routed, mod, w_gate, w_up, w_down, layer, s, b):
    hf, rt, ew, cnt = routed
    nrows, d = x1.shape
    pos, plan, n_rows = _dispatch_plan(rt, cnt, MOE_BM)
    pos = pos[0:2].reshape(-1)
    x_rows = _sc_dispatch(hf, pos, n_rows)
    y_rows = _experts(x_rows, plan, w_gate, w_up, w_down, layer)
    h = nrows // 2
    nbh = h // TR
    assert 2 * nbh * TR == nrows

    def half_specs(o):
        return [pl.BlockSpec((TR, d), lambda i: (i + o, 0)),
                pl.BlockSpec((TR, d // 2), lambda i: (i, 0)),
                pl.BlockSpec((TR, d // 2), lambda i: (i + nbh, 0)),
                pl.BlockSpec((TR, LANES), lambda i: (i + o, 0)),
                pl.BlockSpec((1, 1, N_MOD * d), lambda i: (jnp.minimum((i + o) // (s // TR), b), 0, 0))]

    out = None
    for half in range(2):
        idx = jnp.concatenate([pos[half * h:(half + 1) * h], pos[nrows + half * h:nrows + (half + 1) * h]])
        y01 = _sc_gather(y_rows, idx)
        o = half * nbh
        specs, args = half_specs(o), [x1, y01, y01, ew, mod]
        if out is not None:
            specs.append(pl.BlockSpec(memory_space=pl.ANY))
            args.append(out)
        out = pl.pallas_call(
            _combine_half_kernel,
            grid=(nbh,),
            in_specs=specs,
            out_specs=half_specs(o)[0],
            out_shape=jax.ShapeDtypeStruct((nrows, d), F32),
            input_output_aliases={5: 0} if out is not None else {},
            compiler_params=_cp("parallel"),
            name="moe_combine",
        )(*args)
    return out


def kernel(x, c, ctx, c_ctx, ada_w, ada_b, norm_mix_g, norm_ffn_g, rg_w_in, rg_conv_w, rg_conv_b, rg_wa, rg_ba, rg_wi, rg_bi, rg_lambda, rg_w_out, at_w_qkv, at_q_g, at_k_g, at_w_o, cm_w_in, cm_ln_g, cm_ln_b, cm_w_s, cm_b_s, cm_w_out, moe_w_group, moe_b_group, moe_w_router, moe_b_router, moe_w_gate, moe_w_up, moe_w_down):
    b, s, d = x.shape
    cl = ctx.shape[1]
    depth = ada_w.shape[0]
    n_lat = b * s
    assert b < SUBLANES and s % max(TI, TR, TG) == 0 and (b * cl) % max(TI, TR, TG) == 0 and cl % TM == 0
    assert d == RG_BLOCKS * LANES

    cin = jnp.concatenate([c, c_ctx[None, :], jnp.zeros((SUBLANES - b - 1, d), F32)], axis=0)
    mod_all = _ada_table(cin, ada_w, ada_b).reshape(depth, SUBLANES, 1, N_MOD * d)
    n_tok = n_lat + b * cl

    xa, pre = (x.reshape(n_lat, d), ctx.reshape(b * cl, d)), None
    for l in range(depth):
        kind = l % 3
        j = l // 3
        last = l == depth - 1
        mod = mod_all[l]
        g_mix = norm_mix_g[l].reshape(1, d)
        route = _route_params(norm_ffn_g[l], moe_w_group[l], moe_b_group[l], moe_w_router[l], moe_b_router[l])
        if kind == 0:
            x1, routed = _rglru_mixer(xa, pre, g_mix, mod, _bf16_weight(rg_w_in, j), rg_conv_w[j], rg_conv_b[j],
                                      rg_wa[j], rg_ba[j], rg_wi[j], rg_bi[j], rg_lambda[j],
                                      _bf16_weight(rg_w_out, j), route, s, cl, b, n_lat if last else n_tok)
        elif kind == 1:
            x1, routed = _attention_mixer(xa, pre, g_mix, mod, _bf16_weight(at_w_qkv, j), at_q_g[j], at_k_g[j],
                                          _bf16_weight(at_w_o, j), route, s, cl, b)
        else:
            x1, routed = _gmlp_mixer(xa, pre, g_mix, mod, _bf16_weight(cm_w_in, j), cm_ln_g[j], cm_ln_b[j],
                                     cm_w_s[j], cm_b_s[j], _bf16_weight(cm_w_out, j), route, s, b)
        if last:
            return _last_layer_outputs(x1, routed, mod, moe_w_gate, moe_w_up, moe_w_down, l, s, b).reshape(b, s, d)
        y01 = _expert_outputs(routed, moe_w_gate, moe_w_up, moe_w_down, l)
        xa, pre = x1, (y01, routed[2], mod)
```
